```python
import math
import jax
import jax.numpy as jnp
from jax import lax
import numpy as np

D_MODEL = 1024
BATCH = 16
SEQ = 256
DEPTH = 4
DEC_BATCH = 2
DEC_SEQ = 1024
PAST_LEN = 256

GRID_W = 64
N_MIXERS = 4
Q_BLOCK = 128
ROPE_THETA = 10000.0
EPS = 1e-6
NEG_INF = -1e30
D_FF = 4 * D_MODEL
MOD_CHUNKS = 6

ATT_HEADS = 8
ATT_KV_HEADS = 2
ATT_HEAD_DIM = 128

DIFF_HEADS = 8
DIFF_HEAD_DIM = 64

SWA_HEADS = 16
SWA_KV_HEADS = 4
SWA_HEAD_DIM = 64
WINDOW = 128

MLA_HEADS = 8
MLA_NOPE = 128
MLA_ROPE = 64
MLA_VDIM = 128
MLA_Q_RANK = 512
MLA_KV_RANK = 256

N_ATT = (DEPTH + N_MIXERS - 1) // N_MIXERS
N_DIFF = (DEPTH + N_MIXERS - 2) // N_MIXERS
N_SWA = (DEPTH + N_MIXERS - 3) // N_MIXERS
N_MLA = (DEPTH + N_MIXERS - 4) // N_MIXERS

kernel_name = "hybrid_diffusion_prefix_trunk_step"


def _rms_norm(x, g):
    xf = x.astype(jnp.float32)
    y = xf * lax.rsqrt(jnp.mean(xf * xf, axis=-1, keepdims=True) + EPS)
    return (y * g.astype(jnp.float32)).astype(x.dtype)


def _modulation(cond, w, b):
    m = jax.nn.silu(cond) @ w + b
    return jnp.split(m[..., None, :], MOD_CHUNKS, axis=-1)


def _modulate(h, shift, scale):
    return h * (1.0 + scale) + shift


def _sq_relu_mlp(h, w1, w2):
    return jnp.square(jax.nn.relu(h @ w1)) @ w2


def _axial_rope_tables(rows, rot_dim):
    half = rot_dim // 2
    inv = ROPE_THETA ** (-jnp.arange(0, half, 2, dtype=jnp.float32) / half)
    row = jnp.repeat(jnp.arange(rows, dtype=jnp.float32), GRID_W)
    col = jnp.tile(jnp.arange(GRID_W, dtype=jnp.float32), rows)
    ang_r = row[:, None] * inv[None, :]
    ang_c = col[:, None] * inv[None, :]
    ang = jnp.concatenate([ang_r, ang_r, ang_c, ang_c], axis=-1)
    return jnp.cos(ang), jnp.sin(ang)


def _apply_rope(x, cos, sin):
    x1, x2, x3, x4 = jnp.split(x, 4, axis=-1)
    rot = jnp.concatenate([-x2, x1, -x4, x3], axis=-1)
    return (x * cos + rot * sin).astype(x.dtype)


def _sweep_query_blocks(block_fn, q):
    *lead, s, d = q.shape
    nb = s // Q_BLOCK
    qb = jnp.moveaxis(q.reshape(*lead, nb, Q_BLOCK, d), -3, 0)
    ob = jnp.moveaxis(lax.map(block_fn, qb), 0, -3)
    return ob.reshape(*ob.shape[:-3], nb * Q_BLOCK, ob.shape[-1])


def _gqa_project(h, w_qkv, q_g, k_g, n_heads, n_kv, hd):
    b, s, _ = h.shape
    q, k, v = jnp.split(h @ w_qkv, [n_heads * hd, (n_heads + n_kv) * hd], axis=-1)
    q = q.reshape(b, s, n_kv, n_heads // n_kv, hd).transpose(0, 2, 3, 1, 4)
    k = k.reshape(b, s, n_kv, hd).transpose(0, 2, 1, 3)
    v = v.reshape(b, s, n_kv, hd).transpose(0, 2, 1, 3)
    return _rms_norm(q, q_g), _rms_norm(k, k_g), v


def _gqa_attend(q, k, v):
    scale = q.shape[-1] ** -0.5

    def block(qb):
        s = jnp.einsum('bkgqd,bkld->bkgql', qb, k).astype(jnp.float32) * scale
        p = jax.nn.softmax(s, axis=-1)
        return jnp.einsum('bkgql,bkld->bkgqd', p.astype(v.dtype), v)

    return _sweep_query_blocks(block, q)


def _merge_grouped(o):
    b, kv, g, s, d = o.shape
    return o.transpose(0, 3, 1, 2, 4).reshape(b, s, kv * g * d)


def _merge_heads(o):
    b, h, s, d = o.shape
    return o.transpose(0, 2, 1, 3).reshape(b, s, h * d)


def _diff_project(h, w_qkv, q_g, k_g):
    b, s, _ = h.shape
    q, k, v = jnp.split(h @ w_qkv, 3, axis=-1)
    q = q.reshape(b, s, DIFF_HEADS, 2, DIFF_HEAD_DIM).transpose(0, 2, 3, 1, 4)
    k = k.reshape(b, s, DIFF_HEADS, 2, DIFF_HEAD_DIM).transpose(0, 2, 3, 1, 4)
    v = v.reshape(b, s, DIFF_HEADS, 2 * DIFF_HEAD_DIM).transpose(0, 2, 1, 3)
    return _rms_norm(q, q_g), _rms_norm(k, k_g), v


def _diff_lambda(layer_idx, lq1, lk1, lq2, lk2):
    lam_init = 0.8 - 0.6 * math.exp(-0.3 * layer_idx)
    f = jnp.float32
    lam = (jnp.exp(jnp.sum(lq1.astype(f) * lk1.astype(f)))
           - jnp.exp(jnp.sum(lq2.astype(f) * lk2.astype(f))) + lam_init)
    return lam, lam_init


def _diff_attend(q, k, v, lam):
    scale = DIFF_HEAD_DIM ** -0.5

    def block(qb):
        s = jnp.einsum('bhcqd,bhcld->bhcql', qb, k).astype(jnp.float32) * scale
        p = jax.nn.softmax(s, axis=-1)
        a = p[:, :, 0] - lam * p[:, :, 1]
        return jnp.einsum('bhql,bhld->bhqd', a.astype(v.dtype), v)

    return _sweep_query_blocks(block, q)


def _swa_attend_context(q, k, v, sink):
    scale = SWA_HEAD_DIM ** -0.5
    sk = sink.astype(jnp.float32).reshape(SWA_KV_HEADS, SWA_HEADS // SWA_KV_HEADS)

    def block(qb):
        s = jnp.einsum('bkgqd,bkld->bkgql', qb, k).astype(jnp.float32) * scale
        sinks = jnp.broadcast_to(sk[None, :, :, None, None], s.shape[:-1] + (1,))
        p = jax.nn.softmax(jnp.concatenate([s, sinks], axis=-1), axis=-1)[..., :-1]
        return jnp.einsum('bkgql,bkld->bkgqd', p.astype(v.dtype), v)

    return _sweep_query_blocks(block, q)


def _swa_attend_latent(q, k, v, k_ctx, v_ctx, sink):
    b, kv, g, s, d = q.shape
    nb = s // Q_BLOCK
    n_side = -(-WINDOW // Q_BLOCK)
    span = 2 * n_side + 1
    pad = n_side * Q_BLOCK
    scale = d ** -0.5
    kp = jnp.pad(k, ((0, 0), (0, 0), (pad, pad), (0, 0))).reshape(b, kv, nb + 2 * n_side, Q_BLOCK, d)
    vp = jnp.pad(v, ((0, 0), (0, 0), (pad, pad), (0, 0))).reshape(b, kv, nb + 2 * n_side, Q_BLOCK, d)
    kwin = jnp.concatenate([kp[:, :, o:o + nb] for o in range(span)], axis=3)
    vwin = jnp.concatenate([vp[:, :, o:o + nb] for o in range(span)], axis=3)
    qpos = jnp.arange(nb)[:, None] * Q_BLOCK + jnp.arange(Q_BLOCK)[None, :]
    kpos = jnp.arange(nb)[:, None] * Q_BLOCK - pad + jnp.arange(span * Q_BLOCK)[None, :]
    valid = ((jnp.abs(qpos[:, :, None] - kpos[:, None, :]) <= WINDOW)
             & (kpos >= 0)[:, None, :] & (kpos < s)[:, None, :])
    qb = q.reshape(b, kv, g, nb, Q_BLOCK, d)
    s_loc = jnp.einsum('bkgnqd,bknjd->bkgnqj', qb, kwin).astype(jnp.float32) * scale
    s_loc = jnp.where(valid, s_loc, NEG_INF)
    s_ctx = jnp.einsum('bkgnqd,bkld->bkgnql', qb, k_ctx).astype(jnp.float32) * scale
    sk = sink.astype(jnp.float32).reshape(kv, g)[None, :, :, None, None, None]
    sinks = jnp.broadcast_to(sk, s_loc.shape[:-1] + (1,))
    p = jax.nn.softmax(jnp.concatenate([s_loc, s_ctx, sinks], axis=-1), axis=-1)
    n_loc = span * Q_BLOCK
    p_loc = p[..., :n_loc].astype(v.dtype)
    p_ctx = p[..., n_loc:n_loc + k_ctx.shape[2]].astype(v.dtype)
    o = (jnp.einsum('bkgnqj,bknjd->bkgnqd', p_loc, vwin)
         + jnp.einsum('bkgnql,bkld->bkgnqd', p_ctx, v_ctx))
    return o.reshape(b, kv, g, s, d)


def _mla_split(h, w_in, q_a_g, kv_a_g, w_uq):
    b, s, _ = h.shape
    c_q, c_kv, k_pe = jnp.split(h @ w_in, [MLA_Q_RANK, MLA_Q_RANK + MLA_KV_RANK], axis=-1)
    q = (_rms_norm(c_q, q_a_g) @ w_uq).reshape(b, s, MLA_HEADS, MLA_NOPE + MLA_ROPE).transpose(0, 2, 1, 3)
    return q, _rms_norm(c_kv, kv_a_g), k_pe


def _mla_expand(ckv, k_pe, w_ukv):
    b, l, _ = ckv.shape
    kv = (ckv @ w_ukv).reshape(b, l, MLA_HEADS, MLA_NOPE + MLA_VDIM)
    k_nope, v = jnp.split(kv, [MLA_NOPE], axis=-1)
    k = jnp.concatenate([k_nope, jnp.broadcast_to(k_pe[:, :, None, :], (b, l, MLA_HEADS, MLA_ROPE))], axis=-1)
    return k.transpose(0, 2, 1, 3), v.transpose(0, 2, 1, 3)


def _rope_tail(x, cos, sin):
    return jnp.concatenate([x[..., :MLA_NOPE], _apply_rope(x[..., MLA_NOPE:], cos, sin)], axis=-1)


def setup_inputs(seed: int = 0) -> dict:
    key = jax.random.key(seed)
    keys = iter(jax.random.split(key, 80))

    def nrm(shape, scale=1.0):
        return scale * jax.random.normal(next(keys), shape, jnp.float32)

    def gain(shape):
        return 1.0 + nrm(shape, 0.05)

    d = D_MODEL
    att_qkv = (ATT_HEADS + 2 * ATT_KV_HEADS) * ATT_HEAD_DIM
    diff_qkv = 3 * DIFF_HEADS * 2 * DIFF_HEAD_DIM
    swa_qkv = (SWA_HEADS + 2 * SWA_KV_HEADS) * SWA_HEAD_DIM
    mla_in = MLA_Q_RANK + MLA_KV_RANK + MLA_ROPE
    return {
        'x_prompt': nrm((BATCH, SEQ, d)),
        'x_sample': nrm((DEC_BATCH, DEC_SEQ, d)),
        'cache_att_k': nrm((DEC_BATCH, N_ATT, ATT_KV_HEADS, PAST_LEN, ATT_HEAD_DIM)),
        'cache_att_v': nrm((DEC_BATCH, N_ATT, ATT_KV_HEADS, PAST_LEN, ATT_HEAD_DIM)),
        'cache_diff_k': nrm((DEC_BATCH, N_DIFF, DIFF_HEADS, 2, PAST_LEN, DIFF_HEAD_DIM)),
        'cache_diff_v': nrm((DEC_BATCH, N_DIFF, DIFF_HEADS, PAST_LEN, 2 * DIFF_HEAD_DIM)),
        'cache_swa_k': nrm((DEC_BATCH, N_SWA, SWA_KV_HEADS, PAST_LEN, SWA_HEAD_DIM)),
        'cache_swa_v': nrm((DEC_BATCH, N_SWA, SWA_KV_HEADS, PAST_LEN, SWA_HEAD_DIM)),
        'cache_mla_ckv': nrm((DEC_BATCH, N_MLA, PAST_LEN, MLA_KV_RANK)),
        'cache_mla_kpe': nrm((DEC_BATCH, N_MLA, PAST_LEN, MLA_ROPE)),
        'c': nrm((DEC_BATCH, d)),
        'c_ctx': nrm((d,)),
        'ada_w': nrm((DEPTH, d, MOD_CHUNKS * d), 0.5 * d ** -0.5),
        'ada_b': nrm((DEPTH, MOD_CHUNKS * d), 0.01),
        'norm_mix': gain((DEPTH, d)),
        'norm_ffn': gain((DEPTH, d)),
        'att_w_qkv': nrm((N_ATT, d, att_qkv), d ** -0.5),
        'att_q_norm': gain((N_ATT, ATT_HEAD_DIM)),
        'att_k_norm': gain((N_ATT, ATT_HEAD_DIM)),
        'att_w_o': nrm((N_ATT, ATT_HEADS * ATT_HEAD_DIM, d), (ATT_HEADS * ATT_HEAD_DIM) ** -0.5),
        'diff_w_qkv': nrm((N_DIFF, d, diff_qkv), d ** -0.5),
        'diff_q_norm': gain((N_DIFF, DIFF_HEAD_DIM)),
        'diff_k_norm': gain((N_DIFF, DIFF_HEAD_DIM)),
        'diff_lq1': nrm((N_DIFF, DIFF_HEAD_DIM), 0.1),
        'diff_lk1': nrm((N_DIFF, DIFF_HEAD_DIM), 0.1),
        'diff_lq2': nrm((N_DIFF, DIFF_HEAD_DIM), 0.1),
        'diff_lk2': nrm((N_DIFF, DIFF_HEAD_DIM), 0.1),
        'diff_subln': gain((N_DIFF, 2 * DIFF_HEAD_DIM)),
        'diff_w_o': nrm((N_DIFF, DIFF_HEADS * 2 * DIFF_HEAD_DIM, d), (DIFF_HEADS * 2 * DIFF_HEAD_DIM) ** -0.5),
        'swa_w_qkv': nrm((N_SWA, d, swa_qkv), d ** -0.5),
        'swa_q_norm': gain((N_SWA, SWA_HEAD_DIM)),
        'swa_k_norm': gain((N_SWA, SWA_HEAD_DIM)),
        'swa_sink': nrm((N_SWA, SWA_HEADS), 0.5),
        'swa_w_o': nrm((N_SWA, SWA_HEADS * SWA_HEAD_DIM, d), (SWA_HEADS * SWA_HEAD_DIM) ** -0.5),
        'mla_w_in': nrm((N_MLA, d, mla_in), d ** -0.5),
        'mla_q_a_norm': gain((N_MLA, MLA_Q_RANK)),
        'mla_kv_a_norm': gain((N_MLA, MLA_KV_RANK)),
        'mla_w_uq': nrm((N_MLA, MLA_Q_RANK, MLA_HEADS * (MLA_NOPE + MLA_ROPE)), MLA_Q_RANK ** -0.5),
        'mla_w_ukv': nrm((N_MLA, MLA_KV_RANK, MLA_HEADS * (MLA_NOPE + MLA_VDIM)), MLA_KV_RANK ** -0.5),
        'mla_q_norm': gain((N_MLA, MLA_NOPE + MLA_ROPE)),
        'mla_k_norm': gain((N_MLA, MLA_NOPE + MLA_ROPE)),
        'mla_w_o': nrm((N_MLA, MLA_HEADS * MLA_VDIM, d), (MLA_HEADS * MLA_VDIM) ** -0.5),
        'mlp_w1': nrm((DEPTH, d, D_FF), d ** -0.5),
        'mlp_w2': nrm((DEPTH, D_FF, d), D_FF ** -0.5),
    }


def reference(x_prompt, x_sample, cache_att_k, cache_att_v, cache_diff_k, cache_diff_v,
              cache_swa_k, cache_swa_v, cache_mla_ckv, cache_mla_kpe, c, c_ctx,
              ada_w, ada_b, norm_mix, norm_ffn,
              att_w_qkv, att_q_norm, att_k_norm, att_w_o,
              diff_w_qkv, diff_q_norm, diff_k_norm, diff_lq1, diff_lk1, diff_lq2, diff_lk2,
              diff_subln, diff_w_o,
              swa_w_qkv, swa_q_norm, swa_k_norm, swa_sink, swa_w_o,
              mla_w_in, mla_q_a_norm, mla_kv_a_norm, mla_w_uq, mla_w_ukv, mla_q_norm, mla_k_norm,
              mla_w_o, mlp_w1, mlp_w2):
    xp = x_prompt
    att_k, att_v, diff_k, diff_v, swa_k, swa_v, mla_ckv, mla_kpe = [], [], [], [], [], [], [], []
    for i in range(DEPTH):
        m, j = i % N_MIXERS, i // N_MIXERS
        sh1, sc1, g1, sh2, sc2, g2 = _modulation(c_ctx, ada_w[i], ada_b[i])
        h = _modulate(_rms_norm(xp, norm_mix[i]), sh1, sc1)
        if m == 0:
            q, k, v = _gqa_project(h, att_w_qkv[j], att_q_norm[j], att_k_norm[j],
                                   ATT_HEADS, ATT_KV_HEADS, ATT_HEAD_DIM)
            out = _merge_grouped(_gqa_attend(q, k, v)) @ att_w_o[j]
            att_k.append(k)
            att_v.append(v)
        elif m == 1:
            q, k, v = _diff_project(h, diff_w_qkv[j], diff_q_norm[j], diff_k_norm[j])
            lam, lam_init = _diff_lambda(i, diff_lq1[j], diff_lk1[j], diff_lq2[j], diff_lk2[j])
            o = _rms_norm(_diff_attend(q, k, v, lam), diff_subln[j]) * (1.0 - lam_init)
            out = _merge_heads(o) @ diff_w_o[j]
            diff_k.append(k)
            diff_v.append(v)
        elif m == 2:
            q, k, v = _gqa_project(h, swa_w_qkv[j], swa_q_norm[j], swa_k_norm[j],
                                   SWA_HEADS, SWA_KV_HEADS, SWA_HEAD_DIM)
            out = _merge_grouped(_swa_attend_context(q, k, v, swa_sink[j])) @ swa_w_o[j]
            swa_k.append(k)
            swa_v.append(v)
        else:
            q, ckv, kpe = _mla_split(h, mla_w_in[j], mla_q_a_norm[j], mla_kv_a_norm[j], mla_w_uq[j])
            k, v = _mla_expand(ckv, kpe, mla_w_ukv[j])
            q = _rms_norm(q, mla_q_norm[j])
            k = _rms_norm(k, mla_k_norm[j])
            out = _merge_heads(_gqa_attend(q[:, :, None], k, v)[:, :, 0]) @ mla_w_o[j]
            mla_ckv.append(ckv)
            mla_kpe.append(kpe)
        xp = xp + g1 * out
        h = _modulate(_rms_norm(xp, norm_ffn[i]), sh2, sc2)
        xp = xp + g2 * _sq_relu_mlp(h, mlp_w1[i], mlp_w2[i])
    y_prompt = xp

    xs = x_sample
    rows = x_sample.shape[1] // GRID_W
    cos_att, sin_att = _axial_rope_tables(rows, ATT_HEAD_DIM)
    cos_diff, sin_diff = _axial_rope_tables(rows, DIFF_HEAD_DIM)
    cos_swa, sin_swa = _axial_rope_tables(rows, SWA_HEAD_DIM)
    cos_mla, sin_mla = _axial_rope_tables(rows, MLA_ROPE)
    for i in range(DEPTH):
        m, j = i % N_MIXERS, i // N_MIXERS
        sh1, sc1, g1, sh2, sc2, g2 = _modulation(c, ada_w[i], ada_b[i])
        h = _modulate(_rms_norm(xs, norm_mix[i]), sh1, sc1)
        if m == 0:
            q, k, v = _gqa_project(h, att_w_qkv[j], att_q_norm[j], att_k_norm[j],
                                   ATT_HEADS, ATT_KV_HEADS, ATT_HEAD_DIM)
            q = _apply_rope(q, cos_att, sin_att)
            k = _apply_rope(k, cos_att, sin_att)
            k_all = jnp.concatenate([k, cache_att_k[:, j]], axis=2)
            v_all = jnp.concatenate([v, cache_att_v[:, j]], axis=2)
            out = _merge_grouped(_gqa_attend(q, k_all, v_all)) @ att_w_o[j]
        elif m == 1:
            q, k, v = _diff_project(h, diff_w_qkv[j], diff_q_norm[j], diff_k_norm[j])
            q = _apply_rope(q, cos_diff, sin_diff)
            k = _apply_rope(k, cos_diff, sin_diff)
            lam, lam_init = _diff_lambda(i, diff_lq1[j], diff_lk1[j], diff_lq2[j], diff_lk2[j])
            k_all = jnp.concatenate([k, cache_diff_k[:, j]], axis=3)
            v_all = jnp.concatenate([v, cache_diff_v[:, j]], axis=2)
            o = _rms_norm(_diff_attend(q, k_all, v_all, lam), diff_subln[j]) * (1.0 - lam_init)
            out = _merge_heads(o) @ diff_w_o[j]
        elif m == 2:
            q, k, v = _gqa_project(h, swa_w_qkv[j], swa_q_norm[j], swa_k_norm[j],
                                   SWA_HEADS, SWA_KV_HEADS, SWA_HEAD_DIM)
            q = _apply_rope(q, cos_swa, sin_swa)
            k = _apply_rope(k, cos_swa, sin_swa)
            o = _swa_attend_latent(q, k, v, cache_swa_k[:, j], cache_swa_v[:, j], swa_sink[j])
            out = _merge_grouped(o) @ swa_w_o[j]
        else:
            q, ckv, kpe = _mla_split(h, mla_w_in[j], mla_q_a_norm[j], mla_kv_a_norm[j], mla_w_uq[j])
            k_lat, v_lat = _mla_expand(ckv, kpe, mla_w_ukv[j])
            k_ctx, v_ctx = _mla_expand(cache_mla_ckv[:, j], cache_mla_kpe[:, j], mla_w_ukv[j])
            q = _rope_tail(_rms_norm(q, mla_q_norm[j]), cos_mla, sin_mla)
            k_lat = _rope_tail(_rms_norm(k_lat, mla_k_norm[j]), cos_mla, sin_mla)
            k_ctx = _rms_norm(k_ctx, mla_k_norm[j])
            k_all = jnp.concatenate([k_lat, k_ctx], axis=2)
            v_all = jnp.concatenate([v_lat, v_ctx], axis=2)
            out = _merge_heads(_gqa_attend(q[:, :, None], k_all, v_all)[:, :, 0]) @ mla_w_o[j]
        xs = xs + g1 * out
        h = _modulate(_rms_norm(xs, norm_ffn[i]), sh2, sc2)
        xs = xs + g2 * _sq_relu_mlp(h, mlp_w1[i], mlp_w2[i])
    y_sample = xs

    new_att_k = jnp.stack(att_k, axis=1)
    new_att_v = jnp.stack(att_v, axis=1)
    new_diff_k = jnp.stack(diff_k, axis=1)
    new_diff_v = jnp.stack(diff_v, axis=1)
    new_swa_k = jnp.stack(swa_k, axis=1)
    new_swa_v = jnp.stack(swa_v, axis=1)
    new_mla_ckv = jnp.stack(mla_ckv, axis=1)
    new_mla_kpe = jnp.stack(mla_kpe, axis=1)
    return (y_prompt, y_sample, new_att_k, new_att_v, new_diff_k, new_diff_v,
            new_swa_k, new_swa_v, new_mla_ckv, new_mla_kpe)
```

```python
import functools
import math

import numpy as np
import jax
import jax.numpy as jnp
from jax import lax
from jax.experimental import pallas as pl
from jax.experimental.pallas import tpu as pltpu

D_MODEL = 1024
BATCH = 16
SEQ = 256
DEPTH = 4
DEC_BATCH = 2
DEC_SEQ = 1024
PAST_LEN = 256
GRID_W = 64
ROPE_THETA = 10000.0
EPS = 1e-6
D_FF = 4 * D_MODEL
MOD_CHUNKS = 6

ATT_HEADS, ATT_KV_HEADS, ATT_HEAD_DIM = 8, 2, 128
DIFF_HEADS, DIFF_HEAD_DIM = 8, 64
SWA_HEADS, SWA_KV_HEADS, SWA_HEAD_DIM, WINDOW = 16, 4, 64, 128
MLA_HEADS, MLA_NOPE, MLA_ROPE, MLA_VDIM = 8, 128, 64, 128
MLA_Q_RANK, MLA_KV_RANK = 512, 256

LANES = 128
HALF = LANES // 2
TM = 256
N_PROMPT_TOK = BATCH * SEQ
N_TOK = N_PROMPT_TOK + DEC_BATCH * DEC_SEQ
N_TILES = N_TOK // TM
N_PROMPT_TILES = N_PROMPT_TOK // TM
TILES_PER_DEC = DEC_SEQ // TM
COND_ROWS = 8
MLP_TM = 512
MLP_FF_CHUNK = 1024
SWA_QB = 128
VMEM_LIMIT = 56 * 1024 * 1024

F32 = jnp.float32
BF16 = jnp.bfloat16


def _cparams(sem):
    return pltpu.CompilerParams(dimension_semantics=sem, vmem_limit_bytes=VMEM_LIMIT)


def _dot(a, b):
    return jnp.dot(a, b, preferred_element_type=F32)


def _dot_nt(a, b):
    return lax.dot_general(a, b, (((1,), (1,)), ((), ())), preferred_element_type=F32)


def _const_spec(shape):
    nd = len(shape)
    return pl.BlockSpec(shape, lambda *_: (0,) * nd, pipeline_mode=pl.Buffered(1))


def _tile_group(i):
    return jnp.where(i < N_PROMPT_TILES, 0, 1 + (i - N_PROMPT_TILES) // TILES_PER_DEC)


def _rope_tile(i):
    return jnp.maximum(i - N_PROMPT_TILES, 0) % TILES_PER_DEC


def _norm_mod(x, gain, shift, scale):
    ms = jnp.mean(x * x, axis=-1, keepdims=True)
    y = x * lax.rsqrt(ms + EPS) * gain
    return y * (1.0 + scale) + shift


def _lane_lo(shape):
    return lax.broadcasted_iota(jnp.int32, shape, len(shape) - 1) < HALF


def _rms_full(y, gain):
    ms = jnp.mean(y * y, axis=-1, keepdims=True)
    return y * lax.rsqrt(ms + EPS) * gain


def _rms_halves(y, gain):
    lo = _lane_lo(y.shape)
    sq = y * y
    s_lo = jnp.sum(jnp.where(lo, sq, 0.0), axis=-1, keepdims=True)
    s_hi = jnp.sum(jnp.where(lo, 0.0, sq), axis=-1, keepdims=True)
    r = jnp.where(lo, lax.rsqrt(s_lo * (1.0 / HALF) + EPS), lax.rsqrt(s_hi * (1.0 / HALF) + EPS))
    return y * r * gain


def _rope(y, cos, sin_prev, sin_next, quarter):
    return (y * cos + pltpu.roll(y, quarter, 1) * sin_prev
            + pltpu.roll(y, LANES - quarter, 1) * sin_next)


def _rope_tables(rot_dim):
    half = rot_dim // 2
    quarter = rot_dim // 4
    inv = np.float32(ROPE_THETA) ** (-np.arange(0, half, 2, dtype=np.float32) / np.float32(half))
    pos = np.arange(DEC_SEQ)
    row = (pos // GRID_W).astype(np.float32)
    col = (pos % GRID_W).astype(np.float32)
    lane = np.arange(LANES)
    dd = lane % rot_dim
    q = dd // quarter
    f = dd % quarter
    ang = np.where((q < 2)[None, :], row[:, None], col[:, None]) * inv[f][None, :]
    ang = ang.astype(np.float32)
    cos = np.cos(ang).astype(np.float32)
    sin = np.sin(ang).astype(np.float32)
    odd = (q % 2 == 1)[None, :]
    sin_prev = np.where(odd, sin, 0.0).astype(np.float32)
    sin_next = np.where(odd, 0.0, -sin).astype(np.float32)
    return jnp.asarray(cos), jnp.asarray(sin_prev), jnp.asarray(sin_next)


def _softmax_parts(s_list, extra=None):
    m = jnp.max(s_list[0], axis=-1, keepdims=True)
    for s in s_list[1:]:
        m = jnp.maximum(m, jnp.max(s, axis=-1, keepdims=True))
    if extra is not None:
        m = jnp.maximum(m, extra)
    ps = [jnp.exp(s - m) for s in s_list]
    tot = ps[0].sum(axis=-1, keepdims=True)
    for p in ps[1:]:
        tot = tot + p.sum(axis=-1, keepdims=True)
    if extra is not None:
        tot = tot + jnp.exp(extra - m)
    return ps, 1.0 / tot


def _mod_kernel(cond_ref, w_ref, b_ref, o_ref):
    c = cond_ref[...]
    s = (c * jax.nn.sigmoid(c)).astype(BF16)
    o_ref[0] = _dot(s, w_ref[0].astype(BF16)) + b_ref[0]


def _modulation(cond, ada_w, ada_b):
    tn = 1536
    n = MOD_CHUNKS * D_MODEL
    return pl.pallas_call(
        _mod_kernel,
        grid=(DEPTH, n // tn),
        in_specs=[
            pl.BlockSpec((COND_ROWS, D_MODEL), lambda l, j: (0, 0)),
            pl.BlockSpec((1, D_MODEL, tn), lambda l, j: (l, 0, j)),
            pl.BlockSpec((1, 1, tn), lambda l, j: (l, 0, j)),
        ],
        out_specs=pl.BlockSpec((1, COND_ROWS, tn), lambda l, j: (l, 0, j)),
        out_shape=jax.ShapeDtypeStruct((DEPTH, COND_ROWS, n), F32),
        compiler_params=_cparams(("arbitrary", "arbitrary")),
        name="modulation",
    )(cond, ada_w, ada_b.reshape(DEPTH, 1, n))


def _mod_spec(chunk):
    return pl.BlockSpec((COND_ROWS, D_MODEL), lambda i: (0, chunk))


def _mod_row(ref, i):
    return ref[pl.ds(_tile_group(i), 1), :]


_ROPE_SPEC = pl.BlockSpec((TM, LANES), lambda i: (_rope_tile(i), 0))


def _tok_spec(width):
    return pl.BlockSpec((TM, width), lambda i: (i, 0))


def _proj_att_kernel(x_ref, gain_ref, sh_ref, sc_ref, w_ref, qg_ref, kg_ref,
                     cos_ref, sp_ref, sn_ref, q_ref, k_ref, v_ref):
    i = pl.program_id(0)
    is_lat = i >= N_PROMPT_TILES
    h = _norm_mod(x_ref[...], gain_ref[...], _mod_row(sh_ref, i), _mod_row(sc_ref, i)).astype(BF16)
    y = _dot(h, w_ref[...])
    cos, sp, sn = cos_ref[...], sp_ref[...], sn_ref[...]
    nq, nk = ATT_HEADS, ATT_KV_HEADS
    for c in range(nq + 2 * nk):
        yc = y[:, c * LANES:(c + 1) * LANES]
        if c < nq + nk:
            yc = _rms_full(yc, qg_ref[...] if c < nq else kg_ref[...])
            yc = jnp.where(is_lat, _rope(yc, cos, sp, sn, ATT_HEAD_DIM // 4), yc)
        if c < nq:
            q_ref[:, c * LANES:(c + 1) * LANES] = yc.astype(BF16)
        elif c < nq + nk:
            k_ref[:, (c - nq) * LANES:(c - nq + 1) * LANES] = yc
        else:
            v_ref[:, (c - nq - nk) * LANES:(c - nq - nk + 1) * LANES] = yc


def _proj_att(x, mods, gain, w, qg, kg, tables):
    nq, nk = ATT_HEADS * ATT_HEAD_DIM, ATT_KV_HEADS * ATT_HEAD_DIM
    return pl.pallas_call(
        _proj_att_kernel,
        grid=(N_TILES,),
        in_specs=[_tok_spec(D_MODEL), _const_spec((1, D_MODEL)), _mod_spec(0), _mod_spec(1),
                  _const_spec(w.shape), _const_spec((1, LANES)), _const_spec((1, LANES)),
                  _ROPE_SPEC, _ROPE_SPEC, _ROPE_SPEC],
        out_specs=[_tok_spec(nq), _tok_spec(nk), _tok_spec(nk)],
        out_shape=[jax.ShapeDtypeStruct((N_TOK, nq), BF16),
                   jax.ShapeDtypeStruct((N_TOK, nk), F32),
                   jax.ShapeDtypeStruct((N_TOK, nk), F32)],
        compiler_params=_cparams(("arbitrary",)),
        name="proj_att",
    )(x, gain, mods, mods, w, qg, kg, *tables)


def _proj_diff_kernel(x_ref, gain_ref, sh_ref, sc_ref, w_ref, qg_ref, kg_ref,
                      cos_ref, sp_ref, sn_ref, q_ref, k_ref, v_ref):
    i = pl.program_id(0)
    is_lat = i >= N_PROMPT_TILES
    h = _norm_mod(x_ref[...], gain_ref[...], _mod_row(sh_ref, i), _mod_row(sc_ref, i)).astype(BF16)
    cos, sp, sn = cos_ref[...], sp_ref[...], sn_ref[...]
    nh = DIFF_HEADS
    for part, (g_ref, o_ref) in enumerate(((qg_ref, q_ref), (kg_ref, k_ref), (None, v_ref))):
        y = _dot(h, w_ref[:, part * D_MODEL:(part + 1) * D_MODEL])
        for c in range(nh):
            yc = y[:, c * LANES:(c + 1) * LANES]
            if g_ref is not None:
                yc = _rms_halves(yc, g_ref[...])
                yc = jnp.where(is_lat, _rope(yc, cos, sp, sn, DIFF_HEAD_DIM // 4), yc)
            o_ref[:, c * LANES:(c + 1) * LANES] = yc.astype(o_ref.dtype)


def _proj_diff(x, mods, gain, w, qg, kg, tables):
    n = DIFF_HEADS * 2 * DIFF_HEAD_DIM
    return pl.pallas_call(
        _proj_diff_kernel,
        grid=(N_TILES,),
        in_specs=[_tok_spec(D_MODEL), _const_spec((1, D_MODEL)), _mod_spec(0), _mod_spec(1),
                  _const_spec(w.shape), _const_spec((1, LANES)), _const_spec((1, LANES)),
                  _ROPE_SPEC, _ROPE_SPEC, _ROPE_SPEC],
        out_specs=[_tok_spec(n), _tok_spec(n), _tok_spec(n)],
        out_shape=[jax.ShapeDtypeStruct((N_TOK, n), BF16),
                   jax.ShapeDtypeStruct((N_TOK, n), F32),
                   jax.ShapeDtypeStruct((N_TOK, n), F32)],
        compiler_params=_cparams(("arbitrary",)),
        name="proj_diff",
    )(x, gain, mods, mods, w, qg, kg, *tables)


def _dup_halves(yc):
    lo = _lane_lo(yc.shape)
    sw = pltpu.roll(yc, HALF, 1)
    return jnp.where(lo, yc, sw), jnp.where(lo, sw, yc)


def _proj_swa_kernel(x_ref, gain_ref, sh_ref, sc_ref, w_ref, qg_ref, kg_ref,
                     cos_ref, sp_ref, sn_ref, q_ref, k_ref, v_ref, kd_ref, vd_ref):
    i = pl.program_id(0)
    is_lat = i >= N_PROMPT_TILES
    h = _norm_mod(x_ref[...], gain_ref[...], _mod_row(sh_ref, i), _mod_row(sc_ref, i)).astype(BF16)
    y = _dot(h, w_ref[...])
    cos, sp, sn = cos_ref[...], sp_ref[...], sn_ref[...]
    nq = SWA_HEADS * SWA_HEAD_DIM // LANES
    nk = SWA_KV_HEADS * SWA_HEAD_DIM // LANES
    for c in range(nq + 2 * nk):
        yc = y[:, c * LANES:(c + 1) * LANES]
        if c < nq + nk:
            yc = _rms_halves(yc, qg_ref[...] if c < nq else kg_ref[...])
            yc = jnp.where(is_lat, _rope(yc, cos, sp, sn, SWA_HEAD_DIM // 4), yc)
        if c < nq:
            q_ref[:, c * LANES:(c + 1) * LANES] = yc.astype(BF16)
            continue
        j = c - nq if c < nq + nk else c - nq - nk
        o_ref, d_ref = (k_ref, kd_ref) if c < nq + nk else (v_ref, vd_ref)
        o_ref[:, j * LANES:(j + 1) * LANES] = yc
        da, db = _dup_halves(yc)
        d_ref[:, (2 * j) * LANES:(2 * j + 1) * LANES] = da.astype(BF16)
        d_ref[:, (2 * j + 1) * LANES:(2 * j + 2) * LANES] = db.astype(BF16)


def _proj_swa(x, mods, gain, w, qg, kg, tables):
    nq, nk = SWA_HEADS * SWA_HEAD_DIM, SWA_KV_HEADS * SWA_HEAD_DIM
    return pl.pallas_call(
        _proj_swa_kernel,
        grid=(N_TILES,),
        in_specs=[_tok_spec(D_MODEL), _const_spec((1, D_MODEL)), _mod_spec(0), _mod_spec(1),
                  _const_spec(w.shape), _const_spec((1, LANES)), _const_spec((1, LANES)),
                  _ROPE_SPEC, _ROPE_SPEC, _ROPE_SPEC],
        out_specs=[_tok_spec(nq), _tok_spec(nk), _tok_spec(nk), _tok_spec(2 * nk), _tok_spec(2 * nk)],
        out_shape=[jax.ShapeDtypeStruct((N_TOK, nq), BF16),
                   jax.ShapeDtypeStruct((N_TOK, nk), F32),
                   jax.ShapeDtypeStruct((N_TOK, nk), F32),
                   jax.ShapeDtypeStruct((N_TOK, 2 * nk), BF16),
                   jax.ShapeDtypeStruct((N_TOK, 2 * nk), BF16)],
        compiler_params=_cparams(("arbitrary",)),
        name="proj_swa",
    )(x, gain, mods, mods, w, qg, kg, *tables)


def _proj_mla_kernel(x_ref, gain_ref, sh_ref, sc_ref, w_in_ref, qa_ref, kva_ref, w_uq_ref,
                     qg_ref, qgp_ref, cos_ref, sp_ref, sn_ref,
                     qn_ref, qp_ref, ckv_ref, kpe_ref):
    i = pl.program_id(0)
    is_lat = i >= N_PROMPT_TILES
    h = _norm_mod(x_ref[...], gain_ref[...], _mod_row(sh_ref, i), _mod_row(sc_ref, i)).astype(BF16)
    y = _dot(h, w_in_ref[...])
    c_q = y[:, :MLA_Q_RANK]
    c_kv = y[:, MLA_Q_RANK:MLA_Q_RANK + MLA_KV_RANK]
    kpe_ref[...] = y[:, MLA_Q_RANK + MLA_KV_RANK:]
    ckv_ref[...] = (c_kv * lax.rsqrt(jnp.mean(c_kv * c_kv, axis=-1, keepdims=True) + EPS)
                    * kva_ref[...])
    cq = (c_q * lax.rsqrt(jnp.mean(c_q * c_q, axis=-1, keepdims=True) + EPS) * qa_ref[...])
    q = _dot(cq.astype(BF16), w_uq_ref[...])
    cos, sp, sn = cos_ref[...], sp_ref[...], sn_ref[...]
    n_nope = MLA_HEADS * MLA_NOPE
    lo = _lane_lo((TM, LANES))
    inv_d = 1.0 / (MLA_NOPE + MLA_ROPE)
    for j in range(MLA_HEADS // 2):
        pe = q[:, n_nope + j * LANES:n_nope + (j + 1) * LANES]
        pe_sq = pe * pe
        rs = []
        for a in range(2):
            hh = 2 * j + a
            nope = q[:, hh * LANES:(hh + 1) * LANES]
            ss = (jnp.sum(nope * nope, axis=-1, keepdims=True)
                  + jnp.sum(jnp.where(lo if a == 0 else ~lo, pe_sq, 0.0), axis=-1, keepdims=True))
            r = lax.rsqrt(ss * inv_d + EPS)
            rs.append(r)
            qn_ref[:, hh * LANES:(hh + 1) * LANES] = (nope * r * qg_ref[...]).astype(BF16)
        pe = pe * jnp.where(lo, rs[0], rs[1]) * qgp_ref[...]
        pe = jnp.where(is_lat, _rope(pe, cos, sp, sn, MLA_ROPE // 4), pe)
        qp_ref[:, j * LANES:(j + 1) * LANES] = pe.astype(BF16)


def _proj_mla(x, mods, gain, w_in, qa, kva, w_uq, qg, qgp, tables):
    n_nope = MLA_HEADS * MLA_NOPE
    n_pe = MLA_HEADS * MLA_ROPE
    return pl.pallas_call(
        _proj_mla_kernel,
        grid=(N_TILES,),
        in_specs=[_tok_spec(D_MODEL), _const_spec((1, D_MODEL)), _mod_spec(0), _mod_spec(1),
                  _const_spec(w_in.shape), _const_spec((1, MLA_Q_RANK)), _const_spec((1, MLA_KV_RANK)),
                  _const_spec(w_uq.shape), _const_spec((1, LANES)), _const_spec((1, LANES)),
                  _ROPE_SPEC, _ROPE_SPEC, _ROPE_SPEC],
        out_specs=[_tok_spec(n_nope), _tok_spec(n_pe), _tok_spec(MLA_KV_RANK), _tok_spec(LANES)],
        out_shape=[jax.ShapeDtypeStruct((N_TOK, n_nope), BF16),
                   jax.ShapeDtypeStruct((N_TOK, n_pe), BF16),
                   jax.ShapeDtypeStruct((N_TOK, MLA_KV_RANK), F32),
                   jax.ShapeDtypeStruct((N_TOK, LANES), F32)],
        compiler_params=_cparams(("arbitrary",)),
        name="proj_mla",
    )(x, gain, mods, mods, w_in, qa, kva, w_uq, qg, qgp, *tables)


def _mla_expand_kernel(ckv_ref, kpe_ref, w_ref, kg_ref, kgp_ref, cos_ref, sp_ref, sn_ref,
                       kn_ref, kp_ref, v_ref, *, rope_from_tile):
    i = pl.program_id(0)
    kv = _dot(ckv_ref[...].astype(BF16), w_ref[...])
    kpe = kpe_ref[...]
    lo = _lane_lo(kpe.shape)
    pe_ss = jnp.sum(jnp.where(lo, kpe * kpe, 0.0), axis=-1, keepdims=True)
    inv_d = 1.0 / (MLA_NOPE + MLA_ROPE)
    cos, sp, sn = cos_ref[...], sp_ref[...], sn_ref[...]
    for j in range(MLA_HEADS // 2):
        rs = []
        for a in range(2):
            hh = 2 * j + a
            kn = kv[:, hh * 2 * LANES:hh * 2 * LANES + LANES]
            v_ref[:, hh * LANES:(hh + 1) * LANES] = kv[:, hh * 2 * LANES + LANES:(hh + 1) * 2 * LANES].astype(BF16)
            r = lax.rsqrt((jnp.sum(kn * kn, axis=-1, keepdims=True) + pe_ss) * inv_d + EPS)
            rs.append(r)
            kn_ref[:, hh * LANES:(hh + 1) * LANES] = (kn * r * kg_ref[...]).astype(BF16)
        pe = kpe * jnp.where(lo, rs[0], rs[1]) * kgp_ref[...]
        if rope_from_tile is not None:
            pe = jnp.where(i >= rope_from_tile, _rope(pe, cos, sp, sn, MLA_ROPE // 4), pe)
        kp_ref[:, j * LANES:(j + 1) * LANES] = pe.astype(BF16)


def _mla_expand(ckv, kpe_dup, w_ukv, kg, kgp, tables, rope_from_tile):
    n = ckv.shape[0]
    n_nope = MLA_HEADS * MLA_NOPE
    n_pe = MLA_HEADS * MLA_ROPE
    return pl.pallas_call(
        functools.partial(_mla_expand_kernel, rope_from_tile=rope_from_tile),
        grid=(n // TM,),
        in_specs=[_tok_spec(MLA_KV_RANK), _tok_spec(LANES), _const_spec(w_ukv.shape),
                  _const_spec((1, LANES)), _const_spec((1, LANES)),
                  _ROPE_SPEC, _ROPE_SPEC, _ROPE_SPEC],
        out_specs=[_tok_spec(n_nope), _tok_spec(n_pe), _tok_spec(n_nope)],
        out_shape=[jax.ShapeDtypeStruct((n, n_nope), BF16),
                   jax.ShapeDtypeStruct((n, n_pe), BF16),
                   jax.ShapeDtypeStruct((n, n_nope), BF16)],
        compiler_params=_cparams(("arbitrary",)),
        name="mla_expand",
    )(ckv, kpe_dup, w_ukv, kg, kgp, *tables)


def _gqa_kernel(*refs, n_group, scale, with_ctx):
    if with_ctx:
        q_ref, k_ref, v_ref, kc_ref, vc_ref, o_ref = refs
        kc = kc_ref[...].astype(BF16)
        vc = vc_ref[...].astype(BF16)
    else:
        q_ref, k_ref, v_ref, o_ref = refs
    k = k_ref[...].astype(BF16)
    v = v_ref[...].astype(BF16)
    for g in range(n_group):
        q = q_ref[:, g * LANES:(g + 1) * LANES]
        s_list = [_dot_nt(q, k) * scale]
        if with_ctx:
            s_list.append(_dot_nt(q, kc) * scale)
        ps, inv = _softmax_parts(s_list)
        o = _dot(ps[0].astype(BF16), v)
        if with_ctx:
            o = o + _dot(ps[1].astype(BF16), vc)
        o_ref[:, g * LANES:(g + 1) * LANES] = (o * inv).astype(o_ref.dtype)


def _att_attend(q, k, v, cache_k, cache_v):
    ng = ATT_HEADS // ATT_KV_HEADS
    scale = ATT_HEAD_DIM ** -0.5
    gw = ng * LANES
    out_p = pl.pallas_call(
        functools.partial(_gqa_kernel, n_group=ng, scale=scale, with_ctx=False),
        grid=(N_PROMPT_TILES, ATT_KV_HEADS),
        in_specs=[pl.BlockSpec((TM, gw), lambda b, kv: (b, kv)),
                  pl.BlockSpec((TM, LANES), lambda b, kv: (b, kv)),
                  pl.BlockSpec((TM, LANES), lambda b, kv: (b, kv))],
        out_specs=pl.BlockSpec((TM, gw), lambda b, kv: (b, kv)),
        out_shape=jax.ShapeDtypeStruct((N_PROMPT_TOK, D_MODEL), BF16),
        compiler_params=_cparams(("arbitrary", "arbitrary")),
        name="att_prompt",
    )(q, k, v)
    lat0 = N_PROMPT_TOK // DEC_SEQ
    out_s = pl.pallas_call(
        functools.partial(_gqa_kernel, n_group=ng, scale=scale, with_ctx=True),
        grid=(DEC_BATCH, ATT_KV_HEADS, TILES_PER_DEC),
        in_specs=[pl.BlockSpec((TM, gw), lambda b, kv, t: (N_PROMPT_TILES + b * TILES_PER_DEC + t, kv)),
                  pl.BlockSpec((DEC_SEQ, LANES), lambda b, kv, t: (lat0 + b, kv)),
                  pl.BlockSpec((DEC_SEQ, LANES), lambda b, kv, t: (lat0 + b, kv)),
                  pl.BlockSpec((None, None, None, PAST_LEN, LANES), lambda b, kv, t: (b, 0, kv, 0, 0)),
                  pl.BlockSpec((None, None, None, PAST_LEN, LANES), lambda b, kv, t: (b, 0, kv, 0, 0))],
        out_specs=pl.BlockSpec((TM, gw), lambda b, kv, t: (b * TILES_PER_DEC + t, kv)),
        out_shape=jax.ShapeDtypeStruct((DEC_BATCH * DEC_SEQ, D_MODEL), BF16),
        compiler_params=_cparams(("arbitrary", "arbitrary", "arbitrary")),
        name="att_latent",
    )(q, k, v, cache_k, cache_v)
    return out_p, out_s


def _diff_kernel(*refs, lam_init, with_ctx):
    if with_ctx:
        (q_ref, k_ref, v_ref, kc_ref, vc_ref, lq1_ref, lk1_ref, lq2_ref, lk2_ref, sub_ref, o_ref) = refs
        kc = kc_ref[...].astype(BF16)
        vc = vc_ref[...].astype(BF16)
    else:
        (q_ref, k_ref, v_ref, lq1_ref, lk1_ref, lq2_ref, lk2_ref, sub_ref, o_ref) = refs
    scale = DIFF_HEAD_DIM ** -0.5
    lam = (jnp.exp(jnp.sum(lq1_ref[...] * lk1_ref[...], axis=-1, keepdims=True))
           - jnp.exp(jnp.sum(lq2_ref[...] * lk2_ref[...], axis=-1, keepdims=True)) + lam_init)
    q = q_ref[...]
    k = k_ref[...].astype(BF16)
    v = v_ref[...].astype(BF16)
    lo = _lane_lo(q.shape)
    zero = jnp.zeros_like(q)
    acc = None
    for half in range(2):
        qh = jnp.where(lo, q, zero) if half == 0 else jnp.where(lo, zero, q)
        s_list = [_dot_nt(qh, k) * scale]
        if with_ctx:
            s_list.append(_dot_nt(qh, kc) * scale)
        ps, inv = _softmax_parts(s_list)
        coef = inv if half == 0 else -lam * inv
        ps = [p * coef for p in ps]
        acc = ps if acc is None else [a + p for a, p in zip(acc, ps)]
    o = _dot(acc[0].astype(BF16), v)
    if with_ctx:
        o = o + _dot(acc[1].astype(BF16), vc)
    o = o * lax.rsqrt(jnp.mean(o * o, axis=-1, keepdims=True) + EPS) * sub_ref[...]
    o_ref[...] = (o * (1.0 - lam_init)).astype(o_ref.dtype)


def _diff_attend(q, k, v, cache_k_pair, cache_v, lq1, lk1, lq2, lk2, subln, lam_init):
    small = [lq1, lk1, lq2, lk2]
    lat0 = N_PROMPT_TOK // DEC_SEQ
    sm2 = [pl.BlockSpec((1, DIFF_HEAD_DIM), lambda b, h: (0, 0))] * 4
    out_p = pl.pallas_call(
        functools.partial(_diff_kernel, lam_init=lam_init, with_ctx=False),
        grid=(N_PROMPT_TILES, DIFF_HEADS),
        in_specs=[pl.BlockSpec((TM, LANES), lambda b, h: (b, h))] * 3 + sm2
                 + [pl.BlockSpec((1, LANES), lambda b, h: (0, 0))],
        out_specs=pl.BlockSpec((TM, LANES), lambda b, h: (b, h)),
        out_shape=jax.ShapeDtypeStruct((N_PROMPT_TOK, D_MODEL), BF16),
        compiler_params=_cparams(("arbitrary", "arbitrary")),
        name="diff_prompt",
    )(q, k, v, *small, subln)
    sm3 = [pl.BlockSpec((1, DIFF_HEAD_DIM), lambda b, h, t: (0, 0))] * 4
    out_s = pl.pallas_call(
        functools.partial(_diff_kernel, lam_init=lam_init, with_ctx=True),
        grid=(DEC_BATCH, DIFF_HEADS, TILES_PER_DEC),
        in_specs=[pl.BlockSpec((TM, LANES), lambda b, h, t: (N_PROMPT_TILES + b * TILES_PER_DEC + t, h)),
                  pl.BlockSpec((DEC_SEQ, LANES), lambda b, h, t: (lat0 + b, h)),
                  pl.BlockSpec((DEC_SEQ, LANES), lambda b, h, t: (lat0 + b, h)),
                  pl.BlockSpec((None, None, PAST_LEN, LANES), lambda b, h, t: (b, h, 0, 0)),
                  pl.BlockSpec((None, None, None, PAST_LEN, LANES), lambda b, h, t: (b, 0, h, 0, 0))]
                 + sm3 + [pl.BlockSpec((1, LANES), lambda b, h, t: (0, 0))],
        out_specs=pl.BlockSpec((TM, LANES), lambda b, h, t: (b * TILES_PER_DEC + t, h)),
        out_shape=jax.ShapeDtypeStruct((DEC_BATCH * DEC_SEQ, D_MODEL), BF16),
        compiler_params=_cparams(("arbitrary", "arbitrary", "arbitrary")),
        name="diff_latent",
    )(q, k, v, cache_k_pair, cache_v, *small, subln)
    return out_p, out_s


def _swa_head_pair(q, sink_ref, head0, score_parts, value_parts):
    lo = _lane_lo(q.shape)
    zero = jnp.zeros_like(q)
    outs = []
    for a in range(2):
        qh = jnp.where(lo, q, zero) if a == 0 else jnp.where(lo, zero, q)
        s_list = [fn(qh) for fn in score_parts]
        sink = jnp.full((1, 1), sink_ref[head0 + a], F32)
        ps, inv = _softmax_parts(s_list, extra=sink)
        o = None
        for p, vv in zip(ps, value_parts):
            t = _dot(p.astype(BF16), vv)
            o = t if o is None else o + t
        outs.append(o * inv)
    lo_o = _lane_lo(outs[0].shape)
    return jnp.where(lo_o, outs[0], outs[1])


def _swa_prompt_kernel(sink_ref, q_ref, k_ref, v_ref, o_ref):
    kv = pl.program_id(1)
    scale = SWA_HEAD_DIM ** -0.5
    k = k_ref[...]
    v = v_ref[...]
    n_chunk = SWA_HEADS // SWA_KV_HEADS // 2
    for c in range(n_chunk):
        q = q_ref[:, c * LANES:(c + 1) * LANES]
        head0 = kv * (SWA_HEADS // SWA_KV_HEADS) + 2 * c
        o = _swa_head_pair(q, sink_ref, head0, [lambda qh: _dot_nt(qh, k) * scale], [v])
        o_ref[:, c * LANES:(c + 1) * LANES] = o.astype(o_ref.dtype)


def _swa_latent_kernel(sink_ref, q_ref, k_ref, v_ref, kc_ref, vc_ref, o_ref):
    kv = pl.program_id(1)
    n = pl.program_id(2)
    scale = SWA_HEAD_DIM ** -0.5
    span = 3 * SWA_QB
    start = pl.multiple_of(jnp.clip((n - 1) * SWA_QB, 0, DEC_SEQ - span), SWA_QB)
    k = k_ref[pl.ds(start, span), :]
    v = v_ref[pl.ds(start, span), :]
    kc = kc_ref[...]
    vc = vc_ref[...]
    qpos = n * SWA_QB + lax.broadcasted_iota(jnp.int32, (SWA_QB, span), 0)
    kpos = start + lax.broadcasted_iota(jnp.int32, (SWA_QB, span), 1)
    valid = jnp.abs(qpos - kpos) <= WINDOW
    n_chunk = SWA_HEADS // SWA_KV_HEADS // 2

    def s_loc(qh):
        return jnp.where(valid, _dot_nt(qh, k) * scale, -1e30)

    def s_ctx(qh):
        return _dot_nt(qh, kc) * scale

    for c in range(n_chunk):
        q = q_ref[:, c * LANES:(c + 1) * LANES]
        head0 = kv * (SWA_HEADS // SWA_KV_HEADS) + 2 * c
        o = _swa_head_pair(q, sink_ref, head0, [s_loc, s_ctx], [v, vc])
        o_ref[:, c * LANES:(c + 1) * LANES] = o.astype(o_ref.dtype)


def _swa_attend(q, kd, vd, cache_kd, cache_vd, sink):
    gw = (SWA_HEADS // SWA_KV_HEADS) * SWA_HEAD_DIM
    smem = pl.BlockSpec(memory_space=pltpu.SMEM)
    out_p = pl.pallas_call(
        _swa_prompt_kernel,
        grid=(N_PROMPT_TILES, SWA_KV_HEADS),
        in_specs=[smem,
                  pl.BlockSpec((TM, gw), lambda b, kv: (b, kv)),
                  pl.BlockSpec((TM, LANES), lambda b, kv: (b, kv)),
                  pl.BlockSpec((TM, LANES), lambda b, kv: (b, kv))],
        out_specs=pl.BlockSpec((TM, gw), lambda b, kv: (b, kv)),
        out_shape=jax.ShapeDtypeStruct((N_PROMPT_TOK, D_MODEL), BF16),
        compiler_params=_cparams(("arbitrary", "arbitrary")),
        name="swa_prompt",
    )(sink, q, kd, vd)
    lat0 = N_PROMPT_TOK // DEC_SEQ
    nqb = DEC_SEQ // SWA_QB
    q0 = N_PROMPT_TOK // SWA_QB
    out_s = pl.pallas_call(
        _swa_latent_kernel,
        grid=(DEC_BATCH, SWA_KV_HEADS, nqb),
        in_specs=[smem,
                  pl.BlockSpec((SWA_QB, gw), lambda b, kv, n: (q0 + b * nqb + n, kv)),
                  pl.BlockSpec((DEC_SEQ, LANES), lambda b, kv, n: (lat0 + b, kv)),
                  pl.BlockSpec((DEC_SEQ, LANES), lambda b, kv, n: (lat0 + b, kv)),
                  pl.BlockSpec((None, None, PAST_LEN, LANES), lambda b, kv, n: (b, kv, 0, 0)),
                  pl.BlockSpec((None, None, PAST_LEN, LANES), lambda b, kv, n: (b, kv, 0, 0))],
        out_specs=pl.BlockSpec((SWA_QB, gw), lambda b, kv, n: (b * nqb + n, kv)),
        out_shape=jax.ShapeDtypeStruct((DEC_BATCH * DEC_SEQ, D_MODEL), BF16),
        compiler_params=_cparams(("arbitrary", "arbitrary", "arbitrary")),
        name="swa_latent",
    )(sink, q, kd, vd, cache_kd, cache_vd)
    return out_p, out_s


def _mla_kernel(*refs, with_ctx):
    if with_ctx:
        (qn_ref, qp_ref, kn_ref, kp_ref, v_ref, knc_ref, kpc_ref, vc_ref, o_ref) = refs
    else:
        (qn_ref, qp_ref, kn_ref, kp_ref, v_ref, o_ref) = refs
    hh = pl.program_id(1)
    scale = (MLA_NOPE + MLA_ROPE) ** -0.5
    qn = qn_ref[...]
    qp = qp_ref[...]
    lane_half = lax.broadcasted_iota(jnp.int32, qp.shape, 1) // HALF
    qp = jnp.where(lane_half == hh % 2, qp, jnp.zeros_like(qp))
    s_list = [(_dot_nt(qn, kn_ref[...]) + _dot_nt(qp, kp_ref[...])) * scale]
    if with_ctx:
        s_list.append((_dot_nt(qn, knc_ref[...]) + _dot_nt(qp, kpc_ref[...])) * scale)
    ps, inv = _softmax_parts(s_list)
    o = _dot(ps[0].astype(BF16), v_ref[...])
    if with_ctx:
        o = o + _dot(ps[1].astype(BF16), vc_ref[...])
    o_ref[...] = (o * inv).astype(o_ref.dtype)


def _mla_attend(qn, qp, kn, kp, v, knc, kpc, vc):
    out_p = pl.pallas_call(
        functools.partial(_mla_kernel, with_ctx=False),
        grid=(N_PROMPT_TILES, MLA_HEADS),
        in_specs=[pl.BlockSpec((TM, LANES), lambda b, h: (b, h)),
                  pl.BlockSpec((TM, LANES), lambda b, h: (b, h // 2)),
                  pl.BlockSpec((TM, LANES), lambda b, h: (b, h)),
                  pl.BlockSpec((TM, LANES), lambda b, h: (b, h // 2)),
                  pl.BlockSpec((TM, LANES), lambda b, h: (b, h))],
        out_specs=pl.BlockSpec((TM, LANES), lambda b, h: (b, h)),
        out_shape=jax.ShapeDtypeStruct((N_PROMPT_TOK, D_MODEL), BF16),
        compiler_params=_cparams(("arbitrary", "arbitrary")),
        name="mla_prompt",
    )(qn, qp, kn, kp, v)
    lat0 = N_PROMPT_TOK // DEC_SEQ

    def qrow(b, h, t):
        return N_PROMPT_TILES + b * TILES_PER_DEC + t

    out_s = pl.pallas_call(
        functools.partial(_mla_kernel, with_ctx=True),
        grid=(DEC_BATCH, MLA_HEADS, TILES_PER_DEC),
        in_specs=[pl.BlockSpec((TM, LANES), lambda b, h, t: (qrow(b, h, t), h)),
                  pl.BlockSpec((TM, LANES), lambda b, h, t: (qrow(b, h, t), h // 2)),
                  pl.BlockSpec((DEC_SEQ, LANES), lambda b, h, t: (lat0 + b, h)),
                  pl.BlockSpec((DEC_SEQ, LANES), lambda b, h, t: (lat0 + b, h // 2)),
                  pl.BlockSpec((DEC_SEQ, LANES), lambda b, h, t: (lat0 + b, h)),
                  pl.BlockSpec((PAST_LEN, LANES), lambda b, h, t: (b, h)),
                  pl.BlockSpec((PAST_LEN, LANES), lambda b, h, t: (b, h // 2)),
                  pl.BlockSpec((PAST_LEN, LANES), lambda b, h, t: (b, h))],
        out_specs=pl.BlockSpec((TM, LANES), lambda b, h, t: (b * TILES_PER_DEC + t, h)),
        out_shape=jax.ShapeDtypeStruct((DEC_BATCH * DEC_SEQ, D_MODEL), BF16),
        compiler_params=_cparams(("arbitrary", "arbitrary", "arbitrary")),
        name="mla_latent",
    )(qn, qp, kn, kp, v, knc, kpc, vc)
    return out_p, out_s


def _oproj_kernel(ap_ref, as_ref, w_ref, x_ref, g1_ref, gain_ref, sh_ref, sc_ref, x1_ref, h2_ref):
    i = pl.program_id(0)
    a = jnp.where(i < N_PROMPT_TILES, ap_ref[...], as_ref[...])
    o = _dot(a, w_ref[...])
    x1 = x_ref[...] + _mod_row(g1_ref, i) * o
    x1_ref[...] = x1
    h2_ref[...] = _norm_mod(x1, gain_ref[...], _mod_row(sh_ref, i), _mod_row(sc_ref, i)).astype(BF16)


def _oproj(attn_p, attn_s, w_o, x, mods, gain_ffn):
    return pl.pallas_call(
        _oproj_kernel,
        grid=(N_TILES,),
        in_specs=[pl.BlockSpec((TM, D_MODEL), lambda i: (jnp.minimum(i, N_PROMPT_TILES - 1), 0)),
                  pl.BlockSpec((TM, D_MODEL), lambda i: (jnp.maximum(i - N_PROMPT_TILES, 0), 0)),
                  _const_spec(w_o.shape), _tok_spec(D_MODEL), _mod_spec(2),
                  _const_spec((1, D_MODEL)), _mod_spec(3), _mod_spec(4)],
        out_specs=[_tok_spec(D_MODEL), _tok_spec(D_MODEL)],
        out_shape=[jax.ShapeDtypeStruct((N_TOK, D_MODEL), F32),
                   jax.ShapeDtypeStruct((N_TOK, D_MODEL), BF16)],
        compiler_params=_cparams(("arbitrary",)),
        name="oproj",
    )(attn_p, attn_s, w_o, x, mods, gain_ffn, mods, mods)


def _mlp_kernel(h_ref, w1_ref, w2_ref, x_ref, g2_ref, o_ref):
    i = pl.program_id(0)
    grp = _tile_group(i * (MLP_TM // TM))
    h = h_ref[...]
    acc = None
    for c in range(D_FF // MLP_FF_CHUNK):
        u = _dot(h, w1_ref[:, c * MLP_FF_CHUNK:(c + 1) * MLP_FF_CHUNK])
        u = jnp.square(jnp.maximum(u, 0.0)).astype(BF16)
        t = _dot(u, w2_ref[c * MLP_FF_CHUNK:(c + 1) * MLP_FF_CHUNK, :])
        acc = t if acc is None else acc + t
    o_ref[...] = x_ref[...] + g2_ref[pl.ds(grp, 1), :] * acc


def _mlp(h2, w1, w2, x1, mods):
    spec = pl.BlockSpec((MLP_TM, D_MODEL), lambda i: (i, 0))
    return pl.pallas_call(
        _mlp_kernel,
        grid=(N_TOK // MLP_TM,),
        in_specs=[spec, _const_spec(w1.shape), _const_spec(w2.shape), spec, _mod_spec(5)],
        out_specs=spec,
        out_shape=jax.ShapeDtypeStruct((N_TOK, D_MODEL), F32),
        compiler_params=_cparams(("arbitrary",)),
        name="mlp",
    )(h2, w1, w2, x1, mods)


def _row(v):
    return v.reshape(1, -1).astype(F32)


def _pair(v):
    return jnp.concatenate([v, v]).reshape(1, LANES).astype(F32)


def _heads_out(flat, heads, dim):
    t = flat[:N_PROMPT_TOK].reshape(BATCH, SEQ, heads, dim).transpose(0, 2, 1, 3)
    return t[:, None]


def kernel(x_prompt, x_sample, cache_att_k, cache_att_v, cache_diff_k, cache_diff_v, cache_swa_k, cache_swa_v, cache_mla_ckv, cache_mla_kpe, c, c_ctx, ada_w, ada_b, norm_mix, norm_ffn, att_w_qkv, att_q_norm, att_k_norm, att_w_o, diff_w_qkv, diff_q_norm, diff_k_norm, diff_lq1, diff_lk1, diff_lq2, diff_lk2, diff_subln, diff_w_o, swa_w_qkv, swa_q_norm, swa_k_norm, swa_sink, swa_w_o, mla_w_in, mla_q_a_norm, mla_kv_a_norm, mla_w_uq, mla_w_ukv, mla_q_norm, mla_k_norm, mla_w_o, mlp_w1, mlp_w2):
    x = jnp.concatenate([x_prompt.reshape(N_PROMPT_TOK, D_MODEL),
                         x_sample.reshape(DEC_BATCH * DEC_SEQ, D_MODEL)], axis=0)
    cond = jnp.concatenate([c_ctx[None], c, jnp.zeros((COND_ROWS - 1 - DEC_BATCH, D_MODEL), F32)], axis=0)
    mods_all = _modulation(cond, ada_w, ada_b)

    tab_att = _rope_tables(ATT_HEAD_DIM)
    tab_64 = _rope_tables(DIFF_HEAD_DIM)

    outs = {}
    for layer in range(DEPTH):
        mods = mods_all[layer]
        gain_mix = _row(norm_mix[layer])
        gain_ffn = _row(norm_ffn[layer])
        if layer == 0:
            q, k, v = _proj_att(x, mods, gain_mix, att_w_qkv[0].astype(BF16),
                                _row(att_q_norm[0]), _row(att_k_norm[0]), tab_att)
            attn_p, attn_s = _att_attend(q, k, v, cache_att_k, cache_att_v)
            outs["att_k"] = _heads_out(k, ATT_KV_HEADS, ATT_HEAD_DIM)
            outs["att_v"] = _heads_out(v, ATT_KV_HEADS, ATT_HEAD_DIM)
            w_o = att_w_o[0]
        elif layer == 1:
            q, k, v = _proj_diff(x, mods, gain_mix, diff_w_qkv[0].astype(BF16),
                                 _pair(diff_q_norm[0]), _pair(diff_k_norm[0]), tab_64)
            lam_init = 0.8 - 0.6 * math.exp(-0.3 * layer)
            ck = cache_diff_k[:, 0].transpose(0, 1, 3, 2, 4).reshape(
                DEC_BATCH, DIFF_HEADS, PAST_LEN, LANES)
            attn_p, attn_s = _diff_attend(q, k, v, ck, cache_diff_v,
                                          _row(diff_lq1[0]), _row(diff_lk1[0]),
                                          _row(diff_lq2[0]), _row(diff_lk2[0]),
                                          _row(diff_subln[0]), lam_init)
            kk = k[:N_PROMPT_TOK].reshape(BATCH, SEQ, DIFF_HEADS, 2, DIFF_HEAD_DIM)
            outs["diff_k"] = kk.transpose(0, 2, 3, 1, 4)[:, None]
            outs["diff_v"] = _heads_out(v, DIFF_HEADS, 2 * DIFF_HEAD_DIM)
            w_o = diff_w_o[0]
        elif layer == 2:
            q, k, v, kd, vd = _proj_swa(x, mods, gain_mix, swa_w_qkv[0].astype(BF16),
                                        _pair(swa_q_norm[0]), _pair(swa_k_norm[0]), tab_64)
            ckd = jnp.concatenate([cache_swa_k[:, 0]] * 2, axis=-1).astype(BF16)
            cvd = jnp.concatenate([cache_swa_v[:, 0]] * 2, axis=-1).astype(BF16)
            attn_p, attn_s = _swa_attend(q, kd, vd, ckd, cvd, swa_sink[0].astype(F32))
            outs["swa_k"] = _heads_out(k, SWA_KV_HEADS, SWA_HEAD_DIM)
            outs["swa_v"] = _heads_out(v, SWA_KV_HEADS, SWA_HEAD_DIM)
            w_o = swa_w_o[0]
        else:
            w_in = mla_w_in[0]
            w_in = jnp.concatenate([w_in, w_in[:, -MLA_ROPE:]], axis=1).astype(BF16)
            w_uq = mla_w_uq[0].reshape(MLA_Q_RANK, MLA_HEADS, MLA_NOPE + MLA_ROPE)
            w_uq = jnp.concatenate([w_uq[:, :, :MLA_NOPE].reshape(MLA_Q_RANK, -1),
                                    w_uq[:, :, MLA_NOPE:].reshape(MLA_Q_RANK, -1)], axis=1).astype(BF16)
            w_ukv = mla_w_ukv[0].astype(BF16)
            qg, kg = mla_q_norm[0], mla_k_norm[0]
            qn, qp, ckv, kpe = _proj_mla(x, mods, gain_mix, w_in, _row(mla_q_a_norm[0]),
                                         _row(mla_kv_a_norm[0]), w_uq,
                                         _row(qg[:MLA_NOPE]), _pair(qg[MLA_NOPE:]), tab_64)
            kn, kp, vv = _mla_expand(ckv, kpe, w_ukv, _row(kg[:MLA_NOPE]), _pair(kg[MLA_NOPE:]),
                                     tab_64, N_PROMPT_TILES)
            c_ckv = cache_mla_ckv[:, 0].reshape(DEC_BATCH * PAST_LEN, MLA_KV_RANK)
            c_kpe = cache_mla_kpe[:, 0].reshape(DEC_BATCH * PAST_LEN, MLA_ROPE)
            c_kpe = jnp.concatenate([c_kpe, c_kpe], axis=-1)
            knc, kpc, vc = _mla_expand(c_ckv, c_kpe, w_ukv, _row(kg[:MLA_NOPE]), _pair(kg[MLA_NOPE:]),
                                       tab_64, None)
            attn_p, attn_s = _mla_attend(qn, qp, kn, kp, vv, knc, kpc, vc)
            outs["mla_ckv"] = ckv[:N_PROMPT_TOK].reshape(BATCH, 1, SEQ, MLA_KV_RANK)
            outs["mla_kpe"] = kpe[:N_PROMPT_TOK, :MLA_ROPE].reshape(BATCH, 1, SEQ, MLA_ROPE)
            w_o = mla_w_o[0]
        x1, h2 = _oproj(attn_p, attn_s, w_o.astype(BF16), x, mods, gain_ffn)
        x = _mlp(h2, mlp_w1[layer].astype(BF16), mlp_w2[layer].astype(BF16), x1, mods)

    y_prompt = x[:N_PROMPT_TOK].reshape(BATCH, SEQ, D_MODEL)
    y_sample = x[N_PROMPT_TOK:].reshape(DEC_BATCH, DEC_SEQ, D_MODEL)
    return (y_prompt, y_sample, outs["att_k"], outs["att_v"], outs["diff_k"], outs["diff_v"],
            outs["swa_k"], outs["swa_v"], outs["mla_ckv"], outs["mla_kpe"])
```

```python
import functools
import math

import numpy as np
import jax
import jax.numpy as jnp
from jax import lax
from jax.experimental import pallas as pl
from jax.experimental.pallas import tpu as pltpu

D_MODEL = 1024
BATCH = 16
SEQ = 256
DEPTH = 4
DEC_BATCH = 2
DEC_SEQ = 1024
PAST_LEN = 256
GRID_W = 64
ROPE_THETA = 10000.0
EPS = 1e-6
D_FF = 4 * D_MODEL
MOD_CHUNKS = 6
LOG2E = 1.4426950408889634

ATT_HEADS, ATT_KV_HEADS, ATT_HEAD_DIM = 8, 2, 128
DIFF_HEADS, DIFF_HEAD_DIM = 8, 64
SWA_HEADS, SWA_KV_HEADS, SWA_HEAD_DIM, WINDOW = 16, 4, 64, 128
MLA_HEADS, MLA_NOPE, MLA_ROPE, MLA_VDIM = 8, 128, 64, 128
MLA_Q_RANK, MLA_KV_RANK = 512, 256

LANES = 128
HALF = LANES // 2
TM = 256
N_PROMPT_TOK = BATCH * SEQ
N_LAT_TOK = DEC_BATCH * DEC_SEQ
N_TOK = N_PROMPT_TOK + N_LAT_TOK
N_TILES = N_TOK // TM
N_PROMPT_TILES = N_PROMPT_TOK // TM
TILES_PER_DEC = DEC_SEQ // TM
LAT_BLOCK0 = N_PROMPT_TOK // DEC_SEQ
COND_ROWS = 8
MLP_TM = 512
MLP_FF_CHUNK = 1024
SWA_QB = 128
VMEM_LIMIT = 56 * 1024 * 1024

F32 = jnp.float32
BF16 = jnp.bfloat16


def _cparams(n_axes):
    return pltpu.CompilerParams(dimension_semantics=("arbitrary",) * n_axes,
                                vmem_limit_bytes=VMEM_LIMIT)


def _dot(a, b):
    return jnp.dot(a, b, preferred_element_type=F32)


def _dot_nt(a, b):
    return lax.dot_general(a, b, (((1,), (1,)), ((), ())), preferred_element_type=F32)


def _const_spec(shape):
    nd = len(shape)
    return pl.BlockSpec(shape, lambda *_: (0,) * nd, pipeline_mode=pl.Buffered(1))


def _chunk(ref, c, width=LANES):
    return ref[:, c * width:(c + 1) * width]


def _tile_group(i):
    return jnp.where(i < N_PROMPT_TILES, 0, 1 + (i - N_PROMPT_TILES) // TILES_PER_DEC)


def _rope_tile(i):
    return jnp.maximum(i - N_PROMPT_TILES, 0) % TILES_PER_DEC


def _norm_mod(x, gain, shift, scale):
    ms = jnp.mean(x * x, axis=-1, keepdims=True)
    y = x * lax.rsqrt(ms + EPS) * gain
    return y * (1.0 + scale) + shift


def _lane_lo(shape):
    return lax.broadcasted_iota(jnp.int32, shape, len(shape) - 1) < HALF


def _rms_scale(y):
    return lax.rsqrt(jnp.mean(y * y, axis=-1, keepdims=True) + EPS)


def _rms_scale_halves(y):
    lo = _lane_lo(y.shape)
    sq = y * y
    s_lo = jnp.sum(jnp.where(lo, sq, 0.0), axis=-1, keepdims=True)
    s_hi = jnp.sum(jnp.where(lo, 0.0, sq), axis=-1, keepdims=True)
    return jnp.where(lo, lax.rsqrt(s_lo * (1.0 / HALF) + EPS), lax.rsqrt(s_hi * (1.0 / HALF) + EPS))


def _rope(y, cos, sin_prev, sin_next, quarter):
    return (y * cos + pltpu.roll(y, quarter, 1) * sin_prev
            + pltpu.roll(y, LANES - quarter, 1) * sin_next)


def _rope_tables(rot_dim):
    half = rot_dim // 2
    quarter = rot_dim // 4
    inv = np.float32(ROPE_THETA) ** (-np.arange(0, half, 2, dtype=np.float32) / np.float32(half))
    pos = np.arange(DEC_SEQ)
    row = (pos // GRID_W).astype(np.float32)
    col = (pos % GRID_W).astype(np.float32)
    lane = np.arange(LANES)
    dd = lane % rot_dim
    q = dd // quarter
    f = dd % quarter
    ang = np.where((q < 2)[None, :], row[:, None], col[:, None]) * inv[f][None, :]
    ang = ang.astype(np.float32)
    cos = np.cos(ang).astype(np.float32)
    sin = np.sin(ang).astype(np.float32)
    odd = (q % 2 == 1)[None, :]
    sin_prev = np.where(odd, sin, 0.0).astype(np.float32)
    sin_next = np.where(odd, 0.0, -sin).astype(np.float32)
    return jnp.asarray(cos), jnp.asarray(sin_prev), jnp.asarray(sin_next)


def _softmax2_parts(s_list, extra=None):
    m = jnp.max(s_list[0], axis=-1, keepdims=True)
    for s in s_list[1:]:
        m = jnp.maximum(m, jnp.max(s, axis=-1, keepdims=True))
    if extra is not None:
        m = jnp.maximum(m, extra)
    ps = [jnp.exp2(s - m) for s in s_list]
    tot = ps[0].sum(axis=-1, keepdims=True)
    for p in ps[1:]:
        tot = tot + p.sum(axis=-1, keepdims=True)
    if extra is not None:
        tot = tot + jnp.exp2(extra - m)
    return ps, 1.0 / tot


def _split_halves(q):
    lo = _lane_lo(q.shape)
    zero = jnp.zeros_like(q)
    return jnp.where(lo, q, zero), jnp.where(lo, zero, q)


def _mod_kernel(cond_ref, w_ref, b_ref, o_ref):
    c = cond_ref[...]
    s = (c * jax.nn.sigmoid(c)).astype(BF16)
    o_ref[0] = _dot(s, w_ref[0].astype(BF16)) + b_ref[0]


def _modulation(cond, ada_w, ada_b):
    tn = 1536
    n = MOD_CHUNKS * D_MODEL
    return pl.pallas_call(
        _mod_kernel,
        grid=(DEPTH, n // tn),
        in_specs=[
            pl.BlockSpec((COND_ROWS, D_MODEL), lambda l, j: (0, 0)),
            pl.BlockSpec((1, D_MODEL, tn), lambda l, j: (l, 0, j)),
            pl.BlockSpec((1, 1, tn), lambda l, j: (l, 0, j)),
        ],
        out_specs=pl.BlockSpec((1, COND_ROWS, tn), lambda l, j: (l, 0, j)),
        out_shape=jax.ShapeDtypeStruct((DEPTH, COND_ROWS, n), F32),
        compiler_params=_cparams(2),
        name="modulation",
    )(cond, ada_w, ada_b.reshape(DEPTH, 1, n))


def _mod_spec(chunk):
    return pl.BlockSpec((COND_ROWS, D_MODEL), lambda i: (0, chunk))


def _mod_row(ref, i):
    return ref[pl.ds(_tile_group(i), 1), :]


_ROPE_SPEC = pl.BlockSpec((TM, LANES), lambda i: (_rope_tile(i), 0))


def _tok_spec(width):
    return pl.BlockSpec((TM, width), lambda i: (i, 0))


def _cache_spec(*dims):
    nd = len(dims)
    return pl.BlockSpec((1, 1) + dims,
                        lambda i: (jnp.minimum(i, N_PROMPT_TILES - 1), 0) + (0,) * nd)


def _cache_shape(*dims):
    return jax.ShapeDtypeStruct((BATCH, 1) + dims, F32)


def _proj_att_kernel(x_ref, gain_ref, sh_ref, sc_ref, w_ref, qg_ref, kg_ref,
                     cos_ref, sp_ref, sn_ref, q_ref, k_ref, v_ref, ck_ref, cv_ref):
    i = pl.program_id(0)
    is_lat = i >= N_PROMPT_TILES
    h = _norm_mod(x_ref[...], gain_ref[...], _mod_row(sh_ref, i), _mod_row(sc_ref, i)).astype(BF16)
    y = _dot(h, w_ref[...])
    cos, sp, sn = cos_ref[...], sp_ref[...], sn_ref[...]
    nq, nk = ATT_HEADS, ATT_KV_HEADS
    cache = []
    for c in range(nq + 2 * nk):
        yc = y[:, c * LANES:(c + 1) * LANES]
        if c < nq + nk:
            yc = yc * _rms_scale(yc) * (qg_ref[...] if c < nq else kg_ref[...])
            yc = jnp.where(is_lat, _rope(yc, cos, sp, sn, ATT_HEAD_DIM // 4), yc)
        if c < nq:
            q_ref[:, c * LANES:(c + 1) * LANES] = yc.astype(BF16)
        elif c < nq + nk:
            k_ref[:, (c - nq) * LANES:(c - nq + 1) * LANES] = yc.astype(BF16)
            cache.append((ck_ref, c - nq, yc))
        else:
            v_ref[:, (c - nq - nk) * LANES:(c - nq - nk + 1) * LANES] = yc.astype(BF16)
            cache.append((cv_ref, c - nq - nk, yc))

    @pl.when(i < N_PROMPT_TILES)
    def _():
        for ref, hd, val in cache:
            ref[0, 0, hd] = val


def _proj_att(x, mods, gain, w, qg, kg, tables):
    nq, nk = ATT_HEADS * ATT_HEAD_DIM, ATT_KV_HEADS * ATT_HEAD_DIM
    return pl.pallas_call(
        _proj_att_kernel,
        grid=(N_TILES,),
        in_specs=[_tok_spec(D_MODEL), _const_spec((1, D_MODEL)), _mod_spec(0), _mod_spec(1),
                  _const_spec(w.shape), _const_spec((1, LANES)), _const_spec((1, LANES)),
                  _ROPE_SPEC, _ROPE_SPEC, _ROPE_SPEC],
        out_specs=[_tok_spec(nq), _tok_spec(nk), _tok_spec(nk),
                   _cache_spec(ATT_KV_HEADS, SEQ, ATT_HEAD_DIM), _cache_spec(ATT_KV_HEADS, SEQ, ATT_HEAD_DIM)],
        out_shape=[jax.ShapeDtypeStruct((N_TOK, nq), BF16),
                   jax.ShapeDtypeStruct((N_TOK, nk), BF16),
                   jax.ShapeDtypeStruct((N_TOK, nk), BF16),
                   _cache_shape(ATT_KV_HEADS, SEQ, ATT_HEAD_DIM), _cache_shape(ATT_KV_HEADS, SEQ, ATT_HEAD_DIM)],
        compiler_params=_cparams(1),
        name="proj_att",
    )(x, gain, mods, mods, w, qg, kg, *tables)


def _proj_diff_kernel(x_ref, gain_ref, sh_ref, sc_ref, w_ref, qg_ref, kg_ref,
                      cos_ref, sp_ref, sn_ref, q_ref, k_ref, v_ref, ck_ref, cv_ref):
    i = pl.program_id(0)
    is_lat = i >= N_PROMPT_TILES
    h = _norm_mod(x_ref[...], gain_ref[...], _mod_row(sh_ref, i), _mod_row(sc_ref, i)).astype(BF16)
    cos, sp, sn = cos_ref[...], sp_ref[...], sn_ref[...]
    nh = DIFF_HEADS
    cache_k, cache_v = [], []
    for part, (g_ref, o_ref) in enumerate(((qg_ref, q_ref), (kg_ref, k_ref), (None, v_ref))):
        y = _dot(h, w_ref[:, part * D_MODEL:(part + 1) * D_MODEL])
        for c in range(nh):
            yc = y[:, c * LANES:(c + 1) * LANES]
            if g_ref is not None:
                yc = yc * _rms_scale_halves(yc) * g_ref[...]
                yc = jnp.where(is_lat, _rope(yc, cos, sp, sn, DIFF_HEAD_DIM // 4), yc)
            o_ref[:, c * LANES:(c + 1) * LANES] = yc.astype(BF16)
            if part == 1:
                cache_k.append(yc)
            elif part == 2:
                cache_v.append(yc)

    @pl.when(i < N_PROMPT_TILES)
    def _():
        for hd in range(nh):
            ck_ref[0, 0, hd, 0] = cache_k[hd][:, :HALF]
            ck_ref[0, 0, hd, 1] = cache_k[hd][:, HALF:]
            cv_ref[0, 0, hd] = cache_v[hd]


def _proj_diff(x, mods, gain, w, qg, kg, tables):
    n = DIFF_HEADS * 2 * DIFF_HEAD_DIM
    return pl.pallas_call(
        _proj_diff_kernel,
        grid=(N_TILES,),
        in_specs=[_tok_spec(D_MODEL), _const_spec((1, D_MODEL)), _mod_spec(0), _mod_spec(1),
                  _const_spec(w.shape), _const_spec((1, LANES)), _const_spec((1, LANES)),
                  _ROPE_SPEC, _ROPE_SPEC, _ROPE_SPEC],
        out_specs=[_tok_spec(n), _tok_spec(n), _tok_spec(n),
                   _cache_spec(DIFF_HEADS, 2, SEQ, DIFF_HEAD_DIM), _cache_spec(DIFF_HEADS, SEQ, 2 * DIFF_HEAD_DIM)],
        out_shape=[jax.ShapeDtypeStruct((N_TOK, n), BF16)] * 3
                  + [_cache_shape(DIFF_HEADS, 2, SEQ, DIFF_HEAD_DIM),
                     _cache_shape(DIFF_HEADS, SEQ, 2 * DIFF_HEAD_DIM)],
        compiler_params=_cparams(1),
        name="proj_diff",
    )(x, gain, mods, mods, w, qg, kg, *tables)


def _dup_halves(yc):
    lo = _lane_lo(yc.shape)
    sw = pltpu.roll(yc, HALF, 1)
    return jnp.where(lo, yc, sw), jnp.where(lo, sw, yc)


def _proj_swa_kernel(x_ref, gain_ref, sh_ref, sc_ref, w_ref, qg_ref, kg_ref,
                     cos_ref, sp_ref, sn_ref, q_ref, kd_ref, vd_ref, ck_ref, cv_ref):
    i = pl.program_id(0)
    is_lat = i >= N_PROMPT_TILES
    h = _norm_mod(x_ref[...], gain_ref[...], _mod_row(sh_ref, i), _mod_row(sc_ref, i)).astype(BF16)
    y = _dot(h, w_ref[...])
    cos, sp, sn = cos_ref[...], sp_ref[...], sn_ref[...]
    nq = SWA_HEADS * SWA_HEAD_DIM // LANES
    nk = SWA_KV_HEADS * SWA_HEAD_DIM // LANES
    cache = []
    for c in range(nq + 2 * nk):
        yc = y[:, c * LANES:(c + 1) * LANES]
        if c < nq + nk:
            yc = yc * _rms_scale_halves(yc) * (qg_ref[...] if c < nq else kg_ref[...])
            yc = jnp.where(is_lat, _rope(yc, cos, sp, sn, SWA_HEAD_DIM // 4), yc)
        if c < nq:
            q_ref[:, c * LANES:(c + 1) * LANES] = yc.astype(BF16)
            continue
        j = c - nq if c < nq + nk else c - nq - nk
        c_ref, d_ref = (ck_ref, kd_ref) if c < nq + nk else (cv_ref, vd_ref)
        da, db = _dup_halves(yc)
        d_ref[:, (2 * j) * LANES:(2 * j + 1) * LANES] = da.astype(BF16)
        d_ref[:, (2 * j + 1) * LANES:(2 * j + 2) * LANES] = db.astype(BF16)
        cache.append((c_ref, 2 * j, da))
        cache.append((c_ref, 2 * j + 1, db))

    @pl.when(i < N_PROMPT_TILES)
    def _():
        for ref, hd, val in cache:
            ref[0, 0, hd] = val[:, :HALF]


def _proj_swa(x, mods, gain, w, qg, kg, tables):
    nq, nk = SWA_HEADS * SWA_HEAD_DIM, SWA_KV_HEADS * SWA_HEAD_DIM
    return pl.pallas_call(
        _proj_swa_kernel,
        grid=(N_TILES,),
        in_specs=[_tok_spec(D_MODEL), _const_spec((1, D_MODEL)), _mod_spec(0), _mod_spec(1),
                  _const_spec(w.shape), _const_spec((1, LANES)), _const_spec((1, LANES)),
                  _ROPE_SPEC, _ROPE_SPEC, _ROPE_SPEC],
        out_specs=[_tok_spec(nq), _tok_spec(2 * nk), _tok_spec(2 * nk),
                   _cache_spec(SWA_KV_HEADS, SEQ, SWA_HEAD_DIM), _cache_spec(SWA_KV_HEADS, SEQ, SWA_HEAD_DIM)],
        out_shape=[jax.ShapeDtypeStruct((N_TOK, nq), BF16),
                   jax.ShapeDtypeStruct((N_TOK, 2 * nk), BF16),
                   jax.ShapeDtypeStruct((N_TOK, 2 * nk), BF16),
                   _cache_shape(SWA_KV_HEADS, SEQ, SWA_HEAD_DIM), _cache_shape(SWA_KV_HEADS, SEQ, SWA_HEAD_DIM)],
        compiler_params=_cparams(1),
        name="proj_swa",
    )(x, gain, mods, mods, w, qg, kg, *tables)


def _proj_mla_kernel(x_ref, gain_ref, sh_ref, sc_ref, w_in_ref, qa_ref, kva_ref, w_uq_ref,
                     qg_ref, qgp_ref, cos_ref, sp_ref, sn_ref,
                     qn_ref, qp_ref, ckv_ref, kpe_ref, c_ckv_ref, c_kpe_ref):
    i = pl.program_id(0)
    is_lat = i >= N_PROMPT_TILES
    h = _norm_mod(x_ref[...], gain_ref[...], _mod_row(sh_ref, i), _mod_row(sc_ref, i)).astype(BF16)
    y = _dot(h, w_in_ref[...])
    c_q = y[:, :MLA_Q_RANK]
    c_kv = y[:, MLA_Q_RANK:MLA_Q_RANK + MLA_KV_RANK]
    kpe = y[:, MLA_Q_RANK + MLA_KV_RANK:]
    kpe_ref[...] = kpe
    ckv = c_kv * lax.rsqrt(jnp.mean(c_kv * c_kv, axis=-1, keepdims=True) + EPS) * kva_ref[...]
    ckv_ref[...] = ckv.astype(BF16)

    @pl.when(i < N_PROMPT_TILES)
    def _():
        c_ckv_ref[0, 0] = ckv
        c_kpe_ref[0, 0] = kpe[:, :MLA_ROPE]

    cq = (c_q * lax.rsqrt(jnp.mean(c_q * c_q, axis=-1, keepdims=True) + EPS) * qa_ref[...])
    q = _dot(cq.astype(BF16), w_uq_ref[...])
    cos, sp, sn = cos_ref[...], sp_ref[...], sn_ref[...]
    n_nope = MLA_HEADS * MLA_NOPE
    lo = _lane_lo((TM, LANES))
    inv_d = 1.0 / (MLA_NOPE + MLA_ROPE)
    for j in range(MLA_HEADS // 2):
        pe = q[:, n_nope + j * LANES:n_nope + (j + 1) * LANES]
        pe_sq = pe * pe
        rs = []
        for a in range(2):
            hh = 2 * j + a
            nope = q[:, hh * LANES:(hh + 1) * LANES]
            ss = (jnp.sum(nope * nope, axis=-1, keepdims=True)
                  + jnp.sum(jnp.where(lo, pe_sq, 0.0) if a == 0 else jnp.where(lo, 0.0, pe_sq),
                            axis=-1, keepdims=True))
            r = lax.rsqrt(ss * inv_d + EPS)
            rs.append(r)
            qn_ref[:, hh * LANES:(hh + 1) * LANES] = (nope * r * qg_ref[...]).astype(BF16)
        pe = pe * jnp.where(lo, rs[0], rs[1]) * qgp_ref[...]
        pe = jnp.where(is_lat, _rope(pe, cos, sp, sn, MLA_ROPE // 4), pe)
        qp_ref[:, j * LANES:(j + 1) * LANES] = pe.astype(BF16)


def _proj_mla(x, mods, gain, w_in, qa, kva, w_uq, qg, qgp, tables):
    n_nope = MLA_HEADS * MLA_NOPE
    n_pe = MLA_HEADS * MLA_ROPE
    return pl.pallas_call(
        _proj_mla_kernel,
        grid=(N_TILES,),
        in_specs=[_tok_spec(D_MODEL), _const_spec((1, D_MODEL)), _mod_spec(0), _mod_spec(1),
                  _const_spec(w_in.shape), _const_spec((1, MLA_Q_RANK)), _const_spec((1, MLA_KV_RANK)),
                  _const_spec(w_uq.shape), _const_spec((1, LANES)), _const_spec((1, LANES)),
                  _ROPE_SPEC, _ROPE_SPEC, _ROPE_SPEC],
        out_specs=[_tok_spec(n_nope), _tok_spec(n_pe), _tok_spec(MLA_KV_RANK), _tok_spec(LANES),
                   _cache_spec(SEQ, MLA_KV_RANK), _cache_spec(SEQ, MLA_ROPE)],
        out_shape=[jax.ShapeDtypeStruct((N_TOK, n_nope), BF16),
                   jax.ShapeDtypeStruct((N_TOK, n_pe), BF16),
                   jax.ShapeDtypeStruct((N_TOK, MLA_KV_RANK), BF16),
                   jax.ShapeDtypeStruct((N_TOK, LANES), F32),
                   _cache_shape(SEQ, MLA_KV_RANK), _cache_shape(SEQ, MLA_ROPE)],
        compiler_params=_cparams(1),
        name="proj_mla",
    )(x, gain, mods, mods, w_in, qa, kva, w_uq, qg, qgp, *tables)


def _mla_expand_kernel(ckv_ref, kpe_ref, w_ref, kg_ref, kgp_ref, cos_ref, sp_ref, sn_ref,
                       kn_ref, kp_ref, v_ref, *, rope_from_tile):
    i = pl.program_id(0)
    kv = _dot(ckv_ref[...].astype(BF16), w_ref[...])
    kpe = kpe_ref[...]
    lo = _lane_lo(kpe.shape)
    pe_ss = jnp.sum(jnp.where(lo, kpe * kpe, 0.0), axis=-1, keepdims=True)
    inv_d = 1.0 / (MLA_NOPE + MLA_ROPE)
    cos, sp, sn = cos_ref[...], sp_ref[...], sn_ref[...]
    for j in range(MLA_HEADS // 2):
        rs = []
        for a in range(2):
            hh = 2 * j + a
            kn = kv[:, hh * 2 * LANES:hh * 2 * LANES + LANES]
            v_ref[:, hh * LANES:(hh + 1) * LANES] = kv[:, hh * 2 * LANES + LANES:(hh + 1) * 2 * LANES].astype(BF16)
            r = lax.rsqrt((jnp.sum(kn * kn, axis=-1, keepdims=True) + pe_ss) * inv_d + EPS)
            rs.append(r)
            kn_ref[:, hh * LANES:(hh + 1) * LANES] = (kn * r * kg_ref[...]).astype(BF16)
        pe = kpe * jnp.where(lo, rs[0], rs[1]) * kgp_ref[...]
        if rope_from_tile is not None:
            pe = jnp.where(i >= rope_from_tile, _rope(pe, cos, sp, sn, MLA_ROPE // 4), pe)
        kp_ref[:, j * LANES:(j + 1) * LANES] = pe.astype(BF16)


def _mla_expand(ckv, kpe_dup, w_ukv, kg, kgp, tables, rope_from_tile):
    n = ckv.shape[0]
    n_nope = MLA_HEADS * MLA_NOPE
    n_pe = MLA_HEADS * MLA_ROPE
    return pl.pallas_call(
        functools.partial(_mla_expand_kernel, rope_from_tile=rope_from_tile),
        grid=(n // TM,),
        in_specs=[_tok_spec(MLA_KV_RANK), _tok_spec(LANES), _const_spec(w_ukv.shape),
                  _const_spec((1, LANES)), _const_spec((1, LANES)),
                  _ROPE_SPEC, _ROPE_SPEC, _ROPE_SPEC],
        out_specs=[_tok_spec(n_nope), _tok_spec(n_pe), _tok_spec(n_nope)],
        out_shape=[jax.ShapeDtypeStruct((n, n_nope), BF16),
                   jax.ShapeDtypeStruct((n, n_pe), BF16),
                   jax.ShapeDtypeStruct((n, n_nope), BF16)],
        compiler_params=_cparams(1),
        name="mla_expand",
    )(ckv, kpe_dup, w_ukv, kg, kgp, *tables)


def _prompt_spec(width):
    return pl.BlockSpec((TM, width), lambda b: (b, 0))


def _latq_spec(rows, width):
    per = DEC_SEQ // rows
    return pl.BlockSpec((rows, width), lambda b, t: (N_PROMPT_TOK // rows + b * per + t, 0))


def _latkv_spec(width):
    return pl.BlockSpec((DEC_SEQ, width), lambda b, t: (LAT_BLOCK0 + b, 0))


def _lato_spec(rows):
    per = DEC_SEQ // rows
    return pl.BlockSpec((rows, D_MODEL), lambda b, t: (b * per + t, 0))


def _att_kernel(*refs, with_ctx):
    if with_ctx:
        q_ref, k_ref, v_ref, kc_ref, vc_ref, o_ref = refs
    else:
        q_ref, k_ref, v_ref, o_ref = refs
    tq = q_ref.shape[0]
    ng = ATT_HEADS // ATT_KV_HEADS
    for kv in range(ATT_KV_HEADS):
        q = jnp.concatenate([_chunk(q_ref, kv * ng + g) for g in range(ng)], axis=0)
        s_list = [_dot_nt(q, _chunk(k_ref, kv))]
        if with_ctx:
            s_list.append(_dot_nt(q, kc_ref[kv].astype(BF16)))
        ps, inv = _softmax2_parts(s_list)
        o = _dot(ps[0].astype(BF16), _chunk(v_ref, kv))
        if with_ctx:
            o = o + _dot(ps[1].astype(BF16), vc_ref[kv].astype(BF16))
        o = o * inv
        for g in range(ng):
            o_ref[:, (kv * ng + g) * LANES:(kv * ng + g + 1) * LANES] = o[g * tq:(g + 1) * tq].astype(o_ref.dtype)


def _att_attend(q, k, v, cache_k, cache_v):
    nk = ATT_KV_HEADS * ATT_HEAD_DIM
    out_p = pl.pallas_call(
        functools.partial(_att_kernel, with_ctx=False),
        grid=(N_PROMPT_TILES,),
        in_specs=[_prompt_spec(D_MODEL), _prompt_spec(nk), _prompt_spec(nk)],
        out_specs=_prompt_spec(D_MODEL),
        out_shape=jax.ShapeDtypeStruct((N_PROMPT_TOK, D_MODEL), BF16),
        compiler_params=_cparams(1),
        name="att_prompt",
    )(q, k, v)
    ctx = pl.BlockSpec((None, None, ATT_KV_HEADS, PAST_LEN, LANES), lambda b, t: (b, 0, 0, 0, 0))
    out_s = pl.pallas_call(
        functools.partial(_att_kernel, with_ctx=True),
        grid=(DEC_BATCH, TILES_PER_DEC),
        in_specs=[_latq_spec(TM, D_MODEL), _latkv_spec(nk), _latkv_spec(nk), ctx, ctx],
        out_specs=_lato_spec(TM),
        out_shape=jax.ShapeDtypeStruct((N_LAT_TOK, D_MODEL), BF16),
        compiler_params=_cparams(2),
        name="att_latent",
    )(q, k, v, cache_k, cache_v)
    return out_p, out_s


def _diff_kernel(*refs, lam_init, with_ctx):
    if with_ctx:
        (q_ref, k_ref, v_ref, kc_ref, vc_ref, lq1_ref, lk1_ref, lq2_ref, lk2_ref, sub_ref, o_ref) = refs
    else:
        (q_ref, k_ref, v_ref, lq1_ref, lk1_ref, lq2_ref, lk2_ref, sub_ref, o_ref) = refs
    tq = q_ref.shape[0]
    lam = (jnp.exp(jnp.sum(lq1_ref[...] * lk1_ref[...], axis=-1, keepdims=True))
           - jnp.exp(jnp.sum(lq2_ref[...] * lk2_ref[...], axis=-1, keepdims=True)) + lam_init)
    sub = sub_ref[...] * (1.0 - lam_init)
    for hd in range(DIFF_HEADS):
        q = jnp.concatenate(_split_halves(_chunk(q_ref, hd)), axis=0)
        s_list = [_dot_nt(q, _chunk(k_ref, hd))]
        if with_ctx:
            s_list.append(_dot_nt(q, kc_ref[hd].astype(BF16)))
        ps, inv = _softmax2_parts(s_list)
        c0 = inv[:tq]
        c1 = -lam * inv[tq:]
        a = [p[:tq] * c0 + p[tq:] * c1 for p in ps]
        o = _dot(a[0].astype(BF16), _chunk(v_ref, hd))
        if with_ctx:
            o = o + _dot(a[1].astype(BF16), vc_ref[hd].astype(BF16))
        o = o * lax.rsqrt(jnp.mean(o * o, axis=-1, keepdims=True) + EPS) * sub
        o_ref[:, hd * LANES:(hd + 1) * LANES] = o.astype(o_ref.dtype)


def _diff_attend(q, k, v, cache_k_pair, cache_v, lq1, lk1, lq2, lk2, subln, lam_init):
    small = [lq1, lk1, lq2, lk2, subln]
    small_specs = [_const_spec(s.shape) for s in small]
    out_p = pl.pallas_call(
        functools.partial(_diff_kernel, lam_init=lam_init, with_ctx=False),
        grid=(N_PROMPT_TILES,),
        in_specs=[_prompt_spec(D_MODEL)] * 3 + small_specs,
        out_specs=_prompt_spec(D_MODEL),
        out_shape=jax.ShapeDtypeStruct((N_PROMPT_TOK, D_MODEL), BF16),
        compiler_params=_cparams(1),
        name="diff_prompt",
    )(q, k, v, *small)
    out_s = pl.pallas_call(
        functools.partial(_diff_kernel, lam_init=lam_init, with_ctx=True),
        grid=(DEC_BATCH, TILES_PER_DEC),
        in_specs=[_latq_spec(TM, D_MODEL), _latkv_spec(D_MODEL), _latkv_spec(D_MODEL),
                  pl.BlockSpec((None, DIFF_HEADS, PAST_LEN, LANES), lambda b, t: (b, 0, 0, 0)),
                  pl.BlockSpec((None, None, DIFF_HEADS, PAST_LEN, LANES), lambda b, t: (b, 0, 0, 0, 0))]
                 + small_specs,
        out_specs=_lato_spec(TM),
        out_shape=jax.ShapeDtypeStruct((N_LAT_TOK, D_MODEL), BF16),
        compiler_params=_cparams(2),
        name="diff_latent",
    )(q, k, v, cache_k_pair, cache_v, *small)
    return out_p, out_s


def _swa_group(q_ref, sink_ref, kv, score_fns, values, o_ref):
    tq = q_ref.shape[0]
    ng = SWA_HEADS // SWA_KV_HEADS
    rows = []
    for c in range(ng // 2):
        rows.extend(_split_halves(_chunk(q_ref, kv * (ng // 2) + c)))
    q = jnp.concatenate(rows, axis=0)
    sink = jnp.concatenate([jnp.full((tq, 1), sink_ref[kv * ng + g] * LOG2E, F32) for g in range(ng)],
                           axis=0)
    ps, inv = _softmax2_parts([fn(q) for fn in score_fns], extra=sink)
    o = None
    for p, vv in zip(ps, values):
        t = _dot(p.astype(BF16), vv)
        o = t if o is None else o + t
    o = o * inv
    lo = _lane_lo((tq, LANES))
    for c in range(ng // 2):
        oc = jnp.where(lo, o[(2 * c) * tq:(2 * c + 1) * tq], o[(2 * c + 1) * tq:(2 * c + 2) * tq])
        cc = kv * (ng // 2) + c
        o_ref[:, cc * LANES:(cc + 1) * LANES] = oc.astype(o_ref.dtype)


def _swa_prompt_kernel(sink_ref, q_ref, k_ref, v_ref, o_ref):
    for kv in range(SWA_KV_HEADS):
        k = _chunk(k_ref, kv)
        _swa_group(q_ref, sink_ref, kv, [lambda q, k=k: _dot_nt(q, k)], [_chunk(v_ref, kv)], o_ref)


def _swa_latent_kernel(sink_ref, q_ref, k_ref, v_ref, kc_ref, vc_ref, o_ref):
    n = pl.program_id(1)
    tq = q_ref.shape[0]
    ng = SWA_HEADS // SWA_KV_HEADS
    span = 3 * SWA_QB
    start = pl.multiple_of(jnp.clip((n - 1) * SWA_QB, 0, DEC_SEQ - span), SWA_QB)
    rows = lax.broadcasted_iota(jnp.int32, (ng * tq, span), 0)
    qpos = n * SWA_QB + jnp.bitwise_and(rows, tq - 1)
    kpos = start + lax.broadcasted_iota(jnp.int32, (ng * tq, span), 1)
    valid = jnp.abs(qpos - kpos) <= WINDOW
    for kv in range(SWA_KV_HEADS):
        k = k_ref[pl.ds(start, span), kv * LANES:(kv + 1) * LANES]
        v = v_ref[pl.ds(start, span), kv * LANES:(kv + 1) * LANES]
        kc = kc_ref[kv]
        _swa_group(q_ref, sink_ref, kv,
                   [lambda q, k=k: jnp.where(valid, _dot_nt(q, k), -1e30),
                    lambda q, kc=kc: _dot_nt(q, kc)],
                   [v, vc_ref[kv]], o_ref)


def _swa_attend(q, kd, vd, cache_kd, cache_vd, sink):
    nkd = 2 * SWA_KV_HEADS * SWA_HEAD_DIM
    smem = pl.BlockSpec(memory_space=pltpu.SMEM)
    out_p = pl.pallas_call(
        _swa_prompt_kernel,
        grid=(N_PROMPT_TILES,),
        in_specs=[smem, _prompt_spec(D_MODEL), _prompt_spec(nkd), _prompt_spec(nkd)],
        out_specs=_prompt_spec(D_MODEL),
        out_shape=jax.ShapeDtypeStruct((N_PROMPT_TOK, D_MODEL), BF16),
        compiler_params=_cparams(1),
        name="swa_prompt",
    )(sink, q, kd, vd)
    ctx = pl.BlockSpec((None, SWA_KV_HEADS, PAST_LEN, LANES), lambda b, n: (b, 0, 0, 0))
    out_s = pl.pallas_call(
        _swa_latent_kernel,
        grid=(DEC_BATCH, DEC_SEQ // SWA_QB),
        in_specs=[smem, _latq_spec(SWA_QB, D_MODEL), _latkv_spec(nkd), _latkv_spec(nkd), ctx, ctx],
        out_specs=_lato_spec(SWA_QB),
        out_shape=jax.ShapeDtypeStruct((N_LAT_TOK, D_MODEL), BF16),
        compiler_params=_cparams(2),
        name="swa_latent",
    )(sink, q, kd, vd, cache_kd, cache_vd)
    return out_p, out_s


def _mla_kernel(*refs, with_ctx):
    if with_ctx:
        (qn_ref, qp_ref, kn_ref, kp_ref, v_ref, knc_ref, kpc_ref, vc_ref, o_ref) = refs
    else:
        (qn_ref, qp_ref, kn_ref, kp_ref, v_ref, o_ref) = refs
    for hd in range(MLA_HEADS):
        j, a = hd // 2, hd % 2
        q = jnp.concatenate([_chunk(qn_ref, hd), _split_halves(_chunk(qp_ref, j))[a]], axis=1)
        s_list = [_dot_nt(q, jnp.concatenate([_chunk(kn_ref, hd), _chunk(kp_ref, j)], axis=1))]
        if with_ctx:
            s_list.append(_dot_nt(q, jnp.concatenate([_chunk(knc_ref, hd), _chunk(kpc_ref, j)], axis=1)))
        ps, inv = _softmax2_parts(s_list)
        o = _dot(ps[0].astype(BF16), _chunk(v_ref, hd))
        if with_ctx:
            o = o + _dot(ps[1].astype(BF16), _chunk(vc_ref, hd))
        o_ref[:, hd * LANES:(hd + 1) * LANES] = (o * inv).astype(o_ref.dtype)


def _mla_attend(qn, qp, kn, kp, v, knc, kpc, vc):
    n_pe = MLA_HEADS * MLA_ROPE
    out_p = pl.pallas_call(
        functools.partial(_mla_kernel, with_ctx=False),
        grid=(N_PROMPT_TILES,),
        in_specs=[_prompt_spec(D_MODEL), _prompt_spec(n_pe), _prompt_spec(D_MODEL), _prompt_spec(n_pe),
                  _prompt_spec(D_MODEL)],
        out_specs=_prompt_spec(D_MODEL),
        out_shape=jax.ShapeDtypeStruct((N_PROMPT_TOK, D_MODEL), BF16),
        compiler_params=_cparams(1),
        name="mla_prompt",
    )(qn, qp, kn, kp, v)

    def ctx(width):
        return pl.BlockSpec((PAST_LEN, width), lambda b, t: (b, 0))

    out_s = pl.pallas_call(
        functools.partial(_mla_kernel, with_ctx=True),
        grid=(DEC_BATCH, TILES_PER_DEC),
        in_specs=[_latq_spec(TM, D_MODEL), _latq_spec(TM, n_pe),
                  _latkv_spec(D_MODEL), _latkv_spec(n_pe), _latkv_spec(D_MODEL),
                  ctx(D_MODEL), ctx(n_pe), ctx(D_MODEL)],
        out_specs=_lato_spec(TM),
        out_shape=jax.ShapeDtypeStruct((N_LAT_TOK, D_MODEL), BF16),
        compiler_params=_cparams(2),
        name="mla_latent",
    )(qn, qp, kn, kp, v, knc, kpc, vc)
    return out_p, out_s


def _oproj_kernel(ap_ref, as_ref, w_ref, x_ref, g1_ref, gain_ref, sh_ref, sc_ref, x1_ref, h2_ref):
    i = pl.program_id(0)
    a = jnp.where(i < N_PROMPT_TILES, ap_ref[...], as_ref[...])
    o = _dot(a, w_ref[...])
    x1 = x_ref[...] + _mod_row(g1_ref, i) * o
    x1_ref[...] = x1
    h2_ref[...] = _norm_mod(x1, gain_ref[...], _mod_row(sh_ref, i), _mod_row(sc_ref, i)).astype(BF16)


def _oproj(attn_p, attn_s, w_o, x, mods, gain_ffn):
    return pl.pallas_call(
        _oproj_kernel,
        grid=(N_TILES,),
        in_specs=[pl.BlockSpec((TM, D_MODEL), lambda i: (jnp.minimum(i, N_PROMPT_TILES - 1), 0)),
                  pl.BlockSpec((TM, D_MODEL), lambda i: (jnp.maximum(i - N_PROMPT_TILES, 0), 0)),
                  _const_spec(w_o.shape), _tok_spec(D_MODEL), _mod_spec(2),
                  _const_spec((1, D_MODEL)), _mod_spec(3), _mod_spec(4)],
        out_specs=[_tok_spec(D_MODEL), _tok_spec(D_MODEL)],
        out_shape=[jax.ShapeDtypeStruct((N_TOK, D_MODEL), F32),
                   jax.ShapeDtypeStruct((N_TOK, D_MODEL), BF16)],
        compiler_params=_cparams(1),
        name="oproj",
    )(attn_p, attn_s, w_o, x, mods, gain_ffn, mods, mods)


def _mlp_kernel(h_ref, w1_ref, w2_ref, x_ref, g2_ref, o_ref):
    i = pl.program_id(0)
    grp = _tile_group(i * (MLP_TM // TM))
    h = h_ref[...]
    acc = None
    for c in range(D_FF // MLP_FF_CHUNK):
        u = _dot(h, w1_ref[:, c * MLP_FF_CHUNK:(c + 1) * MLP_FF_CHUNK])
        u = jnp.square(jnp.maximum(u, 0.0)).astype(BF16)
        t = _dot(u, w2_ref[c * MLP_FF_CHUNK:(c + 1) * MLP_FF_CHUNK, :])
        acc = t if acc is None else acc + t
    o_ref[...] = x_ref[...] + g2_ref[pl.ds(grp, 1), :] * acc


def _mlp(h2, w1, w2, x1, mods):
    spec = pl.BlockSpec((MLP_TM, D_MODEL), lambda i: (i, 0))
    return pl.pallas_call(
        _mlp_kernel,
        grid=(N_TOK // MLP_TM,),
        in_specs=[spec, _const_spec(w1.shape), _const_spec(w2.shape), spec, _mod_spec(5)],
        out_specs=spec,
        out_shape=jax.ShapeDtypeStruct((N_TOK, D_MODEL), F32),
        compiler_params=_cparams(1),
        name="mlp",
    )(h2, w1, w2, x1, mods)


def _row(v, scale=1.0):
    return (v.astype(F32) * scale).reshape(1, -1)


def _pair(v, scale=1.0):
    return (jnp.concatenate([v, v]).astype(F32) * scale).reshape(1, LANES)


def kernel(x_prompt, x_sample, cache_att_k, cache_att_v, cache_diff_k, cache_diff_v, cache_swa_k, cache_swa_v, cache_mla_ckv, cache_mla_kpe, c, c_ctx, ada_w, ada_b, norm_mix, norm_ffn, att_w_qkv, att_q_norm, att_k_norm, att_w_o, diff_w_qkv, diff_q_norm, diff_k_norm, diff_lq1, diff_lk1, diff_lq2, diff_lk2, diff_subln, diff_w_o, swa_w_qkv, swa_q_norm, swa_k_norm, swa_sink, swa_w_o, mla_w_in, mla_q_a_norm, mla_kv_a_norm, mla_w_uq, mla_w_ukv, mla_q_norm, mla_k_norm, mla_w_o, mlp_w1, mlp_w2):
    x = jnp.concatenate([x_prompt.reshape(N_PROMPT_TOK, D_MODEL),
                         x_sample.reshape(N_LAT_TOK, D_MODEL)], axis=0)
    cond = jnp.concatenate([c_ctx[None], c, jnp.zeros((COND_ROWS - 1 - DEC_BATCH, D_MODEL), F32)], axis=0)
    mods_all = _modulation(cond, ada_w, ada_b)

    tab_att = _rope_tables(ATT_HEAD_DIM)
    tab_64 = _rope_tables(DIFF_HEAD_DIM)

    outs = {}
    for layer in range(DEPTH):
        mods = mods_all[layer]
        gain_mix = _row(norm_mix[layer])
        gain_ffn = _row(norm_ffn[layer])
        if layer == 0:
            qs = ATT_HEAD_DIM ** -0.5 * LOG2E
            q, k, v, outs["att_k"], outs["att_v"] = _proj_att(
                x, mods, gain_mix, att_w_qkv[0].astype(BF16),
                _row(att_q_norm[0], qs), _row(att_k_norm[0]), tab_att)
            attn_p, attn_s = _att_attend(q, k, v, cache_att_k, cache_att_v)
            w_o = att_w_o[0]
        elif layer == 1:
            qs = DIFF_HEAD_DIM ** -0.5 * LOG2E
            q, k, v, outs["diff_k"], outs["diff_v"] = _proj_diff(
                x, mods, gain_mix, diff_w_qkv[0].astype(BF16),
                _pair(diff_q_norm[0], qs), _pair(diff_k_norm[0]), tab_64)
            lam_init = 0.8 - 0.6 * math.exp(-0.3 * layer)
            ck = cache_diff_k[:, 0].transpose(0, 1, 3, 2, 4).reshape(
                DEC_BATCH, DIFF_HEADS, PAST_LEN, LANES)
            attn_p, attn_s = _diff_attend(q, k, v, ck, cache_diff_v,
                                          _row(diff_lq1[0]), _row(diff_lk1[0]),
                                          _row(diff_lq2[0]), _row(diff_lk2[0]),
                                          _row(diff_subln[0]), lam_init)
            w_o = diff_w_o[0]
        elif layer == 2:
            qs = SWA_HEAD_DIM ** -0.5 * LOG2E
            q, kd, vd, outs["swa_k"], outs["swa_v"] = _proj_swa(
                x, mods, gain_mix, swa_w_qkv[0].astype(BF16),
                _pair(swa_q_norm[0], qs), _pair(swa_k_norm[0]), tab_64)
            ckd = jnp.concatenate([cache_swa_k[:, 0]] * 2, axis=-1).astype(BF16)
            cvd = jnp.concatenate([cache_swa_v[:, 0]] * 2, axis=-1).astype(BF16)
            attn_p, attn_s = _swa_attend(q, kd, vd, ckd, cvd, swa_sink[0].astype(F32))
            w_o = swa_w_o[0]
        else:
            qs = (MLA_NOPE + MLA_ROPE) ** -0.5 * LOG2E
            w_in = mla_w_in[0]
            w_in = jnp.concatenate([w_in, w_in[:, -MLA_ROPE:]], axis=1).astype(BF16)
            w_uq = mla_w_uq[0].reshape(MLA_Q_RANK, MLA_HEADS, MLA_NOPE + MLA_ROPE)
            w_uq = jnp.concatenate([w_uq[:, :, :MLA_NOPE].reshape(MLA_Q_RANK, -1),
                                    w_uq[:, :, MLA_NOPE:].reshape(MLA_Q_RANK, -1)], axis=1).astype(BF16)
            w_ukv = mla_w_ukv[0].astype(BF16)
            qg, kg = mla_q_norm[0], mla_k_norm[0]
            qn, qp, ckv, kpe, outs["mla_ckv"], outs["mla_kpe"] = _proj_mla(
                x, mods, gain_mix, w_in, _row(mla_q_a_norm[0]), _row(mla_kv_a_norm[0]), w_uq,
                _row(qg[:MLA_NOPE], qs), _pair(qg[MLA_NOPE:], qs), tab_64)
            kn, kp, vv = _mla_expand(ckv, kpe, w_ukv, _row(kg[:MLA_NOPE]), _pair(kg[MLA_NOPE:]),
                                     tab_64, N_PROMPT_TILES)
            c_ckv = cache_mla_ckv[:, 0].reshape(DEC_BATCH * PAST_LEN, MLA_KV_RANK)
            c_kpe = cache_mla_kpe[:, 0].reshape(DEC_BATCH * PAST_LEN, MLA_ROPE)
            c_kpe = jnp.concatenate([c_kpe, c_kpe], axis=-1)
            knc, kpc, vc = _mla_expand(c_ckv, c_kpe, w_ukv, _row(kg[:MLA_NOPE]), _pair(kg[MLA_NOPE:]),
                                       tab_64, None)
            attn_p, attn_s = _mla_attend(qn, qp, kn, kp, vv, knc, kpc, vc)
            w_o = mla_w_o[0]
        x1, h2 = _oproj(attn_p, attn_s, w_o.astype(BF16), x, mods, gain_ffn)
        x = _mlp(h2, mlp_w1[layer].astype(BF16), mlp_w2[layer].astype(BF16), x1, mods)

    y_prompt = x[:N_PROMPT_TOK].reshape(BATCH, SEQ, D_MODEL)
    y_sample = x[N_PROMPT_TOK:].reshape(DEC_BATCH, DEC_SEQ, D_MODEL)
    return (y_prompt, y_sample, outs["att_k"], outs["att_v"], outs["diff_k"], outs["diff_v"],
            outs["swa_k"], outs["swa_v"], outs["mla_ckv"], outs["mla_kpe"])
```

```python
import functools
import math

import numpy as np
import jax
import jax.numpy as jnp
from jax import lax
from jax.experimental import pallas as pl
from jax.experimental.pallas import tpu as pltpu

D_MODEL = 1024
BATCH = 16
SEQ = 256
DEPTH = 4
DEC_BATCH = 2
DEC_SEQ = 1024
PAST_LEN = 256
GRID_W = 64
ROPE_THETA = 10000.0
EPS = 1e-6
D_FF = 4 * D_MODEL
MOD_CHUNKS = 6
LOG2E = 1.4426950408889634

ATT_HEADS, ATT_KV_HEADS, ATT_HEAD_DIM = 8, 2, 128
DIFF_HEADS, DIFF_HEAD_DIM = 8, 64
SWA_HEADS, SWA_KV_HEADS, SWA_HEAD_DIM, WINDOW = 16, 4, 64, 128
MLA_HEADS, MLA_NOPE, MLA_ROPE, MLA_VDIM = 8, 128, 64, 128
MLA_Q_RANK, MLA_KV_RANK = 512, 256

LANES = 128
HALF = LANES // 2
TM = 256
N_PROMPT_TOK = BATCH * SEQ
N_LAT_TOK = DEC_BATCH * DEC_SEQ
N_TOK = N_PROMPT_TOK + N_LAT_TOK
N_TILES = N_TOK // TM
N_PROMPT_TILES = N_PROMPT_TOK // TM
TILES_PER_DEC = DEC_SEQ // TM
LAT_BLOCK0 = N_PROMPT_TOK // DEC_SEQ
COND_ROWS = 8
MLP_TM = 512
MLP_FF_CHUNK = 1024
SWA_QB = 128
VMEM_LIMIT = 56 * 1024 * 1024

F32 = jnp.float32
BF16 = jnp.bfloat16


def _cparams(n_axes):
    return pltpu.CompilerParams(dimension_semantics=("arbitrary",) * n_axes,
                                vmem_limit_bytes=VMEM_LIMIT)


def _dot(a, b):
    return jnp.dot(a, b, preferred_element_type=F32)


def _dot_nt(a, b):
    return lax.dot_general(a, b, (((1,), (1,)), ((), ())), preferred_element_type=F32)


def _const_spec(shape):
    nd = len(shape)
    return pl.BlockSpec(shape, lambda *_: (0,) * nd, pipeline_mode=pl.Buffered(1))


def _chunk(ref, c, width=LANES):
    return ref[:, c * width:(c + 1) * width]


def _tile_group(i):
    return jnp.where(i < N_PROMPT_TILES, 0, 1 + (i - N_PROMPT_TILES) // TILES_PER_DEC)


def _rope_tile(i):
    return jnp.maximum(i - N_PROMPT_TILES, 0) % TILES_PER_DEC


def _norm_mod(x, gain, shift, scale):
    ms = jnp.mean(x * x, axis=-1, keepdims=True)
    y = x * lax.rsqrt(ms + EPS) * gain
    return y * (1.0 + scale) + shift


def _lane_lo(shape):
    return lax.broadcasted_iota(jnp.int32, shape, len(shape) - 1) < HALF


def _rms_scale(y):
    return lax.rsqrt(jnp.mean(y * y, axis=-1, keepdims=True) + EPS)


def _rms_scale_halves(y):
    lo = _lane_lo(y.shape)
    sq = y * y
    s_lo = jnp.sum(jnp.where(lo, sq, 0.0), axis=-1, keepdims=True)
    s_hi = jnp.sum(jnp.where(lo, 0.0, sq), axis=-1, keepdims=True)
    return jnp.where(lo, lax.rsqrt(s_lo * (1.0 / HALF) + EPS), lax.rsqrt(s_hi * (1.0 / HALF) + EPS))


def _rope(y, cos, sin_prev, sin_next, quarter):
    return (y * cos + pltpu.roll(y, quarter, 1) * sin_prev
            + pltpu.roll(y, LANES - quarter, 1) * sin_next)


def _rope_tables(rot_dim):
    half = rot_dim // 2
    quarter = rot_dim // 4
    inv = np.float32(ROPE_THETA) ** (-np.arange(0, half, 2, dtype=np.float32) / np.float32(half))
    pos = np.arange(DEC_SEQ)
    row = (pos // GRID_W).astype(np.float32)
    col = (pos % GRID_W).astype(np.float32)
    lane = np.arange(LANES)
    dd = lane % rot_dim
    q = dd // quarter
    f = dd % quarter
    ang = np.where((q < 2)[None, :], row[:, None], col[:, None]) * inv[f][None, :]
    ang = ang.astype(np.float32)
    cos = np.cos(ang).astype(np.float32)
    sin = np.sin(ang).astype(np.float32)
    odd = (q % 2 == 1)[None, :]
    sin_prev = np.where(odd, sin, 0.0).astype(np.float32)
    sin_next = np.where(odd, 0.0, -sin).astype(np.float32)
    return jnp.asarray(cos), jnp.asarray(sin_prev), jnp.asarray(sin_next)


def _softmax2_parts(s_list, extra=None):
    m = jnp.max(s_list[0], axis=0, keepdims=True)
    for s in s_list[1:]:
        m = jnp.maximum(m, jnp.max(s, axis=0, keepdims=True))
    if extra is not None:
        m = jnp.maximum(m, extra)
    ps = [jnp.exp2(s - m) for s in s_list]
    tot = ps[0].sum(axis=0, keepdims=True)
    for p in ps[1:]:
        tot = tot + p.sum(axis=0, keepdims=True)
    if extra is not None:
        tot = tot + jnp.exp2(extra - m)
    return ps, 1.0 / tot


def _dot_tn(a, b):
    return lax.dot_general(a, b, (((0,), (0,)), ((), ())), preferred_element_type=F32)


def _pv(ps, values):
    o = None
    for p, v in zip(ps, values):
        t = _dot_tn(v, p.astype(BF16))
        o = t if o is None else o + t
    return o


def _split_halves(q):
    lo = _lane_lo(q.shape)
    zero = jnp.zeros_like(q)
    return jnp.where(lo, q, zero), jnp.where(lo, zero, q)


def _mod_kernel(cond_ref, w_ref, b_ref, o_ref):
    c = cond_ref[...]
    s = (c * jax.nn.sigmoid(c)).astype(BF16)
    o_ref[0] = _dot(s, w_ref[0].astype(BF16)) + b_ref[0]


def _modulation(cond, ada_w, ada_b):
    tn = 1536
    n = MOD_CHUNKS * D_MODEL
    return pl.pallas_call(
        _mod_kernel,
        grid=(DEPTH, n // tn),
        in_specs=[
            pl.BlockSpec((COND_ROWS, D_MODEL), lambda l, j: (0, 0)),
            pl.BlockSpec((1, D_MODEL, tn), lambda l, j: (l, 0, j)),
            pl.BlockSpec((1, 1, tn), lambda l, j: (l, 0, j)),
        ],
        out_specs=pl.BlockSpec((1, COND_ROWS, tn), lambda l, j: (l, 0, j)),
        out_shape=jax.ShapeDtypeStruct((DEPTH, COND_ROWS, n), F32),
        compiler_params=_cparams(2),
        name="modulation",
    )(cond, ada_w, ada_b.reshape(DEPTH, 1, n))


def _mod_spec(chunk):
    return pl.BlockSpec((COND_ROWS, D_MODEL), lambda i: (0, chunk))


def _mod_row(ref, i):
    return ref[pl.ds(_tile_group(i), 1), :]


_ROPE_SPEC = pl.BlockSpec((TM, LANES), lambda i: (_rope_tile(i), 0))


def _tok_spec(width):
    return pl.BlockSpec((TM, width), lambda i: (i, 0))


def _cache_spec(*dims):
    nd = len(dims)
    return pl.BlockSpec((1, 1) + dims,
                        lambda i: (jnp.minimum(i, N_PROMPT_TILES - 1), 0) + (0,) * nd)


def _cache_shape(*dims):
    return jax.ShapeDtypeStruct((BATCH, 1) + dims, F32)


def _proj_att_kernel(x_ref, gain_ref, sh_ref, sc_ref, w_ref, qg_ref, kg_ref,
                     cos_ref, sp_ref, sn_ref, q_ref, k_ref, v_ref, ck_ref, cv_ref):
    i = pl.program_id(0)
    is_lat = i >= N_PROMPT_TILES
    h = _norm_mod(x_ref[...], gain_ref[...], _mod_row(sh_ref, i), _mod_row(sc_ref, i)).astype(BF16)
    y = _dot(h, w_ref[...])
    cos, sp, sn = cos_ref[...], sp_ref[...], sn_ref[...]
    nq, nk = ATT_HEADS, ATT_KV_HEADS
    cache = []
    for c in range(nq + 2 * nk):
        yc = y[:, c * LANES:(c + 1) * LANES]
        if c < nq + nk:
            yc = yc * _rms_scale(yc) * (qg_ref[...] if c < nq else kg_ref[...])
            yc = jnp.where(is_lat, _rope(yc, cos, sp, sn, ATT_HEAD_DIM // 4), yc)
        if c < nq:
            q_ref[:, c * LANES:(c + 1) * LANES] = yc.astype(BF16)
        elif c < nq + nk:
            k_ref[:, (c - nq) * LANES:(c - nq + 1) * LANES] = yc.astype(BF16)
            cache.append((ck_ref, c - nq, yc))
        else:
            v_ref[:, (c - nq - nk) * LANES:(c - nq - nk + 1) * LANES] = yc.astype(BF16)
            cache.append((cv_ref, c - nq - nk, yc))

    @pl.when(i < N_PROMPT_TILES)
    def _():
        for ref, hd, val in cache:
            ref[0, 0, hd] = val


def _proj_att(x, mods, gain, w, qg, kg, tables):
    nq, nk = ATT_HEADS * ATT_HEAD_DIM, ATT_KV_HEADS * ATT_HEAD_DIM
    return pl.pallas_call(
        _proj_att_kernel,
        grid=(N_TILES,),
        in_specs=[_tok_spec(D_MODEL), _const_spec((1, D_MODEL)), _mod_spec(0), _mod_spec(1),
                  _const_spec(w.shape), _const_spec((1, LANES)), _const_spec((1, LANES)),
                  _ROPE_SPEC, _ROPE_SPEC, _ROPE_SPEC],
        out_specs=[_tok_spec(nq), _tok_spec(nk), _tok_spec(nk),
                   _cache_spec(ATT_KV_HEADS, SEQ, ATT_HEAD_DIM), _cache_spec(ATT_KV_HEADS, SEQ, ATT_HEAD_DIM)],
        out_shape=[jax.ShapeDtypeStruct((N_TOK, nq), BF16),
                   jax.ShapeDtypeStruct((N_TOK, nk), BF16),
                   jax.ShapeDtypeStruct((N_TOK, nk), BF16),
                   _cache_shape(ATT_KV_HEADS, SEQ, ATT_HEAD_DIM), _cache_shape(ATT_KV_HEADS, SEQ, ATT_HEAD_DIM)],
        compiler_params=_cparams(1),
        name="proj_att",
    )(x, gain, mods, mods, w, qg, kg, *tables)


def _proj_diff_kernel(x_ref, gain_ref, sh_ref, sc_ref, w_ref, qg_ref, kg_ref,
                      cos_ref, sp_ref, sn_ref, q_ref, k_ref, v_ref, ck_ref, cv_ref):
    i = pl.program_id(0)
    is_lat = i >= N_PROMPT_TILES
    h = _norm_mod(x_ref[...], gain_ref[...], _mod_row(sh_ref, i), _mod_row(sc_ref, i)).astype(BF16)
    cos, sp, sn = cos_ref[...], sp_ref[...], sn_ref[...]
    nh = DIFF_HEADS
    cache_k, cache_v = [], []
    for part, (g_ref, o_ref) in enumerate(((qg_ref, q_ref), (kg_ref, k_ref), (None, v_ref))):
        y = _dot(h, w_ref[:, part * D_MODEL:(part + 1) * D_MODEL])
        for c in range(nh):
            yc = y[:, c * LANES:(c + 1) * LANES]
            if g_ref is not None:
                yc = yc * _rms_scale_halves(yc) * g_ref[...]
                yc = jnp.where(is_lat, _rope(yc, cos, sp, sn, DIFF_HEAD_DIM // 4), yc)
            o_ref[:, c * LANES:(c + 1) * LANES] = yc.astype(BF16)
            if part == 1:
                cache_k.append(yc)
            elif part == 2:
                cache_v.append(yc)

    @pl.when(i < N_PROMPT_TILES)
    def _():
        for hd in range(nh):
            ck_ref[0, 0, hd, 0] = cache_k[hd][:, :HALF]
            ck_ref[0, 0, hd, 1] = cache_k[hd][:, HALF:]
            cv_ref[0, 0, hd] = cache_v[hd]


def _proj_diff(x, mods, gain, w, qg, kg, tables):
    n = DIFF_HEADS * 2 * DIFF_HEAD_DIM
    return pl.pallas_call(
        _proj_diff_kernel,
        grid=(N_TILES,),
        in_specs=[_tok_spec(D_MODEL), _const_spec((1, D_MODEL)), _mod_spec(0), _mod_spec(1),
                  _const_spec(w.shape), _const_spec((1, LANES)), _const_spec((1, LANES)),
                  _ROPE_SPEC, _ROPE_SPEC, _ROPE_SPEC],
        out_specs=[_tok_spec(n), _tok_spec(n), _tok_spec(n),
                   _cache_spec(DIFF_HEADS, 2, SEQ, DIFF_HEAD_DIM), _cache_spec(DIFF_HEADS, SEQ, 2 * DIFF_HEAD_DIM)],
        out_shape=[jax.ShapeDtypeStruct((N_TOK, n), BF16)] * 3
                  + [_cache_shape(DIFF_HEADS, 2, SEQ, DIFF_HEAD_DIM),
                     _cache_shape(DIFF_HEADS, SEQ, 2 * DIFF_HEAD_DIM)],
        compiler_params=_cparams(1),
        name="proj_diff",
    )(x, gain, mods, mods, w, qg, kg, *tables)


def _dup_halves(yc):
    lo = _lane_lo(yc.shape)
    sw = pltpu.roll(yc, HALF, 1)
    return jnp.where(lo, yc, sw), jnp.where(lo, sw, yc)


def _proj_swa_kernel(x_ref, gain_ref, sh_ref, sc_ref, w_ref, qg_ref, kg_ref,
                     cos_ref, sp_ref, sn_ref, q_ref, kd_ref, vd_ref, ck_ref, cv_ref):
    i = pl.program_id(0)
    is_lat = i >= N_PROMPT_TILES
    h = _norm_mod(x_ref[...], gain_ref[...], _mod_row(sh_ref, i), _mod_row(sc_ref, i)).astype(BF16)
    y = _dot(h, w_ref[...])
    cos, sp, sn = cos_ref[...], sp_ref[...], sn_ref[...]
    nq = SWA_HEADS * SWA_HEAD_DIM // LANES
    nk = SWA_KV_HEADS * SWA_HEAD_DIM // LANES
    cache = []
    for c in range(nq + 2 * nk):
        yc = y[:, c * LANES:(c + 1) * LANES]
        if c < nq + nk:
            yc = yc * _rms_scale_halves(yc) * (qg_ref[...] if c < nq else kg_ref[...])
            yc = jnp.where(is_lat, _rope(yc, cos, sp, sn, SWA_HEAD_DIM // 4), yc)
        if c < nq:
            q_ref[:, c * LANES:(c + 1) * LANES] = yc.astype(BF16)
            continue
        j = c - nq if c < nq + nk else c - nq - nk
        c_ref, d_ref = (ck_ref, kd_ref) if c < nq + nk else (cv_ref, vd_ref)
        da, db = _dup_halves(yc)
        d_ref[:, (2 * j) * LANES:(2 * j + 1) * LANES] = da.astype(BF16)
        d_ref[:, (2 * j + 1) * LANES:(2 * j + 2) * LANES] = db.astype(BF16)
        cache.append((c_ref, 2 * j, da))
        cache.append((c_ref, 2 * j + 1, db))

    @pl.when(i < N_PROMPT_TILES)
    def _():
        for ref, hd, val in cache:
            ref[0, 0, hd] = val[:, :HALF]


def _proj_swa(x, mods, gain, w, qg, kg, tables):
    nq, nk = SWA_HEADS * SWA_HEAD_DIM, SWA_KV_HEADS * SWA_HEAD_DIM
    return pl.pallas_call(
        _proj_swa_kernel,
        grid=(N_TILES,),
        in_specs=[_tok_spec(D_MODEL), _const_spec((1, D_MODEL)), _mod_spec(0), _mod_spec(1),
                  _const_spec(w.shape), _const_spec((1, LANES)), _const_spec((1, LANES)),
                  _ROPE_SPEC, _ROPE_SPEC, _ROPE_SPEC],
        out_specs=[_tok_spec(nq), _tok_spec(2 * nk), _tok_spec(2 * nk),
                   _cache_spec(SWA_KV_HEADS, SEQ, SWA_HEAD_DIM), _cache_spec(SWA_KV_HEADS, SEQ, SWA_HEAD_DIM)],
        out_shape=[jax.ShapeDtypeStruct((N_TOK, nq), BF16),
                   jax.ShapeDtypeStruct((N_TOK, 2 * nk), BF16),
                   jax.ShapeDtypeStruct((N_TOK, 2 * nk), BF16),
                   _cache_shape(SWA_KV_HEADS, SEQ, SWA_HEAD_DIM), _cache_shape(SWA_KV_HEADS, SEQ, SWA_HEAD_DIM)],
        compiler_params=_cparams(1),
        name="proj_swa",
    )(x, gain, mods, mods, w, qg, kg, *tables)


def _proj_mla_kernel(x_ref, gain_ref, sh_ref, sc_ref, w_in_ref, qa_ref, kva_ref, w_uq_ref,
                     qg_ref, qgp_ref, cos_ref, sp_ref, sn_ref,
                     qn_ref, qp_ref, ckv_ref, kpe_ref, c_ckv_ref, c_kpe_ref):
    i = pl.program_id(0)
    is_lat = i >= N_PROMPT_TILES
    h = _norm_mod(x_ref[...], gain_ref[...], _mod_row(sh_ref, i), _mod_row(sc_ref, i)).astype(BF16)
    y = _dot(h, w_in_ref[...])
    c_q = y[:, :MLA_Q_RANK]
    c_kv = y[:, MLA_Q_RANK:MLA_Q_RANK + MLA_KV_RANK]
    kpe = y[:, MLA_Q_RANK + MLA_KV_RANK:]
    kpe_ref[...] = kpe
    ckv = c_kv * lax.rsqrt(jnp.mean(c_kv * c_kv, axis=-1, keepdims=True) + EPS) * kva_ref[...]
    ckv_ref[...] = ckv.astype(BF16)

    @pl.when(i < N_PROMPT_TILES)
    def _():
        c_ckv_ref[0, 0] = ckv
        c_kpe_ref[0, 0] = kpe[:, :MLA_ROPE]

    cq = (c_q * lax.rsqrt(jnp.mean(c_q * c_q, axis=-1, keepdims=True) + EPS) * qa_ref[...])
    q = _dot(cq.astype(BF16), w_uq_ref[...])
    cos, sp, sn = cos_ref[...], sp_ref[...], sn_ref[...]
    n_nope = MLA_HEADS * MLA_NOPE
    lo = _lane_lo((TM, LANES))
    inv_d = 1.0 / (MLA_NOPE + MLA_ROPE)
    for j in range(MLA_HEADS // 2):
        pe = q[:, n_nope + j * LANES:n_nope + (j + 1) * LANES]
        pe_sq = pe * pe
        rs = []
        for a in range(2):
            hh = 2 * j + a
            nope = q[:, hh * LANES:(hh + 1) * LANES]
            ss = (jnp.sum(nope * nope, axis=-1, keepdims=True)
                  + jnp.sum(jnp.where(lo, pe_sq, 0.0) if a == 0 else jnp.where(lo, 0.0, pe_sq),
                            axis=-1, keepdims=True))
            r = lax.rsqrt(ss * inv_d + EPS)
            rs.append(r)
            qn_ref[:, hh * LANES:(hh + 1) * LANES] = (nope * r * qg_ref[...]).astype(BF16)
        pe = pe * jnp.where(lo, rs[0], rs[1]) * qgp_ref[...]
        pe = jnp.where(is_lat, _rope(pe, cos, sp, sn, MLA_ROPE // 4), pe)
        qp_ref[:, j * LANES:(j + 1) * LANES] = pe.astype(BF16)


def _proj_mla(x, mods, gain, w_in, qa, kva, w_uq, qg, qgp, tables):
    n_nope = MLA_HEADS * MLA_NOPE
    n_pe = MLA_HEADS * MLA_ROPE
    return pl.pallas_call(
        _proj_mla_kernel,
        grid=(N_TILES,),
        in_specs=[_tok_spec(D_MODEL), _const_spec((1, D_MODEL)), _mod_spec(0), _mod_spec(1),
                  _const_spec(w_in.shape), _const_spec((1, MLA_Q_RANK)), _const_spec((1, MLA_KV_RANK)),
                  _const_spec(w_uq.shape), _const_spec((1, LANES)), _const_spec((1, LANES)),
                  _ROPE_SPEC, _ROPE_SPEC, _ROPE_SPEC],
        out_specs=[_tok_spec(n_nope), _tok_spec(n_pe), _tok_spec(MLA_KV_RANK), _tok_spec(LANES),
                   _cache_spec(SEQ, MLA_KV_RANK), _cache_spec(SEQ, MLA_ROPE)],
        out_shape=[jax.ShapeDtypeStruct((N_TOK, n_nope), BF16),
                   jax.ShapeDtypeStruct((N_TOK, n_pe), BF16),
                   jax.ShapeDtypeStruct((N_TOK, MLA_KV_RANK), BF16),
                   jax.ShapeDtypeStruct((N_TOK, LANES), F32),
                   _cache_shape(SEQ, MLA_KV_RANK), _cache_shape(SEQ, MLA_ROPE)],
        compiler_params=_cparams(1),
        name="proj_mla",
    )(x, gain, mods, mods, w_in, qa, kva, w_uq, qg, qgp, *tables)


def _mla_expand_kernel(ckv_ref, kpe_ref, w_ref, kg_ref, kgp_ref, cos_ref, sp_ref, sn_ref,
                       kn_ref, kp_ref, v_ref, *, rope_from_tile):
    i = pl.program_id(0)
    kv = _dot(ckv_ref[...].astype(BF16), w_ref[...])
    kpe = kpe_ref[...]
    lo = _lane_lo(kpe.shape)
    pe_ss = jnp.sum(jnp.where(lo, kpe * kpe, 0.0), axis=-1, keepdims=True)
    inv_d = 1.0 / (MLA_NOPE + MLA_ROPE)
    cos, sp, sn = cos_ref[...], sp_ref[...], sn_ref[...]
    for j in range(MLA_HEADS // 2):
        rs = []
        for a in range(2):
            hh = 2 * j + a
            kn = kv[:, hh * 2 * LANES:hh * 2 * LANES + LANES]
            v_ref[:, hh * LANES:(hh + 1) * LANES] = kv[:, hh * 2 * LANES + LANES:(hh + 1) * 2 * LANES].astype(BF16)
            r = lax.rsqrt((jnp.sum(kn * kn, axis=-1, keepdims=True) + pe_ss) * inv_d + EPS)
            rs.append(r)
            kn_ref[:, hh * LANES:(hh + 1) * LANES] = (kn * r * kg_ref[...]).astype(BF16)
        pe = kpe * jnp.where(lo, rs[0], rs[1]) * kgp_ref[...]
        if rope_from_tile is not None:
            pe = jnp.where(i >= rope_from_tile, _rope(pe, cos, sp, sn, MLA_ROPE // 4), pe)
        kp_ref[:, j * LANES:(j + 1) * LANES] = pe.astype(BF16)


def _mla_expand(ckv, kpe_dup, w_ukv, kg, kgp, tables, rope_from_tile):
    n = ckv.shape[0]
    n_nope = MLA_HEADS * MLA_NOPE
    n_pe = MLA_HEADS * MLA_ROPE
    return pl.pallas_call(
        functools.partial(_mla_expand_kernel, rope_from_tile=rope_from_tile),
        grid=(n // TM,),
        in_specs=[_tok_spec(MLA_KV_RANK), _tok_spec(LANES), _const_spec(w_ukv.shape),
                  _const_spec((1, LANES)), _const_spec((1, LANES)),
                  _ROPE_SPEC, _ROPE_SPEC, _ROPE_SPEC],
        out_specs=[_tok_spec(n_nope), _tok_spec(n_pe), _tok_spec(n_nope)],
        out_shape=[jax.ShapeDtypeStruct((n, n_nope), BF16),
                   jax.ShapeDtypeStruct((n, n_pe), BF16),
                   jax.ShapeDtypeStruct((n, n_nope), BF16)],
        compiler_params=_cparams(1),
        name="mla_expand",
    )(ckv, kpe_dup, w_ukv, kg, kgp, *tables)


def _prompt_spec(width):
    return pl.BlockSpec((TM, width), lambda b: (b, 0))


def _latq_spec(rows, width):
    per = DEC_SEQ // rows
    return pl.BlockSpec((rows, width), lambda b, t: (N_PROMPT_TOK // rows + b * per + t, 0))


def _latkv_spec(width):
    return pl.BlockSpec((DEC_SEQ, width), lambda b, t: (LAT_BLOCK0 + b, 0))


def _lato_spec(rows):
    per = DEC_SEQ // rows
    return pl.BlockSpec((rows, D_MODEL), lambda b, t: (b * per + t, 0))


def _att_kernel(*refs, with_ctx):
    if with_ctx:
        q_ref, k_ref, v_ref, kc_ref, vc_ref, o_ref = refs
    else:
        q_ref, k_ref, v_ref, o_ref = refs
    tq = q_ref.shape[0]
    ng = ATT_HEADS // ATT_KV_HEADS
    for kv in range(ATT_KV_HEADS):
        q = jnp.concatenate([_chunk(q_ref, kv * ng + g) for g in range(ng)], axis=0)
        s_list = [_dot_nt(_chunk(k_ref, kv), q)]
        values = [_chunk(v_ref, kv)]
        if with_ctx:
            s_list.append(_dot_nt(kc_ref[kv].astype(BF16), q))
            values.append(vc_ref[kv].astype(BF16))
        ps, inv = _softmax2_parts(s_list)
        o = _pv(ps, values) * inv
        for g in range(ng):
            o_ref[:, (kv * ng + g) * LANES:(kv * ng + g + 1) * LANES] = (
                o[:, g * tq:(g + 1) * tq].T.astype(o_ref.dtype))


def _att_attend(q, k, v, cache_k, cache_v):
    nk = ATT_KV_HEADS * ATT_HEAD_DIM
    out_p = pl.pallas_call(
        functools.partial(_att_kernel, with_ctx=False),
        grid=(N_PROMPT_TILES,),
        in_specs=[_prompt_spec(D_MODEL), _prompt_spec(nk), _prompt_spec(nk)],
        out_specs=_prompt_spec(D_MODEL),
        out_shape=jax.ShapeDtypeStruct((N_PROMPT_TOK, D_MODEL), BF16),
        compiler_params=_cparams(1),
        name="att_prompt",
    )(q, k, v)
    ctx = pl.BlockSpec((None, None, ATT_KV_HEADS, PAST_LEN, LANES), lambda b, t: (b, 0, 0, 0, 0))
    out_s = pl.pallas_call(
        functools.partial(_att_kernel, with_ctx=True),
        grid=(DEC_BATCH, TILES_PER_DEC),
        in_specs=[_latq_spec(TM, D_MODEL), _latkv_spec(nk), _latkv_spec(nk), ctx, ctx],
        out_specs=_lato_spec(TM),
        out_shape=jax.ShapeDtypeStruct((N_LAT_TOK, D_MODEL), BF16),
        compiler_params=_cparams(2),
        name="att_latent",
    )(q, k, v, cache_k, cache_v)
    return out_p, out_s


def _diff_kernel(*refs, lam_init, with_ctx):
    if with_ctx:
        (q_ref, k_ref, v_ref, kc_ref, vc_ref, lq1_ref, lk1_ref, lq2_ref, lk2_ref, sub_ref, o_ref) = refs
    else:
        (q_ref, k_ref, v_ref, lq1_ref, lk1_ref, lq2_ref, lk2_ref, sub_ref, o_ref) = refs
    tq = q_ref.shape[0]
    lam = (jnp.exp(jnp.sum(lq1_ref[...] * lk1_ref[...], axis=-1, keepdims=True))
           - jnp.exp(jnp.sum(lq2_ref[...] * lk2_ref[...], axis=-1, keepdims=True)) + lam_init)
    sub = sub_ref[...] * (1.0 - lam_init)
    for hd in range(DIFF_HEADS):
        q = jnp.concatenate(_split_halves(_chunk(q_ref, hd)), axis=0)
        s_list = [_dot_nt(_chunk(k_ref, hd), q)]
        values = [_chunk(v_ref, hd)]
        if with_ctx:
            s_list.append(_dot_nt(kc_ref[hd].astype(BF16), q))
            values.append(vc_ref[hd].astype(BF16))
        ps, inv = _softmax2_parts(s_list)
        c0 = inv[:, :tq]
        c1 = -lam * inv[:, tq:]
        o = _pv([p[:, :tq] * c0 + p[:, tq:] * c1 for p in ps], values)
        o = o * lax.rsqrt(jnp.mean(o * o, axis=0, keepdims=True) + EPS) * sub
        o_ref[:, hd * LANES:(hd + 1) * LANES] = o.T.astype(o_ref.dtype)


def _diff_attend(q, k, v, cache_k_pair, cache_v, lq1, lk1, lq2, lk2, subln, lam_init):
    small = [lq1, lk1, lq2, lk2, subln]
    small_specs = [_const_spec(s.shape) for s in small]
    out_p = pl.pallas_call(
        functools.partial(_diff_kernel, lam_init=lam_init, with_ctx=False),
        grid=(N_PROMPT_TILES,),
        in_specs=[_prompt_spec(D_MODEL)] * 3 + small_specs,
        out_specs=_prompt_spec(D_MODEL),
        out_shape=jax.ShapeDtypeStruct((N_PROMPT_TOK, D_MODEL), BF16),
        compiler_params=_cparams(1),
        name="diff_prompt",
    )(q, k, v, *small)
    out_s = pl.pallas_call(
        functools.partial(_diff_kernel, lam_init=lam_init, with_ctx=True),
        grid=(DEC_BATCH, TILES_PER_DEC),
        in_specs=[_latq_spec(TM, D_MODEL), _latkv_spec(D_MODEL), _latkv_spec(D_MODEL),
                  pl.BlockSpec((None, DIFF_HEADS, PAST_LEN, LANES), lambda b, t: (b, 0, 0, 0)),
                  pl.BlockSpec((None, None, DIFF_HEADS, PAST_LEN, LANES), lambda b, t: (b, 0, 0, 0, 0))]
                 + small_specs,
        out_specs=_lato_spec(TM),
        out_shape=jax.ShapeDtypeStruct((N_LAT_TOK, D_MODEL), BF16),
        compiler_params=_cparams(2),
        name="diff_latent",
    )(q, k, v, cache_k_pair, cache_v, *small)
    return out_p, out_s


def _swa_group(q_ref, sink_ref, kv, score_fns, values, o_ref):
    tq = q_ref.shape[0]
    ng = SWA_HEADS // SWA_KV_HEADS
    rows = []
    for c in range(ng // 2):
        rows.extend(_split_halves(_chunk(q_ref, kv * (ng // 2) + c)))
    q = jnp.concatenate(rows, axis=0)
    sink = jnp.concatenate([jnp.full((1, tq), sink_ref[kv * ng + g] * LOG2E, F32) for g in range(ng)],
                           axis=1)
    ps, inv = _softmax2_parts([fn(q) for fn in score_fns], extra=sink)
    o = _pv(ps, values) * inv
    first = lax.broadcasted_iota(jnp.int32, (LANES, tq), 0) < HALF
    for c in range(ng // 2):
        oc = jnp.where(first, o[:, (2 * c) * tq:(2 * c + 1) * tq], o[:, (2 * c + 1) * tq:(2 * c + 2) * tq])
        cc = kv * (ng // 2) + c
        o_ref[:, cc * LANES:(cc + 1) * LANES] = oc.T.astype(o_ref.dtype)


def _swa_prompt_kernel(sink_ref, q_ref, k_ref, v_ref, o_ref):
    for kv in range(SWA_KV_HEADS):
        k = _chunk(k_ref, kv)
        _swa_group(q_ref, sink_ref, kv, [lambda q, k=k: _dot_nt(k, q)], [_chunk(v_ref, kv)], o_ref)


def _swa_latent_kernel(sink_ref, q_ref, k_ref, v_ref, kc_ref, vc_ref, o_ref):
    n = pl.program_id(1)
    tq = q_ref.shape[0]
    ng = SWA_HEADS // SWA_KV_HEADS
    span = 3 * SWA_QB
    start = pl.multiple_of(jnp.clip((n - 1) * SWA_QB, 0, DEC_SEQ - span), SWA_QB)
    cols = lax.broadcasted_iota(jnp.int32, (span, ng * tq), 1)
    qpos = n * SWA_QB + jnp.bitwise_and(cols, tq - 1)
    kpos = start + lax.broadcasted_iota(jnp.int32, (span, ng * tq), 0)
    valid = jnp.abs(qpos - kpos) <= WINDOW
    for kv in range(SWA_KV_HEADS):
        k = k_ref[pl.ds(start, span), kv * LANES:(kv + 1) * LANES]
        v = v_ref[pl.ds(start, span), kv * LANES:(kv + 1) * LANES]
        kc = kc_ref[kv]
        _swa_group(q_ref, sink_ref, kv,
                   [lambda q, k=k: jnp.where(valid, _dot_nt(k, q), -1e30),
                    lambda q, kc=kc: _dot_nt(kc, q)],
                   [v, vc_ref[kv]], o_ref)


def _swa_attend(q, kd, vd, cache_kd, cache_vd, sink):
    nkd = 2 * SWA_KV_HEADS * SWA_HEAD_DIM
    smem = pl.BlockSpec(memory_space=pltpu.SMEM)
    out_p = pl.pallas_call(
        _swa_prompt_kernel,
        grid=(N_PROMPT_TILES,),
        in_specs=[smem, _prompt_spec(D_MODEL), _prompt_spec(nkd), _prompt_spec(nkd)],
        out_specs=_prompt_spec(D_MODEL),
        out_shape=jax.ShapeDtypeStruct((N_PROMPT_TOK, D_MODEL), BF16),
        compiler_params=_cparams(1),
        name="swa_prompt",
    )(sink, q, kd, vd)
    ctx = pl.BlockSpec((None, SWA_KV_HEADS, PAST_LEN, LANES), lambda b, n: (b, 0, 0, 0))
    out_s = pl.pallas_call(
        _swa_latent_kernel,
        grid=(DEC_BATCH, DEC_SEQ // SWA_QB),
        in_specs=[smem, _latq_spec(SWA_QB, D_MODEL), _latkv_spec(nkd), _latkv_spec(nkd), ctx, ctx],
        out_specs=_lato_spec(SWA_QB),
        out_shape=jax.ShapeDtypeStruct((N_LAT_TOK, D_MODEL), BF16),
        compiler_params=_cparams(2),
        name="swa_latent",
    )(sink, q, kd, vd, cache_kd, cache_vd)
    return out_p, out_s


def _mla_kernel(*refs, with_ctx):
    if with_ctx:
        (qn_ref, qp_ref, kn_ref, kp_ref, v_ref, knc_ref, kpc_ref, vc_ref, o_ref) = refs
    else:
        (qn_ref, qp_ref, kn_ref, kp_ref, v_ref, o_ref) = refs
    for hd in range(MLA_HEADS):
        j, a = hd // 2, hd % 2
        q = jnp.concatenate([_chunk(qn_ref, hd), _split_halves(_chunk(qp_ref, j))[a]], axis=1)
        s_list = [_dot_nt(jnp.concatenate([_chunk(kn_ref, hd), _chunk(kp_ref, j)], axis=1), q)]
        values = [_chunk(v_ref, hd)]
        if with_ctx:
            s_list.append(_dot_nt(jnp.concatenate([_chunk(knc_ref, hd), _chunk(kpc_ref, j)], axis=1), q))
            values.append(_chunk(vc_ref, hd))
        ps, inv = _softmax2_parts(s_list)
        o_ref[:, hd * LANES:(hd + 1) * LANES] = (_pv(ps, values) * inv).T.astype(o_ref.dtype)


def _mla_attend(qn, qp, kn, kp, v, knc, kpc, vc):
    n_pe = MLA_HEADS * MLA_ROPE
    out_p = pl.pallas_call(
        functools.partial(_mla_kernel, with_ctx=False),
        grid=(N_PROMPT_TILES,),
        in_specs=[_prompt_spec(D_MODEL), _prompt_spec(n_pe), _prompt_spec(D_MODEL), _prompt_spec(n_pe),
                  _prompt_spec(D_MODEL)],
        out_specs=_prompt_spec(D_MODEL),
        out_shape=jax.ShapeDtypeStruct((N_PROMPT_TOK, D_MODEL), BF16),
        compiler_params=_cparams(1),
        name="mla_prompt",
    )(qn, qp, kn, kp, v)

    def ctx(width):
        return pl.BlockSpec((PAST_LEN, width), lambda b, t: (b, 0))

    out_s = pl.pallas_call(
        functools.partial(_mla_kernel, with_ctx=True),
        grid=(DEC_BATCH, TILES_PER_DEC),
        in_specs=[_latq_spec(TM, D_MODEL), _latq_spec(TM, n_pe),
                  _latkv_spec(D_MODEL), _latkv_spec(n_pe), _latkv_spec(D_MODEL),
                  ctx(D_MODEL), ctx(n_pe), ctx(D_MODEL)],
        out_specs=_lato_spec(TM),
        out_shape=jax.ShapeDtypeStruct((N_LAT_TOK, D_MODEL), BF16),
        compiler_params=_cparams(2),
        name="mla_latent",
    )(qn, qp, kn, kp, v, knc, kpc, vc)
    return out_p, out_s


def _oproj_kernel(ap_ref, as_ref, w_ref, x_ref, g1_ref, gain_ref, sh_ref, sc_ref, x1_ref, h2_ref):
    i = pl.program_id(0)
    a = jnp.where(i < N_PROMPT_TILES, ap_ref[...], as_ref[...])
    o = _dot(a, w_ref[...])
    x1 = x_ref[...] + _mod_row(g1_ref, i) * o
    x1_ref[...] = x1
    h2_ref[...] = _norm_mod(x1, gain_ref[...], _mod_row(sh_ref, i), _mod_row(sc_ref, i)).astype(BF16)


def _oproj(attn_p, attn_s, w_o, x, mods, gain_ffn):
    return pl.pallas_call(
        _oproj_kernel,
        grid=(N_TILES,),
        in_specs=[pl.BlockSpec((TM, D_MODEL), lambda i: (jnp.minimum(i, N_PROMPT_TILES - 1), 0)),
                  pl.BlockSpec((TM, D_MODEL), lambda i: (jnp.maximum(i - N_PROMPT_TILES, 0), 0)),
                  _const_spec(w_o.shape), _tok_spec(D_MODEL), _mod_spec(2),
                  _const_spec((1, D_MODEL)), _mod_spec(3), _mod_spec(4)],
        out_specs=[_tok_spec(D_MODEL), _tok_spec(D_MODEL)],
        out_shape=[jax.ShapeDtypeStruct((N_TOK, D_MODEL), F32),
                   jax.ShapeDtypeStruct((N_TOK, D_MODEL), BF16)],
        compiler_params=_cparams(1),
        name="oproj",
    )(attn_p, attn_s, w_o, x, mods, gain_ffn, mods, mods)


def _mlp_kernel(h_ref, w1_ref, w2_ref, x_ref, g2_ref, o_ref):
    i = pl.program_id(0)
    grp = _tile_group(i * (MLP_TM // TM))
    h = h_ref[...]
    acc = None
    for c in range(D_FF // MLP_FF_CHUNK):
        u = _dot(h, w1_ref[:, c * MLP_FF_CHUNK:(c + 1) * MLP_FF_CHUNK])
        u = jnp.square(jnp.maximum(u, 0.0)).astype(BF16)
        t = _dot(u, w2_ref[c * MLP_FF_CHUNK:(c + 1) * MLP_FF_CHUNK, :])
        acc = t if acc is None else acc + t
    o_ref[...] = x_ref[...] + g2_ref[pl.ds(grp, 1), :] * acc


def _mlp(h2, w1, w2, x1, mods):
    spec = pl.BlockSpec((MLP_TM, D_MODEL), lambda i: (i, 0))
    return pl.pallas_call(
        _mlp_kernel,
        grid=(N_TOK // MLP_TM,),
        in_specs=[spec, _const_spec(w1.shape), _const_spec(w2.shape), spec, _mod_spec(5)],
        out_specs=spec,
        out_shape=jax.ShapeDtypeStruct((N_TOK, D_MODEL), F32),
        compiler_params=_cparams(1),
        name="mlp",
    )(h2, w1, w2, x1, mods)


def _row(v, scale=1.0):
    return (v.astype(F32) * scale).reshape(1, -1)


def _pair(v, scale=1.0):
    return (jnp.concatenate([v, v]).astype(F32) * scale).reshape(1, LANES)


def kernel(x_prompt, x_sample, cache_att_k, cache_att_v, cache_diff_k, cache_diff_v, cache_swa_k, cache_swa_v, cache_mla_ckv, cache_mla_kpe, c, c_ctx, ada_w, ada_b, norm_mix, norm_ffn, att_w_qkv, att_q_norm, att_k_norm, att_w_o, diff_w_qkv, diff_q_norm, diff_k_norm, diff_lq1, diff_lk1, diff_lq2, diff_lk2, diff_subln, diff_w_o, swa_w_qkv, swa_q_norm, swa_k_norm, swa_sink, swa_w_o, mla_w_in, mla_q_a_norm, mla_kv_a_norm, mla_w_uq, mla_w_ukv, mla_q_norm, mla_k_norm, mla_w_o, mlp_w1, mlp_w2):
    x = jnp.concatenate([x_prompt.reshape(N_PROMPT_TOK, D_MODEL),
                         x_sample.reshape(N_LAT_TOK, D_MODEL)], axis=0)
    cond = jnp.concatenate([c_ctx[None], c, jnp.zeros((COND_ROWS - 1 - DEC_BATCH, D_MODEL), F32)], axis=0)
    mods_all = _modulation(cond, ada_w, ada_b)

    tab_att = _rope_tables(ATT_HEAD_DIM)
    tab_64 = _rope_tables(DIFF_HEAD_DIM)

    outs = {}
    for layer in range(DEPTH):
        mods = mods_all[layer]
        gain_mix = _row(norm_mix[layer])
        gain_ffn = _row(norm_ffn[layer])
        if layer == 0:
            qs = ATT_HEAD_DIM ** -0.5 * LOG2E
            q, k, v, outs["att_k"], outs["att_v"] = _proj_att(
                x, mods, gain_mix, att_w_qkv[0].astype(BF16),
                _row(att_q_norm[0], qs), _row(att_k_norm[0]), tab_att)
            attn_p, attn_s = _att_attend(q, k, v, cache_att_k, cache_att_v)
            w_o = att_w_o[0]
        elif layer == 1:
            qs = DIFF_HEAD_DIM ** -0.5 * LOG2E
            q, k, v, outs["diff_k"], outs["diff_v"] = _proj_diff(
                x, mods, gain_mix, diff_w_qkv[0].astype(BF16),
                _pair(diff_q_norm[0], qs), _pair(diff_k_norm[0]), tab_64)
            lam_init = 0.8 - 0.6 * math.exp(-0.3 * layer)
            ck = cache_diff_k[:, 0].transpose(0, 1, 3, 2, 4).reshape(
                DEC_BATCH, DIFF_HEADS, PAST_LEN, LANES)
            attn_p, attn_s = _diff_attend(q, k, v, ck, cache_diff_v,
                                          _row(diff_lq1[0]), _row(diff_lk1[0]),
                                          _row(diff_lq2[0]), _row(diff_lk2[0]),
                                          diff_subln[0].astype(F32).reshape(LANES, 1), lam_init)
            w_o = diff_w_o[0]
        elif layer == 2:
            qs = SWA_HEAD_DIM ** -0.5 * LOG2E
            q, kd, vd, outs["swa_k"], outs["swa_v"] = _proj_swa(
                x, mods, gain_mix, swa_w_qkv[0].astype(BF16),
                _pair(swa_q_norm[0], qs), _pair(swa_k_norm[0]), tab_64)
            ckd = jnp.concatenate([cache_swa_k[:, 0]] * 2, axis=-1).astype(BF16)
            cvd = jnp.concatenate([cache_swa_v[:, 0]] * 2, axis=-1).astype(BF16)
            attn_p, attn_s = _swa_attend(q, kd, vd, ckd, cvd, swa_sink[0].astype(F32))
            w_o = swa_w_o[0]
        else:
            qs = (MLA_NOPE + MLA_ROPE) ** -0.5 * LOG2E
            w_in = mla_w_in[0]
            w_in = jnp.concatenate([w_in, w_in[:, -MLA_ROPE:]], axis=1).astype(BF16)
            w_uq = mla_w_uq[0].reshape(MLA_Q_RANK, MLA_HEADS, MLA_NOPE + MLA_ROPE)
            w_uq = jnp.concatenate([w_uq[:, :, :MLA_NOPE].reshape(MLA_Q_RANK, -1),
                                    w_uq[:, :, MLA_NOPE:].reshape(MLA_Q_RANK, -1)], axis=1).astype(BF16)
            w_ukv = mla_w_ukv[0].astype(BF16)
            qg, kg = mla_q_norm[0], mla_k_norm[0]
            qn, qp, ckv, kpe, outs["mla_ckv"], outs["mla_kpe"] = _proj_mla(
                x, mods, gain_mix, w_in, _row(mla_q_a_norm[0]), _row(mla_kv_a_norm[0]), w_uq,
                _row(qg[:MLA_NOPE], qs), _pair(qg[MLA_NOPE:], qs), tab_64)
            kn, kp, vv = _mla_expand(ckv, kpe, w_ukv, _row(kg[:MLA_NOPE]), _pair(kg[MLA_NOPE:]),
                                     tab_64, N_PROMPT_TILES)
            c_ckv = cache_mla_ckv[:, 0].reshape(DEC_BATCH * PAST_LEN, MLA_KV_RANK)
            c_kpe = cache_mla_kpe[:, 0].reshape(DEC_BATCH * PAST_LEN, MLA_ROPE)
            c_kpe = jnp.concatenate([c_kpe, c_kpe], axis=-1)
            knc, kpc, vc = _mla_expand(c_ckv, c_kpe, w_ukv, _row(kg[:MLA_NOPE]), _pair(kg[MLA_NOPE:]),
                                       tab_64, None)
            attn_p, attn_s = _mla_attend(qn, qp, kn, kp, vv, knc, kpc, vc)
            w_o = mla_w_o[0]
        x1, h2 = _oproj(attn_p, attn_s, w_o.astype(BF16), x, mods, gain_ffn)
        x = _mlp(h2, mlp_w1[layer].astype(BF16), mlp_w2[layer].astype(BF16), x1, mods)

    y_prompt = x[:N_PROMPT_TOK].reshape(BATCH, SEQ, D_MODEL)
    y_sample = x[N_PROMPT_TOK:].reshape(DEC_BATCH, DEC_SEQ, D_MODEL)
    return (y_prompt, y_sample, outs["att_k"], outs["att_v"], outs["diff_k"], outs["diff_v"],
            outs["swa_k"], outs["swa_v"], outs["mla_ckv"], outs["mla_kpe"])
```

```python
import functools
import math

import numpy as np
import jax
import jax.numpy as jnp
from jax import lax
from jax.experimental import pallas as pl
from jax.experimental.pallas import tpu as pltpu

D_MODEL = 1024
BATCH = 16
SEQ = 256
DEPTH = 4
DEC_BATCH = 2
DEC_SEQ = 1024
PAST_LEN = 256
GRID_W = 64
ROPE_THETA = 10000.0
EPS = 1e-6
D_FF = 4 * D_MODEL
MOD_CHUNKS = 6
LOG2E = 1.4426950408889634

ATT_HEADS, ATT_KV_HEADS, ATT_HEAD_DIM = 8, 2, 128
DIFF_HEADS, DIFF_HEAD_DIM = 8, 64
SWA_HEADS, SWA_KV_HEADS, SWA_HEAD_DIM, WINDOW = 16, 4, 64, 128
MLA_HEADS, MLA_NOPE, MLA_ROPE, MLA_VDIM = 8, 128, 64, 128
MLA_Q_RANK, MLA_KV_RANK = 512, 256

LANES = 128
HALF = LANES // 2
TM = 256
N_PROMPT_TOK = BATCH * SEQ
N_LAT_TOK = DEC_BATCH * DEC_SEQ
N_TOK = N_PROMPT_TOK + N_LAT_TOK
N_TILES = N_TOK // TM
N_PROMPT_TILES = N_PROMPT_TOK // TM
TILES_PER_DEC = DEC_SEQ // TM
LAT_BLOCK0 = N_PROMPT_TOK // DEC_SEQ
COND_ROWS = 8
MLP_TM = 512
MLP_FF_CHUNK = 1024
SWA_QB = 128
ATT_UNIT_HEADS = 4
VMEM_LIMIT = 56 * 1024 * 1024

F32 = jnp.float32
BF16 = jnp.bfloat16


def _cparams(n_axes):
    return pltpu.CompilerParams(dimension_semantics=("arbitrary",) * n_axes,
                                vmem_limit_bytes=VMEM_LIMIT)


def _dot(a, b):
    return jnp.dot(a, b, preferred_element_type=F32)


def _dot_nt(a, b):
    return lax.dot_general(a, b, (((1,), (1,)), ((), ())), preferred_element_type=F32)


def _const_spec(shape):
    nd = len(shape)
    return pl.BlockSpec(shape, lambda *_: (0,) * nd, pipeline_mode=pl.Buffered(1))


def _chunk(ref, c, width=LANES):
    return ref[:, c * width:(c + 1) * width]


def _tile_group(i):
    return jnp.where(i < N_PROMPT_TILES, 0, 1 + (i - N_PROMPT_TILES) // TILES_PER_DEC)


def _rope_tile(i):
    return jnp.maximum(i - N_PROMPT_TILES, 0) % TILES_PER_DEC


def _norm_mod(x, gain, shift, scale):
    ms = jnp.mean(x * x, axis=-1, keepdims=True)
    y = x * lax.rsqrt(ms + EPS) * gain
    return y * (1.0 + scale) + shift


def _lane_lo(shape):
    return lax.broadcasted_iota(jnp.int32, shape, len(shape) - 1) < HALF


def _rms_scale(y):
    return lax.rsqrt(jnp.mean(y * y, axis=-1, keepdims=True) + EPS)


def _rms_scale_halves(y):
    lo = _lane_lo(y.shape)
    sq = y * y
    s_lo = jnp.sum(jnp.where(lo, sq, 0.0), axis=-1, keepdims=True)
    s_hi = jnp.sum(jnp.where(lo, 0.0, sq), axis=-1, keepdims=True)
    return jnp.where(lo, lax.rsqrt(s_lo * (1.0 / HALF) + EPS), lax.rsqrt(s_hi * (1.0 / HALF) + EPS))


def _rope(y, cos, sin_prev, sin_next, quarter):
    return (y * cos + pltpu.roll(y, quarter, 1) * sin_prev
            + pltpu.roll(y, LANES - quarter, 1) * sin_next)


def _rope_tables(rot_dim):
    half = rot_dim // 2
    quarter = rot_dim // 4
    inv = np.float32(ROPE_THETA) ** (-np.arange(0, half, 2, dtype=np.float32) / np.float32(half))
    pos = np.arange(DEC_SEQ)
    row = (pos // GRID_W).astype(np.float32)
    col = (pos % GRID_W).astype(np.float32)
    lane = np.arange(LANES)
    dd = lane % rot_dim
    q = dd // quarter
    f = dd % quarter
    ang = np.where((q < 2)[None, :], row[:, None], col[:, None]) * inv[f][None, :]
    ang = ang.astype(np.float32)
    cos = np.cos(ang).astype(np.float32)
    sin = np.sin(ang).astype(np.float32)
    odd = (q % 2 == 1)[None, :]
    sin_prev = np.where(odd, sin, 0.0).astype(np.float32)
    sin_next = np.where(odd, 0.0, -sin).astype(np.float32)
    return jnp.asarray(cos), jnp.asarray(sin_prev), jnp.asarray(sin_next)


def _softmax2_parts(s_list, extra=None):
    m = jnp.max(s_list[0], axis=0, keepdims=True)
    for s in s_list[1:]:
        m = jnp.maximum(m, jnp.max(s, axis=0, keepdims=True))
    if extra is not None:
        m = jnp.maximum(m, extra)
    ps = [jnp.exp2(s - m) for s in s_list]
    tot = ps[0].sum(axis=0, keepdims=True)
    for p in ps[1:]:
        tot = tot + p.sum(axis=0, keepdims=True)
    if extra is not None:
        tot = tot + jnp.exp2(extra - m)
    return ps, 1.0 / tot


def _dot_tn(a, b):
    return lax.dot_general(a, b, (((0,), (0,)), ((), ())), preferred_element_type=F32)


def _head_pipeline(n, scores, finish):
    nxt = scores(0)
    for h in range(n):
        cur = nxt
        if h + 1 < n:
            nxt = scores(h + 1)
        finish(h, cur)


def _pv(ps, values):
    o = None
    for p, v in zip(ps, values):
        t = _dot_tn(v, p.astype(BF16))
        o = t if o is None else o + t
    return o


def _split_halves(q):
    lo = _lane_lo(q.shape)
    zero = jnp.zeros_like(q)
    return jnp.where(lo, q, zero), jnp.where(lo, zero, q)


def _mod_kernel(cond_ref, w_ref, b_ref, o_ref):
    c = cond_ref[...]
    s = (c * jax.nn.sigmoid(c)).astype(BF16)
    o_ref[0] = _dot(s, w_ref[0].astype(BF16)) + b_ref[0]


def _modulation(cond, ada_w, ada_b):
    tn = 1536
    n = MOD_CHUNKS * D_MODEL
    return pl.pallas_call(
        _mod_kernel,
        grid=(DEPTH, n // tn),
        in_specs=[
            pl.BlockSpec((COND_ROWS, D_MODEL), lambda l, j: (0, 0)),
            pl.BlockSpec((1, D_MODEL, tn), lambda l, j: (l, 0, j)),
            pl.BlockSpec((1, 1, tn), lambda l, j: (l, 0, j)),
        ],
        out_specs=pl.BlockSpec((1, COND_ROWS, tn), lambda l, j: (l, 0, j)),
        out_shape=jax.ShapeDtypeStruct((DEPTH, COND_ROWS, n), F32),
        compiler_params=_cparams(2),
        name="modulation",
    )(cond, ada_w, ada_b.reshape(DEPTH, 1, n))


def _mod_spec(chunk):
    return pl.BlockSpec((COND_ROWS, D_MODEL), lambda i: (0, chunk))


def _mod_row(ref, i):
    return ref[pl.ds(_tile_group(i), 1), :]


_ROPE_SPEC = pl.BlockSpec((TM, LANES), lambda i: (_rope_tile(i), 0))


def _tok_spec(width):
    return pl.BlockSpec((TM, width), lambda i: (i, 0))


def _cache_spec(*dims):
    nd = len(dims)
    return pl.BlockSpec((1, 1) + dims,
                        lambda i: (jnp.minimum(i, N_PROMPT_TILES - 1), 0) + (0,) * nd)


def _cache_shape(*dims):
    return jax.ShapeDtypeStruct((BATCH, 1) + dims, F32)


def _proj_att_kernel(x_ref, gain_ref, sh_ref, sc_ref, w_ref, qg_ref, kg_ref,
                     cos_ref, sp_ref, sn_ref, q_ref, k_ref, v_ref, ck_ref, cv_ref):
    i = pl.program_id(0)
    is_lat = i >= N_PROMPT_TILES
    h = _norm_mod(x_ref[...], gain_ref[...], _mod_row(sh_ref, i), _mod_row(sc_ref, i)).astype(BF16)
    y = _dot(h, w_ref[...])
    cos, sp, sn = cos_ref[...], sp_ref[...], sn_ref[...]
    nq, nk = ATT_HEADS, ATT_KV_HEADS
    cache = []
    for c in range(nq + 2 * nk):
        yc = y[:, c * LANES:(c + 1) * LANES]
        if c < nq + nk:
            yc = yc * _rms_scale(yc) * (qg_ref[...] if c < nq else kg_ref[...])
            yc = jnp.where(is_lat, _rope(yc, cos, sp, sn, ATT_HEAD_DIM // 4), yc)
        if c < nq:
            q_ref[:, c * LANES:(c + 1) * LANES] = yc.astype(BF16)
        elif c < nq + nk:
            k_ref[:, (c - nq) * LANES:(c - nq + 1) * LANES] = yc.astype(BF16)
            cache.append((ck_ref, c - nq, yc))
        else:
            v_ref[:, (c - nq - nk) * LANES:(c - nq - nk + 1) * LANES] = yc.astype(BF16)
            cache.append((cv_ref, c - nq - nk, yc))

    @pl.when(i < N_PROMPT_TILES)
    def _():
        for ref, hd, val in cache:
            ref[0, 0, hd] = val


def _proj_att(x, mods, gain, w, qg, kg, tables):
    nq, nk = ATT_HEADS * ATT_HEAD_DIM, ATT_KV_HEADS * ATT_HEAD_DIM
    return pl.pallas_call(
        _proj_att_kernel,
        grid=(N_TILES,),
        in_specs=[_tok_spec(D_MODEL), _const_spec((1, D_MODEL)), _mod_spec(0), _mod_spec(1),
                  _const_spec(w.shape), _const_spec((1, LANES)), _const_spec((1, LANES)),
                  _ROPE_SPEC, _ROPE_SPEC, _ROPE_SPEC],
        out_specs=[_tok_spec(nq), _tok_spec(nk), _tok_spec(nk),
                   _cache_spec(ATT_KV_HEADS, SEQ, ATT_HEAD_DIM), _cache_spec(ATT_KV_HEADS, SEQ, ATT_HEAD_DIM)],
        out_shape=[jax.ShapeDtypeStruct((N_TOK, nq), BF16),
                   jax.ShapeDtypeStruct((N_TOK, nk), BF16),
                   jax.ShapeDtypeStruct((N_TOK, nk), BF16),
                   _cache_shape(ATT_KV_HEADS, SEQ, ATT_HEAD_DIM), _cache_shape(ATT_KV_HEADS, SEQ, ATT_HEAD_DIM)],
        compiler_params=_cparams(1),
        name="proj_att",
    )(x, gain, mods, mods, w, qg, kg, *tables)


def _proj_diff_kernel(x_ref, gain_ref, sh_ref, sc_ref, w_ref, qg_ref, kg_ref,
                      cos_ref, sp_ref, sn_ref, q_ref, k_ref, v_ref, ck_ref, cv_ref):
    i = pl.program_id(0)
    is_lat = i >= N_PROMPT_TILES
    h = _norm_mod(x_ref[...], gain_ref[...], _mod_row(sh_ref, i), _mod_row(sc_ref, i)).astype(BF16)
    cos, sp, sn = cos_ref[...], sp_ref[...], sn_ref[...]
    nh = DIFF_HEADS
    cache_k, cache_v = [], []
    for part, (g_ref, o_ref) in enumerate(((qg_ref, q_ref), (kg_ref, k_ref), (None, v_ref))):
        y = _dot(h, w_ref[:, part * D_MODEL:(part + 1) * D_MODEL])
        for c in range(nh):
            yc = y[:, c * LANES:(c + 1) * LANES]
            if g_ref is not None:
                yc = yc * _rms_scale_halves(yc) * g_ref[...]
                yc = jnp.where(is_lat, _rope(yc, cos, sp, sn, DIFF_HEAD_DIM // 4), yc)
            o_ref[:, c * LANES:(c + 1) * LANES] = yc.astype(BF16)
            if part == 1:
                cache_k.append(yc)
            elif part == 2:
                cache_v.append(yc)

    @pl.when(i < N_PROMPT_TILES)
    def _():
        for hd in range(nh):
            ck_ref[0, 0, hd, 0] = cache_k[hd][:, :HALF]
            ck_ref[0, 0, hd, 1] = cache_k[hd][:, HALF:]
            cv_ref[0, 0, hd] = cache_v[hd]


def _proj_diff(x, mods, gain, w, qg, kg, tables):
    n = DIFF_HEADS * 2 * DIFF_HEAD_DIM
    return pl.pallas_call(
        _proj_diff_kernel,
        grid=(N_TILES,),
        in_specs=[_tok_spec(D_MODEL), _const_spec((1, D_MODEL)), _mod_spec(0), _mod_spec(1),
                  _const_spec(w.shape), _const_spec((1, LANES)), _const_spec((1, LANES)),
                  _ROPE_SPEC, _ROPE_SPEC, _ROPE_SPEC],
        out_specs=[_tok_spec(n), _tok_spec(n), _tok_spec(n),
                   _cache_spec(DIFF_HEADS, 2, SEQ, DIFF_HEAD_DIM), _cache_spec(DIFF_HEADS, SEQ, 2 * DIFF_HEAD_DIM)],
        out_shape=[jax.ShapeDtypeStruct((N_TOK, n), BF16)] * 3
                  + [_cache_shape(DIFF_HEADS, 2, SEQ, DIFF_HEAD_DIM),
                     _cache_shape(DIFF_HEADS, SEQ, 2 * DIFF_HEAD_DIM)],
        compiler_params=_cparams(1),
        name="proj_diff",
    )(x, gain, mods, mods, w, qg, kg, *tables)


def _dup_halves(yc):
    lo = _lane_lo(yc.shape)
    sw = pltpu.roll(yc, HALF, 1)
    return jnp.where(lo, yc, sw), jnp.where(lo, sw, yc)


def _proj_swa_kernel(x_ref, gain_ref, sh_ref, sc_ref, w_ref, qg_ref, kg_ref,
                     cos_ref, sp_ref, sn_ref, q_ref, kd_ref, vd_ref, ck_ref, cv_ref):
    i = pl.program_id(0)
    is_lat = i >= N_PROMPT_TILES
    h = _norm_mod(x_ref[...], gain_ref[...], _mod_row(sh_ref, i), _mod_row(sc_ref, i)).astype(BF16)
    y = _dot(h, w_ref[...])
    cos, sp, sn = cos_ref[...], sp_ref[...], sn_ref[...]
    nq = SWA_HEADS * SWA_HEAD_DIM // LANES
    nk = SWA_KV_HEADS * SWA_HEAD_DIM // LANES
    cache = []
    for c in range(nq + 2 * nk):
        yc = y[:, c * LANES:(c + 1) * LANES]
        if c < nq + nk:
            yc = yc * _rms_scale_halves(yc) * (qg_ref[...] if c < nq else kg_ref[...])
            yc = jnp.where(is_lat, _rope(yc, cos, sp, sn, SWA_HEAD_DIM // 4), yc)
        if c < nq:
            q_ref[:, c * LANES:(c + 1) * LANES] = yc.astype(BF16)
            continue
        j = c - nq if c < nq + nk else c - nq - nk
        c_ref, d_ref = (ck_ref, kd_ref) if c < nq + nk else (cv_ref, vd_ref)
        da, db = _dup_halves(yc)
        d_ref[:, (2 * j) * LANES:(2 * j + 1) * LANES] = da.astype(BF16)
        d_ref[:, (2 * j + 1) * LANES:(2 * j + 2) * LANES] = db.astype(BF16)
        cache.append((c_ref, 2 * j, da))
        cache.append((c_ref, 2 * j + 1, db))

    @pl.when(i < N_PROMPT_TILES)
    def _():
        for ref, hd, val in cache:
            ref[0, 0, hd] = val[:, :HALF]


def _proj_swa(x, mods, gain, w, qg, kg, tables):
    nq, nk = SWA_HEADS * SWA_HEAD_DIM, SWA_KV_HEADS * SWA_HEAD_DIM
    return pl.pallas_call(
        _proj_swa_kernel,
        grid=(N_TILES,),
        in_specs=[_tok_spec(D_MODEL), _const_spec((1, D_MODEL)), _mod_spec(0), _mod_spec(1),
                  _const_spec(w.shape), _const_spec((1, LANES)), _const_spec((1, LANES)),
                  _ROPE_SPEC, _ROPE_SPEC, _ROPE_SPEC],
        out_specs=[_tok_spec(nq), _tok_spec(2 * nk), _tok_spec(2 * nk),
                   _cache_spec(SWA_KV_HEADS, SEQ, SWA_HEAD_DIM), _cache_spec(SWA_KV_HEADS, SEQ, SWA_HEAD_DIM)],
        out_shape=[jax.ShapeDtypeStruct((N_TOK, nq), BF16),
                   jax.ShapeDtypeStruct((N_TOK, 2 * nk), BF16),
                   jax.ShapeDtypeStruct((N_TOK, 2 * nk), BF16),
                   _cache_shape(SWA_KV_HEADS, SEQ, SWA_HEAD_DIM), _cache_shape(SWA_KV_HEADS, SEQ, SWA_HEAD_DIM)],
        compiler_params=_cparams(1),
        name="proj_swa",
    )(x, gain, mods, mods, w, qg, kg, *tables)


def _proj_mla_kernel(x_ref, gain_ref, sh_ref, sc_ref, w_in_ref, qa_ref, kva_ref, w_uq_ref,
                     qg_ref, qgp_ref, cos_ref, sp_ref, sn_ref,
                     qn_ref, qp_ref, ckv_ref, kpe_ref, c_ckv_ref, c_kpe_ref):
    i = pl.program_id(0)
    is_lat = i >= N_PROMPT_TILES
    h = _norm_mod(x_ref[...], gain_ref[...], _mod_row(sh_ref, i), _mod_row(sc_ref, i)).astype(BF16)
    y = _dot(h, w_in_ref[...])
    c_q = y[:, :MLA_Q_RANK]
    c_kv = y[:, MLA_Q_RANK:MLA_Q_RANK + MLA_KV_RANK]
    kpe = y[:, MLA_Q_RANK + MLA_KV_RANK:]
    kpe_ref[...] = kpe
    ckv = c_kv * lax.rsqrt(jnp.mean(c_kv * c_kv, axis=-1, keepdims=True) + EPS) * kva_ref[...]
    ckv_ref[...] = ckv.astype(BF16)

    @pl.when(i < N_PROMPT_TILES)
    def _():
        c_ckv_ref[0, 0] = ckv
        c_kpe_ref[0, 0] = kpe[:, :MLA_ROPE]

    cq = (c_q * lax.rsqrt(jnp.mean(c_q * c_q, axis=-1, keepdims=True) + EPS) * qa_ref[...])
    q = _dot(cq.astype(BF16), w_uq_ref[...])
    cos, sp, sn = cos_ref[...], sp_ref[...], sn_ref[...]
    n_nope = MLA_HEADS * MLA_NOPE
    lo = _lane_lo((TM, LANES))
    inv_d = 1.0 / (MLA_NOPE + MLA_ROPE)
    for j in range(MLA_HEADS // 2):
        pe = q[:, n_nope + j * LANES:n_nope + (j + 1) * LANES]
        pe_sq = pe * pe
        rs = []
        for a in range(2):
            hh = 2 * j + a
            nope = q[:, hh * LANES:(hh + 1) * LANES]
            ss = (jnp.sum(nope * nope, axis=-1, keepdims=True)
                  + jnp.sum(jnp.where(lo, pe_sq, 0.0) if a == 0 else jnp.where(lo, 0.0, pe_sq),
                            axis=-1, keepdims=True))
            r = lax.rsqrt(ss * inv_d + EPS)
            rs.append(r)
            qn_ref[:, hh * LANES:(hh + 1) * LANES] = (nope * r * qg_ref[...]).astype(BF16)
        pe = pe * jnp.where(lo, rs[0], rs[1]) * qgp_ref[...]
        pe = jnp.where(is_lat, _rope(pe, cos, sp, sn, MLA_ROPE // 4), pe)
        qp_ref[:, j * LANES:(j + 1) * LANES] = pe.astype(BF16)


def _proj_mla(x, mods, gain, w_in, qa, kva, w_uq, qg, qgp, tables):
    n_nope = MLA_HEADS * MLA_NOPE
    n_pe = MLA_HEADS * MLA_ROPE
    return pl.pallas_call(
        _proj_mla_kernel,
        grid=(N_TILES,),
        in_specs=[_tok_spec(D_MODEL), _const_spec((1, D_MODEL)), _mod_spec(0), _mod_spec(1),
                  _const_spec(w_in.shape), _const_spec((1, MLA_Q_RANK)), _const_spec((1, MLA_KV_RANK)),
                  _const_spec(w_uq.shape), _const_spec((1, LANES)), _const_spec((1, LANES)),
                  _ROPE_SPEC, _ROPE_SPEC, _ROPE_SPEC],
        out_specs=[_tok_spec(n_nope), _tok_spec(n_pe), _tok_spec(MLA_KV_RANK), _tok_spec(LANES),
                   _cache_spec(SEQ, MLA_KV_RANK), _cache_spec(SEQ, MLA_ROPE)],
        out_shape=[jax.ShapeDtypeStruct((N_TOK, n_nope), BF16),
                   jax.ShapeDtypeStruct((N_TOK, n_pe), BF16),
                   jax.ShapeDtypeStruct((N_TOK, MLA_KV_RANK), BF16),
                   jax.ShapeDtypeStruct((N_TOK, LANES), F32),
                   _cache_shape(SEQ, MLA_KV_RANK), _cache_shape(SEQ, MLA_ROPE)],
        compiler_params=_cparams(1),
        name="proj_mla",
    )(x, gain, mods, mods, w_in, qa, kva, w_uq, qg, qgp, *tables)


def _mla_expand_kernel(ckv_ref, kpe_ref, w_ref, kg_ref, kgp_ref, cos_ref, sp_ref, sn_ref,
                       kn_ref, kp_ref, v_ref, *, rope_from_tile):
    i = pl.program_id(0)
    kv = _dot(ckv_ref[...].astype(BF16), w_ref[...])
    kpe = kpe_ref[...]
    lo = _lane_lo(kpe.shape)
    pe_ss = jnp.sum(jnp.where(lo, kpe * kpe, 0.0), axis=-1, keepdims=True)
    inv_d = 1.0 / (MLA_NOPE + MLA_ROPE)
    cos, sp, sn = cos_ref[...], sp_ref[...], sn_ref[...]
    for j in range(MLA_HEADS // 2):
        rs = []
        for a in range(2):
            hh = 2 * j + a
            kn = kv[:, hh * 2 * LANES:hh * 2 * LANES + LANES]
            v_ref[:, hh * LANES:(hh + 1) * LANES] = kv[:, hh * 2 * LANES + LANES:(hh + 1) * 2 * LANES].astype(BF16)
            r = lax.rsqrt((jnp.sum(kn * kn, axis=-1, keepdims=True) + pe_ss) * inv_d + EPS)
            rs.append(r)
            kn_ref[:, hh * LANES:(hh + 1) * LANES] = (kn * r * kg_ref[...]).astype(BF16)
        pe = kpe * jnp.where(lo, rs[0], rs[1]) * kgp_ref[...]
        if rope_from_tile is not None:
            pe = jnp.where(i >= rope_from_tile, _rope(pe, cos, sp, sn, MLA_ROPE // 4), pe)
        kp_ref[:, j * LANES:(j + 1) * LANES] = pe.astype(BF16)


def _mla_expand(ckv, kpe_dup, w_ukv, kg, kgp, tables, rope_from_tile):
    n = ckv.shape[0]
    n_nope = MLA_HEADS * MLA_NOPE
    n_pe = MLA_HEADS * MLA_ROPE
    return pl.pallas_call(
        functools.partial(_mla_expand_kernel, rope_from_tile=rope_from_tile),
        grid=(n // TM,),
        in_specs=[_tok_spec(MLA_KV_RANK), _tok_spec(LANES), _const_spec(w_ukv.shape),
                  _const_spec((1, LANES)), _const_spec((1, LANES)),
                  _ROPE_SPEC, _ROPE_SPEC, _ROPE_SPEC],
        out_specs=[_tok_spec(n_nope), _tok_spec(n_pe), _tok_spec(n_nope)],
        out_shape=[jax.ShapeDtypeStruct((n, n_nope), BF16),
                   jax.ShapeDtypeStruct((n, n_pe), BF16),
                   jax.ShapeDtypeStruct((n, n_nope), BF16)],
        compiler_params=_cparams(1),
        name="mla_expand",
    )(ckv, kpe_dup, w_ukv, kg, kgp, *tables)


def _prompt_spec(width):
    return pl.BlockSpec((TM, width), lambda b: (b, 0))


def _latq_spec(rows, width):
    per = DEC_SEQ // rows
    return pl.BlockSpec((rows, width), lambda b, t: (N_PROMPT_TOK // rows + b * per + t, 0))


def _latkv_spec(width):
    return pl.BlockSpec((DEC_SEQ, width), lambda b, t: (LAT_BLOCK0 + b, 0))


def _lato_spec(rows):
    per = DEC_SEQ // rows
    return pl.BlockSpec((rows, D_MODEL), lambda b, t: (b * per + t, 0))


def _att_kernel(*refs, with_ctx):
    if with_ctx:
        q_ref, k_ref, v_ref, kc_ref, vc_ref, o_ref = refs
    else:
        q_ref, k_ref, v_ref, o_ref = refs
    tq = q_ref.shape[0]
    nu = ATT_UNIT_HEADS
    per_kv = ATT_HEADS // ATT_KV_HEADS // nu

    def scores(u):
        q = jnp.concatenate([_chunk(q_ref, u * nu + g) for g in range(nu)], axis=0)
        s_list = [_dot_nt(_chunk(k_ref, u // per_kv), q)]
        if with_ctx:
            s_list.append(_dot_nt(kc_ref[u // per_kv].astype(BF16), q))
        return s_list

    def finish(u, s_list):
        values = [_chunk(v_ref, u // per_kv)]
        if with_ctx:
            values.append(vc_ref[u // per_kv].astype(BF16))
        ps, inv = _softmax2_parts(s_list)
        o = _pv(ps, values) * inv
        for g in range(nu):
            o_ref[:, (u * nu + g) * LANES:(u * nu + g + 1) * LANES] = (
                o[:, g * tq:(g + 1) * tq].T.astype(o_ref.dtype))

    _head_pipeline(ATT_HEADS // nu, scores, finish)


def _att_attend(q, k, v, cache_k, cache_v):
    nk = ATT_KV_HEADS * ATT_HEAD_DIM
    out_p = pl.pallas_call(
        functools.partial(_att_kernel, with_ctx=False),
        grid=(N_PROMPT_TILES,),
        in_specs=[_prompt_spec(D_MODEL), _prompt_spec(nk), _prompt_spec(nk)],
        out_specs=_prompt_spec(D_MODEL),
        out_shape=jax.ShapeDtypeStruct((N_PROMPT_TOK, D_MODEL), BF16),
        compiler_params=_cparams(1),
        name="att_prompt",
    )(q, k, v)
    ctx = pl.BlockSpec((None, None, ATT_KV_HEADS, PAST_LEN, LANES), lambda b, t: (b, 0, 0, 0, 0))
    out_s = pl.pallas_call(
        functools.partial(_att_kernel, with_ctx=True),
        grid=(DEC_BATCH, TILES_PER_DEC),
        in_specs=[_latq_spec(TM, D_MODEL), _latkv_spec(nk), _latkv_spec(nk), ctx, ctx],
        out_specs=_lato_spec(TM),
        out_shape=jax.ShapeDtypeStruct((N_LAT_TOK, D_MODEL), BF16),
        compiler_params=_cparams(2),
        name="att_latent",
    )(q, k, v, cache_k, cache_v)
    return out_p, out_s


def _diff_kernel(*refs, lam_init, with_ctx):
    if with_ctx:
        (q_ref, k_ref, v_ref, kc_ref, vc_ref, lq1_ref, lk1_ref, lq2_ref, lk2_ref, sub_ref, o_ref) = refs
    else:
        (q_ref, k_ref, v_ref, lq1_ref, lk1_ref, lq2_ref, lk2_ref, sub_ref, o_ref) = refs
    tq = q_ref.shape[0]
    lam = (jnp.exp(jnp.sum(lq1_ref[...] * lk1_ref[...], axis=-1, keepdims=True))
           - jnp.exp(jnp.sum(lq2_ref[...] * lk2_ref[...], axis=-1, keepdims=True)) + lam_init)
    sub = sub_ref[...] * (1.0 - lam_init)

    def scores(hd):
        q = jnp.concatenate(_split_halves(_chunk(q_ref, hd)), axis=0)
        s_list = [_dot_nt(_chunk(k_ref, hd), q)]
        if with_ctx:
            s_list.append(_dot_nt(kc_ref[hd].astype(BF16), q))
        return s_list

    def finish(hd, s_list):
        values = [_chunk(v_ref, hd)]
        if with_ctx:
            values.append(vc_ref[hd].astype(BF16))
        ps, inv = _softmax2_parts(s_list)
        c0 = inv[:, :tq]
        c1 = -lam * inv[:, tq:]
        o = _pv([p[:, :tq] * c0 + p[:, tq:] * c1 for p in ps], values)
        o = o * lax.rsqrt(jnp.mean(o * o, axis=0, keepdims=True) + EPS) * sub
        o_ref[:, hd * LANES:(hd + 1) * LANES] = o.T.astype(o_ref.dtype)

    _head_pipeline(DIFF_HEADS, scores, finish)


def _diff_attend(q, k, v, cache_k_pair, cache_v, lq1, lk1, lq2, lk2, subln, lam_init):
    small = [lq1, lk1, lq2, lk2, subln]
    small_specs = [_const_spec(s.shape) for s in small]
    out_p = pl.pallas_call(
        functools.partial(_diff_kernel, lam_init=lam_init, with_ctx=False),
        grid=(N_PROMPT_TILES,),
        in_specs=[_prompt_spec(D_MODEL)] * 3 + small_specs,
        out_specs=_prompt_spec(D_MODEL),
        out_shape=jax.ShapeDtypeStruct((N_PROMPT_TOK, D_MODEL), BF16),
        compiler_params=_cparams(1),
        name="diff_prompt",
    )(q, k, v, *small)
    out_s = pl.pallas_call(
        functools.partial(_diff_kernel, lam_init=lam_init, with_ctx=True),
        grid=(DEC_BATCH, TILES_PER_DEC),
        in_specs=[_latq_spec(TM, D_MODEL), _latkv_spec(D_MODEL), _latkv_spec(D_MODEL),
                  pl.BlockSpec((None, DIFF_HEADS, PAST_LEN, LANES), lambda b, t: (b, 0, 0, 0)),
                  pl.BlockSpec((None, None, DIFF_HEADS, PAST_LEN, LANES), lambda b, t: (b, 0, 0, 0, 0))]
                 + small_specs,
        out_specs=_lato_spec(TM),
        out_shape=jax.ShapeDtypeStruct((N_LAT_TOK, D_MODEL), BF16),
        compiler_params=_cparams(2),
        name="diff_latent",
    )(q, k, v, cache_k_pair, cache_v, *small)
    return out_p, out_s


def _swa_pipeline(q_ref, sink_ref, score_fns, value_fns, o_ref):
    tq = q_ref.shape[0]
    per_kv = SWA_HEADS // SWA_KV_HEADS // 2
    first = lax.broadcasted_iota(jnp.int32, (LANES, tq), 0) < HALF

    def scores(c):
        q = jnp.concatenate(_split_halves(_chunk(q_ref, c)), axis=0)
        return [fn(c // per_kv, q) for fn in score_fns]

    def finish(c, s_list):
        sink = jnp.concatenate([jnp.full((1, tq), sink_ref[2 * c + a] * LOG2E, F32) for a in range(2)],
                               axis=1)
        ps, inv = _softmax2_parts(s_list, extra=sink)
        o = _pv(ps, [fn(c // per_kv) for fn in value_fns]) * inv
        oc = jnp.where(first, o[:, :tq], o[:, tq:])
        o_ref[:, c * LANES:(c + 1) * LANES] = oc.T.astype(o_ref.dtype)

    _head_pipeline(SWA_HEADS // 2, scores, finish)


def _swa_prompt_kernel(sink_ref, q_ref, k_ref, v_ref, o_ref):
    _swa_pipeline(q_ref, sink_ref, [lambda kv, q: _dot_nt(_chunk(k_ref, kv), q)],
                  [lambda kv: _chunk(v_ref, kv)], o_ref)


def _swa_latent_kernel(sink_ref, q_ref, k_ref, v_ref, kc_ref, vc_ref, o_ref):
    n = pl.program_id(1)
    tq = q_ref.shape[0]
    span = 3 * SWA_QB
    start = pl.multiple_of(jnp.clip((n - 1) * SWA_QB, 0, DEC_SEQ - span), SWA_QB)
    cols = lax.broadcasted_iota(jnp.int32, (span, 2 * tq), 1)
    qpos = n * SWA_QB + jnp.bitwise_and(cols, tq - 1)
    kpos = start + lax.broadcasted_iota(jnp.int32, (span, 2 * tq), 0)
    valid = jnp.abs(qpos - kpos) <= WINDOW

    def local(ref, kv):
        return ref[pl.ds(start, span), kv * LANES:(kv + 1) * LANES]

    _swa_pipeline(q_ref, sink_ref,
                  [lambda kv, q: jnp.where(valid, _dot_nt(local(k_ref, kv), q), -1e30),
                   lambda kv, q: _dot_nt(kc_ref[kv], q)],
                  [lambda kv: local(v_ref, kv), lambda kv: vc_ref[kv]], o_ref)


def _swa_attend(q, kd, vd, cache_kd, cache_vd, sink):
    nkd = 2 * SWA_KV_HEADS * SWA_HEAD_DIM
    smem = pl.BlockSpec(memory_space=pltpu.SMEM)
    out_p = pl.pallas_call(
        _swa_prompt_kernel,
        grid=(N_PROMPT_TILES,),
        in_specs=[smem, _prompt_spec(D_MODEL), _prompt_spec(nkd), _prompt_spec(nkd)],
        out_specs=_prompt_spec(D_MODEL),
        out_shape=jax.ShapeDtypeStruct((N_PROMPT_TOK, D_MODEL), BF16),
        compiler_params=_cparams(1),
        name="swa_prompt",
    )(sink, q, kd, vd)
    ctx = pl.BlockSpec((None, SWA_KV_HEADS, PAST_LEN, LANES), lambda b, n: (b, 0, 0, 0))
    out_s = pl.pallas_call(
        _swa_latent_kernel,
        grid=(DEC_BATCH, DEC_SEQ // SWA_QB),
        in_specs=[smem, _latq_spec(SWA_QB, D_MODEL), _latkv_spec(nkd), _latkv_spec(nkd), ctx, ctx],
        out_specs=_lato_spec(SWA_QB),
        out_shape=jax.ShapeDtypeStruct((N_LAT_TOK, D_MODEL), BF16),
        compiler_params=_cparams(2),
        name="swa_latent",
    )(sink, q, kd, vd, cache_kd, cache_vd)
    return out_p, out_s


def _mla_kernel(*refs, with_ctx):
    if with_ctx:
        (qn_ref, qp_ref, kn_ref, kp_ref, v_ref, knc_ref, kpc_ref, vc_ref, o_ref) = refs
    else:
        (qn_ref, qp_ref, kn_ref, kp_ref, v_ref, o_ref) = refs

    def scores(hd):
        j, a = hd // 2, hd % 2
        q = jnp.concatenate([_chunk(qn_ref, hd), _split_halves(_chunk(qp_ref, j))[a]], axis=1)
        s_list = [_dot_nt(jnp.concatenate([_chunk(kn_ref, hd), _chunk(kp_ref, j)], axis=1), q)]
        if with_ctx:
            s_list.append(_dot_nt(jnp.concatenate([_chunk(knc_ref, hd), _chunk(kpc_ref, j)], axis=1), q))
        return s_list

    def finish(hd, s_list):
        values = [_chunk(v_ref, hd)]
        if with_ctx:
            values.append(_chunk(vc_ref, hd))
        ps, inv = _softmax2_parts(s_list)
        o_ref[:, hd * LANES:(hd + 1) * LANES] = (_pv(ps, values) * inv).T.astype(o_ref.dtype)

    _head_pipeline(MLA_HEADS, scores, finish)


def _mla_attend(qn, qp, kn, kp, v, knc, kpc, vc):
    n_pe = MLA_HEADS * MLA_ROPE
    out_p = pl.pallas_call(
        functools.partial(_mla_kernel, with_ctx=False),
        grid=(N_PROMPT_TILES,),
        in_specs=[_prompt_spec(D_MODEL), _prompt_spec(n_pe), _prompt_spec(D_MODEL), _prompt_spec(n_pe),
                  _prompt_spec(D_MODEL)],
        out_specs=_prompt_spec(D_MODEL),
        out_shape=jax.ShapeDtypeStruct((N_PROMPT_TOK, D_MODEL), BF16),
        compiler_params=_cparams(1),
        name="mla_prompt",
    )(qn, qp, kn, kp, v)

    def ctx(width):
        return pl.BlockSpec((PAST_LEN, width), lambda b, t: (b, 0))

    out_s = pl.pallas_call(
        functools.partial(_mla_kernel, with_ctx=True),
        grid=(DEC_BATCH, TILES_PER_DEC),
        in_specs=[_latq_spec(TM, D_MODEL), _latq_spec(TM, n_pe),
                  _latkv_spec(D_MODEL), _latkv_spec(n_pe), _latkv_spec(D_MODEL),
                  ctx(D_MODEL), ctx(n_pe), ctx(D_MODEL)],
        out_specs=_lato_spec(TM),
        out_shape=jax.ShapeDtypeStruct((N_LAT_TOK, D_MODEL), BF16),
        compiler_params=_cparams(2),
        name="mla_latent",
    )(qn, qp, kn, kp, v, knc, kpc, vc)
    return out_p, out_s


def _oproj_kernel(ap_ref, as_ref, w_ref, x_ref, g1_ref, gain_ref, sh_ref, sc_ref, x1_ref, h2_ref):
    i = pl.program_id(0)
    a = jnp.where(i < N_PROMPT_TILES, ap_ref[...], as_ref[...])
    o = _dot(a, w_ref[...])
    x1 = x_ref[...] + _mod_row(g1_ref, i) * o
    x1_ref[...] = x1
    h2_ref[...] = _norm_mod(x1, gain_ref[...], _mod_row(sh_ref, i), _mod_row(sc_ref, i)).astype(BF16)


def _oproj(attn_p, attn_s, w_o, x, mods, gain_ffn):
    return pl.pallas_call(
        _oproj_kernel,
        grid=(N_TILES,),
        in_specs=[pl.BlockSpec((TM, D_MODEL), lambda i: (jnp.minimum(i, N_PROMPT_TILES - 1), 0)),
                  pl.BlockSpec((TM, D_MODEL), lambda i: (jnp.maximum(i - N_PROMPT_TILES, 0), 0)),
                  _const_spec(w_o.shape), _tok_spec(D_MODEL), _mod_spec(2),
                  _const_spec((1, D_MODEL)), _mod_spec(3), _mod_spec(4)],
        out_specs=[_tok_spec(D_MODEL), _tok_spec(D_MODEL)],
        out_shape=[jax.ShapeDtypeStruct((N_TOK, D_MODEL), F32),
                   jax.ShapeDtypeStruct((N_TOK, D_MODEL), BF16)],
        compiler_params=_cparams(1),
        name="oproj",
    )(attn_p, attn_s, w_o, x, mods, gain_ffn, mods, mods)


def _mlp_kernel(h_ref, w1_ref, w2_ref, x_ref, g2_ref, o_ref):
    i = pl.program_id(0)
    grp = _tile_group(i * (MLP_TM // TM))
    h = h_ref[...]
    acc = None
    for c in range(D_FF // MLP_FF_CHUNK):
        u = _dot(h, w1_ref[:, c * MLP_FF_CHUNK:(c + 1) * MLP_FF_CHUNK])
        u = jnp.square(jnp.maximum(u, 0.0)).astype(BF16)
        t = _dot(u, w2_ref[c * MLP_FF_CHUNK:(c + 1) * MLP_FF_CHUNK, :])
        acc = t if acc is None else acc + t
    o_ref[...] = x_ref[...] + g2_ref[pl.ds(grp, 1), :] * acc


def _mlp(h2, w1, w2, x1, mods):
    spec = pl.BlockSpec((MLP_TM, D_MODEL), lambda i: (i, 0))
    return pl.pallas_call(
        _mlp_kernel,
        grid=(N_TOK // MLP_TM,),
        in_specs=[spec, _const_spec(w1.shape), _const_spec(w2.shape), spec, _mod_spec(5)],
        out_specs=spec,
        out_shape=jax.ShapeDtypeStruct((N_TOK, D_MODEL), F32),
        compiler_params=_cparams(1),
        name="mlp",
    )(h2, w1, w2, x1, mods)


def _row(v, scale=1.0):
    return (v.astype(F32) * scale).reshape(1, -1)


def _pair(v, scale=1.0):
    return (jnp.concatenate([v, v]).astype(F32) * scale).reshape(1, LANES)


def kernel(x_prompt, x_sample, cache_att_k, cache_att_v, cache_diff_k, cache_diff_v, cache_swa_k, cache_swa_v, cache_mla_ckv, cache_mla_kpe, c, c_ctx, ada_w, ada_b, norm_mix, norm_ffn, att_w_qkv, att_q_norm, att_k_norm, att_w_o, diff_w_qkv, diff_q_norm, diff_k_norm, diff_lq1, diff_lk1, diff_lq2, diff_lk2, diff_subln, diff_w_o, swa_w_qkv, swa_q_norm, swa_k_norm, swa_sink, swa_w_o, mla_w_in, mla_q_a_norm, mla_kv_a_norm, mla_w_uq, mla_w_ukv, mla_q_norm, mla_k_norm, mla_w_o, mlp_w1, mlp_w2):
    x = jnp.concatenate([x_prompt.reshape(N_PROMPT_TOK, D_MODEL),
                         x_sample.reshape(N_LAT_TOK, D_MODEL)], axis=0)
    cond = jnp.concatenate([c_ctx[None], c, jnp.zeros((COND_ROWS - 1 - DEC_BATCH, D_MODEL), F32)], axis=0)
    mods_all = _modulation(cond, ada_w, ada_b)

    tab_att = _rope_tables(ATT_HEAD_DIM)
    tab_64 = _rope_tables(DIFF_HEAD_DIM)

    outs = {}
    for layer in range(DEPTH):
        mods = mods_all[layer]
        gain_mix = _row(norm_mix[layer])
        gain_ffn = _row(norm_ffn[layer])
        if layer == 0:
            qs = ATT_HEAD_DIM ** -0.5 * LOG2E
            q, k, v, outs["att_k"], outs["att_v"] = _proj_att(
                x, mods, gain_mix, att_w_qkv[0].astype(BF16),
                _row(att_q_norm[0], qs), _row(att_k_norm[0]), tab_att)
            attn_p, attn_s = _att_attend(q, k, v, cache_att_k, cache_att_v)
            w_o = att_w_o[0]
        elif layer == 1:
            qs = DIFF_HEAD_DIM ** -0.5 * LOG2E
            q, k, v, outs["diff_k"], outs["diff_v"] = _proj_diff(
                x, mods, gain_mix, diff_w_qkv[0].astype(BF16),
                _pair(diff_q_norm[0], qs), _pair(diff_k_norm[0]), tab_64)
            lam_init = 0.8 - 0.6 * math.exp(-0.3 * layer)
            ck = cache_diff_k[:, 0].transpose(0, 1, 3, 2, 4).reshape(
                DEC_BATCH, DIFF_HEADS, PAST_LEN, LANES)
            attn_p, attn_s = _diff_attend(q, k, v, ck, cache_diff_v,
                                          _row(diff_lq1[0]), _row(diff_lk1[0]),
                                          _row(diff_lq2[0]), _row(diff_lk2[0]),
                                          diff_subln[0].astype(F32).reshape(LANES, 1), lam_init)
            w_o = diff_w_o[0]
        elif layer == 2:
            qs = SWA_HEAD_DIM ** -0.5 * LOG2E
            q, kd, vd, outs["swa_k"], outs["swa_v"] = _proj_swa(
                x, mods, gain_mix, swa_w_qkv[0].astype(BF16),
                _pair(swa_q_norm[0], qs), _pair(swa_k_norm[0]), tab_64)
            ckd = jnp.concatenate([cache_swa_k[:, 0]] * 2, axis=-1).astype(BF16)
            cvd = jnp.concatenate([cache_swa_v[:, 0]] * 2, axis=-1).astype(BF16)
            attn_p, attn_s = _swa_attend(q, kd, vd, ckd, cvd, swa_sink[0].astype(F32))
            w_o = swa_w_o[0]
        else:
            qs = (MLA_NOPE + MLA_ROPE) ** -0.5 * LOG2E
            w_in = mla_w_in[0]
            w_in = jnp.concatenate([w_in, w_in[:, -MLA_ROPE:]], axis=1).astype(BF16)
            w_uq = mla_w_uq[0].reshape(MLA_Q_RANK, MLA_HEADS, MLA_NOPE + MLA_ROPE)
            w_uq = jnp.concatenate([w_uq[:, :, :MLA_NOPE].reshape(MLA_Q_RANK, -1),
                                    w_uq[:, :, MLA_NOPE:].reshape(MLA_Q_RANK, -1)], axis=1).astype(BF16)
            w_ukv = mla_w_ukv[0].astype(BF16)
            qg, kg = mla_q_norm[0], mla_k_norm[0]
            qn, qp, ckv, kpe, outs["mla_ckv"], outs["mla_kpe"] = _proj_mla(
                x, mods, gain_mix, w_in, _row(mla_q_a_norm[0]), _row(mla_kv_a_norm[0]), w_uq,
                _row(qg[:MLA_NOPE], qs), _pair(qg[MLA_NOPE:], qs), tab_64)
            kn, kp, vv = _mla_expand(ckv, kpe, w_ukv, _row(kg[:MLA_NOPE]), _pair(kg[MLA_NOPE:]),
                                     tab_64, N_PROMPT_TILES)
            c_ckv = cache_mla_ckv[:, 0].reshape(DEC_BATCH * PAST_LEN, MLA_KV_RANK)
            c_kpe = cache_mla_kpe[:, 0].reshape(DEC_BATCH * PAST_LEN, MLA_ROPE)
            c_kpe = jnp.concatenate([c_kpe, c_kpe], axis=-1)
            knc, kpc, vc = _mla_expand(c_ckv, c_kpe, w_ukv, _row(kg[:MLA_NOPE]), _pair(kg[MLA_NOPE:]),
                                       tab_64, None)
            attn_p, attn_s = _mla_attend(qn, qp, kn, kp, vv, knc, kpc, vc)
            w_o = mla_w_o[0]
        x1, h2 = _oproj(attn_p, attn_s, w_o.astype(BF16), x, mods, gain_ffn)
        x = _mlp(h2, mlp_w1[layer].astype(BF16), mlp_w2[layer].astype(BF16), x1, mods)

    y_prompt = x[:N_PROMPT_TOK].reshape(BATCH, SEQ, D_MODEL)
    y_sample = x[N_PROMPT_TOK:].reshape(DEC_BATCH, DEC_SEQ, D_MODEL)
    return (y_prompt, y_sample, outs["att_k"], outs["att_v"], outs["diff_k"], outs["diff_v"],
            outs["swa_k"], outs["swa_v"], outs["mla_ckv"], outs["mla_kpe"])
```

```python
import functools
import math

import numpy as np
import jax
import jax.numpy as jnp
from jax import lax
from jax.experimental import pallas as pl
from jax.experimental.pallas import tpu as pltpu

D_MODEL = 1024
BATCH = 16
SEQ = 256
DEPTH = 4
DEC_BATCH = 2
DEC_SEQ = 1024
PAST_LEN = 256
GRID_W = 64
ROPE_THETA = 10000.0
EPS = 1e-6
D_FF = 4 * D_MODEL
MOD_CHUNKS = 6
LOG2E = 1.4426950408889634

ATT_HEADS, ATT_KV_HEADS, ATT_HEAD_DIM = 8, 2, 128
DIFF_HEADS, DIFF_HEAD_DIM = 8, 64
SWA_HEADS, SWA_KV_HEADS, SWA_HEAD_DIM, WINDOW = 16, 4, 64, 128
MLA_HEADS, MLA_NOPE, MLA_ROPE, MLA_VDIM = 8, 128, 64, 128
MLA_Q_RANK, MLA_KV_RANK = 512, 256

LANES = 128
HALF = LANES // 2
TM = 256
N_PROMPT_TOK = BATCH * SEQ
N_LAT_TOK = DEC_BATCH * DEC_SEQ
N_TOK = N_PROMPT_TOK + N_LAT_TOK
N_TILES = N_TOK // TM
N_PROMPT_TILES = N_PROMPT_TOK // TM
TILES_PER_DEC = DEC_SEQ // TM
LAT_BLOCK0 = N_PROMPT_TOK // DEC_SEQ
COND_ROWS = 8
MLP_TM = 512
MLP_FF_CHUNK = 1024
MLP_LOAD_COLS = 512
N_W_LOADS = D_FF // MLP_LOAD_COLS
N_LOAD_STEPS = 2 * N_W_LOADS
N_MLP_PROMPT_TILES = N_PROMPT_TOK // MLP_TM
SWA_QB = 128
ATT_UNIT_HEADS = 4
VMEM_LIMIT = 56 * 1024 * 1024

F32 = jnp.float32
BF16 = jnp.bfloat16


def _cparams(n_axes):
    return pltpu.CompilerParams(dimension_semantics=("arbitrary",) * n_axes,
                                vmem_limit_bytes=VMEM_LIMIT)


def _dot(a, b):
    return jnp.dot(a, b, preferred_element_type=F32)


def _dot_nt(a, b):
    return lax.dot_general(a, b, (((1,), (1,)), ((), ())), preferred_element_type=F32)


def _const_spec(shape):
    nd = len(shape)
    return pl.BlockSpec(shape, lambda *_: (0,) * nd, pipeline_mode=pl.Buffered(1))


def _chunk(ref, c, width=LANES):
    return ref[:, c * width:(c + 1) * width]


def _tile_group(i):
    return jnp.where(i < N_PROMPT_TILES, 0, 1 + (i - N_PROMPT_TILES) // TILES_PER_DEC)


def _rope_tile(i):
    return jnp.maximum(i - N_PROMPT_TILES, 0) % TILES_PER_DEC


def _norm_mod(x, gain, shift, scale):
    ms = jnp.mean(x * x, axis=-1, keepdims=True)
    y = x * lax.rsqrt(ms + EPS) * gain
    return y * (1.0 + scale) + shift


def _lane_lo(shape):
    return lax.broadcasted_iota(jnp.int32, shape, len(shape) - 1) < HALF


def _rms_scale(y):
    return lax.rsqrt(jnp.mean(y * y, axis=-1, keepdims=True) + EPS)


def _rms_scale_halves(y):
    lo = _lane_lo(y.shape)
    sq = y * y
    s_lo = jnp.sum(jnp.where(lo, sq, 0.0), axis=-1, keepdims=True)
    s_hi = jnp.sum(jnp.where(lo, 0.0, sq), axis=-1, keepdims=True)
    return jnp.where(lo, lax.rsqrt(s_lo * (1.0 / HALF) + EPS), lax.rsqrt(s_hi * (1.0 / HALF) + EPS))


def _rope(y, cos, sin_prev, sin_next, quarter):
    return (y * cos + pltpu.roll(y, quarter, 1) * sin_prev
            + pltpu.roll(y, LANES - quarter, 1) * sin_next)


def _rope_tables(rot_dim):
    half = rot_dim // 2
    quarter = rot_dim // 4
    inv = np.float32(ROPE_THETA) ** (-np.arange(0, half, 2, dtype=np.float32) / np.float32(half))
    pos = np.arange(DEC_SEQ)
    row = (pos // GRID_W).astype(np.float32)
    col = (pos % GRID_W).astype(np.float32)
    lane = np.arange(LANES)
    dd = lane % rot_dim
    q = dd // quarter
    f = dd % quarter
    ang = np.where((q < 2)[None, :], row[:, None], col[:, None]) * inv[f][None, :]
    ang = ang.astype(np.float32)
    cos = np.cos(ang).astype(np.float32)
    sin = np.sin(ang).astype(np.float32)
    odd = (q % 2 == 1)[None, :]
    sin_prev = np.where(odd, sin, 0.0).astype(np.float32)
    sin_next = np.where(odd, 0.0, -sin).astype(np.float32)
    return jnp.asarray(cos), jnp.asarray(sin_prev), jnp.asarray(sin_next)


def _softmax2_parts(s_list, extra=None):
    m = jnp.max(s_list[0], axis=0, keepdims=True)
    for s in s_list[1:]:
        m = jnp.maximum(m, jnp.max(s, axis=0, keepdims=True))
    if extra is not None:
        m = jnp.maximum(m, extra)
    ps = [jnp.exp2(s - m) for s in s_list]
    tot = ps[0].sum(axis=0, keepdims=True)
    for p in ps[1:]:
        tot = tot + p.sum(axis=0, keepdims=True)
    if extra is not None:
        tot = tot + jnp.exp2(extra - m)
    return ps, 1.0 / tot


def _dot_tn(a, b):
    return lax.dot_general(a, b, (((0,), (0,)), ((), ())), preferred_element_type=F32)


def _head_pipeline(n, scores, finish):
    nxt = scores(0)
    for h in range(n):
        cur = nxt
        if h + 1 < n:
            nxt = scores(h + 1)
        finish(h, cur)


def _pv(ps, values):
    o = None
    for p, v in zip(ps, values):
        t = _dot_tn(v, p.astype(BF16))
        o = t if o is None else o + t
    return o


def _split_halves(q):
    lo = _lane_lo(q.shape)
    zero = jnp.zeros_like(q)
    return jnp.where(lo, q, zero), jnp.where(lo, zero, q)


def _mod_kernel(cond_ref, w_ref, b_ref, o_ref):
    c = cond_ref[...]
    s = (c * jax.nn.sigmoid(c)).astype(BF16)
    o_ref[0] = _dot(s, w_ref[0].astype(BF16)) + b_ref[0]


def _modulation(cond, ada_w, ada_b):
    tn = 1536
    n = MOD_CHUNKS * D_MODEL
    return pl.pallas_call(
        _mod_kernel,
        grid=(DEPTH, n // tn),
        in_specs=[
            pl.BlockSpec((COND_ROWS, D_MODEL), lambda l, j: (0, 0)),
            pl.BlockSpec((1, D_MODEL, tn), lambda l, j: (l, 0, j)),
            pl.BlockSpec((1, 1, tn), lambda l, j: (l, 0, j)),
        ],
        out_specs=pl.BlockSpec((1, COND_ROWS, tn), lambda l, j: (l, 0, j)),
        out_shape=jax.ShapeDtypeStruct((DEPTH, COND_ROWS, n), F32),
        compiler_params=_cparams(2),
        name="modulation",
    )(cond, ada_w, ada_b.reshape(DEPTH, 1, n))


def _mod_spec(chunk):
    return pl.BlockSpec((COND_ROWS, D_MODEL), lambda i: (0, chunk))


def _mod_row(ref, i):
    return ref[pl.ds(_tile_group(i), 1), :]


_ROPE_SPEC = pl.BlockSpec((TM, LANES), lambda i: (_rope_tile(i), 0))


def _tok_spec(width):
    return pl.BlockSpec((TM, width), lambda i: (i, 0))


_XP_SPEC = pl.BlockSpec((TM, D_MODEL), lambda i: (jnp.minimum(i, N_PROMPT_TILES - 1), 0))
_XS_SPEC = pl.BlockSpec((TM, D_MODEL), lambda i: (jnp.maximum(i - N_PROMPT_TILES, 0), 0))


def _x_tile(xp_ref, xs_ref, i):
    return jnp.where(i < N_PROMPT_TILES, xp_ref[...], xs_ref[...])


def _cache_spec(*dims):
    nd = len(dims)
    return pl.BlockSpec((1, 1) + dims,
                        lambda i: (jnp.minimum(i, N_PROMPT_TILES - 1), 0) + (0,) * nd)


def _cache_shape(*dims):
    return jax.ShapeDtypeStruct((BATCH, 1) + dims, F32)


def _proj_att_kernel(xp_ref, xs_ref, gain_ref, sh_ref, sc_ref, w_ref, qg_ref, kg_ref,
                     cos_ref, sp_ref, sn_ref, q_ref, k_ref, v_ref, ck_ref, cv_ref):
    i = pl.program_id(0)
    is_lat = i >= N_PROMPT_TILES
    h = _norm_mod(_x_tile(xp_ref, xs_ref, i), gain_ref[...], _mod_row(sh_ref, i), _mod_row(sc_ref, i)).astype(BF16)
    y = _dot(h, w_ref[...])
    cos, sp, sn = cos_ref[...], sp_ref[...], sn_ref[...]
    nq, nk = ATT_HEADS, ATT_KV_HEADS
    cache = []
    for c in range(nq + 2 * nk):
        yc = y[:, c * LANES:(c + 1) * LANES]
        if c < nq + nk:
            yc = yc * _rms_scale(yc) * (qg_ref[...] if c < nq else kg_ref[...])
            yc = jnp.where(is_lat, _rope(yc, cos, sp, sn, ATT_HEAD_DIM // 4), yc)
        if c < nq:
            q_ref[:, c * LANES:(c + 1) * LANES] = yc.astype(BF16)
        elif c < nq + nk:
            k_ref[:, (c - nq) * LANES:(c - nq + 1) * LANES] = yc.astype(BF16)
            cache.append((ck_ref, c - nq, yc))
        else:
            v_ref[:, (c - nq - nk) * LANES:(c - nq - nk + 1) * LANES] = yc.astype(BF16)
            cache.append((cv_ref, c - nq - nk, yc))

    @pl.when(i < N_PROMPT_TILES)
    def _():
        for ref, hd, val in cache:
            ref[0, 0, hd] = val


def _proj_att(xp, xs, mods, gain, w, qg, kg, tables):
    nq, nk = ATT_HEADS * ATT_HEAD_DIM, ATT_KV_HEADS * ATT_HEAD_DIM
    return pl.pallas_call(
        _proj_att_kernel,
        grid=(N_TILES,),
        in_specs=[_XP_SPEC, _XS_SPEC, _const_spec((1, D_MODEL)), _mod_spec(0), _mod_spec(1),
                  _const_spec(w.shape), _const_spec((1, LANES)), _const_spec((1, LANES)),
                  _ROPE_SPEC, _ROPE_SPEC, _ROPE_SPEC],
        out_specs=[_tok_spec(nq), _tok_spec(nk), _tok_spec(nk),
                   _cache_spec(ATT_KV_HEADS, SEQ, ATT_HEAD_DIM), _cache_spec(ATT_KV_HEADS, SEQ, ATT_HEAD_DIM)],
        out_shape=[jax.ShapeDtypeStruct((N_TOK, nq), BF16),
                   jax.ShapeDtypeStruct((N_TOK, nk), BF16),
                   jax.ShapeDtypeStruct((N_TOK, nk), BF16),
                   _cache_shape(ATT_KV_HEADS, SEQ, ATT_HEAD_DIM), _cache_shape(ATT_KV_HEADS, SEQ, ATT_HEAD_DIM)],
        compiler_params=_cparams(1),
        name="proj_att",
    )(xp, xs, gain, mods, mods, w, qg, kg, *tables)


def _proj_diff_kernel(xp_ref, xs_ref, gain_ref, sh_ref, sc_ref, w_ref, qg_ref, kg_ref,
                      cos_ref, sp_ref, sn_ref, q_ref, k_ref, v_ref, ck_ref, cv_ref):
    i = pl.program_id(0)
    is_lat = i >= N_PROMPT_TILES
    h = _norm_mod(_x_tile(xp_ref, xs_ref, i), gain_ref[...], _mod_row(sh_ref, i), _mod_row(sc_ref, i)).astype(BF16)
    cos, sp, sn = cos_ref[...], sp_ref[...], sn_ref[...]
    nh = DIFF_HEADS
    cache_k, cache_v = [], []
    for part, (g_ref, o_ref) in enumerate(((qg_ref, q_ref), (kg_ref, k_ref), (None, v_ref))):
        y = _dot(h, w_ref[:, part * D_MODEL:(part + 1) * D_MODEL])
        for c in range(nh):
            yc = y[:, c * LANES:(c + 1) * LANES]
            if g_ref is not None:
                yc = yc * _rms_scale_halves(yc) * g_ref[...]
                yc = jnp.where(is_lat, _rope(yc, cos, sp, sn, DIFF_HEAD_DIM // 4), yc)
            o_ref[:, c * LANES:(c + 1) * LANES] = yc.astype(BF16)
            if part == 1:
                cache_k.append(yc)
            elif part == 2:
                cache_v.append(yc)

    @pl.when(i < N_PROMPT_TILES)
    def _():
        for hd in range(nh):
            ck_ref[0, 0, hd, 0] = cache_k[hd][:, :HALF]
            ck_ref[0, 0, hd, 1] = cache_k[hd][:, HALF:]
            cv_ref[0, 0, hd] = cache_v[hd]


def _proj_diff(xp, xs, mods, gain, w, qg, kg, tables):
    n = DIFF_HEADS * 2 * DIFF_HEAD_DIM
    return pl.pallas_call(
        _proj_diff_kernel,
        grid=(N_TILES,),
        in_specs=[_XP_SPEC, _XS_SPEC, _const_spec((1, D_MODEL)), _mod_spec(0), _mod_spec(1),
                  _const_spec(w.shape), _const_spec((1, LANES)), _const_spec((1, LANES)),
                  _ROPE_SPEC, _ROPE_SPEC, _ROPE_SPEC],
        out_specs=[_tok_spec(n), _tok_spec(n), _tok_spec(n),
                   _cache_spec(DIFF_HEADS, 2, SEQ, DIFF_HEAD_DIM), _cache_spec(DIFF_HEADS, SEQ, 2 * DIFF_HEAD_DIM)],
        out_shape=[jax.ShapeDtypeStruct((N_TOK, n), BF16)] * 3
                  + [_cache_shape(DIFF_HEADS, 2, SEQ, DIFF_HEAD_DIM),
                     _cache_shape(DIFF_HEADS, SEQ, 2 * DIFF_HEAD_DIM)],
        compiler_params=_cparams(1),
        name="proj_diff",
    )(xp, xs, gain, mods, mods, w, qg, kg, *tables)


def _dup_halves(yc):
    lo = _lane_lo(yc.shape)
    sw = pltpu.roll(yc, HALF, 1)
    return jnp.where(lo, yc, sw), jnp.where(lo, sw, yc)


def _proj_swa_kernel(xp_ref, xs_ref, gain_ref, sh_ref, sc_ref, w_ref, qg_ref, kg_ref,
                     cos_ref, sp_ref, sn_ref, q_ref, kd_ref, vd_ref, ck_ref, cv_ref):
    i = pl.program_id(0)
    is_lat = i >= N_PROMPT_TILES
    h = _norm_mod(_x_tile(xp_ref, xs_ref, i), gain_ref[...], _mod_row(sh_ref, i), _mod_row(sc_ref, i)).astype(BF16)
    y = _dot(h, w_ref[...])
    cos, sp, sn = cos_ref[...], sp_ref[...], sn_ref[...]
    nq = SWA_HEADS * SWA_HEAD_DIM // LANES
    nk = SWA_KV_HEADS * SWA_HEAD_DIM // LANES
    cache = []
    for c in range(nq + 2 * nk):
        yc = y[:, c * LANES:(c + 1) * LANES]
        if c < nq + nk:
            yc = yc * _rms_scale_halves(yc) * (qg_ref[...] if c < nq else kg_ref[...])
            yc = jnp.where(is_lat, _rope(yc, cos, sp, sn, SWA_HEAD_DIM // 4), yc)
        if c < nq:
            q_ref[:, c * LANES:(c + 1) * LANES] = yc.astype(BF16)
            continue
        j = c - nq if c < nq + nk else c - nq - nk
        c_ref, d_ref = (ck_ref, kd_ref) if c < nq + nk else (cv_ref, vd_ref)
        da, db = _dup_halves(yc)
        d_ref[:, (2 * j) * LANES:(2 * j + 1) * LANES] = da.astype(BF16)
        d_ref[:, (2 * j + 1) * LANES:(2 * j + 2) * LANES] = db.astype(BF16)
        cache.append((c_ref, 2 * j, da))
        cache.append((c_ref, 2 * j + 1, db))

    @pl.when(i < N_PROMPT_TILES)
    def _():
        for ref, hd, val in cache:
            ref[0, 0, hd] = val[:, :HALF]


def _proj_swa(xp, xs, mods, gain, w, qg, kg, tables):
    nq, nk = SWA_HEADS * SWA_HEAD_DIM, SWA_KV_HEADS * SWA_HEAD_DIM
    return pl.pallas_call(
        _proj_swa_kernel,
        grid=(N_TILES,),
        in_specs=[_XP_SPEC, _XS_SPEC, _const_spec((1, D_MODEL)), _mod_spec(0), _mod_spec(1),
                  _const_spec(w.shape), _const_spec((1, LANES)), _const_spec((1, LANES)),
                  _ROPE_SPEC, _ROPE_SPEC, _ROPE_SPEC],
        out_specs=[_tok_spec(nq), _tok_spec(2 * nk), _tok_spec(2 * nk),
                   _cache_spec(SWA_KV_HEADS, SEQ, SWA_HEAD_DIM), _cache_spec(SWA_KV_HEADS, SEQ, SWA_HEAD_DIM)],
        out_shape=[jax.ShapeDtypeStruct((N_TOK, nq), BF16),
                   jax.ShapeDtypeStruct((N_TOK, 2 * nk), BF16),
                   jax.ShapeDtypeStruct((N_TOK, 2 * nk), BF16),
                   _cache_shape(SWA_KV_HEADS, SEQ, SWA_HEAD_DIM), _cache_shape(SWA_KV_HEADS, SEQ, SWA_HEAD_DIM)],
        compiler_params=_cparams(1),
        name="proj_swa",
    )(xp, xs, gain, mods, mods, w, qg, kg, *tables)


def _proj_mla_kernel(xp_ref, xs_ref, gain_ref, sh_ref, sc_ref, w_in_ref, qa_ref, kva_ref, w_uq_ref,
                     qg_ref, qgp_ref, cos_ref, sp_ref, sn_ref,
                     qn_ref, qp_ref, ckv_ref, kpe_ref, c_ckv_ref, c_kpe_ref):
    i = pl.program_id(0)
    is_lat = i >= N_PROMPT_TILES
    h = _norm_mod(_x_tile(xp_ref, xs_ref, i), gain_ref[...], _mod_row(sh_ref, i), _mod_row(sc_ref, i)).astype(BF16)
    y = _dot(h, w_in_ref[...])
    c_q = y[:, :MLA_Q_RANK]
    c_kv = y[:, MLA_Q_RANK:MLA_Q_RANK + MLA_KV_RANK]
    kpe = y[:, MLA_Q_RANK + MLA_KV_RANK:]
    kpe_ref[...] = kpe
    ckv = c_kv * lax.rsqrt(jnp.mean(c_kv * c_kv, axis=-1, keepdims=True) + EPS) * kva_ref[...]
    ckv_ref[...] = ckv.astype(BF16)

    @pl.when(i < N_PROMPT_TILES)
    def _():
        c_ckv_ref[0, 0] = ckv
        c_kpe_ref[0, 0] = kpe[:, :MLA_ROPE]

    cq = (c_q * lax.rsqrt(jnp.mean(c_q * c_q, axis=-1, keepdims=True) + EPS) * qa_ref[...])
    q = _dot(cq.astype(BF16), w_uq_ref[...])
    cos, sp, sn = cos_ref[...], sp_ref[...], sn_ref[...]
    n_nope = MLA_HEADS * MLA_NOPE
    lo = _lane_lo((TM, LANES))
    inv_d = 1.0 / (MLA_NOPE + MLA_ROPE)
    for j in range(MLA_HEADS // 2):
        pe = q[:, n_nope + j * LANES:n_nope + (j + 1) * LANES]
        pe_sq = pe * pe
        rs = []
        for a in range(2):
            hh = 2 * j + a
            nope = q[:, hh * LANES:(hh + 1) * LANES]
            ss = (jnp.sum(nope * nope, axis=-1, keepdims=True)
                  + jnp.sum(jnp.where(lo, pe_sq, 0.0) if a == 0 else jnp.where(lo, 0.0, pe_sq),
                            axis=-1, keepdims=True))
            r = lax.rsqrt(ss * inv_d + EPS)
            rs.append(r)
            qn_ref[:, hh * LANES:(hh + 1) * LANES] = (nope * r * qg_ref[...]).astype(BF16)
        pe = pe * jnp.where(lo, rs[0], rs[1]) * qgp_ref[...]
        pe = jnp.where(is_lat, _rope(pe, cos, sp, sn, MLA_ROPE // 4), pe)
        qp_ref[:, j * LANES:(j + 1) * LANES] = pe.astype(BF16)


def _proj_mla(xp, xs, mods, gain, w_in, qa, kva, w_uq, qg, qgp, tables):
    n_nope = MLA_HEADS * MLA_NOPE
    n_pe = MLA_HEADS * MLA_ROPE
    return pl.pallas_call(
        _proj_mla_kernel,
        grid=(N_TILES,),
        in_specs=[_XP_SPEC, _XS_SPEC, _const_spec((1, D_MODEL)), _mod_spec(0), _mod_spec(1),
                  _const_spec(w_in.shape), _const_spec((1, MLA_Q_RANK)), _const_spec((1, MLA_KV_RANK)),
                  _const_spec(w_uq.shape), _const_spec((1, LANES)), _const_spec((1, LANES)),
                  _ROPE_SPEC, _ROPE_SPEC, _ROPE_SPEC],
        out_specs=[_tok_spec(n_nope), _tok_spec(n_pe), _tok_spec(MLA_KV_RANK), _tok_spec(LANES),
                   _cache_spec(SEQ, MLA_KV_RANK), _cache_spec(SEQ, MLA_ROPE)],
        out_shape=[jax.ShapeDtypeStruct((N_TOK, n_nope), BF16),
                   jax.ShapeDtypeStruct((N_TOK, n_pe), BF16),
                   jax.ShapeDtypeStruct((N_TOK, MLA_KV_RANK), BF16),
                   jax.ShapeDtypeStruct((N_TOK, LANES), F32),
                   _cache_shape(SEQ, MLA_KV_RANK), _cache_shape(SEQ, MLA_ROPE)],
        compiler_params=_cparams(1),
        name="proj_mla",
    )(xp, xs, gain, mods, mods, w_in, qa, kva, w_uq, qg, qgp, *tables)


def _mla_expand_kernel(ckv_ref, kpe_ref, w_ref, kg_ref, kgp_ref, cos_ref, sp_ref, sn_ref,
                       kn_ref, kp_ref, v_ref, *, rope_from_tile):
    i = pl.program_id(0)
    kv = _dot(ckv_ref[...].astype(BF16), w_ref[...])
    kpe = kpe_ref[...]
    lo = _lane_lo(kpe.shape)
    pe_ss = jnp.sum(jnp.where(lo, kpe * kpe, 0.0), axis=-1, keepdims=True)
    inv_d = 1.0 / (MLA_NOPE + MLA_ROPE)
    cos, sp, sn = cos_ref[...], sp_ref[...], sn_ref[...]
    for j in range(MLA_HEADS // 2):
        rs = []
        for a in range(2):
            hh = 2 * j + a
            kn = kv[:, hh * 2 * LANES:hh * 2 * LANES + LANES]
            v_ref[:, hh * LANES:(hh + 1) * LANES] = kv[:, hh * 2 * LANES + LANES:(hh + 1) * 2 * LANES].astype(BF16)
            r = lax.rsqrt((jnp.sum(kn * kn, axis=-1, keepdims=True) + pe_ss) * inv_d + EPS)
            rs.append(r)
            kn_ref[:, hh * LANES:(hh + 1) * LANES] = (kn * r * kg_ref[...]).astype(BF16)
        pe = kpe * jnp.where(lo, rs[0], rs[1]) * kgp_ref[...]
        if rope_from_tile is not None:
            pe = jnp.where(i >= rope_from_tile, _rope(pe, cos, sp, sn, MLA_ROPE // 4), pe)
        kp_ref[:, j * LANES:(j + 1) * LANES] = pe.astype(BF16)


def _mla_expand(ckv, kpe_dup, w_ukv, kg, kgp, tables, rope_from_tile):
    n = ckv.shape[0]
    n_nope = MLA_HEADS * MLA_NOPE
    n_pe = MLA_HEADS * MLA_ROPE
    return pl.pallas_call(
        functools.partial(_mla_expand_kernel, rope_from_tile=rope_from_tile),
        grid=(n // TM,),
        in_specs=[_tok_spec(MLA_KV_RANK), _tok_spec(LANES), _const_spec(w_ukv.shape),
                  _const_spec((1, LANES)), _const_spec((1, LANES)),
                  _ROPE_SPEC, _ROPE_SPEC, _ROPE_SPEC],
        out_specs=[_tok_spec(n_nope), _tok_spec(n_pe), _tok_spec(n_nope)],
        out_shape=[jax.ShapeDtypeStruct((n, n_nope), BF16),
                   jax.ShapeDtypeStruct((n, n_pe), BF16),
                   jax.ShapeDtypeStruct((n, n_nope), BF16)],
        compiler_params=_cparams(1),
        name="mla_expand",
    )(ckv, kpe_dup, w_ukv, kg, kgp, *tables)


def _prompt_spec(width):
    return pl.BlockSpec((TM, width), lambda b: (b, 0))


def _latq_spec(rows, width):
    per = DEC_SEQ // rows
    return pl.BlockSpec((rows, width), lambda b, t: (N_PROMPT_TOK // rows + b * per + t, 0))


def _latkv_spec(width):
    return pl.BlockSpec((DEC_SEQ, width), lambda b, t: (LAT_BLOCK0 + b, 0))


def _lato_spec(rows):
    per = DEC_SEQ // rows
    return pl.BlockSpec((rows, D_MODEL), lambda b, t: (b * per + t, 0))


def _att_kernel(*refs, with_ctx):
    if with_ctx:
        q_ref, k_ref, v_ref, kc_ref, vc_ref, o_ref = refs
    else:
        q_ref, k_ref, v_ref, o_ref = refs
    tq = q_ref.shape[0]
    nu = ATT_UNIT_HEADS
    per_kv = ATT_HEADS // ATT_KV_HEADS // nu

    def scores(u):
        q = jnp.concatenate([_chunk(q_ref, u * nu + g) for g in range(nu)], axis=0)
        s_list = [_dot_nt(_chunk(k_ref, u // per_kv), q)]
        if with_ctx:
            s_list.append(_dot_nt(kc_ref[u // per_kv].astype(BF16), q))
        return s_list

    def finish(u, s_list):
        values = [_chunk(v_ref, u // per_kv)]
        if with_ctx:
            values.append(vc_ref[u // per_kv].astype(BF16))
        ps, inv = _softmax2_parts(s_list)
        o = _pv(ps, values) * inv
        for g in range(nu):
            o_ref[:, (u * nu + g) * LANES:(u * nu + g + 1) * LANES] = (
                o[:, g * tq:(g + 1) * tq].T.astype(o_ref.dtype))

    _head_pipeline(ATT_HEADS // nu, scores, finish)


def _att_attend(q, k, v, cache_k, cache_v):
    nk = ATT_KV_HEADS * ATT_HEAD_DIM
    out_p = pl.pallas_call(
        functools.partial(_att_kernel, with_ctx=False),
        grid=(N_PROMPT_TILES,),
        in_specs=[_prompt_spec(D_MODEL), _prompt_spec(nk), _prompt_spec(nk)],
        out_specs=_prompt_spec(D_MODEL),
        out_shape=jax.ShapeDtypeStruct((N_PROMPT_TOK, D_MODEL), BF16),
        compiler_params=_cparams(1),
        name="att_prompt",
    )(q, k, v)
    ctx = pl.BlockSpec((None, None, ATT_KV_HEADS, PAST_LEN, LANES), lambda b, t: (b, 0, 0, 0, 0))
    out_s = pl.pallas_call(
        functools.partial(_att_kernel, with_ctx=True),
        grid=(DEC_BATCH, TILES_PER_DEC),
        in_specs=[_latq_spec(TM, D_MODEL), _latkv_spec(nk), _latkv_spec(nk), ctx, ctx],
        out_specs=_lato_spec(TM),
        out_shape=jax.ShapeDtypeStruct((N_LAT_TOK, D_MODEL), BF16),
        compiler_params=_cparams(2),
        name="att_latent",
    )(q, k, v, cache_k, cache_v)
    return out_p, out_s


def _diff_kernel(*refs, lam_init, with_ctx):
    if with_ctx:
        (q_ref, k_ref, v_ref, kc_ref, vc_ref, lq1_ref, lk1_ref, lq2_ref, lk2_ref, sub_ref, o_ref) = refs
    else:
        (q_ref, k_ref, v_ref, lq1_ref, lk1_ref, lq2_ref, lk2_ref, sub_ref, o_ref) = refs
    tq = q_ref.shape[0]
    lam = (jnp.exp(jnp.sum(lq1_ref[...] * lk1_ref[...], axis=-1, keepdims=True))
           - jnp.exp(jnp.sum(lq2_ref[...] * lk2_ref[...], axis=-1, keepdims=True)) + lam_init)
    sub = sub_ref[...] * (1.0 - lam_init)

    def scores(hd):
        q = jnp.concatenate(_split_halves(_chunk(q_ref, hd)), axis=0)
        s_list = [_dot_nt(_chunk(k_ref, hd), q)]
        if with_ctx:
            s_list.append(_dot_nt(kc_ref[hd].astype(BF16), q))
        return s_list

    def finish(hd, s_list):
        values = [_chunk(v_ref, hd)]
        if with_ctx:
            values.append(vc_ref[hd].astype(BF16))
        ps, inv = _softmax2_parts(s_list)
        c0 = inv[:, :tq]
        c1 = -lam * inv[:, tq:]
        o = _pv([p[:, :tq] * c0 + p[:, tq:] * c1 for p in ps], values)
        o = o * lax.rsqrt(jnp.mean(o * o, axis=0, keepdims=True) + EPS) * sub
        o_ref[:, hd * LANES:(hd + 1) * LANES] = o.T.astype(o_ref.dtype)

    _head_pipeline(DIFF_HEADS, scores, finish)


def _diff_attend(q, k, v, cache_k_pair, cache_v, lq1, lk1, lq2, lk2, subln, lam_init):
    small = [lq1, lk1, lq2, lk2, subln]
    small_specs = [_const_spec(s.shape) for s in small]
    out_p = pl.pallas_call(
        functools.partial(_diff_kernel, lam_init=lam_init, with_ctx=False),
        grid=(N_PROMPT_TILES,),
        in_specs=[_prompt_spec(D_MODEL)] * 3 + small_specs,
        out_specs=_prompt_spec(D_MODEL),
        out_shape=jax.ShapeDtypeStruct((N_PROMPT_TOK, D_MODEL), BF16),
        compiler_params=_cparams(1),
        name="diff_prompt",
    )(q, k, v, *small)
    out_s = pl.pallas_call(
        functools.partial(_diff_kernel, lam_init=lam_init, with_ctx=True),
        grid=(DEC_BATCH, TILES_PER_DEC),
        in_specs=[_latq_spec(TM, D_MODEL), _latkv_spec(D_MODEL), _latkv_spec(D_MODEL),
                  pl.BlockSpec((None, DIFF_HEADS, PAST_LEN, LANES), lambda b, t: (b, 0, 0, 0)),
                  pl.BlockSpec((None, None, DIFF_HEADS, PAST_LEN, LANES), lambda b, t: (b, 0, 0, 0, 0))]
                 + small_specs,
        out_specs=_lato_spec(TM),
        out_shape=jax.ShapeDtypeStruct((N_LAT_TOK, D_MODEL), BF16),
        compiler_params=_cparams(2),
        name="diff_latent",
    )(q, k, v, cache_k_pair, cache_v, *small)
    return out_p, out_s


def _swa_pipeline(q_ref, sink_ref, score_fns, value_fns, o_ref):
    tq = q_ref.shape[0]
    per_kv = SWA_HEADS // SWA_KV_HEADS // 2
    first = lax.broadcasted_iota(jnp.int32, (LANES, tq), 0) < HALF

    def scores(c):
        q = jnp.concatenate(_split_halves(_chunk(q_ref, c)), axis=0)
        return [fn(c // per_kv, q) for fn in score_fns]

    def finish(c, s_list):
        sink = jnp.concatenate([jnp.full((1, tq), sink_ref[2 * c + a] * LOG2E, F32) for a in range(2)],
                               axis=1)
        ps, inv = _softmax2_parts(s_list, extra=sink)
        o = _pv(ps, [fn(c // per_kv) for fn in value_fns]) * inv
        oc = jnp.where(first, o[:, :tq], o[:, tq:])
        o_ref[:, c * LANES:(c + 1) * LANES] = oc.T.astype(o_ref.dtype)

    _head_pipeline(SWA_HEADS // 2, scores, finish)


def _swa_prompt_kernel(sink_ref, q_ref, k_ref, v_ref, o_ref):
    _swa_pipeline(q_ref, sink_ref, [lambda kv, q: _dot_nt(_chunk(k_ref, kv), q)],
                  [lambda kv: _chunk(v_ref, kv)], o_ref)


def _swa_latent_kernel(sink_ref, q_ref, k_ref, v_ref, kc_ref, vc_ref, o_ref):
    n = pl.program_id(1)
    tq = q_ref.shape[0]
    span = 3 * SWA_QB
    start = pl.multiple_of(jnp.clip((n - 1) * SWA_QB, 0, DEC_SEQ - span), SWA_QB)
    cols = lax.broadcasted_iota(jnp.int32, (span, 2 * tq), 1)
    qpos = n * SWA_QB + jnp.bitwise_and(cols, tq - 1)
    kpos = start + lax.broadcasted_iota(jnp.int32, (span, 2 * tq), 0)
    valid = jnp.abs(qpos - kpos) <= WINDOW

    def local(ref, kv):
        return ref[pl.ds(start, span), kv * LANES:(kv + 1) * LANES]

    _swa_pipeline(q_ref, sink_ref,
                  [lambda kv, q: jnp.where(valid, _dot_nt(local(k_ref, kv), q), -1e30),
                   lambda kv, q: _dot_nt(kc_ref[kv], q)],
                  [lambda kv: local(v_ref, kv), lambda kv: vc_ref[kv]], o_ref)


def _swa_attend(q, kd, vd, cache_kd, cache_vd, sink):
    nkd = 2 * SWA_KV_HEADS * SWA_HEAD_DIM
    smem = pl.BlockSpec(memory_space=pltpu.SMEM)
    out_p = pl.pallas_call(
        _swa_prompt_kernel,
        grid=(N_PROMPT_TILES,),
        in_specs=[smem, _prompt_spec(D_MODEL), _prompt_spec(nkd), _prompt_spec(nkd)],
        out_specs=_prompt_spec(D_MODEL),
        out_shape=jax.ShapeDtypeStruct((N_PROMPT_TOK, D_MODEL), BF16),
        compiler_params=_cparams(1),
        name="swa_prompt",
    )(sink, q, kd, vd)
    ctx = pl.BlockSpec((None, SWA_KV_HEADS, PAST_LEN, LANES), lambda b, n: (b, 0, 0, 0))
    out_s = pl.pallas_call(
        _swa_latent_kernel,
        grid=(DEC_BATCH, DEC_SEQ // SWA_QB),
        in_specs=[smem, _latq_spec(SWA_QB, D_MODEL), _latkv_spec(nkd), _latkv_spec(nkd), ctx, ctx],
        out_specs=_lato_spec(SWA_QB),
        out_shape=jax.ShapeDtypeStruct((N_LAT_TOK, D_MODEL), BF16),
        compiler_params=_cparams(2),
        name="swa_latent",
    )(sink, q, kd, vd, cache_kd, cache_vd)
    return out_p, out_s


def _mla_kernel(*refs, with_ctx):
    if with_ctx:
        (qn_ref, qp_ref, kn_ref, kp_ref, v_ref, knc_ref, kpc_ref, vc_ref, o_ref) = refs
    else:
        (qn_ref, qp_ref, kn_ref, kp_ref, v_ref, o_ref) = refs

    def scores(hd):
        j, a = hd // 2, hd % 2
        q = jnp.concatenate([_chunk(qn_ref, hd), _split_halves(_chunk(qp_ref, j))[a]], axis=1)
        s_list = [_dot_nt(jnp.concatenate([_chunk(kn_ref, hd), _chunk(kp_ref, j)], axis=1), q)]
        if with_ctx:
            s_list.append(_dot_nt(jnp.concatenate([_chunk(knc_ref, hd), _chunk(kpc_ref, j)], axis=1), q))
        return s_list

    def finish(hd, s_list):
        values = [_chunk(v_ref, hd)]
        if with_ctx:
            values.append(_chunk(vc_ref, hd))
        ps, inv = _softmax2_parts(s_list)
        o_ref[:, hd * LANES:(hd + 1) * LANES] = (_pv(ps, values) * inv).T.astype(o_ref.dtype)

    _head_pipeline(MLA_HEADS, scores, finish)


def _mla_attend(qn, qp, kn, kp, v, knc, kpc, vc):
    n_pe = MLA_HEADS * MLA_ROPE
    out_p = pl.pallas_call(
        functools.partial(_mla_kernel, with_ctx=False),
        grid=(N_PROMPT_TILES,),
        in_specs=[_prompt_spec(D_MODEL), _prompt_spec(n_pe), _prompt_spec(D_MODEL), _prompt_spec(n_pe),
                  _prompt_spec(D_MODEL)],
        out_specs=_prompt_spec(D_MODEL),
        out_shape=jax.ShapeDtypeStruct((N_PROMPT_TOK, D_MODEL), BF16),
        compiler_params=_cparams(1),
        name="mla_prompt",
    )(qn, qp, kn, kp, v)

    def ctx(width):
        return pl.BlockSpec((PAST_LEN, width), lambda b, t: (b, 0))

    out_s = pl.pallas_call(
        functools.partial(_mla_kernel, with_ctx=True),
        grid=(DEC_BATCH, TILES_PER_DEC),
        in_specs=[_latq_spec(TM, D_MODEL), _latq_spec(TM, n_pe),
                  _latkv_spec(D_MODEL), _latkv_spec(n_pe), _latkv_spec(D_MODEL),
                  ctx(D_MODEL), ctx(n_pe), ctx(D_MODEL)],
        out_specs=_lato_spec(TM),
        out_shape=jax.ShapeDtypeStruct((N_LAT_TOK, D_MODEL), BF16),
        compiler_params=_cparams(2),
        name="mla_latent",
    )(qn, qp, kn, kp, v, knc, kpc, vc)
    return out_p, out_s


def _omlp_kernel(ap_ref, as_ref, wo_ref, xp_ref, xs_ref, g1_ref, gain_ref, sh_ref, sc_ref, g2_ref,
                 w1c_ref, w2c_ref, op_ref, os_ref, wo_s, w1_s, w2_s):
    s = pl.program_id(0)
    per = MLP_FF_CHUNK // MLP_LOAD_COLS

    @pl.when(s == 0)
    def _():
        wo_s[...] = wo_ref[...].astype(BF16)

    for part in range(per):
        @pl.when((s < N_W_LOADS) & (s % per == part))
        def _(part=part):
            w1_s[s // per, :, part * MLP_LOAD_COLS:(part + 1) * MLP_LOAD_COLS] = w1c_ref[...].astype(BF16)

    @pl.when((s >= N_W_LOADS) & (s < N_LOAD_STEPS))
    def _():
        j = s - N_W_LOADS
        w2_s[j // per, pl.ds(pl.multiple_of((j % per) * MLP_LOAD_COLS, MLP_LOAD_COLS), MLP_LOAD_COLS), :] = (
            w2c_ref[...].astype(BF16))

    @pl.when(s >= N_LOAD_STEPS)
    def _():
        t = s - N_LOAD_STEPS
        is_prompt = t < N_MLP_PROMPT_TILES
        grp = _tile_group(t * (MLP_TM // TM))
        a = jnp.where(is_prompt, ap_ref[...], as_ref[...])
        x = jnp.where(is_prompt, xp_ref[...], xs_ref[...])
        x1 = x + g1_ref[pl.ds(grp, 1), :] * _dot(a, wo_s[...])
        h = _norm_mod(x1, gain_ref[...], sh_ref[pl.ds(grp, 1), :], sc_ref[pl.ds(grp, 1), :]).astype(BF16)
        acc = None
        for c in range(D_FF // MLP_FF_CHUNK):
            u = _dot(h, w1_s[c])
            u = jnp.square(jnp.maximum(u, 0.0)).astype(BF16)
            y = _dot(u, w2_s[c])
            acc = y if acc is None else acc + y
        out = x1 + g2_ref[pl.ds(grp, 1), :] * acc

        @pl.when(is_prompt)
        def _():
            op_ref[...] = out

        @pl.when(jnp.logical_not(is_prompt))
        def _():
            os_ref[...] = out


def _omlp(attn_p, attn_s, w_o, xp, xs, mods, gain_ffn, w1_all, w2_all, layer):
    n_lat_tiles = N_LAT_TOK // MLP_TM

    def tok(s):
        return jnp.maximum(s - N_LOAD_STEPS, 0)

    p_spec = pl.BlockSpec((MLP_TM, D_MODEL), lambda s: (jnp.minimum(tok(s), N_MLP_PROMPT_TILES - 1), 0))
    l_spec = pl.BlockSpec((MLP_TM, D_MODEL),
                          lambda s: (jnp.clip(tok(s) - N_MLP_PROMPT_TILES, 0, n_lat_tiles - 1), 0))
    w1_spec = pl.BlockSpec((None, D_MODEL, MLP_LOAD_COLS),
                           lambda s: (layer, 0, jnp.minimum(s, N_W_LOADS - 1)))
    w2_spec = pl.BlockSpec((None, MLP_LOAD_COLS, D_MODEL),
                           lambda s: (layer, jnp.clip(s - N_W_LOADS, 0, N_W_LOADS - 1), 0))
    n_chunks = D_FF // MLP_FF_CHUNK
    return pl.pallas_call(
        _omlp_kernel,
        grid=(N_LOAD_STEPS + N_TOK // MLP_TM,),
        in_specs=[p_spec, l_spec, _const_spec(w_o.shape), p_spec, l_spec, _mod_spec(2),
                  _const_spec((1, D_MODEL)), _mod_spec(3), _mod_spec(4), _mod_spec(5), w1_spec, w2_spec],
        out_specs=[p_spec, l_spec],
        out_shape=[jax.ShapeDtypeStruct((N_PROMPT_TOK, D_MODEL), F32),
                   jax.ShapeDtypeStruct((N_LAT_TOK, D_MODEL), F32)],
        scratch_shapes=[pltpu.VMEM((D_MODEL, D_MODEL), BF16),
                        pltpu.VMEM((n_chunks, D_MODEL, MLP_FF_CHUNK), BF16),
                        pltpu.VMEM((n_chunks, MLP_FF_CHUNK, D_MODEL), BF16)],
        compiler_params=_cparams(1),
        name="omlp",
    )(attn_p, attn_s, w_o, xp, xs, mods, gain_ffn, mods, mods, mods, w1_all, w2_all)


def _row(v, scale=1.0):
    return (v.astype(F32) * scale).reshape(1, -1)


def _pair(v, scale=1.0):
    return (jnp.concatenate([v, v]).astype(F32) * scale).reshape(1, LANES)


def kernel(x_prompt, x_sample, cache_att_k, cache_att_v, cache_diff_k, cache_diff_v, cache_swa_k, cache_swa_v, cache_mla_ckv, cache_mla_kpe, c, c_ctx, ada_w, ada_b, norm_mix, norm_ffn, att_w_qkv, att_q_norm, att_k_norm, att_w_o, diff_w_qkv, diff_q_norm, diff_k_norm, diff_lq1, diff_lk1, diff_lq2, diff_lk2, diff_subln, diff_w_o, swa_w_qkv, swa_q_norm, swa_k_norm, swa_sink, swa_w_o, mla_w_in, mla_q_a_norm, mla_kv_a_norm, mla_w_uq, mla_w_ukv, mla_q_norm, mla_k_norm, mla_w_o, mlp_w1, mlp_w2):
    xp = x_prompt.reshape(N_PROMPT_TOK, D_MODEL)
    xs = x_sample.reshape(N_LAT_TOK, D_MODEL)
    cond = jnp.concatenate([c_ctx[None], c, jnp.zeros((COND_ROWS - 1 - DEC_BATCH, D_MODEL), F32)], axis=0)
    mods_all = _modulation(cond, ada_w, ada_b)

    tab_att = _rope_tables(ATT_HEAD_DIM)
    tab_64 = _rope_tables(DIFF_HEAD_DIM)

    outs = {}
    for layer in range(DEPTH):
        mods = mods_all[layer]
        gain_mix = _row(norm_mix[layer])
        gain_ffn = _row(norm_ffn[layer])
        if layer == 0:
            qs = ATT_HEAD_DIM ** -0.5 * LOG2E
            q, k, v, outs["att_k"], outs["att_v"] = _proj_att(
                xp, xs, mods, gain_mix, att_w_qkv[0].astype(BF16),
                _row(att_q_norm[0], qs), _row(att_k_norm[0]), tab_att)
            attn_p, attn_s = _att_attend(q, k, v, cache_att_k, cache_att_v)
            w_o = att_w_o[0]
        elif layer == 1:
            qs = DIFF_HEAD_DIM ** -0.5 * LOG2E
            q, k, v, outs["diff_k"], outs["diff_v"] = _proj_diff(
                xp, xs, mods, gain_mix, diff_w_qkv[0].astype(BF16),
                _pair(diff_q_norm[0], qs), _pair(diff_k_norm[0]), tab_64)
            lam_init = 0.8 - 0.6 * math.exp(-0.3 * layer)
            ck = cache_diff_k[:, 0].transpose(0, 1, 3, 2, 4).reshape(
                DEC_BATCH, DIFF_HEADS, PAST_LEN, LANES)
            attn_p, attn_s = _diff_attend(q, k, v, ck, cache_diff_v,
                                          _row(diff_lq1[0]), _row(diff_lk1[0]),
                                          _row(diff_lq2[0]), _row(diff_lk2[0]),
                                          diff_subln[0].astype(F32).reshape(LANES, 1), lam_init)
            w_o = diff_w_o[0]
        elif layer == 2:
            qs = SWA_HEAD_DIM ** -0.5 * LOG2E
            q, kd, vd, outs["swa_k"], outs["swa_v"] = _proj_swa(
                xp, xs, mods, gain_mix, swa_w_qkv[0].astype(BF16),
                _pair(swa_q_norm[0], qs), _pair(swa_k_norm[0]), tab_64)
            ckd = jnp.concatenate([cache_swa_k[:, 0]] * 2, axis=-1).astype(BF16)
            cvd = jnp.concatenate([cache_swa_v[:, 0]] * 2, axis=-1).astype(BF16)
            attn_p, attn_s = _swa_attend(q, kd, vd, ckd, cvd, swa_sink[0].astype(F32))
            w_o = swa_w_o[0]
        else:
            qs = (MLA_NOPE + MLA_ROPE) ** -0.5 * LOG2E
            w_in = mla_w_in[0]
            w_in = jnp.concatenate([w_in, w_in[:, -MLA_ROPE:]], axis=1).astype(BF16)
            w_uq = mla_w_uq[0].reshape(MLA_Q_RANK, MLA_HEADS, MLA_NOPE + MLA_ROPE)
            w_uq = jnp.concatenate([w_uq[:, :, :MLA_NOPE].reshape(MLA_Q_RANK, -1),
                                    w_uq[:, :, MLA_NOPE:].reshape(MLA_Q_RANK, -1)], axis=1).astype(BF16)
            w_ukv = mla_w_ukv[0].astype(BF16)
            qg, kg = mla_q_norm[0], mla_k_norm[0]
            qn, qp, ckv, kpe, outs["mla_ckv"], outs["mla_kpe"] = _proj_mla(
                xp, xs, mods, gain_mix, w_in, _row(mla_q_a_norm[0]), _row(mla_kv_a_norm[0]), w_uq,
                _row(qg[:MLA_NOPE], qs), _pair(qg[MLA_NOPE:], qs), tab_64)
            kn, kp, vv = _mla_expand(ckv, kpe, w_ukv, _row(kg[:MLA_NOPE]), _pair(kg[MLA_NOPE:]),
                                     tab_64, N_PROMPT_TILES)
            c_ckv = cache_mla_ckv[:, 0].reshape(DEC_BATCH * PAST_LEN, MLA_KV_RANK)
            c_kpe = cache_mla_kpe[:, 0].reshape(DEC_BATCH * PAST_LEN, MLA_ROPE)
            c_kpe = jnp.concatenate([c_kpe, c_kpe], axis=-1)
            knc, kpc, vc = _mla_expand(c_ckv, c_kpe, w_ukv, _row(kg[:MLA_NOPE]), _pair(kg[MLA_NOPE:]),
                                       tab_64, None)
            attn_p, attn_s = _mla_attend(qn, qp, kn, kp, vv, knc, kpc, vc)
            w_o = mla_w_o[0]
        xp, xs = _omlp(attn_p, attn_s, w_o, xp, xs, mods, gain_ffn, mlp_w1, mlp_w2, layer)

    y_prompt = xp.reshape(BATCH, SEQ, D_MODEL)
    y_sample = xs.reshape(DEC_BATCH, DEC_SEQ, D_MODEL)
    return (y_prompt, y_sample, outs["att_k"], outs["att_v"], outs["diff_k"], outs["diff_v"],
            outs["swa_k"], outs["swa_v"], outs["mla_ckv"], outs["mla_kpe"])
```

```python
import functools
import math

import numpy as np
import jax
import jax.numpy as jnp
from jax import lax
from jax.experimental import pallas as pl
from jax.experimental.pallas import tpu as pltpu

D_MODEL = 1024
BATCH = 16
SEQ = 256
DEPTH = 4
DEC_BATCH = 2
DEC_SEQ = 1024
PAST_LEN = 256
GRID_W = 64
ROPE_THETA = 10000.0
EPS = 1e-6
D_FF = 4 * D_MODEL
MOD_CHUNKS = 6
LOG2E = 1.4426950408889634

ATT_HEADS, ATT_KV_HEADS, ATT_HEAD_DIM = 8, 2, 128
DIFF_HEADS, DIFF_HEAD_DIM = 8, 64
SWA_HEADS, SWA_KV_HEADS, SWA_HEAD_DIM, WINDOW = 16, 4, 64, 128
MLA_HEADS, MLA_NOPE, MLA_ROPE, MLA_VDIM = 8, 128, 64, 128
MLA_Q_RANK, MLA_KV_RANK = 512, 256

LANES = 128
HALF = LANES // 2
TM = 256
N_PROMPT_TOK = BATCH * SEQ
N_LAT_TOK = DEC_BATCH * DEC_SEQ
N_TOK = N_PROMPT_TOK + N_LAT_TOK
N_TILES = N_TOK // TM
N_PROMPT_TILES = N_PROMPT_TOK // TM
TILES_PER_DEC = DEC_SEQ // TM
LAT_BLOCK0 = N_PROMPT_TOK // DEC_SEQ
COND_ROWS = 8
PROJ_UNIT = 2 * LANES
MLP_TM = 512
MLP_FF_CHUNK = 1024
MLP_LOAD_COLS = 512
N_W_LOADS = D_FF // MLP_LOAD_COLS
N_LOAD_STEPS = 2 * N_W_LOADS
N_MLP_PROMPT_TILES = N_PROMPT_TOK // MLP_TM
SWA_QB = 128
ATT_UNIT_HEADS = 4
VMEM_LIMIT = 56 * 1024 * 1024

F32 = jnp.float32
BF16 = jnp.bfloat16


def _cparams(n_axes):
    return pltpu.CompilerParams(dimension_semantics=("arbitrary",) * n_axes,
                                vmem_limit_bytes=VMEM_LIMIT)


def _dot(a, b):
    return jnp.dot(a, b, preferred_element_type=F32)


def _dot_nt(a, b):
    return lax.dot_general(a, b, (((1,), (1,)), ((), ())), preferred_element_type=F32)


def _dot_tn(a, b):
    return lax.dot_general(a, b, (((0,), (0,)), ((), ())), preferred_element_type=F32)


def _const_spec(shape):
    nd = len(shape)
    return pl.BlockSpec(shape, lambda *_: (0,) * nd, pipeline_mode=pl.Buffered(1))


def _chunk(ref, c, width=LANES):
    return ref[:, c * width:(c + 1) * width]


def _put(ref, c, val):
    ref[:, c * LANES:(c + 1) * LANES] = val.astype(ref.dtype)


def _tile_group(i):
    return jnp.where(i < N_PROMPT_TILES, 0, 1 + (i - N_PROMPT_TILES) // TILES_PER_DEC)


def _rope_tile(i):
    return jnp.maximum(i - N_PROMPT_TILES, 0) % TILES_PER_DEC


def _norm_mod(x, gain, shift, scale):
    ms = jnp.mean(x * x, axis=-1, keepdims=True)
    y = x * lax.rsqrt(ms + EPS) * gain
    return y * (1.0 + scale) + shift


def _lane_lo(shape):
    return lax.broadcasted_iota(jnp.int32, shape, len(shape) - 1) < HALF


def _rope(y, cos, sin_prev, sin_next, quarter):
    return (y * cos + pltpu.roll(y, quarter, 1) * sin_prev
            + pltpu.roll(y, LANES - quarter, 1) * sin_next)


def _rope_tables(rot_dim):
    half = rot_dim // 2
    quarter = rot_dim // 4
    inv = np.float32(ROPE_THETA) ** (-np.arange(0, half, 2, dtype=np.float32) / np.float32(half))
    pos = np.arange(DEC_SEQ)
    row = (pos // GRID_W).astype(np.float32)
    col = (pos % GRID_W).astype(np.float32)
    lane = np.arange(LANES)
    dd = lane % rot_dim
    q = dd // quarter
    f = dd % quarter
    ang = np.where((q < 2)[None, :], row[:, None], col[:, None]) * inv[f][None, :]
    ang = ang.astype(np.float32)
    cos = np.cos(ang).astype(np.float32)
    sin = np.sin(ang).astype(np.float32)
    odd = (q % 2 == 1)[None, :]
    sin_prev = np.where(odd, sin, 0.0).astype(np.float32)
    sin_next = np.where(odd, 0.0, -sin).astype(np.float32)
    return jnp.asarray(cos), jnp.asarray(sin_prev), jnp.asarray(sin_next)


def _lane_sum_matrix(rows, cols, value=1.0):
    lane = np.arange(LANES)
    m = np.where(rows(lane)[:, None] & cols(lane)[None, :], value, 0.0).astype(np.float32)
    return jnp.asarray(m, dtype=BF16)


def _group_mean_matrix(group):
    lane = np.arange(LANES)
    m = np.where((lane[:, None] // group) == (lane[None, :] // group), 1.0 / group, 0.0)
    return jnp.asarray(m.astype(np.float32), dtype=BF16)


def _sq_bf16(y):
    return (y * y).astype(BF16)


def _qk_norm(yc, m_ref, gain, rope):
    yn = yc * lax.rsqrt(_dot(_sq_bf16(yc), m_ref[...]) + EPS) * gain
    return yn if rope is None else _rope(yn, *rope)


def _matmul_units(h, w_ref, n_units, width, emit):
    def unit(u):
        return _dot(h, w_ref[:, u * width:(u + 1) * width])

    nxt = unit(0)
    for u in range(n_units):
        cur = nxt
        if u + 1 < n_units:
            nxt = unit(u + 1)
        emit(u, cur)


def _matmul_chunks(h, w_ref, n_chunks, emit):
    per = PROJ_UNIT // LANES

    def emit_unit(u, y):
        for t in range(per):
            emit(u * per + t, y[:, t * LANES:(t + 1) * LANES])

    _matmul_units(h, w_ref, n_chunks // per, PROJ_UNIT, emit_unit)


def _by_tile_kind(i, body):
    pl.when(i < N_PROMPT_TILES)(functools.partial(body, False))
    pl.when(i >= N_PROMPT_TILES)(functools.partial(body, True))


def _softmax2_parts(s_list, extra=None):
    m = jnp.max(s_list[0], axis=0, keepdims=True)
    for s in s_list[1:]:
        m = jnp.maximum(m, jnp.max(s, axis=0, keepdims=True))
    if extra is not None:
        m = jnp.maximum(m, extra)
    ps = [jnp.exp2(s - m) for s in s_list]
    tot = ps[0].sum(axis=0, keepdims=True)
    for p in ps[1:]:
        tot = tot + p.sum(axis=0, keepdims=True)
    if extra is not None:
        tot = tot + jnp.exp2(extra - m)
    return ps, 1.0 / tot


def _head_pipeline(n, scores, finish):
    nxt = scores(0)
    for h in range(n):
        cur = nxt
        if h + 1 < n:
            nxt = scores(h + 1)
        finish(h, cur)


def _pv(ps, values):
    o = None
    for p, v in zip(ps, values):
        t = _dot_tn(v, p.astype(BF16))
        o = t if o is None else o + t
    return o


def _split_halves(q):
    lo = _lane_lo(q.shape)
    zero = jnp.zeros_like(q)
    return jnp.where(lo, q, zero), jnp.where(lo, zero, q)


def _mod_kernel(cond_ref, w_ref, b_ref, o_ref):
    c = cond_ref[...]
    s = (c * jax.nn.sigmoid(c)).astype(BF16)
    o_ref[0] = _dot(s, w_ref[0].astype(BF16)) + b_ref[0]


def _modulation(cond, ada_w, ada_b):
    tn = 1536
    n = MOD_CHUNKS * D_MODEL
    return pl.pallas_call(
        _mod_kernel,
        grid=(DEPTH, n // tn),
        in_specs=[
            pl.BlockSpec((COND_ROWS, D_MODEL), lambda l, j: (0, 0)),
            pl.BlockSpec((1, D_MODEL, tn), lambda l, j: (l, 0, j)),
            pl.BlockSpec((1, 1, tn), lambda l, j: (l, 0, j)),
        ],
        out_specs=pl.BlockSpec((1, COND_ROWS, tn), lambda l, j: (l, 0, j)),
        out_shape=jax.ShapeDtypeStruct((DEPTH, COND_ROWS, n), F32),
        compiler_params=_cparams(2),
        name="modulation",
    )(cond, ada_w, ada_b.reshape(DEPTH, 1, n))


def _mod_spec(chunk):
    return pl.BlockSpec((COND_ROWS, D_MODEL), lambda i: (0, chunk))


def _mod_row(ref, i):
    return ref[pl.ds(_tile_group(i), 1), :]


_ROPE_SPEC = pl.BlockSpec((TM, LANES), lambda i: (_rope_tile(i), 0))
_LANE_MAT_SPEC = _const_spec((LANES, LANES))


def _tok_spec(width):
    return pl.BlockSpec((TM, width), lambda i: (i, 0))


_XP_SPEC = pl.BlockSpec((TM, D_MODEL), lambda i: (jnp.minimum(i, N_PROMPT_TILES - 1), 0))
_XS_SPEC = pl.BlockSpec((TM, D_MODEL), lambda i: (jnp.maximum(i - N_PROMPT_TILES, 0), 0))


def _x_tile(xp_ref, xs_ref, i):
    return jnp.where(i < N_PROMPT_TILES, xp_ref[...], xs_ref[...])


def _cache_spec(*dims):
    nd = len(dims)
    return pl.BlockSpec((1, 1) + dims,
                        lambda i: (jnp.minimum(i, N_PROMPT_TILES - 1), 0) + (0,) * nd)


def _cache_shape(*dims):
    return jax.ShapeDtypeStruct((BATCH, 1) + dims, F32)


def _proj_att_kernel(xp_ref, xs_ref, gain_ref, sh_ref, sc_ref, w_ref, qg_ref, kg_ref, m_ref,
                     cos_ref, sp_ref, sn_ref, q_ref, k_ref, v_ref, ck_ref, cv_ref):
    i = pl.program_id(0)
    h = _norm_mod(_x_tile(xp_ref, xs_ref, i), gain_ref[...], _mod_row(sh_ref, i), _mod_row(sc_ref, i)).astype(BF16)
    nq, nk = ATT_HEADS, ATT_KV_HEADS

    def body(lat):
        rope = (cos_ref[...], sp_ref[...], sn_ref[...], ATT_HEAD_DIM // 4) if lat else None

        def emit(c, yc):
            if c < nq:
                _put(q_ref, c, _qk_norm(yc, m_ref, qg_ref[...], rope))
            elif c < nq + nk:
                kn = _qk_norm(yc, m_ref, kg_ref[...], rope)
                _put(k_ref, c - nq, kn)
                if not lat:
                    ck_ref[0, 0, c - nq] = kn
            else:
                _put(v_ref, c - nq - nk, yc)
                if not lat:
                    cv_ref[0, 0, c - nq - nk] = yc

        _matmul_chunks(h, w_ref, nq + 2 * nk, emit)

    _by_tile_kind(i, body)


def _proj_att(xp, xs, mods, gain, w, qg, kg, tables):
    nq, nk = ATT_HEADS * ATT_HEAD_DIM, ATT_KV_HEADS * ATT_HEAD_DIM
    return pl.pallas_call(
        _proj_att_kernel,
        grid=(N_TILES,),
        in_specs=[_XP_SPEC, _XS_SPEC, _const_spec((1, D_MODEL)), _mod_spec(0), _mod_spec(1),
                  _const_spec(w.shape), _const_spec((1, LANES)), _const_spec((1, LANES)), _LANE_MAT_SPEC,
                  _ROPE_SPEC, _ROPE_SPEC, _ROPE_SPEC],
        out_specs=[_tok_spec(nq), _tok_spec(nk), _tok_spec(nk),
                   _cache_spec(ATT_KV_HEADS, SEQ, ATT_HEAD_DIM), _cache_spec(ATT_KV_HEADS, SEQ, ATT_HEAD_DIM)],
        out_shape=[jax.ShapeDtypeStruct((N_TOK, nq), BF16),
                   jax.ShapeDtypeStruct((N_TOK, nk), BF16),
                   jax.ShapeDtypeStruct((N_TOK, nk), BF16),
                   _cache_shape(ATT_KV_HEADS, SEQ, ATT_HEAD_DIM), _cache_shape(ATT_KV_HEADS, SEQ, ATT_HEAD_DIM)],
        compiler_params=_cparams(1),
        name="proj_att",
    )(xp, xs, gain, mods, mods, w, qg, kg, _group_mean_matrix(ATT_HEAD_DIM), *tables)


def _proj_diff_kernel(xp_ref, xs_ref, gain_ref, sh_ref, sc_ref, w_ref, qg_ref, kg_ref, m_ref,
                      cos_ref, sp_ref, sn_ref, q_ref, k_ref, v_ref, ck_ref, cv_ref):
    i = pl.program_id(0)
    h = _norm_mod(_x_tile(xp_ref, xs_ref, i), gain_ref[...], _mod_row(sh_ref, i), _mod_row(sc_ref, i)).astype(BF16)
    nh = DIFF_HEADS

    def body(lat):
        rope = (cos_ref[...], sp_ref[...], sn_ref[...], DIFF_HEAD_DIM // 4) if lat else None

        def emit(c, yc):
            if c < nh:
                _put(q_ref, c, _qk_norm(yc, m_ref, qg_ref[...], rope))
            elif c < 2 * nh:
                kn = _qk_norm(yc, m_ref, kg_ref[...], rope)
                _put(k_ref, c - nh, kn)
                if not lat:
                    ck_ref[0, 0, c - nh, 0] = kn[:, :HALF]
                    ck_ref[0, 0, c - nh, 1] = kn[:, HALF:]
            else:
                _put(v_ref, c - 2 * nh, yc)
                if not lat:
                    cv_ref[0, 0, c - 2 * nh] = yc

        _matmul_chunks(h, w_ref, 3 * nh, emit)

    _by_tile_kind(i, body)


def _proj_diff(xp, xs, mods, gain, w, qg, kg, tables):
    n = DIFF_HEADS * 2 * DIFF_HEAD_DIM
    return pl.pallas_call(
        _proj_diff_kernel,
        grid=(N_TILES,),
        in_specs=[_XP_SPEC, _XS_SPEC, _const_spec((1, D_MODEL)), _mod_spec(0), _mod_spec(1),
                  _const_spec(w.shape), _const_spec((1, LANES)), _const_spec((1, LANES)), _LANE_MAT_SPEC,
                  _ROPE_SPEC, _ROPE_SPEC, _ROPE_SPEC],
        out_specs=[_tok_spec(n), _tok_spec(n), _tok_spec(n),
                   _cache_spec(DIFF_HEADS, 2, SEQ, DIFF_HEAD_DIM), _cache_spec(DIFF_HEADS, SEQ, 2 * DIFF_HEAD_DIM)],
        out_shape=[jax.ShapeDtypeStruct((N_TOK, n), BF16)] * 3
                  + [_cache_shape(DIFF_HEADS, 2, SEQ, DIFF_HEAD_DIM),
                     _cache_shape(DIFF_HEADS, SEQ, 2 * DIFF_HEAD_DIM)],
        compiler_params=_cparams(1),
        name="proj_diff",
    )(xp, xs, gain, mods, mods, w, qg, kg, _group_mean_matrix(DIFF_HEAD_DIM), *tables)


def _dup_halves(yc):
    lo = _lane_lo(yc.shape)
    sw = pltpu.roll(yc, HALF, 1)
    return jnp.where(lo, yc, sw), jnp.where(lo, sw, yc)


def _proj_swa_kernel(xp_ref, xs_ref, gain_ref, sh_ref, sc_ref, w_ref, qg_ref, kg_ref, m_ref,
                     cos_ref, sp_ref, sn_ref, q_ref, kd_ref, vd_ref, ck_ref, cv_ref):
    i = pl.program_id(0)
    h = _norm_mod(_x_tile(xp_ref, xs_ref, i), gain_ref[...], _mod_row(sh_ref, i), _mod_row(sc_ref, i)).astype(BF16)
    nq = SWA_HEADS * SWA_HEAD_DIM // LANES
    nk = SWA_KV_HEADS * SWA_HEAD_DIM // LANES

    def body(lat):
        rope = (cos_ref[...], sp_ref[...], sn_ref[...], SWA_HEAD_DIM // 4) if lat else None

        def emit(c, yc):
            if c < nq:
                _put(q_ref, c, _qk_norm(yc, m_ref, qg_ref[...], rope))
                return
            if c < nq + nk:
                j, c_ref, d_ref = c - nq, ck_ref, kd_ref
                yc = _qk_norm(yc, m_ref, kg_ref[...], rope)
            else:
                j, c_ref, d_ref = c - nq - nk, cv_ref, vd_ref
            for t, dup in enumerate(_dup_halves(yc)):
                _put(d_ref, 2 * j + t, dup)
                if not lat:
                    c_ref[0, 0, 2 * j + t] = dup[:, :HALF]

        _matmul_chunks(h, w_ref, nq + 2 * nk, emit)

    _by_tile_kind(i, body)


def _proj_swa(xp, xs, mods, gain, w, qg, kg, tables):
    nq, nk = SWA_HEADS * SWA_HEAD_DIM, SWA_KV_HEADS * SWA_HEAD_DIM
    return pl.pallas_call(
        _proj_swa_kernel,
        grid=(N_TILES,),
        in_specs=[_XP_SPEC, _XS_SPEC, _const_spec((1, D_MODEL)), _mod_spec(0), _mod_spec(1),
                  _const_spec(w.shape), _const_spec((1, LANES)), _const_spec((1, LANES)), _LANE_MAT_SPEC,
                  _ROPE_SPEC, _ROPE_SPEC, _ROPE_SPEC],
        out_specs=[_tok_spec(nq), _tok_spec(2 * nk), _tok_spec(2 * nk),
                   _cache_spec(SWA_KV_HEADS, SEQ, SWA_HEAD_DIM), _cache_spec(SWA_KV_HEADS, SEQ, SWA_HEAD_DIM)],
        out_shape=[jax.ShapeDtypeStruct((N_TOK, nq), BF16),
                   jax.ShapeDtypeStruct((N_TOK, 2 * nk), BF16),
                   jax.ShapeDtypeStruct((N_TOK, 2 * nk), BF16),
                   _cache_shape(SWA_KV_HEADS, SEQ, SWA_HEAD_DIM), _cache_shape(SWA_KV_HEADS, SEQ, SWA_HEAD_DIM)],
        compiler_params=_cparams(1),
        name="proj_swa",
    )(xp, xs, gain, mods, mods, w, qg, kg, _group_mean_matrix(SWA_HEAD_DIM), *tables)


def _mla_lane_matrices():
    everything = lambda lane: lane >= 0
    return (_lane_sum_matrix(everything, everything),
            _lane_sum_matrix(lambda lane: lane < HALF, everything),
            _lane_sum_matrix(lambda lane: lane >= HALF, everything))


def _proj_mla_kernel(xp_ref, xs_ref, gain_ref, sh_ref, sc_ref, w_in_ref, qa_ref, kva_ref, w_uq_ref,
                     qg_ref, qgp_ref, all_ref, lo_ref, hi_ref, cos_ref, sp_ref, sn_ref,
                     qn_ref, qp_ref, ckv_ref, kpe_ref, c_ckv_ref, c_kpe_ref):
    i = pl.program_id(0)
    h = _norm_mod(_x_tile(xp_ref, xs_ref, i), gain_ref[...], _mod_row(sh_ref, i), _mod_row(sc_ref, i)).astype(BF16)
    y = _dot(h, w_in_ref[...])
    c_q = y[:, :MLA_Q_RANK]
    c_kv = y[:, MLA_Q_RANK:MLA_Q_RANK + MLA_KV_RANK]
    kpe = y[:, MLA_Q_RANK + MLA_KV_RANK:]
    kpe_ref[...] = kpe
    ckv = c_kv * lax.rsqrt(jnp.mean(c_kv * c_kv, axis=-1, keepdims=True) + EPS) * kva_ref[...]
    ckv_ref[...] = ckv.astype(BF16)
    cq = (c_q * lax.rsqrt(jnp.mean(c_q * c_q, axis=-1, keepdims=True) + EPS) * qa_ref[...]).astype(BF16)
    lo = _lane_lo((TM, LANES))
    inv_d = 1.0 / (MLA_NOPE + MLA_ROPE)

    def body(lat):
        if not lat:
            c_ckv_ref[0, 0] = ckv
            c_kpe_ref[0, 0] = kpe[:, :MLA_ROPE]

        def emit(j, yq):
            pe = yq[:, 2 * LANES:]
            pe_sq = _sq_bf16(pe)
            rs = []
            for a, half_ref in enumerate((lo_ref, hi_ref)):
                nope = yq[:, a * LANES:(a + 1) * LANES]
                ss = _dot(_sq_bf16(nope), all_ref[...]) + _dot(pe_sq, half_ref[...])
                r = lax.rsqrt(ss * inv_d + EPS)
                rs.append(r)
                _put(qn_ref, 2 * j + a, nope * r * qg_ref[...])
            pe = pe * jnp.where(lo, rs[0], rs[1]) * qgp_ref[...]
            if lat:
                pe = _rope(pe, cos_ref[...], sp_ref[...], sn_ref[...], MLA_ROPE // 4)
            _put(qp_ref, j, pe)

        _matmul_units(cq, w_uq_ref, MLA_HEADS // 2, 3 * LANES, emit)

    _by_tile_kind(i, body)


def _proj_mla(xp, xs, mods, gain, w_in, qa, kva, w_uq, qg, qgp, tables):
    n_nope = MLA_HEADS * MLA_NOPE
    n_pe = MLA_HEADS * MLA_ROPE
    return pl.pallas_call(
        _proj_mla_kernel,
        grid=(N_TILES,),
        in_specs=[_XP_SPEC, _XS_SPEC, _const_spec((1, D_MODEL)), _mod_spec(0), _mod_spec(1),
                  _const_spec(w_in.shape), _const_spec((1, MLA_Q_RANK)), _const_spec((1, MLA_KV_RANK)),
                  _const_spec(w_uq.shape), _const_spec((1, LANES)), _const_spec((1, LANES)),
                  _LANE_MAT_SPEC, _LANE_MAT_SPEC, _LANE_MAT_SPEC,
                  _ROPE_SPEC, _ROPE_SPEC, _ROPE_SPEC],
        out_specs=[_tok_spec(n_nope), _tok_spec(n_pe), _tok_spec(MLA_KV_RANK), _tok_spec(LANES),
                   _cache_spec(SEQ, MLA_KV_RANK), _cache_spec(SEQ, MLA_ROPE)],
        out_shape=[jax.ShapeDtypeStruct((N_TOK, n_nope), BF16),
                   jax.ShapeDtypeStruct((N_TOK, n_pe), BF16),
                   jax.ShapeDtypeStruct((N_TOK, MLA_KV_RANK), BF16),
                   jax.ShapeDtypeStruct((N_TOK, LANES), F32),
                   _cache_shape(SEQ, MLA_KV_RANK), _cache_shape(SEQ, MLA_ROPE)],
        compiler_params=_cparams(1),
        name="proj_mla",
    )(xp, xs, gain, mods, mods, w_in, qa, kva, w_uq, qg, qgp, *_mla_lane_matrices(), *tables)


def _mla_expand_kernel(ckv_ref, kpe_ref, w_ref, kg_ref, kgp_ref, all_ref, lo_ref,
                       cos_ref, sp_ref, sn_ref, kn_ref, kp_ref, v_ref, *, rope_from_tile):
    i = pl.program_id(0)
    ckv = ckv_ref[...].astype(BF16)
    kpe = kpe_ref[...]
    pe_ss = _dot(_sq_bf16(kpe), lo_ref[...])
    lo = _lane_lo(kpe.shape)
    inv_d = 1.0 / (MLA_NOPE + MLA_ROPE)

    def body(lat):
        rs = []

        def emit(hh, y):
            kn = y[:, :LANES]
            _put(v_ref, hh, y[:, LANES:])
            r = lax.rsqrt((_dot(_sq_bf16(kn), all_ref[...]) + pe_ss) * inv_d + EPS)
            rs.append(r)
            _put(kn_ref, hh, kn * r * kg_ref[...])
            if hh % 2 == 1:
                pe = kpe * jnp.where(lo, rs[hh - 1], rs[hh]) * kgp_ref[...]
                if lat:
                    pe = _rope(pe, cos_ref[...], sp_ref[...], sn_ref[...], MLA_ROPE // 4)
                _put(kp_ref, hh // 2, pe)

        _matmul_units(ckv, w_ref, MLA_HEADS, 2 * LANES, emit)

    if rope_from_tile is None:
        body(False)
    else:
        pl.when(i < rope_from_tile)(functools.partial(body, False))
        pl.when(i >= rope_from_tile)(functools.partial(body, True))


def _mla_expand(ckv, kpe_dup, w_ukv, kg, kgp, tables, rope_from_tile):
    n = ckv.shape[0]
    n_nope = MLA_HEADS * MLA_NOPE
    n_pe = MLA_HEADS * MLA_ROPE
    m_all, m_lo, _ = _mla_lane_matrices()
    return pl.pallas_call(
        functools.partial(_mla_expand_kernel, rope_from_tile=rope_from_tile),
        grid=(n // TM,),
        in_specs=[_tok_spec(MLA_KV_RANK), _tok_spec(LANES), _const_spec(w_ukv.shape),
                  _const_spec((1, LANES)), _const_spec((1, LANES)), _LANE_MAT_SPEC, _LANE_MAT_SPEC,
                  _ROPE_SPEC, _ROPE_SPEC, _ROPE_SPEC],
        out_specs=[_tok_spec(n_nope), _tok_spec(n_pe), _tok_spec(n_nope)],
        out_shape=[jax.ShapeDtypeStruct((n, n_nope), BF16),
                   jax.ShapeDtypeStruct((n, n_pe), BF16),
                   jax.ShapeDtypeStruct((n, n_nope), BF16)],
        compiler_params=_cparams(1),
        name="mla_expand",
    )(ckv, kpe_dup, w_ukv, kg, kgp, m_all, m_lo, *tables)


def _prompt_spec(width):
    return pl.BlockSpec((TM, width), lambda b: (b, 0))


def _latq_spec(rows, width):
    per = DEC_SEQ // rows
    return pl.BlockSpec((rows, width), lambda b, t: (N_PROMPT_TOK // rows + b * per + t, 0))


def _latkv_spec(width):
    return pl.BlockSpec((DEC_SEQ, width), lambda b, t: (LAT_BLOCK0 + b, 0))


def _lato_spec(rows):
    per = DEC_SEQ // rows
    return pl.BlockSpec((rows, D_MODEL), lambda b, t: (b * per + t, 0))


def _att_kernel(*refs, with_ctx):
    if with_ctx:
        q_ref, k_ref, v_ref, kc_ref, vc_ref, o_ref = refs
    else:
        q_ref, k_ref, v_ref, o_ref = refs
    tq = q_ref.shape[0]
    nu = ATT_UNIT_HEADS
    per_kv = ATT_HEADS // ATT_KV_HEADS // nu

    def scores(u):
        q = jnp.concatenate([_chunk(q_ref, u * nu + g) for g in range(nu)], axis=0)
        s_list = [_dot_nt(_chunk(k_ref, u // per_kv), q)]
        if with_ctx:
            s_list.append(_dot_nt(kc_ref[u // per_kv].astype(BF16), q))
        return s_list

    def finish(u, s_list):
        values = [_chunk(v_ref, u // per_kv)]
        if with_ctx:
            values.append(vc_ref[u // per_kv].astype(BF16))
        ps, inv = _softmax2_parts(s_list)
        o = _pv(ps, values) * inv
        for g in range(nu):
            o_ref[:, (u * nu + g) * LANES:(u * nu + g + 1) * LANES] = (
                o[:, g * tq:(g + 1) * tq].T.astype(o_ref.dtype))

    _head_pipeline(ATT_HEADS // nu, scores, finish)


def _att_attend(q, k, v, cache_k, cache_v):
    nk = ATT_KV_HEADS * ATT_HEAD_DIM
    out_p = pl.pallas_call(
        functools.partial(_att_kernel, with_ctx=False),
        grid=(N_PROMPT_TILES,),
        in_specs=[_prompt_spec(D_MODEL), _prompt_spec(nk), _prompt_spec(nk)],
        out_specs=_prompt_spec(D_MODEL),
        out_shape=jax.ShapeDtypeStruct((N_PROMPT_TOK, D_MODEL), BF16),
        compiler_params=_cparams(1),
        name="att_prompt",
    )(q, k, v)
    ctx = pl.BlockSpec((None, None, ATT_KV_HEADS, PAST_LEN, LANES), lambda b, t: (b, 0, 0, 0, 0))
    out_s = pl.pallas_call(
        functools.partial(_att_kernel, with_ctx=True),
        grid=(DEC_BATCH, TILES_PER_DEC),
        in_specs=[_latq_spec(TM, D_MODEL), _latkv_spec(nk), _latkv_spec(nk), ctx, ctx],
        out_specs=_lato_spec(TM),
        out_shape=jax.ShapeDtypeStruct((N_LAT_TOK, D_MODEL), BF16),
        compiler_params=_cparams(2),
        name="att_latent",
    )(q, k, v, cache_k, cache_v)
    return out_p, out_s


def _diff_kernel(*refs, lam_init, with_ctx):
    if with_ctx:
        (q_ref, k_ref, v_ref, kc_ref, vc_ref, lq1_ref, lk1_ref, lq2_ref, lk2_ref, sub_ref, o_ref) = refs
    else:
        (q_ref, k_ref, v_ref, lq1_ref, lk1_ref, lq2_ref, lk2_ref, sub_ref, o_ref) = refs
    tq = q_ref.shape[0]
    lam = (jnp.exp(jnp.sum(lq1_ref[...] * lk1_ref[...], axis=-1, keepdims=True))
           - jnp.exp(jnp.sum(lq2_ref[...] * lk2_ref[...], axis=-1, keepdims=True)) + lam_init)
    sub = sub_ref[...] * (1.0 - lam_init)

    def scores(hd):
        q = jnp.concatenate(_split_halves(_chunk(q_ref, hd)), axis=0)
        s_list = [_dot_nt(_chunk(k_ref, hd), q)]
        if with_ctx:
            s_list.append(_dot_nt(kc_ref[hd].astype(BF16), q))
        return s_list

    def finish(hd, s_list):
        values = [_chunk(v_ref, hd)]
        if with_ctx:
            values.append(vc_ref[hd].astype(BF16))
        ps, inv = _softmax2_parts(s_list)
        c0 = inv[:, :tq]
        c1 = -lam * inv[:, tq:]
        o = _pv([p[:, :tq] * c0 + p[:, tq:] * c1 for p in ps], values)
        o = o * lax.rsqrt(jnp.mean(o * o, axis=0, keepdims=True) + EPS) * sub
        o_ref[:, hd * LANES:(hd + 1) * LANES] = o.T.astype(o_ref.dtype)

    _head_pipeline(DIFF_HEADS, scores, finish)


def _diff_attend(q, k, v, cache_k_pair, cache_v, lq1, lk1, lq2, lk2, subln, lam_init):
    small = [lq1, lk1, lq2, lk2, subln]
    small_specs = [_const_spec(s.shape) for s in small]
    out_p = pl.pallas_call(
        functools.partial(_diff_kernel, lam_init=lam_init, with_ctx=False),
        grid=(N_PROMPT_TILES,),
        in_specs=[_prompt_spec(D_MODEL)] * 3 + small_specs,
        out_specs=_prompt_spec(D_MODEL),
        out_shape=jax.ShapeDtypeStruct((N_PROMPT_TOK, D_MODEL), BF16),
        compiler_params=_cparams(1),
        name="diff_prompt",
    )(q, k, v, *small)
    out_s = pl.pallas_call(
        functools.partial(_diff_kernel, lam_init=lam_init, with_ctx=True),
        grid=(DEC_BATCH, TILES_PER_DEC),
        in_specs=[_latq_spec(TM, D_MODEL), _latkv_spec(D_MODEL), _latkv_spec(D_MODEL),
                  pl.BlockSpec((None, DIFF_HEADS, PAST_LEN, LANES), lambda b, t: (b, 0, 0, 0)),
                  pl.BlockSpec((None, None, DIFF_HEADS, PAST_LEN, LANES), lambda b, t: (b, 0, 0, 0, 0))]
                 + small_specs,
        out_specs=_lato_spec(TM),
        out_shape=jax.ShapeDtypeStruct((N_LAT_TOK, D_MODEL), BF16),
        compiler_params=_cparams(2),
        name="diff_latent",
    )(q, k, v, cache_k_pair, cache_v, *small)
    return out_p, out_s


def _swa_pipeline(q_ref, sink_ref, score_fns, value_fns, o_ref):
    tq = q_ref.shape[0]
    per_kv = SWA_HEADS // SWA_KV_HEADS // 2
    first = lax.broadcasted_iota(jnp.int32, (LANES, tq), 0) < HALF

    def scores(c):
        q = jnp.concatenate(_split_halves(_chunk(q_ref, c)), axis=0)
        return [fn(c // per_kv, q) for fn in score_fns]

    def finish(c, s_list):
        sink = jnp.concatenate([jnp.full((1, tq), sink_ref[2 * c + a] * LOG2E, F32) for a in range(2)],
                               axis=1)
        ps, inv = _softmax2_parts(s_list, extra=sink)
        o = _pv(ps, [fn(c // per_kv) for fn in value_fns]) * inv
        oc = jnp.where(first, o[:, :tq], o[:, tq:])
        o_ref[:, c * LANES:(c + 1) * LANES] = oc.T.astype(o_ref.dtype)

    _head_pipeline(SWA_HEADS // 2, scores, finish)


def _swa_prompt_kernel(sink_ref, q_ref, k_ref, v_ref, o_ref):
    _swa_pipeline(q_ref, sink_ref, [lambda kv, q: _dot_nt(_chunk(k_ref, kv), q)],
                  [lambda kv: _chunk(v_ref, kv)], o_ref)


def _swa_latent_kernel(sink_ref, q_ref, k_ref, v_ref, kc_ref, vc_ref, o_ref):
    n = pl.program_id(1)
    tq = q_ref.shape[0]
    span = 3 * SWA_QB
    start = pl.multiple_of(jnp.clip((n - 1) * SWA_QB, 0, DEC_SEQ - span), SWA_QB)
    cols = lax.broadcasted_iota(jnp.int32, (span, 2 * tq), 1)
    qpos = n * SWA_QB + jnp.bitwise_and(cols, tq - 1)
    kpos = start + lax.broadcasted_iota(jnp.int32, (span, 2 * tq), 0)
    valid = jnp.abs(qpos - kpos) <= WINDOW

    def local(ref, kv):
        return ref[pl.ds(start, span), kv * LANES:(kv + 1) * LANES]

    _swa_pipeline(q_ref, sink_ref,
                  [lambda kv, q: jnp.where(valid, _dot_nt(local(k_ref, kv), q), -1e30),
                   lambda kv, q: _dot_nt(kc_ref[kv], q)],
                  [lambda kv: local(v_ref, kv), lambda kv: vc_ref[kv]], o_ref)


def _swa_attend(q, kd, vd, cache_kd, cache_vd, sink):
    nkd = 2 * SWA_KV_HEADS * SWA_HEAD_DIM
    smem = pl.BlockSpec(memory_space=pltpu.SMEM)
    out_p = pl.pallas_call(
        _swa_prompt_kernel,
        grid=(N_PROMPT_TILES,),
        in_specs=[smem, _prompt_spec(D_MODEL), _prompt_spec(nkd), _prompt_spec(nkd)],
        out_specs=_prompt_spec(D_MODEL),
        out_shape=jax.ShapeDtypeStruct((N_PROMPT_TOK, D_MODEL), BF16),
        compiler_params=_cparams(1),
        name="swa_prompt",
    )(sink, q, kd, vd)
    ctx = pl.BlockSpec((None, SWA_KV_HEADS, PAST_LEN, LANES), lambda b, n: (b, 0, 0, 0))
    out_s = pl.pallas_call(
        _swa_latent_kernel,
        grid=(DEC_BATCH, DEC_SEQ // SWA_QB),
        in_specs=[smem, _latq_spec(SWA_QB, D_MODEL), _latkv_spec(nkd), _latkv_spec(nkd), ctx, ctx],
        out_specs=_lato_spec(SWA_QB),
        out_shape=jax.ShapeDtypeStruct((N_LAT_TOK, D_MODEL), BF16),
        compiler_params=_cparams(2),
        name="swa_latent",
    )(sink, q, kd, vd, cache_kd, cache_vd)
    return out_p, out_s


def _mla_kernel(*refs, with_ctx):
    if with_ctx:
        (qn_ref, qp_ref, kn_ref, kp_ref, v_ref, knc_ref, kpc_ref, vc_ref, o_ref) = refs
    else:
        (qn_ref, qp_ref, kn_ref, kp_ref, v_ref, o_ref) = refs

    def scores(hd):
        j, a = hd // 2, hd % 2
        q = jnp.concatenate([_chunk(qn_ref, hd), _split_halves(_chunk(qp_ref, j))[a]], axis=1)
        s_list = [_dot_nt(jnp.concatenate([_chunk(kn_ref, hd), _chunk(kp_ref, j)], axis=1), q)]
        if with_ctx:
            s_list.append(_dot_nt(jnp.concatenate([_chunk(knc_ref, hd), _chunk(kpc_ref, j)], axis=1), q))
        return s_list

    def finish(hd, s_list):
        values = [_chunk(v_ref, hd)]
        if with_ctx:
            values.append(_chunk(vc_ref, hd))
        ps, inv = _softmax2_parts(s_list)
        o_ref[:, hd * LANES:(hd + 1) * LANES] = (_pv(ps, values) * inv).T.astype(o_ref.dtype)

    _head_pipeline(MLA_HEADS, scores, finish)


def _mla_attend(qn, qp, kn, kp, v, knc, kpc, vc):
    n_pe = MLA_HEADS * MLA_ROPE
    out_p = pl.pallas_call(
        functools.partial(_mla_kernel, with_ctx=False),
        grid=(N_PROMPT_TILES,),
        in_specs=[_prompt_spec(D_MODEL), _prompt_spec(n_pe), _prompt_spec(D_MODEL), _prompt_spec(n_pe),
                  _prompt_spec(D_MODEL)],
        out_specs=_prompt_spec(D_MODEL),
        out_shape=jax.ShapeDtypeStruct((N_PROMPT_TOK, D_MODEL), BF16),
        compiler_params=_cparams(1),
        name="mla_prompt",
    )(qn, qp, kn, kp, v)

    def ctx(width):
        return pl.BlockSpec((PAST_LEN, width), lambda b, t: (b, 0))

    out_s = pl.pallas_call(
        functools.partial(_mla_kernel, with_ctx=True),
        grid=(DEC_BATCH, TILES_PER_DEC),
        in_specs=[_latq_spec(TM, D_MODEL), _latq_spec(TM, n_pe),
                  _latkv_spec(D_MODEL), _latkv_spec(n_pe), _latkv_spec(D_MODEL),
                  ctx(D_MODEL), ctx(n_pe), ctx(D_MODEL)],
        out_specs=_lato_spec(TM),
        out_shape=jax.ShapeDtypeStruct((N_LAT_TOK, D_MODEL), BF16),
        compiler_params=_cparams(2),
        name="mla_latent",
    )(qn, qp, kn, kp, v, knc, kpc, vc)
    return out_p, out_s


def _omlp_kernel(ap_ref, as_ref, wo_ref, xp_ref, xs_ref, g1_ref, gain_ref, sh_ref, sc_ref, g2_ref,
                 w1c_ref, w2c_ref, op_ref, os_ref, wo_s, w1_s, w2_s):
    s = pl.program_id(0)
    per = MLP_FF_CHUNK // MLP_LOAD_COLS

    @pl.when(s == 0)
    def _():
        wo_s[...] = wo_ref[...].astype(BF16)

    for part in range(per):
        @pl.when((s < N_W_LOADS) & (s % per == part))
        def _(part=part):
            w1_s[s // per, :, part * MLP_LOAD_COLS:(part + 1) * MLP_LOAD_COLS] = w1c_ref[...].astype(BF16)

    @pl.when((s >= N_W_LOADS) & (s < N_LOAD_STEPS))
    def _():
        j = s - N_W_LOADS
        w2_s[j // per, pl.ds(pl.multiple_of((j % per) * MLP_LOAD_COLS, MLP_LOAD_COLS), MLP_LOAD_COLS), :] = (
            w2c_ref[...].astype(BF16))

    @pl.when(s >= N_LOAD_STEPS)
    def _():
        t = s - N_LOAD_STEPS
        is_prompt = t < N_MLP_PROMPT_TILES
        grp = _tile_group(t * (MLP_TM // TM))
        a = jnp.where(is_prompt, ap_ref[...], as_ref[...])
        x = jnp.where(is_prompt, xp_ref[...], xs_ref[...])
        x1 = x + g1_ref[pl.ds(grp, 1), :] * _dot(a, wo_s[...])
        h = _norm_mod(x1, gain_ref[...], sh_ref[pl.ds(grp, 1), :], sc_ref[pl.ds(grp, 1), :]).astype(BF16)
        acc = None
        for c in range(D_FF // MLP_FF_CHUNK):
            u = _dot(h, w1_s[c])
            u = jnp.square(jnp.maximum(u, 0.0)).astype(BF16)
            y = _dot(u, w2_s[c])
            acc = y if acc is None else acc + y
        out = x1 + g2_ref[pl.ds(grp, 1), :] * acc

        @pl.when(is_prompt)
        def _():
            op_ref[...] = out

        @pl.when(jnp.logical_not(is_prompt))
        def _():
            os_ref[...] = out


def _omlp(attn_p, attn_s, w_o, xp, xs, mods, gain_ffn, w1_all, w2_all, layer):
    n_lat_tiles = N_LAT_TOK // MLP_TM

    def tok(s):
        return jnp.maximum(s - N_LOAD_STEPS, 0)

    p_spec = pl.BlockSpec((MLP_TM, D_MODEL), lambda s: (jnp.minimum(tok(s), N_MLP_PROMPT_TILES - 1), 0))
    l_spec = pl.BlockSpec((MLP_TM, D_MODEL),
                          lambda s: (jnp.clip(tok(s) - N_MLP_PROMPT_TILES, 0, n_lat_tiles - 1), 0))
    w1_spec = pl.BlockSpec((None, D_MODEL, MLP_LOAD_COLS),
                           lambda s: (layer, 0, jnp.minimum(s, N_W_LOADS - 1)))
    w2_spec = pl.BlockSpec((None, MLP_LOAD_COLS, D_MODEL),
                           lambda s: (layer, jnp.clip(s - N_W_LOADS, 0, N_W_LOADS - 1), 0))
    n_chunks = D_FF // MLP_FF_CHUNK
    return pl.pallas_call(
        _omlp_kernel,
        grid=(N_LOAD_STEPS + N_TOK // MLP_TM,),
        in_specs=[p_spec, l_spec, _const_spec(w_o.shape), p_spec, l_spec, _mod_spec(2),
                  _const_spec((1, D_MODEL)), _mod_spec(3), _mod_spec(4), _mod_spec(5), w1_spec, w2_spec],
        out_specs=[p_spec, l_spec],
        out_shape=[jax.ShapeDtypeStruct((N_PROMPT_TOK, D_MODEL), F32),
                   jax.ShapeDtypeStruct((N_LAT_TOK, D_MODEL), F32)],
        scratch_shapes=[pltpu.VMEM((D_MODEL, D_MODEL), BF16),
                        pltpu.VMEM((n_chunks, D_MODEL, MLP_FF_CHUNK), BF16),
                        pltpu.VMEM((n_chunks, MLP_FF_CHUNK, D_MODEL), BF16)],
        compiler_params=_cparams(1),
        name="omlp",
    )(attn_p, attn_s, w_o, xp, xs, mods, gain_ffn, mods, mods, mods, w1_all, w2_all)


def _row(v, scale=1.0):
    return (v.astype(F32) * scale).reshape(1, -1)


def _pair(v, scale=1.0):
    return (jnp.concatenate([v, v]).astype(F32) * scale).reshape(1, LANES)


def kernel(x_prompt, x_sample, cache_att_k, cache_att_v, cache_diff_k, cache_diff_v, cache_swa_k, cache_swa_v, cache_mla_ckv, cache_mla_kpe, c, c_ctx, ada_w, ada_b, norm_mix, norm_ffn, att_w_qkv, att_q_norm, att_k_norm, att_w_o, diff_w_qkv, diff_q_norm, diff_k_norm, diff_lq1, diff_lk1, diff_lq2, diff_lk2, diff_subln, diff_w_o, swa_w_qkv, swa_q_norm, swa_k_norm, swa_sink, swa_w_o, mla_w_in, mla_q_a_norm, mla_kv_a_norm, mla_w_uq, mla_w_ukv, mla_q_norm, mla_k_norm, mla_w_o, mlp_w1, mlp_w2):
    xp = x_prompt.reshape(N_PROMPT_TOK, D_MODEL)
    xs = x_sample.reshape(N_LAT_TOK, D_MODEL)
    cond = jnp.concatenate([c_ctx[None], c, jnp.zeros((COND_ROWS - 1 - DEC_BATCH, D_MODEL), F32)], axis=0)
    mods_all = _modulation(cond, ada_w, ada_b)

    tab_att = _rope_tables(ATT_HEAD_DIM)
    tab_64 = _rope_tables(DIFF_HEAD_DIM)

    outs = {}
    for layer in range(DEPTH):
        mods = mods_all[layer]
        gain_mix = _row(norm_mix[layer])
        gain_ffn = _row(norm_ffn[layer])
        if layer == 0:
            qs = ATT_HEAD_DIM ** -0.5 * LOG2E
            q, k, v, outs["att_k"], outs["att_v"] = _proj_att(
                xp, xs, mods, gain_mix, att_w_qkv[0].astype(BF16),
                _row(att_q_norm[0], qs), _row(att_k_norm[0]), tab_att)
            attn_p, attn_s = _att_attend(q, k, v, cache_att_k, cache_att_v)
            w_o = att_w_o[0]
        elif layer == 1:
            qs = DIFF_HEAD_DIM ** -0.5 * LOG2E
            q, k, v, outs["diff_k"], outs["diff_v"] = _proj_diff(
                xp, xs, mods, gain_mix, diff_w_qkv[0].astype(BF16),
                _pair(diff_q_norm[0], qs), _pair(diff_k_norm[0]), tab_64)
            lam_init = 0.8 - 0.6 * math.exp(-0.3 * layer)
            ck = cache_diff_k[:, 0].transpose(0, 1, 3, 2, 4).reshape(
                DEC_BATCH, DIFF_HEADS, PAST_LEN, LANES)
            attn_p, attn_s = _diff_attend(q, k, v, ck, cache_diff_v,
                                          _row(diff_lq1[0]), _row(diff_lk1[0]),
                                          _row(diff_lq2[0]), _row(diff_lk2[0]),
                                          diff_subln[0].astype(F32).reshape(LANES, 1), lam_init)
            w_o = diff_w_o[0]
        elif layer == 2:
            qs = SWA_HEAD_DIM ** -0.5 * LOG2E
            q, kd, vd, outs["swa_k"], outs["swa_v"] = _proj_swa(
                xp, xs, mods, gain_mix, swa_w_qkv[0].astype(BF16),
                _pair(swa_q_norm[0], qs), _pair(swa_k_norm[0]), tab_64)
            ckd = jnp.concatenate([cache_swa_k[:, 0]] * 2, axis=-1).astype(BF16)
            cvd = jnp.concatenate([cache_swa_v[:, 0]] * 2, axis=-1).astype(BF16)
            attn_p, attn_s = _swa_attend(q, kd, vd, ckd, cvd, swa_sink[0].astype(F32))
            w_o = swa_w_o[0]
        else:
            qs = (MLA_NOPE + MLA_ROPE) ** -0.5 * LOG2E
            w_in = mla_w_in[0]
            w_in = jnp.concatenate([w_in, w_in[:, -MLA_ROPE:]], axis=1).astype(BF16)
            w_uq = mla_w_uq[0].reshape(MLA_Q_RANK, MLA_HEADS // 2, 2, MLA_NOPE + MLA_ROPE)
            w_uq = jnp.concatenate([w_uq[..., :MLA_NOPE].reshape(MLA_Q_RANK, MLA_HEADS // 2, 2 * MLA_NOPE),
                                    w_uq[..., MLA_NOPE:].reshape(MLA_Q_RANK, MLA_HEADS // 2, 2 * MLA_ROPE)],
                                   axis=-1).reshape(MLA_Q_RANK, -1).astype(BF16)
            w_ukv = mla_w_ukv[0].astype(BF16)
            qg, kg = mla_q_norm[0], mla_k_norm[0]
            qn, qp, ckv, kpe, outs["mla_ckv"], outs["mla_kpe"] = _proj_mla(
                xp, xs, mods, gain_mix, w_in, _row(mla_q_a_norm[0]), _row(mla_kv_a_norm[0]), w_uq,
                _row(qg[:MLA_NOPE], qs), _pair(qg[MLA_NOPE:], qs), tab_64)
            kn, kp, vv = _mla_expand(ckv, kpe, w_ukv, _row(kg[:MLA_NOPE]), _pair(kg[MLA_NOPE:]),
                                     tab_64, N_PROMPT_TILES)
            c_ckv = cache_mla_ckv[:, 0].reshape(DEC_BATCH * PAST_LEN, MLA_KV_RANK)
            c_kpe = cache_mla_kpe[:, 0].reshape(DEC_BATCH * PAST_LEN, MLA_ROPE)
            c_kpe = jnp.concatenate([c_kpe, c_kpe], axis=-1)
            knc, kpc, vc = _mla_expand(c_ckv, c_kpe, w_ukv, _row(kg[:MLA_NOPE]), _pair(kg[MLA_NOPE:]),
                                       tab_64, None)
            attn_p, attn_s = _mla_attend(qn, qp, kn, kp, vv, knc, kpc, vc)
            w_o = mla_w_o[0]
        xp, xs = _omlp(attn_p, attn_s, w_o, xp, xs, mods, gain_ffn, mlp_w1, mlp_w2, layer)

    y_prompt = xp.reshape(BATCH, SEQ, D_MODEL)
    y_sample = xs.reshape(DEC_BATCH, DEC_SEQ, D_MODEL)
    return (y_prompt, y_sample, outs["att_k"], outs["att_v"], outs["diff_k"], outs["diff_v"],
            outs["swa_k"], outs["swa_v"], outs["mla_ckv"], outs["mla_kpe"])
```

```python
import functools
import math

import numpy as np
import jax
import jax.numpy as jnp
from jax import lax
from jax.experimental import pallas as pl
from jax.experimental.pallas import tpu as pltpu

D_MODEL = 1024
BATCH = 16
SEQ = 256
DEPTH = 4
DEC_BATCH = 2
DEC_SEQ = 1024
PAST_LEN = 256
GRID_W = 64
ROPE_THETA = 10000.0
EPS = 1e-6
D_FF = 4 * D_MODEL
MOD_CHUNKS = 6
LOG2E = 1.4426950408889634

ATT_HEADS, ATT_KV_HEADS, ATT_HEAD_DIM = 8, 2, 128
DIFF_HEADS, DIFF_HEAD_DIM = 8, 64
SWA_HEADS, SWA_KV_HEADS, SWA_HEAD_DIM, WINDOW = 16, 4, 64, 128
MLA_HEADS, MLA_NOPE, MLA_ROPE, MLA_VDIM = 8, 128, 64, 128
MLA_Q_RANK, MLA_KV_RANK = 512, 256

LANES = 128
HALF = LANES // 2
TM = 256
N_PROMPT_TOK = BATCH * SEQ
N_LAT_TOK = DEC_BATCH * DEC_SEQ
N_TOK = N_PROMPT_TOK + N_LAT_TOK
N_PROMPT_TILES = N_PROMPT_TOK // TM
TILES_PER_DEC = DEC_SEQ // TM
LAT_BLOCK0 = N_PROMPT_TOK // DEC_SEQ
COND_ROWS = 8
PROJ_TM = 512
PROJ_BATCHES = PROJ_TM // SEQ
N_PROJ_TILES = N_TOK // PROJ_TM
N_PROJ_PROMPT = N_PROMPT_TOK // PROJ_TM
PROJ_UNIT = 2 * LANES
MLP_TM = 512
MLP_FF_CHUNK = 1024
MLP_LOAD_COLS = 256
N_W_LOADS = D_FF // MLP_LOAD_COLS
N_LOAD_STEPS = 2 * N_W_LOADS
N_MLP_PROMPT_TILES = N_PROMPT_TOK // MLP_TM
SWA_QB = 128
ATT_UNIT_HEADS = 4
VMEM_LIMIT = 56 * 1024 * 1024

F32 = jnp.float32
BF16 = jnp.bfloat16


def _cparams(n_axes):
    return pltpu.CompilerParams(dimension_semantics=("arbitrary",) * n_axes,
                                vmem_limit_bytes=VMEM_LIMIT)


def _dot(a, b):
    return jnp.dot(a, b, preferred_element_type=F32)


def _dot_nt(a, b):
    return lax.dot_general(a, b, (((1,), (1,)), ((), ())), preferred_element_type=F32)


def _dot_tn(a, b):
    return lax.dot_general(a, b, (((0,), (0,)), ((), ())), preferred_element_type=F32)


def _const_spec(shape):
    nd = len(shape)
    return pl.BlockSpec(shape, lambda *_: (0,) * nd, pipeline_mode=pl.Buffered(1))


def _chunk(ref, c, width=LANES):
    return ref[:, c * width:(c + 1) * width]


def _put(ref, c, val):
    ref[:, c * LANES:(c + 1) * LANES] = val.astype(ref.dtype)


def _tile_group(i, rows):
    n_prompt = N_PROMPT_TOK // rows
    return jnp.where(i < n_prompt, 0, 1 + (i - n_prompt) // (DEC_SEQ // rows))


def _rope_tile(i):
    return jnp.maximum(i - N_PROJ_PROMPT, 0) % (DEC_SEQ // PROJ_TM)


def _norm_mod(x, gain, shift, scale):
    ms = jnp.mean(x * x, axis=-1, keepdims=True)
    return x * lax.rsqrt(ms + EPS) * (gain * (1.0 + scale)) + shift


def _lane_lo(shape):
    return lax.broadcasted_iota(jnp.int32, shape, len(shape) - 1) < HALF


def _rope(y, cos, sin_prev, sin_next, quarter):
    return (y * cos + pltpu.roll(y, quarter, 1) * sin_prev
            + pltpu.roll(y, LANES - quarter, 1) * sin_next)


def _rope_tables(rot_dim):
    half = rot_dim // 2
    quarter = rot_dim // 4
    inv = np.float32(ROPE_THETA) ** (-np.arange(0, half, 2, dtype=np.float32) / np.float32(half))
    pos = np.arange(DEC_SEQ)
    row = (pos // GRID_W).astype(np.float32)
    col = (pos % GRID_W).astype(np.float32)
    lane = np.arange(LANES)
    dd = lane % rot_dim
    q = dd // quarter
    f = dd % quarter
    ang = np.where((q < 2)[None, :], row[:, None], col[:, None]) * inv[f][None, :]
    ang = ang.astype(np.float32)
    cos = np.cos(ang).astype(np.float32)
    sin = np.sin(ang).astype(np.float32)
    odd = (q % 2 == 1)[None, :]
    sin_prev = np.where(odd, sin, 0.0).astype(np.float32)
    sin_next = np.where(odd, 0.0, -sin).astype(np.float32)
    return jnp.asarray(cos), jnp.asarray(sin_prev), jnp.asarray(sin_next)


def _lane_sum_matrix(rows, cols, value=1.0):
    lane = np.arange(LANES)
    m = np.where(rows(lane)[:, None] & cols(lane)[None, :], value, 0.0).astype(np.float32)
    return jnp.asarray(m, dtype=BF16)


def _group_mean_matrix(group):
    lane = np.arange(PROJ_UNIT)
    m = np.where((lane[:, None] // group) == (lane[None, :] // group), 1.0 / group, 0.0)
    return jnp.asarray(m.astype(np.float32), dtype=BF16)


def _group_sum_matrix():
    lane = np.arange(PROJ_UNIT)
    m = np.where((lane[:, None] // LANES) == (lane[None, :] // LANES), 1.0, 0.0)
    return jnp.asarray(m.astype(np.float32), dtype=BF16)


def _sq_bf16(y):
    return (y * y).astype(BF16)


def _head_norm(y, m_ref, gain):
    return y * lax.rsqrt(_dot(_sq_bf16(y), m_ref[...]) + EPS) * gain


def _halves(y):
    return [y[:, t * LANES:(t + 1) * LANES] for t in range(y.shape[1] // LANES)]


def _matmul_units(h, w_ref, n_units, width, emit):
    def unit(u):
        return _dot(h, w_ref[:, u * width:(u + 1) * width])

    nxt = unit(0)
    for u in range(n_units):
        cur = nxt
        if u + 1 < n_units:
            nxt = unit(u + 1)
        emit(u, cur)


def _by_tile_kind(i, body):
    pl.when(i < N_PROJ_PROMPT)(functools.partial(body, False))
    pl.when(i >= N_PROJ_PROMPT)(functools.partial(body, True))


def _rope_args(lat, cos_ref, sp_ref, sn_ref, rot_dim):
    return (cos_ref[...], sp_ref[...], sn_ref[...], rot_dim // 4) if lat else None


def _maybe_rope(y, rope):
    return y if rope is None else _rope(y, *rope)


def _cache_rows(ref, index, val):
    for b in range(PROJ_BATCHES):
        ref[(b, 0) + tuple(index)] = val[b * SEQ:(b + 1) * SEQ]


def _softmax2_parts(s_list, extra=None):
    m = jnp.max(s_list[0], axis=0, keepdims=True)
    for s in s_list[1:]:
        m = jnp.maximum(m, jnp.max(s, axis=0, keepdims=True))
    if extra is not None:
        m = jnp.maximum(m, extra)
    ps = [jnp.exp2(s - m) for s in s_list]
    tot = ps[0].sum(axis=0, keepdims=True)
    for p in ps[1:]:
        tot = tot + p.sum(axis=0, keepdims=True)
    if extra is not None:
        tot = tot + jnp.exp2(extra - m)
    return ps, 1.0 / tot


def _head_pipeline(n, scores, finish):
    nxt = scores(0)
    for h in range(n):
        cur = nxt
        if h + 1 < n:
            nxt = scores(h + 1)
        finish(h, cur)


def _pv(ps, values):
    o = None
    for p, v in zip(ps, values):
        t = _dot_tn(v, p.astype(BF16))
        o = t if o is None else o + t
    return o


def _split_halves(q):
    lo = _lane_lo(q.shape)
    zero = jnp.zeros_like(q)
    return jnp.where(lo, q, zero), jnp.where(lo, zero, q)


def _mod_kernel(cond_ref, w_ref, b_ref, o_ref):
    c = cond_ref[...]
    s = (c * jax.nn.sigmoid(c)).astype(BF16)
    o_ref[0] = _dot(s, w_ref[0].astype(BF16)) + b_ref[0]


def _modulation(cond, ada_w, ada_b):
    tn = 1536
    n = MOD_CHUNKS * D_MODEL
    return pl.pallas_call(
        _mod_kernel,
        grid=(DEPTH, n // tn),
        in_specs=[
            pl.BlockSpec((COND_ROWS, D_MODEL), lambda l, j: (0, 0)),
            pl.BlockSpec((1, D_MODEL, tn), lambda l, j: (l, 0, j)),
            pl.BlockSpec((1, 1, tn), lambda l, j: (l, 0, j)),
        ],
        out_specs=pl.BlockSpec((1, COND_ROWS, tn), lambda l, j: (l, 0, j)),
        out_shape=jax.ShapeDtypeStruct((DEPTH, COND_ROWS, n), F32),
        compiler_params=_cparams(2),
        name="modulation",
    )(cond, ada_w, ada_b.reshape(DEPTH, 1, n))


def _mod_spec(chunk):
    return pl.BlockSpec((COND_ROWS, D_MODEL), lambda i: (0, chunk))


def _mod_row(ref, i):
    return ref[pl.ds(_tile_group(i, PROJ_TM), 1), :]


_ROPE_SPEC = pl.BlockSpec((PROJ_TM, LANES), lambda i: (_rope_tile(i), 0))
_LANE_MAT_SPEC = _const_spec((LANES, LANES))
_UNIT_MAT_SPEC = _const_spec((PROJ_UNIT, PROJ_UNIT))


def _tok_spec(width):
    return pl.BlockSpec((PROJ_TM, width), lambda i: (i, 0))


_XP_SPEC = pl.BlockSpec((PROJ_TM, D_MODEL), lambda i: (jnp.minimum(i, N_PROJ_PROMPT - 1), 0))
_XS_SPEC = pl.BlockSpec((PROJ_TM, D_MODEL), lambda i: (jnp.maximum(i - N_PROJ_PROMPT, 0), 0))


def _cache_spec(*dims):
    nd = len(dims)
    return pl.BlockSpec((PROJ_BATCHES, 1) + dims,
                        lambda i: (jnp.minimum(i, N_PROJ_PROMPT - 1), 0) + (0,) * nd)


def _cache_shape(*dims):
    return jax.ShapeDtypeStruct((BATCH, 1) + dims, F32)


def _proj_att_kernel(xp_ref, xs_ref, gain_ref, sh_ref, sc_ref, w_ref, qg_ref, kg_ref, m_ref,
                     cos_ref, sp_ref, sn_ref, q_ref, k_ref, v_ref, ck_ref, cv_ref):
    i = pl.program_id(0)
    x = jnp.where(i < N_PROJ_PROMPT, xp_ref[...], xs_ref[...])
    h = _norm_mod(x, gain_ref[...], _mod_row(sh_ref, i), _mod_row(sc_ref, i)).astype(BF16)
    per = PROJ_UNIT // LANES
    nq, nk = ATT_HEADS // per, ATT_KV_HEADS // per

    def body(lat):
        rope = _rope_args(lat, cos_ref, sp_ref, sn_ref, ATT_HEAD_DIM)

        def emit(u, y):
            if u < nq + nk:
                y = _head_norm(y, m_ref, qg_ref[...] if u < nq else kg_ref[...])
            for t, yc in enumerate(_halves(y)):
                if u < nq:
                    _put(q_ref, u * per + t, _maybe_rope(yc, rope))
                elif u < nq + nk:
                    kn = _maybe_rope(yc, rope)
                    _put(k_ref, (u - nq) * per + t, kn)
                    if not lat:
                        _cache_rows(ck_ref, [(u - nq) * per + t], kn)
                else:
                    _put(v_ref, (u - nq - nk) * per + t, yc)
                    if not lat:
                        _cache_rows(cv_ref, [(u - nq - nk) * per + t], yc)

        _matmul_units(h, w_ref, nq + 2 * nk, PROJ_UNIT, emit)

    _by_tile_kind(i, body)


def _proj_att(xp, xs, mods, gain, w, qg, kg, tables):
    nq, nk = ATT_HEADS * ATT_HEAD_DIM, ATT_KV_HEADS * ATT_HEAD_DIM
    return pl.pallas_call(
        _proj_att_kernel,
        grid=(N_PROJ_TILES,),
        in_specs=[_XP_SPEC, _XS_SPEC, _const_spec((1, D_MODEL)), _mod_spec(0), _mod_spec(1),
                  _const_spec(w.shape), _const_spec((1, PROJ_UNIT)), _const_spec((1, PROJ_UNIT)),
                  _UNIT_MAT_SPEC, _ROPE_SPEC, _ROPE_SPEC, _ROPE_SPEC],
        out_specs=[_tok_spec(nq), _tok_spec(nk), _tok_spec(nk),
                   _cache_spec(ATT_KV_HEADS, SEQ, ATT_HEAD_DIM), _cache_spec(ATT_KV_HEADS, SEQ, ATT_HEAD_DIM)],
        out_shape=[jax.ShapeDtypeStruct((N_TOK, nq), BF16),
                   jax.ShapeDtypeStruct((N_TOK, nk), BF16),
                   jax.ShapeDtypeStruct((N_TOK, nk), BF16),
                   _cache_shape(ATT_KV_HEADS, SEQ, ATT_HEAD_DIM), _cache_shape(ATT_KV_HEADS, SEQ, ATT_HEAD_DIM)],
        compiler_params=_cparams(1),
        name="proj_att",
    )(xp, xs, gain, mods, mods, w, qg, kg, _group_mean_matrix(ATT_HEAD_DIM), *tables)


def _proj_diff_kernel(h_ref, w_ref, qg_ref, kg_ref, m_ref,
                      cos_ref, sp_ref, sn_ref, q_ref, k_ref, v_ref, ck_ref, cv_ref):
    i = pl.program_id(0)
    h = h_ref[...]
    per = PROJ_UNIT // LANES
    nu = DIFF_HEADS // per

    def body(lat):
        rope = _rope_args(lat, cos_ref, sp_ref, sn_ref, DIFF_HEAD_DIM)

        def emit(u, y):
            if u < 2 * nu:
                y = _head_norm(y, m_ref, qg_ref[...] if u < nu else kg_ref[...])
            for t, yc in enumerate(_halves(y)):
                hd = (u % nu) * per + t
                if u < nu:
                    _put(q_ref, hd, _maybe_rope(yc, rope))
                elif u < 2 * nu:
                    kn = _maybe_rope(yc, rope)
                    _put(k_ref, hd, kn)
                    if not lat:
                        _cache_rows(ck_ref, [hd, 0], kn[:, :HALF])
                        _cache_rows(ck_ref, [hd, 1], kn[:, HALF:])
                else:
                    _put(v_ref, hd, yc)
                    if not lat:
                        _cache_rows(cv_ref, [hd], yc)

        _matmul_units(h, w_ref, 3 * nu, PROJ_UNIT, emit)

    _by_tile_kind(i, body)


def _proj_diff(h, w, qg, kg, tables):
    n = DIFF_HEADS * 2 * DIFF_HEAD_DIM
    return pl.pallas_call(
        _proj_diff_kernel,
        grid=(N_PROJ_TILES,),
        in_specs=[_tok_spec(D_MODEL),
                  _const_spec(w.shape), _const_spec((1, PROJ_UNIT)), _const_spec((1, PROJ_UNIT)),
                  _UNIT_MAT_SPEC, _ROPE_SPEC, _ROPE_SPEC, _ROPE_SPEC],
        out_specs=[_tok_spec(n), _tok_spec(n), _tok_spec(n),
                   _cache_spec(DIFF_HEADS, 2, SEQ, DIFF_HEAD_DIM), _cache_spec(DIFF_HEADS, SEQ, 2 * DIFF_HEAD_DIM)],
        out_shape=[jax.ShapeDtypeStruct((N_TOK, n), BF16)] * 3
                  + [_cache_shape(DIFF_HEADS, 2, SEQ, DIFF_HEAD_DIM),
                     _cache_shape(DIFF_HEADS, SEQ, 2 * DIFF_HEAD_DIM)],
        compiler_params=_cparams(1),
        name="proj_diff",
    )(h, w, qg, kg, _group_mean_matrix(DIFF_HEAD_DIM), *tables)


def _dup_halves(yc):
    lo = _lane_lo(yc.shape)
    sw = pltpu.roll(yc, HALF, 1)
    return jnp.where(lo, yc, sw), jnp.where(lo, sw, yc)


def _proj_swa_kernel(h_ref, w_ref, qg_ref, kg_ref, m_ref,
                     cos_ref, sp_ref, sn_ref, q_ref, kd_ref, vd_ref, ck_ref, cv_ref):
    i = pl.program_id(0)
    h = h_ref[...]
    per = PROJ_UNIT // LANES
    nq = SWA_HEADS * SWA_HEAD_DIM // PROJ_UNIT
    nk = SWA_KV_HEADS * SWA_HEAD_DIM // PROJ_UNIT

    def body(lat):
        rope = _rope_args(lat, cos_ref, sp_ref, sn_ref, SWA_HEAD_DIM)

        def emit(u, y):
            if u < nq + nk:
                y = _head_norm(y, m_ref, qg_ref[...] if u < nq else kg_ref[...])
            for t, yc in enumerate(_halves(y)):
                if u < nq:
                    _put(q_ref, u * per + t, _maybe_rope(yc, rope))
                    continue
                if u < nq + nk:
                    j, c_ref, d_ref = (u - nq) * per + t, ck_ref, kd_ref
                    yc = _maybe_rope(yc, rope)
                else:
                    j, c_ref, d_ref = (u - nq - nk) * per + t, cv_ref, vd_ref
                for a, dup in enumerate(_dup_halves(yc)):
                    _put(d_ref, 2 * j + a, dup)
                    if not lat:
                        _cache_rows(c_ref, [2 * j + a], dup[:, :HALF])

        _matmul_units(h, w_ref, nq + 2 * nk, PROJ_UNIT, emit)

    _by_tile_kind(i, body)


def _proj_swa(h, w, qg, kg, tables):
    nq, nk = SWA_HEADS * SWA_HEAD_DIM, SWA_KV_HEADS * SWA_HEAD_DIM
    return pl.pallas_call(
        _proj_swa_kernel,
        grid=(N_PROJ_TILES,),
        in_specs=[_tok_spec(D_MODEL),
                  _const_spec(w.shape), _const_spec((1, PROJ_UNIT)), _const_spec((1, PROJ_UNIT)),
                  _UNIT_MAT_SPEC, _ROPE_SPEC, _ROPE_SPEC, _ROPE_SPEC],
        out_specs=[_tok_spec(nq), _tok_spec(2 * nk), _tok_spec(2 * nk),
                   _cache_spec(SWA_KV_HEADS, SEQ, SWA_HEAD_DIM), _cache_spec(SWA_KV_HEADS, SEQ, SWA_HEAD_DIM)],
        out_shape=[jax.ShapeDtypeStruct((N_TOK, nq), BF16),
                   jax.ShapeDtypeStruct((N_TOK, 2 * nk), BF16),
                   jax.ShapeDtypeStruct((N_TOK, 2 * nk), BF16),
                   _cache_shape(SWA_KV_HEADS, SEQ, SWA_HEAD_DIM), _cache_shape(SWA_KV_HEADS, SEQ, SWA_HEAD_DIM)],
        compiler_params=_cparams(1),
        name="proj_swa",
    )(h, w, qg, kg, _group_mean_matrix(SWA_HEAD_DIM), *tables)


def _mla_lane_matrices():
    everything = lambda lane: lane >= 0
    return (_lane_sum_matrix(everything, everything),
            _lane_sum_matrix(lambda lane: lane < HALF, everything),
            _lane_sum_matrix(lambda lane: lane >= HALF, everything))


def _proj_mla_kernel(h_ref, w_in_ref, qa_ref, kva_ref, w_uq_ref,
                     qg_ref, qgp_ref, all_ref, lo_ref, hi_ref, cos_ref, sp_ref, sn_ref,
                     qn_ref, qp_ref, ckv_ref, kpe_ref, c_ckv_ref, c_kpe_ref):
    i = pl.program_id(0)
    y = _dot(h_ref[...], w_in_ref[...])
    c_q = y[:, :MLA_Q_RANK]
    c_kv = y[:, MLA_Q_RANK:MLA_Q_RANK + MLA_KV_RANK]
    kpe = y[:, MLA_Q_RANK + MLA_KV_RANK:]
    kpe_ref[...] = kpe
    ckv = c_kv * lax.rsqrt(jnp.mean(c_kv * c_kv, axis=-1, keepdims=True) + EPS) * kva_ref[...]
    ckv_ref[...] = ckv.astype(BF16)
    cq = (c_q * lax.rsqrt(jnp.mean(c_q * c_q, axis=-1, keepdims=True) + EPS) * qa_ref[...]).astype(BF16)
    lo = _lane_lo((PROJ_TM, LANES))
    inv_d = 1.0 / (MLA_NOPE + MLA_ROPE)

    def body(lat):
        if not lat:
            _cache_rows(c_ckv_ref, [], ckv)
            _cache_rows(c_kpe_ref, [], kpe[:, :MLA_ROPE])

        def emit(j, yq):
            pe = yq[:, 2 * LANES:]
            pe_sq = _sq_bf16(pe)
            rs = []
            for a, half_ref in enumerate((lo_ref, hi_ref)):
                nope = yq[:, a * LANES:(a + 1) * LANES]
                ss = _dot(_sq_bf16(nope), all_ref[...]) + _dot(pe_sq, half_ref[...])
                r = lax.rsqrt(ss * inv_d + EPS)
                rs.append(r)
                _put(qn_ref, 2 * j + a, nope * r * qg_ref[...])
            pe = pe * jnp.where(lo, rs[0], rs[1]) * qgp_ref[...]
            if lat:
                pe = _rope(pe, cos_ref[...], sp_ref[...], sn_ref[...], MLA_ROPE // 4)
            _put(qp_ref, j, pe)

        _matmul_units(cq, w_uq_ref, MLA_HEADS // 2, 3 * LANES, emit)

    _by_tile_kind(i, body)


def _proj_mla(h, w_in, qa, kva, w_uq, qg, qgp, tables):
    n_nope = MLA_HEADS * MLA_NOPE
    n_pe = MLA_HEADS * MLA_ROPE
    return pl.pallas_call(
        _proj_mla_kernel,
        grid=(N_PROJ_TILES,),
        in_specs=[_tok_spec(D_MODEL),
                  _const_spec(w_in.shape), _const_spec((1, MLA_Q_RANK)), _const_spec((1, MLA_KV_RANK)),
                  _const_spec(w_uq.shape), _const_spec((1, LANES)), _const_spec((1, LANES)),
                  _LANE_MAT_SPEC, _LANE_MAT_SPEC, _LANE_MAT_SPEC,
                  _ROPE_SPEC, _ROPE_SPEC, _ROPE_SPEC],
        out_specs=[_tok_spec(n_nope), _tok_spec(n_pe), _tok_spec(MLA_KV_RANK), _tok_spec(LANES),
                   _cache_spec(SEQ, MLA_KV_RANK), _cache_spec(SEQ, MLA_ROPE)],
        out_shape=[jax.ShapeDtypeStruct((N_TOK, n_nope), BF16),
                   jax.ShapeDtypeStruct((N_TOK, n_pe), BF16),
                   jax.ShapeDtypeStruct((N_TOK, MLA_KV_RANK), BF16),
                   jax.ShapeDtypeStruct((N_TOK, LANES), F32),
                   _cache_shape(SEQ, MLA_KV_RANK), _cache_shape(SEQ, MLA_ROPE)],
        compiler_params=_cparams(1),
        name="proj_mla",
    )(h, w_in, qa, kva, w_uq, qg, qgp, *_mla_lane_matrices(), *tables)


def _mla_expand_kernel(ckv_ref, kpe_ref, w_ref, kg_ref, kgp_ref, sum_ref, lo_ref,
                       cos_ref, sp_ref, sn_ref, kn_ref, kp_ref, v_ref, *, rope):
    i = pl.program_id(0)
    ckv = ckv_ref[...].astype(BF16)
    kpe = kpe_ref[...]
    pe_ss = _dot(_sq_bf16(kpe), lo_ref[...])
    pe_ss = jnp.concatenate([pe_ss, pe_ss], axis=1)
    lo = _lane_lo(kpe.shape)
    inv_d = 1.0 / (MLA_NOPE + MLA_ROPE)

    def body(lat):
        def emit(j, y):
            kn = jnp.concatenate([y[:, :LANES], y[:, 2 * LANES:3 * LANES]], axis=1)
            r = lax.rsqrt((_dot(_sq_bf16(kn), sum_ref[...]) + pe_ss) * inv_d + EPS)
            kn = kn * r * kg_ref[...]
            for a in range(2):
                _put(kn_ref, 2 * j + a, kn[:, a * LANES:(a + 1) * LANES])
                _put(v_ref, 2 * j + a, y[:, (2 * a + 1) * LANES:(2 * a + 2) * LANES])
            pe = kpe * jnp.where(lo, r[:, :LANES], r[:, LANES:]) * kgp_ref[...]
            if lat:
                pe = _rope(pe, cos_ref[...], sp_ref[...], sn_ref[...], MLA_ROPE // 4)
            _put(kp_ref, j, pe)

        _matmul_units(ckv, w_ref, MLA_HEADS // 2, 4 * LANES, emit)

    if rope:
        _by_tile_kind(i, body)
    else:
        body(False)


def _mla_expand(ckv, kpe_dup, w_ukv, kg, kgp, tables, rope):
    n = ckv.shape[0]
    n_nope = MLA_HEADS * MLA_NOPE
    n_pe = MLA_HEADS * MLA_ROPE
    _, m_lo, _ = _mla_lane_matrices()
    return pl.pallas_call(
        functools.partial(_mla_expand_kernel, rope=rope),
        grid=(n // PROJ_TM,),
        in_specs=[_tok_spec(MLA_KV_RANK), _tok_spec(LANES), _const_spec(w_ukv.shape),
                  _const_spec((1, PROJ_UNIT)), _const_spec((1, LANES)), _UNIT_MAT_SPEC, _LANE_MAT_SPEC,
                  _ROPE_SPEC, _ROPE_SPEC, _ROPE_SPEC],
        out_specs=[_tok_spec(n_nope), _tok_spec(n_pe), _tok_spec(n_nope)],
        out_shape=[jax.ShapeDtypeStruct((n, n_nope), BF16),
                   jax.ShapeDtypeStruct((n, n_pe), BF16),
                   jax.ShapeDtypeStruct((n, n_nope), BF16)],
        compiler_params=_cparams(1),
        name="mla_expand",
    )(ckv, kpe_dup, w_ukv, kg, kgp, _group_sum_matrix(), m_lo, *tables)


def _prompt_spec(width):
    return pl.BlockSpec((TM, width), lambda b: (b, 0))


def _latq_spec(rows, width):
    per = DEC_SEQ // rows
    return pl.BlockSpec((rows, width), lambda b, t: (N_PROMPT_TOK // rows + b * per + t, 0))


def _latkv_spec(width):
    return pl.BlockSpec((DEC_SEQ, width), lambda b, t: (LAT_BLOCK0 + b, 0))


def _lato_spec(rows):
    per = DEC_SEQ // rows
    return pl.BlockSpec((rows, D_MODEL), lambda b, t: (b * per + t, 0))


def _att_kernel(*refs, with_ctx):
    if with_ctx:
        q_ref, k_ref, v_ref, kc_ref, vc_ref, o_ref = refs
    else:
        q_ref, k_ref, v_ref, o_ref = refs
    tq = q_ref.shape[0]
    nu = ATT_UNIT_HEADS
    per_kv = ATT_HEADS // ATT_KV_HEADS // nu

    def scores(u):
        q = jnp.concatenate([_chunk(q_ref, u * nu + g) for g in range(nu)], axis=0)
        s_list = [_dot_nt(_chunk(k_ref, u // per_kv), q)]
        if with_ctx:
            s_list.append(_dot_nt(kc_ref[u // per_kv].astype(BF16), q))
        return s_list

    def finish(u, s_list):
        values = [_chunk(v_ref, u // per_kv)]
        if with_ctx:
            values.append(vc_ref[u // per_kv].astype(BF16))
        ps, inv = _softmax2_parts(s_list)
        o = _pv(ps, values) * inv
        for g in range(nu):
            o_ref[:, (u * nu + g) * LANES:(u * nu + g + 1) * LANES] = (
                o[:, g * tq:(g + 1) * tq].T.astype(o_ref.dtype))

    _head_pipeline(ATT_HEADS // nu, scores, finish)


def _att_attend(q, k, v, cache_k, cache_v):
    nk = ATT_KV_HEADS * ATT_HEAD_DIM
    out_p = pl.pallas_call(
        functools.partial(_att_kernel, with_ctx=False),
        grid=(N_PROMPT_TILES,),
        in_specs=[_prompt_spec(D_MODEL), _prompt_spec(nk), _prompt_spec(nk)],
        out_specs=_prompt_spec(D_MODEL),
        out_shape=jax.ShapeDtypeStruct((N_PROMPT_TOK, D_MODEL), BF16),
        compiler_params=_cparams(1),
        name="att_prompt",
    )(q, k, v)
    ctx = pl.BlockSpec((None, None, ATT_KV_HEADS, PAST_LEN, LANES), lambda b, t: (b, 0, 0, 0, 0))
    out_s = pl.pallas_call(
        functools.partial(_att_kernel, with_ctx=True),
        grid=(DEC_BATCH, TILES_PER_DEC),
        in_specs=[_latq_spec(TM, D_MODEL), _latkv_spec(nk), _latkv_spec(nk), ctx, ctx],
        out_specs=_lato_spec(TM),
        out_shape=jax.ShapeDtypeStruct((N_LAT_TOK, D_MODEL), BF16),
        compiler_params=_cparams(2),
        name="att_latent",
    )(q, k, v, cache_k, cache_v)
    return out_p, out_s


def _diff_kernel(*refs, lam_init, with_ctx):
    if with_ctx:
        (q_ref, k_ref, v_ref, kc_ref, vc_ref, lq1_ref, lk1_ref, lq2_ref, lk2_ref, sub_ref, o_ref) = refs
    else:
        (q_ref, k_ref, v_ref, lq1_ref, lk1_ref, lq2_ref, lk2_ref, sub_ref, o_ref) = refs
    tq = q_ref.shape[0]
    lam = (jnp.exp(jnp.sum(lq1_ref[...] * lk1_ref[...], axis=-1, keepdims=True))
           - jnp.exp(jnp.sum(lq2_ref[...] * lk2_ref[...], axis=-1, keepdims=True)) + lam_init)
    sub = sub_ref[...] * (1.0 - lam_init)

    def scores(hd):
        q = jnp.concatenate(_split_halves(_chunk(q_ref, hd)), axis=0)
        s_list = [_dot_nt(_chunk(k_ref, hd), q)]
        if with_ctx:
            s_list.append(_dot_nt(kc_ref[hd].astype(BF16), q))
        return s_list

    def finish(hd, s_list):
        values = [_chunk(v_ref, hd)]
        if with_ctx:
            values.append(vc_ref[hd].astype(BF16))
        ps, inv = _softmax2_parts(s_list)
        c0 = inv[:, :tq]
        c1 = -lam * inv[:, tq:]
        o = _pv([p[:, :tq] * c0 + p[:, tq:] * c1 for p in ps], values)
        o = o * lax.rsqrt(jnp.mean(o * o, axis=0, keepdims=True) + EPS) * sub
        o_ref[:, hd * LANES:(hd + 1) * LANES] = o.T.astype(o_ref.dtype)

    _head_pipeline(DIFF_HEADS, scores, finish)


def _diff_attend(q, k, v, cache_k_pair, cache_v, lq1, lk1, lq2, lk2, subln, lam_init):
    small = [lq1, lk1, lq2, lk2, subln]
    small_specs = [_const_spec(s.shape) for s in small]
    out_p = pl.pallas_call(
        functools.partial(_diff_kernel, lam_init=lam_init, with_ctx=False),
        grid=(N_PROMPT_TILES,),
        in_specs=[_prompt_spec(D_MODEL)] * 3 + small_specs,
        out_specs=_prompt_spec(D_MODEL),
        out_shape=jax.ShapeDtypeStruct((N_PROMPT_TOK, D_MODEL), BF16),
        compiler_params=_cparams(1),
        name="diff_prompt",
    )(q, k, v, *small)
    out_s = pl.pallas_call(
        functools.partial(_diff_kernel, lam_init=lam_init, with_ctx=True),
        grid=(DEC_BATCH, TILES_PER_DEC),
        in_specs=[_latq_spec(TM, D_MODEL), _latkv_spec(D_MODEL), _latkv_spec(D_MODEL),
                  pl.BlockSpec((None, DIFF_HEADS, PAST_LEN, LANES), lambda b, t: (b, 0, 0, 0)),
                  pl.BlockSpec((None, None, DIFF_HEADS, PAST_LEN, LANES), lambda b, t: (b, 0, 0, 0, 0))]
                 + small_specs,
        out_specs=_lato_spec(TM),
        out_shape=jax.ShapeDtypeStruct((N_LAT_TOK, D_MODEL), BF16),
        compiler_params=_cparams(2),
        name="diff_latent",
    )(q, k, v, cache_k_pair, cache_v, *small)
    return out_p, out_s


def _swa_pipeline(q_ref, sink_ref, score_fns, value_fns, o_ref):
    tq = q_ref.shape[0]
    per_kv = SWA_HEADS // SWA_KV_HEADS // 2
    first = lax.broadcasted_iota(jnp.int32, (LANES, tq), 0) < HALF

    def scores(c):
        q = jnp.concatenate(_split_halves(_chunk(q_ref, c)), axis=0)
        return [fn(c // per_kv, q) for fn in score_fns]

    def finish(c, s_list):
        sink = jnp.concatenate([jnp.full((1, tq), sink_ref[2 * c + a] * LOG2E, F32) for a in range(2)],
                               axis=1)
        ps, inv = _softmax2_parts(s_list, extra=sink)
        o = _pv(ps, [fn(c // per_kv) for fn in value_fns]) * inv
        oc = jnp.where(first, o[:, :tq], o[:, tq:])
        o_ref[:, c * LANES:(c + 1) * LANES] = oc.T.astype(o_ref.dtype)

    _head_pipeline(SWA_HEADS // 2, scores, finish)


def _swa_prompt_kernel(sink_ref, q_ref, k_ref, v_ref, o_ref):
    _swa_pipeline(q_ref, sink_ref, [lambda kv, q: _dot_nt(_chunk(k_ref, kv), q)],
                  [lambda kv: _chunk(v_ref, kv)], o_ref)


def _swa_latent_kernel(sink_ref, q_ref, k_ref, v_ref, kc_ref, vc_ref, o_ref):
    n = pl.program_id(1)
    tq = q_ref.shape[0]
    span = 3 * SWA_QB
    start = pl.multiple_of(jnp.clip((n - 1) * SWA_QB, 0, DEC_SEQ - span), SWA_QB)
    cols = lax.broadcasted_iota(jnp.int32, (span, 2 * tq), 1)
    qpos = n * SWA_QB + jnp.bitwise_and(cols, tq - 1)
    kpos = start + lax.broadcasted_iota(jnp.int32, (span, 2 * tq), 0)
    valid = jnp.abs(qpos - kpos) <= WINDOW

    def local(ref, kv):
        return ref[pl.ds(start, span), kv * LANES:(kv + 1) * LANES]

    _swa_pipeline(q_ref, sink_ref,
                  [lambda kv, q: jnp.where(valid, _dot_nt(local(k_ref, kv), q), -1e30),
                   lambda kv, q: _dot_nt(kc_ref[kv], q)],
                  [lambda kv: local(v_ref, kv), lambda kv: vc_ref[kv]], o_ref)


def _swa_attend(q, kd, vd, cache_kd, cache_vd, sink):
    nkd = 2 * SWA_KV_HEADS * SWA_HEAD_DIM
    smem = pl.BlockSpec(memory_space=pltpu.SMEM)
    out_p = pl.pallas_call(
        _swa_prompt_kernel,
        grid=(N_PROMPT_TILES,),
        in_specs=[smem, _prompt_spec(D_MODEL), _prompt_spec(nkd), _prompt_spec(nkd)],
        out_specs=_prompt_spec(D_MODEL),
        out_shape=jax.ShapeDtypeStruct((N_PROMPT_TOK, D_MODEL), BF16),
        compiler_params=_cparams(1),
        name="swa_prompt",
    )(sink, q, kd, vd)
    ctx = pl.BlockSpec((None, SWA_KV_HEADS, PAST_LEN, LANES), lambda b, n: (b, 0, 0, 0))
    out_s = pl.pallas_call(
        _swa_latent_kernel,
        grid=(DEC_BATCH, DEC_SEQ // SWA_QB),
        in_specs=[smem, _latq_spec(SWA_QB, D_MODEL), _latkv_spec(nkd), _latkv_spec(nkd), ctx, ctx],
        out_specs=_lato_spec(SWA_QB),
        out_shape=jax.ShapeDtypeStruct((N_LAT_TOK, D_MODEL), BF16),
        compiler_params=_cparams(2),
        name="swa_latent",
    )(sink, q, kd, vd, cache_kd, cache_vd)
    return out_p, out_s


def _mla_kernel(*refs, with_ctx):
    if with_ctx:
        (qn_ref, qp_ref, kn_ref, kp_ref, v_ref, knc_ref, kpc_ref, vc_ref, o_ref) = refs
    else:
        (qn_ref, qp_ref, kn_ref, kp_ref, v_ref, o_ref) = refs

    def scores(hd):
        j, a = hd // 2, hd % 2
        q = jnp.concatenate([_chunk(qn_ref, hd), _split_halves(_chunk(qp_ref, j))[a]], axis=1)
        s_list = [_dot_nt(jnp.concatenate([_chunk(kn_ref, hd), _chunk(kp_ref, j)], axis=1), q)]
        if with_ctx:
            s_list.append(_dot_nt(jnp.concatenate([_chunk(knc_ref, hd), _chunk(kpc_ref, j)], axis=1), q))
        return s_list

    def finish(hd, s_list):
        values = [_chunk(v_ref, hd)]
        if with_ctx:
            values.append(_chunk(vc_ref, hd))
        ps, inv = _softmax2_parts(s_list)
        o_ref[:, hd * LANES:(hd + 1) * LANES] = (_pv(ps, values) * inv).T.astype(o_ref.dtype)

    _head_pipeline(MLA_HEADS, scores, finish)


def _mla_attend(qn, qp, kn, kp, v, knc, kpc, vc):
    n_pe = MLA_HEADS * MLA_ROPE
    out_p = pl.pallas_call(
        functools.partial(_mla_kernel, with_ctx=False),
        grid=(N_PROMPT_TILES,),
        in_specs=[_prompt_spec(D_MODEL), _prompt_spec(n_pe), _prompt_spec(D_MODEL), _prompt_spec(n_pe),
                  _prompt_spec(D_MODEL)],
        out_specs=_prompt_spec(D_MODEL),
        out_shape=jax.ShapeDtypeStruct((N_PROMPT_TOK, D_MODEL), BF16),
        compiler_params=_cparams(1),
        name="mla_prompt",
    )(qn, qp, kn, kp, v)

    def ctx(width):
        return pl.BlockSpec((PAST_LEN, width), lambda b, t: (b, 0))

    out_s = pl.pallas_call(
        functools.partial(_mla_kernel, with_ctx=True),
        grid=(DEC_BATCH, TILES_PER_DEC),
        in_specs=[_latq_spec(TM, D_MODEL), _latq_spec(TM, n_pe),
                  _latkv_spec(D_MODEL), _latkv_spec(n_pe), _latkv_spec(D_MODEL),
                  ctx(D_MODEL), ctx(n_pe), ctx(D_MODEL)],
        out_specs=_lato_spec(TM),
        out_shape=jax.ShapeDtypeStruct((N_LAT_TOK, D_MODEL), BF16),
        compiler_params=_cparams(2),
        name="mla_latent",
    )(qn, qp, kn, kp, v, knc, kpc, vc)
    return out_p, out_s


def _omlp_kernel(*refs, emit_next):
    (ap_ref, as_ref, wo_ref, xp_ref, xs_ref, g1_ref, gain_ref, sh_ref, sc_ref, g2_ref,
     w1c_ref, w2c_ref) = refs[:12]
    if emit_next:
        ngain_ref, nsh_ref, nsc_ref, op_ref, os_ref, hn_ref, wo_s, w1_s, w2_s = refs[12:]
    else:
        op_ref, os_ref, wo_s, w1_s, w2_s = refs[12:]
    s = pl.program_id(0)
    per = MLP_FF_CHUNK // MLP_LOAD_COLS

    @pl.when(s == 0)
    def _():
        wo_s[...] = wo_ref[...].astype(BF16)

    for part in range(per):
        @pl.when((s < N_W_LOADS) & (s % per == part))
        def _(part=part):
            w1_s[s // per, :, part * MLP_LOAD_COLS:(part + 1) * MLP_LOAD_COLS] = w1c_ref[...].astype(BF16)

    @pl.when((s >= N_W_LOADS) & (s < N_LOAD_STEPS))
    def _():
        j = s - N_W_LOADS
        w2_s[j // per, pl.ds(pl.multiple_of((j % per) * MLP_LOAD_COLS, MLP_LOAD_COLS), MLP_LOAD_COLS), :] = (
            w2c_ref[...].astype(BF16))

    @pl.when(s >= N_LOAD_STEPS)
    def _():
        t = s - N_LOAD_STEPS
        is_prompt = t < N_MLP_PROMPT_TILES
        grp = _tile_group(t, MLP_TM)
        a = jnp.where(is_prompt, ap_ref[...], as_ref[...])
        x = jnp.where(is_prompt, xp_ref[...], xs_ref[...])
        x1 = x + g1_ref[pl.ds(grp, 1), :] * _dot(a, wo_s[...])
        h = _norm_mod(x1, gain_ref[...], sh_ref[pl.ds(grp, 1), :], sc_ref[pl.ds(grp, 1), :]).astype(BF16)
        acc = None
        for c in range(D_FF // MLP_FF_CHUNK):
            u = _dot(h, w1_s[c])
            u = jnp.square(jnp.maximum(u, 0.0)).astype(BF16)
            y = _dot(u, w2_s[c])
            acc = y if acc is None else acc + y
        out = x1 + g2_ref[pl.ds(grp, 1), :] * acc
        if emit_next:
            hn_ref[...] = _norm_mod(out, ngain_ref[...], nsh_ref[pl.ds(grp, 1), :],
                                    nsc_ref[pl.ds(grp, 1), :]).astype(BF16)

        @pl.when(is_prompt)
        def _():
            op_ref[...] = out

        @pl.when(jnp.logical_not(is_prompt))
        def _():
            os_ref[...] = out


def _omlp(attn_p, attn_s, w_o, xp, xs, mods, gain_ffn, w1_all, w2_all, layer, next_gain, next_mods):
    n_lat_tiles = N_LAT_TOK // MLP_TM
    emit_next = next_gain is not None

    def tok(s):
        return jnp.maximum(s - N_LOAD_STEPS, 0)

    p_spec = pl.BlockSpec((MLP_TM, D_MODEL), lambda s: (jnp.minimum(tok(s), N_MLP_PROMPT_TILES - 1), 0))
    l_spec = pl.BlockSpec((MLP_TM, D_MODEL),
                          lambda s: (jnp.clip(tok(s) - N_MLP_PROMPT_TILES, 0, n_lat_tiles - 1), 0))
    w1_spec = pl.BlockSpec((None, D_MODEL, MLP_LOAD_COLS),
                           lambda s: (layer, 0, jnp.minimum(s, N_W_LOADS - 1)))
    w2_spec = pl.BlockSpec((None, MLP_LOAD_COLS, D_MODEL),
                           lambda s: (layer, jnp.clip(s - N_W_LOADS, 0, N_W_LOADS - 1), 0))
    n_chunks = D_FF // MLP_FF_CHUNK
    in_specs = [p_spec, l_spec, _const_spec(w_o.shape), p_spec, l_spec, _mod_spec(2),
                _const_spec((1, D_MODEL)), _mod_spec(3), _mod_spec(4), _mod_spec(5), w1_spec, w2_spec]
    args = [attn_p, attn_s, w_o, xp, xs, mods, gain_ffn, mods, mods, mods, w1_all, w2_all]
    out_specs = [p_spec, l_spec]
    out_shape = [jax.ShapeDtypeStruct((N_PROMPT_TOK, D_MODEL), F32),
                 jax.ShapeDtypeStruct((N_LAT_TOK, D_MODEL), F32)]
    if emit_next:
        in_specs += [_const_spec((1, D_MODEL)), _mod_spec(0), _mod_spec(1)]
        args += [next_gain, next_mods, next_mods]
        out_specs.append(pl.BlockSpec((MLP_TM, D_MODEL), lambda s: (tok(s), 0)))
        out_shape.append(jax.ShapeDtypeStruct((N_TOK, D_MODEL), BF16))
    return pl.pallas_call(
        functools.partial(_omlp_kernel, emit_next=emit_next),
        grid=(N_LOAD_STEPS + N_TOK // MLP_TM,),
        in_specs=in_specs,
        out_specs=out_specs,
        out_shape=out_shape,
        scratch_shapes=[pltpu.VMEM((D_MODEL, D_MODEL), BF16),
                        pltpu.VMEM((n_chunks, D_MODEL, MLP_FF_CHUNK), BF16),
                        pltpu.VMEM((n_chunks, MLP_FF_CHUNK, D_MODEL), BF16)],
        compiler_params=_cparams(1),
        name="omlp",
    )(*args)


def _row(v, scale=1.0):
    return (v.astype(F32) * scale).reshape(1, -1)


def _pair(v, scale=1.0):
    return (jnp.concatenate([v, v]).astype(F32) * scale).reshape(1, LANES)


def _unit_gain(v, scale=1.0):
    return (jnp.tile(v.astype(F32), PROJ_UNIT // v.shape[0]) * scale).reshape(1, PROJ_UNIT)


def kernel(x_prompt, x_sample, cache_att_k, cache_att_v, cache_diff_k, cache_diff_v, cache_swa_k, cache_swa_v, cache_mla_ckv, cache_mla_kpe, c, c_ctx, ada_w, ada_b, norm_mix, norm_ffn, att_w_qkv, att_q_norm, att_k_norm, att_w_o, diff_w_qkv, diff_q_norm, diff_k_norm, diff_lq1, diff_lk1, diff_lq2, diff_lk2, diff_subln, diff_w_o, swa_w_qkv, swa_q_norm, swa_k_norm, swa_sink, swa_w_o, mla_w_in, mla_q_a_norm, mla_kv_a_norm, mla_w_uq, mla_w_ukv, mla_q_norm, mla_k_norm, mla_w_o, mlp_w1, mlp_w2):
    xp = x_prompt.reshape(N_PROMPT_TOK, D_MODEL)
    xs = x_sample.reshape(N_LAT_TOK, D_MODEL)
    cond = jnp.concatenate([c_ctx[None], c, jnp.zeros((COND_ROWS - 1 - DEC_BATCH, D_MODEL), F32)], axis=0)
    mods_all = _modulation(cond, ada_w, ada_b)

    tab_att = _rope_tables(ATT_HEAD_DIM)
    tab_64 = _rope_tables(DIFF_HEAD_DIM)

    outs = {}
    for layer in range(DEPTH):
        mods = mods_all[layer]
        gain_ffn = _row(norm_ffn[layer])
        if layer == 0:
            qs = ATT_HEAD_DIM ** -0.5 * LOG2E
            q, k, v, outs["att_k"], outs["att_v"] = _proj_att(
                xp, xs, mods, _row(norm_mix[layer]), att_w_qkv[0].astype(BF16),
                _unit_gain(att_q_norm[0], qs), _unit_gain(att_k_norm[0]), tab_att)
            attn_p, attn_s = _att_attend(q, k, v, cache_att_k, cache_att_v)
            w_o = att_w_o[0]
        elif layer == 1:
            qs = DIFF_HEAD_DIM ** -0.5 * LOG2E
            q, k, v, outs["diff_k"], outs["diff_v"] = _proj_diff(
                h, diff_w_qkv[0].astype(BF16),
                _unit_gain(diff_q_norm[0], qs), _unit_gain(diff_k_norm[0]), tab_64)
            lam_init = 0.8 - 0.6 * math.exp(-0.3 * layer)
            ck = cache_diff_k[:, 0].transpose(0, 1, 3, 2, 4).reshape(
                DEC_BATCH, DIFF_HEADS, PAST_LEN, LANES)
            attn_p, attn_s = _diff_attend(q, k, v, ck, cache_diff_v,
                                          _row(diff_lq1[0]), _row(diff_lk1[0]),
                                          _row(diff_lq2[0]), _row(diff_lk2[0]),
                                          diff_subln[0].astype(F32).reshape(LANES, 1), lam_init)
            w_o = diff_w_o[0]
        elif layer == 2:
            qs = SWA_HEAD_DIM ** -0.5 * LOG2E
            q, kd, vd, outs["swa_k"], outs["swa_v"] = _proj_swa(
                h, swa_w_qkv[0].astype(BF16),
                _unit_gain(swa_q_norm[0], qs), _unit_gain(swa_k_norm[0]), tab_64)
            ckd = jnp.concatenate([cache_swa_k[:, 0]] * 2, axis=-1).astype(BF16)
            cvd = jnp.concatenate([cache_swa_v[:, 0]] * 2, axis=-1).astype(BF16)
            attn_p, attn_s = _swa_attend(q, kd, vd, ckd, cvd, swa_sink[0].astype(F32))
            w_o = swa_w_o[0]
        else:
            qs = (MLA_NOPE + MLA_ROPE) ** -0.5 * LOG2E
            w_in = mla_w_in[0]
            w_in = jnp.concatenate([w_in, w_in[:, -MLA_ROPE:]], axis=1).astype(BF16)
            w_uq = mla_w_uq[0].reshape(MLA_Q_RANK, MLA_HEADS // 2, 2, MLA_NOPE + MLA_ROPE)
            w_uq = jnp.concatenate([w_uq[..., :MLA_NOPE].reshape(MLA_Q_RANK, MLA_HEADS // 2, 2 * MLA_NOPE),
                                    w_uq[..., MLA_NOPE:].reshape(MLA_Q_RANK, MLA_HEADS // 2, 2 * MLA_ROPE)],
                                   axis=-1).reshape(MLA_Q_RANK, -1).astype(BF16)
            w_ukv = mla_w_ukv[0].astype(BF16)
            qg, kg = mla_q_norm[0], mla_k_norm[0]
            qn, qp, ckv, kpe, outs["mla_ckv"], outs["mla_kpe"] = _proj_mla(
                h, w_in, _row(mla_q_a_norm[0]), _row(mla_kv_a_norm[0]), w_uq,
                _row(qg[:MLA_NOPE], qs), _pair(qg[MLA_NOPE:], qs), tab_64)
            kn, kp, vv = _mla_expand(ckv, kpe, w_ukv, _unit_gain(kg[:MLA_NOPE]), _pair(kg[MLA_NOPE:]),
                                     tab_64, True)
            c_ckv = cache_mla_ckv[:, 0].reshape(DEC_BATCH * PAST_LEN, MLA_KV_RANK)
            c_kpe = cache_mla_kpe[:, 0].reshape(DEC_BATCH * PAST_LEN, MLA_ROPE)
            c_kpe = jnp.concatenate([c_kpe, c_kpe], axis=-1)
            knc, kpc, vc = _mla_expand(c_ckv, c_kpe, w_ukv, _unit_gain(kg[:MLA_NOPE]), _pair(kg[MLA_NOPE:]),
                                       tab_64, False)
            attn_p, attn_s = _mla_attend(qn, qp, kn, kp, vv, knc, kpc, vc)
            w_o = mla_w_o[0]
        if layer + 1 < DEPTH:
            xp, xs, h = _omlp(attn_p, attn_s, w_o, xp, xs, mods, gain_ffn, mlp_w1, mlp_w2, layer,
                              _row(norm_mix[layer + 1]), mods_all[layer + 1])
        else:
            xp, xs = _omlp(attn_p, attn_s, w_o, xp, xs, mods, gain_ffn, mlp_w1, mlp_w2, layer, None, None)

    y_prompt = xp.reshape(BATCH, SEQ, D_MODEL)
    y_sample = xs.reshape(DEC_BATCH, DEC_SEQ, D_MODEL)
    return (y_prompt, y_sample, outs["att_k"], outs["att_v"], outs["diff_k"], outs["diff_v"],
            outs["swa_k"], outs["swa_v"], outs["mla_ckv"], outs["mla_kpe"])
```

```python
import functools
import math

import numpy as np
import jax
import jax.numpy as jnp
from jax import lax
from jax.experimental import pallas as pl
from jax.experimental.pallas import tpu as pltpu

D_MODEL = 1024
BATCH = 16
SEQ = 256
DEPTH = 4
DEC_BATCH = 2
DEC_SEQ = 1024
PAST_LEN = 256
GRID_W = 64
ROPE_THETA = 10000.0
EPS = 1e-6
D_FF = 4 * D_MODEL
MOD_CHUNKS = 6
LOG2E = 1.4426950408889634

ATT_HEADS, ATT_KV_HEADS, ATT_HEAD_DIM = 8, 2, 128
DIFF_HEADS, DIFF_HEAD_DIM = 8, 64
SWA_HEADS, SWA_KV_HEADS, SWA_HEAD_DIM, WINDOW = 16, 4, 64, 128
MLA_HEADS, MLA_NOPE, MLA_ROPE, MLA_VDIM = 8, 128, 64, 128
MLA_Q_RANK, MLA_KV_RANK = 512, 256

LANES = 128
HALF = LANES // 2
TM = 256
N_PROMPT_TOK = BATCH * SEQ
N_LAT_TOK = DEC_BATCH * DEC_SEQ
N_TOK = N_PROMPT_TOK + N_LAT_TOK
N_PROMPT_TILES = N_PROMPT_TOK // TM
TILES_PER_DEC = DEC_SEQ // TM
LAT_BLOCK0 = N_PROMPT_TOK // DEC_SEQ
COND_ROWS = 8
PROJ_TM = 512
PROJ_BATCHES = PROJ_TM // SEQ
N_PROJ_TILES = N_TOK // PROJ_TM
N_PROJ_PROMPT = N_PROMPT_TOK // PROJ_TM
PROJ_UNIT = 2 * LANES
MLP_TM = 512
MLP_FF_CHUNK = 1024
MLP_LOAD_COLS = 256
N_LOAD_STEPS = D_FF // MLP_LOAD_COLS
N_MLP_PROMPT_TILES = N_PROMPT_TOK // MLP_TM
SWA_QB = 128
ATT_UNIT_HEADS = 4
VMEM_LIMIT = 56 * 1024 * 1024

F32 = jnp.float32
BF16 = jnp.bfloat16


def _cparams(n_axes):
    return pltpu.CompilerParams(dimension_semantics=("arbitrary",) * n_axes,
                                vmem_limit_bytes=VMEM_LIMIT)


def _dot(a, b):
    return jnp.dot(a, b, preferred_element_type=F32)


def _dot_nt(a, b):
    return lax.dot_general(a, b, (((1,), (1,)), ((), ())), preferred_element_type=F32)


def _dot_tn(a, b):
    return lax.dot_general(a, b, (((0,), (0,)), ((), ())), preferred_element_type=F32)


def _const_spec(shape):
    nd = len(shape)
    return pl.BlockSpec(shape, lambda *_: (0,) * nd, pipeline_mode=pl.Buffered(1))


def _chunk(ref, c, width=LANES):
    return ref[:, c * width:(c + 1) * width]


def _put(ref, c, val):
    ref[:, c * LANES:(c + 1) * LANES] = val.astype(ref.dtype)


def _tile_group(i, rows):
    n_prompt = N_PROMPT_TOK // rows
    return jnp.where(i < n_prompt, 0, 1 + (i - n_prompt) // (DEC_SEQ // rows))


def _rope_tile(i):
    return jnp.maximum(i - N_PROJ_PROMPT, 0) % (DEC_SEQ // PROJ_TM)


def _norm_mod(x, gain, shift, scale):
    ms = jnp.mean(x * x, axis=-1, keepdims=True)
    return x * lax.rsqrt(ms + EPS) * (gain * (1.0 + scale)) + shift


def _lane_lo(shape):
    return lax.broadcasted_iota(jnp.int32, shape, len(shape) - 1) < HALF


def _rope(y, cos, sin_prev, sin_next, quarter):
    return (y * cos + pltpu.roll(y, quarter, 1) * sin_prev
            + pltpu.roll(y, LANES - quarter, 1) * sin_next)


def _rope_tables(rot_dim):
    half = rot_dim // 2
    quarter = rot_dim // 4
    inv = np.float32(ROPE_THETA) ** (-np.arange(0, half, 2, dtype=np.float32) / np.float32(half))
    pos = np.arange(DEC_SEQ)
    row = (pos // GRID_W).astype(np.float32)
    col = (pos % GRID_W).astype(np.float32)
    lane = np.arange(LANES)
    dd = lane % rot_dim
    q = dd // quarter
    f = dd % quarter
    ang = np.where((q < 2)[None, :], row[:, None], col[:, None]) * inv[f][None, :]
    ang = ang.astype(np.float32)
    cos = np.cos(ang).astype(np.float32)
    sin = np.sin(ang).astype(np.float32)
    odd = (q % 2 == 1)[None, :]
    sin_prev = np.where(odd, sin, 0.0).astype(np.float32)
    sin_next = np.where(odd, 0.0, -sin).astype(np.float32)
    return jnp.asarray(cos), jnp.asarray(sin_prev), jnp.asarray(sin_next)


def _lane_sum_matrix(rows, cols, value=1.0):
    lane = np.arange(LANES)
    m = np.where(rows(lane)[:, None] & cols(lane)[None, :], value, 0.0).astype(np.float32)
    return jnp.asarray(m, dtype=BF16)


def _group_mean_matrix(group):
    lane = np.arange(PROJ_UNIT)
    m = np.where((lane[:, None] // group) == (lane[None, :] // group), 1.0 / group, 0.0)
    return jnp.asarray(m.astype(np.float32), dtype=BF16)


def _group_sum_matrix():
    lane = np.arange(PROJ_UNIT)
    m = np.where((lane[:, None] // LANES) == (lane[None, :] // LANES), 1.0, 0.0)
    return jnp.asarray(m.astype(np.float32), dtype=BF16)


def _sq_bf16(y):
    return (y * y).astype(BF16)


def _head_norm(y, m_ref, gain):
    return y * lax.rsqrt(_dot(_sq_bf16(y), m_ref[...]) + EPS) * gain


def _halves(y):
    return [y[:, t * LANES:(t + 1) * LANES] for t in range(y.shape[1] // LANES)]


def _matmul_units(h, w_ref, n_units, width, emit):
    def unit(u):
        return _dot(h, w_ref[:, u * width:(u + 1) * width])

    nxt = unit(0)
    for u in range(n_units):
        cur = nxt
        if u + 1 < n_units:
            nxt = unit(u + 1)
        emit(u, cur)


def _by_tile_kind(i, body):
    pl.when(i < N_PROJ_PROMPT)(functools.partial(body, False))
    pl.when(i >= N_PROJ_PROMPT)(functools.partial(body, True))


def _rope_args(lat, cos_ref, sp_ref, sn_ref, rot_dim):
    return (cos_ref[...], sp_ref[...], sn_ref[...], rot_dim // 4) if lat else None


def _maybe_rope(y, rope):
    return y if rope is None else _rope(y, *rope)


def _cache_rows(ref, index, val):
    for b in range(PROJ_BATCHES):
        ref[(b, 0) + tuple(index)] = val[b * SEQ:(b + 1) * SEQ]


def _softmax2_parts(s_list, extra=None):
    m = jnp.max(s_list[0], axis=0, keepdims=True)
    for s in s_list[1:]:
        m = jnp.maximum(m, jnp.max(s, axis=0, keepdims=True))
    if extra is not None:
        m = jnp.maximum(m, extra)
    ps = [jnp.exp2(s - m) for s in s_list]
    mass = ps[0].sum(axis=0, keepdims=True)
    for p in ps[1:]:
        mass = mass + p.sum(axis=0, keepdims=True)
    if extra is not None:
        mass = mass + jnp.exp2(extra - m)
    return [p.astype(BF16) for p in ps], 1.0 / mass


def _head_pipeline(n, scores, finish):
    nxt = scores(0)
    for h in range(n):
        cur = nxt
        if h + 1 < n:
            nxt = scores(h + 1)
        finish(h, cur)


def _pv(ps, values):
    o = None
    for p, v in zip(ps, values):
        t = _dot_tn(v, p)
        o = t if o is None else o + t
    return o


def _split_halves(q):
    lo = _lane_lo(q.shape)
    zero = jnp.zeros_like(q)
    return jnp.where(lo, q, zero), jnp.where(lo, zero, q)


def _mod_kernel(cond_ref, w_ref, b_ref, o_ref):
    c = cond_ref[...]
    s = (c * jax.nn.sigmoid(c)).astype(BF16)
    o_ref[0] = _dot(s, w_ref[0].astype(BF16)) + b_ref[0]


def _modulation(cond, ada_w, ada_b):
    tn = 1536
    n = MOD_CHUNKS * D_MODEL
    return pl.pallas_call(
        _mod_kernel,
        grid=(DEPTH, n // tn),
        in_specs=[
            pl.BlockSpec((COND_ROWS, D_MODEL), lambda l, j: (0, 0)),
            pl.BlockSpec((1, D_MODEL, tn), lambda l, j: (l, 0, j)),
            pl.BlockSpec((1, 1, tn), lambda l, j: (l, 0, j)),
        ],
        out_specs=pl.BlockSpec((1, COND_ROWS, tn), lambda l, j: (l, 0, j)),
        out_shape=jax.ShapeDtypeStruct((DEPTH, COND_ROWS, n), F32),
        compiler_params=_cparams(2),
        name="modulation",
    )(cond, ada_w, ada_b.reshape(DEPTH, 1, n))


def _mod_spec(chunk):
    return pl.BlockSpec((COND_ROWS, D_MODEL), lambda i: (0, chunk))


def _mod_row(ref, i):
    return ref[pl.ds(_tile_group(i, PROJ_TM), 1), :]


_ROPE_SPEC = pl.BlockSpec((PROJ_TM, LANES), lambda i: (_rope_tile(i), 0))
_LANE_MAT_SPEC = _const_spec((LANES, LANES))
_UNIT_MAT_SPEC = _const_spec((PROJ_UNIT, PROJ_UNIT))


def _tok_spec(width):
    return pl.BlockSpec((PROJ_TM, width), lambda i: (i, 0))


_XP_SPEC = pl.BlockSpec((PROJ_TM, D_MODEL), lambda i: (jnp.minimum(i, N_PROJ_PROMPT - 1), 0))
_XS_SPEC = pl.BlockSpec((PROJ_TM, D_MODEL), lambda i: (jnp.maximum(i - N_PROJ_PROMPT, 0), 0))


def _cache_spec(*dims):
    nd = len(dims)
    return pl.BlockSpec((PROJ_BATCHES, 1) + dims,
                        lambda i: (jnp.minimum(i, N_PROJ_PROMPT - 1), 0) + (0,) * nd)


def _cache_shape(*dims):
    return jax.ShapeDtypeStruct((BATCH, 1) + dims, F32)


def _proj_att_kernel(xp_ref, xs_ref, gain_ref, sh_ref, sc_ref, w_ref, qg_ref, kg_ref, m_ref,
                     cos_ref, sp_ref, sn_ref, q_ref, k_ref, v_ref, ck_ref, cv_ref):
    i = pl.program_id(0)
    x = jnp.where(i < N_PROJ_PROMPT, xp_ref[...], xs_ref[...])
    h = _norm_mod(x, gain_ref[...], _mod_row(sh_ref, i), _mod_row(sc_ref, i)).astype(BF16)
    per = PROJ_UNIT // LANES
    nq, nk = ATT_HEADS // per, ATT_KV_HEADS // per

    def body(lat):
        rope = _rope_args(lat, cos_ref, sp_ref, sn_ref, ATT_HEAD_DIM)

        def emit(u, y):
            if u < nq + nk:
                y = _head_norm(y, m_ref, qg_ref[...] if u < nq else kg_ref[...])
            for t, yc in enumerate(_halves(y)):
                if u < nq:
                    _put(q_ref, u * per + t, _maybe_rope(yc, rope))
                elif u < nq + nk:
                    kn = _maybe_rope(yc, rope)
                    _put(k_ref, (u - nq) * per + t, kn)
                    if not lat:
                        _cache_rows(ck_ref, [(u - nq) * per + t], kn)
                else:
                    _put(v_ref, (u - nq - nk) * per + t, yc)
                    if not lat:
                        _cache_rows(cv_ref, [(u - nq - nk) * per + t], yc)

        _matmul_units(h, w_ref, nq + 2 * nk, PROJ_UNIT, emit)

    _by_tile_kind(i, body)


def _proj_att(xp, xs, mods, gain, w, qg, kg, tables):
    nq, nk = ATT_HEADS * ATT_HEAD_DIM, ATT_KV_HEADS * ATT_HEAD_DIM
    return pl.pallas_call(
        _proj_att_kernel,
        grid=(N_PROJ_TILES,),
        in_specs=[_XP_SPEC, _XS_SPEC, _const_spec((1, D_MODEL)), _mod_spec(0), _mod_spec(1),
                  _const_spec(w.shape), _const_spec((1, PROJ_UNIT)), _const_spec((1, PROJ_UNIT)),
                  _UNIT_MAT_SPEC, _ROPE_SPEC, _ROPE_SPEC, _ROPE_SPEC],
        out_specs=[_tok_spec(nq), _tok_spec(nk), _tok_spec(nk),
                   _cache_spec(ATT_KV_HEADS, SEQ, ATT_HEAD_DIM), _cache_spec(ATT_KV_HEADS, SEQ, ATT_HEAD_DIM)],
        out_shape=[jax.ShapeDtypeStruct((N_TOK, nq), BF16),
                   jax.ShapeDtypeStruct((N_TOK, nk), BF16),
                   jax.ShapeDtypeStruct((N_TOK, nk), BF16),
                   _cache_shape(ATT_KV_HEADS, SEQ, ATT_HEAD_DIM), _cache_shape(ATT_KV_HEADS, SEQ, ATT_HEAD_DIM)],
        compiler_params=_cparams(1),
        name="proj_att",
    )(xp, xs, gain, mods, mods, w, qg, kg, _group_mean_matrix(ATT_HEAD_DIM), *tables)


def _proj_diff_kernel(h_ref, w_ref, qg_ref, kg_ref, m_ref,
                      cos_ref, sp_ref, sn_ref, q_ref, k_ref, v_ref, ck_ref, cv_ref):
    i = pl.program_id(0)
    h = h_ref[...]
    per = PROJ_UNIT // LANES
    nu = DIFF_HEADS // per

    def body(lat):
        rope = _rope_args(lat, cos_ref, sp_ref, sn_ref, DIFF_HEAD_DIM)

        def emit(u, y):
            if u < 2 * nu:
                y = _head_norm(y, m_ref, qg_ref[...] if u < nu else kg_ref[...])
            for t, yc in enumerate(_halves(y)):
                hd = (u % nu) * per + t
                if u < nu:
                    _put(q_ref, hd, _maybe_rope(yc, rope))
                elif u < 2 * nu:
                    kn = _maybe_rope(yc, rope)
                    _put(k_ref, hd, kn)
                    if not lat:
                        _cache_rows(ck_ref, [hd, 0], kn[:, :HALF])
                        _cache_rows(ck_ref, [hd, 1], kn[:, HALF:])
                else:
                    _put(v_ref, hd, yc)
                    if not lat:
                        _cache_rows(cv_ref, [hd], yc)

        _matmul_units(h, w_ref, 3 * nu, PROJ_UNIT, emit)

    _by_tile_kind(i, body)


def _proj_diff(h, w, qg, kg, tables):
    n = DIFF_HEADS * 2 * DIFF_HEAD_DIM
    return pl.pallas_call(
        _proj_diff_kernel,
        grid=(N_PROJ_TILES,),
        in_specs=[_tok_spec(D_MODEL),
                  _const_spec(w.shape), _const_spec((1, PROJ_UNIT)), _const_spec((1, PROJ_UNIT)),
                  _UNIT_MAT_SPEC, _ROPE_SPEC, _ROPE_SPEC, _ROPE_SPEC],
        out_specs=[_tok_spec(n), _tok_spec(n), _tok_spec(n),
                   _cache_spec(DIFF_HEADS, 2, SEQ, DIFF_HEAD_DIM), _cache_spec(DIFF_HEADS, SEQ, 2 * DIFF_HEAD_DIM)],
        out_shape=[jax.ShapeDtypeStruct((N_TOK, n), BF16)] * 3
                  + [_cache_shape(DIFF_HEADS, 2, SEQ, DIFF_HEAD_DIM),
                     _cache_shape(DIFF_HEADS, SEQ, 2 * DIFF_HEAD_DIM)],
        compiler_params=_cparams(1),
        name="proj_diff",
    )(h, w, qg, kg, _group_mean_matrix(DIFF_HEAD_DIM), *tables)


def _dup_halves(yc):
    lo = _lane_lo(yc.shape)
    sw = pltpu.roll(yc, HALF, 1)
    return jnp.where(lo, yc, sw), jnp.where(lo, sw, yc)


def _proj_swa_kernel(h_ref, w_ref, qg_ref, kg_ref, m_ref,
                     cos_ref, sp_ref, sn_ref, q_ref, kd_ref, vd_ref, ck_ref, cv_ref):
    i = pl.program_id(0)
    h = h_ref[...]
    per = PROJ_UNIT // LANES
    nq = SWA_HEADS * SWA_HEAD_DIM // PROJ_UNIT
    nk = SWA_KV_HEADS * SWA_HEAD_DIM // PROJ_UNIT

    def body(lat):
        rope = _rope_args(lat, cos_ref, sp_ref, sn_ref, SWA_HEAD_DIM)

        def emit(u, y):
            if u < nq + nk:
                y = _head_norm(y, m_ref, qg_ref[...] if u < nq else kg_ref[...])
            for t, yc in enumerate(_halves(y)):
                if u < nq:
                    _put(q_ref, u * per + t, _maybe_rope(yc, rope))
                    continue
                if u < nq + nk:
                    j, c_ref, d_ref = (u - nq) * per + t, ck_ref, kd_ref
                    yc = _maybe_rope(yc, rope)
                else:
                    j, c_ref, d_ref = (u - nq - nk) * per + t, cv_ref, vd_ref
                for a, dup in enumerate(_dup_halves(yc)):
                    _put(d_ref, 2 * j + a, dup)
                    if not lat:
                        _cache_rows(c_ref, [2 * j + a], dup[:, :HALF])

        _matmul_units(h, w_ref, nq + 2 * nk, PROJ_UNIT, emit)

    _by_tile_kind(i, body)


def _proj_swa(h, w, qg, kg, tables):
    nq, nk = SWA_HEADS * SWA_HEAD_DIM, SWA_KV_HEADS * SWA_HEAD_DIM
    return pl.pallas_call(
        _proj_swa_kernel,
        grid=(N_PROJ_TILES,),
        in_specs=[_tok_spec(D_MODEL),
                  _const_spec(w.shape), _const_spec((1, PROJ_UNIT)), _const_spec((1, PROJ_UNIT)),
                  _UNIT_MAT_SPEC, _ROPE_SPEC, _ROPE_SPEC, _ROPE_SPEC],
        out_specs=[_tok_spec(nq), _tok_spec(2 * nk), _tok_spec(2 * nk),
                   _cache_spec(SWA_KV_HEADS, SEQ, SWA_HEAD_DIM), _cache_spec(SWA_KV_HEADS, SEQ, SWA_HEAD_DIM)],
        out_shape=[jax.ShapeDtypeStruct((N_TOK, nq), BF16),
                   jax.ShapeDtypeStruct((N_TOK, 2 * nk), BF16),
                   jax.ShapeDtypeStruct((N_TOK, 2 * nk), BF16),
                   _cache_shape(SWA_KV_HEADS, SEQ, SWA_HEAD_DIM), _cache_shape(SWA_KV_HEADS, SEQ, SWA_HEAD_DIM)],
        compiler_params=_cparams(1),
        name="proj_swa",
    )(h, w, qg, kg, _group_mean_matrix(SWA_HEAD_DIM), *tables)


def _mla_lane_matrices():
    everything = lambda lane: lane >= 0
    return (_lane_sum_matrix(everything, everything),
            _lane_sum_matrix(lambda lane: lane < HALF, everything),
            _lane_sum_matrix(lambda lane: lane >= HALF, everything))


def _proj_mla_kernel(h_ref, w_in_ref, qa_ref, kva_ref, w_uq_ref,
                     qg_ref, qgp_ref, all_ref, lo_ref, hi_ref, cos_ref, sp_ref, sn_ref,
                     qn_ref, qp_ref, ckv_ref, kpe_ref, c_ckv_ref, c_kpe_ref):
    i = pl.program_id(0)
    y = _dot(h_ref[...], w_in_ref[...])
    c_q = y[:, :MLA_Q_RANK]
    c_kv = y[:, MLA_Q_RANK:MLA_Q_RANK + MLA_KV_RANK]
    kpe = y[:, MLA_Q_RANK + MLA_KV_RANK:]
    kpe_ref[...] = kpe
    ckv = c_kv * lax.rsqrt(jnp.mean(c_kv * c_kv, axis=-1, keepdims=True) + EPS) * kva_ref[...]
    ckv_ref[...] = ckv.astype(BF16)
    cq = (c_q * lax.rsqrt(jnp.mean(c_q * c_q, axis=-1, keepdims=True) + EPS) * qa_ref[...]).astype(BF16)
    lo = _lane_lo((PROJ_TM, LANES))
    inv_d = 1.0 / (MLA_NOPE + MLA_ROPE)

    def body(lat):
        if not lat:
            _cache_rows(c_ckv_ref, [], ckv)
            _cache_rows(c_kpe_ref, [], kpe[:, :MLA_ROPE])

        def emit(j, yq):
            pe = yq[:, 2 * LANES:]
            pe_sq = _sq_bf16(pe)
            rs = []
            for a, half_ref in enumerate((lo_ref, hi_ref)):
                nope = yq[:, a * LANES:(a + 1) * LANES]
                ss = _dot(_sq_bf16(nope), all_ref[...]) + _dot(pe_sq, half_ref[...])
                r = lax.rsqrt(ss * inv_d + EPS)
                rs.append(r)
                _put(qn_ref, 2 * j + a, nope * r * qg_ref[...])
            pe = pe * jnp.where(lo, rs[0], rs[1]) * qgp_ref[...]
            if lat:
                pe = _rope(pe, cos_ref[...], sp_ref[...], sn_ref[...], MLA_ROPE // 4)
            _put(qp_ref, j, pe)

        _matmul_units(cq, w_uq_ref, MLA_HEADS // 2, 3 * LANES, emit)

    _by_tile_kind(i, body)


def _proj_mla(h, w_in, qa, kva, w_uq, qg, qgp, tables):
    n_nope = MLA_HEADS * MLA_NOPE
    n_pe = MLA_HEADS * MLA_ROPE
    return pl.pallas_call(
        _proj_mla_kernel,
        grid=(N_PROJ_TILES,),
        in_specs=[_tok_spec(D_MODEL),
                  _const_spec(w_in.shape), _const_spec((1, MLA_Q_RANK)), _const_spec((1, MLA_KV_RANK)),
                  _const_spec(w_uq.shape), _const_spec((1, LANES)), _const_spec((1, LANES)),
                  _LANE_MAT_SPEC, _LANE_MAT_SPEC, _LANE_MAT_SPEC,
                  _ROPE_SPEC, _ROPE_SPEC, _ROPE_SPEC],
        out_specs=[_tok_spec(n_nope), _tok_spec(n_pe), _tok_spec(MLA_KV_RANK), _tok_spec(LANES),
                   _cache_spec(SEQ, MLA_KV_RANK), _cache_spec(SEQ, MLA_ROPE)],
        out_shape=[jax.ShapeDtypeStruct((N_TOK, n_nope), BF16),
                   jax.ShapeDtypeStruct((N_TOK, n_pe), BF16),
                   jax.ShapeDtypeStruct((N_TOK, MLA_KV_RANK), BF16),
                   jax.ShapeDtypeStruct((N_TOK, LANES), F32),
                   _cache_shape(SEQ, MLA_KV_RANK), _cache_shape(SEQ, MLA_ROPE)],
        compiler_params=_cparams(1),
        name="proj_mla",
    )(h, w_in, qa, kva, w_uq, qg, qgp, *_mla_lane_matrices(), *tables)


def _mla_expand_kernel(ckv_ref, kpe_ref, w_ref, kg_ref, kgp_ref, sum_ref, lo_ref,
                       cos_ref, sp_ref, sn_ref, kn_ref, kp_ref, v_ref, *, rope):
    i = pl.program_id(0)
    ckv = ckv_ref[...].astype(BF16)
    kpe = kpe_ref[...]
    pe_ss = _dot(_sq_bf16(kpe), lo_ref[...])
    pe_ss = jnp.concatenate([pe_ss, pe_ss], axis=1)
    lo = _lane_lo(kpe.shape)
    inv_d = 1.0 / (MLA_NOPE + MLA_ROPE)

    def body(lat):
        def emit(j, y):
            kn = jnp.concatenate([y[:, :LANES], y[:, 2 * LANES:3 * LANES]], axis=1)
            r = lax.rsqrt((_dot(_sq_bf16(kn), sum_ref[...]) + pe_ss) * inv_d + EPS)
            kn = kn * r * kg_ref[...]
            for a in range(2):
                _put(kn_ref, 2 * j + a, kn[:, a * LANES:(a + 1) * LANES])
                _put(v_ref, 2 * j + a, y[:, (2 * a + 1) * LANES:(2 * a + 2) * LANES])
            pe = kpe * jnp.where(lo, r[:, :LANES], r[:, LANES:]) * kgp_ref[...]
            if lat:
                pe = _rope(pe, cos_ref[...], sp_ref[...], sn_ref[...], MLA_ROPE // 4)
            _put(kp_ref, j, pe)

        _matmul_units(ckv, w_ref, MLA_HEADS // 2, 4 * LANES, emit)

    if rope:
        _by_tile_kind(i, body)
    else:
        body(False)


def _mla_expand(ckv, kpe_dup, w_ukv, kg, kgp, tables, rope):
    n = ckv.shape[0]
    n_nope = MLA_HEADS * MLA_NOPE
    n_pe = MLA_HEADS * MLA_ROPE
    _, m_lo, _ = _mla_lane_matrices()
    return pl.pallas_call(
        functools.partial(_mla_expand_kernel, rope=rope),
        grid=(n // PROJ_TM,),
        in_specs=[_tok_spec(MLA_KV_RANK), _tok_spec(LANES), _const_spec(w_ukv.shape),
                  _const_spec((1, PROJ_UNIT)), _const_spec((1, LANES)), _UNIT_MAT_SPEC, _LANE_MAT_SPEC,
                  _ROPE_SPEC, _ROPE_SPEC, _ROPE_SPEC],
        out_specs=[_tok_spec(n_nope), _tok_spec(n_pe), _tok_spec(n_nope)],
        out_shape=[jax.ShapeDtypeStruct((n, n_nope), BF16),
                   jax.ShapeDtypeStruct((n, n_pe), BF16),
                   jax.ShapeDtypeStruct((n, n_nope), BF16)],
        compiler_params=_cparams(1),
        name="mla_expand",
    )(ckv, kpe_dup, w_ukv, kg, kgp, _group_sum_matrix(), m_lo, *tables)


def _prompt_spec(width):
    return pl.BlockSpec((TM, width), lambda b: (b, 0))


def _latq_spec(rows, width):
    per = DEC_SEQ // rows
    return pl.BlockSpec((rows, width), lambda b, t: (N_PROMPT_TOK // rows + b * per + t, 0))


def _latkv_spec(width):
    return pl.BlockSpec((DEC_SEQ, width), lambda b, t: (LAT_BLOCK0 + b, 0))


def _lato_spec(rows):
    per = DEC_SEQ // rows
    return pl.BlockSpec((rows, D_MODEL), lambda b, t: (b * per + t, 0))


def _att_kernel(*refs, with_ctx):
    if with_ctx:
        q_ref, k_ref, v_ref, kc_ref, vc_ref, o_ref = refs
    else:
        q_ref, k_ref, v_ref, o_ref = refs
    tq = q_ref.shape[0]
    nu = ATT_UNIT_HEADS
    per_kv = ATT_HEADS // ATT_KV_HEADS // nu

    def scores(u):
        q = jnp.concatenate([_chunk(q_ref, u * nu + g) for g in range(nu)], axis=0)
        s_list = [_dot_nt(_chunk(k_ref, u // per_kv), q)]
        if with_ctx:
            s_list.append(_dot_nt(kc_ref[u // per_kv].astype(BF16), q))
        return s_list

    def finish(u, s_list):
        values = [_chunk(v_ref, u // per_kv)]
        if with_ctx:
            values.append(vc_ref[u // per_kv].astype(BF16))
        ps, inv = _softmax2_parts(s_list)
        o = _pv(ps, values) * inv
        for g in range(nu):
            o_ref[:, (u * nu + g) * LANES:(u * nu + g + 1) * LANES] = (
                o[:, g * tq:(g + 1) * tq].T.astype(o_ref.dtype))

    _head_pipeline(ATT_HEADS // nu, scores, finish)


def _att_attend(q, k, v, cache_k, cache_v):
    nk = ATT_KV_HEADS * ATT_HEAD_DIM
    out_p = pl.pallas_call(
        functools.partial(_att_kernel, with_ctx=False),
        grid=(N_PROMPT_TILES,),
        in_specs=[_prompt_spec(D_MODEL), _prompt_spec(nk), _prompt_spec(nk)],
        out_specs=_prompt_spec(D_MODEL),
        out_shape=jax.ShapeDtypeStruct((N_PROMPT_TOK, D_MODEL), BF16),
        compiler_params=_cparams(1),
        name="att_prompt",
    )(q, k, v)
    ctx = pl.BlockSpec((None, None, ATT_KV_HEADS, PAST_LEN, LANES), lambda b, t: (b, 0, 0, 0, 0))
    out_s = pl.pallas_call(
        functools.partial(_att_kernel, with_ctx=True),
        grid=(DEC_BATCH, TILES_PER_DEC),
        in_specs=[_latq_spec(TM, D_MODEL), _latkv_spec(nk), _latkv_spec(nk), ctx, ctx],
        out_specs=_lato_spec(TM),
        out_shape=jax.ShapeDtypeStruct((N_LAT_TOK, D_MODEL), BF16),
        compiler_params=_cparams(2),
        name="att_latent",
    )(q, k, v, cache_k, cache_v)
    return out_p, out_s


def _diff_kernel(*refs, lam_init, with_ctx):
    if with_ctx:
        (q_ref, k_ref, v_ref, kc_ref, vc_ref, lq1_ref, lk1_ref, lq2_ref, lk2_ref, sub_ref, o_ref) = refs
    else:
        (q_ref, k_ref, v_ref, lq1_ref, lk1_ref, lq2_ref, lk2_ref, sub_ref, o_ref) = refs
    tq = q_ref.shape[0]
    lam = (jnp.exp(jnp.sum(lq1_ref[...] * lk1_ref[...], axis=-1, keepdims=True))
           - jnp.exp(jnp.sum(lq2_ref[...] * lk2_ref[...], axis=-1, keepdims=True)) + lam_init)
    sub = sub_ref[...] * (1.0 - lam_init)

    def scores(hd):
        q = jnp.concatenate(_split_halves(_chunk(q_ref, hd)), axis=0)
        s_list = [_dot_nt(_chunk(k_ref, hd), q)]
        if with_ctx:
            s_list.append(_dot_nt(kc_ref[hd].astype(BF16), q))
        return s_list

    def finish(hd, s_list):
        values = [_chunk(v_ref, hd)]
        if with_ctx:
            values.append(vc_ref[hd].astype(BF16))
        ps, inv = _softmax2_parts(s_list)
        o = (_pv([p[:, :tq] for p in ps], values) * inv[:, :tq]
             - _pv([p[:, tq:] for p in ps], values) * (lam * inv[:, tq:]))
        o = o * lax.rsqrt(jnp.mean(o * o, axis=0, keepdims=True) + EPS) * sub
        o_ref[:, hd * LANES:(hd + 1) * LANES] = o.T.astype(o_ref.dtype)

    _head_pipeline(DIFF_HEADS, scores, finish)


def _diff_attend(q, k, v, cache_k_pair, cache_v, lq1, lk1, lq2, lk2, subln, lam_init):
    small = [lq1, lk1, lq2, lk2, subln]
    small_specs = [_const_spec(s.shape) for s in small]
    out_p = pl.pallas_call(
        functools.partial(_diff_kernel, lam_init=lam_init, with_ctx=False),
        grid=(N_PROMPT_TILES,),
        in_specs=[_prompt_spec(D_MODEL)] * 3 + small_specs,
        out_specs=_prompt_spec(D_MODEL),
        out_shape=jax.ShapeDtypeStruct((N_PROMPT_TOK, D_MODEL), BF16),
        compiler_params=_cparams(1),
        name="diff_prompt",
    )(q, k, v, *small)
    out_s = pl.pallas_call(
        functools.partial(_diff_kernel, lam_init=lam_init, with_ctx=True),
        grid=(DEC_BATCH, TILES_PER_DEC),
        in_specs=[_latq_spec(TM, D_MODEL), _latkv_spec(D_MODEL), _latkv_spec(D_MODEL),
                  pl.BlockSpec((None, DIFF_HEADS, PAST_LEN, LANES), lambda b, t: (b, 0, 0, 0)),
                  pl.BlockSpec((None, None, DIFF_HEADS, PAST_LEN, LANES), lambda b, t: (b, 0, 0, 0, 0))]
                 + small_specs,
        out_specs=_lato_spec(TM),
        out_shape=jax.ShapeDtypeStruct((N_LAT_TOK, D_MODEL), BF16),
        compiler_params=_cparams(2),
        name="diff_latent",
    )(q, k, v, cache_k_pair, cache_v, *small)
    return out_p, out_s


def _swa_pipeline(q_ref, sink_ref, score_fns, value_fns, o_ref):
    tq = q_ref.shape[0]
    per_kv = SWA_HEADS // SWA_KV_HEADS // 2
    first = lax.broadcasted_iota(jnp.int32, (LANES, tq), 0) < HALF

    def scores(c):
        q = jnp.concatenate(_split_halves(_chunk(q_ref, c)), axis=0)
        return [fn(c // per_kv, q) for fn in score_fns]

    def finish(c, s_list):
        sink = jnp.concatenate([jnp.full((1, tq), sink_ref[2 * c + a] * LOG2E, F32) for a in range(2)],
                               axis=1)
        ps, inv = _softmax2_parts(s_list, extra=sink)
        o = _pv(ps, [fn(c // per_kv) for fn in value_fns]) * inv
        oc = jnp.where(first, o[:, :tq], o[:, tq:])
        o_ref[:, c * LANES:(c + 1) * LANES] = oc.T.astype(o_ref.dtype)

    _head_pipeline(SWA_HEADS // 2, scores, finish)


def _swa_prompt_kernel(sink_ref, q_ref, k_ref, v_ref, o_ref):
    _swa_pipeline(q_ref, sink_ref, [lambda kv, q: _dot_nt(_chunk(k_ref, kv), q)],
                  [lambda kv: _chunk(v_ref, kv)], o_ref)


def _swa_latent_kernel(sink_ref, q_ref, k_ref, v_ref, kc_ref, vc_ref, o_ref):
    n = pl.program_id(1)
    tq = q_ref.shape[0]
    span = 3 * SWA_QB
    start = pl.multiple_of(jnp.clip((n - 1) * SWA_QB, 0, DEC_SEQ - span), SWA_QB)
    cols = lax.broadcasted_iota(jnp.int32, (span, 2 * tq), 1)
    qpos = n * SWA_QB + jnp.bitwise_and(cols, tq - 1)
    kpos = start + lax.broadcasted_iota(jnp.int32, (span, 2 * tq), 0)
    valid = jnp.abs(qpos - kpos) <= WINDOW

    def local(ref, kv):
        return ref[pl.ds(start, span), kv * LANES:(kv + 1) * LANES]

    _swa_pipeline(q_ref, sink_ref,
                  [lambda kv, q: jnp.where(valid, _dot_nt(local(k_ref, kv), q), -1e30),
                   lambda kv, q: _dot_nt(kc_ref[kv], q)],
                  [lambda kv: local(v_ref, kv), lambda kv: vc_ref[kv]], o_ref)


def _swa_attend(q, kd, vd, cache_kd, cache_vd, sink):
    nkd = 2 * SWA_KV_HEADS * SWA_HEAD_DIM
    smem = pl.BlockSpec(memory_space=pltpu.SMEM)
    out_p = pl.pallas_call(
        _swa_prompt_kernel,
        grid=(N_PROMPT_TILES,),
        in_specs=[smem, _prompt_spec(D_MODEL), _prompt_spec(nkd), _prompt_spec(nkd)],
        out_specs=_prompt_spec(D_MODEL),
        out_shape=jax.ShapeDtypeStruct((N_PROMPT_TOK, D_MODEL), BF16),
        compiler_params=_cparams(1),
        name="swa_prompt",
    )(sink, q, kd, vd)
    ctx = pl.BlockSpec((None, SWA_KV_HEADS, PAST_LEN, LANES), lambda b, n: (b, 0, 0, 0))
    out_s = pl.pallas_call(
        _swa_latent_kernel,
        grid=(DEC_BATCH, DEC_SEQ // SWA_QB),
        in_specs=[smem, _latq_spec(SWA_QB, D_MODEL), _latkv_spec(nkd), _latkv_spec(nkd), ctx, ctx],
        out_specs=_lato_spec(SWA_QB),
        out_shape=jax.ShapeDtypeStruct((N_LAT_TOK, D_MODEL), BF16),
        compiler_params=_cparams(2),
        name="swa_latent",
    )(sink, q, kd, vd, cache_kd, cache_vd)
    return out_p, out_s


def _mla_kernel(*refs, with_ctx):
    if with_ctx:
        (qn_ref, qp_ref, kn_ref, kp_ref, v_ref, knc_ref, kpc_ref, vc_ref, o_ref) = refs
    else:
        (qn_ref, qp_ref, kn_ref, kp_ref, v_ref, o_ref) = refs

    def scores(hd):
        j, a = hd // 2, hd % 2
        q = jnp.concatenate([_chunk(qn_ref, hd), _split_halves(_chunk(qp_ref, j))[a]], axis=1)
        s_list = [_dot_nt(jnp.concatenate([_chunk(kn_ref, hd), _chunk(kp_ref, j)], axis=1), q)]
        if with_ctx:
            s_list.append(_dot_nt(jnp.concatenate([_chunk(knc_ref, hd), _chunk(kpc_ref, j)], axis=1), q))
        return s_list

    def finish(hd, s_list):
        values = [_chunk(v_ref, hd)]
        if with_ctx:
            values.append(_chunk(vc_ref, hd))
        ps, inv = _softmax2_parts(s_list)
        o_ref[:, hd * LANES:(hd + 1) * LANES] = (_pv(ps, values) * inv).T.astype(o_ref.dtype)

    _head_pipeline(MLA_HEADS, scores, finish)


def _mla_attend(qn, qp, kn, kp, v, knc, kpc, vc):
    n_pe = MLA_HEADS * MLA_ROPE
    out_p = pl.pallas_call(
        functools.partial(_mla_kernel, with_ctx=False),
        grid=(N_PROMPT_TILES,),
        in_specs=[_prompt_spec(D_MODEL), _prompt_spec(n_pe), _prompt_spec(D_MODEL), _prompt_spec(n_pe),
                  _prompt_spec(D_MODEL)],
        out_specs=_prompt_spec(D_MODEL),
        out_shape=jax.ShapeDtypeStruct((N_PROMPT_TOK, D_MODEL), BF16),
        compiler_params=_cparams(1),
        name="mla_prompt",
    )(qn, qp, kn, kp, v)

    def ctx(width):
        return pl.BlockSpec((PAST_LEN, width), lambda b, t: (b, 0))

    out_s = pl.pallas_call(
        functools.partial(_mla_kernel, with_ctx=True),
        grid=(DEC_BATCH, TILES_PER_DEC),
        in_specs=[_latq_spec(TM, D_MODEL), _latq_spec(TM, n_pe),
                  _latkv_spec(D_MODEL), _latkv_spec(n_pe), _latkv_spec(D_MODEL),
                  ctx(D_MODEL), ctx(n_pe), ctx(D_MODEL)],
        out_specs=_lato_spec(TM),
        out_shape=jax.ShapeDtypeStruct((N_LAT_TOK, D_MODEL), BF16),
        compiler_params=_cparams(2),
        name="mla_latent",
    )(qn, qp, kn, kp, v, knc, kpc, vc)
    return out_p, out_s


def _omlp_kernel(*refs, emit_next):
    (ap_ref, as_ref, wo_ref, xp_ref, xs_ref, g1_ref, gain_ref, sh_ref, sc_ref, g2_ref,
     w1c_ref, w2c_ref) = refs[:12]
    if emit_next:
        ngain_ref, nsh_ref, nsc_ref, op_ref, os_ref, hn_ref, wo_s, w1_s, w2_s = refs[12:]
    else:
        op_ref, os_ref, wo_s, w1_s, w2_s = refs[12:]
    s = pl.program_id(0)
    per = MLP_FF_CHUNK // MLP_LOAD_COLS

    @pl.when(s == 0)
    def _():
        wo_s[...] = wo_ref[...].astype(BF16)

    for part in range(per):
        @pl.when((s < N_LOAD_STEPS) & (s % per == part))
        def _(part=part):
            w1_s[s // per, :, part * MLP_LOAD_COLS:(part + 1) * MLP_LOAD_COLS] = w1c_ref[...].astype(BF16)

    @pl.when(s < N_LOAD_STEPS)
    def _():
        w2_s[s // per, pl.ds(pl.multiple_of((s % per) * MLP_LOAD_COLS, MLP_LOAD_COLS), MLP_LOAD_COLS), :] = (
            w2c_ref[...].astype(BF16))

    @pl.when(s >= N_LOAD_STEPS)
    def _():
        t = s - N_LOAD_STEPS
        is_prompt = t < N_MLP_PROMPT_TILES
        grp = _tile_group(t, MLP_TM)
        a = jnp.where(is_prompt, ap_ref[...], as_ref[...])
        x = jnp.where(is_prompt, xp_ref[...], xs_ref[...])
        x1 = x + g1_ref[pl.ds(grp, 1), :] * _dot(a, wo_s[...])
        h = _norm_mod(x1, gain_ref[...], sh_ref[pl.ds(grp, 1), :], sc_ref[pl.ds(grp, 1), :]).astype(BF16)
        acc = None
        for c in range(D_FF // MLP_FF_CHUNK):
            u = _dot(h, w1_s[c])
            u = jnp.square(jnp.maximum(u, 0.0)).astype(BF16)
            y = _dot(u, w2_s[c])
            acc = y if acc is None else acc + y
        out = x1 + g2_ref[pl.ds(grp, 1), :] * acc
        if emit_next:
            hn_ref[...] = _norm_mod(out, ngain_ref[...], nsh_ref[pl.ds(grp, 1), :],
                                    nsc_ref[pl.ds(grp, 1), :]).astype(BF16)

        @pl.when(is_prompt)
        def _():
            op_ref[...] = out

        @pl.when(jnp.logical_not(is_prompt))
        def _():
            os_ref[...] = out


def _omlp(attn_p, attn_s, w_o, xp, xs, mods, gain_ffn, w1_all, w2_all, layer, next_gain, next_mods):
    n_lat_tiles = N_LAT_TOK // MLP_TM
    emit_next = next_gain is not None

    def tok(s):
        return jnp.maximum(s - N_LOAD_STEPS, 0)

    p_spec = pl.BlockSpec((MLP_TM, D_MODEL), lambda s: (jnp.minimum(tok(s), N_MLP_PROMPT_TILES - 1), 0))
    l_spec = pl.BlockSpec((MLP_TM, D_MODEL),
                          lambda s: (jnp.clip(tok(s) - N_MLP_PROMPT_TILES, 0, n_lat_tiles - 1), 0))
    w1_spec = pl.BlockSpec((None, D_MODEL, MLP_LOAD_COLS),
                           lambda s: (layer, 0, jnp.minimum(s, N_LOAD_STEPS - 1)))
    w2_spec = pl.BlockSpec((None, MLP_LOAD_COLS, D_MODEL),
                           lambda s: (layer, jnp.minimum(s, N_LOAD_STEPS - 1), 0))
    n_chunks = D_FF // MLP_FF_CHUNK
    in_specs = [p_spec, l_spec, _const_spec(w_o.shape), p_spec, l_spec, _mod_spec(2),
                _const_spec((1, D_MODEL)), _mod_spec(3), _mod_spec(4), _mod_spec(5), w1_spec, w2_spec]
    args = [attn_p, attn_s, w_o, xp, xs, mods, gain_ffn, mods, mods, mods, w1_all, w2_all]
    out_specs = [p_spec, l_spec]
    out_shape = [jax.ShapeDtypeStruct((N_PROMPT_TOK, D_MODEL), F32),
                 jax.ShapeDtypeStruct((N_LAT_TOK, D_MODEL), F32)]
    if emit_next:
        in_specs += [_const_spec((1, D_MODEL)), _mod_spec(0), _mod_spec(1)]
        args += [next_gain, next_mods, next_mods]
        out_specs.append(pl.BlockSpec((MLP_TM, D_MODEL), lambda s: (tok(s), 0)))
        out_shape.append(jax.ShapeDtypeStruct((N_TOK, D_MODEL), BF16))
    return pl.pallas_call(
        functools.partial(_omlp_kernel, emit_next=emit_next),
        grid=(N_LOAD_STEPS + N_TOK // MLP_TM,),
        in_specs=in_specs,
        out_specs=out_specs,
        out_shape=out_shape,
        scratch_shapes=[pltpu.VMEM((D_MODEL, D_MODEL), BF16),
                        pltpu.VMEM((n_chunks, D_MODEL, MLP_FF_CHUNK), BF16),
                        pltpu.VMEM((n_chunks, MLP_FF_CHUNK, D_MODEL), BF16)],
        compiler_params=_cparams(1),
        name="omlp",
    )(*args)


def _row(v, scale=1.0):
    return (v.astype(F32) * scale).reshape(1, -1)


def _pair(v, scale=1.0):
    return (jnp.concatenate([v, v]).astype(F32) * scale).reshape(1, LANES)


def _unit_gain(v, scale=1.0):
    return (jnp.tile(v.astype(F32), PROJ_UNIT // v.shape[0]) * scale).reshape(1, PROJ_UNIT)


def kernel(x_prompt, x_sample, cache_att_k, cache_att_v, cache_diff_k, cache_diff_v, cache_swa_k, cache_swa_v, cache_mla_ckv, cache_mla_kpe, c, c_ctx, ada_w, ada_b, norm_mix, norm_ffn, att_w_qkv, att_q_norm, att_k_norm, att_w_o, diff_w_qkv, diff_q_norm, diff_k_norm, diff_lq1, diff_lk1, diff_lq2, diff_lk2, diff_subln, diff_w_o, swa_w_qkv, swa_q_norm, swa_k_norm, swa_sink, swa_w_o, mla_w_in, mla_q_a_norm, mla_kv_a_norm, mla_w_uq, mla_w_ukv, mla_q_norm, mla_k_norm, mla_w_o, mlp_w1, mlp_w2):
    xp = x_prompt.reshape(N_PROMPT_TOK, D_MODEL)
    xs = x_sample.reshape(N_LAT_TOK, D_MODEL)
    cond = jnp.concatenate([c_ctx[None], c, jnp.zeros((COND_ROWS - 1 - DEC_BATCH, D_MODEL), F32)], axis=0)
    mods_all = _modulation(cond, ada_w, ada_b)

    tab_att = _rope_tables(ATT_HEAD_DIM)
    tab_64 = _rope_tables(DIFF_HEAD_DIM)

    outs = {}
    for layer in range(DEPTH):
        mods = mods_all[layer]
        gain_ffn = _row(norm_ffn[layer])
        if layer == 0:
            qs = ATT_HEAD_DIM ** -0.5 * LOG2E
            q, k, v, outs["att_k"], outs["att_v"] = _proj_att(
                xp, xs, mods, _row(norm_mix[layer]), att_w_qkv[0].astype(BF16),
                _unit_gain(att_q_norm[0], qs), _unit_gain(att_k_norm[0]), tab_att)
            attn_p, attn_s = _att_attend(q, k, v, cache_att_k, cache_att_v)
            w_o = att_w_o[0]
        elif layer == 1:
            qs = DIFF_HEAD_DIM ** -0.5 * LOG2E
            q, k, v, outs["diff_k"], outs["diff_v"] = _proj_diff(
                h, diff_w_qkv[0].astype(BF16),
                _unit_gain(diff_q_norm[0], qs), _unit_gain(diff_k_norm[0]), tab_64)
            lam_init = 0.8 - 0.6 * math.exp(-0.3 * layer)
            ck = cache_diff_k[:, 0].transpose(0, 1, 3, 2, 4).reshape(
                DEC_BATCH, DIFF_HEADS, PAST_LEN, LANES)
            attn_p, attn_s = _diff_attend(q, k, v, ck, cache_diff_v,
                                          _row(diff_lq1[0]), _row(diff_lk1[0]),
                                          _row(diff_lq2[0]), _row(diff_lk2[0]),
                                          diff_subln[0].astype(F32).reshape(LANES, 1), lam_init)
            w_o = diff_w_o[0]
        elif layer == 2:
            qs = SWA_HEAD_DIM ** -0.5 * LOG2E
            q, kd, vd, outs["swa_k"], outs["swa_v"] = _proj_swa(
                h, swa_w_qkv[0].astype(BF16),
                _unit_gain(swa_q_norm[0], qs), _unit_gain(swa_k_norm[0]), tab_64)
            ckd = jnp.concatenate([cache_swa_k[:, 0]] * 2, axis=-1).astype(BF16)
            cvd = jnp.concatenate([cache_swa_v[:, 0]] * 2, axis=-1).astype(BF16)
            attn_p, attn_s = _swa_attend(q, kd, vd, ckd, cvd, swa_sink[0].astype(F32))
            w_o = swa_w_o[0]
        else:
            qs = (MLA_NOPE + MLA_ROPE) ** -0.5 * LOG2E
            w_in = mla_w_in[0]
            w_in = jnp.concatenate([w_in, w_in[:, -MLA_ROPE:]], axis=1).astype(BF16)
            w_uq = mla_w_uq[0].reshape(MLA_Q_RANK, MLA_HEADS // 2, 2, MLA_NOPE + MLA_ROPE)
            w_uq = jnp.concatenate([w_uq[..., :MLA_NOPE].reshape(MLA_Q_RANK, MLA_HEADS // 2, 2 * MLA_NOPE),
                                    w_uq[..., MLA_NOPE:].reshape(MLA_Q_RANK, MLA_HEADS // 2, 2 * MLA_ROPE)],
                                   axis=-1).reshape(MLA_Q_RANK, -1).astype(BF16)
            w_ukv = mla_w_ukv[0].astype(BF16)
            qg, kg = mla_q_norm[0], mla_k_norm[0]
            qn, qp, ckv, kpe, outs["mla_ckv"], outs["mla_kpe"] = _proj_mla(
                h, w_in, _row(mla_q_a_norm[0]), _row(mla_kv_a_norm[0]), w_uq,
                _row(qg[:MLA_NOPE], qs), _pair(qg[MLA_NOPE:], qs), tab_64)
            kn, kp, vv = _mla_expand(ckv, kpe, w_ukv, _unit_gain(kg[:MLA_NOPE]), _pair(kg[MLA_NOPE:]),
                                     tab_64, True)
            c_ckv = cache_mla_ckv[:, 0].reshape(DEC_BATCH * PAST_LEN, MLA_KV_RANK)
            c_kpe = cache_mla_kpe[:, 0].reshape(DEC_BATCH * PAST_LEN, MLA_ROPE)
            c_kpe = jnp.concatenate([c_kpe, c_kpe], axis=-1)
            knc, kpc, vc = _mla_expand(c_ckv, c_kpe, w_ukv, _unit_gain(kg[:MLA_NOPE]), _pair(kg[MLA_NOPE:]),
                                       tab_64, False)
            attn_p, attn_s = _mla_attend(qn, qp, kn, kp, vv, knc, kpc, vc)
            w_o = mla_w_o[0]
        if layer + 1 < DEPTH:
            xp, xs, h = _omlp(attn_p, attn_s, w_o, xp, xs, mods, gain_ffn, mlp_w1, mlp_w2, layer,
                              _row(norm_mix[layer + 1]), mods_all[layer + 1])
        else:
            xp, xs = _omlp(attn_p, attn_s, w_o, xp, xs, mods, gain_ffn, mlp_w1, mlp_w2, layer, None, None)

    y_prompt = xp.reshape(BATCH, SEQ, D_MODEL)
    y_sample = xs.reshape(DEC_BATCH, DEC_SEQ, D_MODEL)
    return (y_prompt, y_sample, outs["att_k"], outs["att_v"], outs["diff_k"], outs["diff_v"],
            outs["swa_k"], outs["swa_v"], outs["mla_ckv"], outs["mla_kpe"])
```

```python
import functools
import math

import numpy as np
import jax
import jax.numpy as jnp
from jax import lax
from jax.experimental import pallas as pl
from jax.experimental.pallas import tpu as pltpu

D_MODEL = 1024
BATCH = 16
SEQ = 256
DEPTH = 4
DEC_BATCH = 2
DEC_SEQ = 1024
PAST_LEN = 256
GRID_W = 64
ROPE_THETA = 10000.0
EPS = 1e-6
D_FF = 4 * D_MODEL
MOD_CHUNKS = 6
LOG2E = 1.4426950408889634

ATT_HEADS, ATT_KV_HEADS, ATT_HEAD_DIM = 8, 2, 128
DIFF_HEADS, DIFF_HEAD_DIM = 8, 64
SWA_HEADS, SWA_KV_HEADS, SWA_HEAD_DIM, WINDOW = 16, 4, 64, 128
MLA_HEADS, MLA_NOPE, MLA_ROPE, MLA_VDIM = 8, 128, 64, 128
MLA_Q_RANK, MLA_KV_RANK = 512, 256

LANES = 128
HALF = LANES // 2
TM = 256
N_PROMPT_TOK = BATCH * SEQ
N_LAT_TOK = DEC_BATCH * DEC_SEQ
N_TOK = N_PROMPT_TOK + N_LAT_TOK
N_PROMPT_TILES = N_PROMPT_TOK // TM
TILES_PER_DEC = DEC_SEQ // TM
LAT_BLOCK0 = N_PROMPT_TOK // DEC_SEQ
COND_ROWS = 8
PROJ_TM = 512
PROJ_BATCHES = PROJ_TM // SEQ
N_PROJ_TILES = N_TOK // PROJ_TM
N_PROJ_PROMPT = N_PROMPT_TOK // PROJ_TM
PROJ_UNIT = 2 * LANES
MLP_TM = 512
MLP_FF_CHUNK = 512
MLP_LOAD_COLS = 256
N_LOAD_STEPS = D_FF // MLP_LOAD_COLS
N_MLP_PROMPT_TILES = N_PROMPT_TOK // MLP_TM
SWA_QB = 128
ATT_UNIT_HEADS = 4
VMEM_LIMIT = 56 * 1024 * 1024

F32 = jnp.float32
BF16 = jnp.bfloat16


def _cparams(n_axes):
    return pltpu.CompilerParams(dimension_semantics=("arbitrary",) * n_axes,
                                vmem_limit_bytes=VMEM_LIMIT)


def _dot(a, b):
    return jnp.dot(a, b, preferred_element_type=F32)


def _dot_nt(a, b):
    return lax.dot_general(a, b, (((1,), (1,)), ((), ())), preferred_element_type=F32)


def _dot_tn(a, b):
    return lax.dot_general(a, b, (((0,), (0,)), ((), ())), preferred_element_type=F32)


def _const_spec(shape):
    nd = len(shape)
    return pl.BlockSpec(shape, lambda *_: (0,) * nd, pipeline_mode=pl.Buffered(1))


def _chunk(ref, c, width=LANES):
    return ref[:, c * width:(c + 1) * width]


def _put(ref, c, val):
    ref[:, c * LANES:(c + 1) * LANES] = val.astype(ref.dtype)


def _tile_group(i, rows):
    n_prompt = N_PROMPT_TOK // rows
    return jnp.where(i < n_prompt, 0, 1 + (i - n_prompt) // (DEC_SEQ // rows))


def _rope_tile(i):
    return jnp.maximum(i - N_PROJ_PROMPT, 0) % (DEC_SEQ // PROJ_TM)


def _norm_mod(x, gain, shift, scale):
    ms = jnp.mean(x * x, axis=-1, keepdims=True)
    return x * lax.rsqrt(ms + EPS) * (gain * (1.0 + scale)) + shift


def _lane_lo(shape):
    return lax.broadcasted_iota(jnp.int32, shape, len(shape) - 1) < HALF


def _rope(y, cos, sin_prev, sin_next, quarter):
    return (y * cos + pltpu.roll(y, quarter, 1) * sin_prev
            + pltpu.roll(y, LANES - quarter, 1) * sin_next)


def _rope_tables(rot_dim):
    half = rot_dim // 2
    quarter = rot_dim // 4
    inv = np.float32(ROPE_THETA) ** (-np.arange(0, half, 2, dtype=np.float32) / np.float32(half))
    pos = np.arange(DEC_SEQ)
    row = (pos // GRID_W).astype(np.float32)
    col = (pos % GRID_W).astype(np.float32)
    lane = np.arange(LANES)
    dd = lane % rot_dim
    q = dd // quarter
    f = dd % quarter
    ang = np.where((q < 2)[None, :], row[:, None], col[:, None]) * inv[f][None, :]
    ang = ang.astype(np.float32)
    cos = np.cos(ang).astype(np.float32)
    sin = np.sin(ang).astype(np.float32)
    odd = (q % 2 == 1)[None, :]
    sin_prev = np.where(odd, sin, 0.0).astype(np.float32)
    sin_next = np.where(odd, 0.0, -sin).astype(np.float32)
    return jnp.asarray(cos), jnp.asarray(sin_prev), jnp.asarray(sin_next)


def _lane_sum_matrix(rows, cols, value=1.0):
    lane = np.arange(LANES)
    m = np.where(rows(lane)[:, None] & cols(lane)[None, :], value, 0.0).astype(np.float32)
    return jnp.asarray(m, dtype=BF16)


def _group_mean_matrix(group):
    lane = np.arange(PROJ_UNIT)
    m = np.where((lane[:, None] // group) == (lane[None, :] // group), 1.0 / group, 0.0)
    return jnp.asarray(m.astype(np.float32), dtype=BF16)


def _group_sum_matrix():
    lane = np.arange(PROJ_UNIT)
    m = np.where((lane[:, None] // LANES) == (lane[None, :] // LANES), 1.0, 0.0)
    return jnp.asarray(m.astype(np.float32), dtype=BF16)


def _sq_bf16(y):
    return (y * y).astype(BF16)


def _head_norm(y, m_ref, gain):
    return y * lax.rsqrt(_dot(_sq_bf16(y), m_ref[...]) + EPS) * gain


def _halves(y):
    return [y[:, t * LANES:(t + 1) * LANES] for t in range(y.shape[1] // LANES)]


def _matmul_units(h, w_ref, n_units, width, emit):
    def unit(u):
        return _dot(h, w_ref[:, u * width:(u + 1) * width])

    nxt = unit(0)
    for u in range(n_units):
        cur = nxt
        if u + 1 < n_units:
            nxt = unit(u + 1)
        emit(u, cur)


def _by_tile_kind(i, body):
    pl.when(i < N_PROJ_PROMPT)(functools.partial(body, False))
    pl.when(i >= N_PROJ_PROMPT)(functools.partial(body, True))


def _rope_args(lat, cos_ref, sp_ref, sn_ref, rot_dim):
    return (cos_ref[...], sp_ref[...], sn_ref[...], rot_dim // 4) if lat else None


def _maybe_rope(y, rope):
    return y if rope is None else _rope(y, *rope)


def _cache_rows(ref, index, val):
    for b in range(PROJ_BATCHES):
        ref[(b, 0) + tuple(index)] = val[b * SEQ:(b + 1) * SEQ]


def _softmax2_parts(s_list, extra=None):
    m = jnp.max(s_list[0], axis=0, keepdims=True)
    for s in s_list[1:]:
        m = jnp.maximum(m, jnp.max(s, axis=0, keepdims=True))
    if extra is not None:
        m = jnp.maximum(m, extra)
    ps = [jnp.exp2(s - m) for s in s_list]
    mass = ps[0].sum(axis=0, keepdims=True)
    for p in ps[1:]:
        mass = mass + p.sum(axis=0, keepdims=True)
    if extra is not None:
        mass = mass + jnp.exp2(extra - m)
    return [p.astype(BF16) for p in ps], 1.0 / mass


def _head_pipeline(n, scores, finish):
    nxt = scores(0)
    for h in range(n):
        cur = nxt
        if h + 1 < n:
            nxt = scores(h + 1)
        finish(h, cur)


def _pv(ps, values):
    o = None
    for p, v in zip(ps, values):
        t = _dot_tn(v, p)
        o = t if o is None else o + t
    return o


def _split_halves(q):
    lo = _lane_lo(q.shape)
    zero = jnp.zeros_like(q)
    return jnp.where(lo, q, zero), jnp.where(lo, zero, q)


def _mod_kernel(cond_ref, w_ref, b_ref, o_ref):
    c = cond_ref[...]
    s = (c * jax.nn.sigmoid(c)).astype(BF16)
    o_ref[0] = _dot(s, w_ref[0].astype(BF16)) + b_ref[0]


def _modulation(cond, ada_w, ada_b):
    tn = 1536
    n = MOD_CHUNKS * D_MODEL
    return pl.pallas_call(
        _mod_kernel,
        grid=(DEPTH, n // tn),
        in_specs=[
            pl.BlockSpec((COND_ROWS, D_MODEL), lambda l, j: (0, 0)),
            pl.BlockSpec((1, D_MODEL, tn), lambda l, j: (l, 0, j)),
            pl.BlockSpec((1, 1, tn), lambda l, j: (l, 0, j)),
        ],
        out_specs=pl.BlockSpec((1, COND_ROWS, tn), lambda l, j: (l, 0, j)),
        out_shape=jax.ShapeDtypeStruct((DEPTH, COND_ROWS, n), F32),
        compiler_params=_cparams(2),
        name="modulation",
    )(cond, ada_w, ada_b.reshape(DEPTH, 1, n))


def _mod_spec(chunk):
    return pl.BlockSpec((COND_ROWS, D_MODEL), lambda i: (0, chunk))


def _mod_row(ref, i):
    return ref[pl.ds(_tile_group(i, PROJ_TM), 1), :]


_ROPE_SPEC = pl.BlockSpec((PROJ_TM, LANES), lambda i: (_rope_tile(i), 0))
_LANE_MAT_SPEC = _const_spec((LANES, LANES))
_UNIT_MAT_SPEC = _const_spec((PROJ_UNIT, PROJ_UNIT))


def _tok_spec(width):
    return pl.BlockSpec((PROJ_TM, width), lambda i: (i, 0))


_XP_SPEC = pl.BlockSpec((PROJ_TM, D_MODEL), lambda i: (jnp.minimum(i, N_PROJ_PROMPT - 1), 0))
_XS_SPEC = pl.BlockSpec((PROJ_TM, D_MODEL), lambda i: (jnp.maximum(i - N_PROJ_PROMPT, 0), 0))


def _cache_spec(*dims):
    nd = len(dims)
    return pl.BlockSpec((PROJ_BATCHES, 1) + dims,
                        lambda i: (jnp.minimum(i, N_PROJ_PROMPT - 1), 0) + (0,) * nd)


def _cache_shape(*dims):
    return jax.ShapeDtypeStruct((BATCH, 1) + dims, F32)


def _proj_att_kernel(xp_ref, xs_ref, gain_ref, sh_ref, sc_ref, w_ref, qg_ref, kg_ref, m_ref,
                     cos_ref, sp_ref, sn_ref, q_ref, k_ref, v_ref, ck_ref, cv_ref):
    i = pl.program_id(0)
    x = jnp.where(i < N_PROJ_PROMPT, xp_ref[...], xs_ref[...])
    h = _norm_mod(x, gain_ref[...], _mod_row(sh_ref, i), _mod_row(sc_ref, i)).astype(BF16)
    per = PROJ_UNIT // LANES
    nq, nk = ATT_HEADS // per, ATT_KV_HEADS // per

    def body(lat):
        rope = _rope_args(lat, cos_ref, sp_ref, sn_ref, ATT_HEAD_DIM)

        def emit(u, y):
            if u < nq + nk:
                y = _head_norm(y, m_ref, qg_ref[...] if u < nq else kg_ref[...])
            for t, yc in enumerate(_halves(y)):
                if u < nq:
                    _put(q_ref, u * per + t, _maybe_rope(yc, rope))
                elif u < nq + nk:
                    kn = _maybe_rope(yc, rope)
                    _put(k_ref, (u - nq) * per + t, kn)
                    if not lat:
                        _cache_rows(ck_ref, [(u - nq) * per + t], kn)
                else:
                    _put(v_ref, (u - nq - nk) * per + t, yc)
                    if not lat:
                        _cache_rows(cv_ref, [(u - nq - nk) * per + t], yc)

        _matmul_units(h, w_ref, nq + 2 * nk, PROJ_UNIT, emit)

    _by_tile_kind(i, body)


def _proj_att(xp, xs, mods, gain, w, qg, kg, tables):
    nq, nk = ATT_HEADS * ATT_HEAD_DIM, ATT_KV_HEADS * ATT_HEAD_DIM
    return pl.pallas_call(
        _proj_att_kernel,
        grid=(N_PROJ_TILES,),
        in_specs=[_XP_SPEC, _XS_SPEC, _const_spec((1, D_MODEL)), _mod_spec(0), _mod_spec(1),
                  _const_spec(w.shape), _const_spec((1, PROJ_UNIT)), _const_spec((1, PROJ_UNIT)),
                  _UNIT_MAT_SPEC, _ROPE_SPEC, _ROPE_SPEC, _ROPE_SPEC],
        out_specs=[_tok_spec(nq), _tok_spec(nk), _tok_spec(nk),
                   _cache_spec(ATT_KV_HEADS, SEQ, ATT_HEAD_DIM), _cache_spec(ATT_KV_HEADS, SEQ, ATT_HEAD_DIM)],
        out_shape=[jax.ShapeDtypeStruct((N_TOK, nq), BF16),
                   jax.ShapeDtypeStruct((N_TOK, nk), BF16),
                   jax.ShapeDtypeStruct((N_TOK, nk), BF16),
                   _cache_shape(ATT_KV_HEADS, SEQ, ATT_HEAD_DIM), _cache_shape(ATT_KV_HEADS, SEQ, ATT_HEAD_DIM)],
        compiler_params=_cparams(1),
        name="proj_att",
    )(xp, xs, gain, mods, mods, w, qg, kg, _group_mean_matrix(ATT_HEAD_DIM), *tables)


def _proj_diff_kernel(h_ref, w_ref, qg_ref, kg_ref, m_ref,
                      cos_ref, sp_ref, sn_ref, q_ref, k_ref, v_ref, ck_ref, cv_ref):
    i = pl.program_id(0)
    h = h_ref[...]
    per = PROJ_UNIT // LANES
    nu = DIFF_HEADS // per

    def body(lat):
        rope = _rope_args(lat, cos_ref, sp_ref, sn_ref, DIFF_HEAD_DIM)

        def emit(u, y):
            if u < 2 * nu:
                y = _head_norm(y, m_ref, qg_ref[...] if u < nu else kg_ref[...])
            for t, yc in enumerate(_halves(y)):
                hd = (u % nu) * per + t
                if u < nu:
                    _put(q_ref, hd, _maybe_rope(yc, rope))
                elif u < 2 * nu:
                    kn = _maybe_rope(yc, rope)
                    _put(k_ref, hd, kn)
                    if not lat:
                        _cache_rows(ck_ref, [hd, 0], kn[:, :HALF])
                        _cache_rows(ck_ref, [hd, 1], kn[:, HALF:])
                else:
                    _put(v_ref, hd, yc)
                    if not lat:
                        _cache_rows(cv_ref, [hd], yc)

        _matmul_units(h, w_ref, 3 * nu, PROJ_UNIT, emit)

    _by_tile_kind(i, body)


def _proj_diff(h, w, qg, kg, tables):
    n = DIFF_HEADS * 2 * DIFF_HEAD_DIM
    return pl.pallas_call(
        _proj_diff_kernel,
        grid=(N_PROJ_TILES,),
        in_specs=[_tok_spec(D_MODEL),
                  _const_spec(w.shape), _const_spec((1, PROJ_UNIT)), _const_spec((1, PROJ_UNIT)),
                  _UNIT_MAT_SPEC, _ROPE_SPEC, _ROPE_SPEC, _ROPE_SPEC],
        out_specs=[_tok_spec(n), _tok_spec(n), _tok_spec(n),
                   _cache_spec(DIFF_HEADS, 2, SEQ, DIFF_HEAD_DIM), _cache_spec(DIFF_HEADS, SEQ, 2 * DIFF_HEAD_DIM)],
        out_shape=[jax.ShapeDtypeStruct((N_TOK, n), BF16)] * 3
                  + [_cache_shape(DIFF_HEADS, 2, SEQ, DIFF_HEAD_DIM),
                     _cache_shape(DIFF_HEADS, SEQ, 2 * DIFF_HEAD_DIM)],
        compiler_params=_cparams(1),
        name="proj_diff",
    )(h, w, qg, kg, _group_mean_matrix(DIFF_HEAD_DIM), *tables)


def _dup_halves(yc):
    lo = _lane_lo(yc.shape)
    sw = pltpu.roll(yc, HALF, 1)
    return jnp.where(lo, yc, sw), jnp.where(lo, sw, yc)


def _proj_swa_kernel(h_ref, w_ref, qg_ref, kg_ref, m_ref,
                     cos_ref, sp_ref, sn_ref, q_ref, kd_ref, vd_ref, ck_ref, cv_ref):
    i = pl.program_id(0)
    h = h_ref[...]
    per = PROJ_UNIT // LANES
    nq = SWA_HEADS * SWA_HEAD_DIM // PROJ_UNIT
    nk = SWA_KV_HEADS * SWA_HEAD_DIM // PROJ_UNIT

    def body(lat):
        rope = _rope_args(lat, cos_ref, sp_ref, sn_ref, SWA_HEAD_DIM)

        def emit(u, y):
            if u < nq + nk:
                y = _head_norm(y, m_ref, qg_ref[...] if u < nq else kg_ref[...])
            for t, yc in enumerate(_halves(y)):
                if u < nq:
                    _put(q_ref, u * per + t, _maybe_rope(yc, rope))
                    continue
                if u < nq + nk:
                    j, c_ref, d_ref = (u - nq) * per + t, ck_ref, kd_ref
                    yc = _maybe_rope(yc, rope)
                else:
                    j, c_ref, d_ref = (u - nq - nk) * per + t, cv_ref, vd_ref
                for a, dup in enumerate(_dup_halves(yc)):
                    _put(d_ref, 2 * j + a, dup)
                    if not lat:
                        _cache_rows(c_ref, [2 * j + a], dup[:, :HALF])

        _matmul_units(h, w_ref, nq + 2 * nk, PROJ_UNIT, emit)

    _by_tile_kind(i, body)


def _proj_swa(h, w, qg, kg, tables):
    nq, nk = SWA_HEADS * SWA_HEAD_DIM, SWA_KV_HEADS * SWA_HEAD_DIM
    return pl.pallas_call(
        _proj_swa_kernel,
        grid=(N_PROJ_TILES,),
        in_specs=[_tok_spec(D_MODEL),
                  _const_spec(w.shape), _const_spec((1, PROJ_UNIT)), _const_spec((1, PROJ_UNIT)),
                  _UNIT_MAT_SPEC, _ROPE_SPEC, _ROPE_SPEC, _ROPE_SPEC],
        out_specs=[_tok_spec(nq), _tok_spec(2 * nk), _tok_spec(2 * nk),
                   _cache_spec(SWA_KV_HEADS, SEQ, SWA_HEAD_DIM), _cache_spec(SWA_KV_HEADS, SEQ, SWA_HEAD_DIM)],
        out_shape=[jax.ShapeDtypeStruct((N_TOK, nq), BF16),
                   jax.ShapeDtypeStruct((N_TOK, 2 * nk), BF16),
                   jax.ShapeDtypeStruct((N_TOK, 2 * nk), BF16),
                   _cache_shape(SWA_KV_HEADS, SEQ, SWA_HEAD_DIM), _cache_shape(SWA_KV_HEADS, SEQ, SWA_HEAD_DIM)],
        compiler_params=_cparams(1),
        name="proj_swa",
    )(h, w, qg, kg, _group_mean_matrix(SWA_HEAD_DIM), *tables)


def _mla_lane_matrices():
    everything = lambda lane: lane >= 0
    return (_lane_sum_matrix(everything, everything),
            _lane_sum_matrix(lambda lane: lane < HALF, everything),
            _lane_sum_matrix(lambda lane: lane >= HALF, everything))


def _proj_mla_kernel(h_ref, w_in_ref, qa_ref, kva_ref, w_uq_ref,
                     qg_ref, qgp_ref, all_ref, lo_ref, hi_ref, cos_ref, sp_ref, sn_ref,
                     qn_ref, qp_ref, ckv_ref, kpe_ref, c_ckv_ref, c_kpe_ref):
    i = pl.program_id(0)
    y = _dot(h_ref[...], w_in_ref[...])
    c_q = y[:, :MLA_Q_RANK]
    c_kv = y[:, MLA_Q_RANK:MLA_Q_RANK + MLA_KV_RANK]
    kpe = y[:, MLA_Q_RANK + MLA_KV_RANK:]
    kpe_ref[...] = kpe
    ckv = c_kv * lax.rsqrt(jnp.mean(c_kv * c_kv, axis=-1, keepdims=True) + EPS) * kva_ref[...]
    ckv_ref[...] = ckv.astype(BF16)
    cq = (c_q * lax.rsqrt(jnp.mean(c_q * c_q, axis=-1, keepdims=True) + EPS) * qa_ref[...]).astype(BF16)
    lo = _lane_lo((PROJ_TM, LANES))
    inv_d = 1.0 / (MLA_NOPE + MLA_ROPE)

    def body(lat):
        if not lat:
            _cache_rows(c_ckv_ref, [], ckv)
            _cache_rows(c_kpe_ref, [], kpe[:, :MLA_ROPE])

        def emit(j, yq):
            pe = yq[:, 2 * LANES:]
            pe_sq = _sq_bf16(pe)
            rs = []
            for a, half_ref in enumerate((lo_ref, hi_ref)):
                nope = yq[:, a * LANES:(a + 1) * LANES]
                ss = _dot(_sq_bf16(nope), all_ref[...]) + _dot(pe_sq, half_ref[...])
                r = lax.rsqrt(ss * inv_d + EPS)
                rs.append(r)
                _put(qn_ref, 2 * j + a, nope * r * qg_ref[...])
            pe = pe * jnp.where(lo, rs[0], rs[1]) * qgp_ref[...]
            if lat:
                pe = _rope(pe, cos_ref[...], sp_ref[...], sn_ref[...], MLA_ROPE // 4)
            _put(qp_ref, j, pe)

        _matmul_units(cq, w_uq_ref, MLA_HEADS // 2, 3 * LANES, emit)

    _by_tile_kind(i, body)


def _proj_mla(h, w_in, qa, kva, w_uq, qg, qgp, tables):
    n_nope = MLA_HEADS * MLA_NOPE
    n_pe = MLA_HEADS * MLA_ROPE
    return pl.pallas_call(
        _proj_mla_kernel,
        grid=(N_PROJ_TILES,),
        in_specs=[_tok_spec(D_MODEL),
                  _const_spec(w_in.shape), _const_spec((1, MLA_Q_RANK)), _const_spec((1, MLA_KV_RANK)),
                  _const_spec(w_uq.shape), _const_spec((1, LANES)), _const_spec((1, LANES)),
                  _LANE_MAT_SPEC, _LANE_MAT_SPEC, _LANE_MAT_SPEC,
                  _ROPE_SPEC, _ROPE_SPEC, _ROPE_SPEC],
        out_specs=[_tok_spec(n_nope), _tok_spec(n_pe), _tok_spec(MLA_KV_RANK), _tok_spec(LANES),
                   _cache_spec(SEQ, MLA_KV_RANK), _cache_spec(SEQ, MLA_ROPE)],
        out_shape=[jax.ShapeDtypeStruct((N_TOK, n_nope), BF16),
                   jax.ShapeDtypeStruct((N_TOK, n_pe), BF16),
                   jax.ShapeDtypeStruct((N_TOK, MLA_KV_RANK), BF16),
                   jax.ShapeDtypeStruct((N_TOK, LANES), F32),
                   _cache_shape(SEQ, MLA_KV_RANK), _cache_shape(SEQ, MLA_ROPE)],
        compiler_params=_cparams(1),
        name="proj_mla",
    )(h, w_in, qa, kva, w_uq, qg, qgp, *_mla_lane_matrices(), *tables)


def _mla_expand_kernel(ckv_ref, kpe_ref, w_ref, kg_ref, kgp_ref, sum_ref, lo_ref,
                       cos_ref, sp_ref, sn_ref, kn_ref, kp_ref, v_ref, *, rope):
    i = pl.program_id(0)
    ckv = ckv_ref[...].astype(BF16)
    kpe = kpe_ref[...]
    pe_ss = _dot(_sq_bf16(kpe), lo_ref[...])
    pe_ss = jnp.concatenate([pe_ss, pe_ss], axis=1)
    lo = _lane_lo(kpe.shape)
    inv_d = 1.0 / (MLA_NOPE + MLA_ROPE)

    def body(lat):
        def emit(j, y):
            kn = jnp.concatenate([y[:, :LANES], y[:, 2 * LANES:3 * LANES]], axis=1)
            r = lax.rsqrt((_dot(_sq_bf16(kn), sum_ref[...]) + pe_ss) * inv_d + EPS)
            kn = kn * r * kg_ref[...]
            for a in range(2):
                _put(kn_ref, 2 * j + a, kn[:, a * LANES:(a + 1) * LANES])
                _put(v_ref, 2 * j + a, y[:, (2 * a + 1) * LANES:(2 * a + 2) * LANES])
            pe = kpe * jnp.where(lo, r[:, :LANES], r[:, LANES:]) * kgp_ref[...]
            if lat:
                pe = _rope(pe, cos_ref[...], sp_ref[...], sn_ref[...], MLA_ROPE // 4)
            _put(kp_ref, j, pe)

        _matmul_units(ckv, w_ref, MLA_HEADS // 2, 4 * LANES, emit)

    if rope:
        _by_tile_kind(i, body)
    else:
        body(False)


def _mla_expand(ckv, kpe_dup, w_ukv, kg, kgp, tables, rope):
    n = ckv.shape[0]
    n_nope = MLA_HEADS * MLA_NOPE
    n_pe = MLA_HEADS * MLA_ROPE
    _, m_lo, _ = _mla_lane_matrices()
    return pl.pallas_call(
        functools.partial(_mla_expand_kernel, rope=rope),
        grid=(n // PROJ_TM,),
        in_specs=[_tok_spec(MLA_KV_RANK), _tok_spec(LANES), _const_spec(w_ukv.shape),
                  _const_spec((1, PROJ_UNIT)), _const_spec((1, LANES)), _UNIT_MAT_SPEC, _LANE_MAT_SPEC,
                  _ROPE_SPEC, _ROPE_SPEC, _ROPE_SPEC],
        out_specs=[_tok_spec(n_nope), _tok_spec(n_pe), _tok_spec(n_nope)],
        out_shape=[jax.ShapeDtypeStruct((n, n_nope), BF16),
                   jax.ShapeDtypeStruct((n, n_pe), BF16),
                   jax.ShapeDtypeStruct((n, n_nope), BF16)],
        compiler_params=_cparams(1),
        name="mla_expand",
    )(ckv, kpe_dup, w_ukv, kg, kgp, _group_sum_matrix(), m_lo, *tables)


def _prompt_spec(width):
    return pl.BlockSpec((TM, width), lambda b: (b, 0))


def _latq_spec(rows, width):
    per = DEC_SEQ // rows
    return pl.BlockSpec((rows, width), lambda b, t: (N_PROMPT_TOK // rows + b * per + t, 0))


def _latkv_spec(width):
    return pl.BlockSpec((DEC_SEQ, width), lambda b, t: (LAT_BLOCK0 + b, 0))


def _lato_spec(rows):
    per = DEC_SEQ // rows
    return pl.BlockSpec((rows, D_MODEL), lambda b, t: (b * per + t, 0))


def _att_kernel(*refs, with_ctx):
    if with_ctx:
        q_ref, k_ref, v_ref, kc_ref, vc_ref, o_ref = refs
    else:
        q_ref, k_ref, v_ref, o_ref = refs
    tq = q_ref.shape[0]
    nu = ATT_UNIT_HEADS
    per_kv = ATT_HEADS // ATT_KV_HEADS // nu

    def scores(u):
        q = jnp.concatenate([_chunk(q_ref, u * nu + g) for g in range(nu)], axis=0)
        s_list = [_dot_nt(_chunk(k_ref, u // per_kv), q)]
        if with_ctx:
            s_list.append(_dot_nt(kc_ref[u // per_kv].astype(BF16), q))
        return s_list

    def finish(u, s_list):
        values = [_chunk(v_ref, u // per_kv)]
        if with_ctx:
            values.append(vc_ref[u // per_kv].astype(BF16))
        ps, inv = _softmax2_parts(s_list)
        o = _pv(ps, values) * inv
        for g in range(nu):
            o_ref[:, (u * nu + g) * LANES:(u * nu + g + 1) * LANES] = (
                o[:, g * tq:(g + 1) * tq].T.astype(o_ref.dtype))

    _head_pipeline(ATT_HEADS // nu, scores, finish)


def _att_attend(q, k, v, cache_k, cache_v):
    nk = ATT_KV_HEADS * ATT_HEAD_DIM
    out_p = pl.pallas_call(
        functools.partial(_att_kernel, with_ctx=False),
        grid=(N_PROMPT_TILES,),
        in_specs=[_prompt_spec(D_MODEL), _prompt_spec(nk), _prompt_spec(nk)],
        out_specs=_prompt_spec(D_MODEL),
        out_shape=jax.ShapeDtypeStruct((N_PROMPT_TOK, D_MODEL), BF16),
        compiler_params=_cparams(1),
        name="att_prompt",
    )(q, k, v)
    ctx = pl.BlockSpec((None, None, ATT_KV_HEADS, PAST_LEN, LANES), lambda b, t: (b, 0, 0, 0, 0))
    out_s = pl.pallas_call(
        functools.partial(_att_kernel, with_ctx=True),
        grid=(DEC_BATCH, TILES_PER_DEC),
        in_specs=[_latq_spec(TM, D_MODEL), _latkv_spec(nk), _latkv_spec(nk), ctx, ctx],
        out_specs=_lato_spec(TM),
        out_shape=jax.ShapeDtypeStruct((N_LAT_TOK, D_MODEL), BF16),
        compiler_params=_cparams(2),
        name="att_latent",
    )(q, k, v, cache_k, cache_v)
    return out_p, out_s


def _diff_kernel(*refs, lam_init, with_ctx):
    if with_ctx:
        (q_ref, k_ref, v_ref, kc_ref, vc_ref, lq1_ref, lk1_ref, lq2_ref, lk2_ref, sub_ref, o_ref) = refs
    else:
        (q_ref, k_ref, v_ref, lq1_ref, lk1_ref, lq2_ref, lk2_ref, sub_ref, o_ref) = refs
    tq = q_ref.shape[0]
    lam = (jnp.exp(jnp.sum(lq1_ref[...] * lk1_ref[...], axis=-1, keepdims=True))
           - jnp.exp(jnp.sum(lq2_ref[...] * lk2_ref[...], axis=-1, keepdims=True)) + lam_init)
    sub = sub_ref[...] * (1.0 - lam_init)

    def scores(hd):
        q = jnp.concatenate(_split_halves(_chunk(q_ref, hd)), axis=0)
        s_list = [_dot_nt(_chunk(k_ref, hd), q)]
        if with_ctx:
            s_list.append(_dot_nt(kc_ref[hd].astype(BF16), q))
        return s_list

    def finish(hd, s_list):
        values = [_chunk(v_ref, hd)]
        if with_ctx:
            values.append(vc_ref[hd].astype(BF16))
        ps, inv = _softmax2_parts(s_list)
        o = (_pv([p[:, :tq] for p in ps], values) * inv[:, :tq]
             - _pv([p[:, tq:] for p in ps], values) * (lam * inv[:, tq:]))
        o = o * lax.rsqrt(jnp.mean(o * o, axis=0, keepdims=True) + EPS) * sub
        o_ref[:, hd * LANES:(hd + 1) * LANES] = o.T.astype(o_ref.dtype)

    _head_pipeline(DIFF_HEADS, scores, finish)


def _diff_attend(q, k, v, cache_k_pair, cache_v, lq1, lk1, lq2, lk2, subln, lam_init):
    small = [lq1, lk1, lq2, lk2, subln]
    small_specs = [_const_spec(s.shape) for s in small]
    out_p = pl.pallas_call(
        functools.partial(_diff_kernel, lam_init=lam_init, with_ctx=False),
        grid=(N_PROMPT_TILES,),
        in_specs=[_prompt_spec(D_MODEL)] * 3 + small_specs,
        out_specs=_prompt_spec(D_MODEL),
        out_shape=jax.ShapeDtypeStruct((N_PROMPT_TOK, D_MODEL), BF16),
        compiler_params=_cparams(1),
        name="diff_prompt",
    )(q, k, v, *small)
    out_s = pl.pallas_call(
        functools.partial(_diff_kernel, lam_init=lam_init, with_ctx=True),
        grid=(DEC_BATCH, TILES_PER_DEC),
        in_specs=[_latq_spec(TM, D_MODEL), _latkv_spec(D_MODEL), _latkv_spec(D_MODEL),
                  pl.BlockSpec((None, DIFF_HEADS, PAST_LEN, LANES), lambda b, t: (b, 0, 0, 0)),
                  pl.BlockSpec((None, None, DIFF_HEADS, PAST_LEN, LANES), lambda b, t: (b, 0, 0, 0, 0))]
                 + small_specs,
        out_specs=_lato_spec(TM),
        out_shape=jax.ShapeDtypeStruct((N_LAT_TOK, D_MODEL), BF16),
        compiler_params=_cparams(2),
        name="diff_latent",
    )(q, k, v, cache_k_pair, cache_v, *small)
    return out_p, out_s


def _swa_pipeline(q_ref, sink_ref, score_fns, value_fns, o_ref):
    tq = q_ref.shape[0]
    per_kv = SWA_HEADS // SWA_KV_HEADS // 2
    first = lax.broadcasted_iota(jnp.int32, (LANES, tq), 0) < HALF

    def scores(c):
        q = jnp.concatenate(_split_halves(_chunk(q_ref, c)), axis=0)
        return [fn(c // per_kv, q) for fn in score_fns]

    def finish(c, s_list):
        sink = jnp.concatenate([jnp.full((1, tq), sink_ref[2 * c + a] * LOG2E, F32) for a in range(2)],
                               axis=1)
        ps, inv = _softmax2_parts(s_list, extra=sink)
        o = _pv(ps, [fn(c // per_kv) for fn in value_fns]) * inv
        oc = jnp.where(first, o[:, :tq], o[:, tq:])
        o_ref[:, c * LANES:(c + 1) * LANES] = oc.T.astype(o_ref.dtype)

    _head_pipeline(SWA_HEADS // 2, scores, finish)


def _swa_prompt_kernel(sink_ref, q_ref, k_ref, v_ref, o_ref):
    _swa_pipeline(q_ref, sink_ref, [lambda kv, q: _dot_nt(_chunk(k_ref, kv), q)],
                  [lambda kv: _chunk(v_ref, kv)], o_ref)


def _swa_latent_kernel(sink_ref, q_ref, k_ref, v_ref, kc_ref, vc_ref, o_ref):
    n = pl.program_id(1)
    tq = q_ref.shape[0]
    span = 3 * SWA_QB
    start = pl.multiple_of(jnp.clip((n - 1) * SWA_QB, 0, DEC_SEQ - span), SWA_QB)
    cols = lax.broadcasted_iota(jnp.int32, (span, 2 * tq), 1)
    qpos = n * SWA_QB + jnp.bitwise_and(cols, tq - 1)
    kpos = start + lax.broadcasted_iota(jnp.int32, (span, 2 * tq), 0)
    valid = jnp.abs(qpos - kpos) <= WINDOW

    def local(ref, kv):
        return ref[pl.ds(start, span), kv * LANES:(kv + 1) * LANES]

    _swa_pipeline(q_ref, sink_ref,
                  [lambda kv, q: jnp.where(valid, _dot_nt(local(k_ref, kv), q), -1e30),
                   lambda kv, q: _dot_nt(kc_ref[kv], q)],
                  [lambda kv: local(v_ref, kv), lambda kv: vc_ref[kv]], o_ref)


def _swa_attend(q, kd, vd, cache_kd, cache_vd, sink):
    nkd = 2 * SWA_KV_HEADS * SWA_HEAD_DIM
    smem = pl.BlockSpec(memory_space=pltpu.SMEM)
    out_p = pl.pallas_call(
        _swa_prompt_kernel,
        grid=(N_PROMPT_TILES,),
        in_specs=[smem, _prompt_spec(D_MODEL), _prompt_spec(nkd), _prompt_spec(nkd)],
        out_specs=_prompt_spec(D_MODEL),
        out_shape=jax.ShapeDtypeStruct((N_PROMPT_TOK, D_MODEL), BF16),
        compiler_params=_cparams(1),
        name="swa_prompt",
    )(sink, q, kd, vd)
    ctx = pl.BlockSpec((None, SWA_KV_HEADS, PAST_LEN, LANES), lambda b, n: (b, 0, 0, 0))
    out_s = pl.pallas_call(
        _swa_latent_kernel,
        grid=(DEC_BATCH, DEC_SEQ // SWA_QB),
        in_specs=[smem, _latq_spec(SWA_QB, D_MODEL), _latkv_spec(nkd), _latkv_spec(nkd), ctx, ctx],
        out_specs=_lato_spec(SWA_QB),
        out_shape=jax.ShapeDtypeStruct((N_LAT_TOK, D_MODEL), BF16),
        compiler_params=_cparams(2),
        name="swa_latent",
    )(sink, q, kd, vd, cache_kd, cache_vd)
    return out_p, out_s


def _mla_kernel(*refs, with_ctx):
    if with_ctx:
        (qn_ref, qp_ref, kn_ref, kp_ref, v_ref, knc_ref, kpc_ref, vc_ref, o_ref) = refs
    else:
        (qn_ref, qp_ref, kn_ref, kp_ref, v_ref, o_ref) = refs

    def scores(hd):
        j, a = hd // 2, hd % 2
        q = jnp.concatenate([_chunk(qn_ref, hd), _split_halves(_chunk(qp_ref, j))[a]], axis=1)
        s_list = [_dot_nt(jnp.concatenate([_chunk(kn_ref, hd), _chunk(kp_ref, j)], axis=1), q)]
        if with_ctx:
            s_list.append(_dot_nt(jnp.concatenate([_chunk(knc_ref, hd), _chunk(kpc_ref, j)], axis=1), q))
        return s_list

    def finish(hd, s_list):
        values = [_chunk(v_ref, hd)]
        if with_ctx:
            values.append(_chunk(vc_ref, hd))
        ps, inv = _softmax2_parts(s_list)
        o_ref[:, hd * LANES:(hd + 1) * LANES] = (_pv(ps, values) * inv).T.astype(o_ref.dtype)

    _head_pipeline(MLA_HEADS, scores, finish)


def _mla_attend(qn, qp, kn, kp, v, knc, kpc, vc):
    n_pe = MLA_HEADS * MLA_ROPE
    out_p = pl.pallas_call(
        functools.partial(_mla_kernel, with_ctx=False),
        grid=(N_PROMPT_TILES,),
        in_specs=[_prompt_spec(D_MODEL), _prompt_spec(n_pe), _prompt_spec(D_MODEL), _prompt_spec(n_pe),
                  _prompt_spec(D_MODEL)],
        out_specs=_prompt_spec(D_MODEL),
        out_shape=jax.ShapeDtypeStruct((N_PROMPT_TOK, D_MODEL), BF16),
        compiler_params=_cparams(1),
        name="mla_prompt",
    )(qn, qp, kn, kp, v)

    def ctx(width):
        return pl.BlockSpec((PAST_LEN, width), lambda b, t: (b, 0))

    out_s = pl.pallas_call(
        functools.partial(_mla_kernel, with_ctx=True),
        grid=(DEC_BATCH, TILES_PER_DEC),
        in_specs=[_latq_spec(TM, D_MODEL), _latq_spec(TM, n_pe),
                  _latkv_spec(D_MODEL), _latkv_spec(n_pe), _latkv_spec(D_MODEL),
                  ctx(D_MODEL), ctx(n_pe), ctx(D_MODEL)],
        out_specs=_lato_spec(TM),
        out_shape=jax.ShapeDtypeStruct((N_LAT_TOK, D_MODEL), BF16),
        compiler_params=_cparams(2),
        name="mla_latent",
    )(qn, qp, kn, kp, v, knc, kpc, vc)
    return out_p, out_s


def _omlp_kernel(*refs, first, last):
    refs = list(refs)
    ap_ref, as_ref, wo_ref = refs[:3]
    x_refs = refs[3:5] if first else refs[3:4]
    refs = refs[3 + len(x_refs):]
    g1_ref, gain_ref, sh_ref, sc_ref, g2_ref, w1c_ref, w2c_ref = refs[:7]
    refs = refs[7:]
    if last:
        op_ref, os_ref, wo_s, w1_s, w2_s = refs
    else:
        ngain_ref, nsh_ref, nsc_ref, o_ref, hn_ref, wo_s, w1_s, w2_s = refs
    s = pl.program_id(0)
    per = MLP_FF_CHUNK // MLP_LOAD_COLS
    n_chunks = D_FF // MLP_FF_CHUNK
    half = MLP_TM // 2

    @pl.when(s == 0)
    def _():
        wo_s[...] = wo_ref[...].astype(BF16)

    for part in range(per):
        @pl.when((s < N_LOAD_STEPS) & (s % per == part))
        def _(part=part):
            w1_s[s // per, :, part * MLP_LOAD_COLS:(part + 1) * MLP_LOAD_COLS] = w1c_ref[...].astype(BF16)

    @pl.when(s < N_LOAD_STEPS)
    def _():
        w2_s[s // per, pl.ds(pl.multiple_of((s % per) * MLP_LOAD_COLS, MLP_LOAD_COLS), MLP_LOAD_COLS), :] = (
            w2c_ref[...].astype(BF16))

    @pl.when(s >= N_LOAD_STEPS)
    def _():
        t = s - N_LOAD_STEPS
        is_prompt = t < N_MLP_PROMPT_TILES
        grp = _tile_group(t, MLP_TM)

        def mod(ref):
            return ref[pl.ds(grp, 1), :]

        rows = [slice(r * half, (r + 1) * half) for r in range(2)]
        o = [_dot(jnp.where(is_prompt, ap_ref[rw, :], as_ref[rw, :]), wo_s[...]) for rw in rows]
        x1, h, u0 = [], [], []
        for r, rw in enumerate(rows):
            x = jnp.where(is_prompt, x_refs[0][rw, :], x_refs[1][rw, :]) if first else x_refs[0][rw, :]
            x1.append(x + mod(g1_ref) * o[r])
            h.append(_norm_mod(x1[r], gain_ref[...], mod(sh_ref), mod(sc_ref)).astype(BF16))
            u0.append(_dot(h[r], w1_s[0]))
        h = jnp.concatenate(h, axis=0)
        acc = []

        def up(c):
            return jnp.concatenate(u0, axis=0) if c == 0 else _dot(h, w1_s[c])

        def down(c, u):
            u = jnp.square(jnp.maximum(u, 0.0)).astype(BF16)
            if c + 1 < n_chunks:
                y = _dot(u, w2_s[c])
                acc[:] = [y if not acc else acc[0] + y]
            else:
                acc[:] = [acc[0][rw] + _dot(u[rw], w2_s[c]) for rw in rows]

        _head_pipeline(n_chunks, up, down)
        for r, rw in enumerate(rows):
            out = x1[r] + mod(g2_ref) * acc[r]
            if last:
                @pl.when(is_prompt)
                def _(out=out, rw=rw):
                    op_ref[rw, :] = out

                @pl.when(jnp.logical_not(is_prompt))
                def _(out=out, rw=rw):
                    os_ref[rw, :] = out
            else:
                o_ref[rw, :] = out
                hn_ref[rw, :] = _norm_mod(out, ngain_ref[...], mod(nsh_ref), mod(nsc_ref)).astype(BF16)


def _omlp(attn_p, attn_s, w_o, x, mods, gain_ffn, w1_all, w2_all, layer, next_gain, next_mods):
    first, last = layer == 0, next_gain is None
    n_lat_tiles = N_LAT_TOK // MLP_TM

    def tok(s):
        return jnp.maximum(s - N_LOAD_STEPS, 0)

    p_spec = pl.BlockSpec((MLP_TM, D_MODEL), lambda s: (jnp.minimum(tok(s), N_MLP_PROMPT_TILES - 1), 0))
    l_spec = pl.BlockSpec((MLP_TM, D_MODEL),
                          lambda s: (jnp.clip(tok(s) - N_MLP_PROMPT_TILES, 0, n_lat_tiles - 1), 0))
    w1_spec = pl.BlockSpec((None, D_MODEL, MLP_LOAD_COLS),
                           lambda s: (layer, 0, jnp.minimum(s, N_LOAD_STEPS - 1)))
    w2_spec = pl.BlockSpec((None, MLP_LOAD_COLS, D_MODEL),
                           lambda s: (layer, jnp.minimum(s, N_LOAD_STEPS - 1), 0))
    t_spec = pl.BlockSpec((MLP_TM, D_MODEL), lambda s: (tok(s), 0))
    n_chunks = D_FF // MLP_FF_CHUNK
    split = ([p_spec, l_spec], [jax.ShapeDtypeStruct((N_PROMPT_TOK, D_MODEL), F32),
                                jax.ShapeDtypeStruct((N_LAT_TOK, D_MODEL), F32)])
    in_specs = ([p_spec, l_spec, _const_spec(w_o.shape)] + (split[0] if first else [t_spec])
                + [_mod_spec(2), _const_spec((1, D_MODEL)), _mod_spec(3), _mod_spec(4), _mod_spec(5),
                   w1_spec, w2_spec])
    args = ([attn_p, attn_s, w_o] + (list(x) if first else [x])
            + [mods, gain_ffn, mods, mods, mods, w1_all, w2_all])
    if last:
        out_specs, out_shape = split
    else:
        in_specs += [_const_spec((1, D_MODEL)), _mod_spec(0), _mod_spec(1)]
        args += [next_gain, next_mods, next_mods]
        out_specs = [t_spec, t_spec]
        out_shape = [jax.ShapeDtypeStruct((N_TOK, D_MODEL), F32), jax.ShapeDtypeStruct((N_TOK, D_MODEL), BF16)]
    return pl.pallas_call(
        functools.partial(_omlp_kernel, first=first, last=last),
        grid=(N_LOAD_STEPS + N_TOK // MLP_TM,),
        in_specs=in_specs,
        out_specs=out_specs,
        out_shape=out_shape,
        scratch_shapes=[pltpu.VMEM((D_MODEL, D_MODEL), BF16),
                        pltpu.VMEM((n_chunks, D_MODEL, MLP_FF_CHUNK), BF16),
                        pltpu.VMEM((n_chunks, MLP_FF_CHUNK, D_MODEL), BF16)],
        compiler_params=_cparams(1),
        name="omlp",
    )(*args)


def _row(v, scale=1.0):
    return (v.astype(F32) * scale).reshape(1, -1)


def _pair(v, scale=1.0):
    return (jnp.concatenate([v, v]).astype(F32) * scale).reshape(1, LANES)


def _unit_gain(v, scale=1.0):
    return (jnp.tile(v.astype(F32), PROJ_UNIT // v.shape[0]) * scale).reshape(1, PROJ_UNIT)


def kernel(x_prompt, x_sample, cache_att_k, cache_att_v, cache_diff_k, cache_diff_v, cache_swa_k, cache_swa_v, cache_mla_ckv, cache_mla_kpe, c, c_ctx, ada_w, ada_b, norm_mix, norm_ffn, att_w_qkv, att_q_norm, att_k_norm, att_w_o, diff_w_qkv, diff_q_norm, diff_k_norm, diff_lq1, diff_lk1, diff_lq2, diff_lk2, diff_subln, diff_w_o, swa_w_qkv, swa_q_norm, swa_k_norm, swa_sink, swa_w_o, mla_w_in, mla_q_a_norm, mla_kv_a_norm, mla_w_uq, mla_w_ukv, mla_q_norm, mla_k_norm, mla_w_o, mlp_w1, mlp_w2):
    xp = x_prompt.reshape(N_PROMPT_TOK, D_MODEL)
    xs = x_sample.reshape(N_LAT_TOK, D_MODEL)
    cond = jnp.concatenate([c_ctx[None], c, jnp.zeros((COND_ROWS - 1 - DEC_BATCH, D_MODEL), F32)], axis=0)
    mods_all = _modulation(cond, ada_w, ada_b)

    tab_att = _rope_tables(ATT_HEAD_DIM)
    tab_64 = _rope_tables(DIFF_HEAD_DIM)

    outs = {}
    x = (xp, xs)
    for layer in range(DEPTH):
        mods = mods_all[layer]
        gain_ffn = _row(norm_ffn[layer])
        if layer == 0:
            qs = ATT_HEAD_DIM ** -0.5 * LOG2E
            q, k, v, outs["att_k"], outs["att_v"] = _proj_att(
                xp, xs, mods, _row(norm_mix[layer]), att_w_qkv[0].astype(BF16),
                _unit_gain(att_q_norm[0], qs), _unit_gain(att_k_norm[0]), tab_att)
            attn_p, attn_s = _att_attend(q, k, v, cache_att_k, cache_att_v)
            w_o = att_w_o[0]
        elif layer == 1:
            qs = DIFF_HEAD_DIM ** -0.5 * LOG2E
            q, k, v, outs["diff_k"], outs["diff_v"] = _proj_diff(
                h, diff_w_qkv[0].astype(BF16),
                _unit_gain(diff_q_norm[0], qs), _unit_gain(diff_k_norm[0]), tab_64)
            lam_init = 0.8 - 0.6 * math.exp(-0.3 * layer)
            ck = cache_diff_k[:, 0].transpose(0, 1, 3, 2, 4).reshape(
                DEC_BATCH, DIFF_HEADS, PAST_LEN, LANES)
            attn_p, attn_s = _diff_attend(q, k, v, ck, cache_diff_v,
                                          _row(diff_lq1[0]), _row(diff_lk1[0]),
                                          _row(diff_lq2[0]), _row(diff_lk2[0]),
                                          diff_subln[0].astype(F32).reshape(LANES, 1), lam_init)
            w_o = diff_w_o[0]
        elif layer == 2:
            qs = SWA_HEAD_DIM ** -0.5 * LOG2E
            q, kd, vd, outs["swa_k"], outs["swa_v"] = _proj_swa(
                h, swa_w_qkv[0].astype(BF16),
                _unit_gain(swa_q_norm[0], qs), _unit_gain(swa_k_norm[0]), tab_64)
            ckd = jnp.concatenate([cache_swa_k[:, 0]] * 2, axis=-1).astype(BF16)
            cvd = jnp.concatenate([cache_swa_v[:, 0]] * 2, axis=-1).astype(BF16)
            attn_p, attn_s = _swa_attend(q, kd, vd, ckd, cvd, swa_sink[0].astype(F32))
            w_o = swa_w_o[0]
        else:
            qs = (MLA_NOPE + MLA_ROPE) ** -0.5 * LOG2E
            w_in = mla_w_in[0]
            w_in = jnp.concatenate([w_in, w_in[:, -MLA_ROPE:]], axis=1).astype(BF16)
            w_uq = mla_w_uq[0].reshape(MLA_Q_RANK, MLA_HEADS // 2, 2, MLA_NOPE + MLA_ROPE)
            w_uq = jnp.concatenate([w_uq[..., :MLA_NOPE].reshape(MLA_Q_RANK, MLA_HEADS // 2, 2 * MLA_NOPE),
                                    w_uq[..., MLA_NOPE:].reshape(MLA_Q_RANK, MLA_HEADS // 2, 2 * MLA_ROPE)],
                                   axis=-1).reshape(MLA_Q_RANK, -1).astype(BF16)
            w_ukv = mla_w_ukv[0].astype(BF16)
            qg, kg = mla_q_norm[0], mla_k_norm[0]
            qn, qp, ckv, kpe, outs["mla_ckv"], outs["mla_kpe"] = _proj_mla(
                h, w_in, _row(mla_q_a_norm[0]), _row(mla_kv_a_norm[0]), w_uq,
                _row(qg[:MLA_NOPE], qs), _pair(qg[MLA_NOPE:], qs), tab_64)
            kn, kp, vv = _mla_expand(ckv, kpe, w_ukv, _unit_gain(kg[:MLA_NOPE]), _pair(kg[MLA_NOPE:]),
                                     tab_64, True)
            c_ckv = cache_mla_ckv[:, 0].reshape(DEC_BATCH * PAST_LEN, MLA_KV_RANK)
            c_kpe = cache_mla_kpe[:, 0].reshape(DEC_BATCH * PAST_LEN, MLA_ROPE)
            c_kpe = jnp.concatenate([c_kpe, c_kpe], axis=-1)
            knc, kpc, vc = _mla_expand(c_ckv, c_kpe, w_ukv, _unit_gain(kg[:MLA_NOPE]), _pair(kg[MLA_NOPE:]),
                                       tab_64, False)
            attn_p, attn_s = _mla_attend(qn, qp, kn, kp, vv, knc, kpc, vc)
            w_o = mla_w_o[0]
        if layer + 1 < DEPTH:
            x, h = _omlp(attn_p, attn_s, w_o, x, mods, gain_ffn, mlp_w1, mlp_w2, layer,
                         _row(norm_mix[layer + 1]), mods_all[layer + 1])
        else:
            xp, xs = _omlp(attn_p, attn_s, w_o, x, mods, gain_ffn, mlp_w1, mlp_w2, layer, None, None)

    y_prompt = xp.reshape(BATCH, SEQ, D_MODEL)
    y_sample = xs.reshape(DEC_BATCH, DEC_SEQ, D_MODEL)
    return (y_prompt, y_sample, outs["att_k"], outs["att_v"], outs["diff_k"], outs["diff_v"],
            outs["swa_k"], outs["swa_v"], outs["mla_ckv"], outs["mla_kpe"])
```

```python
import functools
import math

import numpy as np
import jax
import jax.numpy as jnp
from jax import lax
from jax.experimental import pallas as pl
from jax.experimental.pallas import tpu as pltpu

D_MODEL = 1024
BATCH = 16
SEQ = 256
DEPTH = 4
DEC_BATCH = 2
DEC_SEQ = 1024
PAST_LEN = 256
GRID_W = 64
ROPE_THETA = 10000.0
EPS = 1e-6
D_FF = 4 * D_MODEL
MOD_CHUNKS = 6
LOG2E = 1.4426950408889634

ATT_HEADS, ATT_KV_HEADS, ATT_HEAD_DIM = 8, 2, 128
DIFF_HEADS, DIFF_HEAD_DIM = 8, 64
SWA_HEADS, SWA_KV_HEADS, SWA_HEAD_DIM, WINDOW = 16, 4, 64, 128
MLA_HEADS, MLA_NOPE, MLA_ROPE, MLA_VDIM = 8, 128, 64, 128
MLA_Q_RANK, MLA_KV_RANK = 512, 256

LANES = 128
HALF = LANES // 2
TM = 256
N_PROMPT_TOK = BATCH * SEQ
N_LAT_TOK = DEC_BATCH * DEC_SEQ
N_TOK = N_PROMPT_TOK + N_LAT_TOK
N_PROMPT_TILES = N_PROMPT_TOK // TM
TILES_PER_DEC = DEC_SEQ // TM
LAT_BLOCK0 = N_PROMPT_TOK // DEC_SEQ
COND_ROWS = 8
PROJ_TM = 512
PROJ_BATCHES = PROJ_TM // SEQ
N_PROJ_TILES = N_TOK // PROJ_TM
N_PROJ_PROMPT = N_PROMPT_TOK // PROJ_TM
PROJ_UNIT = 2 * LANES
MLP_TM = 512
MLP_FF_CHUNK = 512
MLP_LOAD_COLS = 256
N_LOAD_STEPS = D_FF // MLP_LOAD_COLS
N_MLP_PROMPT_TILES = N_PROMPT_TOK // MLP_TM
SWA_QB = 128
ATT_UNIT_HEADS = 4
PROMPT_SEQS = 4
VMEM_LIMIT = 56 * 1024 * 1024

F32 = jnp.float32
BF16 = jnp.bfloat16


def _cparams(n_axes):
    return pltpu.CompilerParams(dimension_semantics=("arbitrary",) * n_axes,
                                vmem_limit_bytes=VMEM_LIMIT)


def _dot(a, b):
    return jnp.dot(a, b, preferred_element_type=F32)


def _dot_nt(a, b):
    return lax.dot_general(a, b, (((1,), (1,)), ((), ())), preferred_element_type=F32)


def _dot_tn(a, b):
    return lax.dot_general(a, b, (((0,), (0,)), ((), ())), preferred_element_type=F32)


def _const_spec(shape):
    nd = len(shape)
    return pl.BlockSpec(shape, lambda *_: (0,) * nd, pipeline_mode=pl.Buffered(1))


def _chunk(ref, c, width=LANES):
    return ref[:, c * width:(c + 1) * width]


def _put(ref, c, val):
    ref[:, c * LANES:(c + 1) * LANES] = val.astype(ref.dtype)


def _tile_group(i, rows):
    n_prompt = N_PROMPT_TOK // rows
    return jnp.where(i < n_prompt, 0, 1 + (i - n_prompt) // (DEC_SEQ // rows))


def _rope_tile(i):
    return jnp.maximum(i - N_PROJ_PROMPT, 0) % (DEC_SEQ // PROJ_TM)


def _norm_mod(x, gain, shift, scale):
    ms = jnp.mean(x * x, axis=-1, keepdims=True)
    return x * lax.rsqrt(ms + EPS) * (gain * (1.0 + scale)) + shift


def _lane_lo(shape):
    return lax.broadcasted_iota(jnp.int32, shape, len(shape) - 1) < HALF


def _rope(y, cos, sin_prev, sin_next, quarter):
    return (y * cos + pltpu.roll(y, quarter, 1) * sin_prev
            + pltpu.roll(y, LANES - quarter, 1) * sin_next)


def _rope_tables(rot_dim):
    half = rot_dim // 2
    quarter = rot_dim // 4
    inv = np.float32(ROPE_THETA) ** (-np.arange(0, half, 2, dtype=np.float32) / np.float32(half))
    pos = np.arange(DEC_SEQ)
    row = (pos // GRID_W).astype(np.float32)
    col = (pos % GRID_W).astype(np.float32)
    lane = np.arange(LANES)
    dd = lane % rot_dim
    q = dd // quarter
    f = dd % quarter
    ang = np.where((q < 2)[None, :], row[:, None], col[:, None]) * inv[f][None, :]
    ang = ang.astype(np.float32)
    cos = np.cos(ang).astype(np.float32)
    sin = np.sin(ang).astype(np.float32)
    odd = (q % 2 == 1)[None, :]
    sin_prev = np.where(odd, sin, 0.0).astype(np.float32)
    sin_next = np.where(odd, 0.0, -sin).astype(np.float32)
    return jnp.asarray(cos), jnp.asarray(sin_prev), jnp.asarray(sin_next)


def _lane_sum_matrix(rows, cols, value=1.0):
    lane = np.arange(LANES)
    m = np.where(rows(lane)[:, None] & cols(lane)[None, :], value, 0.0).astype(np.float32)
    return jnp.asarray(m, dtype=BF16)


def _group_mean_matrix(group):
    lane = np.arange(PROJ_UNIT)
    m = np.where((lane[:, None] // group) == (lane[None, :] // group), 1.0 / group, 0.0)
    return jnp.asarray(m.astype(np.float32), dtype=BF16)


def _group_sum_matrix():
    lane = np.arange(PROJ_UNIT)
    m = np.where((lane[:, None] // LANES) == (lane[None, :] // LANES), 1.0, 0.0)
    return jnp.asarray(m.astype(np.float32), dtype=BF16)


def _sq_bf16(y):
    return (y * y).astype(BF16)


def _head_norm(y, m_ref, gain):
    return y * lax.rsqrt(_dot(_sq_bf16(y), m_ref[...]) + EPS) * gain


def _halves(y):
    return [y[:, t * LANES:(t + 1) * LANES] for t in range(y.shape[1] // LANES)]


def _matmul_units(h, w_ref, n_units, width, emit):
    def unit(u):
        return _dot(h, w_ref[:, u * width:(u + 1) * width])

    nxt = unit(0)
    for u in range(n_units):
        cur = nxt
        if u + 1 < n_units:
            nxt = unit(u + 1)
        emit(u, cur)


def _by_tile_kind(i, body):
    pl.when(i < N_PROJ_PROMPT)(functools.partial(body, False))
    pl.when(i >= N_PROJ_PROMPT)(functools.partial(body, True))


def _rope_args(lat, cos_ref, sp_ref, sn_ref, rot_dim):
    return (cos_ref[...], sp_ref[...], sn_ref[...], rot_dim // 4) if lat else None


def _maybe_rope(y, rope):
    return y if rope is None else _rope(y, *rope)


def _cache_rows(ref, index, val):
    for b in range(PROJ_BATCHES):
        ref[(b, 0) + tuple(index)] = val[b * SEQ:(b + 1) * SEQ]


def _softmax2_parts(s_list, extra=None):
    m = jnp.max(s_list[0], axis=0, keepdims=True)
    for s in s_list[1:]:
        m = jnp.maximum(m, jnp.max(s, axis=0, keepdims=True))
    if extra is not None:
        m = jnp.maximum(m, extra)
    ps = [jnp.exp2(s - m) for s in s_list]
    mass = ps[0].sum(axis=0, keepdims=True)
    for p in ps[1:]:
        mass = mass + p.sum(axis=0, keepdims=True)
    if extra is not None:
        mass = mass + jnp.exp2(extra - m)
    return [p.astype(BF16) for p in ps], 1.0 / mass


def _head_pipeline(n, scores, finish):
    nxt = scores(0)
    for h in range(n):
        cur = nxt
        if h + 1 < n:
            nxt = scores(h + 1)
        finish(h, cur)


def _seq_pipeline(refs, seqs, n, make):
    fns = []
    for b in range(seqs):
        views = [r.at[b * (r.shape[0] // seqs):(b + 1) * (r.shape[0] // seqs)] for r in refs]
        fns.append(make(views))
    _head_pipeline(seqs * n, lambda i: fns[i // n][0](i % n), lambda i, s: fns[i // n][1](i % n, s))


def _pv(ps, values):
    o = None
    for p, v in zip(ps, values):
        t = _dot_tn(v, p)
        o = t if o is None else o + t
    return o


def _split_halves(q):
    lo = _lane_lo(q.shape)
    zero = jnp.zeros_like(q)
    return jnp.where(lo, q, zero), jnp.where(lo, zero, q)


def _mod_kernel(cond_ref, w_ref, b_ref, o_ref):
    c = cond_ref[...]
    s = (c * jax.nn.sigmoid(c)).astype(BF16)
    o_ref[0] = _dot(s, w_ref[0].astype(BF16)) + b_ref[0]


def _modulation(cond, ada_w, ada_b):
    tn = 1536
    n = MOD_CHUNKS * D_MODEL
    return pl.pallas_call(
        _mod_kernel,
        grid=(DEPTH, n // tn),
        in_specs=[
            pl.BlockSpec((COND_ROWS, D_MODEL), lambda l, j: (0, 0)),
            pl.BlockSpec((1, D_MODEL, tn), lambda l, j: (l, 0, j)),
            pl.BlockSpec((1, 1, tn), lambda l, j: (l, 0, j)),
        ],
        out_specs=pl.BlockSpec((1, COND_ROWS, tn), lambda l, j: (l, 0, j)),
        out_shape=jax.ShapeDtypeStruct((DEPTH, COND_ROWS, n), F32),
        compiler_params=_cparams(2),
        name="modulation",
    )(cond, ada_w, ada_b.reshape(DEPTH, 1, n))


def _mod_spec(chunk):
    return pl.BlockSpec((COND_ROWS, D_MODEL), lambda i: (0, chunk))


def _mod_row(ref, i):
    return ref[pl.ds(_tile_group(i, PROJ_TM), 1), :]


_ROPE_SPEC = pl.BlockSpec((PROJ_TM, LANES), lambda i: (_rope_tile(i), 0))
_LANE_MAT_SPEC = _const_spec((LANES, LANES))
_UNIT_MAT_SPEC = _const_spec((PROJ_UNIT, PROJ_UNIT))


def _tok_spec(width):
    return pl.BlockSpec((PROJ_TM, width), lambda i: (i, 0))


_XP_SPEC = pl.BlockSpec((PROJ_TM, D_MODEL), lambda i: (jnp.minimum(i, N_PROJ_PROMPT - 1), 0))
_XS_SPEC = pl.BlockSpec((PROJ_TM, D_MODEL), lambda i: (jnp.maximum(i - N_PROJ_PROMPT, 0), 0))


def _cache_spec(*dims):
    nd = len(dims)
    return pl.BlockSpec((PROJ_BATCHES, 1) + dims,
                        lambda i: (jnp.minimum(i, N_PROJ_PROMPT - 1), 0) + (0,) * nd)


def _cache_shape(*dims):
    return jax.ShapeDtypeStruct((BATCH, 1) + dims, F32)


def _proj_att_kernel(xp_ref, xs_ref, gain_ref, sh_ref, sc_ref, w_ref, qg_ref, kg_ref, m_ref,
                     cos_ref, sp_ref, sn_ref, q_ref, k_ref, v_ref, ck_ref, cv_ref):
    i = pl.program_id(0)
    x = jnp.where(i < N_PROJ_PROMPT, xp_ref[...], xs_ref[...])
    h = _norm_mod(x, gain_ref[...], _mod_row(sh_ref, i), _mod_row(sc_ref, i)).astype(BF16)
    per = PROJ_UNIT // LANES
    nq, nk = ATT_HEADS // per, ATT_KV_HEADS // per

    def body(lat):
        rope = _rope_args(lat, cos_ref, sp_ref, sn_ref, ATT_HEAD_DIM)

        def emit(u, y):
            if u < nq + nk:
                y = _head_norm(y, m_ref, qg_ref[...] if u < nq else kg_ref[...])
            for t, yc in enumerate(_halves(y)):
                if u < nq:
                    _put(q_ref, u * per + t, _maybe_rope(yc, rope))
                elif u < nq + nk:
                    kn = _maybe_rope(yc, rope)
                    _put(k_ref, (u - nq) * per + t, kn)
                    if not lat:
                        _cache_rows(ck_ref, [(u - nq) * per + t], kn)
                else:
                    _put(v_ref, (u - nq - nk) * per + t, yc)
                    if not lat:
                        _cache_rows(cv_ref, [(u - nq - nk) * per + t], yc)

        _matmul_units(h, w_ref, nq + 2 * nk, PROJ_UNIT, emit)

    _by_tile_kind(i, body)


def _proj_att(xp, xs, mods, gain, w, qg, kg, tables):
    nq, nk = ATT_HEADS * ATT_HEAD_DIM, ATT_KV_HEADS * ATT_HEAD_DIM
    return pl.pallas_call(
        _proj_att_kernel,
        grid=(N_PROJ_TILES,),
        in_specs=[_XP_SPEC, _XS_SPEC, _const_spec((1, D_MODEL)), _mod_spec(0), _mod_spec(1),
                  _const_spec(w.shape), _const_spec((1, PROJ_UNIT)), _const_spec((1, PROJ_UNIT)),
                  _UNIT_MAT_SPEC, _ROPE_SPEC, _ROPE_SPEC, _ROPE_SPEC],
        out_specs=[_tok_spec(nq), _tok_spec(nk), _tok_spec(nk),
                   _cache_spec(ATT_KV_HEADS, SEQ, ATT_HEAD_DIM), _cache_spec(ATT_KV_HEADS, SEQ, ATT_HEAD_DIM)],
        out_shape=[jax.ShapeDtypeStruct((N_TOK, nq), BF16),
                   jax.ShapeDtypeStruct((N_TOK, nk), BF16),
                   jax.ShapeDtypeStruct((N_TOK, nk), BF16),
                   _cache_shape(ATT_KV_HEADS, SEQ, ATT_HEAD_DIM), _cache_shape(ATT_KV_HEADS, SEQ, ATT_HEAD_DIM)],
        compiler_params=_cparams(1),
        name="proj_att",
    )(xp, xs, gain, mods, mods, w, qg, kg, _group_mean_matrix(ATT_HEAD_DIM), *tables)


def _proj_diff_kernel(h_ref, w_ref, qg_ref, kg_ref, m_ref,
                      cos_ref, sp_ref, sn_ref, q_ref, k_ref, v_ref, ck_ref, cv_ref):
    i = pl.program_id(0)
    h = h_ref[...]
    per = PROJ_UNIT // LANES
    nu = DIFF_HEADS // per

    def body(lat):
        rope = _rope_args(lat, cos_ref, sp_ref, sn_ref, DIFF_HEAD_DIM)

        def emit(u, y):
            if u < 2 * nu:
                y = _head_norm(y, m_ref, qg_ref[...] if u < nu else kg_ref[...])
            for t, yc in enumerate(_halves(y)):
                hd = (u % nu) * per + t
                if u < nu:
                    _put(q_ref, hd, _maybe_rope(yc, rope))
                elif u < 2 * nu:
                    kn = _maybe_rope(yc, rope)
                    _put(k_ref, hd, kn)
                    if not lat:
                        _cache_rows(ck_ref, [hd, 0], kn[:, :HALF])
                        _cache_rows(ck_ref, [hd, 1], kn[:, HALF:])
                else:
                    _put(v_ref, hd, yc)
                    if not lat:
                        _cache_rows(cv_ref, [hd], yc)

        _matmul_units(h, w_ref, 3 * nu, PROJ_UNIT, emit)

    _by_tile_kind(i, body)


def _proj_diff(h, w, qg, kg, tables):
    n = DIFF_HEADS * 2 * DIFF_HEAD_DIM
    return pl.pallas_call(
        _proj_diff_kernel,
        grid=(N_PROJ_TILES,),
        in_specs=[_tok_spec(D_MODEL),
                  _const_spec(w.shape), _const_spec((1, PROJ_UNIT)), _const_spec((1, PROJ_UNIT)),
                  _UNIT_MAT_SPEC, _ROPE_SPEC, _ROPE_SPEC, _ROPE_SPEC],
        out_specs=[_tok_spec(n), _tok_spec(n), _tok_spec(n),
                   _cache_spec(DIFF_HEADS, 2, SEQ, DIFF_HEAD_DIM), _cache_spec(DIFF_HEADS, SEQ, 2 * DIFF_HEAD_DIM)],
        out_shape=[jax.ShapeDtypeStruct((N_TOK, n), BF16)] * 3
                  + [_cache_shape(DIFF_HEADS, 2, SEQ, DIFF_HEAD_DIM),
                     _cache_shape(DIFF_HEADS, SEQ, 2 * DIFF_HEAD_DIM)],
        compiler_params=_cparams(1),
        name="proj_diff",
    )(h, w, qg, kg, _group_mean_matrix(DIFF_HEAD_DIM), *tables)


def _dup_halves(yc):
    lo = _lane_lo(yc.shape)
    sw = pltpu.roll(yc, HALF, 1)
    return jnp.where(lo, yc, sw), jnp.where(lo, sw, yc)


def _proj_swa_kernel(h_ref, w_ref, qg_ref, kg_ref, m_ref,
                     cos_ref, sp_ref, sn_ref, q_ref, kd_ref, vd_ref, ck_ref, cv_ref):
    i = pl.program_id(0)
    h = h_ref[...]
    per = PROJ_UNIT // LANES
    nq = SWA_HEADS * SWA_HEAD_DIM // PROJ_UNIT
    nk = SWA_KV_HEADS * SWA_HEAD_DIM // PROJ_UNIT

    def body(lat):
        rope = _rope_args(lat, cos_ref, sp_ref, sn_ref, SWA_HEAD_DIM)

        def emit(u, y):
            if u < nq + nk:
                y = _head_norm(y, m_ref, qg_ref[...] if u < nq else kg_ref[...])
            for t, yc in enumerate(_halves(y)):
                if u < nq:
                    _put(q_ref, u * per + t, _maybe_rope(yc, rope))
                    continue
                if u < nq + nk:
                    j, c_ref, d_ref = (u - nq) * per + t, ck_ref, kd_ref
                    yc = _maybe_rope(yc, rope)
                else:
                    j, c_ref, d_ref = (u - nq - nk) * per + t, cv_ref, vd_ref
                for a, dup in enumerate(_dup_halves(yc)):
                    _put(d_ref, 2 * j + a, dup)
                    if not lat:
                        _cache_rows(c_ref, [2 * j + a], dup[:, :HALF])

        _matmul_units(h, w_ref, nq + 2 * nk, PROJ_UNIT, emit)

    _by_tile_kind(i, body)


def _proj_swa(h, w, qg, kg, tables):
    nq, nk = SWA_HEADS * SWA_HEAD_DIM, SWA_KV_HEADS * SWA_HEAD_DIM
    return pl.pallas_call(
        _proj_swa_kernel,
        grid=(N_PROJ_TILES,),
        in_specs=[_tok_spec(D_MODEL),
                  _const_spec(w.shape), _const_spec((1, PROJ_UNIT)), _const_spec((1, PROJ_UNIT)),
                  _UNIT_MAT_SPEC, _ROPE_SPEC, _ROPE_SPEC, _ROPE_SPEC],
        out_specs=[_tok_spec(nq), _tok_spec(2 * nk), _tok_spec(2 * nk),
                   _cache_spec(SWA_KV_HEADS, SEQ, SWA_HEAD_DIM), _cache_spec(SWA_KV_HEADS, SEQ, SWA_HEAD_DIM)],
        out_shape=[jax.ShapeDtypeStruct((N_TOK, nq), BF16),
                   jax.ShapeDtypeStruct((N_TOK, 2 * nk), BF16),
                   jax.ShapeDtypeStruct((N_TOK, 2 * nk), BF16),
                   _cache_shape(SWA_KV_HEADS, SEQ, SWA_HEAD_DIM), _cache_shape(SWA_KV_HEADS, SEQ, SWA_HEAD_DIM)],
        compiler_params=_cparams(1),
        name="proj_swa",
    )(h, w, qg, kg, _group_mean_matrix(SWA_HEAD_DIM), *tables)


def _mla_lane_matrices():
    everything = lambda lane: lane >= 0
    return (_lane_sum_matrix(everything, everything),
            _lane_sum_matrix(lambda lane: lane < HALF, everything),
            _lane_sum_matrix(lambda lane: lane >= HALF, everything))


def _proj_mla_kernel(h_ref, w_in_ref, qa_ref, kva_ref, w_uq_ref,
                     qg_ref, qgp_ref, all_ref, lo_ref, hi_ref, cos_ref, sp_ref, sn_ref,
                     qn_ref, qp_ref, ckv_ref, kpe_ref, c_ckv_ref, c_kpe_ref):
    i = pl.program_id(0)
    y = _dot(h_ref[...], w_in_ref[...])
    c_q = y[:, :MLA_Q_RANK]
    c_kv = y[:, MLA_Q_RANK:MLA_Q_RANK + MLA_KV_RANK]
    kpe = y[:, MLA_Q_RANK + MLA_KV_RANK:]
    kpe_ref[...] = kpe
    ckv = c_kv * lax.rsqrt(jnp.mean(c_kv * c_kv, axis=-1, keepdims=True) + EPS) * kva_ref[...]
    ckv_ref[...] = ckv.astype(BF16)
    cq = (c_q * lax.rsqrt(jnp.mean(c_q * c_q, axis=-1, keepdims=True) + EPS) * qa_ref[...]).astype(BF16)
    lo = _lane_lo((PROJ_TM, LANES))
    inv_d = 1.0 / (MLA_NOPE + MLA_ROPE)

    def body(lat):
        if not lat:
            _cache_rows(c_ckv_ref, [], ckv)
            _cache_rows(c_kpe_ref, [], kpe[:, :MLA_ROPE])

        def emit(j, yq):
            pe = yq[:, 2 * LANES:]
            pe_sq = _sq_bf16(pe)
            rs = []
            for a, half_ref in enumerate((lo_ref, hi_ref)):
                nope = yq[:, a * LANES:(a + 1) * LANES]
                ss = _dot(_sq_bf16(nope), all_ref[...]) + _dot(pe_sq, half_ref[...])
                r = lax.rsqrt(ss * inv_d + EPS)
                rs.append(r)
                _put(qn_ref, 2 * j + a, nope * r * qg_ref[...])
            pe = pe * jnp.where(lo, rs[0], rs[1]) * qgp_ref[...]
            if lat:
                pe = _rope(pe, cos_ref[...], sp_ref[...], sn_ref[...], MLA_ROPE // 4)
            _put(qp_ref, j, pe)

        _matmul_units(cq, w_uq_ref, MLA_HEADS // 2, 3 * LANES, emit)

    _by_tile_kind(i, body)


def _proj_mla(h, w_in, qa, kva, w_uq, qg, qgp, tables):
    n_nope = MLA_HEADS * MLA_NOPE
    n_pe = MLA_HEADS * MLA_ROPE
    return pl.pallas_call(
        _proj_mla_kernel,
        grid=(N_PROJ_TILES,),
        in_specs=[_tok_spec(D_MODEL),
                  _const_spec(w_in.shape), _const_spec((1, MLA_Q_RANK)), _const_spec((1, MLA_KV_RANK)),
                  _const_spec(w_uq.shape), _const_spec((1, LANES)), _const_spec((1, LANES)),
                  _LANE_MAT_SPEC, _LANE_MAT_SPEC, _LANE_MAT_SPEC,
                  _ROPE_SPEC, _ROPE_SPEC, _ROPE_SPEC],
        out_specs=[_tok_spec(n_nope), _tok_spec(n_pe), _tok_spec(MLA_KV_RANK), _tok_spec(LANES),
                   _cache_spec(SEQ, MLA_KV_RANK), _cache_spec(SEQ, MLA_ROPE)],
        out_shape=[jax.ShapeDtypeStruct((N_TOK, n_nope), BF16),
                   jax.ShapeDtypeStruct((N_TOK, n_pe), BF16),
                   jax.ShapeDtypeStruct((N_TOK, MLA_KV_RANK), BF16),
                   jax.ShapeDtypeStruct((N_TOK, LANES), F32),
                   _cache_shape(SEQ, MLA_KV_RANK), _cache_shape(SEQ, MLA_ROPE)],
        compiler_params=_cparams(1),
        name="proj_mla",
    )(h, w_in, qa, kva, w_uq, qg, qgp, *_mla_lane_matrices(), *tables)


def _mla_expand_kernel(ckv_ref, kpe_ref, w_ref, kg_ref, kgp_ref, sum_ref, lo_ref,
                       cos_ref, sp_ref, sn_ref, kn_ref, kp_ref, v_ref, *, rope):
    i = pl.program_id(0)
    ckv = ckv_ref[...].astype(BF16)
    kpe = kpe_ref[...]
    pe_ss = _dot(_sq_bf16(kpe), lo_ref[...])
    pe_ss = jnp.concatenate([pe_ss, pe_ss], axis=1)
    lo = _lane_lo(kpe.shape)
    inv_d = 1.0 / (MLA_NOPE + MLA_ROPE)

    def body(lat):
        def emit(j, y):
            kn = jnp.concatenate([y[:, :LANES], y[:, 2 * LANES:3 * LANES]], axis=1)
            r = lax.rsqrt((_dot(_sq_bf16(kn), sum_ref[...]) + pe_ss) * inv_d + EPS)
            kn = kn * r * kg_ref[...]
            for a in range(2):
                _put(kn_ref, 2 * j + a, kn[:, a * LANES:(a + 1) * LANES])
                _put(v_ref, 2 * j + a, y[:, (2 * a + 1) * LANES:(2 * a + 2) * LANES])
            pe = kpe * jnp.where(lo, r[:, :LANES], r[:, LANES:]) * kgp_ref[...]
            if lat:
                pe = _rope(pe, cos_ref[...], sp_ref[...], sn_ref[...], MLA_ROPE // 4)
            _put(kp_ref, j, pe)

        _matmul_units(ckv, w_ref, MLA_HEADS // 2, 4 * LANES, emit)

    if rope:
        _by_tile_kind(i, body)
    else:
        body(False)


def _mla_expand(ckv, kpe_dup, w_ukv, kg, kgp, tables, rope):
    n = ckv.shape[0]
    n_nope = MLA_HEADS * MLA_NOPE
    n_pe = MLA_HEADS * MLA_ROPE
    _, m_lo, _ = _mla_lane_matrices()
    return pl.pallas_call(
        functools.partial(_mla_expand_kernel, rope=rope),
        grid=(n // PROJ_TM,),
        in_specs=[_tok_spec(MLA_KV_RANK), _tok_spec(LANES), _const_spec(w_ukv.shape),
                  _const_spec((1, PROJ_UNIT)), _const_spec((1, LANES)), _UNIT_MAT_SPEC, _LANE_MAT_SPEC,
                  _ROPE_SPEC, _ROPE_SPEC, _ROPE_SPEC],
        out_specs=[_tok_spec(n_nope), _tok_spec(n_pe), _tok_spec(n_nope)],
        out_shape=[jax.ShapeDtypeStruct((n, n_nope), BF16),
                   jax.ShapeDtypeStruct((n, n_pe), BF16),
                   jax.ShapeDtypeStruct((n, n_nope), BF16)],
        compiler_params=_cparams(1),
        name="mla_expand",
    )(ckv, kpe_dup, w_ukv, kg, kgp, _group_sum_matrix(), m_lo, *tables)


def _prompt_spec(width):
    return pl.BlockSpec((PROMPT_SEQS * TM, width), lambda b: (b, 0))


def _latq_spec(rows, width):
    per = DEC_SEQ // rows
    return pl.BlockSpec((rows, width), lambda b, t: (N_PROMPT_TOK // rows + b * per + t, 0))


def _latkv_spec(width):
    return pl.BlockSpec((DEC_SEQ, width), lambda b, t: (LAT_BLOCK0 + b, 0))


def _lato_spec(rows):
    per = DEC_SEQ // rows
    return pl.BlockSpec((rows, D_MODEL), lambda b, t: (b * per + t, 0))


def _att_kernel(*refs, with_ctx, seqs):
    if with_ctx:
        q_ref, k_ref, v_ref, kc_ref, vc_ref, o_ref = refs
    else:
        q_ref, k_ref, v_ref, o_ref = refs
    tq = q_ref.shape[0] // seqs
    nu = ATT_UNIT_HEADS
    per_kv = ATT_HEADS // ATT_KV_HEADS // nu

    def make(views):
        q_v, k_v, v_v, o_v = views

        def scores(u):
            q = jnp.concatenate([_chunk(q_v, u * nu + g) for g in range(nu)], axis=0)
            s_list = [_dot_nt(_chunk(k_v, u // per_kv), q)]
            if with_ctx:
                s_list.append(_dot_nt(kc_ref[u // per_kv].astype(BF16), q))
            return s_list

        def finish(u, s_list):
            values = [_chunk(v_v, u // per_kv)]
            if with_ctx:
                values.append(vc_ref[u // per_kv].astype(BF16))
            ps, inv = _softmax2_parts(s_list)
            o = _pv(ps, values) * inv
            for g in range(nu):
                o_v[:, (u * nu + g) * LANES:(u * nu + g + 1) * LANES] = (
                    o[:, g * tq:(g + 1) * tq].T.astype(o_v.dtype))

        return scores, finish

    _seq_pipeline((q_ref, k_ref, v_ref, o_ref), seqs, ATT_HEADS // nu, make)


def _att_attend(q, k, v, cache_k, cache_v):
    nk = ATT_KV_HEADS * ATT_HEAD_DIM
    out_p = pl.pallas_call(
        functools.partial(_att_kernel, with_ctx=False, seqs=PROMPT_SEQS),
        grid=(N_PROMPT_TILES // PROMPT_SEQS,),
        in_specs=[_prompt_spec(D_MODEL), _prompt_spec(nk), _prompt_spec(nk)],
        out_specs=_prompt_spec(D_MODEL),
        out_shape=jax.ShapeDtypeStruct((N_PROMPT_TOK, D_MODEL), BF16),
        compiler_params=_cparams(1),
        name="att_prompt",
    )(q, k, v)
    ctx = pl.BlockSpec((None, None, ATT_KV_HEADS, PAST_LEN, LANES), lambda b, t: (b, 0, 0, 0, 0))
    out_s = pl.pallas_call(
        functools.partial(_att_kernel, with_ctx=True, seqs=1),
        grid=(DEC_BATCH, TILES_PER_DEC),
        in_specs=[_latq_spec(TM, D_MODEL), _latkv_spec(nk), _latkv_spec(nk), ctx, ctx],
        out_specs=_lato_spec(TM),
        out_shape=jax.ShapeDtypeStruct((N_LAT_TOK, D_MODEL), BF16),
        compiler_params=_cparams(2),
        name="att_latent",
    )(q, k, v, cache_k, cache_v)
    return out_p, out_s


def _diff_kernel(*refs, lam_init, with_ctx, seqs):
    if with_ctx:
        (q_ref, k_ref, v_ref, kc_ref, vc_ref, lq1_ref, lk1_ref, lq2_ref, lk2_ref, sub_ref, o_ref) = refs
    else:
        (q_ref, k_ref, v_ref, lq1_ref, lk1_ref, lq2_ref, lk2_ref, sub_ref, o_ref) = refs
    tq = q_ref.shape[0] // seqs
    lam = (jnp.exp(jnp.sum(lq1_ref[...] * lk1_ref[...], axis=-1, keepdims=True))
           - jnp.exp(jnp.sum(lq2_ref[...] * lk2_ref[...], axis=-1, keepdims=True)) + lam_init)
    sub = sub_ref[...] * (1.0 - lam_init)

    def make(views):
        q_v, k_v, v_v, o_v = views

        def scores(hd):
            q = jnp.concatenate(_split_halves(_chunk(q_v, hd)), axis=0)
            s_list = [_dot_nt(_chunk(k_v, hd), q)]
            if with_ctx:
                s_list.append(_dot_nt(kc_ref[hd].astype(BF16), q))
            return s_list

        def finish(hd, s_list):
            values = [_chunk(v_v, hd)]
            if with_ctx:
                values.append(vc_ref[hd].astype(BF16))
            ps, inv = _softmax2_parts(s_list)
            o = (_pv([p[:, :tq] for p in ps], values) * inv[:, :tq]
                 - _pv([p[:, tq:] for p in ps], values) * (lam * inv[:, tq:]))
            o = o * lax.rsqrt(jnp.mean(o * o, axis=0, keepdims=True) + EPS) * sub
            o_v[:, hd * LANES:(hd + 1) * LANES] = o.T.astype(o_v.dtype)

        return scores, finish

    _seq_pipeline((q_ref, k_ref, v_ref, o_ref), seqs, DIFF_HEADS, make)


def _diff_attend(q, k, v, cache_k_pair, cache_v, lq1, lk1, lq2, lk2, subln, lam_init):
    small = [lq1, lk1, lq2, lk2, subln]
    small_specs = [_const_spec(s.shape) for s in small]
    out_p = pl.pallas_call(
        functools.partial(_diff_kernel, lam_init=lam_init, with_ctx=False, seqs=PROMPT_SEQS),
        grid=(N_PROMPT_TILES // PROMPT_SEQS,),
        in_specs=[_prompt_spec(D_MODEL)] * 3 + small_specs,
        out_specs=_prompt_spec(D_MODEL),
        out_shape=jax.ShapeDtypeStruct((N_PROMPT_TOK, D_MODEL), BF16),
        compiler_params=_cparams(1),
        name="diff_prompt",
    )(q, k, v, *small)
    out_s = pl.pallas_call(
        functools.partial(_diff_kernel, lam_init=lam_init, with_ctx=True, seqs=1),
        grid=(DEC_BATCH, TILES_PER_DEC),
        in_specs=[_latq_spec(TM, D_MODEL), _latkv_spec(D_MODEL), _latkv_spec(D_MODEL),
                  pl.BlockSpec((None, DIFF_HEADS, PAST_LEN, LANES), lambda b, t: (b, 0, 0, 0)),
                  pl.BlockSpec((None, None, DIFF_HEADS, PAST_LEN, LANES), lambda b, t: (b, 0, 0, 0, 0))]
                 + small_specs,
        out_specs=_lato_spec(TM),
        out_shape=jax.ShapeDtypeStruct((N_LAT_TOK, D_MODEL), BF16),
        compiler_params=_cparams(2),
        name="diff_latent",
    )(q, k, v, cache_k_pair, cache_v, *small)
    return out_p, out_s


def _swa_pipeline(q_ref, o_ref, seq_refs, sink_ref, score_fns, value_fns, seqs=1):
    tq = q_ref.shape[0] // seqs
    per_kv = SWA_HEADS // SWA_KV_HEADS // 2
    first = lax.broadcasted_iota(jnp.int32, (LANES, tq), 0) < HALF

    def make(views):
        q_v, o_v = views[:2]
        kv_views = views[2:]

        def scores(c):
            q = jnp.concatenate(_split_halves(_chunk(q_v, c)), axis=0)
            return [fn(kv_views, c // per_kv, q) for fn in score_fns]

        def finish(c, s_list):
            sink = jnp.concatenate([jnp.full((1, tq), sink_ref[2 * c + a] * LOG2E, F32) for a in range(2)],
                                   axis=1)
            ps, inv = _softmax2_parts(s_list, extra=sink)
            o = _pv(ps, [fn(kv_views, c // per_kv) for fn in value_fns]) * inv
            oc = jnp.where(first, o[:, :tq], o[:, tq:])
            o_v[:, c * LANES:(c + 1) * LANES] = oc.T.astype(o_v.dtype)

        return scores, finish

    _seq_pipeline((q_ref, o_ref) + tuple(seq_refs), seqs, SWA_HEADS // 2, make)


def _swa_prompt_kernel(sink_ref, q_ref, k_ref, v_ref, o_ref):
    _swa_pipeline(q_ref, o_ref, (k_ref, v_ref), sink_ref,
                  [lambda kv_v, kv, q: _dot_nt(_chunk(kv_v[0], kv), q)],
                  [lambda kv_v, kv: _chunk(kv_v[1], kv)], seqs=PROMPT_SEQS)


def _swa_latent_kernel(sink_ref, q_ref, k_ref, v_ref, kc_ref, vc_ref, o_ref):
    n = pl.program_id(1)
    tq = q_ref.shape[0]
    span = 3 * SWA_QB
    start = pl.multiple_of(jnp.clip((n - 1) * SWA_QB, 0, DEC_SEQ - span), SWA_QB)
    cols = lax.broadcasted_iota(jnp.int32, (span, 2 * tq), 1)
    qpos = n * SWA_QB + jnp.bitwise_and(cols, tq - 1)
    kpos = start + lax.broadcasted_iota(jnp.int32, (span, 2 * tq), 0)
    valid = jnp.abs(qpos - kpos) <= WINDOW

    def local(ref, kv):
        return ref[pl.ds(start, span), kv * LANES:(kv + 1) * LANES]

    _swa_pipeline(q_ref, o_ref, (), sink_ref,
                  [lambda _, kv, q: jnp.where(valid, _dot_nt(local(k_ref, kv), q), -1e30),
                   lambda _, kv, q: _dot_nt(kc_ref[kv], q)],
                  [lambda _, kv: local(v_ref, kv), lambda _, kv: vc_ref[kv]])


def _swa_attend(q, kd, vd, cache_kd, cache_vd, sink):
    nkd = 2 * SWA_KV_HEADS * SWA_HEAD_DIM
    smem = pl.BlockSpec(memory_space=pltpu.SMEM)
    out_p = pl.pallas_call(
        _swa_prompt_kernel,
        grid=(N_PROMPT_TILES // PROMPT_SEQS,),
        in_specs=[smem, _prompt_spec(D_MODEL), _prompt_spec(nkd), _prompt_spec(nkd)],
        out_specs=_prompt_spec(D_MODEL),
        out_shape=jax.ShapeDtypeStruct((N_PROMPT_TOK, D_MODEL), BF16),
        compiler_params=_cparams(1),
        name="swa_prompt",
    )(sink, q, kd, vd)
    ctx = pl.BlockSpec((None, SWA_KV_HEADS, PAST_LEN, LANES), lambda b, n: (b, 0, 0, 0))
    out_s = pl.pallas_call(
        _swa_latent_kernel,
        grid=(DEC_BATCH, DEC_SEQ // SWA_QB),
        in_specs=[smem, _latq_spec(SWA_QB, D_MODEL), _latkv_spec(nkd), _latkv_spec(nkd), ctx, ctx],
        out_specs=_lato_spec(SWA_QB),
        out_shape=jax.ShapeDtypeStruct((N_LAT_TOK, D_MODEL), BF16),
        compiler_params=_cparams(2),
        name="swa_latent",
    )(sink, q, kd, vd, cache_kd, cache_vd)
    return out_p, out_s


def _mla_kernel(*refs, with_ctx, seqs):
    if with_ctx:
        (qn_ref, qp_ref, kn_ref, kp_ref, v_ref, knc_ref, kpc_ref, vc_ref, o_ref) = refs
    else:
        (qn_ref, qp_ref, kn_ref, kp_ref, v_ref, o_ref) = refs

    def make(views):
        qn_v, qp_v, kn_v, kp_v, v_v, o_v = views

        def scores(hd):
            j, a = hd // 2, hd % 2
            q = jnp.concatenate([_chunk(qn_v, hd), _split_halves(_chunk(qp_v, j))[a]], axis=1)
            s_list = [_dot_nt(jnp.concatenate([_chunk(kn_v, hd), _chunk(kp_v, j)], axis=1), q)]
            if with_ctx:
                s_list.append(_dot_nt(jnp.concatenate([_chunk(knc_ref, hd), _chunk(kpc_ref, j)], axis=1), q))
            return s_list

        def finish(hd, s_list):
            values = [_chunk(v_v, hd)]
            if with_ctx:
                values.append(_chunk(vc_ref, hd))
            ps, inv = _softmax2_parts(s_list)
            o_v[:, hd * LANES:(hd + 1) * LANES] = (_pv(ps, values) * inv).T.astype(o_v.dtype)

        return scores, finish

    _seq_pipeline((qn_ref, qp_ref, kn_ref, kp_ref, v_ref, o_ref), seqs, MLA_HEADS, make)


def _mla_attend(qn, qp, kn, kp, v, knc, kpc, vc):
    n_pe = MLA_HEADS * MLA_ROPE
    out_p = pl.pallas_call(
        functools.partial(_mla_kernel, with_ctx=False, seqs=PROMPT_SEQS),
        grid=(N_PROMPT_TILES // PROMPT_SEQS,),
        in_specs=[_prompt_spec(D_MODEL), _prompt_spec(n_pe), _prompt_spec(D_MODEL), _prompt_spec(n_pe),
                  _prompt_spec(D_MODEL)],
        out_specs=_prompt_spec(D_MODEL),
        out_shape=jax.ShapeDtypeStruct((N_PROMPT_TOK, D_MODEL), BF16),
        compiler_params=_cparams(1),
        name="mla_prompt",
    )(qn, qp, kn, kp, v)

    def ctx(width):
        return pl.BlockSpec((PAST_LEN, width), lambda b, t: (b, 0))

    out_s = pl.pallas_call(
        functools.partial(_mla_kernel, with_ctx=True, seqs=1),
        grid=(DEC_BATCH, TILES_PER_DEC),
        in_specs=[_latq_spec(TM, D_MODEL), _latq_spec(TM, n_pe),
                  _latkv_spec(D_MODEL), _latkv_spec(n_pe), _latkv_spec(D_MODEL),
                  ctx(D_MODEL), ctx(n_pe), ctx(D_MODEL)],
        out_specs=_lato_spec(TM),
        out_shape=jax.ShapeDtypeStruct((N_LAT_TOK, D_MODEL), BF16),
        compiler_params=_cparams(2),
        name="mla_latent",
    )(qn, qp, kn, kp, v, knc, kpc, vc)
    return out_p, out_s


def _omlp_kernel(*refs, first, last):
    refs = list(refs)
    ap_ref, as_ref, wo_ref = refs[:3]
    x_refs = refs[3:5] if first else refs[3:4]
    refs = refs[3 + len(x_refs):]
    g1_ref, gain_ref, sh_ref, sc_ref, g2_ref, w1c_ref, w2c_ref = refs[:7]
    refs = refs[7:]
    if last:
        op_ref, os_ref, wo_s, w1_s, w2_s = refs
    else:
        ngain_ref, nsh_ref, nsc_ref, o_ref, hn_ref, wo_s, w1_s, w2_s = refs
    s = pl.program_id(0)
    per = MLP_FF_CHUNK // MLP_LOAD_COLS
    n_chunks = D_FF // MLP_FF_CHUNK
    half = MLP_TM // 2

    @pl.when(s == 0)
    def _():
        wo_s[...] = wo_ref[...].astype(BF16)

    for part in range(per):
        @pl.when((s < N_LOAD_STEPS) & (s % per == part))
        def _(part=part):
            w1_s[s // per, :, part * MLP_LOAD_COLS:(part + 1) * MLP_LOAD_COLS] = w1c_ref[...].astype(BF16)

    @pl.when(s < N_LOAD_STEPS)
    def _():
        w2_s[s // per, pl.ds(pl.multiple_of((s % per) * MLP_LOAD_COLS, MLP_LOAD_COLS), MLP_LOAD_COLS), :] = (
            w2c_ref[...].astype(BF16))

    @pl.when(s >= N_LOAD_STEPS)
    def _():
        t = s - N_LOAD_STEPS
        is_prompt = t < N_MLP_PROMPT_TILES
        grp = _tile_group(t, MLP_TM)

        def mod(ref):
            return ref[pl.ds(grp, 1), :]

        rows = [slice(r * half, (r + 1) * half) for r in range(2)]
        o = [_dot(jnp.where(is_prompt, ap_ref[rw, :], as_ref[rw, :]), wo_s[...]) for rw in rows]
        x1, h, u0 = [], [], []
        for r, rw in enumerate(rows):
            x = jnp.where(is_prompt, x_refs[0][rw, :], x_refs[1][rw, :]) if first else x_refs[0][rw, :]
            x1.append(x + mod(g1_ref) * o[r])
            h.append(_norm_mod(x1[r], gain_ref[...], mod(sh_ref), mod(sc_ref)).astype(BF16))
            u0.append(_dot(h[r], w1_s[0]))
        h = jnp.concatenate(h, axis=0)
        acc = []

        def up(c):
            return jnp.concatenate(u0, axis=0) if c == 0 else _dot(h, w1_s[c])

        def down(c, u):
            u = jnp.square(jnp.maximum(u, 0.0)).astype(BF16)
            if c + 1 < n_chunks:
                y = _dot(u, w2_s[c])
                acc[:] = [y if not acc else acc[0] + y]
            else:
                acc[:] = [acc[0][rw] + _dot(u[rw], w2_s[c]) for rw in rows]

        _head_pipeline(n_chunks, up, down)
        for r, rw in enumerate(rows):
            out = x1[r] + mod(g2_ref) * acc[r]
            if last:
                @pl.when(is_prompt)
                def _(out=out, rw=rw):
                    op_ref[rw, :] = out

                @pl.when(jnp.logical_not(is_prompt))
                def _(out=out, rw=rw):
                    os_ref[rw, :] = out
            else:
                o_ref[rw, :] = out
                hn_ref[rw, :] = _norm_mod(out, ngain_ref[...], mod(nsh_ref), mod(nsc_ref)).astype(BF16)


def _omlp(attn_p, attn_s, w_o, x, mods, gain_ffn, w1_all, w2_all, layer, next_gain, next_mods):
    first, last = layer == 0, next_gain is None
    n_lat_tiles = N_LAT_TOK // MLP_TM

    def tok(s):
        return jnp.maximum(s - N_LOAD_STEPS, 0)

    p_spec = pl.BlockSpec((MLP_TM, D_MODEL), lambda s: (jnp.minimum(tok(s), N_MLP_PROMPT_TILES - 1), 0))
    l_spec = pl.BlockSpec((MLP_TM, D_MODEL),
                          lambda s: (jnp.clip(tok(s) - N_MLP_PROMPT_TILES, 0, n_lat_tiles - 1), 0))
    w1_spec = pl.BlockSpec((None, D_MODEL, MLP_LOAD_COLS),
                           lambda s: (layer, 0, jnp.minimum(s, N_LOAD_STEPS - 1)))
    w2_spec = pl.BlockSpec((None, MLP_LOAD_COLS, D_MODEL),
                           lambda s: (layer, jnp.minimum(s, N_LOAD_STEPS - 1), 0))
    t_spec = pl.BlockSpec((MLP_TM, D_MODEL), lambda s: (tok(s), 0))
    n_chunks = D_FF // MLP_FF_CHUNK
    split = ([p_spec, l_spec], [jax.ShapeDtypeStruct((N_PROMPT_TOK, D_MODEL), F32),
                                jax.ShapeDtypeStruct((N_LAT_TOK, D_MODEL), F32)])
    in_specs = ([p_spec, l_spec, _const_spec(w_o.shape)] + (split[0] if first else [t_spec])
                + [_mod_spec(2), _const_spec((1, D_MODEL)), _mod_spec(3), _mod_spec(4), _mod_spec(5),
                   w1_spec, w2_spec])
    args = ([attn_p, attn_s, w_o] + (list(x) if first else [x])
            + [mods, gain_ffn, mods, mods, mods, w1_all, w2_all])
    if last:
        out_specs, out_shape = split
    else:
        in_specs += [_const_spec((1, D_MODEL)), _mod_spec(0), _mod_spec(1)]
        args += [next_gain, next_mods, next_mods]
        out_specs = [t_spec, t_spec]
        out_shape = [jax.ShapeDtypeStruct((N_TOK, D_MODEL), F32), jax.ShapeDtypeStruct((N_TOK, D_MODEL), BF16)]
    return pl.pallas_call(
        functools.partial(_omlp_kernel, first=first, last=last),
        grid=(N_LOAD_STEPS + N_TOK // MLP_TM,),
        in_specs=in_specs,
        out_specs=out_specs,
        out_shape=out_shape,
        scratch_shapes=[pltpu.VMEM((D_MODEL, D_MODEL), BF16),
                        pltpu.VMEM((n_chunks, D_MODEL, MLP_FF_CHUNK), BF16),
                        pltpu.VMEM((n_chunks, MLP_FF_CHUNK, D_MODEL), BF16)],
        compiler_params=_cparams(1),
        name="omlp",
    )(*args)


def _row(v, scale=1.0):
    return (v.astype(F32) * scale).reshape(1, -1)


def _pair(v, scale=1.0):
    return (jnp.concatenate([v, v]).astype(F32) * scale).reshape(1, LANES)


def _unit_gain(v, scale=1.0):
    return (jnp.tile(v.astype(F32), PROJ_UNIT // v.shape[0]) * scale).reshape(1, PROJ_UNIT)


def kernel(x_prompt, x_sample, cache_att_k, cache_att_v, cache_diff_k, cache_diff_v, cache_swa_k, cache_swa_v, cache_mla_ckv, cache_mla_kpe, c, c_ctx, ada_w, ada_b, norm_mix, norm_ffn, att_w_qkv, att_q_norm, att_k_norm, att_w_o, diff_w_qkv, diff_q_norm, diff_k_norm, diff_lq1, diff_lk1, diff_lq2, diff_lk2, diff_subln, diff_w_o, swa_w_qkv, swa_q_norm, swa_k_norm, swa_sink, swa_w_o, mla_w_in, mla_q_a_norm, mla_kv_a_norm, mla_w_uq, mla_w_ukv, mla_q_norm, mla_k_norm, mla_w_o, mlp_w1, mlp_w2):
    xp = x_prompt.reshape(N_PROMPT_TOK, D_MODEL)
    xs = x_sample.reshape(N_LAT_TOK, D_MODEL)
    cond = jnp.concatenate([c_ctx[None], c, jnp.zeros((COND_ROWS - 1 - DEC_BATCH, D_MODEL), F32)], axis=0)
    mods_all = _modulation(cond, ada_w, ada_b)

    tab_att = _rope_tables(ATT_HEAD_DIM)
    tab_64 = _rope_tables(DIFF_HEAD_DIM)

    outs = {}
    x = (xp, xs)
    for layer in range(DEPTH):
        mods = mods_all[layer]
        gain_ffn = _row(norm_ffn[layer])
        if layer == 0:
            qs = ATT_HEAD_DIM ** -0.5 * LOG2E
            q, k, v, outs["att_k"], outs["att_v"] = _proj_att(
                xp, xs, mods, _row(norm_mix[layer]), att_w_qkv[0].astype(BF16),
                _unit_gain(att_q_norm[0], qs), _unit_gain(att_k_norm[0]), tab_att)
            attn_p, attn_s = _att_attend(q, k, v, cache_att_k, cache_att_v)
            w_o = att_w_o[0]
        elif layer == 1:
            qs = DIFF_HEAD_DIM ** -0.5 * LOG2E
            q, k, v, outs["diff_k"], outs["diff_v"] = _proj_diff(
                h, diff_w_qkv[0].astype(BF16),
                _unit_gain(diff_q_norm[0], qs), _unit_gain(diff_k_norm[0]), tab_64)
            lam_init = 0.8 - 0.6 * math.exp(-0.3 * layer)
            ck = cache_diff_k[:, 0].transpose(0, 1, 3, 2, 4).reshape(
                DEC_BATCH, DIFF_HEADS, PAST_LEN, LANES)
            attn_p, attn_s = _diff_attend(q, k, v, ck, cache_diff_v,
                                          _row(diff_lq1[0]), _row(diff_lk1[0]),
                                          _row(diff_lq2[0]), _row(diff_lk2[0]),
                                          diff_subln[0].astype(F32).reshape(LANES, 1), lam_init)
            w_o = diff_w_o[0]
        elif layer == 2:
            qs = SWA_HEAD_DIM ** -0.5 * LOG2E
            q, kd, vd, outs["swa_k"], outs["swa_v"] = _proj_swa(
                h, swa_w_qkv[0].astype(BF16),
                _unit_gain(swa_q_norm[0], qs), _unit_gain(swa_k_norm[0]), tab_64)
            ckd = jnp.concatenate([cache_swa_k[:, 0]] * 2, axis=-1).astype(BF16)
            cvd = jnp.concatenate([cache_swa_v[:, 0]] * 2, axis=-1).astype(BF16)
            attn_p, attn_s = _swa_attend(q, kd, vd, ckd, cvd, swa_sink[0].astype(F32))
            w_o = swa_w_o[0]
        else:
            qs = (MLA_NOPE + MLA_ROPE) ** -0.5 * LOG2E
            w_in = mla_w_in[0]
            w_in = jnp.concatenate([w_in, w_in[:, -MLA_ROPE:]], axis=1).astype(BF16)
            w_uq = mla_w_uq[0].reshape(MLA_Q_RANK, MLA_HEADS // 2, 2, MLA_NOPE + MLA_ROPE)
            w_uq = jnp.concatenate([w_uq[..., :MLA_NOPE].reshape(MLA_Q_RANK, MLA_HEADS // 2, 2 * MLA_NOPE),
                                    w_uq[..., MLA_NOPE:].reshape(MLA_Q_RANK, MLA_HEADS // 2, 2 * MLA_ROPE)],
                                   axis=-1).reshape(MLA_Q_RANK, -1).astype(BF16)
            w_ukv = mla_w_ukv[0].astype(BF16)
            qg, kg = mla_q_norm[0], mla_k_norm[0]
            qn, qp, ckv, kpe, outs["mla_ckv"], outs["mla_kpe"] = _proj_mla(
                h, w_in, _row(mla_q_a_norm[0]), _row(mla_kv_a_norm[0]), w_uq,
                _row(qg[:MLA_NOPE], qs), _pair(qg[MLA_NOPE:], qs), tab_64)
            kn, kp, vv = _mla_expand(ckv, kpe, w_ukv, _unit_gain(kg[:MLA_NOPE]), _pair(kg[MLA_NOPE:]),
                                     tab_64, True)
            c_ckv = cache_mla_ckv[:, 0].reshape(DEC_BATCH * PAST_LEN, MLA_KV_RANK)
            c_kpe = cache_mla_kpe[:, 0].reshape(DEC_BATCH * PAST_LEN, MLA_ROPE)
            c_kpe = jnp.concatenate([c_kpe, c_kpe], axis=-1)
            knc, kpc, vc = _mla_expand(c_ckv, c_kpe, w_ukv, _unit_gain(kg[:MLA_NOPE]), _pair(kg[MLA_NOPE:]),
                                       tab_64, False)
            attn_p, attn_s = _mla_attend(qn, qp, kn, kp, vv, knc, kpc, vc)
            w_o = mla_w_o[0]
        if layer + 1 < DEPTH:
            x, h = _omlp(attn_p, attn_s, w_o, x, mods, gain_ffn, mlp_w1, mlp_w2, layer,
                         _row(norm_mix[layer + 1]), mods_all[layer + 1])
        else:
            xp, xs = _omlp(attn_p, attn_s, w_o, x, mods, gain_ffn, mlp_w1, mlp_w2, layer, None, None)

    y_prompt = xp.reshape(BATCH, SEQ, D_MODEL)
    y_sample = xs.reshape(DEC_BATCH, DEC_SEQ, D_MODEL)
    return (y_prompt, y_sample, outs["att_k"], outs["att_v"], outs["diff_k"], outs["diff_v"],
            outs["swa_k"], outs["swa_v"], outs["mla_ckv"], outs["mla_kpe"])
```

```python
import functools
import math

import numpy as np
import jax
import jax.numpy as jnp
from jax import lax
from jax.experimental import pallas as pl
from jax.experimental.pallas import tpu as pltpu

D_MODEL = 1024
BATCH = 16
SEQ = 256
DEPTH = 4
DEC_BATCH = 2
DEC_SEQ = 1024
PAST_LEN = 256
GRID_W = 64
ROPE_THETA = 10000.0
EPS = 1e-6
D_FF = 4 * D_MODEL
MOD_CHUNKS = 6
LOG2E = 1.4426950408889634

ATT_HEADS, ATT_KV_HEADS, ATT_HEAD_DIM = 8, 2, 128
DIFF_HEADS, DIFF_HEAD_DIM = 8, 64
SWA_HEADS, SWA_KV_HEADS, SWA_HEAD_DIM, WINDOW = 16, 4, 64, 128
MLA_HEADS, MLA_NOPE, MLA_ROPE, MLA_VDIM = 8, 128, 64, 128
MLA_Q_RANK, MLA_KV_RANK = 512, 256

LANES = 128
HALF = LANES // 2
TM = 256
N_PROMPT_TOK = BATCH * SEQ
N_LAT_TOK = DEC_BATCH * DEC_SEQ
N_TOK = N_PROMPT_TOK + N_LAT_TOK
N_PROMPT_TILES = N_PROMPT_TOK // TM
TILES_PER_DEC = DEC_SEQ // TM
LAT_BLOCK0 = N_PROMPT_TOK // DEC_SEQ
COND_ROWS = 8
PROJ_TM = 512
PROJ_BATCHES = PROJ_TM // SEQ
N_PROJ_TILES = N_TOK // PROJ_TM
N_PROJ_PROMPT = N_PROMPT_TOK // PROJ_TM
PROJ_UNIT = 2 * LANES
MLP_TM = 512
MLP_FF_CHUNK = 512
MLP_LOAD_COLS = 256
N_LOAD_STEPS = D_FF // MLP_LOAD_COLS
N_MLP_PROMPT_TILES = N_PROMPT_TOK // MLP_TM
SWA_QB = 128
ATT_UNIT_HEADS = 4
PROMPT_SEQS = 4
VMEM_LIMIT = 56 * 1024 * 1024

F32 = jnp.float32
BF16 = jnp.bfloat16


def _cparams(n_axes):
    return pltpu.CompilerParams(dimension_semantics=("arbitrary",) * n_axes,
                                vmem_limit_bytes=VMEM_LIMIT)


def _dot(a, b):
    return jnp.dot(a, b, preferred_element_type=F32)


def _dot_nt(a, b):
    return lax.dot_general(a, b, (((1,), (1,)), ((), ())), preferred_element_type=F32)


def _dot_tn(a, b):
    return lax.dot_general(a, b, (((0,), (0,)), ((), ())), preferred_element_type=F32)


def _const_spec(shape):
    nd = len(shape)
    return pl.BlockSpec(shape, lambda *_: (0,) * nd, pipeline_mode=pl.Buffered(1))


def _chunk(ref, c, width=LANES):
    return ref[:, c * width:(c + 1) * width]


def _put(ref, c, val):
    ref[:, c * LANES:(c + 1) * LANES] = val.astype(ref.dtype)


def _put_t(ref, c, val):
    ref[c * LANES:(c + 1) * LANES, :] = val.T.astype(ref.dtype)


def _tile_group(i, rows):
    n_prompt = N_PROMPT_TOK // rows
    return jnp.where(i < n_prompt, 0, 1 + (i - n_prompt) // (DEC_SEQ // rows))


def _rope_tile(i):
    return jnp.maximum(i - N_PROJ_PROMPT, 0) % (DEC_SEQ // PROJ_TM)


def _norm_mod(x, gain, shift, scale):
    ms = jnp.mean(x * x, axis=-1, keepdims=True)
    return x * lax.rsqrt(ms + EPS) * (gain * (1.0 + scale)) + shift


def _lane_lo(shape):
    return lax.broadcasted_iota(jnp.int32, shape, len(shape) - 1) < HALF


def _rope(y, cos, sin_prev, sin_next, quarter):
    return (y * cos + pltpu.roll(y, quarter, 1) * sin_prev
            + pltpu.roll(y, LANES - quarter, 1) * sin_next)


def _rope_tables(rot_dim):
    half = rot_dim // 2
    quarter = rot_dim // 4
    inv = np.float32(ROPE_THETA) ** (-np.arange(0, half, 2, dtype=np.float32) / np.float32(half))
    pos = np.arange(DEC_SEQ)
    row = (pos // GRID_W).astype(np.float32)
    col = (pos % GRID_W).astype(np.float32)
    lane = np.arange(LANES)
    dd = lane % rot_dim
    q = dd // quarter
    f = dd % quarter
    ang = np.where((q < 2)[None, :], row[:, None], col[:, None]) * inv[f][None, :]
    ang = ang.astype(np.float32)
    cos = np.cos(ang).astype(np.float32)
    sin = np.sin(ang).astype(np.float32)
    odd = (q % 2 == 1)[None, :]
    sin_prev = np.where(odd, sin, 0.0).astype(np.float32)
    sin_next = np.where(odd, 0.0, -sin).astype(np.float32)
    return jnp.asarray(cos), jnp.asarray(sin_prev), jnp.asarray(sin_next)


def _lane_sum_matrix(rows, cols, value=1.0):
    lane = np.arange(LANES)
    m = np.where(rows(lane)[:, None] & cols(lane)[None, :], value, 0.0).astype(np.float32)
    return jnp.asarray(m, dtype=BF16)


def _group_mean_matrix(group):
    lane = np.arange(PROJ_UNIT)
    m = np.where((lane[:, None] // group) == (lane[None, :] // group), 1.0 / group, 0.0)
    return jnp.asarray(m.astype(np.float32), dtype=BF16)


def _group_sum_matrix():
    lane = np.arange(PROJ_UNIT)
    m = np.where((lane[:, None] // LANES) == (lane[None, :] // LANES), 1.0, 0.0)
    return jnp.asarray(m.astype(np.float32), dtype=BF16)


def _sq_bf16(y):
    return (y * y).astype(BF16)


def _head_norm(y, m_ref, gain):
    return y * lax.rsqrt(_dot(_sq_bf16(y), m_ref[...]) + EPS) * gain


def _halves(y):
    return [y[:, t * LANES:(t + 1) * LANES] for t in range(y.shape[1] // LANES)]


def _matmul_units(h, w_ref, n_units, width, emit):
    def unit(u):
        return _dot(h, w_ref[:, u * width:(u + 1) * width])

    nxt = unit(0)
    for u in range(n_units):
        cur = nxt
        if u + 1 < n_units:
            nxt = unit(u + 1)
        emit(u, cur)


def _by_tile_kind(i, body):
    pl.when(i < N_PROJ_PROMPT)(functools.partial(body, False))
    pl.when(i >= N_PROJ_PROMPT)(functools.partial(body, True))


def _rope_args(lat, cos_ref, sp_ref, sn_ref, rot_dim):
    return (cos_ref[...], sp_ref[...], sn_ref[...], rot_dim // 4) if lat else None


def _maybe_rope(y, rope):
    return y if rope is None else _rope(y, *rope)


def _cache_rows(ref, index, val):
    for b in range(PROJ_BATCHES):
        ref[(b, 0) + tuple(index)] = val[b * SEQ:(b + 1) * SEQ]


def _softmax2_parts(s_list, extra=None):
    m = jnp.max(s_list[0], axis=0, keepdims=True)
    for s in s_list[1:]:
        m = jnp.maximum(m, jnp.max(s, axis=0, keepdims=True))
    if extra is not None:
        m = jnp.maximum(m, extra)
    ps = [jnp.exp2(s - m) for s in s_list]
    mass = ps[0].sum(axis=0, keepdims=True)
    for p in ps[1:]:
        mass = mass + p.sum(axis=0, keepdims=True)
    if extra is not None:
        mass = mass + jnp.exp2(extra - m)
    return [p.astype(BF16) for p in ps], 1.0 / mass


def _head_pipeline(n, scores, finish):
    nxt = scores(0)
    for h in range(n):
        cur = nxt
        if h + 1 < n:
            nxt = scores(h + 1)
        finish(h, cur)


def _seq_pipeline(refs, seqs, n, make):
    fns = []
    for b in range(seqs):
        views = [r.at[b * (r.shape[0] // seqs):(b + 1) * (r.shape[0] // seqs)] for r in refs]
        fns.append(make(views))
    _head_pipeline(seqs * n, lambda i: fns[i // n][0](i % n), lambda i, s: fns[i // n][1](i % n, s))


def _pv(ps, values):
    o = None
    for p, v in zip(ps, values):
        t = _dot_tn(v, p)
        o = t if o is None else o + t
    return o


def _split_halves(q):
    lo = _lane_lo(q.shape)
    zero = jnp.zeros_like(q)
    return jnp.where(lo, q, zero), jnp.where(lo, zero, q)


def _mod_kernel(cond_ref, w_ref, b_ref, o_ref):
    c = cond_ref[...]
    s = (c * jax.nn.sigmoid(c)).astype(BF16)
    o_ref[0] = _dot(s, w_ref[0].astype(BF16)) + b_ref[0]


def _modulation(cond, ada_w, ada_b):
    tn = 1536
    n = MOD_CHUNKS * D_MODEL
    return pl.pallas_call(
        _mod_kernel,
        grid=(DEPTH, n // tn),
        in_specs=[
            pl.BlockSpec((COND_ROWS, D_MODEL), lambda l, j: (0, 0)),
            pl.BlockSpec((1, D_MODEL, tn), lambda l, j: (l, 0, j)),
            pl.BlockSpec((1, 1, tn), lambda l, j: (l, 0, j)),
        ],
        out_specs=pl.BlockSpec((1, COND_ROWS, tn), lambda l, j: (l, 0, j)),
        out_shape=jax.ShapeDtypeStruct((DEPTH, COND_ROWS, n), F32),
        compiler_params=_cparams(2),
        name="modulation",
    )(cond, ada_w, ada_b.reshape(DEPTH, 1, n))


def _mod_spec(chunk):
    return pl.BlockSpec((COND_ROWS, D_MODEL), lambda i: (0, chunk))


def _mod_row(ref, i):
    return ref[pl.ds(_tile_group(i, PROJ_TM), 1), :]


_ROPE_SPEC = pl.BlockSpec((PROJ_TM, LANES), lambda i: (_rope_tile(i), 0))
_LANE_MAT_SPEC = _const_spec((LANES, LANES))
_UNIT_MAT_SPEC = _const_spec((PROJ_UNIT, PROJ_UNIT))


def _tok_spec(width):
    return pl.BlockSpec((PROJ_TM, width), lambda i: (i, 0))


def _feat_spec(width):
    return pl.BlockSpec((width, PROJ_TM), lambda i: (0, i))


_XP_SPEC = pl.BlockSpec((PROJ_TM, D_MODEL), lambda i: (jnp.minimum(i, N_PROJ_PROMPT - 1), 0))
_XS_SPEC = pl.BlockSpec((PROJ_TM, D_MODEL), lambda i: (jnp.maximum(i - N_PROJ_PROMPT, 0), 0))


def _cache_spec(*dims):
    nd = len(dims)
    return pl.BlockSpec((PROJ_BATCHES, 1) + dims,
                        lambda i: (jnp.minimum(i, N_PROJ_PROMPT - 1), 0) + (0,) * nd)


def _cache_shape(*dims):
    return jax.ShapeDtypeStruct((BATCH, 1) + dims, F32)


def _proj_att_kernel(xp_ref, xs_ref, gain_ref, sh_ref, sc_ref, w_ref, qg_ref, kg_ref, m_ref,
                     cos_ref, sp_ref, sn_ref, q_ref, k_ref, v_ref, ck_ref, cv_ref):
    i = pl.program_id(0)
    x = jnp.where(i < N_PROJ_PROMPT, xp_ref[...], xs_ref[...])
    h = _norm_mod(x, gain_ref[...], _mod_row(sh_ref, i), _mod_row(sc_ref, i)).astype(BF16)
    per = PROJ_UNIT // LANES
    nq, nk = ATT_HEADS // per, ATT_KV_HEADS // per

    def body(lat):
        rope = _rope_args(lat, cos_ref, sp_ref, sn_ref, ATT_HEAD_DIM)

        def emit(u, y):
            if u < nq + nk:
                y = _head_norm(y, m_ref, qg_ref[...] if u < nq else kg_ref[...])
            for t, yc in enumerate(_halves(y)):
                if u < nq:
                    _put_t(q_ref, u * per + t, _maybe_rope(yc, rope))
                elif u < nq + nk:
                    kn = _maybe_rope(yc, rope)
                    _put(k_ref, (u - nq) * per + t, kn)
                    if not lat:
                        _cache_rows(ck_ref, [(u - nq) * per + t], kn)
                else:
                    _put_t(v_ref, (u - nq - nk) * per + t, yc)
                    if not lat:
                        _cache_rows(cv_ref, [(u - nq - nk) * per + t], yc)

        _matmul_units(h, w_ref, nq + 2 * nk, PROJ_UNIT, emit)

    _by_tile_kind(i, body)


def _proj_att(xp, xs, mods, gain, w, qg, kg, tables):
    nq, nk = ATT_HEADS * ATT_HEAD_DIM, ATT_KV_HEADS * ATT_HEAD_DIM
    return pl.pallas_call(
        _proj_att_kernel,
        grid=(N_PROJ_TILES,),
        in_specs=[_XP_SPEC, _XS_SPEC, _const_spec((1, D_MODEL)), _mod_spec(0), _mod_spec(1),
                  _const_spec(w.shape), _const_spec((1, PROJ_UNIT)), _const_spec((1, PROJ_UNIT)),
                  _UNIT_MAT_SPEC, _ROPE_SPEC, _ROPE_SPEC, _ROPE_SPEC],
        out_specs=[_feat_spec(nq), _tok_spec(nk), _feat_spec(nk),
                   _cache_spec(ATT_KV_HEADS, SEQ, ATT_HEAD_DIM), _cache_spec(ATT_KV_HEADS, SEQ, ATT_HEAD_DIM)],
        out_shape=[jax.ShapeDtypeStruct((nq, N_TOK), BF16),
                   jax.ShapeDtypeStruct((N_TOK, nk), BF16),
                   jax.ShapeDtypeStruct((nk, N_TOK), BF16),
                   _cache_shape(ATT_KV_HEADS, SEQ, ATT_HEAD_DIM), _cache_shape(ATT_KV_HEADS, SEQ, ATT_HEAD_DIM)],
        compiler_params=_cparams(1),
        name="proj_att",
    )(xp, xs, gain, mods, mods, w, qg, kg, _group_mean_matrix(ATT_HEAD_DIM), *tables)


def _proj_diff_kernel(h_ref, w_ref, qg_ref, kg_ref, m_ref,
                      cos_ref, sp_ref, sn_ref, q_ref, k_ref, v_ref, ck_ref, cv_ref):
    i = pl.program_id(0)
    h = h_ref[...]
    per = PROJ_UNIT // LANES
    nu = DIFF_HEADS // per

    def body(lat):
        rope = _rope_args(lat, cos_ref, sp_ref, sn_ref, DIFF_HEAD_DIM)

        def emit(u, y):
            if u < 2 * nu:
                y = _head_norm(y, m_ref, qg_ref[...] if u < nu else kg_ref[...])
            for t, yc in enumerate(_halves(y)):
                hd = (u % nu) * per + t
                if u < nu:
                    _put(q_ref, hd, _maybe_rope(yc, rope))
                elif u < 2 * nu:
                    kn = _maybe_rope(yc, rope)
                    _put(k_ref, hd, kn)
                    if not lat:
                        _cache_rows(ck_ref, [hd, 0], kn[:, :HALF])
                        _cache_rows(ck_ref, [hd, 1], kn[:, HALF:])
                else:
                    _put(v_ref, hd, yc)
                    if not lat:
                        _cache_rows(cv_ref, [hd], yc)

        _matmul_units(h, w_ref, 3 * nu, PROJ_UNIT, emit)

    _by_tile_kind(i, body)


def _proj_diff(h, w, qg, kg, tables):
    n = DIFF_HEADS * 2 * DIFF_HEAD_DIM
    return pl.pallas_call(
        _proj_diff_kernel,
        grid=(N_PROJ_TILES,),
        in_specs=[_tok_spec(D_MODEL),
                  _const_spec(w.shape), _const_spec((1, PROJ_UNIT)), _const_spec((1, PROJ_UNIT)),
                  _UNIT_MAT_SPEC, _ROPE_SPEC, _ROPE_SPEC, _ROPE_SPEC],
        out_specs=[_tok_spec(n), _tok_spec(n), _tok_spec(n),
                   _cache_spec(DIFF_HEADS, 2, SEQ, DIFF_HEAD_DIM), _cache_spec(DIFF_HEADS, SEQ, 2 * DIFF_HEAD_DIM)],
        out_shape=[jax.ShapeDtypeStruct((N_TOK, n), BF16)] * 3
                  + [_cache_shape(DIFF_HEADS, 2, SEQ, DIFF_HEAD_DIM),
                     _cache_shape(DIFF_HEADS, SEQ, 2 * DIFF_HEAD_DIM)],
        compiler_params=_cparams(1),
        name="proj_diff",
    )(h, w, qg, kg, _group_mean_matrix(DIFF_HEAD_DIM), *tables)


def _dup_halves(yc):
    lo = _lane_lo(yc.shape)
    sw = pltpu.roll(yc, HALF, 1)
    return jnp.where(lo, yc, sw), jnp.where(lo, sw, yc)


def _proj_swa_kernel(h_ref, w_ref, qg_ref, kg_ref, m_ref,
                     cos_ref, sp_ref, sn_ref, q_ref, kd_ref, vd_ref, ck_ref, cv_ref):
    i = pl.program_id(0)
    h = h_ref[...]
    per = PROJ_UNIT // LANES
    nq = SWA_HEADS * SWA_HEAD_DIM // PROJ_UNIT
    nk = SWA_KV_HEADS * SWA_HEAD_DIM // PROJ_UNIT

    def body(lat):
        rope = _rope_args(lat, cos_ref, sp_ref, sn_ref, SWA_HEAD_DIM)

        def emit(u, y):
            if u < nq + nk:
                y = _head_norm(y, m_ref, qg_ref[...] if u < nq else kg_ref[...])
            for t, yc in enumerate(_halves(y)):
                if u < nq:
                    _put(q_ref, u * per + t, _maybe_rope(yc, rope))
                    continue
                if u < nq + nk:
                    j, c_ref, d_ref = (u - nq) * per + t, ck_ref, kd_ref
                    yc = _maybe_rope(yc, rope)
                else:
                    j, c_ref, d_ref = (u - nq - nk) * per + t, cv_ref, vd_ref
                for a, dup in enumerate(_dup_halves(yc)):
                    _put(d_ref, 2 * j + a, dup)
                    if not lat:
                        _cache_rows(c_ref, [2 * j + a], dup[:, :HALF])

        _matmul_units(h, w_ref, nq + 2 * nk, PROJ_UNIT, emit)

    _by_tile_kind(i, body)


def _proj_swa(h, w, qg, kg, tables):
    nq, nk = SWA_HEADS * SWA_HEAD_DIM, SWA_KV_HEADS * SWA_HEAD_DIM
    return pl.pallas_call(
        _proj_swa_kernel,
        grid=(N_PROJ_TILES,),
        in_specs=[_tok_spec(D_MODEL),
                  _const_spec(w.shape), _const_spec((1, PROJ_UNIT)), _const_spec((1, PROJ_UNIT)),
                  _UNIT_MAT_SPEC, _ROPE_SPEC, _ROPE_SPEC, _ROPE_SPEC],
        out_specs=[_tok_spec(nq), _tok_spec(2 * nk), _tok_spec(2 * nk),
                   _cache_spec(SWA_KV_HEADS, SEQ, SWA_HEAD_DIM), _cache_spec(SWA_KV_HEADS, SEQ, SWA_HEAD_DIM)],
        out_shape=[jax.ShapeDtypeStruct((N_TOK, nq), BF16),
                   jax.ShapeDtypeStruct((N_TOK, 2 * nk), BF16),
                   jax.ShapeDtypeStruct((N_TOK, 2 * nk), BF16),
                   _cache_shape(SWA_KV_HEADS, SEQ, SWA_HEAD_DIM), _cache_shape(SWA_KV_HEADS, SEQ, SWA_HEAD_DIM)],
        compiler_params=_cparams(1),
        name="proj_swa",
    )(h, w, qg, kg, _group_mean_matrix(SWA_HEAD_DIM), *tables)


def _mla_lane_matrices():
    everything = lambda lane: lane >= 0
    return (_lane_sum_matrix(everything, everything),
            _lane_sum_matrix(lambda lane: lane < HALF, everything),
            _lane_sum_matrix(lambda lane: lane >= HALF, everything))


def _proj_mla_kernel(h_ref, w_in_ref, qa_ref, kva_ref, w_uq_ref,
                     qg_ref, qgp_ref, all_ref, lo_ref, hi_ref, cos_ref, sp_ref, sn_ref,
                     qn_ref, qp_ref, ckv_ref, kpe_ref, c_ckv_ref, c_kpe_ref):
    i = pl.program_id(0)
    y = _dot(h_ref[...], w_in_ref[...])
    c_q = y[:, :MLA_Q_RANK]
    c_kv = y[:, MLA_Q_RANK:MLA_Q_RANK + MLA_KV_RANK]
    kpe = y[:, MLA_Q_RANK + MLA_KV_RANK:]
    kpe_ref[...] = kpe
    ckv = c_kv * lax.rsqrt(jnp.mean(c_kv * c_kv, axis=-1, keepdims=True) + EPS) * kva_ref[...]
    ckv_ref[...] = ckv.astype(BF16)
    cq = (c_q * lax.rsqrt(jnp.mean(c_q * c_q, axis=-1, keepdims=True) + EPS) * qa_ref[...]).astype(BF16)
    lo = _lane_lo((PROJ_TM, LANES))
    inv_d = 1.0 / (MLA_NOPE + MLA_ROPE)

    def body(lat):
        if not lat:
            _cache_rows(c_ckv_ref, [], ckv)
            _cache_rows(c_kpe_ref, [], kpe[:, :MLA_ROPE])

        def emit(j, yq):
            pe = yq[:, 2 * LANES:]
            pe_sq = _sq_bf16(pe)
            rs = []
            for a, half_ref in enumerate((lo_ref, hi_ref)):
                nope = yq[:, a * LANES:(a + 1) * LANES]
                ss = _dot(_sq_bf16(nope), all_ref[...]) + _dot(pe_sq, half_ref[...])
                r = lax.rsqrt(ss * inv_d + EPS)
                rs.append(r)
                _put(qn_ref, 2 * j + a, nope * r * qg_ref[...])
            pe = pe * jnp.where(lo, rs[0], rs[1]) * qgp_ref[...]
            if lat:
                pe = _rope(pe, cos_ref[...], sp_ref[...], sn_ref[...], MLA_ROPE // 4)
            _put(qp_ref, j, pe)

        _matmul_units(cq, w_uq_ref, MLA_HEADS // 2, 3 * LANES, emit)

    _by_tile_kind(i, body)


def _proj_mla(h, w_in, qa, kva, w_uq, qg, qgp, tables):
    n_nope = MLA_HEADS * MLA_NOPE
    n_pe = MLA_HEADS * MLA_ROPE
    return pl.pallas_call(
        _proj_mla_kernel,
        grid=(N_PROJ_TILES,),
        in_specs=[_tok_spec(D_MODEL),
                  _const_spec(w_in.shape), _const_spec((1, MLA_Q_RANK)), _const_spec((1, MLA_KV_RANK)),
                  _const_spec(w_uq.shape), _const_spec((1, LANES)), _const_spec((1, LANES)),
                  _LANE_MAT_SPEC, _LANE_MAT_SPEC, _LANE_MAT_SPEC,
                  _ROPE_SPEC, _ROPE_SPEC, _ROPE_SPEC],
        out_specs=[_tok_spec(n_nope), _tok_spec(n_pe), _tok_spec(MLA_KV_RANK), _tok_spec(LANES),
                   _cache_spec(SEQ, MLA_KV_RANK), _cache_spec(SEQ, MLA_ROPE)],
        out_shape=[jax.ShapeDtypeStruct((N_TOK, n_nope), BF16),
                   jax.ShapeDtypeStruct((N_TOK, n_pe), BF16),
                   jax.ShapeDtypeStruct((N_TOK, MLA_KV_RANK), BF16),
                   jax.ShapeDtypeStruct((N_TOK, LANES), F32),
                   _cache_shape(SEQ, MLA_KV_RANK), _cache_shape(SEQ, MLA_ROPE)],
        compiler_params=_cparams(1),
        name="proj_mla",
    )(h, w_in, qa, kva, w_uq, qg, qgp, *_mla_lane_matrices(), *tables)


def _mla_expand_kernel(ckv_ref, kpe_ref, w_ref, kg_ref, kgp_ref, sum_ref, lo_ref,
                       cos_ref, sp_ref, sn_ref, kn_ref, kp_ref, v_ref, *, rope):
    i = pl.program_id(0)
    ckv = ckv_ref[...].astype(BF16)
    kpe = kpe_ref[...]
    pe_ss = _dot(_sq_bf16(kpe), lo_ref[...])
    pe_ss = jnp.concatenate([pe_ss, pe_ss], axis=1)
    lo = _lane_lo(kpe.shape)
    inv_d = 1.0 / (MLA_NOPE + MLA_ROPE)

    def body(lat):
        def emit(j, y):
            kn = jnp.concatenate([y[:, :LANES], y[:, 2 * LANES:3 * LANES]], axis=1)
            r = lax.rsqrt((_dot(_sq_bf16(kn), sum_ref[...]) + pe_ss) * inv_d + EPS)
            kn = kn * r * kg_ref[...]
            for a in range(2):
                _put(kn_ref, 2 * j + a, kn[:, a * LANES:(a + 1) * LANES])
                _put(v_ref, 2 * j + a, y[:, (2 * a + 1) * LANES:(2 * a + 2) * LANES])
            pe = kpe * jnp.where(lo, r[:, :LANES], r[:, LANES:]) * kgp_ref[...]
            if lat:
                pe = _rope(pe, cos_ref[...], sp_ref[...], sn_ref[...], MLA_ROPE // 4)
            _put(kp_ref, j, pe)

        _matmul_units(ckv, w_ref, MLA_HEADS // 2, 4 * LANES, emit)

    if rope:
        _by_tile_kind(i, body)
    else:
        body(False)


def _mla_expand(ckv, kpe_dup, w_ukv, kg, kgp, tables, rope):
    n = ckv.shape[0]
    n_nope = MLA_HEADS * MLA_NOPE
    n_pe = MLA_HEADS * MLA_ROPE
    _, m_lo, _ = _mla_lane_matrices()
    return pl.pallas_call(
        functools.partial(_mla_expand_kernel, rope=rope),
        grid=(n // PROJ_TM,),
        in_specs=[_tok_spec(MLA_KV_RANK), _tok_spec(LANES), _const_spec(w_ukv.shape),
                  _const_spec((1, PROJ_UNIT)), _const_spec((1, LANES)), _UNIT_MAT_SPEC, _LANE_MAT_SPEC,
                  _ROPE_SPEC, _ROPE_SPEC, _ROPE_SPEC],
        out_specs=[_tok_spec(n_nope), _tok_spec(n_pe), _tok_spec(n_nope)],
        out_shape=[jax.ShapeDtypeStruct((n, n_nope), BF16),
                   jax.ShapeDtypeStruct((n, n_pe), BF16),
                   jax.ShapeDtypeStruct((n, n_nope), BF16)],
        compiler_params=_cparams(1),
        name="mla_expand",
    )(ckv, kpe_dup, w_ukv, kg, kgp, _group_sum_matrix(), m_lo, *tables)


def _prompt_spec(width):
    return pl.BlockSpec((PROMPT_SEQS * TM, width), lambda b: (b, 0))


def _latq_spec(rows, width):
    per = DEC_SEQ // rows
    return pl.BlockSpec((rows, width), lambda b, t: (N_PROMPT_TOK // rows + b * per + t, 0))


def _latkv_spec(width):
    return pl.BlockSpec((DEC_SEQ, width), lambda b, t: (LAT_BLOCK0 + b, 0))


def _lato_spec(rows):
    per = DEC_SEQ // rows
    return pl.BlockSpec((rows, D_MODEL), lambda b, t: (b * per + t, 0))


def _att_kernel(*refs, with_ctx, seqs):
    if with_ctx:
        q_ref, k_ref, v_ref, kc_ref, vc_ref, o_ref = refs
    else:
        q_ref, k_ref, v_ref, o_ref = refs
    tq = q_ref.shape[1] // seqs
    tk = k_ref.shape[0] // seqs
    nu = ATT_UNIT_HEADS
    per_kv = ATT_HEADS // ATT_KV_HEADS // nu

    def scores(i):
        b, u = divmod(i, ATT_HEADS // nu)
        q = jnp.concatenate([q_ref[(u * nu + g) * LANES:(u * nu + g + 1) * LANES, b * tq:(b + 1) * tq]
                             for g in range(nu)], axis=1)
        kv = u // per_kv
        s_list = [_dot(k_ref[b * tk:(b + 1) * tk, kv * LANES:(kv + 1) * LANES], q)]
        if with_ctx:
            s_list.append(_dot(kc_ref[kv].astype(BF16), q))
        return s_list

    def finish(i, s_list):
        b, u = divmod(i, ATT_HEADS // nu)
        kv = u // per_kv
        ps, inv = _softmax2_parts(s_list)
        o = _dot(v_ref[kv * LANES:(kv + 1) * LANES, b * tk:(b + 1) * tk], ps[0])
        if with_ctx:
            o = o + _dot(vc_ref[kv].T.astype(BF16), ps[1])
        o = o * inv
        for g in range(nu):
            o_ref[b * tq:(b + 1) * tq, (u * nu + g) * LANES:(u * nu + g + 1) * LANES] = (
                o[:, g * tq:(g + 1) * tq].T.astype(o_ref.dtype))

    _head_pipeline(seqs * (ATT_HEADS // nu), scores, finish)


def _att_attend(q_t, k, v_t, cache_k, cache_v):
    nk = ATT_KV_HEADS * ATT_HEAD_DIM
    rows = PROMPT_SEQS * TM
    out_p = pl.pallas_call(
        functools.partial(_att_kernel, with_ctx=False, seqs=PROMPT_SEQS),
        grid=(N_PROMPT_TILES // PROMPT_SEQS,),
        in_specs=[pl.BlockSpec((D_MODEL, rows), lambda b: (0, b)), _prompt_spec(nk),
                  pl.BlockSpec((nk, rows), lambda b: (0, b))],
        out_specs=_prompt_spec(D_MODEL),
        out_shape=jax.ShapeDtypeStruct((N_PROMPT_TOK, D_MODEL), BF16),
        compiler_params=_cparams(1),
        name="att_prompt",
    )(q_t, k, v_t)
    ctx = pl.BlockSpec((None, None, ATT_KV_HEADS, PAST_LEN, LANES), lambda b, t: (b, 0, 0, 0, 0))
    out_s = pl.pallas_call(
        functools.partial(_att_kernel, with_ctx=True, seqs=1),
        grid=(DEC_BATCH, TILES_PER_DEC),
        in_specs=[pl.BlockSpec((D_MODEL, TM), lambda b, t: (0, N_PROMPT_TILES + b * TILES_PER_DEC + t)),
                  _latkv_spec(nk),
                  pl.BlockSpec((nk, DEC_SEQ), lambda b, t: (0, LAT_BLOCK0 + b)), ctx, ctx],
        out_specs=_lato_spec(TM),
        out_shape=jax.ShapeDtypeStruct((N_LAT_TOK, D_MODEL), BF16),
        compiler_params=_cparams(2),
        name="att_latent",
    )(q_t, k, v_t, cache_k, cache_v)
    return out_p, out_s


def _diff_kernel(*refs, lam_init, with_ctx, seqs):
    if with_ctx:
        (q_ref, k_ref, v_ref, kc_ref, vc_ref, lq1_ref, lk1_ref, lq2_ref, lk2_ref, sub_ref, o_ref) = refs
    else:
        (q_ref, k_ref, v_ref, lq1_ref, lk1_ref, lq2_ref, lk2_ref, sub_ref, o_ref) = refs
    tq = q_ref.shape[0] // seqs
    lam = (jnp.exp(jnp.sum(lq1_ref[...] * lk1_ref[...], axis=-1, keepdims=True))
           - jnp.exp(jnp.sum(lq2_ref[...] * lk2_ref[...], axis=-1, keepdims=True)) + lam_init)
    sub = sub_ref[...] * (1.0 - lam_init)

    def make(views):
        q_v, k_v, v_v, o_v = views

        def scores(hd):
            q = jnp.concatenate(_split_halves(_chunk(q_v, hd)), axis=0)
            s_list = [_dot_nt(_chunk(k_v, hd), q)]
            if with_ctx:
                s_list.append(_dot_nt(kc_ref[hd].astype(BF16), q))
            return s_list

        def finish(hd, s_list):
            values = [_chunk(v_v, hd)]
            if with_ctx:
                values.append(vc_ref[hd].astype(BF16))
            ps, inv = _softmax2_parts(s_list)
            o = (_pv([p[:, :tq] for p in ps], values) * inv[:, :tq]
                 - _pv([p[:, tq:] for p in ps], values) * (lam * inv[:, tq:]))
            o = o * lax.rsqrt(jnp.mean(o * o, axis=0, keepdims=True) + EPS) * sub
            o_v[:, hd * LANES:(hd + 1) * LANES] = o.T.astype(o_v.dtype)

        return scores, finish

    _seq_pipeline((q_ref, k_ref, v_ref, o_ref), seqs, DIFF_HEADS, make)


def _diff_attend(q, k, v, cache_k_pair, cache_v, lq1, lk1, lq2, lk2, subln, lam_init):
    small = [lq1, lk1, lq2, lk2, subln]
    small_specs = [_const_spec(s.shape) for s in small]
    out_p = pl.pallas_call(
        functools.partial(_diff_kernel, lam_init=lam_init, with_ctx=False, seqs=PROMPT_SEQS),
        grid=(N_PROMPT_TILES // PROMPT_SEQS,),
        in_specs=[_prompt_spec(D_MODEL)] * 3 + small_specs,
        out_specs=_prompt_spec(D_MODEL),
        out_shape=jax.ShapeDtypeStruct((N_PROMPT_TOK, D_MODEL), BF16),
        compiler_params=_cparams(1),
        name="diff_prompt",
    )(q, k, v, *small)
    out_s = pl.pallas_call(
        functools.partial(_diff_kernel, lam_init=lam_init, with_ctx=True, seqs=1),
        grid=(DEC_BATCH, TILES_PER_DEC),
        in_specs=[_latq_spec(TM, D_MODEL), _latkv_spec(D_MODEL), _latkv_spec(D_MODEL),
                  pl.BlockSpec((None, DIFF_HEADS, PAST_LEN, LANES), lambda b, t: (b, 0, 0, 0)),
                  pl.BlockSpec((None, None, DIFF_HEADS, PAST_LEN, LANES), lambda b, t: (b, 0, 0, 0, 0))]
                 + small_specs,
        out_specs=_lato_spec(TM),
        out_shape=jax.ShapeDtypeStruct((N_LAT_TOK, D_MODEL), BF16),
        compiler_params=_cparams(2),
        name="diff_latent",
    )(q, k, v, cache_k_pair, cache_v, *small)
    return out_p, out_s


def _swa_pipeline(q_ref, o_ref, seq_refs, sink_ref, score_fns, value_fns, seqs=1):
    tq = q_ref.shape[0] // seqs
    per_kv = SWA_HEADS // SWA_KV_HEADS // 2
    first = lax.broadcasted_iota(jnp.int32, (LANES, tq), 0) < HALF

    def make(views):
        q_v, o_v = views[:2]
        kv_views = views[2:]

        def scores(c):
            q = jnp.concatenate(_split_halves(_chunk(q_v, c)), axis=0)
            return [fn(kv_views, c // per_kv, q) for fn in score_fns]

        def finish(c, s_list):
            sink = jnp.concatenate([jnp.full((1, tq), sink_ref[2 * c + a] * LOG2E, F32) for a in range(2)],
                                   axis=1)
            ps, inv = _softmax2_parts(s_list, extra=sink)
            o = _pv(ps, [fn(kv_views, c // per_kv) for fn in value_fns]) * inv
            oc = jnp.where(first, o[:, :tq], o[:, tq:])
            o_v[:, c * LANES:(c + 1) * LANES] = oc.T.astype(o_v.dtype)

        return scores, finish

    _seq_pipeline((q_ref, o_ref) + tuple(seq_refs), seqs, SWA_HEADS // 2, make)


def _swa_prompt_kernel(sink_ref, q_ref, k_ref, v_ref, o_ref):
    _swa_pipeline(q_ref, o_ref, (k_ref, v_ref), sink_ref,
                  [lambda kv_v, kv, q: _dot_nt(_chunk(kv_v[0], kv), q)],
                  [lambda kv_v, kv: _chunk(kv_v[1], kv)], seqs=PROMPT_SEQS)


def _swa_latent_kernel(sink_ref, q_ref, k_ref, v_ref, kc_ref, vc_ref, o_ref):
    n = pl.program_id(1)
    tq = q_ref.shape[0]
    span = 3 * SWA_QB
    start = pl.multiple_of(jnp.clip((n - 1) * SWA_QB, 0, DEC_SEQ - span), SWA_QB)
    cols = lax.broadcasted_iota(jnp.int32, (span, 2 * tq), 1)
    qpos = n * SWA_QB + jnp.bitwise_and(cols, tq - 1)
    kpos = start + lax.broadcasted_iota(jnp.int32, (span, 2 * tq), 0)
    valid = jnp.abs(qpos - kpos) <= WINDOW

    def local(ref, kv):
        return ref[pl.ds(start, span), kv * LANES:(kv + 1) * LANES]

    _swa_pipeline(q_ref, o_ref, (), sink_ref,
                  [lambda _, kv, q: jnp.where(valid, _dot_nt(local(k_ref, kv), q), -1e30),
                   lambda _, kv, q: _dot_nt(kc_ref[kv], q)],
                  [lambda _, kv: local(v_ref, kv), lambda _, kv: vc_ref[kv]])


def _swa_attend(q, kd, vd, cache_kd, cache_vd, sink):
    nkd = 2 * SWA_KV_HEADS * SWA_HEAD_DIM
    smem = pl.BlockSpec(memory_space=pltpu.SMEM)
    out_p = pl.pallas_call(
        _swa_prompt_kernel,
        grid=(N_PROMPT_TILES // PROMPT_SEQS,),
        in_specs=[smem, _prompt_spec(D_MODEL), _prompt_spec(nkd), _prompt_spec(nkd)],
        out_specs=_prompt_spec(D_MODEL),
        out_shape=jax.ShapeDtypeStruct((N_PROMPT_TOK, D_MODEL), BF16),
        compiler_params=_cparams(1),
        name="swa_prompt",
    )(sink, q, kd, vd)
    ctx = pl.BlockSpec((None, SWA_KV_HEADS, PAST_LEN, LANES), lambda b, n: (b, 0, 0, 0))
    out_s = pl.pallas_call(
        _swa_latent_kernel,
        grid=(DEC_BATCH, DEC_SEQ // SWA_QB),
        in_specs=[smem, _latq_spec(SWA_QB, D_MODEL), _latkv_spec(nkd), _latkv_spec(nkd), ctx, ctx],
        out_specs=_lato_spec(SWA_QB),
        out_shape=jax.ShapeDtypeStruct((N_LAT_TOK, D_MODEL), BF16),
        compiler_params=_cparams(2),
        name="swa_latent",
    )(sink, q, kd, vd, cache_kd, cache_vd)
    return out_p, out_s


def _mla_kernel(*refs, with_ctx, seqs):
    if with_ctx:
        (qn_ref, qp_ref, kn_ref, kp_ref, v_ref, knc_ref, kpc_ref, vc_ref, o_ref) = refs
    else:
        (qn_ref, qp_ref, kn_ref, kp_ref, v_ref, o_ref) = refs

    def make(views):
        qn_v, qp_v, kn_v, kp_v, v_v, o_v = views

        def scores(hd):
            j, a = hd // 2, hd % 2
            q = jnp.concatenate([_chunk(qn_v, hd), _split_halves(_chunk(qp_v, j))[a]], axis=1)
            s_list = [_dot_nt(jnp.concatenate([_chunk(kn_v, hd), _chunk(kp_v, j)], axis=1), q)]
            if with_ctx:
                s_list.append(_dot_nt(jnp.concatenate([_chunk(knc_ref, hd), _chunk(kpc_ref, j)], axis=1), q))
            return s_list

        def finish(hd, s_list):
            values = [_chunk(v_v, hd)]
            if with_ctx:
                values.append(_chunk(vc_ref, hd))
            ps, inv = _softmax2_parts(s_list)
            o_v[:, hd * LANES:(hd + 1) * LANES] = (_pv(ps, values) * inv).T.astype(o_v.dtype)

        return scores, finish

    _seq_pipeline((qn_ref, qp_ref, kn_ref, kp_ref, v_ref, o_ref), seqs, MLA_HEADS, make)


def _mla_attend(qn, qp, kn, kp, v, knc, kpc, vc):
    n_pe = MLA_HEADS * MLA_ROPE
    out_p = pl.pallas_call(
        functools.partial(_mla_kernel, with_ctx=False, seqs=PROMPT_SEQS),
        grid=(N_PROMPT_TILES // PROMPT_SEQS,),
        in_specs=[_prompt_spec(D_MODEL), _prompt_spec(n_pe), _prompt_spec(D_MODEL), _prompt_spec(n_pe),
                  _prompt_spec(D_MODEL)],
        out_specs=_prompt_spec(D_MODEL),
        out_shape=jax.ShapeDtypeStruct((N_PROMPT_TOK, D_MODEL), BF16),
        compiler_params=_cparams(1),
        name="mla_prompt",
    )(qn, qp, kn, kp, v)

    def ctx(width):
        return pl.BlockSpec((PAST_LEN, width), lambda b, t: (b, 0))

    out_s = pl.pallas_call(
        functools.partial(_mla_kernel, with_ctx=True, seqs=1),
        grid=(DEC_BATCH, TILES_PER_DEC),
        in_specs=[_latq_spec(TM, D_MODEL), _latq_spec(TM, n_pe),
                  _latkv_spec(D_MODEL), _latkv_spec(n_pe), _latkv_spec(D_MODEL),
                  ctx(D_MODEL), ctx(n_pe), ctx(D_MODEL)],
        out_specs=_lato_spec(TM),
        out_shape=jax.ShapeDtypeStruct((N_LAT_TOK, D_MODEL), BF16),
        compiler_params=_cparams(2),
        name="mla_latent",
    )(qn, qp, kn, kp, v, knc, kpc, vc)
    return out_p, out_s


def _omlp_kernel(*refs, first, last):
    refs = list(refs)
    ap_ref, as_ref, wo_ref = refs[:3]
    x_refs = refs[3:5] if first else refs[3:4]
    refs = refs[3 + len(x_refs):]
    g1_ref, gain_ref, sh_ref, sc_ref, g2_ref, w1c_ref, w2c_ref = refs[:7]
    refs = refs[7:]
    if last:
        op_ref, os_ref, wo_s, w1_s, w2_s = refs
    else:
        ngain_ref, nsh_ref, nsc_ref, o_ref, hn_ref, wo_s, w1_s, w2_s = refs
    s = pl.program_id(0)
    per = MLP_FF_CHUNK // MLP_LOAD_COLS
    n_chunks = D_FF // MLP_FF_CHUNK
    half = MLP_TM // 2

    @pl.when(s == 0)
    def _():
        wo_s[...] = wo_ref[...].astype(BF16)

    for part in range(per):
        @pl.when((s < N_LOAD_STEPS) & (s % per == part))
        def _(part=part):
            w1_s[s // per, :, part * MLP_LOAD_COLS:(part + 1) * MLP_LOAD_COLS] = w1c_ref[...].astype(BF16)

    @pl.when(s < N_LOAD_STEPS)
    def _():
        w2_s[s // per, pl.ds(pl.multiple_of((s % per) * MLP_LOAD_COLS, MLP_LOAD_COLS), MLP_LOAD_COLS), :] = (
            w2c_ref[...].astype(BF16))

    @pl.when(s >= N_LOAD_STEPS)
    def _():
        t = s - N_LOAD_STEPS
        is_prompt = t < N_MLP_PROMPT_TILES
        grp = _tile_group(t, MLP_TM)

        def mod(ref):
            return ref[pl.ds(grp, 1), :]

        rows = [slice(r * half, (r + 1) * half) for r in range(2)]
        o = [_dot(jnp.where(is_prompt, ap_ref[rw, :], as_ref[rw, :]), wo_s[...]) for rw in rows]
        x1, h, u0 = [], [], []
        for r, rw in enumerate(rows):
            x = jnp.where(is_prompt, x_refs[0][rw, :], x_refs[1][rw, :]) if first else x_refs[0][rw, :]
            x1.append(x + mod(g1_ref) * o[r])
            h.append(_norm_mod(x1[r], gain_ref[...], mod(sh_ref), mod(sc_ref)).astype(BF16))
            u0.append(_dot(h[r], w1_s[0]))
        h = jnp.concatenate(h, axis=0)
        acc = []

        def up(c):
            return jnp.concatenate(u0, axis=0) if c == 0 else _dot(h, w1_s[c])

        def down(c, u):
            u = jnp.square(jnp.maximum(u, 0.0)).astype(BF16)
            if c + 1 < n_chunks:
                y = _dot(u, w2_s[c])
                acc[:] = [y if not acc else acc[0] + y]
            else:
                acc[:] = [acc[0][rw] + _dot(u[rw], w2_s[c]) for rw in rows]

        _head_pipeline(n_chunks, up, down)
        for r, rw in enumerate(rows):
            out = x1[r] + mod(g2_ref) * acc[r]
            if last:
                @pl.when(is_prompt)
                def _(out=out, rw=rw):
                    op_ref[rw, :] = out

                @pl.when(jnp.logical_not(is_prompt))
                def _(out=out, rw=rw):
                    os_ref[rw, :] = out
            else:
                o_ref[rw, :] = out
                hn_ref[rw, :] = _norm_mod(out, ngain_ref[...], mod(nsh_ref), mod(nsc_ref)).astype(BF16)


def _omlp(attn_p, attn_s, w_o, x, mods, gain_ffn, w1_all, w2_all, layer, next_gain, next_mods):
    first, last = layer == 0, next_gain is None
    n_lat_tiles = N_LAT_TOK // MLP_TM

    def tok(s):
        return jnp.maximum(s - N_LOAD_STEPS, 0)

    p_spec = pl.BlockSpec((MLP_TM, D_MODEL), lambda s: (jnp.minimum(tok(s), N_MLP_PROMPT_TILES - 1), 0))
    l_spec = pl.BlockSpec((MLP_TM, D_MODEL),
                          lambda s: (jnp.clip(tok(s) - N_MLP_PROMPT_TILES, 0, n_lat_tiles - 1), 0))
    w1_spec = pl.BlockSpec((None, D_MODEL, MLP_LOAD_COLS),
                           lambda s: (layer, 0, jnp.minimum(s, N_LOAD_STEPS - 1)))
    w2_spec = pl.BlockSpec((None, MLP_LOAD_COLS, D_MODEL),
                           lambda s: (layer, jnp.minimum(s, N_LOAD_STEPS - 1), 0))
    t_spec = pl.BlockSpec((MLP_TM, D_MODEL), lambda s: (tok(s), 0))
    n_chunks = D_FF // MLP_FF_CHUNK
    split = ([p_spec, l_spec], [jax.ShapeDtypeStruct((N_PROMPT_TOK, D_MODEL), F32),
                                jax.ShapeDtypeStruct((N_LAT_TOK, D_MODEL), F32)])
    in_specs = ([p_spec, l_spec, _const_spec(w_o.shape)] + (split[0] if first else [t_spec])
                + [_mod_spec(2), _const_spec((1, D_MODEL)), _mod_spec(3), _mod_spec(4), _mod_spec(5),
                   w1_spec, w2_spec])
    args = ([attn_p, attn_s, w_o] + (list(x) if first else [x])
            + [mods, gain_ffn, mods, mods, mods, w1_all, w2_all])
    if last:
        out_specs, out_shape = split
    else:
        in_specs += [_const_spec((1, D_MODEL)), _mod_spec(0), _mod_spec(1)]
        args += [next_gain, next_mods, next_mods]
        out_specs = [t_spec, t_spec]
        out_shape = [jax.ShapeDtypeStruct((N_TOK, D_MODEL), F32), jax.ShapeDtypeStruct((N_TOK, D_MODEL), BF16)]
    return pl.pallas_call(
        functools.partial(_omlp_kernel, first=first, last=last),
        grid=(N_LOAD_STEPS + N_TOK // MLP_TM,),
        in_specs=in_specs,
        out_specs=out_specs,
        out_shape=out_shape,
        scratch_shapes=[pltpu.VMEM((D_MODEL, D_MODEL), BF16),
                        pltpu.VMEM((n_chunks, D_MODEL, MLP_FF_CHUNK), BF16),
                        pltpu.VMEM((n_chunks, MLP_FF_CHUNK, D_MODEL), BF16)],
        compiler_params=_cparams(1),
        name="omlp",
    )(*args)


def _row(v, scale=1.0):
    return (v.astype(F32) * scale).reshape(1, -1)


def _pair(v, scale=1.0):
    return (jnp.concatenate([v, v]).astype(F32) * scale).reshape(1, LANES)


def _unit_gain(v, scale=1.0):
    return (jnp.tile(v.astype(F32), PROJ_UNIT // v.shape[0]) * scale).reshape(1, PROJ_UNIT)


def kernel(x_prompt, x_sample, cache_att_k, cache_att_v, cache_diff_k, cache_diff_v, cache_swa_k, cache_swa_v, cache_mla_ckv, cache_mla_kpe, c, c_ctx, ada_w, ada_b, norm_mix, norm_ffn, att_w_qkv, att_q_norm, att_k_norm, att_w_o, diff_w_qkv, diff_q_norm, diff_k_norm, diff_lq1, diff_lk1, diff_lq2, diff_lk2, diff_subln, diff_w_o, swa_w_qkv, swa_q_norm, swa_k_norm, swa_sink, swa_w_o, mla_w_in, mla_q_a_norm, mla_kv_a_norm, mla_w_uq, mla_w_ukv, mla_q_norm, mla_k_norm, mla_w_o, mlp_w1, mlp_w2):
    xp = x_prompt.reshape(N_PROMPT_TOK, D_MODEL)
    xs = x_sample.reshape(N_LAT_TOK, D_MODEL)
    cond = jnp.concatenate([c_ctx[None], c, jnp.zeros((COND_ROWS - 1 - DEC_BATCH, D_MODEL), F32)], axis=0)
    mods_all = _modulation(cond, ada_w, ada_b)

    tab_att = _rope_tables(ATT_HEAD_DIM)
    tab_64 = _rope_tables(DIFF_HEAD_DIM)

    outs = {}
    x = (xp, xs)
    for layer in range(DEPTH):
        mods = mods_all[layer]
        gain_ffn = _row(norm_ffn[layer])
        if layer == 0:
            qs = ATT_HEAD_DIM ** -0.5 * LOG2E
            q, k, v, outs["att_k"], outs["att_v"] = _proj_att(
                xp, xs, mods, _row(norm_mix[layer]), att_w_qkv[0].astype(BF16),
                _unit_gain(att_q_norm[0], qs), _unit_gain(att_k_norm[0]), tab_att)
            attn_p, attn_s = _att_attend(q, k, v, cache_att_k, cache_att_v)
            w_o = att_w_o[0]
        elif layer == 1:
            qs = DIFF_HEAD_DIM ** -0.5 * LOG2E
            q, k, v, outs["diff_k"], outs["diff_v"] = _proj_diff(
                h, diff_w_qkv[0].astype(BF16),
                _unit_gain(diff_q_norm[0], qs), _unit_gain(diff_k_norm[0]), tab_64)
            lam_init = 0.8 - 0.6 * math.exp(-0.3 * layer)
            ck = cache_diff_k[:, 0].transpose(0, 1, 3, 2, 4).reshape(
                DEC_BATCH, DIFF_HEADS, PAST_LEN, LANES)
            attn_p, attn_s = _diff_attend(q, k, v, ck, cache_diff_v,
                                          _row(diff_lq1[0]), _row(diff_lk1[0]),
                                          _row(diff_lq2[0]), _row(diff_lk2[0]),
                                          diff_subln[0].astype(F32).reshape(LANES, 1), lam_init)
            w_o = diff_w_o[0]
        elif layer == 2:
            qs = SWA_HEAD_DIM ** -0.5 * LOG2E
            q, kd, vd, outs["swa_k"], outs["swa_v"] = _proj_swa(
                h, swa_w_qkv[0].astype(BF16),
                _unit_gain(swa_q_norm[0], qs), _unit_gain(swa_k_norm[0]), tab_64)
            ckd = jnp.concatenate([cache_swa_k[:, 0]] * 2, axis=-1).astype(BF16)
            cvd = jnp.concatenate([cache_swa_v[:, 0]] * 2, axis=-1).astype(BF16)
            attn_p, attn_s = _swa_attend(q, kd, vd, ckd, cvd, swa_sink[0].astype(F32))
            w_o = swa_w_o[0]
        else:
            qs = (MLA_NOPE + MLA_ROPE) ** -0.5 * LOG2E
            w_in = mla_w_in[0]
            w_in = jnp.concatenate([w_in, w_in[:, -MLA_ROPE:]], axis=1).astype(BF16)
            w_uq = mla_w_uq[0].reshape(MLA_Q_RANK, MLA_HEADS // 2, 2, MLA_NOPE + MLA_ROPE)
            w_uq = jnp.concatenate([w_uq[..., :MLA_NOPE].reshape(MLA_Q_RANK, MLA_HEADS // 2, 2 * MLA_NOPE),
                                    w_uq[..., MLA_NOPE:].reshape(MLA_Q_RANK, MLA_HEADS // 2, 2 * MLA_ROPE)],
                                   axis=-1).reshape(MLA_Q_RANK, -1).astype(BF16)
            w_ukv = mla_w_ukv[0].astype(BF16)
            qg, kg = mla_q_norm[0], mla_k_norm[0]
            qn, qp, ckv, kpe, outs["mla_ckv"], outs["mla_kpe"] = _proj_mla(
                h, w_in, _row(mla_q_a_norm[0]), _row(mla_kv_a_norm[0]), w_uq,
                _row(qg[:MLA_NOPE], qs), _pair(qg[MLA_NOPE:], qs), tab_64)
            kn, kp, vv = _mla_expand(ckv, kpe, w_ukv, _unit_gain(kg[:MLA_NOPE]), _pair(kg[MLA_NOPE:]),
                                     tab_64, True)
            c_ckv = cache_mla_ckv[:, 0].reshape(DEC_BATCH * PAST_LEN, MLA_KV_RANK)
            c_kpe = cache_mla_kpe[:, 0].reshape(DEC_BATCH * PAST_LEN, MLA_ROPE)
            c_kpe = jnp.concatenate([c_kpe, c_kpe], axis=-1)
            knc, kpc, vc = _mla_expand(c_ckv, c_kpe, w_ukv, _unit_gain(kg[:MLA_NOPE]), _pair(kg[MLA_NOPE:]),
                                       tab_64, False)
            attn_p, attn_s = _mla_attend(qn, qp, kn, kp, vv, knc, kpc, vc)
            w_o = mla_w_o[0]
        if layer + 1 < DEPTH:
            x, h = _omlp(attn_p, attn_s, w_o, x, mods, gain_ffn, mlp_w1, mlp_w2, layer,
                         _row(norm_mix[layer + 1]), mods_all[layer + 1])
        else:
            xp, xs = _omlp(attn_p, attn_s, w_o, x, mods, gain_ffn, mlp_w1, mlp_w2, layer, None, None)

    y_prompt = xp.reshape(BATCH, SEQ, D_MODEL)
    y_sample = xs.reshape(DEC_BATCH, DEC_SEQ, D_MODEL)
    return (y_prompt, y_sample, outs["att_k"], outs["att_v"], outs["diff_k"], outs["diff_v"],
            outs["swa_k"], outs["swa_v"], outs["mla_ckv"], outs["mla_kpe"])
```

```python
import functools
import math

import numpy as np
import jax
import jax.numpy as jnp
from jax import lax
from jax.experimental import pallas as pl
from jax.experimental.pallas import tpu as pltpu

D_MODEL = 1024
BATCH = 16
SEQ = 256
DEPTH = 4
DEC_BATCH = 2
DEC_SEQ = 1024
PAST_LEN = 256
GRID_W = 64
ROPE_THETA = 10000.0
EPS = 1e-6
D_FF = 4 * D_MODEL
MOD_CHUNKS = 6
LOG2E = 1.4426950408889634

ATT_HEADS, ATT_KV_HEADS, ATT_HEAD_DIM = 8, 2, 128
DIFF_HEADS, DIFF_HEAD_DIM = 8, 64
SWA_HEADS, SWA_KV_HEADS, SWA_HEAD_DIM, WINDOW = 16, 4, 64, 128
MLA_HEADS, MLA_NOPE, MLA_ROPE, MLA_VDIM = 8, 128, 64, 128
MLA_Q_RANK, MLA_KV_RANK = 512, 256

LANES = 128
HALF = LANES // 2
TM = 256
N_PROMPT_TOK = BATCH * SEQ
N_LAT_TOK = DEC_BATCH * DEC_SEQ
N_TOK = N_PROMPT_TOK + N_LAT_TOK
N_PROMPT_TILES = N_PROMPT_TOK // TM
TILES_PER_DEC = DEC_SEQ // TM
LAT_BLOCK0 = N_PROMPT_TOK // DEC_SEQ
COND_ROWS = 8
PROJ_TM = 512
PROJ_BATCHES = PROJ_TM // SEQ
N_PROJ_TILES = N_TOK // PROJ_TM
N_PROJ_PROMPT = N_PROMPT_TOK // PROJ_TM
PROJ_UNIT = 2 * LANES
MLP_TM = 512
MLP_FF_CHUNK = 512
MLP_LOAD_COLS = 256
N_LOAD_STEPS = D_FF // MLP_LOAD_COLS
N_MLP_PROMPT_TILES = N_PROMPT_TOK // MLP_TM
SWA_QB = 128
ATT_UNIT_HEADS = 4
PROMPT_SEQS = 4
VMEM_LIMIT = 56 * 1024 * 1024

F32 = jnp.float32
BF16 = jnp.bfloat16


def _cparams(n_axes):
    return pltpu.CompilerParams(dimension_semantics=("arbitrary",) * n_axes,
                                vmem_limit_bytes=VMEM_LIMIT)


def _dot(a, b):
    return jnp.dot(a, b, preferred_element_type=F32)


def _dot_nt(a, b):
    return lax.dot_general(a, b, (((1,), (1,)), ((), ())), preferred_element_type=F32)


def _dot_tn(a, b):
    return lax.dot_general(a, b, (((0,), (0,)), ((), ())), preferred_element_type=F32)


def _const_spec(shape):
    nd = len(shape)
    return pl.BlockSpec(shape, lambda *_: (0,) * nd, pipeline_mode=pl.Buffered(1))


def _chunk(ref, c, width=LANES):
    return ref[:, c * width:(c + 1) * width]


def _put(ref, c, val):
    ref[:, c * LANES:(c + 1) * LANES] = val.astype(ref.dtype)


def _tile_group(i, rows):
    n_prompt = N_PROMPT_TOK // rows
    return jnp.where(i < n_prompt, 0, 1 + (i - n_prompt) // (DEC_SEQ // rows))


def _rope_tile(i):
    return jnp.maximum(i - N_PROJ_PROMPT, 0) % (DEC_SEQ // PROJ_TM)


def _norm_mod(x, gain, shift, scale):
    ms = jnp.mean(x * x, axis=-1, keepdims=True)
    return x * lax.rsqrt(ms + EPS) * (gain * (1.0 + scale)) + shift


def _lane_lo(shape):
    return lax.broadcasted_iota(jnp.int32, shape, len(shape) - 1) < HALF


def _rope(y, cos, sin_prev, sin_next, quarter):
    return (y * cos + pltpu.roll(y, quarter, 1) * sin_prev
            + pltpu.roll(y, LANES - quarter, 1) * sin_next)


def _rope_tables(rot_dim):
    half = rot_dim // 2
    quarter = rot_dim // 4
    inv = np.float32(ROPE_THETA) ** (-np.arange(0, half, 2, dtype=np.float32) / np.float32(half))
    pos = np.arange(DEC_SEQ)
    row = (pos // GRID_W).astype(np.float32)
    col = (pos % GRID_W).astype(np.float32)
    lane = np.arange(LANES)
    dd = lane % rot_dim
    q = dd // quarter
    f = dd % quarter
    ang = np.where((q < 2)[None, :], row[:, None], col[:, None]) * inv[f][None, :]
    ang = ang.astype(np.float32)
    cos = np.cos(ang).astype(np.float32)
    sin = np.sin(ang).astype(np.float32)
    odd = (q % 2 == 1)[None, :]
    sin_prev = np.where(odd, sin, 0.0).astype(np.float32)
    sin_next = np.where(odd, 0.0, -sin).astype(np.float32)
    return jnp.asarray(cos), jnp.asarray(sin_prev), jnp.asarray(sin_next)


def _lane_sum_matrix(rows, cols, value=1.0):
    lane = np.arange(LANES)
    m = np.where(rows(lane)[:, None] & cols(lane)[None, :], value, 0.0).astype(np.float32)
    return jnp.asarray(m, dtype=BF16)


def _group_mean_matrix(group):
    lane = np.arange(PROJ_UNIT)
    m = np.where((lane[:, None] // group) == (lane[None, :] // group), 1.0 / group, 0.0)
    return jnp.asarray(m.astype(np.float32), dtype=BF16)


def _group_sum_matrix():
    lane = np.arange(PROJ_UNIT)
    m = np.where((lane[:, None] // LANES) == (lane[None, :] // LANES), 1.0, 0.0)
    return jnp.asarray(m.astype(np.float32), dtype=BF16)


def _sq_bf16(y):
    return (y * y).astype(BF16)


def _head_norm(y, m_ref, gain):
    return y * lax.rsqrt(_dot(_sq_bf16(y), m_ref[...]) + EPS) * gain


def _halves(y):
    return [y[:, t * LANES:(t + 1) * LANES] for t in range(y.shape[1] // LANES)]


def _matmul_units(h, w_ref, n_units, width, emit):
    def unit(u):
        return _dot(h, w_ref[:, u * width:(u + 1) * width])

    nxt = unit(0)
    for u in range(n_units):
        cur = nxt
        if u + 1 < n_units:
            nxt = unit(u + 1)
        emit(u, cur)


def _by_tile_kind(i, body):
    pl.when(i < N_PROJ_PROMPT)(functools.partial(body, False))
    pl.when(i >= N_PROJ_PROMPT)(functools.partial(body, True))


def _rope_args(lat, cos_ref, sp_ref, sn_ref, rot_dim):
    return (cos_ref[...], sp_ref[...], sn_ref[...], rot_dim // 4) if lat else None


def _maybe_rope(y, rope):
    return y if rope is None else _rope(y, *rope)


def _cache_rows(ref, index, val):
    for b in range(PROJ_BATCHES):
        ref[(b, 0) + tuple(index)] = val[b * SEQ:(b + 1) * SEQ]


def _softmax2_parts(s_list, extra=None):
    m = jnp.max(s_list[0], axis=0, keepdims=True)
    for s in s_list[1:]:
        m = jnp.maximum(m, jnp.max(s, axis=0, keepdims=True))
    if extra is not None:
        m = jnp.maximum(m, extra)
    ps = [jnp.exp2(s - m) for s in s_list]
    mass = ps[0].sum(axis=0, keepdims=True)
    for p in ps[1:]:
        mass = mass + p.sum(axis=0, keepdims=True)
    if extra is not None:
        mass = mass + jnp.exp2(extra - m)
    return [p.astype(BF16) for p in ps], 1.0 / mass


def _head_pipeline(n, scores, finish):
    nxt = scores(0)
    for h in range(n):
        cur = nxt
        if h + 1 < n:
            nxt = scores(h + 1)
        finish(h, cur)


def _seq_pipeline(refs, seqs, n, make):
    fns = []
    for b in range(seqs):
        views = [r.at[b * (r.shape[0] // seqs):(b + 1) * (r.shape[0] // seqs)] for r in refs]
        fns.append(make(views))
    _head_pipeline(seqs * n, lambda i: fns[i // n][0](i % n), lambda i, s: fns[i // n][1](i % n, s))


def _pv(ps, values):
    o = None
    for p, v in zip(ps, values):
        t = _dot_tn(v, p)
        o = t if o is None else o + t
    return o


def _split_halves(q):
    lo = _lane_lo(q.shape)
    zero = jnp.zeros_like(q)
    return jnp.where(lo, q, zero), jnp.where(lo, zero, q)


def _mod_kernel(cond_ref, w_ref, b_ref, o_ref):
    c = cond_ref[...]
    s = (c * jax.nn.sigmoid(c)).astype(BF16)
    o_ref[0] = _dot(s, w_ref[0].astype(BF16)) + b_ref[0]


def _modulation(cond, ada_w, ada_b):
    tn = 1536
    n = MOD_CHUNKS * D_MODEL
    return pl.pallas_call(
        _mod_kernel,
        grid=(DEPTH, n // tn),
        in_specs=[
            pl.BlockSpec((COND_ROWS, D_MODEL), lambda l, j: (0, 0)),
            pl.BlockSpec((1, D_MODEL, tn), lambda l, j: (l, 0, j)),
            pl.BlockSpec((1, 1, tn), lambda l, j: (l, 0, j)),
        ],
        out_specs=pl.BlockSpec((1, COND_ROWS, tn), lambda l, j: (l, 0, j)),
        out_shape=jax.ShapeDtypeStruct((DEPTH, COND_ROWS, n), F32),
        compiler_params=_cparams(2),
        name="modulation",
    )(cond, ada_w, ada_b.reshape(DEPTH, 1, n))


def _mod_spec(chunk):
    return pl.BlockSpec((COND_ROWS, D_MODEL), lambda i: (0, chunk))


def _mod_row(ref, i):
    return ref[pl.ds(_tile_group(i, PROJ_TM), 1), :]


_ROPE_SPEC = pl.BlockSpec((PROJ_TM, LANES), lambda i: (_rope_tile(i), 0))
_LANE_MAT_SPEC = _const_spec((LANES, LANES))
_UNIT_MAT_SPEC = _const_spec((PROJ_UNIT, PROJ_UNIT))


def _tok_spec(width):
    return pl.BlockSpec((PROJ_TM, width), lambda i: (i, 0))


_XP_SPEC = pl.BlockSpec((PROJ_TM, D_MODEL), lambda i: (jnp.minimum(i, N_PROJ_PROMPT - 1), 0))
_XS_SPEC = pl.BlockSpec((PROJ_TM, D_MODEL), lambda i: (jnp.maximum(i - N_PROJ_PROMPT, 0), 0))


def _cache_spec(*dims):
    nd = len(dims)
    return pl.BlockSpec((PROJ_BATCHES, 1) + dims,
                        lambda i: (jnp.minimum(i, N_PROJ_PROMPT - 1), 0) + (0,) * nd)


def _cache_shape(*dims):
    return jax.ShapeDtypeStruct((BATCH, 1) + dims, F32)


def _proj_att_kernel(xp_ref, xs_ref, gain_ref, sh_ref, sc_ref, w_ref, qg_ref, kg_ref, m_ref,
                     cos_ref, sp_ref, sn_ref, q_ref, k_ref, v_ref, ck_ref, cv_ref):
    i = pl.program_id(0)
    x = jnp.where(i < N_PROJ_PROMPT, xp_ref[...], xs_ref[...])
    h = _norm_mod(x, gain_ref[...], _mod_row(sh_ref, i), _mod_row(sc_ref, i)).astype(BF16)
    per = PROJ_UNIT // LANES
    nq, nk = ATT_HEADS // per, ATT_KV_HEADS // per

    def body(lat):
        rope = _rope_args(lat, cos_ref, sp_ref, sn_ref, ATT_HEAD_DIM)

        def emit(u, y):
            if u < nq + nk:
                y = _head_norm(y, m_ref, qg_ref[...] if u < nq else kg_ref[...])
            for t, yc in enumerate(_halves(y)):
                if u < nq:
                    _put(q_ref, u * per + t, _maybe_rope(yc, rope))
                elif u < nq + nk:
                    kn = _maybe_rope(yc, rope)
                    _put(k_ref, (u - nq) * per + t, kn)
                    if not lat:
                        _cache_rows(ck_ref, [(u - nq) * per + t], kn)
                else:
                    _put(v_ref, (u - nq - nk) * per + t, yc)
                    if not lat:
                        _cache_rows(cv_ref, [(u - nq - nk) * per + t], yc)

        _matmul_units(h, w_ref, nq + 2 * nk, PROJ_UNIT, emit)

    _by_tile_kind(i, body)


def _proj_att(xp, xs, mods, gain, w, qg, kg, tables):
    nq, nk = ATT_HEADS * ATT_HEAD_DIM, ATT_KV_HEADS * ATT_HEAD_DIM
    return pl.pallas_call(
        _proj_att_kernel,
        grid=(N_PROJ_TILES,),
        in_specs=[_XP_SPEC, _XS_SPEC, _const_spec((1, D_MODEL)), _mod_spec(0), _mod_spec(1),
                  _const_spec(w.shape), _const_spec((1, PROJ_UNIT)), _const_spec((1, PROJ_UNIT)),
                  _UNIT_MAT_SPEC, _ROPE_SPEC, _ROPE_SPEC, _ROPE_SPEC],
        out_specs=[_tok_spec(nq), _tok_spec(nk), _tok_spec(nk),
                   _cache_spec(ATT_KV_HEADS, SEQ, ATT_HEAD_DIM), _cache_spec(ATT_KV_HEADS, SEQ, ATT_HEAD_DIM)],
        out_shape=[jax.ShapeDtypeStruct((N_TOK, nq), BF16),
                   jax.ShapeDtypeStruct((N_TOK, nk), BF16),
                   jax.ShapeDtypeStruct((N_TOK, nk), BF16),
                   _cache_shape(ATT_KV_HEADS, SEQ, ATT_HEAD_DIM), _cache_shape(ATT_KV_HEADS, SEQ, ATT_HEAD_DIM)],
        compiler_params=_cparams(1),
        name="proj_att",
    )(xp, xs, gain, mods, mods, w, qg, kg, _group_mean_matrix(ATT_HEAD_DIM), *tables)


def _proj_diff_kernel(h_ref, w_ref, qg_ref, kg_ref, m_ref,
                      cos_ref, sp_ref, sn_ref, q_ref, k_ref, v_ref, ck_ref, cv_ref):
    i = pl.program_id(0)
    h = h_ref[...]
    per = PROJ_UNIT // LANES
    nu = DIFF_HEADS // per

    def body(lat):
        rope = _rope_args(lat, cos_ref, sp_ref, sn_ref, DIFF_HEAD_DIM)

        def emit(u, y):
            if u < 2 * nu:
                y = _head_norm(y, m_ref, qg_ref[...] if u < nu else kg_ref[...])
            for t, yc in enumerate(_halves(y)):
                hd = (u % nu) * per + t
                if u < nu:
                    _put(q_ref, hd, _maybe_rope(yc, rope))
                elif u < 2 * nu:
                    kn = _maybe_rope(yc, rope)
                    _put(k_ref, hd, kn)
                    if not lat:
                        _cache_rows(ck_ref, [hd, 0], kn[:, :HALF])
                        _cache_rows(ck_ref, [hd, 1], kn[:, HALF:])
                else:
                    _put(v_ref, hd, yc)
                    if not lat:
                        _cache_rows(cv_ref, [hd], yc)

        _matmul_units(h, w_ref, 3 * nu, PROJ_UNIT, emit)

    _by_tile_kind(i, body)


def _proj_diff(h, w, qg, kg, tables):
    n = DIFF_HEADS * 2 * DIFF_HEAD_DIM
    return pl.pallas_call(
        _proj_diff_kernel,
        grid=(N_PROJ_TILES,),
        in_specs=[_tok_spec(D_MODEL),
                  _const_spec(w.shape), _const_spec((1, PROJ_UNIT)), _const_spec((1, PROJ_UNIT)),
                  _UNIT_MAT_SPEC, _ROPE_SPEC, _ROPE_SPEC, _ROPE_SPEC],
        out_specs=[_tok_spec(n), _tok_spec(n), _tok_spec(n),
                   _cache_spec(DIFF_HEADS, 2, SEQ, DIFF_HEAD_DIM), _cache_spec(DIFF_HEADS, SEQ, 2 * DIFF_HEAD_DIM)],
        out_shape=[jax.ShapeDtypeStruct((N_TOK, n), BF16)] * 3
                  + [_cache_shape(DIFF_HEADS, 2, SEQ, DIFF_HEAD_DIM),
                     _cache_shape(DIFF_HEADS, SEQ, 2 * DIFF_HEAD_DIM)],
        compiler_params=_cparams(1),
        name="proj_diff",
    )(h, w, qg, kg, _group_mean_matrix(DIFF_HEAD_DIM), *tables)


def _dup_halves(yc):
    lo = _lane_lo(yc.shape)
    sw = pltpu.roll(yc, HALF, 1)
    return jnp.where(lo, yc, sw), jnp.where(lo, sw, yc)


def _proj_swa_kernel(h_ref, w_ref, qg_ref, kg_ref, m_ref,
                     cos_ref, sp_ref, sn_ref, q_ref, kd_ref, vd_ref, ck_ref, cv_ref):
    i = pl.program_id(0)
    h = h_ref[...]
    per = PROJ_UNIT // LANES
    nq = SWA_HEADS * SWA_HEAD_DIM // PROJ_UNIT
    nk = SWA_KV_HEADS * SWA_HEAD_DIM // PROJ_UNIT

    def body(lat):
        rope = _rope_args(lat, cos_ref, sp_ref, sn_ref, SWA_HEAD_DIM)

        def emit(u, y):
            if u < nq + nk:
                y = _head_norm(y, m_ref, qg_ref[...] if u < nq else kg_ref[...])
            for t, yc in enumerate(_halves(y)):
                if u < nq:
                    _put(q_ref, u * per + t, _maybe_rope(yc, rope))
                    continue
                if u < nq + nk:
                    j, c_ref, d_ref = (u - nq) * per + t, ck_ref, kd_ref
                    yc = _maybe_rope(yc, rope)
                else:
                    j, c_ref, d_ref = (u - nq - nk) * per + t, cv_ref, vd_ref
                for a, dup in enumerate(_dup_halves(yc)):
                    _put(d_ref, 2 * j + a, dup)
                    if not lat:
                        _cache_rows(c_ref, [2 * j + a], dup[:, :HALF])

        _matmul_units(h, w_ref, nq + 2 * nk, PROJ_UNIT, emit)

    _by_tile_kind(i, body)


def _proj_swa(h, w, qg, kg, tables):
    nq, nk = SWA_HEADS * SWA_HEAD_DIM, SWA_KV_HEADS * SWA_HEAD_DIM
    return pl.pallas_call(
        _proj_swa_kernel,
        grid=(N_PROJ_TILES,),
        in_specs=[_tok_spec(D_MODEL),
                  _const_spec(w.shape), _const_spec((1, PROJ_UNIT)), _const_spec((1, PROJ_UNIT)),
                  _UNIT_MAT_SPEC, _ROPE_SPEC, _ROPE_SPEC, _ROPE_SPEC],
        out_specs=[_tok_spec(nq), _tok_spec(2 * nk), _tok_spec(2 * nk),
                   _cache_spec(SWA_KV_HEADS, SEQ, SWA_HEAD_DIM), _cache_spec(SWA_KV_HEADS, SEQ, SWA_HEAD_DIM)],
        out_shape=[jax.ShapeDtypeStruct((N_TOK, nq), BF16),
                   jax.ShapeDtypeStruct((N_TOK, 2 * nk), BF16),
                   jax.ShapeDtypeStruct((N_TOK, 2 * nk), BF16),
                   _cache_shape(SWA_KV_HEADS, SEQ, SWA_HEAD_DIM), _cache_shape(SWA_KV_HEADS, SEQ, SWA_HEAD_DIM)],
        compiler_params=_cparams(1),
        name="proj_swa",
    )(h, w, qg, kg, _group_mean_matrix(SWA_HEAD_DIM), *tables)


def _mla_lane_matrices():
    everything = lambda lane: lane >= 0
    return (_lane_sum_matrix(everything, everything),
            _lane_sum_matrix(lambda lane: lane < HALF, everything),
            _lane_sum_matrix(lambda lane: lane >= HALF, everything))


def _proj_mla_kernel(h_ref, w_in_ref, qa_ref, kva_ref, w_uq_ref,
                     qg_ref, qgp_ref, all_ref, lo_ref, hi_ref, cos_ref, sp_ref, sn_ref,
                     qn_ref, qp_ref, ckv_ref, kpe_ref, c_ckv_ref, c_kpe_ref):
    i = pl.program_id(0)
    y = _dot(h_ref[...], w_in_ref[...])
    c_q = y[:, :MLA_Q_RANK]
    c_kv = y[:, MLA_Q_RANK:MLA_Q_RANK + MLA_KV_RANK]
    kpe = y[:, MLA_Q_RANK + MLA_KV_RANK:]
    kpe_ref[...] = kpe
    ckv = c_kv * lax.rsqrt(jnp.mean(c_kv * c_kv, axis=-1, keepdims=True) + EPS) * kva_ref[...]
    ckv_ref[...] = ckv.astype(BF16)
    cq = (c_q * lax.rsqrt(jnp.mean(c_q * c_q, axis=-1, keepdims=True) + EPS) * qa_ref[...]).astype(BF16)
    lo = _lane_lo((PROJ_TM, LANES))
    inv_d = 1.0 / (MLA_NOPE + MLA_ROPE)

    def body(lat):
        if not lat:
            _cache_rows(c_ckv_ref, [], ckv)
            _cache_rows(c_kpe_ref, [], kpe[:, :MLA_ROPE])

        def emit(j, yq):
            pe = yq[:, 2 * LANES:]
            pe_sq = _sq_bf16(pe)
            rs = []
            for a, half_ref in enumerate((lo_ref, hi_ref)):
                nope = yq[:, a * LANES:(a + 1) * LANES]
                ss = _dot(_sq_bf16(nope), all_ref[...]) + _dot(pe_sq, half_ref[...])
                r = lax.rsqrt(ss * inv_d + EPS)
                rs.append(r)
                _put(qn_ref, 2 * j + a, nope * r * qg_ref[...])
            pe = pe * jnp.where(lo, rs[0], rs[1]) * qgp_ref[...]
            if lat:
                pe = _rope(pe, cos_ref[...], sp_ref[...], sn_ref[...], MLA_ROPE // 4)
            _put(qp_ref, j, pe)

        _matmul_units(cq, w_uq_ref, MLA_HEADS // 2, 3 * LANES, emit)

    _by_tile_kind(i, body)


def _proj_mla(h, w_in, qa, kva, w_uq, qg, qgp, tables):
    n_nope = MLA_HEADS * MLA_NOPE
    n_pe = MLA_HEADS * MLA_ROPE
    return pl.pallas_call(
        _proj_mla_kernel,
        grid=(N_PROJ_TILES,),
        in_specs=[_tok_spec(D_MODEL),
                  _const_spec(w_in.shape), _const_spec((1, MLA_Q_RANK)), _const_spec((1, MLA_KV_RANK)),
                  _const_spec(w_uq.shape), _const_spec((1, LANES)), _const_spec((1, LANES)),
                  _LANE_MAT_SPEC, _LANE_MAT_SPEC, _LANE_MAT_SPEC,
                  _ROPE_SPEC, _ROPE_SPEC, _ROPE_SPEC],
        out_specs=[_tok_spec(n_nope), _tok_spec(n_pe), _tok_spec(MLA_KV_RANK), _tok_spec(LANES),
                   _cache_spec(SEQ, MLA_KV_RANK), _cache_spec(SEQ, MLA_ROPE)],
        out_shape=[jax.ShapeDtypeStruct((N_TOK, n_nope), BF16),
                   jax.ShapeDtypeStruct((N_TOK, n_pe), BF16),
                   jax.ShapeDtypeStruct((N_TOK, MLA_KV_RANK), BF16),
                   jax.ShapeDtypeStruct((N_TOK, LANES), F32),
                   _cache_shape(SEQ, MLA_KV_RANK), _cache_shape(SEQ, MLA_ROPE)],
        compiler_params=_cparams(1),
        name="proj_mla",
    )(h, w_in, qa, kva, w_uq, qg, qgp, *_mla_lane_matrices(), *tables)


def _mla_expand_kernel(ckv_ref, kpe_ref, w_ref, kg_ref, kgp_ref, sum_ref, lo_ref,
                       cos_ref, sp_ref, sn_ref, kn_ref, kp_ref, v_ref, *, rope):
    i = pl.program_id(0)
    ckv = ckv_ref[...].astype(BF16)
    kpe = kpe_ref[...]
    pe_ss = _dot(_sq_bf16(kpe), lo_ref[...])
    pe_ss = jnp.concatenate([pe_ss, pe_ss], axis=1)
    lo = _lane_lo(kpe.shape)
    inv_d = 1.0 / (MLA_NOPE + MLA_ROPE)

    def body(lat):
        def emit(j, y):
            kn = jnp.concatenate([y[:, :LANES], y[:, 2 * LANES:3 * LANES]], axis=1)
            r = lax.rsqrt((_dot(_sq_bf16(kn), sum_ref[...]) + pe_ss) * inv_d + EPS)
            kn = kn * r * kg_ref[...]
            for a in range(2):
                _put(kn_ref, 2 * j + a, kn[:, a * LANES:(a + 1) * LANES])
                _put(v_ref, 2 * j + a, y[:, (2 * a + 1) * LANES:(2 * a + 2) * LANES])
            pe = kpe * jnp.where(lo, r[:, :LANES], r[:, LANES:]) * kgp_ref[...]
            if lat:
                pe = _rope(pe, cos_ref[...], sp_ref[...], sn_ref[...], MLA_ROPE // 4)
            _put(kp_ref, j, pe)

        _matmul_units(ckv, w_ref, MLA_HEADS // 2, 4 * LANES, emit)

    if rope:
        _by_tile_kind(i, body)
    else:
        body(False)


def _mla_expand(ckv, kpe_dup, w_ukv, kg, kgp, tables, rope):
    n = ckv.shape[0]
    n_nope = MLA_HEADS * MLA_NOPE
    n_pe = MLA_HEADS * MLA_ROPE
    _, m_lo, _ = _mla_lane_matrices()
    return pl.pallas_call(
        functools.partial(_mla_expand_kernel, rope=rope),
        grid=(n // PROJ_TM,),
        in_specs=[_tok_spec(MLA_KV_RANK), _tok_spec(LANES), _const_spec(w_ukv.shape),
                  _const_spec((1, PROJ_UNIT)), _const_spec((1, LANES)), _UNIT_MAT_SPEC, _LANE_MAT_SPEC,
                  _ROPE_SPEC, _ROPE_SPEC, _ROPE_SPEC],
        out_specs=[_tok_spec(n_nope), _tok_spec(n_pe), _tok_spec(n_nope)],
        out_shape=[jax.ShapeDtypeStruct((n, n_nope), BF16),
                   jax.ShapeDtypeStruct((n, n_pe), BF16),
                   jax.ShapeDtypeStruct((n, n_nope), BF16)],
        compiler_params=_cparams(1),
        name="mla_expand",
    )(ckv, kpe_dup, w_ukv, kg, kgp, _group_sum_matrix(), m_lo, *tables)


def _prompt_spec(width):
    return pl.BlockSpec((PROMPT_SEQS * TM, width), lambda b: (b, 0))


def _latq_spec(rows, width):
    per = DEC_SEQ // rows
    return pl.BlockSpec((rows, width), lambda b, t: (N_PROMPT_TOK // rows + b * per + t, 0))


def _latkv_spec(width):
    return pl.BlockSpec((DEC_SEQ, width), lambda b, t: (LAT_BLOCK0 + b, 0))


def _lato_spec(rows):
    per = DEC_SEQ // rows
    return pl.BlockSpec((rows, D_MODEL), lambda b, t: (b * per + t, 0))


def _att_kernel(*refs, with_ctx, seqs):
    if with_ctx:
        q_ref, k_ref, v_ref, kc_ref, vc_ref, o_ref = refs
    else:
        q_ref, k_ref, v_ref, o_ref = refs
    tq = q_ref.shape[0] // seqs
    nu = ATT_UNIT_HEADS
    per_kv = ATT_HEADS // ATT_KV_HEADS // nu

    def make(views):
        q_v, k_v, v_v, o_v = views

        def scores(u):
            q = jnp.concatenate([_chunk(q_v, u * nu + g) for g in range(nu)], axis=0)
            s_list = [_dot_nt(_chunk(k_v, u // per_kv), q)]
            if with_ctx:
                s_list.append(_dot_nt(kc_ref[u // per_kv].astype(BF16), q))
            return s_list

        def finish(u, s_list):
            values = [_chunk(v_v, u // per_kv)]
            if with_ctx:
                values.append(vc_ref[u // per_kv].astype(BF16))
            ps, inv = _softmax2_parts(s_list)
            o = _pv(ps, values) * inv
            for g in range(nu):
                o_v[:, (u * nu + g) * LANES:(u * nu + g + 1) * LANES] = (
                    o[:, g * tq:(g + 1) * tq].T.astype(o_v.dtype))

        return scores, finish

    _seq_pipeline((q_ref, k_ref, v_ref, o_ref), seqs, ATT_HEADS // nu, make)


def _att_attend(q, k, v, cache_k, cache_v):
    nk = ATT_KV_HEADS * ATT_HEAD_DIM
    out_p = pl.pallas_call(
        functools.partial(_att_kernel, with_ctx=False, seqs=PROMPT_SEQS),
        grid=(N_PROMPT_TILES // PROMPT_SEQS,),
        in_specs=[_prompt_spec(D_MODEL), _prompt_spec(nk), _prompt_spec(nk)],
        out_specs=_prompt_spec(D_MODEL),
        out_shape=jax.ShapeDtypeStruct((N_PROMPT_TOK, D_MODEL), BF16),
        compiler_params=_cparams(1),
        name="att_prompt",
    )(q, k, v)
    ctx = pl.BlockSpec((None, None, ATT_KV_HEADS, PAST_LEN, LANES), lambda b, t: (b, 0, 0, 0, 0))
    out_s = pl.pallas_call(
        functools.partial(_att_kernel, with_ctx=True, seqs=1),
        grid=(DEC_BATCH, TILES_PER_DEC),
        in_specs=[_latq_spec(TM, D_MODEL), _latkv_spec(nk), _latkv_spec(nk), ctx, ctx],
        out_specs=_lato_spec(TM),
        out_shape=jax.ShapeDtypeStruct((N_LAT_TOK, D_MODEL), BF16),
        compiler_params=_cparams(2),
        name="att_latent",
    )(q, k, v, cache_k, cache_v)
    return out_p, out_s


def _diff_kernel(*refs, lam_init, with_ctx, seqs):
    if with_ctx:
        (q_ref, k_ref, v_ref, kc_ref, vc_ref, lq1_ref, lk1_ref, lq2_ref, lk2_ref, sub_ref, o_ref) = refs
    else:
        (q_ref, k_ref, v_ref, lq1_ref, lk1_ref, lq2_ref, lk2_ref, sub_ref, o_ref) = refs
    tq = q_ref.shape[0] // seqs
    lam = (jnp.exp(jnp.sum(lq1_ref[...] * lk1_ref[...], axis=-1, keepdims=True))
           - jnp.exp(jnp.sum(lq2_ref[...] * lk2_ref[...], axis=-1, keepdims=True)) + lam_init)
    sub = sub_ref[...] * (1.0 - lam_init)

    def make(views):
        q_v, k_v, v_v, o_v = views

        def scores(hd):
            q = jnp.concatenate(_split_halves(_chunk(q_v, hd)), axis=0)
            s_list = [_dot_nt(_chunk(k_v, hd), q)]
            if with_ctx:
                s_list.append(_dot_nt(kc_ref[hd].astype(BF16), q))
            return s_list

        def finish(hd, s_list):
            values = [_chunk(v_v, hd)]
            if with_ctx:
                values.append(vc_ref[hd].astype(BF16))
            ps, inv = _softmax2_parts(s_list)
            o = (_pv([p[:, :tq] for p in ps], values) * inv[:, :tq]
                 - _pv([p[:, tq:] for p in ps], values) * (lam * inv[:, tq:]))
            o = o * lax.rsqrt(jnp.mean(o * o, axis=0, keepdims=True) + EPS) * sub
            o_v[:, hd * LANES:(hd + 1) * LANES] = o.T.astype(o_v.dtype)

        return scores, finish

    _seq_pipeline((q_ref, k_ref, v_ref, o_ref), seqs, DIFF_HEADS, make)


def _diff_attend(q, k, v, cache_k_pair, cache_v, lq1, lk1, lq2, lk2, subln, lam_init):
    small = [lq1, lk1, lq2, lk2, subln]
    small_specs = [_const_spec(s.shape) for s in small]
    out_p = pl.pallas_call(
        functools.partial(_diff_kernel, lam_init=lam_init, with_ctx=False, seqs=PROMPT_SEQS),
        grid=(N_PROMPT_TILES // PROMPT_SEQS,),
        in_specs=[_prompt_spec(D_MODEL)] * 3 + small_specs,
        out_specs=_prompt_spec(D_MODEL),
        out_shape=jax.ShapeDtypeStruct((N_PROMPT_TOK, D_MODEL), BF16),
        compiler_params=_cparams(1),
        name="diff_prompt",
    )(q, k, v, *small)
    out_s = pl.pallas_call(
        functools.partial(_diff_kernel, lam_init=lam_init, with_ctx=True, seqs=1),
        grid=(DEC_BATCH, TILES_PER_DEC),
        in_specs=[_latq_spec(TM, D_MODEL), _latkv_spec(D_MODEL), _latkv_spec(D_MODEL),
                  pl.BlockSpec((None, DIFF_HEADS, PAST_LEN, LANES), lambda b, t: (b, 0, 0, 0)),
                  pl.BlockSpec((None, None, DIFF_HEADS, PAST_LEN, LANES), lambda b, t: (b, 0, 0, 0, 0))]
                 + small_specs,
        out_specs=_lato_spec(TM),
        out_shape=jax.ShapeDtypeStruct((N_LAT_TOK, D_MODEL), BF16),
        compiler_params=_cparams(2),
        name="diff_latent",
    )(q, k, v, cache_k_pair, cache_v, *small)
    return out_p, out_s


def _swa_pipeline(q_ref, o_ref, seq_refs, sink_ref, score_fns, value_fns, seqs=1):
    tq = q_ref.shape[0] // seqs
    per_kv = SWA_HEADS // SWA_KV_HEADS // 2
    first = lax.broadcasted_iota(jnp.int32, (LANES, tq), 0) < HALF

    def make(views):
        q_v, o_v = views[:2]
        kv_views = views[2:]

        def scores(c):
            q = jnp.concatenate(_split_halves(_chunk(q_v, c)), axis=0)
            return [fn(kv_views, c // per_kv, q) for fn in score_fns]

        def finish(c, s_list):
            sink = jnp.concatenate([jnp.full((1, tq), sink_ref[2 * c + a] * LOG2E, F32) for a in range(2)],
                                   axis=1)
            ps, inv = _softmax2_parts(s_list, extra=sink)
            o = _pv(ps, [fn(kv_views, c // per_kv) for fn in value_fns]) * inv
            oc = jnp.where(first, o[:, :tq], o[:, tq:])
            o_v[:, c * LANES:(c + 1) * LANES] = oc.T.astype(o_v.dtype)

        return scores, finish

    _seq_pipeline((q_ref, o_ref) + tuple(seq_refs), seqs, SWA_HEADS // 2, make)


def _swa_prompt_kernel(sink_ref, q_ref, k_ref, v_ref, o_ref):
    _swa_pipeline(q_ref, o_ref, (k_ref, v_ref), sink_ref,
                  [lambda kv_v, kv, q: _dot_nt(_chunk(kv_v[0], kv), q)],
                  [lambda kv_v, kv: _chunk(kv_v[1], kv)], seqs=PROMPT_SEQS)


def _swa_latent_kernel(sink_ref, q_ref, k_ref, v_ref, kc_ref, vc_ref, o_ref):
    n = pl.program_id(1)
    tq = q_ref.shape[0]
    span = 3 * SWA_QB
    start = pl.multiple_of(jnp.clip((n - 1) * SWA_QB, 0, DEC_SEQ - span), SWA_QB)
    cols = lax.broadcasted_iota(jnp.int32, (span, 2 * tq), 1)
    qpos = n * SWA_QB + jnp.bitwise_and(cols, tq - 1)
    kpos = start + lax.broadcasted_iota(jnp.int32, (span, 2 * tq), 0)
    valid = jnp.abs(qpos - kpos) <= WINDOW

    def local(ref, kv):
        return ref[pl.ds(start, span), kv * LANES:(kv + 1) * LANES]

    _swa_pipeline(q_ref, o_ref, (), sink_ref,
                  [lambda _, kv, q: jnp.where(valid, _dot_nt(local(k_ref, kv), q), -1e30),
                   lambda _, kv, q: _dot_nt(kc_ref[kv], q)],
                  [lambda _, kv: local(v_ref, kv), lambda _, kv: vc_ref[kv]])


def _swa_attend(q, kd, vd, cache_kd, cache_vd, sink):
    nkd = 2 * SWA_KV_HEADS * SWA_HEAD_DIM
    smem = pl.BlockSpec(memory_space=pltpu.SMEM)
    out_p = pl.pallas_call(
        _swa_prompt_kernel,
        grid=(N_PROMPT_TILES // PROMPT_SEQS,),
        in_specs=[smem, _prompt_spec(D_MODEL), _prompt_spec(nkd), _prompt_spec(nkd)],
        out_specs=_prompt_spec(D_MODEL),
        out_shape=jax.ShapeDtypeStruct((N_PROMPT_TOK, D_MODEL), BF16),
        compiler_params=_cparams(1),
        name="swa_prompt",
    )(sink, q, kd, vd)
    ctx = pl.BlockSpec((None, SWA_KV_HEADS, PAST_LEN, LANES), lambda b, n: (b, 0, 0, 0))
    out_s = pl.pallas_call(
        _swa_latent_kernel,
        grid=(DEC_BATCH, DEC_SEQ // SWA_QB),
        in_specs=[smem, _latq_spec(SWA_QB, D_MODEL), _latkv_spec(nkd), _latkv_spec(nkd), ctx, ctx],
        out_specs=_lato_spec(SWA_QB),
        out_shape=jax.ShapeDtypeStruct((N_LAT_TOK, D_MODEL), BF16),
        compiler_params=_cparams(2),
        name="swa_latent",
    )(sink, q, kd, vd, cache_kd, cache_vd)
    return out_p, out_s


def _mla_kernel(*refs, with_ctx, seqs):
    if with_ctx:
        (qn_ref, qp_ref, kn_ref, kp_ref, v_ref, knc_ref, kpc_ref, vc_ref, o_ref) = refs
    else:
        (qn_ref, qp_ref, kn_ref, kp_ref, v_ref, o_ref) = refs

    def make(views):
        qn_v, qp_v, kn_v, kp_v, v_v, o_v = views

        def scores(hd):
            j, a = hd // 2, hd % 2
            q = jnp.concatenate([_chunk(qn_v, hd), _split_halves(_chunk(qp_v, j))[a]], axis=1)
            s_list = [_dot_nt(jnp.concatenate([_chunk(kn_v, hd), _chunk(kp_v, j)], axis=1), q)]
            if with_ctx:
                s_list.append(_dot_nt(jnp.concatenate([_chunk(knc_ref, hd), _chunk(kpc_ref, j)], axis=1), q))
            return s_list

        def finish(hd, s_list):
            values = [_chunk(v_v, hd)]
            if with_ctx:
                values.append(_chunk(vc_ref, hd))
            ps, inv = _softmax2_parts(s_list)
            o_v[:, hd * LANES:(hd + 1) * LANES] = (_pv(ps, values) * inv).T.astype(o_v.dtype)

        return scores, finish

    _seq_pipeline((qn_ref, qp_ref, kn_ref, kp_ref, v_ref, o_ref), seqs, MLA_HEADS, make)


def _mla_attend(qn, qp, kn, kp, v, knc, kpc, vc):
    n_pe = MLA_HEADS * MLA_ROPE
    out_p = pl.pallas_call(
        functools.partial(_mla_kernel, with_ctx=False, seqs=PROMPT_SEQS),
        grid=(N_PROMPT_TILES // PROMPT_SEQS,),
        in_specs=[_prompt_spec(D_MODEL), _prompt_spec(n_pe), _prompt_spec(D_MODEL), _prompt_spec(n_pe),
                  _prompt_spec(D_MODEL)],
        out_specs=_prompt_spec(D_MODEL),
        out_shape=jax.ShapeDtypeStruct((N_PROMPT_TOK, D_MODEL), BF16),
        compiler_params=_cparams(1),
        name="mla_prompt",
    )(qn, qp, kn, kp, v)

    def ctx(width):
        return pl.BlockSpec((PAST_LEN, width), lambda b, t: (b, 0))

    out_s = pl.pallas_call(
        functools.partial(_mla_kernel, with_ctx=True, seqs=1),
        grid=(DEC_BATCH, TILES_PER_DEC),
        in_specs=[_latq_spec(TM, D_MODEL), _latq_spec(TM, n_pe),
                  _latkv_spec(D_MODEL), _latkv_spec(n_pe), _latkv_spec(D_MODEL),
                  ctx(D_MODEL), ctx(n_pe), ctx(D_MODEL)],
        out_specs=_lato_spec(TM),
        out_shape=jax.ShapeDtypeStruct((N_LAT_TOK, D_MODEL), BF16),
        compiler_params=_cparams(2),
        name="mla_latent",
    )(qn, qp, kn, kp, v, knc, kpc, vc)
    return out_p, out_s


def _omlp_kernel(*refs, first, last):
    refs = list(refs)
    ap_ref, as_ref, wo_ref = refs[:3]
    x_refs = refs[3:5] if first else refs[3:4]
    refs = refs[3 + len(x_refs):]
    g1_ref, gain_ref, sh_ref, sc_ref, g2_ref, w1c_ref, w2c_ref = refs[:7]
    refs = refs[7:]
    if last:
        op_ref, os_ref = refs[:2]
    else:
        ngain_ref, nsh_ref, nsc_ref, o_ref, hn_ref = refs[:5]
    wo_s, w1_s, w2_s, h0_s, x1_s, acc_s = refs[-6:]
    s = pl.program_id(0)
    per = MLP_FF_CHUNK // MLP_LOAD_COLS
    n_chunks = D_FF // MLP_FF_CHUNK
    half = MLP_TM // 2
    rows = [slice(r * half, (r + 1) * half) for r in range(2)]

    def head(rw, is_prompt, mod):
        a = ap_ref[rw, :] if is_prompt is True else jnp.where(is_prompt, ap_ref[rw, :], as_ref[rw, :])
        if not first:
            x = x_refs[0][rw, :]
        elif is_prompt is True:
            x = x_refs[0][rw, :]
        else:
            x = jnp.where(is_prompt, x_refs[0][rw, :], x_refs[1][rw, :])
        x1 = x + mod(g1_ref) * _dot(a, wo_s[...])
        return x1, _norm_mod(x1, gain_ref[...], mod(sh_ref), mod(sc_ref)).astype(BF16)

    def tail(rw, x1, acc, is_prompt, mod):
        out = x1 + mod(g2_ref) * acc
        if not last:
            o_ref[rw, :] = out
            hn_ref[rw, :] = _norm_mod(out, ngain_ref[...], mod(nsh_ref), mod(nsc_ref)).astype(BF16)
        elif is_prompt is True:
            op_ref[rw, :] = out
        else:
            @pl.when(is_prompt)
            def _():
                op_ref[rw, :] = out

            @pl.when(jnp.logical_not(is_prompt))
            def _():
                os_ref[rw, :] = out

    def first_row(ref):
        return ref[0:1, :]

    @pl.when(s == 0)
    def _():
        wo_s[...] = wo_ref[...].astype(BF16)
        for rw in rows:
            x1, h = head(rw, True, first_row)
            x1_s[rw, :] = x1
            h0_s[rw, :] = h
        acc_s[...] = jnp.zeros_like(acc_s)

    for part in range(per):
        @pl.when((s < N_LOAD_STEPS) & (s % per == part))
        def _(part=part):
            w1c = w1c_ref[...].astype(BF16)
            w2c = w2c_ref[...].astype(BF16)
            w1_s[s // per, :, part * MLP_LOAD_COLS:(part + 1) * MLP_LOAD_COLS] = w1c
            w2_s[s // per, part * MLP_LOAD_COLS:(part + 1) * MLP_LOAD_COLS, :] = w2c
            u = jnp.square(jnp.maximum(_dot(h0_s[...], w1c), 0.0)).astype(BF16)
            acc_s[...] += _dot(u, w2c)

    @pl.when(s == N_LOAD_STEPS - 1)
    def _():
        for rw in rows:
            tail(rw, x1_s[rw, :], acc_s[rw, :], True, first_row)

    @pl.when(s >= N_LOAD_STEPS)
    def _():
        t = s - N_LOAD_STEPS + 1
        is_prompt = t < N_MLP_PROMPT_TILES
        grp = _tile_group(t, MLP_TM)

        def mod(ref):
            return ref[pl.ds(grp, 1), :]

        x1, h, u0 = [], [], []
        for rw in rows:
            x1_r, h_r = head(rw, is_prompt, mod)
            x1.append(x1_r)
            h.append(h_r)
            u0.append(_dot(h_r, w1_s[0]))
        h = jnp.concatenate(h, axis=0)
        acc = []

        def up(c):
            return jnp.concatenate(u0, axis=0) if c == 0 else _dot(h, w1_s[c])

        def down(c, u):
            u = jnp.square(jnp.maximum(u, 0.0)).astype(BF16)
            if c + 1 < n_chunks:
                y = _dot(u, w2_s[c])
                acc[:] = [y if not acc else acc[0] + y]
            else:
                acc[:] = [acc[0][rw] + _dot(u[rw], w2_s[c]) for rw in rows]

        _head_pipeline(n_chunks, up, down)
        for r, rw in enumerate(rows):
            tail(rw, x1[r], acc[r], is_prompt, mod)


def _omlp(attn_p, attn_s, w_o, x, mods, gain_ffn, w1_all, w2_all, layer, next_gain, next_mods):
    first, last = layer == 0, next_gain is None
    n_lat_tiles = N_LAT_TOK // MLP_TM

    def tok(s):
        return jnp.maximum(s - N_LOAD_STEPS + 1, 0)

    p_spec = pl.BlockSpec((MLP_TM, D_MODEL), lambda s: (jnp.minimum(tok(s), N_MLP_PROMPT_TILES - 1), 0))
    l_spec = pl.BlockSpec((MLP_TM, D_MODEL),
                          lambda s: (jnp.clip(tok(s) - N_MLP_PROMPT_TILES, 0, n_lat_tiles - 1), 0))
    w1_spec = pl.BlockSpec((None, D_MODEL, MLP_LOAD_COLS),
                           lambda s: (layer, 0, jnp.minimum(s, N_LOAD_STEPS - 1)))
    w2_spec = pl.BlockSpec((None, MLP_LOAD_COLS, D_MODEL),
                           lambda s: (layer, jnp.minimum(s, N_LOAD_STEPS - 1), 0))
    t_spec = pl.BlockSpec((MLP_TM, D_MODEL), lambda s: (tok(s), 0))
    n_chunks = D_FF // MLP_FF_CHUNK
    split = ([p_spec, l_spec], [jax.ShapeDtypeStruct((N_PROMPT_TOK, D_MODEL), F32),
                                jax.ShapeDtypeStruct((N_LAT_TOK, D_MODEL), F32)])
    in_specs = ([p_spec, l_spec, _const_spec(w_o.shape)] + (split[0] if first else [t_spec])
                + [_mod_spec(2), _const_spec((1, D_MODEL)), _mod_spec(3), _mod_spec(4), _mod_spec(5),
                   w1_spec, w2_spec])
    args = ([attn_p, attn_s, w_o] + (list(x) if first else [x])
            + [mods, gain_ffn, mods, mods, mods, w1_all, w2_all])
    if last:
        out_specs, out_shape = split
    else:
        in_specs += [_const_spec((1, D_MODEL)), _mod_spec(0), _mod_spec(1)]
        args += [next_gain, next_mods, next_mods]
        out_specs = [t_spec, t_spec]
        out_shape = [jax.ShapeDtypeStruct((N_TOK, D_MODEL), F32), jax.ShapeDtypeStruct((N_TOK, D_MODEL), BF16)]
    return pl.pallas_call(
        functools.partial(_omlp_kernel, first=first, last=last),
        grid=(N_LOAD_STEPS + N_TOK // MLP_TM - 1,),
        in_specs=in_specs,
        out_specs=out_specs,
        out_shape=out_shape,
        scratch_shapes=[pltpu.VMEM((D_MODEL, D_MODEL), BF16),
                        pltpu.VMEM((n_chunks, D_MODEL, MLP_FF_CHUNK), BF16),
                        pltpu.VMEM((n_chunks, MLP_FF_CHUNK, D_MODEL), BF16),
                        pltpu.VMEM((MLP_TM, D_MODEL), BF16),
                        pltpu.VMEM((MLP_TM, D_MODEL), F32),
                        pltpu.VMEM((MLP_TM, D_MODEL), F32)],
        compiler_params=_cparams(1),
        name="omlp",
    )(*args)


def _row(v, scale=1.0):
    return (v.astype(F32) * scale).reshape(1, -1)


def _pair(v, scale=1.0):
    return (jnp.concatenate([v, v]).astype(F32) * scale).reshape(1, LANES)


def _unit_gain(v, scale=1.0):
    return (jnp.tile(v.astype(F32), PROJ_UNIT // v.shape[0]) * scale).reshape(1, PROJ_UNIT)


def kernel(x_prompt, x_sample, cache_att_k, cache_att_v, cache_diff_k, cache_diff_v, cache_swa_k, cache_swa_v, cache_mla_ckv, cache_mla_kpe, c, c_ctx, ada_w, ada_b, norm_mix, norm_ffn, att_w_qkv, att_q_norm, att_k_norm, att_w_o, diff_w_qkv, diff_q_norm, diff_k_norm, diff_lq1, diff_lk1, diff_lq2, diff_lk2, diff_subln, diff_w_o, swa_w_qkv, swa_q_norm, swa_k_norm, swa_sink, swa_w_o, mla_w_in, mla_q_a_norm, mla_kv_a_norm, mla_w_uq, mla_w_ukv, mla_q_norm, mla_k_norm, mla_w_o, mlp_w1, mlp_w2):
    xp = x_prompt.reshape(N_PROMPT_TOK, D_MODEL)
    xs = x_sample.reshape(N_LAT_TOK, D_MODEL)
    cond = jnp.concatenate([c_ctx[None], c, jnp.zeros((COND_ROWS - 1 - DEC_BATCH, D_MODEL), F32)], axis=0)
    mods_all = _modulation(cond, ada_w, ada_b)

    tab_att = _rope_tables(ATT_HEAD_DIM)
    tab_64 = _rope_tables(DIFF_HEAD_DIM)

    outs = {}
    x = (xp, xs)
    for layer in range(DEPTH):
        mods = mods_all[layer]
        gain_ffn = _row(norm_ffn[layer])
        if layer == 0:
            qs = ATT_HEAD_DIM ** -0.5 * LOG2E
            q, k, v, outs["att_k"], outs["att_v"] = _proj_att(
                xp, xs, mods, _row(norm_mix[layer]), att_w_qkv[0].astype(BF16),
                _unit_gain(att_q_norm[0], qs), _unit_gain(att_k_norm[0]), tab_att)
            attn_p, attn_s = _att_attend(q, k, v, cache_att_k, cache_att_v)
            w_o = att_w_o[0]
        elif layer == 1:
            qs = DIFF_HEAD_DIM ** -0.5 * LOG2E
            q, k, v, outs["diff_k"], outs["diff_v"] = _proj_diff(
                h, diff_w_qkv[0].astype(BF16),
                _unit_gain(diff_q_norm[0], qs), _unit_gain(diff_k_norm[0]), tab_64)
            lam_init = 0.8 - 0.6 * math.exp(-0.3 * layer)
            ck = cache_diff_k[:, 0].transpose(0, 1, 3, 2, 4).reshape(
                DEC_BATCH, DIFF_HEADS, PAST_LEN, LANES)
            attn_p, attn_s = _diff_attend(q, k, v, ck, cache_diff_v,
                                          _row(diff_lq1[0]), _row(diff_lk1[0]),
                                          _row(diff_lq2[0]), _row(diff_lk2[0]),
                                          diff_subln[0].astype(F32).reshape(LANES, 1), lam_init)
            w_o = diff_w_o[0]
        elif layer == 2:
            qs = SWA_HEAD_DIM ** -0.5 * LOG2E
            q, kd, vd, outs["swa_k"], outs["swa_v"] = _proj_swa(
                h, swa_w_qkv[0].astype(BF16),
                _unit_gain(swa_q_norm[0], qs), _unit_gain(swa_k_norm[0]), tab_64)
            ckd = jnp.concatenate([cache_swa_k[:, 0]] * 2, axis=-1).astype(BF16)
            cvd = jnp.concatenate([cache_swa_v[:, 0]] * 2, axis=-1).astype(BF16)
            attn_p, attn_s = _swa_attend(q, kd, vd, ckd, cvd, swa_sink[0].astype(F32))
            w_o = swa_w_o[0]
        else:
            qs = (MLA_NOPE + MLA_ROPE) ** -0.5 * LOG2E
            w_in = mla_w_in[0]
            w_in = jnp.concatenate([w_in, w_in[:, -MLA_ROPE:]], axis=1).astype(BF16)
            w_uq = mla_w_uq[0].reshape(MLA_Q_RANK, MLA_HEADS // 2, 2, MLA_NOPE + MLA_ROPE)
            w_uq = jnp.concatenate([w_uq[..., :MLA_NOPE].reshape(MLA_Q_RANK, MLA_HEADS // 2, 2 * MLA_NOPE),
                                    w_uq[..., MLA_NOPE:].reshape(MLA_Q_RANK, MLA_HEADS // 2, 2 * MLA_ROPE)],
                                   axis=-1).reshape(MLA_Q_RANK, -1).astype(BF16)
            w_ukv = mla_w_ukv[0].astype(BF16)
            qg, kg = mla_q_norm[0], mla_k_norm[0]
            qn, qp, ckv, kpe, outs["mla_ckv"], outs["mla_kpe"] = _proj_mla(
                h, w_in, _row(mla_q_a_norm[0]), _row(mla_kv_a_norm[0]), w_uq,
                _row(qg[:MLA_NOPE], qs), _pair(qg[MLA_NOPE:], qs), tab_64)
            kn, kp, vv = _mla_expand(ckv, kpe, w_ukv, _unit_gain(kg[:MLA_NOPE]), _pair(kg[MLA_NOPE:]),
                                     tab_64, True)
            c_ckv = cache_mla_ckv[:, 0].reshape(DEC_BATCH * PAST_LEN, MLA_KV_RANK)
            c_kpe = cache_mla_kpe[:, 0].reshape(DEC_BATCH * PAST_LEN, MLA_ROPE)
            c_kpe = jnp.concatenate([c_kpe, c_kpe], axis=-1)
            knc, kpc, vc = _mla_expand(c_ckv, c_kpe, w_ukv, _unit_gain(kg[:MLA_NOPE]), _pair(kg[MLA_NOPE:]),
                                       tab_64, False)
            attn_p, attn_s = _mla_attend(qn, qp, kn, kp, vv, knc, kpc, vc)
            w_o = mla_w_o[0]
        if layer + 1 < DEPTH:
            x, h = _omlp(attn_p, attn_s, w_o, x, mods, gain_ffn, mlp_w1, mlp_w2, layer,
                         _row(norm_mix[layer + 1]), mods_all[layer + 1])
        else:
            xp, xs = _omlp(attn_p, attn_s, w_o, x, mods, gain_ffn, mlp_w1, mlp_w2, layer, None, None)

    y_prompt = xp.reshape(BATCH, SEQ, D_MODEL)
    y_sample = xs.reshape(DEC_BATCH, DEC_SEQ, D_MODEL)
    return (y_prompt, y_sample, outs["att_k"], outs["att_v"], outs["diff_k"], outs["diff_v"],
            outs["swa_k"], outs["swa_v"], outs["mla_ckv"], outs["mla_kpe"])
```

```python
import functools
import math

import numpy as np
import jax
import jax.numpy as jnp
from jax import lax
from jax.experimental import pallas as pl
from jax.experimental.pallas import tpu as pltpu

D_MODEL = 1024
BATCH = 16
SEQ = 256
DEPTH = 4
DEC_BATCH = 2
DEC_SEQ = 1024
PAST_LEN = 256
GRID_W = 64
ROPE_THETA = 10000.0
EPS = 1e-6
D_FF = 4 * D_MODEL
MOD_CHUNKS = 6
LOG2E = 1.4426950408889634

ATT_HEADS, ATT_KV_HEADS, ATT_HEAD_DIM = 8, 2, 128
DIFF_HEADS, DIFF_HEAD_DIM = 8, 64
SWA_HEADS, SWA_KV_HEADS, SWA_HEAD_DIM, WINDOW = 16, 4, 64, 128
MLA_HEADS, MLA_NOPE, MLA_ROPE, MLA_VDIM = 8, 128, 64, 128
MLA_Q_RANK, MLA_KV_RANK = 512, 256

LANES = 128
HALF = LANES // 2
TM = 256
N_PROMPT_TOK = BATCH * SEQ
N_LAT_TOK = DEC_BATCH * DEC_SEQ
N_TOK = N_PROMPT_TOK + N_LAT_TOK
N_PROMPT_TILES = N_PROMPT_TOK // TM
TILES_PER_DEC = DEC_SEQ // TM
LAT_BLOCK0 = N_PROMPT_TOK // DEC_SEQ
COND_ROWS = 8
PROJ_TM = 512
PROJ_BATCHES = PROJ_TM // SEQ
N_PROJ_TILES = N_TOK // PROJ_TM
N_PROJ_PROMPT = N_PROMPT_TOK // PROJ_TM
PROJ_UNIT = 2 * LANES
MLP_TM = 512
MLP_FF_CHUNK = 512
MLP_LOAD_COLS = 256
N_LOAD_STEPS = D_FF // MLP_LOAD_COLS
N_MLP_PROMPT_TILES = N_PROMPT_TOK // MLP_TM
SWA_QB = 128
ATT_UNIT_HEADS = 4
PROMPT_SEQS = 4
VMEM_LIMIT = 56 * 1024 * 1024

F32 = jnp.float32
BF16 = jnp.bfloat16


def _cparams(n_axes):
    return pltpu.CompilerParams(dimension_semantics=("arbitrary",) * n_axes,
                                vmem_limit_bytes=VMEM_LIMIT)


def _dot(a, b):
    return jnp.dot(a, b, preferred_element_type=F32)


def _dot_nt(a, b):
    return lax.dot_general(a, b, (((1,), (1,)), ((), ())), preferred_element_type=F32)


def _dot_tn(a, b):
    return lax.dot_general(a, b, (((0,), (0,)), ((), ())), preferred_element_type=F32)


def _const_spec(shape):
    nd = len(shape)
    return pl.BlockSpec(shape, lambda *_: (0,) * nd, pipeline_mode=pl.Buffered(1))


def _chunk(ref, c, width=LANES):
    return ref[:, c * width:(c + 1) * width]


def _put(ref, c, val):
    ref[:, c * LANES:(c + 1) * LANES] = val.astype(ref.dtype)


def _tile_group(i, rows):
    n_prompt = N_PROMPT_TOK // rows
    return jnp.where(i < n_prompt, 0, 1 + (i - n_prompt) // (DEC_SEQ // rows))


def _rope_tile(i):
    return jnp.maximum(i - N_PROJ_PROMPT, 0) % (DEC_SEQ // PROJ_TM)


def _norm_mod(x, gain, shift, scale):
    ms = jnp.mean(x * x, axis=-1, keepdims=True)
    return x * lax.rsqrt(ms + EPS) * (gain * (1.0 + scale)) + shift


def _lane_lo(shape):
    return lax.broadcasted_iota(jnp.int32, shape, len(shape) - 1) < HALF


def _rope(y, cos, sin_prev, sin_next, quarter):
    return (y * cos + pltpu.roll(y, quarter, 1) * sin_prev
            + pltpu.roll(y, LANES - quarter, 1) * sin_next)


def _rope_tables(rot_dim):
    half = rot_dim // 2
    quarter = rot_dim // 4
    inv = np.float32(ROPE_THETA) ** (-np.arange(0, half, 2, dtype=np.float32) / np.float32(half))
    pos = np.arange(DEC_SEQ)
    row = (pos // GRID_W).astype(np.float32)
    col = (pos % GRID_W).astype(np.float32)
    lane = np.arange(LANES)
    dd = lane % rot_dim
    q = dd // quarter
    f = dd % quarter
    ang = np.where((q < 2)[None, :], row[:, None], col[:, None]) * inv[f][None, :]
    ang = ang.astype(np.float32)
    cos = np.cos(ang).astype(np.float32)
    sin = np.sin(ang).astype(np.float32)
    odd = (q % 2 == 1)[None, :]
    sin_prev = np.where(odd, sin, 0.0).astype(np.float32)
    sin_next = np.where(odd, 0.0, -sin).astype(np.float32)
    return jnp.asarray(cos), jnp.asarray(sin_prev), jnp.asarray(sin_next)


def _lane_sum_matrix(rows, cols, value=1.0):
    lane = np.arange(LANES)
    m = np.where(rows(lane)[:, None] & cols(lane)[None, :], value, 0.0).astype(np.float32)
    return jnp.asarray(m, dtype=BF16)


def _group_mean_matrix(group):
    lane = np.arange(PROJ_UNIT)
    m = np.where((lane[:, None] // group) == (lane[None, :] // group), 1.0 / group, 0.0)
    return jnp.asarray(m.astype(np.float32), dtype=BF16)


def _group_sum_matrix():
    lane = np.arange(PROJ_UNIT)
    m = np.where((lane[:, None] // LANES) == (lane[None, :] // LANES), 1.0, 0.0)
    return jnp.asarray(m.astype(np.float32), dtype=BF16)


def _sq_bf16(y):
    return (y * y).astype(BF16)


def _head_norm(y, m_ref, gain):
    return y * lax.rsqrt(_dot(_sq_bf16(y), m_ref[...]) + EPS) * gain


def _halves(y):
    return [y[:, t * LANES:(t + 1) * LANES] for t in range(y.shape[1] // LANES)]


def _matmul_units(h, w_ref, n_units, width, emit):
    def unit(u):
        return _dot(h, w_ref[:, u * width:(u + 1) * width])

    nxt = unit(0)
    for u in range(n_units):
        cur = nxt
        if u + 1 < n_units:
            nxt = unit(u + 1)
        emit(u, cur)


def _by_tile_kind(i, body):
    pl.when(i < N_PROJ_PROMPT)(functools.partial(body, False))
    pl.when(i >= N_PROJ_PROMPT)(functools.partial(body, True))


def _rope_args(lat, cos_ref, sp_ref, sn_ref, rot_dim):
    return (cos_ref[...], sp_ref[...], sn_ref[...], rot_dim // 4) if lat else None


def _maybe_rope(y, rope):
    return y if rope is None else _rope(y, *rope)


def _cache_rows(ref, index, val):
    for b in range(PROJ_BATCHES):
        ref[(b, 0) + tuple(index)] = val[b * SEQ:(b + 1) * SEQ]


def _cache_rows_t(ref, indices, val):
    for b in range(PROJ_BATCHES):
        t = val[b * SEQ:(b + 1) * SEQ].T
        for j, index in enumerate(indices):
            ref[(b, 0) + tuple(index)] = t[j * HALF:(j + 1) * HALF]


def _softmax2_parts(s_list, extra=None):
    m = jnp.max(s_list[0], axis=0, keepdims=True)
    for s in s_list[1:]:
        m = jnp.maximum(m, jnp.max(s, axis=0, keepdims=True))
    if extra is not None:
        m = jnp.maximum(m, extra)
    ps = [jnp.exp2(s - m) for s in s_list]
    mass = ps[0].sum(axis=0, keepdims=True)
    for p in ps[1:]:
        mass = mass + p.sum(axis=0, keepdims=True)
    if extra is not None:
        mass = mass + jnp.exp2(extra - m)
    return [p.astype(BF16) for p in ps], 1.0 / mass


def _head_pipeline(n, scores, finish):
    nxt = scores(0)
    for h in range(n):
        cur = nxt
        if h + 1 < n:
            nxt = scores(h + 1)
        finish(h, cur)


def _seq_pipeline(refs, seqs, n, make):
    fns = []
    for b in range(seqs):
        views = [r.at[b * (r.shape[0] // seqs):(b + 1) * (r.shape[0] // seqs)] for r in refs]
        fns.append(make(views))
    _head_pipeline(seqs * n, lambda i: fns[i // n][0](i % n), lambda i, s: fns[i // n][1](i % n, s))


def _pv(ps, values):
    o = None
    for p, v in zip(ps, values):
        t = _dot_tn(v, p)
        o = t if o is None else o + t
    return o


def _split_halves(q):
    lo = _lane_lo(q.shape)
    zero = jnp.zeros_like(q)
    return jnp.where(lo, q, zero), jnp.where(lo, zero, q)


def _mod_kernel(cond_ref, w_ref, b_ref, o_ref):
    c = cond_ref[...]
    s = (c * jax.nn.sigmoid(c)).astype(BF16)
    o_ref[0] = _dot(s, w_ref[0].astype(BF16)) + b_ref[0]


def _modulation(cond, ada_w, ada_b):
    tn = 1536
    n = MOD_CHUNKS * D_MODEL
    return pl.pallas_call(
        _mod_kernel,
        grid=(DEPTH, n // tn),
        in_specs=[
            pl.BlockSpec((COND_ROWS, D_MODEL), lambda l, j: (0, 0)),
            pl.BlockSpec((1, D_MODEL, tn), lambda l, j: (l, 0, j)),
            pl.BlockSpec((1, 1, tn), lambda l, j: (l, 0, j)),
        ],
        out_specs=pl.BlockSpec((1, COND_ROWS, tn), lambda l, j: (l, 0, j)),
        out_shape=jax.ShapeDtypeStruct((DEPTH, COND_ROWS, n), F32),
        compiler_params=_cparams(2),
        name="modulation",
    )(cond, ada_w, ada_b.reshape(DEPTH, 1, n))


def _mod_spec(layer, chunk):
    return pl.BlockSpec((None, COND_ROWS, D_MODEL), lambda i: (layer, 0, chunk))


def _mod_row(ref, i):
    return ref[pl.ds(_tile_group(i, PROJ_TM), 1), :]


_ROPE_SPEC = pl.BlockSpec((PROJ_TM, LANES), lambda i: (_rope_tile(i), 0))
_LANE_MAT_SPEC = _const_spec((LANES, LANES))
_UNIT_MAT_SPEC = _const_spec((PROJ_UNIT, PROJ_UNIT))


def _tok_spec(width):
    return pl.BlockSpec((PROJ_TM, width), lambda i: (i, 0))


_XP_SPEC = pl.BlockSpec((PROJ_TM, D_MODEL), lambda i: (jnp.minimum(i, N_PROJ_PROMPT - 1), 0))
_XS_SPEC = pl.BlockSpec((PROJ_TM, D_MODEL), lambda i: (jnp.maximum(i - N_PROJ_PROMPT, 0), 0))


def _cache_spec(*dims):
    nd = len(dims)
    return pl.BlockSpec((PROJ_BATCHES, 1) + dims,
                        lambda i: (jnp.minimum(i, N_PROJ_PROMPT - 1), 0) + (0,) * nd)


def _cache_shape(*dims):
    return jax.ShapeDtypeStruct((BATCH, 1) + dims, F32)


def _proj_att_kernel(xp_ref, xs_ref, gain_ref, sh_ref, sc_ref, w_ref, qg_ref, kg_ref, m_ref,
                     cos_ref, sp_ref, sn_ref, q_ref, k_ref, v_ref, ck_ref, cv_ref):
    i = pl.program_id(0)
    x = jnp.where(i < N_PROJ_PROMPT, xp_ref[...], xs_ref[...])
    h = _norm_mod(x, gain_ref[...], _mod_row(sh_ref, i), _mod_row(sc_ref, i)).astype(BF16)
    per = PROJ_UNIT // LANES
    nq, nk = ATT_HEADS // per, ATT_KV_HEADS // per

    def body(lat):
        rope = _rope_args(lat, cos_ref, sp_ref, sn_ref, ATT_HEAD_DIM)

        def emit(u, y):
            if u < nq + nk:
                y = _head_norm(y, m_ref, qg_ref[...] if u < nq else kg_ref[...])
            for t, yc in enumerate(_halves(y)):
                if u < nq:
                    _put(q_ref, u * per + t, _maybe_rope(yc, rope))
                elif u < nq + nk:
                    kn = _maybe_rope(yc, rope)
                    _put(k_ref, (u - nq) * per + t, kn)
                    if not lat:
                        _cache_rows(ck_ref, [(u - nq) * per + t], kn)
                else:
                    _put(v_ref, (u - nq - nk) * per + t, yc)
                    if not lat:
                        _cache_rows(cv_ref, [(u - nq - nk) * per + t], yc)

        _matmul_units(h, w_ref, nq + 2 * nk, PROJ_UNIT, emit)

    _by_tile_kind(i, body)


def _proj_att(xp, xs, mods, gain, w, qg, kg, tables):
    nq, nk = ATT_HEADS * ATT_HEAD_DIM, ATT_KV_HEADS * ATT_HEAD_DIM
    return pl.pallas_call(
        _proj_att_kernel,
        grid=(N_PROJ_TILES,),
        in_specs=[_XP_SPEC, _XS_SPEC, _const_spec((1, D_MODEL)), _mod_spec(0, 0), _mod_spec(0, 1),
                  _const_spec(w.shape), _const_spec((1, PROJ_UNIT)), _const_spec((1, PROJ_UNIT)),
                  _UNIT_MAT_SPEC, _ROPE_SPEC, _ROPE_SPEC, _ROPE_SPEC],
        out_specs=[_tok_spec(nq), _tok_spec(nk), _tok_spec(nk),
                   _cache_spec(ATT_KV_HEADS, SEQ, ATT_HEAD_DIM), _cache_spec(ATT_KV_HEADS, SEQ, ATT_HEAD_DIM)],
        out_shape=[jax.ShapeDtypeStruct((N_TOK, nq), BF16),
                   jax.ShapeDtypeStruct((N_TOK, nk), BF16),
                   jax.ShapeDtypeStruct((N_TOK, nk), BF16),
                   _cache_shape(ATT_KV_HEADS, SEQ, ATT_HEAD_DIM), _cache_shape(ATT_KV_HEADS, SEQ, ATT_HEAD_DIM)],
        compiler_params=_cparams(1),
        name="proj_att",
    )(xp, xs, gain, mods, mods, w, qg, kg, _group_mean_matrix(ATT_HEAD_DIM), *tables)


def _proj_diff_kernel(h_ref, w_ref, qg_ref, kg_ref, m_ref,
                      cos_ref, sp_ref, sn_ref, q_ref, k_ref, v_ref, ck_ref, cv_ref):
    i = pl.program_id(0)
    h = h_ref[...]
    per = PROJ_UNIT // LANES
    nu = DIFF_HEADS // per

    def body(lat):
        rope = _rope_args(lat, cos_ref, sp_ref, sn_ref, DIFF_HEAD_DIM)

        def emit(u, y):
            if u < 2 * nu:
                y = _head_norm(y, m_ref, qg_ref[...] if u < nu else kg_ref[...])
            for t, yc in enumerate(_halves(y)):
                hd = (u % nu) * per + t
                if u < nu:
                    _put(q_ref, hd, _maybe_rope(yc, rope))
                elif u < 2 * nu:
                    kn = _maybe_rope(yc, rope)
                    _put(k_ref, hd, kn)
                    if not lat:
                        _cache_rows_t(ck_ref, [[hd, 0], [hd, 1]], kn)
                else:
                    _put(v_ref, hd, yc)
                    if not lat:
                        _cache_rows(cv_ref, [hd], yc)

        _matmul_units(h, w_ref, 3 * nu, PROJ_UNIT, emit)

    _by_tile_kind(i, body)


def _proj_diff(h, w, qg, kg, tables):
    n = DIFF_HEADS * 2 * DIFF_HEAD_DIM
    return pl.pallas_call(
        _proj_diff_kernel,
        grid=(N_PROJ_TILES,),
        in_specs=[_tok_spec(D_MODEL),
                  _const_spec(w.shape), _const_spec((1, PROJ_UNIT)), _const_spec((1, PROJ_UNIT)),
                  _UNIT_MAT_SPEC, _ROPE_SPEC, _ROPE_SPEC, _ROPE_SPEC],
        out_specs=[_tok_spec(n), _tok_spec(n), _tok_spec(n),
                   _cache_spec(DIFF_HEADS, 2, DIFF_HEAD_DIM, SEQ), _cache_spec(DIFF_HEADS, SEQ, 2 * DIFF_HEAD_DIM)],
        out_shape=[jax.ShapeDtypeStruct((N_TOK, n), BF16)] * 3
                  + [_cache_shape(DIFF_HEADS, 2, DIFF_HEAD_DIM, SEQ),
                     _cache_shape(DIFF_HEADS, SEQ, 2 * DIFF_HEAD_DIM)],
        compiler_params=_cparams(1),
        name="proj_diff",
    )(h, w, qg, kg, _group_mean_matrix(DIFF_HEAD_DIM), *tables)


def _dup_halves(yc):
    lo = _lane_lo(yc.shape)
    sw = pltpu.roll(yc, HALF, 1)
    return jnp.where(lo, yc, sw), jnp.where(lo, sw, yc)


def _proj_swa_kernel(h_ref, w_ref, qg_ref, kg_ref, m_ref,
                     cos_ref, sp_ref, sn_ref, q_ref, kd_ref, vd_ref, ck_ref, cv_ref):
    i = pl.program_id(0)
    h = h_ref[...]
    per = PROJ_UNIT // LANES
    nq = SWA_HEADS * SWA_HEAD_DIM // PROJ_UNIT
    nk = SWA_KV_HEADS * SWA_HEAD_DIM // PROJ_UNIT

    def body(lat):
        rope = _rope_args(lat, cos_ref, sp_ref, sn_ref, SWA_HEAD_DIM)

        def emit(u, y):
            if u < nq + nk:
                y = _head_norm(y, m_ref, qg_ref[...] if u < nq else kg_ref[...])
            for t, yc in enumerate(_halves(y)):
                if u < nq:
                    _put(q_ref, u * per + t, _maybe_rope(yc, rope))
                    continue
                if u < nq + nk:
                    j, c_ref, d_ref = (u - nq) * per + t, ck_ref, kd_ref
                    yc = _maybe_rope(yc, rope)
                else:
                    j, c_ref, d_ref = (u - nq - nk) * per + t, cv_ref, vd_ref
                for a, dup in enumerate(_dup_halves(yc)):
                    _put(d_ref, 2 * j + a, dup)
                if not lat:
                    _cache_rows_t(c_ref, [[2 * j], [2 * j + 1]], yc)

        _matmul_units(h, w_ref, nq + 2 * nk, PROJ_UNIT, emit)

    _by_tile_kind(i, body)


def _proj_swa(h, w, qg, kg, tables):
    nq, nk = SWA_HEADS * SWA_HEAD_DIM, SWA_KV_HEADS * SWA_HEAD_DIM
    return pl.pallas_call(
        _proj_swa_kernel,
        grid=(N_PROJ_TILES,),
        in_specs=[_tok_spec(D_MODEL),
                  _const_spec(w.shape), _const_spec((1, PROJ_UNIT)), _const_spec((1, PROJ_UNIT)),
                  _UNIT_MAT_SPEC, _ROPE_SPEC, _ROPE_SPEC, _ROPE_SPEC],
        out_specs=[_tok_spec(nq), _tok_spec(2 * nk), _tok_spec(2 * nk),
                   _cache_spec(SWA_KV_HEADS, SWA_HEAD_DIM, SEQ), _cache_spec(SWA_KV_HEADS, SWA_HEAD_DIM, SEQ)],
        out_shape=[jax.ShapeDtypeStruct((N_TOK, nq), BF16),
                   jax.ShapeDtypeStruct((N_TOK, 2 * nk), BF16),
                   jax.ShapeDtypeStruct((N_TOK, 2 * nk), BF16),
                   _cache_shape(SWA_KV_HEADS, SWA_HEAD_DIM, SEQ), _cache_shape(SWA_KV_HEADS, SWA_HEAD_DIM, SEQ)],
        compiler_params=_cparams(1),
        name="proj_swa",
    )(h, w, qg, kg, _group_mean_matrix(SWA_HEAD_DIM), *tables)


def _mla_lane_matrices():
    everything = lambda lane: lane >= 0
    return (_lane_sum_matrix(everything, everything),
            _lane_sum_matrix(lambda lane: lane < HALF, everything),
            _lane_sum_matrix(lambda lane: lane >= HALF, everything))


def _proj_mla_kernel(h_ref, w_in_ref, qa_ref, kva_ref, w_uq_ref,
                     qg_ref, qgp_ref, all_ref, lo_ref, hi_ref, cos_ref, sp_ref, sn_ref,
                     qn_ref, qp_ref, ckv_ref, kpe_ref, c_ckv_ref, c_kpe_ref):
    i = pl.program_id(0)
    y = _dot(h_ref[...], w_in_ref[...])
    c_q = y[:, :MLA_Q_RANK]
    c_kv = y[:, MLA_Q_RANK:MLA_Q_RANK + MLA_KV_RANK]
    kpe = y[:, MLA_Q_RANK + MLA_KV_RANK:]
    kpe_ref[...] = kpe
    ckv = c_kv * lax.rsqrt(jnp.mean(c_kv * c_kv, axis=-1, keepdims=True) + EPS) * kva_ref[...]
    ckv_ref[...] = ckv.astype(BF16)
    cq = (c_q * lax.rsqrt(jnp.mean(c_q * c_q, axis=-1, keepdims=True) + EPS) * qa_ref[...]).astype(BF16)
    lo = _lane_lo((PROJ_TM, LANES))
    inv_d = 1.0 / (MLA_NOPE + MLA_ROPE)

    def body(lat):
        if not lat:
            _cache_rows(c_ckv_ref, [], ckv)
            _cache_rows_t(c_kpe_ref, [[]], kpe)

        def emit(j, yq):
            pe = yq[:, 2 * LANES:]
            pe_sq = _sq_bf16(pe)
            rs = []
            for a, half_ref in enumerate((lo_ref, hi_ref)):
                nope = yq[:, a * LANES:(a + 1) * LANES]
                ss = _dot(_sq_bf16(nope), all_ref[...]) + _dot(pe_sq, half_ref[...])
                r = lax.rsqrt(ss * inv_d + EPS)
                rs.append(r)
                _put(qn_ref, 2 * j + a, nope * r * qg_ref[...])
            pe = pe * jnp.where(lo, rs[0], rs[1]) * qgp_ref[...]
            if lat:
                pe = _rope(pe, cos_ref[...], sp_ref[...], sn_ref[...], MLA_ROPE // 4)
            _put(qp_ref, j, pe)

        _matmul_units(cq, w_uq_ref, MLA_HEADS // 2, 3 * LANES, emit)

    _by_tile_kind(i, body)


def _proj_mla(h, w_in, qa, kva, w_uq, qg, qgp, tables):
    n_nope = MLA_HEADS * MLA_NOPE
    n_pe = MLA_HEADS * MLA_ROPE
    return pl.pallas_call(
        _proj_mla_kernel,
        grid=(N_PROJ_TILES,),
        in_specs=[_tok_spec(D_MODEL),
                  _const_spec(w_in.shape), _const_spec((1, MLA_Q_RANK)), _const_spec((1, MLA_KV_RANK)),
                  _const_spec(w_uq.shape), _const_spec((1, LANES)), _const_spec((1, LANES)),
                  _LANE_MAT_SPEC, _LANE_MAT_SPEC, _LANE_MAT_SPEC,
                  _ROPE_SPEC, _ROPE_SPEC, _ROPE_SPEC],
        out_specs=[_tok_spec(n_nope), _tok_spec(n_pe), _tok_spec(MLA_KV_RANK), _tok_spec(LANES),
                   _cache_spec(SEQ, MLA_KV_RANK), _cache_spec(MLA_ROPE, SEQ)],
        out_shape=[jax.ShapeDtypeStruct((N_TOK, n_nope), BF16),
                   jax.ShapeDtypeStruct((N_TOK, n_pe), BF16),
                   jax.ShapeDtypeStruct((N_TOK, MLA_KV_RANK), BF16),
                   jax.ShapeDtypeStruct((N_TOK, LANES), F32),
                   _cache_shape(SEQ, MLA_KV_RANK), _cache_shape(MLA_ROPE, SEQ)],
        compiler_params=_cparams(1),
        name="proj_mla",
    )(h, w_in, qa, kva, w_uq, qg, qgp, *_mla_lane_matrices(), *tables)


def _mla_expand_kernel(ckv_ref, kpe_ref, w_ref, kg_ref, kgp_ref, sum_ref, lo_ref,
                       cos_ref, sp_ref, sn_ref, kn_ref, kp_ref, v_ref, *, rope):
    i = pl.program_id(0)
    ckv = ckv_ref[...].astype(BF16)
    kpe = kpe_ref[...]
    pe_ss = _dot(_sq_bf16(kpe), lo_ref[...])
    pe_ss = jnp.concatenate([pe_ss, pe_ss], axis=1)
    lo = _lane_lo(kpe.shape)
    inv_d = 1.0 / (MLA_NOPE + MLA_ROPE)

    def body(lat):
        def emit(j, y):
            kn = jnp.concatenate([y[:, :LANES], y[:, 2 * LANES:3 * LANES]], axis=1)
            r = lax.rsqrt((_dot(_sq_bf16(kn), sum_ref[...]) + pe_ss) * inv_d + EPS)
            kn = kn * r * kg_ref[...]
            for a in range(2):
                _put(kn_ref, 2 * j + a, kn[:, a * LANES:(a + 1) * LANES])
                _put(v_ref, 2 * j + a, y[:, (2 * a + 1) * LANES:(2 * a + 2) * LANES])
            pe = kpe * jnp.where(lo, r[:, :LANES], r[:, LANES:]) * kgp_ref[...]
            if lat:
                pe = _rope(pe, cos_ref[...], sp_ref[...], sn_ref[...], MLA_ROPE // 4)
            _put(kp_ref, j, pe)

        _matmul_units(ckv, w_ref, MLA_HEADS // 2, 4 * LANES, emit)

    if rope:
        _by_tile_kind(i, body)
    else:
        body(False)


def _mla_expand(ckv, kpe_dup, w_ukv, kg, kgp, tables, rope):
    n = ckv.shape[0]
    n_nope = MLA_HEADS * MLA_NOPE
    n_pe = MLA_HEADS * MLA_ROPE
    _, m_lo, _ = _mla_lane_matrices()
    return pl.pallas_call(
        functools.partial(_mla_expand_kernel, rope=rope),
        grid=(n // PROJ_TM,),
        in_specs=[_tok_spec(MLA_KV_RANK), _tok_spec(LANES), _const_spec(w_ukv.shape),
                  _const_spec((1, PROJ_UNIT)), _const_spec((1, LANES)), _UNIT_MAT_SPEC, _LANE_MAT_SPEC,
                  _ROPE_SPEC, _ROPE_SPEC, _ROPE_SPEC],
        out_specs=[_tok_spec(n_nope), _tok_spec(n_pe), _tok_spec(n_nope)],
        out_shape=[jax.ShapeDtypeStruct((n, n_nope), BF16),
                   jax.ShapeDtypeStruct((n, n_pe), BF16),
                   jax.ShapeDtypeStruct((n, n_nope), BF16)],
        compiler_params=_cparams(1),
        name="mla_expand",
    )(ckv, kpe_dup, w_ukv, kg, kgp, _group_sum_matrix(), m_lo, *tables)


def _prompt_spec(width):
    return pl.BlockSpec((PROMPT_SEQS * TM, width), lambda b: (b, 0))


def _latq_spec(rows, width):
    per = DEC_SEQ // rows
    return pl.BlockSpec((rows, width), lambda b, t: (N_PROMPT_TOK // rows + b * per + t, 0))


def _latkv_spec(width):
    return pl.BlockSpec((DEC_SEQ, width), lambda b, t: (LAT_BLOCK0 + b, 0))


def _lato_spec(rows):
    per = DEC_SEQ // rows
    return pl.BlockSpec((rows, D_MODEL), lambda b, t: (b * per + t, 0))


def _att_kernel(*refs, with_ctx, seqs):
    if with_ctx:
        q_ref, k_ref, v_ref, kc_ref, vc_ref, o_ref = refs
    else:
        q_ref, k_ref, v_ref, o_ref = refs
    tq = q_ref.shape[0] // seqs
    nu = ATT_UNIT_HEADS
    per_kv = ATT_HEADS // ATT_KV_HEADS // nu

    def make(views):
        q_v, k_v, v_v, o_v = views

        def scores(u):
            q = jnp.concatenate([_chunk(q_v, u * nu + g) for g in range(nu)], axis=0)
            s_list = [_dot_nt(_chunk(k_v, u // per_kv), q)]
            if with_ctx:
                s_list.append(_dot_nt(kc_ref[u // per_kv].astype(BF16), q))
            return s_list

        def finish(u, s_list):
            values = [_chunk(v_v, u // per_kv)]
            if with_ctx:
                values.append(vc_ref[u // per_kv].astype(BF16))
            ps, inv = _softmax2_parts(s_list)
            o = _pv(ps, values) * inv
            for g in range(nu):
                o_v[:, (u * nu + g) * LANES:(u * nu + g + 1) * LANES] = (
                    o[:, g * tq:(g + 1) * tq].T.astype(o_v.dtype))

        return scores, finish

    _seq_pipeline((q_ref, k_ref, v_ref, o_ref), seqs, ATT_HEADS // nu, make)


def _att_attend(q, k, v, cache_k, cache_v):
    nk = ATT_KV_HEADS * ATT_HEAD_DIM
    out_p = pl.pallas_call(
        functools.partial(_att_kernel, with_ctx=False, seqs=PROMPT_SEQS),
        grid=(N_PROMPT_TILES // PROMPT_SEQS,),
        in_specs=[_prompt_spec(D_MODEL), _prompt_spec(nk), _prompt_spec(nk)],
        out_specs=_prompt_spec(D_MODEL),
        out_shape=jax.ShapeDtypeStruct((N_PROMPT_TOK, D_MODEL), BF16),
        compiler_params=_cparams(1),
        name="att_prompt",
    )(q, k, v)
    ctx = pl.BlockSpec((None, None, ATT_KV_HEADS, PAST_LEN, LANES), lambda b, t: (b, 0, 0, 0, 0))
    out_s = pl.pallas_call(
        functools.partial(_att_kernel, with_ctx=True, seqs=1),
        grid=(DEC_BATCH, TILES_PER_DEC),
        in_specs=[_latq_spec(TM, D_MODEL), _latkv_spec(nk), _latkv_spec(nk), ctx, ctx],
        out_specs=_lato_spec(TM),
        out_shape=jax.ShapeDtypeStruct((N_LAT_TOK, D_MODEL), BF16),
        compiler_params=_cparams(2),
        name="att_latent",
    )(q, k, v, cache_k, cache_v)
    return out_p, out_s


def _diff_kernel(*refs, lam_init, with_ctx, seqs):
    if with_ctx:
        (q_ref, k_ref, v_ref, kc_ref, vc_ref, lq1_ref, lk1_ref, lq2_ref, lk2_ref, sub_ref, o_ref) = refs
    else:
        (q_ref, k_ref, v_ref, lq1_ref, lk1_ref, lq2_ref, lk2_ref, sub_ref, o_ref) = refs
    tq = q_ref.shape[0] // seqs
    lam = (jnp.exp(jnp.sum(lq1_ref[...] * lk1_ref[...], axis=-1, keepdims=True))
           - jnp.exp(jnp.sum(lq2_ref[...] * lk2_ref[...], axis=-1, keepdims=True)) + lam_init)
    sub = sub_ref[...] * (1.0 - lam_init)

    def make(views):
        q_v, k_v, v_v, o_v = views

        def scores(hd):
            q = jnp.concatenate(_split_halves(_chunk(q_v, hd)), axis=0)
            s_list = [_dot_nt(_chunk(k_v, hd), q)]
            if with_ctx:
                s_list.append(_dot_nt(kc_ref[hd].astype(BF16), q))
            return s_list

        def finish(hd, s_list):
            values = [_chunk(v_v, hd)]
            if with_ctx:
                values.append(vc_ref[hd].astype(BF16))
            ps, inv = _softmax2_parts(s_list)
            o = (_pv([p[:, :tq] for p in ps], values) * inv[:, :tq]
                 - _pv([p[:, tq:] for p in ps], values) * (lam * inv[:, tq:]))
            o = o * lax.rsqrt(jnp.mean(o * o, axis=0, keepdims=True) + EPS) * sub
            o_v[:, hd * LANES:(hd + 1) * LANES] = o.T.astype(o_v.dtype)

        return scores, finish

    _seq_pipeline((q_ref, k_ref, v_ref, o_ref), seqs, DIFF_HEADS, make)


def _diff_attend(q, k, v, cache_k_pair, cache_v, lq1, lk1, lq2, lk2, subln, lam_init):
    small = [lq1, lk1, lq2, lk2, subln]
    small_specs = [_const_spec(s.shape) for s in small]
    out_p = pl.pallas_call(
        functools.partial(_diff_kernel, lam_init=lam_init, with_ctx=False, seqs=PROMPT_SEQS),
        grid=(N_PROMPT_TILES // PROMPT_SEQS,),
        in_specs=[_prompt_spec(D_MODEL)] * 3 + small_specs,
        out_specs=_prompt_spec(D_MODEL),
        out_shape=jax.ShapeDtypeStruct((N_PROMPT_TOK, D_MODEL), BF16),
        compiler_params=_cparams(1),
        name="diff_prompt",
    )(q, k, v, *small)
    out_s = pl.pallas_call(
        functools.partial(_diff_kernel, lam_init=lam_init, with_ctx=True, seqs=1),
        grid=(DEC_BATCH, TILES_PER_DEC),
        in_specs=[_latq_spec(TM, D_MODEL), _latkv_spec(D_MODEL), _latkv_spec(D_MODEL),
                  pl.BlockSpec((None, DIFF_HEADS, PAST_LEN, LANES), lambda b, t: (b, 0, 0, 0)),
                  pl.BlockSpec((None, None, DIFF_HEADS, PAST_LEN, LANES), lambda b, t: (b, 0, 0, 0, 0))]
                 + small_specs,
        out_specs=_lato_spec(TM),
        out_shape=jax.ShapeDtypeStruct((N_LAT_TOK, D_MODEL), BF16),
        compiler_params=_cparams(2),
        name="diff_latent",
    )(q, k, v, cache_k_pair, cache_v, *small)
    return out_p, out_s


def _swa_pipeline(q_ref, o_ref, seq_refs, sink_ref, score_fns, value_fns, seqs=1):
    tq = q_ref.shape[0] // seqs
    per_kv = SWA_HEADS // SWA_KV_HEADS // 2
    first = lax.broadcasted_iota(jnp.int32, (LANES, tq), 0) < HALF

    def make(views):
        q_v, o_v = views[:2]
        kv_views = views[2:]

        def scores(c):
            q = jnp.concatenate(_split_halves(_chunk(q_v, c)), axis=0)
            return [fn(kv_views, c // per_kv, q) for fn in score_fns]

        def finish(c, s_list):
            sink = jnp.concatenate([jnp.full((1, tq), sink_ref[2 * c + a] * LOG2E, F32) for a in range(2)],
                                   axis=1)
            ps, inv = _softmax2_parts(s_list, extra=sink)
            o = _pv(ps, [fn(kv_views, c // per_kv) for fn in value_fns]) * inv
            oc = jnp.where(first, o[:, :tq], o[:, tq:])
            o_v[:, c * LANES:(c + 1) * LANES] = oc.T.astype(o_v.dtype)

        return scores, finish

    _seq_pipeline((q_ref, o_ref) + tuple(seq_refs), seqs, SWA_HEADS // 2, make)


def _swa_prompt_kernel(sink_ref, q_ref, k_ref, v_ref, o_ref):
    _swa_pipeline(q_ref, o_ref, (k_ref, v_ref), sink_ref,
                  [lambda kv_v, kv, q: _dot_nt(_chunk(kv_v[0], kv), q)],
                  [lambda kv_v, kv: _chunk(kv_v[1], kv)], seqs=PROMPT_SEQS)


def _swa_latent_kernel(sink_ref, q_ref, k_ref, v_ref, kc_ref, vc_ref, o_ref):
    n = pl.program_id(1)
    tq = q_ref.shape[0]
    span = 3 * SWA_QB
    start = pl.multiple_of(jnp.clip((n - 1) * SWA_QB, 0, DEC_SEQ - span), SWA_QB)
    cols = lax.broadcasted_iota(jnp.int32, (span, 2 * tq), 1)
    qpos = n * SWA_QB + jnp.bitwise_and(cols, tq - 1)
    kpos = start + lax.broadcasted_iota(jnp.int32, (span, 2 * tq), 0)
    valid = jnp.abs(qpos - kpos) <= WINDOW

    def local(ref, kv):
        return ref[pl.ds(start, span), kv * LANES:(kv + 1) * LANES]

    _swa_pipeline(q_ref, o_ref, (), sink_ref,
                  [lambda _, kv, q: jnp.where(valid, _dot_nt(local(k_ref, kv), q), -1e30),
                   lambda _, kv, q: _dot_nt(kc_ref[kv], q)],
                  [lambda _, kv: local(v_ref, kv), lambda _, kv: vc_ref[kv]])


def _swa_attend(q, kd, vd, cache_kd, cache_vd, sink):
    nkd = 2 * SWA_KV_HEADS * SWA_HEAD_DIM
    smem = pl.BlockSpec(memory_space=pltpu.SMEM)
    out_p = pl.pallas_call(
        _swa_prompt_kernel,
        grid=(N_PROMPT_TILES // PROMPT_SEQS,),
        in_specs=[smem, _prompt_spec(D_MODEL), _prompt_spec(nkd), _prompt_spec(nkd)],
        out_specs=_prompt_spec(D_MODEL),
        out_shape=jax.ShapeDtypeStruct((N_PROMPT_TOK, D_MODEL), BF16),
        compiler_params=_cparams(1),
        name="swa_prompt",
    )(sink, q, kd, vd)
    ctx = pl.BlockSpec((None, SWA_KV_HEADS, PAST_LEN, LANES), lambda b, n: (b, 0, 0, 0))
    out_s = pl.pallas_call(
        _swa_latent_kernel,
        grid=(DEC_BATCH, DEC_SEQ // SWA_QB),
        in_specs=[smem, _latq_spec(SWA_QB, D_MODEL), _latkv_spec(nkd), _latkv_spec(nkd), ctx, ctx],
        out_specs=_lato_spec(SWA_QB),
        out_shape=jax.ShapeDtypeStruct((N_LAT_TOK, D_MODEL), BF16),
        compiler_params=_cparams(2),
        name="swa_latent",
    )(sink, q, kd, vd, cache_kd, cache_vd)
    return out_p, out_s


def _mla_kernel(*refs, with_ctx, seqs):
    if with_ctx:
        (qn_ref, qp_ref, kn_ref, kp_ref, v_ref, knc_ref, kpc_ref, vc_ref, o_ref) = refs
    else:
        (qn_ref, qp_ref, kn_ref, kp_ref, v_ref, o_ref) = refs

    def make(views):
        qn_v, qp_v, kn_v, kp_v, v_v, o_v = views

        def scores(hd):
            j, a = hd // 2, hd % 2
            q = jnp.concatenate([_chunk(qn_v, hd), _split_halves(_chunk(qp_v, j))[a]], axis=1)
            s_list = [_dot_nt(jnp.concatenate([_chunk(kn_v, hd), _chunk(kp_v, j)], axis=1), q)]
            if with_ctx:
                s_list.append(_dot_nt(jnp.concatenate([_chunk(knc_ref, hd), _chunk(kpc_ref, j)], axis=1), q))
            return s_list

        def finish(hd, s_list):
            values = [_chunk(v_v, hd)]
            if with_ctx:
                values.append(_chunk(vc_ref, hd))
            ps, inv = _softmax2_parts(s_list)
            o_v[:, hd * LANES:(hd + 1) * LANES] = (_pv(ps, values) * inv).T.astype(o_v.dtype)

        return scores, finish

    _seq_pipeline((qn_ref, qp_ref, kn_ref, kp_ref, v_ref, o_ref), seqs, MLA_HEADS, make)


def _mla_attend(qn, qp, kn, kp, v, knc, kpc, vc):
    n_pe = MLA_HEADS * MLA_ROPE
    out_p = pl.pallas_call(
        functools.partial(_mla_kernel, with_ctx=False, seqs=PROMPT_SEQS),
        grid=(N_PROMPT_TILES // PROMPT_SEQS,),
        in_specs=[_prompt_spec(D_MODEL), _prompt_spec(n_pe), _prompt_spec(D_MODEL), _prompt_spec(n_pe),
                  _prompt_spec(D_MODEL)],
        out_specs=_prompt_spec(D_MODEL),
        out_shape=jax.ShapeDtypeStruct((N_PROMPT_TOK, D_MODEL), BF16),
        compiler_params=_cparams(1),
        name="mla_prompt",
    )(qn, qp, kn, kp, v)

    def ctx(width):
        return pl.BlockSpec((PAST_LEN, width), lambda b, t: (b, 0))

    out_s = pl.pallas_call(
        functools.partial(_mla_kernel, with_ctx=True, seqs=1),
        grid=(DEC_BATCH, TILES_PER_DEC),
        in_specs=[_latq_spec(TM, D_MODEL), _latq_spec(TM, n_pe),
                  _latkv_spec(D_MODEL), _latkv_spec(n_pe), _latkv_spec(D_MODEL),
                  ctx(D_MODEL), ctx(n_pe), ctx(D_MODEL)],
        out_specs=_lato_spec(TM),
        out_shape=jax.ShapeDtypeStruct((N_LAT_TOK, D_MODEL), BF16),
        compiler_params=_cparams(2),
        name="mla_latent",
    )(qn, qp, kn, kp, v, knc, kpc, vc)
    return out_p, out_s


def _omlp_kernel(*refs, first, last):
    refs = list(refs)
    ap_ref, as_ref, wo_ref = refs[:3]
    x_refs = refs[3:5] if first else refs[3:4]
    refs = refs[3 + len(x_refs):]
    g1_ref, gain_ref, sh_ref, sc_ref, g2_ref, w1c_ref, w2c_ref = refs[:7]
    refs = refs[7:]
    if last:
        op_ref, os_ref, wo_s, w1_s, w2_s = refs
    else:
        ngain_ref, nsh_ref, nsc_ref, o_ref, hn_ref, wo_s, w1_s, w2_s = refs
    s = pl.program_id(0)
    per = MLP_FF_CHUNK // MLP_LOAD_COLS
    n_chunks = D_FF // MLP_FF_CHUNK
    half = MLP_TM // 2

    @pl.when(s == 0)
    def _():
        wo_s[...] = wo_ref[...].astype(BF16)

    for part in range(per):
        @pl.when((s < N_LOAD_STEPS) & (s % per == part))
        def _(part=part):
            w1_s[s // per, :, part * MLP_LOAD_COLS:(part + 1) * MLP_LOAD_COLS] = w1c_ref[...].astype(BF16)

    @pl.when(s < N_LOAD_STEPS)
    def _():
        w2_s[s // per, pl.ds(pl.multiple_of((s % per) * MLP_LOAD_COLS, MLP_LOAD_COLS), MLP_LOAD_COLS), :] = (
            w2c_ref[...].astype(BF16))

    @pl.when(s >= N_LOAD_STEPS)
    def _():
        t = s - N_LOAD_STEPS
        is_prompt = t < N_MLP_PROMPT_TILES
        grp = _tile_group(t, MLP_TM)

        def mod(ref):
            return ref[pl.ds(grp, 1), :]

        rows = [slice(r * half, (r + 1) * half) for r in range(2)]
        o = [_dot(jnp.where(is_prompt, ap_ref[rw, :], as_ref[rw, :]), wo_s[...]) for rw in rows]
        x1, h, u0 = [], [], []
        for r, rw in enumerate(rows):
            x = jnp.where(is_prompt, x_refs[0][rw, :], x_refs[1][rw, :]) if first else x_refs[0][rw, :]
            x1.append(x + mod(g1_ref) * o[r])
            h.append(_norm_mod(x1[r], gain_ref[...], mod(sh_ref), mod(sc_ref)).astype(BF16))
            u0.append(_dot(h[r], w1_s[0]))
        h = jnp.concatenate(h, axis=0)
        acc = []

        def up(c):
            return jnp.concatenate(u0, axis=0) if c == 0 else _dot(h, w1_s[c])

        def down(c, u):
            u = jnp.square(jnp.maximum(u, 0.0)).astype(BF16)
            if c + 1 < n_chunks:
                y = _dot(u, w2_s[c])
                acc[:] = [y if not acc else acc[0] + y]
            else:
                acc[:] = [acc[0][rw] + _dot(u[rw], w2_s[c]) for rw in rows]

        _head_pipeline(n_chunks, up, down)
        for r, rw in enumerate(rows):
            out = x1[r] + mod(g2_ref) * acc[r]
            if last:
                @pl.when(is_prompt)
                def _(out=out, rw=rw):
                    op_ref[rw, :] = out

                @pl.when(jnp.logical_not(is_prompt))
                def _(out=out, rw=rw):
                    os_ref[rw, :] = out
            else:
                o_ref[rw, :] = out
                hn_ref[rw, :] = _norm_mod(out, ngain_ref[...], mod(nsh_ref), mod(nsc_ref)).astype(BF16)


def _omlp(attn_p, attn_s, w_o, x, mods, gain_ffn, w1_all, w2_all, layer, next_gain):
    first, last = layer == 0, next_gain is None
    n_lat_tiles = N_LAT_TOK // MLP_TM

    def tok(s):
        return jnp.maximum(s - N_LOAD_STEPS, 0)

    p_spec = pl.BlockSpec((MLP_TM, D_MODEL), lambda s: (jnp.minimum(tok(s), N_MLP_PROMPT_TILES - 1), 0))
    l_spec = pl.BlockSpec((MLP_TM, D_MODEL),
                          lambda s: (jnp.clip(tok(s) - N_MLP_PROMPT_TILES, 0, n_lat_tiles - 1), 0))
    w1_spec = pl.BlockSpec((None, D_MODEL, MLP_LOAD_COLS),
                           lambda s: (layer, 0, jnp.minimum(s, N_LOAD_STEPS - 1)))
    w2_spec = pl.BlockSpec((None, MLP_LOAD_COLS, D_MODEL),
                           lambda s: (layer, jnp.minimum(s, N_LOAD_STEPS - 1), 0))
    t_spec = pl.BlockSpec((MLP_TM, D_MODEL), lambda s: (tok(s), 0))
    n_chunks = D_FF // MLP_FF_CHUNK
    split = ([p_spec, l_spec], [jax.ShapeDtypeStruct((N_PROMPT_TOK, D_MODEL), F32),
                                jax.ShapeDtypeStruct((N_LAT_TOK, D_MODEL), F32)])
    in_specs = ([p_spec, l_spec, _const_spec(w_o.shape)] + (split[0] if first else [t_spec])
                + [_mod_spec(layer, 2), _const_spec((1, D_MODEL)), _mod_spec(layer, 3), _mod_spec(layer, 4),
                   _mod_spec(layer, 5),
                   w1_spec, w2_spec])
    args = ([attn_p, attn_s, w_o] + (list(x) if first else [x])
            + [mods, gain_ffn, mods, mods, mods, w1_all, w2_all])
    if last:
        out_specs, out_shape = split
    else:
        in_specs += [_const_spec((1, D_MODEL)), _mod_spec(layer + 1, 0), _mod_spec(layer + 1, 1)]
        args += [next_gain, mods, mods]
        out_specs = [t_spec, t_spec]
        out_shape = [jax.ShapeDtypeStruct((N_TOK, D_MODEL), F32), jax.ShapeDtypeStruct((N_TOK, D_MODEL), BF16)]
    return pl.pallas_call(
        functools.partial(_omlp_kernel, first=first, last=last),
        grid=(N_LOAD_STEPS + N_TOK // MLP_TM,),
        in_specs=in_specs,
        out_specs=out_specs,
        out_shape=out_shape,
        scratch_shapes=[pltpu.VMEM((D_MODEL, D_MODEL), BF16),
                        pltpu.VMEM((n_chunks, D_MODEL, MLP_FF_CHUNK), BF16),
                        pltpu.VMEM((n_chunks, MLP_FF_CHUNK, D_MODEL), BF16)],
        compiler_params=_cparams(1),
        name="omlp",
    )(*args)


def _row(v, scale=1.0):
    return (v.astype(F32) * scale).reshape(1, -1)


def _pair(v, scale=1.0):
    return (jnp.concatenate([v, v]).astype(F32) * scale).reshape(1, LANES)


def _unit_gain(v, scale=1.0):
    return (jnp.tile(v.astype(F32), PROJ_UNIT // v.shape[0]) * scale).reshape(1, PROJ_UNIT)


def kernel(x_prompt, x_sample, cache_att_k, cache_att_v, cache_diff_k, cache_diff_v, cache_swa_k, cache_swa_v, cache_mla_ckv, cache_mla_kpe, c, c_ctx, ada_w, ada_b, norm_mix, norm_ffn, att_w_qkv, att_q_norm, att_k_norm, att_w_o, diff_w_qkv, diff_q_norm, diff_k_norm, diff_lq1, diff_lk1, diff_lq2, diff_lk2, diff_subln, diff_w_o, swa_w_qkv, swa_q_norm, swa_k_norm, swa_sink, swa_w_o, mla_w_in, mla_q_a_norm, mla_kv_a_norm, mla_w_uq, mla_w_ukv, mla_q_norm, mla_k_norm, mla_w_o, mlp_w1, mlp_w2):
    xp = x_prompt.reshape(N_PROMPT_TOK, D_MODEL)
    xs = x_sample.reshape(N_LAT_TOK, D_MODEL)
    cond = jnp.concatenate([c_ctx[None], c, jnp.zeros((COND_ROWS - 1 - DEC_BATCH, D_MODEL), F32)], axis=0)
    mods_all = _modulation(cond, ada_w, ada_b)

    tab_att = _rope_tables(ATT_HEAD_DIM)
    tab_64 = _rope_tables(DIFF_HEAD_DIM)

    outs = {}
    x = (xp, xs)
    for layer in range(DEPTH):
        gain_ffn = _row(norm_ffn[layer])
        if layer == 0:
            qs = ATT_HEAD_DIM ** -0.5 * LOG2E
            q, k, v, outs["att_k"], outs["att_v"] = _proj_att(
                xp, xs, mods_all, _row(norm_mix[layer]), att_w_qkv[0].astype(BF16),
                _unit_gain(att_q_norm[0], qs), _unit_gain(att_k_norm[0]), tab_att)
            attn_p, attn_s = _att_attend(q, k, v, cache_att_k, cache_att_v)
            w_o = att_w_o[0]
        elif layer == 1:
            qs = DIFF_HEAD_DIM ** -0.5 * LOG2E
            q, k, v, outs["diff_k"], outs["diff_v"] = _proj_diff(
                h, diff_w_qkv[0].astype(BF16),
                _unit_gain(diff_q_norm[0], qs), _unit_gain(diff_k_norm[0]), tab_64)
            lam_init = 0.8 - 0.6 * math.exp(-0.3 * layer)
            ck = cache_diff_k[:, 0].transpose(0, 1, 3, 2, 4).reshape(
                DEC_BATCH, DIFF_HEADS, PAST_LEN, LANES)
            attn_p, attn_s = _diff_attend(q, k, v, ck, cache_diff_v,
                                          _row(diff_lq1[0]), _row(diff_lk1[0]),
                                          _row(diff_lq2[0]), _row(diff_lk2[0]),
                                          diff_subln[0].astype(F32).reshape(LANES, 1), lam_init)
            w_o = diff_w_o[0]
        elif layer == 2:
            qs = SWA_HEAD_DIM ** -0.5 * LOG2E
            q, kd, vd, outs["swa_k"], outs["swa_v"] = _proj_swa(
                h, swa_w_qkv[0].astype(BF16),
                _unit_gain(swa_q_norm[0], qs), _unit_gain(swa_k_norm[0]), tab_64)
            ckd = jnp.concatenate([cache_swa_k[:, 0]] * 2, axis=-1).astype(BF16)
            cvd = jnp.concatenate([cache_swa_v[:, 0]] * 2, axis=-1).astype(BF16)
            attn_p, attn_s = _swa_attend(q, kd, vd, ckd, cvd, swa_sink[0].astype(F32))
            w_o = swa_w_o[0]
        else:
            qs = (MLA_NOPE + MLA_ROPE) ** -0.5 * LOG2E
            w_in = mla_w_in[0]
            w_in = jnp.concatenate([w_in, w_in[:, -MLA_ROPE:]], axis=1).astype(BF16)
            w_uq = mla_w_uq[0].reshape(MLA_Q_RANK, MLA_HEADS // 2, 2, MLA_NOPE + MLA_ROPE)
            w_uq = jnp.concatenate([w_uq[..., :MLA_NOPE].reshape(MLA_Q_RANK, MLA_HEADS // 2, 2 * MLA_NOPE),
                                    w_uq[..., MLA_NOPE:].reshape(MLA_Q_RANK, MLA_HEADS // 2, 2 * MLA_ROPE)],
                                   axis=-1).reshape(MLA_Q_RANK, -1).astype(BF16)
            w_ukv = mla_w_ukv[0].astype(BF16)
            qg, kg = mla_q_norm[0], mla_k_norm[0]
            qn, qp, ckv, kpe, outs["mla_ckv"], outs["mla_kpe"] = _proj_mla(
                h, w_in, _row(mla_q_a_norm[0]), _row(mla_kv_a_norm[0]), w_uq,
                _row(qg[:MLA_NOPE], qs), _pair(qg[MLA_NOPE:], qs), tab_64)
            kn, kp, vv = _mla_expand(ckv, kpe, w_ukv, _unit_gain(kg[:MLA_NOPE]), _pair(kg[MLA_NOPE:]),
                                     tab_64, True)
            c_ckv = cache_mla_ckv[:, 0].reshape(DEC_BATCH * PAST_LEN, MLA_KV_RANK)
            c_kpe = cache_mla_kpe[:, 0].reshape(DEC_BATCH * PAST_LEN, MLA_ROPE)
            c_kpe = jnp.concatenate([c_kpe, c_kpe], axis=-1)
            knc, kpc, vc = _mla_expand(c_ckv, c_kpe, w_ukv, _unit_gain(kg[:MLA_NOPE]), _pair(kg[MLA_NOPE:]),
                                       tab_64, False)
            attn_p, attn_s = _mla_attend(qn, qp, kn, kp, vv, knc, kpc, vc)
            w_o = mla_w_o[0]
        if layer + 1 < DEPTH:
            x, h = _omlp(attn_p, attn_s, w_o, x, mods_all, gain_ffn, mlp_w1, mlp_w2, layer,
                         _row(norm_mix[layer + 1]))
        else:
            xp, xs = _omlp(attn_p, attn_s, w_o, x, mods_all, gain_ffn, mlp_w1, mlp_w2, layer, None)

    y_prompt = xp.reshape(BATCH, SEQ, D_MODEL)
    y_sample = xs.reshape(DEC_BATCH, DEC_SEQ, D_MODEL)
    for name in ("diff_k", "swa_k", "swa_v", "mla_kpe"):
        outs[name] = jnp.swapaxes(outs[name], -1, -2)
    return (y_prompt, y_sample, outs["att_k"], outs["att_v"], outs["diff_k"], outs["diff_v"],
            outs["swa_k"], outs["swa_v"], outs["mla_ckv"], outs["mla_kpe"])
```

```python
import functools
import math

import numpy as np
import jax
import jax.numpy as jnp
from jax import lax
from jax.experimental import pallas as pl
from jax.experimental.pallas import tpu as pltpu

D_MODEL = 1024
BATCH = 16
SEQ = 256
DEPTH = 4
DEC_BATCH = 2
DEC_SEQ = 1024
PAST_LEN = 256
GRID_W = 64
ROPE_THETA = 10000.0
EPS = 1e-6
D_FF = 4 * D_MODEL
MOD_CHUNKS = 6
LOG2E = 1.4426950408889634

ATT_HEADS, ATT_KV_HEADS, ATT_HEAD_DIM = 8, 2, 128
DIFF_HEADS, DIFF_HEAD_DIM = 8, 64
SWA_HEADS, SWA_KV_HEADS, SWA_HEAD_DIM, WINDOW = 16, 4, 64, 128
MLA_HEADS, MLA_NOPE, MLA_ROPE, MLA_VDIM = 8, 128, 64, 128
MLA_Q_RANK, MLA_KV_RANK = 512, 256

LANES = 128
HALF = LANES // 2
TM = 256
N_PROMPT_TOK = BATCH * SEQ
N_LAT_TOK = DEC_BATCH * DEC_SEQ
N_TOK = N_PROMPT_TOK + N_LAT_TOK
N_PROMPT_TILES = N_PROMPT_TOK // TM
TILES_PER_DEC = DEC_SEQ // TM
LAT_BLOCK0 = N_PROMPT_TOK // DEC_SEQ
COND_ROWS = 8
PROJ_TM = 512
PROJ_BATCHES = PROJ_TM // SEQ
N_PROJ_TILES = N_TOK // PROJ_TM
N_PROJ_PROMPT = N_PROMPT_TOK // PROJ_TM
PROJ_UNIT = 2 * LANES
MLP_TM = 512
MLP_FF_CHUNK = 512
MLP_LOAD_COLS = 256
N_LOAD_STEPS = D_FF // MLP_LOAD_COLS
N_MLP_PROMPT_TILES = N_PROMPT_TOK // MLP_TM
SWA_QB = 128
ATT_UNIT_HEADS = 4
PROMPT_SEQS = 4
VMEM_LIMIT = 56 * 1024 * 1024

F32 = jnp.float32
BF16 = jnp.bfloat16


def _cparams(n_axes):
    return pltpu.CompilerParams(dimension_semantics=("arbitrary",) * n_axes,
                                vmem_limit_bytes=VMEM_LIMIT)


def _dot(a, b):
    return jnp.dot(a, b, preferred_element_type=F32)


def _dot_nt(a, b):
    return lax.dot_general(a, b, (((1,), (1,)), ((), ())), preferred_element_type=F32)


def _dot_tn(a, b):
    return lax.dot_general(a, b, (((0,), (0,)), ((), ())), preferred_element_type=F32)


def _const_spec(shape):
    nd = len(shape)
    return pl.BlockSpec(shape, lambda *_: (0,) * nd, pipeline_mode=pl.Buffered(1))


def _chunk(ref, c, width=LANES):
    return ref[:, c * width:(c + 1) * width]


def _put(ref, c, val):
    ref[:, c * LANES:(c + 1) * LANES] = val.astype(ref.dtype)


def _tile_group(i, rows):
    n_prompt = N_PROMPT_TOK // rows
    return jnp.where(i < n_prompt, 0, 1 + (i - n_prompt) // (DEC_SEQ // rows))


def _rope_tile(i):
    return jnp.maximum(i - N_PROJ_PROMPT, 0) % (DEC_SEQ // PROJ_TM)


def _norm_mod(x, gain, shift, scale):
    ms = jnp.mean(x * x, axis=-1, keepdims=True)
    return x * lax.rsqrt(ms + EPS) * (gain * (1.0 + scale)) + shift


def _lane_lo(shape):
    return lax.broadcasted_iota(jnp.int32, shape, len(shape) - 1) < HALF


def _rope(y, cos, sin_prev, sin_next, quarter):
    return (y * cos + pltpu.roll(y, quarter, 1) * sin_prev
            + pltpu.roll(y, LANES - quarter, 1) * sin_next)


def _rope_tables(rot_dim):
    half = rot_dim // 2
    quarter = rot_dim // 4
    inv = np.float32(ROPE_THETA) ** (-np.arange(0, half, 2, dtype=np.float32) / np.float32(half))
    pos = np.arange(DEC_SEQ)
    row = (pos // GRID_W).astype(np.float32)
    col = (pos % GRID_W).astype(np.float32)
    lane = np.arange(LANES)
    dd = lane % rot_dim
    q = dd // quarter
    f = dd % quarter
    ang = np.where((q < 2)[None, :], row[:, None], col[:, None]) * inv[f][None, :]
    ang = ang.astype(np.float32)
    cos = np.cos(ang).astype(np.float32)
    sin = np.sin(ang).astype(np.float32)
    odd = (q % 2 == 1)[None, :]
    sin_prev = np.where(odd, sin, 0.0).astype(np.float32)
    sin_next = np.where(odd, 0.0, -sin).astype(np.float32)
    return jnp.asarray(cos), jnp.asarray(sin_prev), jnp.asarray(sin_next)


def _lane_sum_matrix(rows, cols, value=1.0):
    lane = np.arange(LANES)
    m = np.where(rows(lane)[:, None] & cols(lane)[None, :], value, 0.0).astype(np.float32)
    return jnp.asarray(m, dtype=BF16)


def _group_mean_matrix(group):
    lane = np.arange(PROJ_UNIT)
    m = np.where((lane[:, None] // group) == (lane[None, :] // group), 1.0 / group, 0.0)
    return jnp.asarray(m.astype(np.float32), dtype=BF16)


def _group_sum_matrix():
    lane = np.arange(PROJ_UNIT)
    m = np.where((lane[:, None] // LANES) == (lane[None, :] // LANES), 1.0, 0.0)
    return jnp.asarray(m.astype(np.float32), dtype=BF16)


def _sq_bf16(y):
    return (y * y).astype(BF16)


def _head_norm(y, m_ref, gain):
    return y * lax.rsqrt(_dot(_sq_bf16(y), m_ref[...]) + EPS) * gain


def _halves(y):
    return [y[:, t * LANES:(t + 1) * LANES] for t in range(y.shape[1] // LANES)]


def _matmul_units(h, w_ref, n_units, width, emit):
    def unit(u):
        return _dot(h, w_ref[:, u * width:(u + 1) * width])

    nxt = unit(0)
    for u in range(n_units):
        cur = nxt
        if u + 1 < n_units:
            nxt = unit(u + 1)
        emit(u, cur)


def _cast_once(i, w_ref, w_s):
    @pl.when(i == 0)
    def _():
        w_s[...] = w_ref[...].astype(BF16)


def _by_tile_kind(i, body):
    pl.when(i < N_PROJ_PROMPT)(functools.partial(body, False))
    pl.when(i >= N_PROJ_PROMPT)(functools.partial(body, True))


def _rope_args(lat, cos_ref, sp_ref, sn_ref, rot_dim):
    return (cos_ref[...], sp_ref[...], sn_ref[...], rot_dim // 4) if lat else None


def _maybe_rope(y, rope):
    return y if rope is None else _rope(y, *rope)


def _cache_rows(ref, index, val):
    for b in range(PROJ_BATCHES):
        ref[(b, 0) + tuple(index)] = val[b * SEQ:(b + 1) * SEQ]


def _cache_rows_t(ref, indices, val):
    for b in range(PROJ_BATCHES):
        t = val[b * SEQ:(b + 1) * SEQ].T
        for j, index in enumerate(indices):
            ref[(b, 0) + tuple(index)] = t[j * HALF:(j + 1) * HALF]


def _softmax2_parts(s_list, extra=None):
    m = jnp.max(s_list[0], axis=0, keepdims=True)
    for s in s_list[1:]:
        m = jnp.maximum(m, jnp.max(s, axis=0, keepdims=True))
    if extra is not None:
        m = jnp.maximum(m, extra)
    ps = [jnp.exp2(s - m) for s in s_list]
    mass = ps[0].sum(axis=0, keepdims=True)
    for p in ps[1:]:
        mass = mass + p.sum(axis=0, keepdims=True)
    if extra is not None:
        mass = mass + jnp.exp2(extra - m)
    return [p.astype(BF16) for p in ps], 1.0 / mass


def _head_pipeline(n, scores, finish):
    nxt = scores(0)
    for h in range(n):
        cur = nxt
        if h + 1 < n:
            nxt = scores(h + 1)
        finish(h, cur)


def _seq_pipeline(refs, seqs, n, make):
    fns = []
    for b in range(seqs):
        views = [r.at[b * (r.shape[0] // seqs):(b + 1) * (r.shape[0] // seqs)] for r in refs]
        fns.append(make(views))
    _head_pipeline(seqs * n, lambda i: fns[i // n][0](i % n), lambda i, s: fns[i // n][1](i % n, s))


def _pv(ps, values):
    o = None
    for p, v in zip(ps, values):
        t = _dot_tn(v, p)
        o = t if o is None else o + t
    return o


def _split_halves(q):
    lo = _lane_lo(q.shape)
    zero = jnp.zeros_like(q)
    return jnp.where(lo, q, zero), jnp.where(lo, zero, q)


def _mod_kernel(cond_ref, w_ref, b_ref, o_ref):
    c = cond_ref[...]
    s = (c * jax.nn.sigmoid(c)).astype(BF16)
    o_ref[0] = _dot(s, w_ref[0].astype(BF16)) + b_ref[0]


def _modulation(cond, ada_w, ada_b):
    tn = 1536
    n = MOD_CHUNKS * D_MODEL
    return pl.pallas_call(
        _mod_kernel,
        grid=(DEPTH, n // tn),
        in_specs=[
            pl.BlockSpec((COND_ROWS, D_MODEL), lambda l, j: (0, 0)),
            pl.BlockSpec((1, D_MODEL, tn), lambda l, j: (l, 0, j)),
            pl.BlockSpec((1, 1, tn), lambda l, j: (l, 0, j)),
        ],
        out_specs=pl.BlockSpec((1, COND_ROWS, tn), lambda l, j: (l, 0, j)),
        out_shape=jax.ShapeDtypeStruct((DEPTH, COND_ROWS, n), F32),
        compiler_params=_cparams(2),
        name="modulation",
    )(cond, ada_w, ada_b.reshape(DEPTH, 1, n))


def _mod_spec(layer, chunk):
    return pl.BlockSpec((None, COND_ROWS, D_MODEL), lambda i: (layer, 0, chunk))


def _mod_row(ref, i):
    return ref[pl.ds(_tile_group(i, PROJ_TM), 1), :]


_ROPE_SPEC = pl.BlockSpec((PROJ_TM, LANES), lambda i: (_rope_tile(i), 0))
_LANE_MAT_SPEC = _const_spec((LANES, LANES))
_UNIT_MAT_SPEC = _const_spec((PROJ_UNIT, PROJ_UNIT))


def _tok_spec(width):
    return pl.BlockSpec((PROJ_TM, width), lambda i: (i, 0))


_XP_SPEC = pl.BlockSpec((PROJ_TM, D_MODEL), lambda i: (jnp.minimum(i, N_PROJ_PROMPT - 1), 0))
_XS_SPEC = pl.BlockSpec((PROJ_TM, D_MODEL), lambda i: (jnp.maximum(i - N_PROJ_PROMPT, 0), 0))


def _cache_spec(*dims):
    nd = len(dims)
    return pl.BlockSpec((PROJ_BATCHES, 1) + dims,
                        lambda i: (jnp.minimum(i, N_PROJ_PROMPT - 1), 0) + (0,) * nd)


def _cache_shape(*dims):
    return jax.ShapeDtypeStruct((BATCH, 1) + dims, F32)


def _proj_att_kernel(xp_ref, xs_ref, gain_ref, sh_ref, sc_ref, w_ref, qg_ref, kg_ref, m_ref,
                     cos_ref, sp_ref, sn_ref, q_ref, k_ref, v_ref, ck_ref, cv_ref, w_s):
    i = pl.program_id(0)
    _cast_once(i, w_ref, w_s)
    x = jnp.where(i < N_PROJ_PROMPT, xp_ref[...], xs_ref[...])
    h = _norm_mod(x, gain_ref[...], _mod_row(sh_ref, i), _mod_row(sc_ref, i)).astype(BF16)
    per = PROJ_UNIT // LANES
    nq, nk = ATT_HEADS // per, ATT_KV_HEADS // per

    def body(lat):
        rope = _rope_args(lat, cos_ref, sp_ref, sn_ref, ATT_HEAD_DIM)

        def emit(u, y):
            if u < nq + nk:
                y = _head_norm(y, m_ref, qg_ref[...] if u < nq else kg_ref[...])
            for t, yc in enumerate(_halves(y)):
                if u < nq:
                    _put(q_ref, u * per + t, _maybe_rope(yc, rope))
                elif u < nq + nk:
                    kn = _maybe_rope(yc, rope)
                    _put(k_ref, (u - nq) * per + t, kn)
                    if not lat:
                        _cache_rows(ck_ref, [(u - nq) * per + t], kn)
                else:
                    _put(v_ref, (u - nq - nk) * per + t, yc)
                    if not lat:
                        _cache_rows(cv_ref, [(u - nq - nk) * per + t], yc)

        _matmul_units(h, w_s, nq + 2 * nk, PROJ_UNIT, emit)

    _by_tile_kind(i, body)


def _proj_att(xp, xs, mods, gain, w, qg, kg, tables):
    nq, nk = ATT_HEADS * ATT_HEAD_DIM, ATT_KV_HEADS * ATT_HEAD_DIM
    return pl.pallas_call(
        _proj_att_kernel,
        grid=(N_PROJ_TILES,),
        in_specs=[_XP_SPEC, _XS_SPEC, _const_spec((1, D_MODEL)), _mod_spec(0, 0), _mod_spec(0, 1),
                  _const_spec(w.shape), _const_spec((1, PROJ_UNIT)), _const_spec((1, PROJ_UNIT)),
                  _UNIT_MAT_SPEC, _ROPE_SPEC, _ROPE_SPEC, _ROPE_SPEC],
        out_specs=[_tok_spec(nq), _tok_spec(nk), _tok_spec(nk),
                   _cache_spec(ATT_KV_HEADS, SEQ, ATT_HEAD_DIM), _cache_spec(ATT_KV_HEADS, SEQ, ATT_HEAD_DIM)],
        out_shape=[jax.ShapeDtypeStruct((N_TOK, nq), BF16),
                   jax.ShapeDtypeStruct((N_TOK, nk), BF16),
                   jax.ShapeDtypeStruct((N_TOK, nk), BF16),
                   _cache_shape(ATT_KV_HEADS, SEQ, ATT_HEAD_DIM), _cache_shape(ATT_KV_HEADS, SEQ, ATT_HEAD_DIM)],
        scratch_shapes=[pltpu.VMEM(w.shape, BF16)],
        compiler_params=_cparams(1),
        name="proj_att",
    )(xp, xs, gain, mods, mods, w, qg, kg, _group_mean_matrix(ATT_HEAD_DIM), *tables)


def _proj_diff_kernel(h_ref, w_ref, qg_ref, kg_ref, m_ref,
                      cos_ref, sp_ref, sn_ref, q_ref, k_ref, v_ref, ck_ref, cv_ref, w_s):
    i = pl.program_id(0)
    _cast_once(i, w_ref, w_s)
    h = h_ref[...]
    per = PROJ_UNIT // LANES
    nu = DIFF_HEADS // per

    def body(lat):
        rope = _rope_args(lat, cos_ref, sp_ref, sn_ref, DIFF_HEAD_DIM)

        def emit(u, y):
            if u < 2 * nu:
                y = _head_norm(y, m_ref, qg_ref[...] if u < nu else kg_ref[...])
            for t, yc in enumerate(_halves(y)):
                hd = (u % nu) * per + t
                if u < nu:
                    _put(q_ref, hd, _maybe_rope(yc, rope))
                elif u < 2 * nu:
                    kn = _maybe_rope(yc, rope)
                    _put(k_ref, hd, kn)
                    if not lat:
                        _cache_rows_t(ck_ref, [[hd, 0], [hd, 1]], kn)
                else:
                    _put(v_ref, hd, yc)
                    if not lat:
                        _cache_rows(cv_ref, [hd], yc)

        _matmul_units(h, w_s, 3 * nu, PROJ_UNIT, emit)

    _by_tile_kind(i, body)


def _proj_diff(h, w, qg, kg, tables):
    n = DIFF_HEADS * 2 * DIFF_HEAD_DIM
    return pl.pallas_call(
        _proj_diff_kernel,
        grid=(N_PROJ_TILES,),
        in_specs=[_tok_spec(D_MODEL),
                  _const_spec(w.shape), _const_spec((1, PROJ_UNIT)), _const_spec((1, PROJ_UNIT)),
                  _UNIT_MAT_SPEC, _ROPE_SPEC, _ROPE_SPEC, _ROPE_SPEC],
        out_specs=[_tok_spec(n), _tok_spec(n), _tok_spec(n),
                   _cache_spec(DIFF_HEADS, 2, DIFF_HEAD_DIM, SEQ), _cache_spec(DIFF_HEADS, SEQ, 2 * DIFF_HEAD_DIM)],
        out_shape=[jax.ShapeDtypeStruct((N_TOK, n), BF16)] * 3
                  + [_cache_shape(DIFF_HEADS, 2, DIFF_HEAD_DIM, SEQ),
                     _cache_shape(DIFF_HEADS, SEQ, 2 * DIFF_HEAD_DIM)],
        scratch_shapes=[pltpu.VMEM(w.shape, BF16)],
        compiler_params=_cparams(1),
        name="proj_diff",
    )(h, w, qg, kg, _group_mean_matrix(DIFF_HEAD_DIM), *tables)


def _dup_halves(yc):
    lo = _lane_lo(yc.shape)
    sw = pltpu.roll(yc, HALF, 1)
    return jnp.where(lo, yc, sw), jnp.where(lo, sw, yc)


def _proj_swa_kernel(h_ref, w_ref, qg_ref, kg_ref, m_ref,
                     cos_ref, sp_ref, sn_ref, q_ref, kd_ref, vd_ref, ck_ref, cv_ref, w_s):
    i = pl.program_id(0)
    _cast_once(i, w_ref, w_s)
    h = h_ref[...]
    per = PROJ_UNIT // LANES
    nq = SWA_HEADS * SWA_HEAD_DIM // PROJ_UNIT
    nk = SWA_KV_HEADS * SWA_HEAD_DIM // PROJ_UNIT

    def body(lat):
        rope = _rope_args(lat, cos_ref, sp_ref, sn_ref, SWA_HEAD_DIM)

        def emit(u, y):
            if u < nq + nk:
                y = _head_norm(y, m_ref, qg_ref[...] if u < nq else kg_ref[...])
            for t, yc in enumerate(_halves(y)):
                if u < nq:
                    _put(q_ref, u * per + t, _maybe_rope(yc, rope))
                    continue
                if u < nq + nk:
                    j, c_ref, d_ref = (u - nq) * per + t, ck_ref, kd_ref
                    yc = _maybe_rope(yc, rope)
                else:
                    j, c_ref, d_ref = (u - nq - nk) * per + t, cv_ref, vd_ref
                for a, dup in enumerate(_dup_halves(yc)):
                    _put(d_ref, 2 * j + a, dup)
                if not lat:
                    _cache_rows_t(c_ref, [[2 * j], [2 * j + 1]], yc)

        _matmul_units(h, w_s, nq + 2 * nk, PROJ_UNIT, emit)

    _by_tile_kind(i, body)


def _proj_swa(h, w, qg, kg, tables):
    nq, nk = SWA_HEADS * SWA_HEAD_DIM, SWA_KV_HEADS * SWA_HEAD_DIM
    return pl.pallas_call(
        _proj_swa_kernel,
        grid=(N_PROJ_TILES,),
        in_specs=[_tok_spec(D_MODEL),
                  _const_spec(w.shape), _const_spec((1, PROJ_UNIT)), _const_spec((1, PROJ_UNIT)),
                  _UNIT_MAT_SPEC, _ROPE_SPEC, _ROPE_SPEC, _ROPE_SPEC],
        out_specs=[_tok_spec(nq), _tok_spec(2 * nk), _tok_spec(2 * nk),
                   _cache_spec(SWA_KV_HEADS, SWA_HEAD_DIM, SEQ), _cache_spec(SWA_KV_HEADS, SWA_HEAD_DIM, SEQ)],
        out_shape=[jax.ShapeDtypeStruct((N_TOK, nq), BF16),
                   jax.ShapeDtypeStruct((N_TOK, 2 * nk), BF16),
                   jax.ShapeDtypeStruct((N_TOK, 2 * nk), BF16),
                   _cache_shape(SWA_KV_HEADS, SWA_HEAD_DIM, SEQ), _cache_shape(SWA_KV_HEADS, SWA_HEAD_DIM, SEQ)],
        scratch_shapes=[pltpu.VMEM(w.shape, BF16)],
        compiler_params=_cparams(1),
        name="proj_swa",
    )(h, w, qg, kg, _group_mean_matrix(SWA_HEAD_DIM), *tables)


def _mla_lane_matrices():
    everything = lambda lane: lane >= 0
    return (_lane_sum_matrix(everything, everything),
            _lane_sum_matrix(lambda lane: lane < HALF, everything),
            _lane_sum_matrix(lambda lane: lane >= HALF, everything))


def _proj_mla_kernel(h_ref, w_in_ref, qa_ref, kva_ref, w_uq_ref,
                     qg_ref, qgp_ref, all_ref, lo_ref, hi_ref, cos_ref, sp_ref, sn_ref,
                     qn_ref, qp_ref, ckv_ref, kpe_ref, c_ckv_ref, c_kpe_ref):
    i = pl.program_id(0)
    y = _dot(h_ref[...], w_in_ref[...])
    c_q = y[:, :MLA_Q_RANK]
    c_kv = y[:, MLA_Q_RANK:MLA_Q_RANK + MLA_KV_RANK]
    kpe = y[:, MLA_Q_RANK + MLA_KV_RANK:]
    kpe_ref[...] = kpe
    ckv = c_kv * lax.rsqrt(jnp.mean(c_kv * c_kv, axis=-1, keepdims=True) + EPS) * kva_ref[...]
    ckv_ref[...] = ckv.astype(BF16)
    cq = (c_q * lax.rsqrt(jnp.mean(c_q * c_q, axis=-1, keepdims=True) + EPS) * qa_ref[...]).astype(BF16)
    lo = _lane_lo((PROJ_TM, LANES))
    inv_d = 1.0 / (MLA_NOPE + MLA_ROPE)

    def body(lat):
        if not lat:
            _cache_rows(c_ckv_ref, [], ckv)
            _cache_rows_t(c_kpe_ref, [[]], kpe)

        def emit(j, yq):
            pe = yq[:, 2 * LANES:]
            pe_sq = _sq_bf16(pe)
            rs = []
            for a, half_ref in enumerate((lo_ref, hi_ref)):
                nope = yq[:, a * LANES:(a + 1) * LANES]
                ss = _dot(_sq_bf16(nope), all_ref[...]) + _dot(pe_sq, half_ref[...])
                r = lax.rsqrt(ss * inv_d + EPS)
                rs.append(r)
                _put(qn_ref, 2 * j + a, nope * r * qg_ref[...])
            pe = pe * jnp.where(lo, rs[0], rs[1]) * qgp_ref[...]
            if lat:
                pe = _rope(pe, cos_ref[...], sp_ref[...], sn_ref[...], MLA_ROPE // 4)
            _put(qp_ref, j, pe)

        _matmul_units(cq, w_uq_ref, MLA_HEADS // 2, 3 * LANES, emit)

    _by_tile_kind(i, body)


def _proj_mla(h, w_in, qa, kva, w_uq, qg, qgp, tables):
    n_nope = MLA_HEADS * MLA_NOPE
    n_pe = MLA_HEADS * MLA_ROPE
    return pl.pallas_call(
        _proj_mla_kernel,
        grid=(N_PROJ_TILES,),
        in_specs=[_tok_spec(D_MODEL),
                  _const_spec(w_in.shape), _const_spec((1, MLA_Q_RANK)), _const_spec((1, MLA_KV_RANK)),
                  _const_spec(w_uq.shape), _const_spec((1, LANES)), _const_spec((1, LANES)),
                  _LANE_MAT_SPEC, _LANE_MAT_SPEC, _LANE_MAT_SPEC,
                  _ROPE_SPEC, _ROPE_SPEC, _ROPE_SPEC],
        out_specs=[_tok_spec(n_nope), _tok_spec(n_pe), _tok_spec(MLA_KV_RANK), _tok_spec(LANES),
                   _cache_spec(SEQ, MLA_KV_RANK), _cache_spec(MLA_ROPE, SEQ)],
        out_shape=[jax.ShapeDtypeStruct((N_TOK, n_nope), BF16),
                   jax.ShapeDtypeStruct((N_TOK, n_pe), BF16),
                   jax.ShapeDtypeStruct((N_TOK, MLA_KV_RANK), BF16),
                   jax.ShapeDtypeStruct((N_TOK, LANES), F32),
                   _cache_shape(SEQ, MLA_KV_RANK), _cache_shape(MLA_ROPE, SEQ)],
        compiler_params=_cparams(1),
        name="proj_mla",
    )(h, w_in, qa, kva, w_uq, qg, qgp, *_mla_lane_matrices(), *tables)


def _mla_expand_kernel(ckv_ref, kpe_ref, w_ref, kg_ref, kgp_ref, sum_ref, lo_ref,
                       cos_ref, sp_ref, sn_ref, kn_ref, kp_ref, v_ref, *, rope):
    i = pl.program_id(0)
    ckv = ckv_ref[...].astype(BF16)
    kpe = kpe_ref[...]
    pe_ss = _dot(_sq_bf16(kpe), lo_ref[...])
    pe_ss = jnp.concatenate([pe_ss, pe_ss], axis=1)
    lo = _lane_lo(kpe.shape)
    inv_d = 1.0 / (MLA_NOPE + MLA_ROPE)

    def body(lat):
        def emit(j, y):
            kn = jnp.concatenate([y[:, :LANES], y[:, 2 * LANES:3 * LANES]], axis=1)
            r = lax.rsqrt((_dot(_sq_bf16(kn), sum_ref[...]) + pe_ss) * inv_d + EPS)
            kn = kn * r * kg_ref[...]
            for a in range(2):
                _put(kn_ref, 2 * j + a, kn[:, a * LANES:(a + 1) * LANES])
                _put(v_ref, 2 * j + a, y[:, (2 * a + 1) * LANES:(2 * a + 2) * LANES])
            pe = kpe * jnp.where(lo, r[:, :LANES], r[:, LANES:]) * kgp_ref[...]
            if lat:
                pe = _rope(pe, cos_ref[...], sp_ref[...], sn_ref[...], MLA_ROPE // 4)
            _put(kp_ref, j, pe)

        _matmul_units(ckv, w_ref, MLA_HEADS // 2, 4 * LANES, emit)

    if rope:
        _by_tile_kind(i, body)
    else:
        body(False)


def _mla_expand(ckv, kpe_dup, w_ukv, kg, kgp, tables, rope):
    n = ckv.shape[0]
    n_nope = MLA_HEADS * MLA_NOPE
    n_pe = MLA_HEADS * MLA_ROPE
    _, m_lo, _ = _mla_lane_matrices()
    return pl.pallas_call(
        functools.partial(_mla_expand_kernel, rope=rope),
        grid=(n // PROJ_TM,),
        in_specs=[_tok_spec(MLA_KV_RANK), _tok_spec(LANES), _const_spec(w_ukv.shape),
                  _const_spec((1, PROJ_UNIT)), _const_spec((1, LANES)), _UNIT_MAT_SPEC, _LANE_MAT_SPEC,
                  _ROPE_SPEC, _ROPE_SPEC, _ROPE_SPEC],
        out_specs=[_tok_spec(n_nope), _tok_spec(n_pe), _tok_spec(n_nope)],
        out_shape=[jax.ShapeDtypeStruct((n, n_nope), BF16),
                   jax.ShapeDtypeStruct((n, n_pe), BF16),
                   jax.ShapeDtypeStruct((n, n_nope), BF16)],
        compiler_params=_cparams(1),
        name="mla_expand",
    )(ckv, kpe_dup, w_ukv, kg, kgp, _group_sum_matrix(), m_lo, *tables)


def _prompt_spec(width):
    return pl.BlockSpec((PROMPT_SEQS * TM, width), lambda b: (b, 0))


def _latq_spec(rows, width):
    per = DEC_SEQ // rows
    return pl.BlockSpec((rows, width), lambda b, t: (N_PROMPT_TOK // rows + b * per + t, 0))


def _latkv_spec(width):
    return pl.BlockSpec((DEC_SEQ, width), lambda b, t: (LAT_BLOCK0 + b, 0))


def _lato_spec(rows):
    per = DEC_SEQ // rows
    return pl.BlockSpec((rows, D_MODEL), lambda b, t: (b * per + t, 0))


def _att_kernel(*refs, with_ctx, seqs):
    if with_ctx:
        q_ref, k_ref, v_ref, kc_ref, vc_ref, o_ref = refs
    else:
        q_ref, k_ref, v_ref, o_ref = refs
    tq = q_ref.shape[0] // seqs
    nu = ATT_UNIT_HEADS
    per_kv = ATT_HEADS // ATT_KV_HEADS // nu

    def make(views):
        q_v, k_v, v_v, o_v = views

        def scores(u):
            q = jnp.concatenate([_chunk(q_v, u * nu + g) for g in range(nu)], axis=0)
            s_list = [_dot_nt(_chunk(k_v, u // per_kv), q)]
            if with_ctx:
                s_list.append(_dot_nt(kc_ref[u // per_kv].astype(BF16), q))
            return s_list

        def finish(u, s_list):
            values = [_chunk(v_v, u // per_kv)]
            if with_ctx:
                values.append(vc_ref[u // per_kv].astype(BF16))
            ps, inv = _softmax2_parts(s_list)
            o = _pv(ps, values) * inv
            for g in range(nu):
                o_v[:, (u * nu + g) * LANES:(u * nu + g + 1) * LANES] = (
                    o[:, g * tq:(g + 1) * tq].T.astype(o_v.dtype))

        return scores, finish

    _seq_pipeline((q_ref, k_ref, v_ref, o_ref), seqs, ATT_HEADS // nu, make)


def _att_attend(q, k, v, cache_k, cache_v):
    nk = ATT_KV_HEADS * ATT_HEAD_DIM
    out_p = pl.pallas_call(
        functools.partial(_att_kernel, with_ctx=False, seqs=PROMPT_SEQS),
        grid=(N_PROMPT_TILES // PROMPT_SEQS,),
        in_specs=[_prompt_spec(D_MODEL), _prompt_spec(nk), _prompt_spec(nk)],
        out_specs=_prompt_spec(D_MODEL),
        out_shape=jax.ShapeDtypeStruct((N_PROMPT_TOK, D_MODEL), BF16),
        compiler_params=_cparams(1),
        name="att_prompt",
    )(q, k, v)
    ctx = pl.BlockSpec((None, None, ATT_KV_HEADS, PAST_LEN, LANES), lambda b, t: (b, 0, 0, 0, 0))
    out_s = pl.pallas_call(
        functools.partial(_att_kernel, with_ctx=True, seqs=1),
        grid=(DEC_BATCH, TILES_PER_DEC),
        in_specs=[_latq_spec(TM, D_MODEL), _latkv_spec(nk), _latkv_spec(nk), ctx, ctx],
        out_specs=_lato_spec(TM),
        out_shape=jax.ShapeDtypeStruct((N_LAT_TOK, D_MODEL), BF16),
        compiler_params=_cparams(2),
        name="att_latent",
    )(q, k, v, cache_k, cache_v)
    return out_p, out_s


def _diff_kernel(*refs, lam_init, with_ctx, seqs):
    if with_ctx:
        (q_ref, k_ref, v_ref, kc_ref, vc_ref, lq1_ref, lk1_ref, lq2_ref, lk2_ref, sub_ref, o_ref) = refs
    else:
        (q_ref, k_ref, v_ref, lq1_ref, lk1_ref, lq2_ref, lk2_ref, sub_ref, o_ref) = refs
    tq = q_ref.shape[0] // seqs
    lam = (jnp.exp(jnp.sum(lq1_ref[...] * lk1_ref[...], axis=-1, keepdims=True))
           - jnp.exp(jnp.sum(lq2_ref[...] * lk2_ref[...], axis=-1, keepdims=True)) + lam_init)
    sub = sub_ref[...] * (1.0 - lam_init)

    def make(views):
        q_v, k_v, v_v, o_v = views

        def scores(hd):
            q = jnp.concatenate(_split_halves(_chunk(q_v, hd)), axis=0)
            s_list = [_dot_nt(_chunk(k_v, hd), q)]
            if with_ctx:
                s_list.append(_dot_nt(kc_ref[hd].astype(BF16), q))
            return s_list

        def finish(hd, s_list):
            values = [_chunk(v_v, hd)]
            if with_ctx:
                values.append(vc_ref[hd].astype(BF16))
            ps, inv = _softmax2_parts(s_list)
            o = (_pv([p[:, :tq] for p in ps], values) * inv[:, :tq]
                 - _pv([p[:, tq:] for p in ps], values) * (lam * inv[:, tq:]))
            o = o * lax.rsqrt(jnp.mean(o * o, axis=0, keepdims=True) + EPS) * sub
            o_v[:, hd * LANES:(hd + 1) * LANES] = o.T.astype(o_v.dtype)

        return scores, finish

    _seq_pipeline((q_ref, k_ref, v_ref, o_ref), seqs, DIFF_HEADS, make)


def _diff_attend(q, k, v, cache_k_pair, cache_v, lq1, lk1, lq2, lk2, subln, lam_init):
    small = [lq1, lk1, lq2, lk2, subln]
    small_specs = [_const_spec(s.shape) for s in small]
    out_p = pl.pallas_call(
        functools.partial(_diff_kernel, lam_init=lam_init, with_ctx=False, seqs=PROMPT_SEQS),
        grid=(N_PROMPT_TILES // PROMPT_SEQS,),
        in_specs=[_prompt_spec(D_MODEL)] * 3 + small_specs,
        out_specs=_prompt_spec(D_MODEL),
        out_shape=jax.ShapeDtypeStruct((N_PROMPT_TOK, D_MODEL), BF16),
        compiler_params=_cparams(1),
        name="diff_prompt",
    )(q, k, v, *small)
    out_s = pl.pallas_call(
        functools.partial(_diff_kernel, lam_init=lam_init, with_ctx=True, seqs=1),
        grid=(DEC_BATCH, TILES_PER_DEC),
        in_specs=[_latq_spec(TM, D_MODEL), _latkv_spec(D_MODEL), _latkv_spec(D_MODEL),
                  pl.BlockSpec((None, DIFF_HEADS, PAST_LEN, LANES), lambda b, t: (b, 0, 0, 0)),
                  pl.BlockSpec((None, None, DIFF_HEADS, PAST_LEN, LANES), lambda b, t: (b, 0, 0, 0, 0))]
                 + small_specs,
        out_specs=_lato_spec(TM),
        out_shape=jax.ShapeDtypeStruct((N_LAT_TOK, D_MODEL), BF16),
        compiler_params=_cparams(2),
        name="diff_latent",
    )(q, k, v, cache_k_pair, cache_v, *small)
    return out_p, out_s


def _swa_pipeline(q_ref, o_ref, seq_refs, sink_ref, score_fns, value_fns, seqs=1):
    tq = q_ref.shape[0] // seqs
    per_kv = SWA_HEADS // SWA_KV_HEADS // 2
    first = lax.broadcasted_iota(jnp.int32, (LANES, tq), 0) < HALF

    def make(views):
        q_v, o_v = views[:2]
        kv_views = views[2:]

        def scores(c):
            q = jnp.concatenate(_split_halves(_chunk(q_v, c)), axis=0)
            return [fn(kv_views, c // per_kv, q) for fn in score_fns]

        def finish(c, s_list):
            sink = jnp.concatenate([jnp.full((1, tq), sink_ref[2 * c + a] * LOG2E, F32) for a in range(2)],
                                   axis=1)
            ps, inv = _softmax2_parts(s_list, extra=sink)
            o = _pv(ps, [fn(kv_views, c // per_kv) for fn in value_fns]) * inv
            oc = jnp.where(first, o[:, :tq], o[:, tq:])
            o_v[:, c * LANES:(c + 1) * LANES] = oc.T.astype(o_v.dtype)

        return scores, finish

    _seq_pipeline((q_ref, o_ref) + tuple(seq_refs), seqs, SWA_HEADS // 2, make)


def _swa_prompt_kernel(sink_ref, q_ref, k_ref, v_ref, o_ref):
    _swa_pipeline(q_ref, o_ref, (k_ref, v_ref), sink_ref,
                  [lambda kv_v, kv, q: _dot_nt(_chunk(kv_v[0], kv), q)],
                  [lambda kv_v, kv: _chunk(kv_v[1], kv)], seqs=PROMPT_SEQS)


def _swa_latent_kernel(sink_ref, q_ref, k_ref, v_ref, kc_ref, vc_ref, o_ref):
    n = pl.program_id(1)
    tq = q_ref.shape[0]
    span = 3 * SWA_QB
    start = pl.multiple_of(jnp.clip((n - 1) * SWA_QB, 0, DEC_SEQ - span), SWA_QB)
    cols = lax.broadcasted_iota(jnp.int32, (span, 2 * tq), 1)
    qpos = n * SWA_QB + jnp.bitwise_and(cols, tq - 1)
    kpos = start + lax.broadcasted_iota(jnp.int32, (span, 2 * tq), 0)
    valid = jnp.abs(qpos - kpos) <= WINDOW

    def local(ref, kv):
        return ref[pl.ds(start, span), kv * LANES:(kv + 1) * LANES]

    _swa_pipeline(q_ref, o_ref, (), sink_ref,
                  [lambda _, kv, q: jnp.where(valid, _dot_nt(local(k_ref, kv), q), -1e30),
                   lambda _, kv, q: _dot_nt(kc_ref[kv], q)],
                  [lambda _, kv: local(v_ref, kv), lambda _, kv: vc_ref[kv]])


def _swa_attend(q, kd, vd, cache_kd, cache_vd, sink):
    nkd = 2 * SWA_KV_HEADS * SWA_HEAD_DIM
    smem = pl.BlockSpec(memory_space=pltpu.SMEM)
    out_p = pl.pallas_call(
        _swa_prompt_kernel,
        grid=(N_PROMPT_TILES // PROMPT_SEQS,),
        in_specs=[smem, _prompt_spec(D_MODEL), _prompt_spec(nkd), _prompt_spec(nkd)],
        out_specs=_prompt_spec(D_MODEL),
        out_shape=jax.ShapeDtypeStruct((N_PROMPT_TOK, D_MODEL), BF16),
        compiler_params=_cparams(1),
        name="swa_prompt",
    )(sink, q, kd, vd)
    ctx = pl.BlockSpec((None, SWA_KV_HEADS, PAST_LEN, LANES), lambda b, n: (b, 0, 0, 0))
    out_s = pl.pallas_call(
        _swa_latent_kernel,
        grid=(DEC_BATCH, DEC_SEQ // SWA_QB),
        in_specs=[smem, _latq_spec(SWA_QB, D_MODEL), _latkv_spec(nkd), _latkv_spec(nkd), ctx, ctx],
        out_specs=_lato_spec(SWA_QB),
        out_shape=jax.ShapeDtypeStruct((N_LAT_TOK, D_MODEL), BF16),
        compiler_params=_cparams(2),
        name="swa_latent",
    )(sink, q, kd, vd, cache_kd, cache_vd)
    return out_p, out_s


def _mla_kernel(*refs, with_ctx, seqs):
    if with_ctx:
        (qn_ref, qp_ref, kn_ref, kp_ref, v_ref, knc_ref, kpc_ref, vc_ref, o_ref) = refs
    else:
        (qn_ref, qp_ref, kn_ref, kp_ref, v_ref, o_ref) = refs

    def make(views):
        qn_v, qp_v, kn_v, kp_v, v_v, o_v = views

        def scores(hd):
            j, a = hd // 2, hd % 2
            q = jnp.concatenate([_chunk(qn_v, hd), _split_halves(_chunk(qp_v, j))[a]], axis=1)
            s_list = [_dot_nt(jnp.concatenate([_chunk(kn_v, hd), _chunk(kp_v, j)], axis=1), q)]
            if with_ctx:
                s_list.append(_dot_nt(jnp.concatenate([_chunk(knc_ref, hd), _chunk(kpc_ref, j)], axis=1), q))
            return s_list

        def finish(hd, s_list):
            values = [_chunk(v_v, hd)]
            if with_ctx:
                values.append(_chunk(vc_ref, hd))
            ps, inv = _softmax2_parts(s_list)
            o_v[:, hd * LANES:(hd + 1) * LANES] = (_pv(ps, values) * inv).T.astype(o_v.dtype)

        return scores, finish

    _seq_pipeline((qn_ref, qp_ref, kn_ref, kp_ref, v_ref, o_ref), seqs, MLA_HEADS, make)


def _mla_attend(qn, qp, kn, kp, v, knc, kpc, vc):
    n_pe = MLA_HEADS * MLA_ROPE
    out_p = pl.pallas_call(
        functools.partial(_mla_kernel, with_ctx=False, seqs=PROMPT_SEQS),
        grid=(N_PROMPT_TILES // PROMPT_SEQS,),
        in_specs=[_prompt_spec(D_MODEL), _prompt_spec(n_pe), _prompt_spec(D_MODEL), _prompt_spec(n_pe),
                  _prompt_spec(D_MODEL)],
        out_specs=_prompt_spec(D_MODEL),
        out_shape=jax.ShapeDtypeStruct((N_PROMPT_TOK, D_MODEL), BF16),
        compiler_params=_cparams(1),
        name="mla_prompt",
    )(qn, qp, kn, kp, v)

    def ctx(width):
        return pl.BlockSpec((PAST_LEN, width), lambda b, t: (b, 0))

    out_s = pl.pallas_call(
        functools.partial(_mla_kernel, with_ctx=True, seqs=1),
        grid=(DEC_BATCH, TILES_PER_DEC),
        in_specs=[_latq_spec(TM, D_MODEL), _latq_spec(TM, n_pe),
                  _latkv_spec(D_MODEL), _latkv_spec(n_pe), _latkv_spec(D_MODEL),
                  ctx(D_MODEL), ctx(n_pe), ctx(D_MODEL)],
        out_specs=_lato_spec(TM),
        out_shape=jax.ShapeDtypeStruct((N_LAT_TOK, D_MODEL), BF16),
        compiler_params=_cparams(2),
        name="mla_latent",
    )(qn, qp, kn, kp, v, knc, kpc, vc)
    return out_p, out_s


def _omlp_kernel(*refs, first, last):
    refs = list(refs)
    ap_ref, as_ref, wo_ref = refs[:3]
    x_refs = refs[3:5] if first else refs[3:4]
    refs = refs[3 + len(x_refs):]
    g1_ref, gain_ref, sh_ref, sc_ref, g2_ref, w1c_ref, w2c_ref = refs[:7]
    refs = refs[7:]
    if last:
        op_ref, os_ref, wo_s, w1_s, w2_s = refs
    else:
        ngain_ref, nsh_ref, nsc_ref, o_ref, hn_ref, wo_s, w1_s, w2_s = refs
    s = pl.program_id(0)
    per = MLP_FF_CHUNK // MLP_LOAD_COLS
    n_chunks = D_FF // MLP_FF_CHUNK
    half = MLP_TM // 2

    @pl.when(s == 0)
    def _():
        wo_s[...] = wo_ref[...].astype(BF16)

    for part in range(per):
        @pl.when((s < N_LOAD_STEPS) & (s % per == part))
        def _(part=part):
            w1_s[s // per, :, part * MLP_LOAD_COLS:(part + 1) * MLP_LOAD_COLS] = w1c_ref[...].astype(BF16)

    @pl.when(s < N_LOAD_STEPS)
    def _():
        w2_s[s // per, pl.ds(pl.multiple_of((s % per) * MLP_LOAD_COLS, MLP_LOAD_COLS), MLP_LOAD_COLS), :] = (
            w2c_ref[...].astype(BF16))

    @pl.when(s >= N_LOAD_STEPS)
    def _():
        t = s - N_LOAD_STEPS
        is_prompt = t < N_MLP_PROMPT_TILES
        grp = _tile_group(t, MLP_TM)

        def mod(ref):
            return ref[pl.ds(grp, 1), :]

        rows = [slice(r * half, (r + 1) * half) for r in range(2)]
        o = [_dot(jnp.where(is_prompt, ap_ref[rw, :], as_ref[rw, :]), wo_s[...]) for rw in rows]
        x1, h, u0 = [], [], []
        for r, rw in enumerate(rows):
            x = jnp.where(is_prompt, x_refs[0][rw, :], x_refs[1][rw, :]) if first else x_refs[0][rw, :]
            x1.append(x + mod(g1_ref) * o[r])
            h.append(_norm_mod(x1[r], gain_ref[...], mod(sh_ref), mod(sc_ref)).astype(BF16))
            u0.append(_dot(h[r], w1_s[0]))
        h = jnp.concatenate(h, axis=0)
        acc = []

        def up(c):
            return jnp.concatenate(u0, axis=0) if c == 0 else _dot(h, w1_s[c])

        def down(c, u):
            u = jnp.square(jnp.maximum(u, 0.0)).astype(BF16)
            if c + 1 < n_chunks:
                y = _dot(u, w2_s[c])
                acc[:] = [y if not acc else acc[0] + y]
            else:
                acc[:] = [acc[0][rw] + _dot(u[rw], w2_s[c]) for rw in rows]

        _head_pipeline(n_chunks, up, down)
        for r, rw in enumerate(rows):
            out = x1[r] + mod(g2_ref) * acc[r]
            if last:
                @pl.when(is_prompt)
                def _(out=out, rw=rw):
                    op_ref[rw, :] = out

                @pl.when(jnp.logical_not(is_prompt))
                def _(out=out, rw=rw):
                    os_ref[rw, :] = out
            else:
                o_ref[rw, :] = out
                hn_ref[rw, :] = _norm_mod(out, ngain_ref[...], mod(nsh_ref), mod(nsc_ref)).astype(BF16)


def _omlp(attn_p, attn_s, w_o, x, mods, gain_ffn, w1_all, w2_all, layer, next_gain):
    first, last = layer == 0, next_gain is None
    n_lat_tiles = N_LAT_TOK // MLP_TM

    def tok(s):
        return jnp.maximum(s - N_LOAD_STEPS, 0)

    p_spec = pl.BlockSpec((MLP_TM, D_MODEL), lambda s: (jnp.minimum(tok(s), N_MLP_PROMPT_TILES - 1), 0))
    l_spec = pl.BlockSpec((MLP_TM, D_MODEL),
                          lambda s: (jnp.clip(tok(s) - N_MLP_PROMPT_TILES, 0, n_lat_tiles - 1), 0))
    w1_spec = pl.BlockSpec((None, D_MODEL, MLP_LOAD_COLS),
                           lambda s: (layer, 0, jnp.minimum(s, N_LOAD_STEPS - 1)))
    w2_spec = pl.BlockSpec((None, MLP_LOAD_COLS, D_MODEL),
                           lambda s: (layer, jnp.minimum(s, N_LOAD_STEPS - 1), 0))
    t_spec = pl.BlockSpec((MLP_TM, D_MODEL), lambda s: (tok(s), 0))
    n_chunks = D_FF // MLP_FF_CHUNK
    split = ([p_spec, l_spec], [jax.ShapeDtypeStruct((N_PROMPT_TOK, D_MODEL), F32),
                                jax.ShapeDtypeStruct((N_LAT_TOK, D_MODEL), F32)])
    in_specs = ([p_spec, l_spec, _const_spec(w_o.shape)] + (split[0] if first else [t_spec])
                + [_mod_spec(layer, 2), _const_spec((1, D_MODEL)), _mod_spec(layer, 3), _mod_spec(layer, 4),
                   _mod_spec(layer, 5),
                   w1_spec, w2_spec])
    args = ([attn_p, attn_s, w_o] + (list(x) if first else [x])
            + [mods, gain_ffn, mods, mods, mods, w1_all, w2_all])
    if last:
        out_specs, out_shape = split
    else:
        in_specs += [_const_spec((1, D_MODEL)), _mod_spec(layer + 1, 0), _mod_spec(layer + 1, 1)]
        args += [next_gain, mods, mods]
        out_specs = [t_spec, t_spec]
        out_shape = [jax.ShapeDtypeStruct((N_TOK, D_MODEL), F32), jax.ShapeDtypeStruct((N_TOK, D_MODEL), BF16)]
    return pl.pallas_call(
        functools.partial(_omlp_kernel, first=first, last=last),
        grid=(N_LOAD_STEPS + N_TOK // MLP_TM,),
        in_specs=in_specs,
        out_specs=out_specs,
        out_shape=out_shape,
        scratch_shapes=[pltpu.VMEM((D_MODEL, D_MODEL), BF16),
                        pltpu.VMEM((n_chunks, D_MODEL, MLP_FF_CHUNK), BF16),
                        pltpu.VMEM((n_chunks, MLP_FF_CHUNK, D_MODEL), BF16)],
        compiler_params=_cparams(1),
        name="omlp",
    )(*args)


def _row(v, scale=1.0):
    return (v.astype(F32) * scale).reshape(1, -1)


def _pair(v, scale=1.0):
    return (jnp.concatenate([v, v]).astype(F32) * scale).reshape(1, LANES)


def _unit_gain(v, scale=1.0):
    return (jnp.tile(v.astype(F32), PROJ_UNIT // v.shape[0]) * scale).reshape(1, PROJ_UNIT)


def kernel(x_prompt, x_sample, cache_att_k, cache_att_v, cache_diff_k, cache_diff_v, cache_swa_k, cache_swa_v, cache_mla_ckv, cache_mla_kpe, c, c_ctx, ada_w, ada_b, norm_mix, norm_ffn, att_w_qkv, att_q_norm, att_k_norm, att_w_o, diff_w_qkv, diff_q_norm, diff_k_norm, diff_lq1, diff_lk1, diff_lq2, diff_lk2, diff_subln, diff_w_o, swa_w_qkv, swa_q_norm, swa_k_norm, swa_sink, swa_w_o, mla_w_in, mla_q_a_norm, mla_kv_a_norm, mla_w_uq, mla_w_ukv, mla_q_norm, mla_k_norm, mla_w_o, mlp_w1, mlp_w2):
    xp = x_prompt.reshape(N_PROMPT_TOK, D_MODEL)
    xs = x_sample.reshape(N_LAT_TOK, D_MODEL)
    cond = jnp.concatenate([c_ctx[None], c, jnp.zeros((COND_ROWS - 1 - DEC_BATCH, D_MODEL), F32)], axis=0)
    mods_all = _modulation(cond, ada_w, ada_b)

    tab_att = _rope_tables(ATT_HEAD_DIM)
    tab_64 = _rope_tables(DIFF_HEAD_DIM)

    outs = {}
    x = (xp, xs)
    for layer in range(DEPTH):
        gain_ffn = _row(norm_ffn[layer])
        if layer == 0:
            qs = ATT_HEAD_DIM ** -0.5 * LOG2E
            q, k, v, outs["att_k"], outs["att_v"] = _proj_att(
                xp, xs, mods_all, _row(norm_mix[layer]), att_w_qkv[0],
                _unit_gain(att_q_norm[0], qs), _unit_gain(att_k_norm[0]), tab_att)
            attn_p, attn_s = _att_attend(q, k, v, cache_att_k, cache_att_v)
            w_o = att_w_o[0]
        elif layer == 1:
            qs = DIFF_HEAD_DIM ** -0.5 * LOG2E
            q, k, v, outs["diff_k"], outs["diff_v"] = _proj_diff(
                h, diff_w_qkv[0],
                _unit_gain(diff_q_norm[0], qs), _unit_gain(diff_k_norm[0]), tab_64)
            lam_init = 0.8 - 0.6 * math.exp(-0.3 * layer)
            ck = cache_diff_k[:, 0].transpose(0, 1, 3, 2, 4).reshape(
                DEC_BATCH, DIFF_HEADS, PAST_LEN, LANES)
            attn_p, attn_s = _diff_attend(q, k, v, ck, cache_diff_v,
                                          _row(diff_lq1[0]), _row(diff_lk1[0]),
                                          _row(diff_lq2[0]), _row(diff_lk2[0]),
                                          diff_subln[0].astype(F32).reshape(LANES, 1), lam_init)
            w_o = diff_w_o[0]
        elif layer == 2:
            qs = SWA_HEAD_DIM ** -0.5 * LOG2E
            q, kd, vd, outs["swa_k"], outs["swa_v"] = _proj_swa(
                h, swa_w_qkv[0],
                _unit_gain(swa_q_norm[0], qs), _unit_gain(swa_k_norm[0]), tab_64)
            ckd = jnp.concatenate([cache_swa_k[:, 0]] * 2, axis=-1).astype(BF16)
            cvd = jnp.concatenate([cache_swa_v[:, 0]] * 2, axis=-1).astype(BF16)
            attn_p, attn_s = _swa_attend(q, kd, vd, ckd, cvd, swa_sink[0].astype(F32))
            w_o = swa_w_o[0]
        else:
            qs = (MLA_NOPE + MLA_ROPE) ** -0.5 * LOG2E
            w_in = mla_w_in[0]
            w_in = jnp.concatenate([w_in, w_in[:, -MLA_ROPE:]], axis=1).astype(BF16)
            w_uq = mla_w_uq[0].reshape(MLA_Q_RANK, MLA_HEADS // 2, 2, MLA_NOPE + MLA_ROPE)
            w_uq = jnp.concatenate([w_uq[..., :MLA_NOPE].reshape(MLA_Q_RANK, MLA_HEADS // 2, 2 * MLA_NOPE),
                                    w_uq[..., MLA_NOPE:].reshape(MLA_Q_RANK, MLA_HEADS // 2, 2 * MLA_ROPE)],
                                   axis=-1).reshape(MLA_Q_RANK, -1).astype(BF16)
            w_ukv = mla_w_ukv[0].astype(BF16)
            qg, kg = mla_q_norm[0], mla_k_norm[0]
            qn, qp, ckv, kpe, outs["mla_ckv"], outs["mla_kpe"] = _proj_mla(
                h, w_in, _row(mla_q_a_norm[0]), _row(mla_kv_a_norm[0]), w_uq,
                _row(qg[:MLA_NOPE], qs), _pair(qg[MLA_NOPE:], qs), tab_64)
            kn, kp, vv = _mla_expand(ckv, kpe, w_ukv, _unit_gain(kg[:MLA_NOPE]), _pair(kg[MLA_NOPE:]),
                                     tab_64, True)
            c_ckv = cache_mla_ckv[:, 0].reshape(DEC_BATCH * PAST_LEN, MLA_KV_RANK)
            c_kpe = cache_mla_kpe[:, 0].reshape(DEC_BATCH * PAST_LEN, MLA_ROPE)
            c_kpe = jnp.concatenate([c_kpe, c_kpe], axis=-1)
            knc, kpc, vc = _mla_expand(c_ckv, c_kpe, w_ukv, _unit_gain(kg[:MLA_NOPE]), _pair(kg[MLA_NOPE:]),
                                       tab_64, False)
            attn_p, attn_s = _mla_attend(qn, qp, kn, kp, vv, knc, kpc, vc)
            w_o = mla_w_o[0]
        if layer + 1 < DEPTH:
            x, h = _omlp(attn_p, attn_s, w_o, x, mods_all, gain_ffn, mlp_w1, mlp_w2, layer,
                         _row(norm_mix[layer + 1]))
        else:
            xp, xs = _omlp(attn_p, attn_s, w_o, x, mods_all, gain_ffn, mlp_w1, mlp_w2, layer, None)

    y_prompt = xp.reshape(BATCH, SEQ, D_MODEL)
    y_sample = xs.reshape(DEC_BATCH, DEC_SEQ, D_MODEL)
    for name in ("diff_k", "swa_k", "swa_v", "mla_kpe"):
        outs[name] = jnp.swapaxes(outs[name], -1, -2)
    return (y_prompt, y_sample, outs["att_k"], outs["att_v"], outs["diff_k"], outs["diff_v"],
            outs["swa_k"], outs["swa_v"], outs["mla_ckv"], outs["mla_kpe"])
```

```python
import functools
import math

import numpy as np
import jax
import jax.numpy as jnp
from jax import lax
from jax.experimental import pallas as pl
from jax.experimental.pallas import tpu as pltpu

D_MODEL = 1024
BATCH = 16
SEQ = 256
DEPTH = 4
DEC_BATCH = 2
DEC_SEQ = 1024
PAST_LEN = 256
GRID_W = 64
ROPE_THETA = 10000.0
EPS = 1e-6
D_FF = 4 * D_MODEL
MOD_CHUNKS = 6
LOG2E = 1.4426950408889634

ATT_HEADS, ATT_KV_HEADS, ATT_HEAD_DIM = 8, 2, 128
DIFF_HEADS, DIFF_HEAD_DIM = 8, 64
SWA_HEADS, SWA_KV_HEADS, SWA_HEAD_DIM, WINDOW = 16, 4, 64, 128
MLA_HEADS, MLA_NOPE, MLA_ROPE, MLA_VDIM = 8, 128, 64, 128
MLA_Q_RANK, MLA_KV_RANK = 512, 256

LANES = 128
HALF = LANES // 2
TM = 256
N_PROMPT_TOK = BATCH * SEQ
N_LAT_TOK = DEC_BATCH * DEC_SEQ
N_TOK = N_PROMPT_TOK + N_LAT_TOK
N_PROMPT_TILES = N_PROMPT_TOK // TM
LAT_TQ = 512
LAT_BLOCK0 = N_PROMPT_TOK // DEC_SEQ
COND_ROWS = 8
PROJ_TM = 512
PROJ_BATCHES = PROJ_TM // SEQ
N_PROJ_TILES = N_TOK // PROJ_TM
N_PROJ_PROMPT = N_PROMPT_TOK // PROJ_TM
PROJ_UNIT = 2 * LANES
MLP_TM = 512
MLP_FF_CHUNK = 512
MLP_LOAD_COLS = 256
N_LOAD_STEPS = D_FF // MLP_LOAD_COLS
N_MLP_PROMPT_TILES = N_PROMPT_TOK // MLP_TM
SWA_QB = 128
ATT_UNIT_HEADS = 4
PROMPT_SEQS = 4
VMEM_LIMIT = 56 * 1024 * 1024

F32 = jnp.float32
BF16 = jnp.bfloat16


def _cparams(n_axes):
    return pltpu.CompilerParams(dimension_semantics=("arbitrary",) * n_axes,
                                vmem_limit_bytes=VMEM_LIMIT)


def _dot(a, b):
    return jnp.dot(a, b, preferred_element_type=F32)


def _dot_nt(a, b):
    return lax.dot_general(a, b, (((1,), (1,)), ((), ())), preferred_element_type=F32)


def _dot_tn(a, b):
    return lax.dot_general(a, b, (((0,), (0,)), ((), ())), preferred_element_type=F32)


def _const_spec(shape):
    nd = len(shape)
    return pl.BlockSpec(shape, lambda *_: (0,) * nd, pipeline_mode=pl.Buffered(1))


class _ParamPack:
    def __init__(self):
        self._rows, self.array = [], None

    def add(self, v, repeat=1, scale=1.0):
        n = v.shape[0] * repeat
        row = _ParamRow(self, -(-n // LANES) * LANES)
        self._rows.append((row, [v] * repeat, scale, n))
        return row

    def build(self):
        pieces, scales, offset = [], [], 0
        for row, vs, scale, n in sorted(self._rows, key=lambda r: -r[0].width):
            row.offset = offset
            pieces += vs + ([jnp.zeros((row.width - n,), F32)] if row.width > n else [])
            scales.append(np.full((row.width,), scale, np.float32))
            offset += row.width
        flat = jnp.concatenate([p.astype(F32) for p in pieces]) * jnp.asarray(np.concatenate(scales))
        self.array = flat.reshape(1, offset)


class _ParamRow:
    def __init__(self, pack, width):
        self.pack, self.width, self.offset = pack, width, None

    @property
    def array(self):
        return self.pack.array

    @property
    def spec(self):
        block = self.offset // self.width
        return pl.BlockSpec((1, self.width), lambda *_: (0, block), pipeline_mode=pl.Buffered(1))


def _chunk(ref, c, width=LANES):
    return ref[:, c * width:(c + 1) * width]


def _put(ref, c, val):
    ref[:, c * LANES:(c + 1) * LANES] = val.astype(ref.dtype)


def _tile_group(i, rows):
    n_prompt = N_PROMPT_TOK // rows
    return jnp.where(i < n_prompt, 0, 1 + (i - n_prompt) // (DEC_SEQ // rows))


def _rope_tile(i):
    return jnp.maximum(i - N_PROJ_PROMPT, 0) % (DEC_SEQ // PROJ_TM)


def _norm_mod(x, gain, shift, scale):
    ms = jnp.mean(x * x, axis=-1, keepdims=True)
    return x * lax.rsqrt(ms + EPS) * (gain * (1.0 + scale)) + shift


def _lane_lo(shape):
    return lax.broadcasted_iota(jnp.int32, shape, len(shape) - 1) < HALF


def _rope(y, cos, sin_prev, sin_next, quarter):
    return (y * cos + pltpu.roll(y, quarter, 1) * sin_prev
            + pltpu.roll(y, LANES - quarter, 1) * sin_next)


def _rope_tables(rot_dim):
    half = rot_dim // 2
    quarter = rot_dim // 4
    inv = np.float32(ROPE_THETA) ** (-np.arange(0, half, 2, dtype=np.float32) / np.float32(half))
    pos = np.arange(DEC_SEQ)
    row = (pos // GRID_W).astype(np.float32)
    col = (pos % GRID_W).astype(np.float32)
    lane = np.arange(LANES)
    dd = lane % rot_dim
    q = dd // quarter
    f = dd % quarter
    ang = np.where((q < 2)[None, :], row[:, None], col[:, None]) * inv[f][None, :]
    ang = ang.astype(np.float32)
    cos = np.cos(ang).astype(np.float32)
    sin = np.sin(ang).astype(np.float32)
    odd = (q % 2 == 1)[None, :]
    sin_prev = np.where(odd, sin, 0.0).astype(np.float32)
    sin_next = np.where(odd, 0.0, -sin).astype(np.float32)
    return jnp.asarray(cos), jnp.asarray(sin_prev), jnp.asarray(sin_next)


def _lane_sum_matrix(rows, cols, value=1.0):
    lane = np.arange(LANES)
    m = np.where(rows(lane)[:, None] & cols(lane)[None, :], value, 0.0).astype(np.float32)
    return jnp.asarray(m, dtype=BF16)


def _group_mean_matrix(group):
    lane = np.arange(PROJ_UNIT)
    m = np.where((lane[:, None] // group) == (lane[None, :] // group), 1.0 / group, 0.0)
    return jnp.asarray(m.astype(np.float32), dtype=BF16)


def _group_sum_matrix():
    lane = np.arange(PROJ_UNIT)
    m = np.where((lane[:, None] // LANES) == (lane[None, :] // LANES), 1.0, 0.0)
    return jnp.asarray(m.astype(np.float32), dtype=BF16)


def _sq_bf16(y):
    return (y * y).astype(BF16)


def _head_norm(y, m_ref, gain):
    return y * lax.rsqrt(_dot(_sq_bf16(y), m_ref[...]) + EPS) * gain


def _halves(y):
    return [y[:, t * LANES:(t + 1) * LANES] for t in range(y.shape[1] // LANES)]


def _matmul_units(h, w_ref, n_units, width, emit):
    def unit(u):
        return _dot(h, w_ref[:, u * width:(u + 1) * width])

    nxt = unit(0)
    for u in range(n_units):
        cur = nxt
        if u + 1 < n_units:
            nxt = unit(u + 1)
        emit(u, cur)


def _cast_once(i, w_ref, w_s):
    @pl.when(i == 0)
    def _():
        w_s[...] = w_ref[...].astype(BF16)


def _by_tile_kind(i, body):
    pl.when(i < N_PROJ_PROMPT)(functools.partial(body, False))
    pl.when(i >= N_PROJ_PROMPT)(functools.partial(body, True))


def _rope_args(lat, cos_ref, sp_ref, sn_ref, rot_dim):
    return (cos_ref[...], sp_ref[...], sn_ref[...], rot_dim // 4) if lat else None


def _maybe_rope(y, rope):
    return y if rope is None else _rope(y, *rope)


def _cache_rows(ref, index, val):
    for b in range(PROJ_BATCHES):
        ref[(b, 0) + tuple(index)] = val[b * SEQ:(b + 1) * SEQ]


def _cache_rows_t(ref, indices, val):
    for b in range(PROJ_BATCHES):
        t = val[b * SEQ:(b + 1) * SEQ].T
        for j, index in enumerate(indices):
            ref[(b, 0) + tuple(index)] = t[j * HALF:(j + 1) * HALF]


def _softmax2_parts(s_list, extra=None):
    m = jnp.max(s_list[0], axis=0, keepdims=True)
    for s in s_list[1:]:
        m = jnp.maximum(m, jnp.max(s, axis=0, keepdims=True))
    if extra is not None:
        m = jnp.maximum(m, extra)
    ps = [jnp.exp2(s - m) for s in s_list]
    mass = ps[0].sum(axis=0, keepdims=True)
    for p in ps[1:]:
        mass = mass + p.sum(axis=0, keepdims=True)
    if extra is not None:
        mass = mass + jnp.exp2(extra - m)
    return [p.astype(BF16) for p in ps], 1.0 / mass


def _head_pipeline(n, scores, finish):
    nxt = scores(0)
    for h in range(n):
        cur = nxt
        if h + 1 < n:
            nxt = scores(h + 1)
        finish(h, cur)


def _seq_pipeline(refs, seqs, n, make):
    fns = []
    for b in range(seqs):
        views = [r.at[b * (r.shape[0] // seqs):(b + 1) * (r.shape[0] // seqs)] for r in refs]
        fns.append(make(views))
    _head_pipeline(seqs * n, lambda i: fns[i // n][0](i % n), lambda i, s: fns[i // n][1](i % n, s))


def _pv(ps, values):
    o = None
    for p, v in zip(ps, values):
        t = _dot_tn(v, p)
        o = t if o is None else o + t
    return o


def _split_halves(q):
    lo = _lane_lo(q.shape)
    zero = jnp.zeros_like(q)
    return jnp.where(lo, q, zero), jnp.where(lo, zero, q)


def _mod_kernel(cond_ref, w_ref, b_ref, o_ref):
    c = cond_ref[...]
    s = (c * jax.nn.sigmoid(c)).astype(BF16)
    o_ref[0] = _dot(s, w_ref[0].astype(BF16)) + b_ref[0]


def _modulation(cond, ada_w, ada_b):
    tn = 1536
    n = MOD_CHUNKS * D_MODEL
    return pl.pallas_call(
        _mod_kernel,
        grid=(DEPTH, n // tn),
        in_specs=[
            pl.BlockSpec((COND_ROWS, D_MODEL), lambda l, j: (0, 0)),
            pl.BlockSpec((1, D_MODEL, tn), lambda l, j: (l, 0, j)),
            pl.BlockSpec((1, 1, tn), lambda l, j: (l, 0, j)),
        ],
        out_specs=pl.BlockSpec((1, COND_ROWS, tn), lambda l, j: (l, 0, j)),
        out_shape=jax.ShapeDtypeStruct((DEPTH, COND_ROWS, n), F32),
        compiler_params=_cparams(2),
        name="modulation",
    )(cond, ada_w, ada_b.reshape(DEPTH, 1, n))


def _mod_spec(layer, chunk):
    return pl.BlockSpec((None, COND_ROWS, D_MODEL), lambda i: (layer, 0, chunk))


def _mod_row(ref, i):
    return ref[pl.ds(_tile_group(i, PROJ_TM), 1), :]


_ROPE_SPEC = pl.BlockSpec((PROJ_TM, LANES), lambda i: (_rope_tile(i), 0))
_LANE_MAT_SPEC = _const_spec((LANES, LANES))
_UNIT_MAT_SPEC = _const_spec((PROJ_UNIT, PROJ_UNIT))


def _tok_spec(width):
    return pl.BlockSpec((PROJ_TM, width), lambda i: (i, 0))


_XP_SPEC = pl.BlockSpec((PROJ_TM, D_MODEL), lambda i: (jnp.minimum(i, N_PROJ_PROMPT - 1), 0))
_XS_SPEC = pl.BlockSpec((PROJ_TM, D_MODEL), lambda i: (jnp.maximum(i - N_PROJ_PROMPT, 0), 0))


def _cache_spec(*dims):
    nd = len(dims)
    return pl.BlockSpec((PROJ_BATCHES, 1) + dims,
                        lambda i: (jnp.minimum(i, N_PROJ_PROMPT - 1), 0) + (0,) * nd)


def _cache_shape(*dims):
    return jax.ShapeDtypeStruct((BATCH, 1) + dims, F32)


def _proj_att_kernel(xp_ref, xs_ref, gain_ref, sh_ref, sc_ref, w_ref, qg_ref, kg_ref, m_ref,
                     cos_ref, sp_ref, sn_ref, q_ref, k_ref, v_ref, ck_ref, cv_ref, w_s):
    i = pl.program_id(0)
    _cast_once(i, w_ref, w_s)
    x = jnp.where(i < N_PROJ_PROMPT, xp_ref[...], xs_ref[...])
    h = _norm_mod(x, gain_ref[...], _mod_row(sh_ref, i), _mod_row(sc_ref, i)).astype(BF16)
    per = PROJ_UNIT // LANES
    nq, nk = ATT_HEADS // per, ATT_KV_HEADS // per

    def body(lat):
        rope = _rope_args(lat, cos_ref, sp_ref, sn_ref, ATT_HEAD_DIM)

        def emit(u, y):
            if u < nq + nk:
                y = _head_norm(y, m_ref, qg_ref[...] if u < nq else kg_ref[...])
            for t, yc in enumerate(_halves(y)):
                if u < nq:
                    _put(q_ref, u * per + t, _maybe_rope(yc, rope))
                elif u < nq + nk:
                    kn = _maybe_rope(yc, rope)
                    _put(k_ref, (u - nq) * per + t, kn)
                    if not lat:
                        _cache_rows(ck_ref, [(u - nq) * per + t], kn)
                else:
                    _put(v_ref, (u - nq - nk) * per + t, yc)
                    if not lat:
                        _cache_rows(cv_ref, [(u - nq - nk) * per + t], yc)

        _matmul_units(h, w_s, nq + 2 * nk, PROJ_UNIT, emit)

    _by_tile_kind(i, body)


def _proj_att(xp, xs, mods, gain, w, qg, kg, tables):
    nq, nk = ATT_HEADS * ATT_HEAD_DIM, ATT_KV_HEADS * ATT_HEAD_DIM
    return pl.pallas_call(
        _proj_att_kernel,
        grid=(N_PROJ_TILES,),
        in_specs=[_XP_SPEC, _XS_SPEC, gain.spec, _mod_spec(0, 0), _mod_spec(0, 1),
                  _const_spec(w.shape), qg.spec, kg.spec,
                  _UNIT_MAT_SPEC, _ROPE_SPEC, _ROPE_SPEC, _ROPE_SPEC],
        out_specs=[_tok_spec(nq), _tok_spec(nk), _tok_spec(nk),
                   _cache_spec(ATT_KV_HEADS, SEQ, ATT_HEAD_DIM), _cache_spec(ATT_KV_HEADS, SEQ, ATT_HEAD_DIM)],
        out_shape=[jax.ShapeDtypeStruct((N_TOK, nq), BF16),
                   jax.ShapeDtypeStruct((N_TOK, nk), BF16),
                   jax.ShapeDtypeStruct((N_TOK, nk), BF16),
                   _cache_shape(ATT_KV_HEADS, SEQ, ATT_HEAD_DIM), _cache_shape(ATT_KV_HEADS, SEQ, ATT_HEAD_DIM)],
        scratch_shapes=[pltpu.VMEM(w.shape, BF16)],
        compiler_params=_cparams(1),
        name="proj_att",
    )(xp, xs, gain.array, mods, mods, w, qg.array, kg.array, _group_mean_matrix(ATT_HEAD_DIM), *tables)


def _proj_diff_kernel(h_ref, w_ref, qg_ref, kg_ref, m_ref,
                      cos_ref, sp_ref, sn_ref, q_ref, k_ref, v_ref, ck_ref, cv_ref, w_s):
    i = pl.program_id(0)
    _cast_once(i, w_ref, w_s)
    h = h_ref[...]
    per = PROJ_UNIT // LANES
    nu = DIFF_HEADS // per

    def body(lat):
        rope = _rope_args(lat, cos_ref, sp_ref, sn_ref, DIFF_HEAD_DIM)

        def emit(u, y):
            if u < 2 * nu:
                y = _head_norm(y, m_ref, qg_ref[...] if u < nu else kg_ref[...])
            for t, yc in enumerate(_halves(y)):
                hd = (u % nu) * per + t
                if u < nu:
                    _put(q_ref, hd, _maybe_rope(yc, rope))
                elif u < 2 * nu:
                    kn = _maybe_rope(yc, rope)
                    _put(k_ref, hd, kn)
                    if not lat:
                        _cache_rows_t(ck_ref, [[hd, 0], [hd, 1]], kn)
                else:
                    _put(v_ref, hd, yc)
                    if not lat:
                        _cache_rows(cv_ref, [hd], yc)

        _matmul_units(h, w_s, 3 * nu, PROJ_UNIT, emit)

    _by_tile_kind(i, body)


def _proj_diff(h, w, qg, kg, tables):
    n = DIFF_HEADS * 2 * DIFF_HEAD_DIM
    return pl.pallas_call(
        _proj_diff_kernel,
        grid=(N_PROJ_TILES,),
        in_specs=[_tok_spec(D_MODEL), _const_spec(w.shape), qg.spec, kg.spec,
                  _UNIT_MAT_SPEC, _ROPE_SPEC, _ROPE_SPEC, _ROPE_SPEC],
        out_specs=[_tok_spec(n), _tok_spec(n), _tok_spec(n),
                   _cache_spec(DIFF_HEADS, 2, DIFF_HEAD_DIM, SEQ), _cache_spec(DIFF_HEADS, SEQ, 2 * DIFF_HEAD_DIM)],
        out_shape=[jax.ShapeDtypeStruct((N_TOK, n), BF16)] * 3
                  + [_cache_shape(DIFF_HEADS, 2, DIFF_HEAD_DIM, SEQ),
                     _cache_shape(DIFF_HEADS, SEQ, 2 * DIFF_HEAD_DIM)],
        scratch_shapes=[pltpu.VMEM(w.shape, BF16)],
        compiler_params=_cparams(1),
        name="proj_diff",
    )(h, w, qg.array, kg.array, _group_mean_matrix(DIFF_HEAD_DIM), *tables)


def _dup_halves(yc):
    lo = _lane_lo(yc.shape)
    sw = pltpu.roll(yc, HALF, 1)
    return jnp.where(lo, yc, sw), jnp.where(lo, sw, yc)


def _proj_swa_kernel(h_ref, w_ref, qg_ref, kg_ref, m_ref,
                     cos_ref, sp_ref, sn_ref, q_ref, kd_ref, vd_ref, ck_ref, cv_ref, w_s):
    i = pl.program_id(0)
    _cast_once(i, w_ref, w_s)
    h = h_ref[...]
    per = PROJ_UNIT // LANES
    nq = SWA_HEADS * SWA_HEAD_DIM // PROJ_UNIT
    nk = SWA_KV_HEADS * SWA_HEAD_DIM // PROJ_UNIT

    def body(lat):
        rope = _rope_args(lat, cos_ref, sp_ref, sn_ref, SWA_HEAD_DIM)

        def emit(u, y):
            if u < nq + nk:
                y = _head_norm(y, m_ref, qg_ref[...] if u < nq else kg_ref[...])
            for t, yc in enumerate(_halves(y)):
                if u < nq:
                    _put(q_ref, u * per + t, _maybe_rope(yc, rope))
                    continue
                if u < nq + nk:
                    j, c_ref, d_ref = (u - nq) * per + t, ck_ref, kd_ref
                    yc = _maybe_rope(yc, rope)
                else:
                    j, c_ref, d_ref = (u - nq - nk) * per + t, cv_ref, vd_ref
                for a, dup in enumerate(_dup_halves(yc)):
                    _put(d_ref, 2 * j + a, dup)
                if not lat:
                    _cache_rows_t(c_ref, [[2 * j], [2 * j + 1]], yc)

        _matmul_units(h, w_s, nq + 2 * nk, PROJ_UNIT, emit)

    _by_tile_kind(i, body)


def _proj_swa(h, w, qg, kg, tables):
    nq, nk = SWA_HEADS * SWA_HEAD_DIM, SWA_KV_HEADS * SWA_HEAD_DIM
    return pl.pallas_call(
        _proj_swa_kernel,
        grid=(N_PROJ_TILES,),
        in_specs=[_tok_spec(D_MODEL), _const_spec(w.shape), qg.spec, kg.spec,
                  _UNIT_MAT_SPEC, _ROPE_SPEC, _ROPE_SPEC, _ROPE_SPEC],
        out_specs=[_tok_spec(nq), _tok_spec(2 * nk), _tok_spec(2 * nk),
                   _cache_spec(SWA_KV_HEADS, SWA_HEAD_DIM, SEQ), _cache_spec(SWA_KV_HEADS, SWA_HEAD_DIM, SEQ)],
        out_shape=[jax.ShapeDtypeStruct((N_TOK, nq), BF16),
                   jax.ShapeDtypeStruct((N_TOK, 2 * nk), BF16),
                   jax.ShapeDtypeStruct((N_TOK, 2 * nk), BF16),
                   _cache_shape(SWA_KV_HEADS, SWA_HEAD_DIM, SEQ), _cache_shape(SWA_KV_HEADS, SWA_HEAD_DIM, SEQ)],
        scratch_shapes=[pltpu.VMEM(w.shape, BF16)],
        compiler_params=_cparams(1),
        name="proj_swa",
    )(h, w, qg.array, kg.array, _group_mean_matrix(SWA_HEAD_DIM), *tables)


def _mla_lane_matrices():
    everything = lambda lane: lane >= 0
    return (_lane_sum_matrix(everything, everything),
            _lane_sum_matrix(lambda lane: lane < HALF, everything),
            _lane_sum_matrix(lambda lane: lane >= HALF, everything))


def _proj_mla_kernel(h_ref, w_in_ref, qa_ref, kva_ref, w_uq_ref,
                     qg_ref, qgp_ref, all_ref, lo_ref, hi_ref, cos_ref, sp_ref, sn_ref,
                     qn_ref, qp_ref, ckv_ref, kpe_ref, c_ckv_ref, c_kpe_ref):
    i = pl.program_id(0)
    y = _dot(h_ref[...], w_in_ref[...])
    c_q = y[:, :MLA_Q_RANK]
    c_kv = y[:, MLA_Q_RANK:MLA_Q_RANK + MLA_KV_RANK]
    kpe = y[:, MLA_Q_RANK + MLA_KV_RANK:]
    kpe_ref[...] = kpe
    ckv = c_kv * lax.rsqrt(jnp.mean(c_kv * c_kv, axis=-1, keepdims=True) + EPS) * kva_ref[...]
    ckv_ref[...] = ckv.astype(BF16)
    cq = (c_q * lax.rsqrt(jnp.mean(c_q * c_q, axis=-1, keepdims=True) + EPS) * qa_ref[...]).astype(BF16)
    lo = _lane_lo((PROJ_TM, LANES))
    inv_d = 1.0 / (MLA_NOPE + MLA_ROPE)

    def body(lat):
        if not lat:
            _cache_rows(c_ckv_ref, [], ckv)
            _cache_rows_t(c_kpe_ref, [[]], kpe)

        def emit(j, yq):
            pe = yq[:, 2 * LANES:]
            pe_sq = _sq_bf16(pe)
            rs = []
            for a, half_ref in enumerate((lo_ref, hi_ref)):
                nope = yq[:, a * LANES:(a + 1) * LANES]
                ss = _dot(_sq_bf16(nope), all_ref[...]) + _dot(pe_sq, half_ref[...])
                r = lax.rsqrt(ss * inv_d + EPS)
                rs.append(r)
                _put(qn_ref, 2 * j + a, nope * r * qg_ref[...])
            pe = pe * jnp.where(lo, rs[0], rs[1]) * qgp_ref[...]
            if lat:
                pe = _rope(pe, cos_ref[...], sp_ref[...], sn_ref[...], MLA_ROPE // 4)
            _put(qp_ref, j, pe)

        _matmul_units(cq, w_uq_ref, MLA_HEADS // 2, 3 * LANES, emit)

    _by_tile_kind(i, body)


def _proj_mla(h, w_in, qa, kva, w_uq, qg, qgp, tables):
    n_nope = MLA_HEADS * MLA_NOPE
    n_pe = MLA_HEADS * MLA_ROPE
    return pl.pallas_call(
        _proj_mla_kernel,
        grid=(N_PROJ_TILES,),
        in_specs=[_tok_spec(D_MODEL),
                  _const_spec(w_in.shape), qa.spec, kva.spec,
                  _const_spec(w_uq.shape), qg.spec, qgp.spec,
                  _LANE_MAT_SPEC, _LANE_MAT_SPEC, _LANE_MAT_SPEC,
                  _ROPE_SPEC, _ROPE_SPEC, _ROPE_SPEC],
        out_specs=[_tok_spec(n_nope), _tok_spec(n_pe), _tok_spec(MLA_KV_RANK), _tok_spec(LANES),
                   _cache_spec(SEQ, MLA_KV_RANK), _cache_spec(MLA_ROPE, SEQ)],
        out_shape=[jax.ShapeDtypeStruct((N_TOK, n_nope), BF16),
                   jax.ShapeDtypeStruct((N_TOK, n_pe), BF16),
                   jax.ShapeDtypeStruct((N_TOK, MLA_KV_RANK), BF16),
                   jax.ShapeDtypeStruct((N_TOK, LANES), F32),
                   _cache_shape(SEQ, MLA_KV_RANK), _cache_shape(MLA_ROPE, SEQ)],
        compiler_params=_cparams(1),
        name="proj_mla",
    )(h, w_in, qa.array, kva.array, w_uq, qg.array, qgp.array, *_mla_lane_matrices(), *tables)


def _mla_expand_kernel(ckv_ref, kpe_ref, w_ref, kg_ref, kgp_ref, sum_ref, lo_ref,
                       cos_ref, sp_ref, sn_ref, kn_ref, kp_ref, v_ref, *, rope):
    i = pl.program_id(0)
    ckv = ckv_ref[...].astype(BF16)
    kpe = kpe_ref[...]
    pe_ss = _dot(_sq_bf16(kpe), lo_ref[...])
    pe_ss = jnp.concatenate([pe_ss, pe_ss], axis=1)
    lo = _lane_lo(kpe.shape)
    inv_d = 1.0 / (MLA_NOPE + MLA_ROPE)

    def body(lat):
        def emit(j, y):
            kn = jnp.concatenate([y[:, :LANES], y[:, 2 * LANES:3 * LANES]], axis=1)
            r = lax.rsqrt((_dot(_sq_bf16(kn), sum_ref[...]) + pe_ss) * inv_d + EPS)
            kn = kn * r * kg_ref[...]
            for a in range(2):
                _put(kn_ref, 2 * j + a, kn[:, a * LANES:(a + 1) * LANES])
                _put(v_ref, 2 * j + a, y[:, (2 * a + 1) * LANES:(2 * a + 2) * LANES])
            pe = kpe * jnp.where(lo, r[:, :LANES], r[:, LANES:]) * kgp_ref[...]
            if lat:
                pe = _rope(pe, cos_ref[...], sp_ref[...], sn_ref[...], MLA_ROPE // 4)
            _put(kp_ref, j, pe)

        _matmul_units(ckv, w_ref, MLA_HEADS // 2, 4 * LANES, emit)

    if rope:
        _by_tile_kind(i, body)
    else:
        body(False)


def _mla_expand(ckv, kpe_dup, w_ukv, kg, kgp, tables, rope):
    n = ckv.shape[0]
    n_nope = MLA_HEADS * MLA_NOPE
    n_pe = MLA_HEADS * MLA_ROPE
    _, m_lo, _ = _mla_lane_matrices()
    return pl.pallas_call(
        functools.partial(_mla_expand_kernel, rope=rope),
        grid=(n // PROJ_TM,),
        in_specs=[_tok_spec(MLA_KV_RANK), _tok_spec(LANES), _const_spec(w_ukv.shape),
                  kg.spec, kgp.spec, _UNIT_MAT_SPEC, _LANE_MAT_SPEC,
                  _ROPE_SPEC, _ROPE_SPEC, _ROPE_SPEC],
        out_specs=[_tok_spec(n_nope), _tok_spec(n_pe), _tok_spec(n_nope)],
        out_shape=[jax.ShapeDtypeStruct((n, n_nope), BF16),
                   jax.ShapeDtypeStruct((n, n_pe), BF16),
                   jax.ShapeDtypeStruct((n, n_nope), BF16)],
        compiler_params=_cparams(1),
        name="mla_expand",
    )(ckv, kpe_dup, w_ukv, kg.array, kgp.array, _group_sum_matrix(), m_lo, *tables)


def _prompt_spec(width):
    return pl.BlockSpec((PROMPT_SEQS * TM, width), lambda b: (b, 0))


def _latq_spec(rows, width):
    per = DEC_SEQ // rows
    return pl.BlockSpec((rows, width), lambda b, t: (N_PROMPT_TOK // rows + b * per + t, 0))


def _latkv_spec(width):
    return pl.BlockSpec((DEC_SEQ, width), lambda b, t: (LAT_BLOCK0 + b, 0))


def _lato_spec(rows):
    per = DEC_SEQ // rows
    return pl.BlockSpec((rows, D_MODEL), lambda b, t: (b * per + t, 0))


def _att_kernel(*refs, with_ctx, seqs):
    if with_ctx:
        q_ref, k_ref, v_ref, kc_ref, vc_ref, o_ref = refs
    else:
        q_ref, k_ref, v_ref, o_ref = refs
    tq = q_ref.shape[0] // seqs
    nu = ATT_UNIT_HEADS
    per_kv = ATT_HEADS // ATT_KV_HEADS // nu

    def make(views):
        q_v, k_v, v_v, o_v = views

        def scores(u):
            q = jnp.concatenate([_chunk(q_v, u * nu + g) for g in range(nu)], axis=0)
            s_list = [_dot_nt(_chunk(k_v, u // per_kv), q)]
            if with_ctx:
                s_list.append(_dot_nt(kc_ref[u // per_kv].astype(BF16), q))
            return s_list

        def finish(u, s_list):
            values = [_chunk(v_v, u // per_kv)]
            if with_ctx:
                values.append(vc_ref[u // per_kv].astype(BF16))
            ps, inv = _softmax2_parts(s_list)
            o = _pv(ps, values) * inv
            for g in range(nu):
                o_v[:, (u * nu + g) * LANES:(u * nu + g + 1) * LANES] = (
                    o[:, g * tq:(g + 1) * tq].T.astype(o_v.dtype))

        return scores, finish

    _seq_pipeline((q_ref, k_ref, v_ref, o_ref), seqs, ATT_HEADS // nu, make)


def _att_attend(q, k, v, cache_k, cache_v):
    nk = ATT_KV_HEADS * ATT_HEAD_DIM
    out_p = pl.pallas_call(
        functools.partial(_att_kernel, with_ctx=False, seqs=PROMPT_SEQS),
        grid=(N_PROMPT_TILES // PROMPT_SEQS,),
        in_specs=[_prompt_spec(D_MODEL), _prompt_spec(nk), _prompt_spec(nk)],
        out_specs=_prompt_spec(D_MODEL),
        out_shape=jax.ShapeDtypeStruct((N_PROMPT_TOK, D_MODEL), BF16),
        compiler_params=_cparams(1),
        name="att_prompt",
    )(q, k, v)
    ctx = pl.BlockSpec((None, None, ATT_KV_HEADS, PAST_LEN, LANES), lambda b, t: (b, 0, 0, 0, 0))
    out_s = pl.pallas_call(
        functools.partial(_att_kernel, with_ctx=True, seqs=1),
        grid=(DEC_BATCH, DEC_SEQ // LAT_TQ),
        in_specs=[_latq_spec(LAT_TQ, D_MODEL), _latkv_spec(nk), _latkv_spec(nk), ctx, ctx],
        out_specs=_lato_spec(LAT_TQ),
        out_shape=jax.ShapeDtypeStruct((N_LAT_TOK, D_MODEL), BF16),
        compiler_params=_cparams(2),
        name="att_latent",
    )(q, k, v, cache_k, cache_v)
    return out_p, out_s


def _diff_kernel(*refs, lam_init, with_ctx, seqs):
    if with_ctx:
        (q_ref, k_ref, v_ref, kc_ref, vc_ref, lq1_ref, lk1_ref, lq2_ref, lk2_ref, sub_ref, o_ref) = refs
    else:
        (q_ref, k_ref, v_ref, lq1_ref, lk1_ref, lq2_ref, lk2_ref, sub_ref, o_ref) = refs
    tq = q_ref.shape[0] // seqs
    lam = (jnp.exp(jnp.sum(lq1_ref[...] * lk1_ref[...], axis=-1, keepdims=True))
           - jnp.exp(jnp.sum(lq2_ref[...] * lk2_ref[...], axis=-1, keepdims=True)) + lam_init)
    diag = (lax.broadcasted_iota(jnp.int32, (LANES, LANES), 0)
            == lax.broadcasted_iota(jnp.int32, (LANES, LANES), 1))
    sub = jnp.sum(jnp.where(diag, sub_ref[...] * (1.0 - lam_init), 0.0), axis=1, keepdims=True)

    def make(views):
        q_v, k_v, v_v, o_v = views

        def scores(hd):
            q = jnp.concatenate(_split_halves(_chunk(q_v, hd)), axis=0)
            s_list = [_dot_nt(_chunk(k_v, hd), q)]
            if with_ctx:
                s_list.append(_dot_nt(kc_ref[hd].astype(BF16), q))
            return s_list

        def finish(hd, s_list):
            values = [_chunk(v_v, hd)]
            if with_ctx:
                values.append(vc_ref[hd].astype(BF16))
            ps, inv = _softmax2_parts(s_list)
            o = (_pv([p[:, :tq] for p in ps], values) * inv[:, :tq]
                 - _pv([p[:, tq:] for p in ps], values) * (lam * inv[:, tq:]))
            o = o * lax.rsqrt(jnp.mean(o * o, axis=0, keepdims=True) + EPS) * sub
            o_v[:, hd * LANES:(hd + 1) * LANES] = o.T.astype(o_v.dtype)

        return scores, finish

    _seq_pipeline((q_ref, k_ref, v_ref, o_ref), seqs, DIFF_HEADS, make)


def _diff_attend(q, k, v, cache_k_pair, cache_v, lq1, lk1, lq2, lk2, subln, lam_init):
    small_specs = [p.spec for p in (lq1, lk1, lq2, lk2, subln)]
    small = [p.array for p in (lq1, lk1, lq2, lk2, subln)]
    out_p = pl.pallas_call(
        functools.partial(_diff_kernel, lam_init=lam_init, with_ctx=False, seqs=PROMPT_SEQS),
        grid=(N_PROMPT_TILES // PROMPT_SEQS,),
        in_specs=[_prompt_spec(D_MODEL)] * 3 + small_specs,
        out_specs=_prompt_spec(D_MODEL),
        out_shape=jax.ShapeDtypeStruct((N_PROMPT_TOK, D_MODEL), BF16),
        compiler_params=_cparams(1),
        name="diff_prompt",
    )(q, k, v, *small)
    out_s = pl.pallas_call(
        functools.partial(_diff_kernel, lam_init=lam_init, with_ctx=True, seqs=1),
        grid=(DEC_BATCH, DEC_SEQ // LAT_TQ),
        in_specs=[_latq_spec(LAT_TQ, D_MODEL), _latkv_spec(D_MODEL), _latkv_spec(D_MODEL),
                  pl.BlockSpec((None, DIFF_HEADS, PAST_LEN, LANES), lambda b, t: (b, 0, 0, 0)),
                  pl.BlockSpec((None, None, DIFF_HEADS, PAST_LEN, LANES), lambda b, t: (b, 0, 0, 0, 0))]
                 + small_specs,
        out_specs=_lato_spec(LAT_TQ),
        out_shape=jax.ShapeDtypeStruct((N_LAT_TOK, D_MODEL), BF16),
        compiler_params=_cparams(2),
        name="diff_latent",
    )(q, k, v, cache_k_pair, cache_v, *small)
    return out_p, out_s


def _swa_pipeline(q_ref, o_ref, seq_refs, sink_ref, score_fns, value_fns, seqs=1):
    tq = q_ref.shape[0] // seqs
    per_kv = SWA_HEADS // SWA_KV_HEADS // 2
    first = lax.broadcasted_iota(jnp.int32, (LANES, tq), 0) < HALF

    def make(views):
        q_v, o_v = views[:2]
        kv_views = views[2:]

        def scores(c):
            q = jnp.concatenate(_split_halves(_chunk(q_v, c)), axis=0)
            return [fn(kv_views, c // per_kv, q) for fn in score_fns]

        def finish(c, s_list):
            sink = jnp.concatenate([jnp.full((1, tq), sink_ref[2 * c + a] * LOG2E, F32) for a in range(2)],
                                   axis=1)
            ps, inv = _softmax2_parts(s_list, extra=sink)
            o = _pv(ps, [fn(kv_views, c // per_kv) for fn in value_fns]) * inv
            oc = jnp.where(first, o[:, :tq], o[:, tq:])
            o_v[:, c * LANES:(c + 1) * LANES] = oc.T.astype(o_v.dtype)

        return scores, finish

    _seq_pipeline((q_ref, o_ref) + tuple(seq_refs), seqs, SWA_HEADS // 2, make)


def _swa_prompt_kernel(sink_ref, q_ref, k_ref, v_ref, o_ref):
    _swa_pipeline(q_ref, o_ref, (k_ref, v_ref), sink_ref,
                  [lambda kv_v, kv, q: _dot_nt(_chunk(kv_v[0], kv), q)],
                  [lambda kv_v, kv: _chunk(kv_v[1], kv)], seqs=PROMPT_SEQS)


def _swa_latent_kernel(sink_ref, q_ref, k_ref, v_ref, kc_ref, vc_ref, o_ref):
    n = pl.program_id(1)
    tq = q_ref.shape[0]
    span = SWA_QB + 2 * WINDOW
    start = pl.multiple_of(jnp.clip(n * SWA_QB - WINDOW, 0, DEC_SEQ - span), WINDOW)
    cols = lax.broadcasted_iota(jnp.int32, (span, 2 * tq), 1)
    qpos = n * SWA_QB + jnp.bitwise_and(cols, tq - 1)
    kpos = start + lax.broadcasted_iota(jnp.int32, (span, 2 * tq), 0)
    valid = jnp.abs(qpos - kpos) <= WINDOW

    def local(ref, kv):
        return ref[pl.ds(start, span), kv * LANES:(kv + 1) * LANES]

    _swa_pipeline(q_ref, o_ref, (), sink_ref,
                  [lambda _, kv, q: jnp.where(valid, _dot_nt(local(k_ref, kv), q), -1e30),
                   lambda _, kv, q: _dot_nt(kc_ref[kv], q)],
                  [lambda _, kv: local(v_ref, kv), lambda _, kv: vc_ref[kv]])


def _swa_attend(q, kd, vd, cache_kd, cache_vd, sink):
    nkd = 2 * SWA_KV_HEADS * SWA_HEAD_DIM
    smem = pl.BlockSpec(memory_space=pltpu.SMEM)
    out_p = pl.pallas_call(
        _swa_prompt_kernel,
        grid=(N_PROMPT_TILES // PROMPT_SEQS,),
        in_specs=[smem, _prompt_spec(D_MODEL), _prompt_spec(nkd), _prompt_spec(nkd)],
        out_specs=_prompt_spec(D_MODEL),
        out_shape=jax.ShapeDtypeStruct((N_PROMPT_TOK, D_MODEL), BF16),
        compiler_params=_cparams(1),
        name="swa_prompt",
    )(sink, q, kd, vd)
    ctx = pl.BlockSpec((None, SWA_KV_HEADS, PAST_LEN, LANES), lambda b, n: (b, 0, 0, 0))
    out_s = pl.pallas_call(
        _swa_latent_kernel,
        grid=(DEC_BATCH, DEC_SEQ // SWA_QB),
        in_specs=[smem, _latq_spec(SWA_QB, D_MODEL), _latkv_spec(nkd), _latkv_spec(nkd), ctx, ctx],
        out_specs=_lato_spec(SWA_QB),
        out_shape=jax.ShapeDtypeStruct((N_LAT_TOK, D_MODEL), BF16),
        compiler_params=_cparams(2),
        name="swa_latent",
    )(sink, q, kd, vd, cache_kd, cache_vd)
    return out_p, out_s


def _mla_kernel(*refs, with_ctx, seqs):
    if with_ctx:
        (qn_ref, qp_ref, kn_ref, kp_ref, v_ref, knc_ref, kpc_ref, vc_ref, o_ref) = refs
    else:
        (qn_ref, qp_ref, kn_ref, kp_ref, v_ref, o_ref) = refs

    def make(views):
        qn_v, qp_v, kn_v, kp_v, v_v, o_v = views

        def scores(hd):
            j, a = hd // 2, hd % 2
            q = jnp.concatenate([_chunk(qn_v, hd), _split_halves(_chunk(qp_v, j))[a]], axis=1)
            s_list = [_dot_nt(jnp.concatenate([_chunk(kn_v, hd), _chunk(kp_v, j)], axis=1), q)]
            if with_ctx:
                s_list.append(_dot_nt(jnp.concatenate([_chunk(knc_ref, hd), _chunk(kpc_ref, j)], axis=1), q))
            return s_list

        def finish(hd, s_list):
            values = [_chunk(v_v, hd)]
            if with_ctx:
                values.append(_chunk(vc_ref, hd))
            ps, inv = _softmax2_parts(s_list)
            o_v[:, hd * LANES:(hd + 1) * LANES] = (_pv(ps, values) * inv).T.astype(o_v.dtype)

        return scores, finish

    _seq_pipeline((qn_ref, qp_ref, kn_ref, kp_ref, v_ref, o_ref), seqs, MLA_HEADS, make)


def _mla_attend(qn, qp, kn, kp, v, knc, kpc, vc):
    n_pe = MLA_HEADS * MLA_ROPE
    out_p = pl.pallas_call(
        functools.partial(_mla_kernel, with_ctx=False, seqs=PROMPT_SEQS),
        grid=(N_PROMPT_TILES // PROMPT_SEQS,),
        in_specs=[_prompt_spec(D_MODEL), _prompt_spec(n_pe), _prompt_spec(D_MODEL), _prompt_spec(n_pe),
                  _prompt_spec(D_MODEL)],
        out_specs=_prompt_spec(D_MODEL),
        out_shape=jax.ShapeDtypeStruct((N_PROMPT_TOK, D_MODEL), BF16),
        compiler_params=_cparams(1),
        name="mla_prompt",
    )(qn, qp, kn, kp, v)

    def ctx(width):
        return pl.BlockSpec((PAST_LEN, width), lambda b, t: (b, 0))

    out_s = pl.pallas_call(
        functools.partial(_mla_kernel, with_ctx=True, seqs=1),
        grid=(DEC_BATCH, DEC_SEQ // LAT_TQ),
        in_specs=[_latq_spec(LAT_TQ, D_MODEL), _latq_spec(LAT_TQ, n_pe),
                  _latkv_spec(D_MODEL), _latkv_spec(n_pe), _latkv_spec(D_MODEL),
                  ctx(D_MODEL), ctx(n_pe), ctx(D_MODEL)],
        out_specs=_lato_spec(LAT_TQ),
        out_shape=jax.ShapeDtypeStruct((N_LAT_TOK, D_MODEL), BF16),
        compiler_params=_cparams(2),
        name="mla_latent",
    )(qn, qp, kn, kp, v, knc, kpc, vc)
    return out_p, out_s


def _omlp_kernel(*refs, first, last):
    refs = list(refs)
    ap_ref, as_ref, wo_ref = refs[:3]
    x_refs = refs[3:5] if first else refs[3:4]
    refs = refs[3 + len(x_refs):]
    g1_ref, gain_ref, sh_ref, sc_ref, g2_ref, w1c_ref, w2c_ref = refs[:7]
    refs = refs[7:]
    if last:
        op_ref, os_ref, wo_s, w1_s, w2_s = refs
    else:
        ngain_ref, nsh_ref, nsc_ref, o_ref, hn_ref, wo_s, w1_s, w2_s = refs
    s = pl.program_id(0)
    per = MLP_FF_CHUNK // MLP_LOAD_COLS
    n_chunks = D_FF // MLP_FF_CHUNK
    half = MLP_TM // 2

    @pl.when(s == 0)
    def _():
        wo_s[...] = wo_ref[...].astype(BF16)

    for part in range(per):
        @pl.when((s < N_LOAD_STEPS) & (s % per == part))
        def _(part=part):
            w1_s[s // per, :, part * MLP_LOAD_COLS:(part + 1) * MLP_LOAD_COLS] = w1c_ref[...].astype(BF16)

    @pl.when(s < N_LOAD_STEPS)
    def _():
        w2_s[s // per, pl.ds(pl.multiple_of((s % per) * MLP_LOAD_COLS, MLP_LOAD_COLS), MLP_LOAD_COLS), :] = (
            w2c_ref[...].astype(BF16))

    @pl.when(s >= N_LOAD_STEPS)
    def _():
        t = s - N_LOAD_STEPS
        is_prompt = t < N_MLP_PROMPT_TILES
        grp = _tile_group(t, MLP_TM)

        def mod(ref):
            return ref[pl.ds(grp, 1), :]

        rows = [slice(r * half, (r + 1) * half) for r in range(2)]
        o = [_dot(jnp.where(is_prompt, ap_ref[rw, :], as_ref[rw, :]), wo_s[...]) for rw in rows]
        x1, h, u0 = [], [], []
        for r, rw in enumerate(rows):
            x = jnp.where(is_prompt, x_refs[0][rw, :], x_refs[1][rw, :]) if first else x_refs[0][rw, :]
            x1.append(x + mod(g1_ref) * o[r])
            h.append(_norm_mod(x1[r], gain_ref[...], mod(sh_ref), mod(sc_ref)).astype(BF16))
            u0.append(_dot(h[r], w1_s[0]))
        h = jnp.concatenate(h, axis=0)
        acc = []

        def up(c):
            return jnp.concatenate(u0, axis=0) if c == 0 else _dot(h, w1_s[c])

        def down(c, u):
            u = jnp.square(jnp.maximum(u, 0.0)).astype(BF16)
            if c + 1 < n_chunks:
                y = _dot(u, w2_s[c])
                acc[:] = [y if not acc else acc[0] + y]
            else:
                acc[:] = [acc[0][rw] + _dot(u[rw], w2_s[c]) for rw in rows]

        _head_pipeline(n_chunks, up, down)
        for r, rw in enumerate(rows):
            out = x1[r] + mod(g2_ref) * acc[r]
            if last:
                @pl.when(is_prompt)
                def _(out=out, rw=rw):
                    op_ref[rw, :] = out

                @pl.when(jnp.logical_not(is_prompt))
                def _(out=out, rw=rw):
                    os_ref[rw, :] = out
            else:
                o_ref[rw, :] = out
                hn_ref[rw, :] = _norm_mod(out, ngain_ref[...], mod(nsh_ref), mod(nsc_ref)).astype(BF16)


def _omlp(attn_p, attn_s, w_o, x, mods, gain_ffn, w1_all, w2_all, layer, next_gain):
    first, last = layer == 0, next_gain is None
    n_lat_tiles = N_LAT_TOK // MLP_TM

    def tok(s):
        return jnp.maximum(s - N_LOAD_STEPS, 0)

    p_spec = pl.BlockSpec((MLP_TM, D_MODEL), lambda s: (jnp.minimum(tok(s), N_MLP_PROMPT_TILES - 1), 0))
    l_spec = pl.BlockSpec((MLP_TM, D_MODEL),
                          lambda s: (jnp.clip(tok(s) - N_MLP_PROMPT_TILES, 0, n_lat_tiles - 1), 0))
    w1_spec = pl.BlockSpec((None, D_MODEL, MLP_LOAD_COLS),
                           lambda s: (layer, 0, jnp.minimum(s, N_LOAD_STEPS - 1)))
    w2_spec = pl.BlockSpec((None, MLP_LOAD_COLS, D_MODEL),
                           lambda s: (layer, jnp.minimum(s, N_LOAD_STEPS - 1), 0))
    t_spec = pl.BlockSpec((MLP_TM, D_MODEL), lambda s: (tok(s), 0))
    n_chunks = D_FF // MLP_FF_CHUNK
    split = ([p_spec, l_spec], [jax.ShapeDtypeStruct((N_PROMPT_TOK, D_MODEL), F32),
                                jax.ShapeDtypeStruct((N_LAT_TOK, D_MODEL), F32)])
    in_specs = ([p_spec, l_spec, _const_spec(w_o.shape)] + (split[0] if first else [t_spec])
                + [_mod_spec(layer, 2), gain_ffn.spec, _mod_spec(layer, 3), _mod_spec(layer, 4),
                   _mod_spec(layer, 5),
                   w1_spec, w2_spec])
    args = ([attn_p, attn_s, w_o] + (list(x) if first else [x])
            + [mods, gain_ffn.array, mods, mods, mods, w1_all, w2_all])
    if last:
        out_specs, out_shape = split
    else:
        in_specs += [next_gain.spec, _mod_spec(layer + 1, 0), _mod_spec(layer + 1, 1)]
        args += [next_gain.array, mods, mods]
        out_specs = [t_spec, t_spec]
        out_shape = [jax.ShapeDtypeStruct((N_TOK, D_MODEL), F32), jax.ShapeDtypeStruct((N_TOK, D_MODEL), BF16)]
    return pl.pallas_call(
        functools.partial(_omlp_kernel, first=first, last=last),
        grid=(N_LOAD_STEPS + N_TOK // MLP_TM,),
        in_specs=in_specs,
        out_specs=out_specs,
        out_shape=out_shape,
        scratch_shapes=[pltpu.VMEM((D_MODEL, D_MODEL), BF16),
                        pltpu.VMEM((n_chunks, D_MODEL, MLP_FF_CHUNK), BF16),
                        pltpu.VMEM((n_chunks, MLP_FF_CHUNK, D_MODEL), BF16)],
        compiler_params=_cparams(1),
        name="omlp",
    )(*args)


def kernel(x_prompt, x_sample, cache_att_k, cache_att_v, cache_diff_k, cache_diff_v, cache_swa_k, cache_swa_v, cache_mla_ckv, cache_mla_kpe, c, c_ctx, ada_w, ada_b, norm_mix, norm_ffn, att_w_qkv, att_q_norm, att_k_norm, att_w_o, diff_w_qkv, diff_q_norm, diff_k_norm, diff_lq1, diff_lk1, diff_lq2, diff_lk2, diff_subln, diff_w_o, swa_w_qkv, swa_q_norm, swa_k_norm, swa_sink, swa_w_o, mla_w_in, mla_q_a_norm, mla_kv_a_norm, mla_w_uq, mla_w_ukv, mla_q_norm, mla_k_norm, mla_w_o, mlp_w1, mlp_w2):
    xp = x_prompt.reshape(N_PROMPT_TOK, D_MODEL)
    xs = x_sample.reshape(N_LAT_TOK, D_MODEL)
    cond = jnp.concatenate([c_ctx[None], c, jnp.zeros((COND_ROWS - 1 - DEC_BATCH, D_MODEL), F32)], axis=0)
    mods_all = _modulation(cond, ada_w, ada_b)

    tab_att = _rope_tables(ATT_HEAD_DIM)
    tab_64 = _rope_tables(DIFF_HEAD_DIM)

    pk = _ParamPack()
    g_mix = [pk.add(norm_mix[l]) for l in range(DEPTH)]
    g_ffn = [pk.add(norm_ffn[l]) for l in range(DEPTH)]
    att_qg = pk.add(att_q_norm[0], PROJ_UNIT // ATT_HEAD_DIM, ATT_HEAD_DIM ** -0.5 * LOG2E)
    att_kg = pk.add(att_k_norm[0], PROJ_UNIT // ATT_HEAD_DIM)
    diff_qg = pk.add(diff_q_norm[0], PROJ_UNIT // DIFF_HEAD_DIM, DIFF_HEAD_DIM ** -0.5 * LOG2E)
    diff_kg = pk.add(diff_k_norm[0], PROJ_UNIT // DIFF_HEAD_DIM)
    diff_small = [pk.add(v[0]) for v in (diff_lq1, diff_lk1, diff_lq2, diff_lk2, diff_subln)]
    swa_qg = pk.add(swa_q_norm[0], PROJ_UNIT // SWA_HEAD_DIM, SWA_HEAD_DIM ** -0.5 * LOG2E)
    swa_kg = pk.add(swa_k_norm[0], PROJ_UNIT // SWA_HEAD_DIM)
    mla_qs = (MLA_NOPE + MLA_ROPE) ** -0.5 * LOG2E
    mla_qa = pk.add(mla_q_a_norm[0])
    mla_kva = pk.add(mla_kv_a_norm[0])
    mla_qg = pk.add(mla_q_norm[0][:MLA_NOPE], 1, mla_qs)
    mla_qgp = pk.add(mla_q_norm[0][MLA_NOPE:], LANES // MLA_ROPE, mla_qs)
    mla_kg = pk.add(mla_k_norm[0][:MLA_NOPE], PROJ_UNIT // MLA_NOPE)
    mla_kgp = pk.add(mla_k_norm[0][MLA_NOPE:], LANES // MLA_ROPE)
    pk.build()

    outs = {}
    x = (xp, xs)
    for layer in range(DEPTH):
        gain_ffn = g_ffn[layer]
        if layer == 0:
            q, k, v, outs["att_k"], outs["att_v"] = _proj_att(
                xp, xs, mods_all, g_mix[layer], att_w_qkv[0], att_qg, att_kg, tab_att)
            attn_p, attn_s = _att_attend(q, k, v, cache_att_k, cache_att_v)
            w_o = att_w_o[0]
        elif layer == 1:
            q, k, v, outs["diff_k"], outs["diff_v"] = _proj_diff(h, diff_w_qkv[0], diff_qg, diff_kg, tab_64)
            lam_init = 0.8 - 0.6 * math.exp(-0.3 * layer)
            ck = cache_diff_k[:, 0].transpose(0, 1, 3, 2, 4).reshape(
                DEC_BATCH, DIFF_HEADS, PAST_LEN, LANES)
            attn_p, attn_s = _diff_attend(q, k, v, ck, cache_diff_v, *diff_small, lam_init)
            w_o = diff_w_o[0]
        elif layer == 2:
            q, kd, vd, outs["swa_k"], outs["swa_v"] = _proj_swa(h, swa_w_qkv[0], swa_qg, swa_kg, tab_64)
            ckd = jnp.concatenate([cache_swa_k[:, 0]] * 2, axis=-1).astype(BF16)
            cvd = jnp.concatenate([cache_swa_v[:, 0]] * 2, axis=-1).astype(BF16)
            attn_p, attn_s = _swa_attend(q, kd, vd, ckd, cvd, swa_sink[0].astype(F32))
            w_o = swa_w_o[0]
        else:
            w_in = mla_w_in[0]
            w_in = jnp.concatenate([w_in, w_in[:, -MLA_ROPE:]], axis=1).astype(BF16)
            w_uq = mla_w_uq[0].reshape(MLA_Q_RANK, MLA_HEADS // 2, 2, MLA_NOPE + MLA_ROPE)
            w_uq = jnp.concatenate([w_uq[..., :MLA_NOPE].reshape(MLA_Q_RANK, MLA_HEADS // 2, 2 * MLA_NOPE),
                                    w_uq[..., MLA_NOPE:].reshape(MLA_Q_RANK, MLA_HEADS // 2, 2 * MLA_ROPE)],
                                   axis=-1).reshape(MLA_Q_RANK, -1).astype(BF16)
            w_ukv = mla_w_ukv[0].astype(BF16)
            qn, qp, ckv, kpe, outs["mla_ckv"], outs["mla_kpe"] = _proj_mla(
                h, w_in, mla_qa, mla_kva, w_uq, mla_qg, mla_qgp, tab_64)
            kn, kp, vv = _mla_expand(ckv, kpe, w_ukv, mla_kg, mla_kgp, tab_64, True)
            c_ckv = cache_mla_ckv[:, 0].reshape(DEC_BATCH * PAST_LEN, MLA_KV_RANK)
            c_kpe = cache_mla_kpe[:, 0].reshape(DEC_BATCH * PAST_LEN, MLA_ROPE)
            c_kpe = jnp.concatenate([c_kpe, c_kpe], axis=-1)
            knc, kpc, vc = _mla_expand(c_ckv, c_kpe, w_ukv, mla_kg, mla_kgp, tab_64, False)
            attn_p, attn_s = _mla_attend(qn, qp, kn, kp, vv, knc, kpc, vc)
            w_o = mla_w_o[0]
        if layer + 1 < DEPTH:
            x, h = _omlp(attn_p, attn_s, w_o, x, mods_all, gain_ffn, mlp_w1, mlp_w2, layer,
                         g_mix[layer + 1])
        else:
            xp, xs = _omlp(attn_p, attn_s, w_o, x, mods_all, gain_ffn, mlp_w1, mlp_w2, layer, None)

    y_prompt = xp.reshape(BATCH, SEQ, D_MODEL)
    y_sample = xs.reshape(DEC_BATCH, DEC_SEQ, D_MODEL)
    for name in ("diff_k", "swa_k", "swa_v", "mla_kpe"):
        outs[name] = jnp.swapaxes(outs[name], -1, -2)
    return (y_prompt, y_sample, outs["att_k"], outs["att_v"], outs["diff_k"], outs["diff_v"],
            outs["swa_k"], outs["swa_v"], outs["mla_ckv"], outs["mla_kpe"])
```

```python
import functools
import math

import numpy as np
import jax
import jax.numpy as jnp
from jax import lax
from jax.experimental import pallas as pl
from jax.experimental.pallas import tpu as pltpu

D_MODEL = 1024
BATCH = 16
SEQ = 256
DEPTH = 4
DEC_BATCH = 2
DEC_SEQ = 1024
PAST_LEN = 256
GRID_W = 64
ROPE_THETA = 10000.0
EPS = 1e-6
D_FF = 4 * D_MODEL
MOD_CHUNKS = 6
LOG2E = 1.4426950408889634

ATT_HEADS, ATT_KV_HEADS, ATT_HEAD_DIM = 8, 2, 128
DIFF_HEADS, DIFF_HEAD_DIM = 8, 64
SWA_HEADS, SWA_KV_HEADS, SWA_HEAD_DIM, WINDOW = 16, 4, 64, 128
MLA_HEADS, MLA_NOPE, MLA_ROPE, MLA_VDIM = 8, 128, 64, 128
MLA_Q_RANK, MLA_KV_RANK = 512, 256

LANES = 128
HALF = LANES // 2
TM = 256
N_PROMPT_TOK = BATCH * SEQ
N_LAT_TOK = DEC_BATCH * DEC_SEQ
N_TOK = N_PROMPT_TOK + N_LAT_TOK
N_PROMPT_TILES = N_PROMPT_TOK // TM
LAT_TQ = 512
LAT_BLOCK0 = N_PROMPT_TOK // DEC_SEQ
COND_ROWS = 8
PROJ_TM = 512
PROJ_BATCHES = PROJ_TM // SEQ
N_PROJ_TILES = N_TOK // PROJ_TM
N_PROJ_PROMPT = N_PROMPT_TOK // PROJ_TM
PROJ_UNIT = 2 * LANES
MLP_TM = 512
MLP_FF_CHUNK = 512
MLP_LOAD_COLS = 256
N_LOAD_STEPS = D_FF // MLP_LOAD_COLS
N_MLP_PROMPT_TILES = N_PROMPT_TOK // MLP_TM
SWA_QB = 128
ATT_UNIT_HEADS = 4
PROMPT_SEQS = 4
VMEM_LIMIT = 56 * 1024 * 1024

F32 = jnp.float32
BF16 = jnp.bfloat16


def _cparams(n_axes):
    return pltpu.CompilerParams(dimension_semantics=("arbitrary",) * n_axes,
                                vmem_limit_bytes=VMEM_LIMIT)


def _dot(a, b):
    return jnp.dot(a, b, preferred_element_type=F32)


def _dot_nt(a, b):
    return lax.dot_general(a, b, (((1,), (1,)), ((), ())), preferred_element_type=F32)


def _dot_tn(a, b):
    return lax.dot_general(a, b, (((0,), (0,)), ((), ())), preferred_element_type=F32)


def _const_spec(shape):
    nd = len(shape)
    return pl.BlockSpec(shape, lambda *_: (0,) * nd, pipeline_mode=pl.Buffered(1))


class _ParamPack:
    def __init__(self):
        self._rows, self.array = [], None

    def add(self, v, repeat=1, scale=1.0):
        n = v.shape[0] * repeat
        row = _ParamRow(self, -(-n // LANES) * LANES)
        self._rows.append((row, [v] * repeat, scale, n))
        return row

    def build(self):
        pieces, scales, offset = [], [], 0
        for row, vs, scale, n in sorted(self._rows, key=lambda r: -r[0].width):
            row.offset = offset
            pieces += vs + ([jnp.zeros((row.width - n,), F32)] if row.width > n else [])
            scales.append(np.full((row.width,), scale, np.float32))
            offset += row.width
        flat = jnp.concatenate([p.astype(F32) for p in pieces]) * jnp.asarray(np.concatenate(scales))
        self.array = flat.reshape(1, offset)


class _ParamRow:
    def __init__(self, pack, width):
        self.pack, self.width, self.offset = pack, width, None

    @property
    def array(self):
        return self.pack.array

    @property
    def spec(self):
        block = self.offset // self.width
        return pl.BlockSpec((1, self.width), lambda *_: (0, block), pipeline_mode=pl.Buffered(1))


def _chunk(ref, c, width=LANES):
    return ref[:, c * width:(c + 1) * width]


def _put(ref, c, val):
    ref[:, c * LANES:(c + 1) * LANES] = val.astype(ref.dtype)


def _tile_group(i, rows):
    n_prompt = N_PROMPT_TOK // rows
    return jnp.where(i < n_prompt, 0, 1 + (i - n_prompt) // (DEC_SEQ // rows))


def _rope_tile(i):
    return jnp.maximum(i - N_PROJ_PROMPT, 0) % (DEC_SEQ // PROJ_TM)


def _norm_mod(x, gain, shift, scale):
    ms = jnp.mean(x * x, axis=-1, keepdims=True)
    return x * lax.rsqrt(ms + EPS) * (gain * (1.0 + scale)) + shift


def _lane_lo(shape):
    return lax.broadcasted_iota(jnp.int32, shape, len(shape) - 1) < HALF


def _rope(y, cos, sin_prev, sin_next, quarter):
    return (y * cos + pltpu.roll(y, quarter, 1) * sin_prev
            + pltpu.roll(y, LANES - quarter, 1) * sin_next)


def _rope_tables(rot_dim):
    half = rot_dim // 2
    quarter = rot_dim // 4
    inv = np.float32(ROPE_THETA) ** (-np.arange(0, half, 2, dtype=np.float32) / np.float32(half))
    pos = np.arange(DEC_SEQ)
    row = (pos // GRID_W).astype(np.float32)
    col = (pos % GRID_W).astype(np.float32)
    lane = np.arange(LANES)
    dd = lane % rot_dim
    q = dd // quarter
    f = dd % quarter
    ang = np.where((q < 2)[None, :], row[:, None], col[:, None]) * inv[f][None, :]
    ang = ang.astype(np.float32)
    cos = np.cos(ang).astype(np.float32)
    sin = np.sin(ang).astype(np.float32)
    odd = (q % 2 == 1)[None, :]
    sin_prev = np.where(odd, sin, 0.0).astype(np.float32)
    sin_next = np.where(odd, 0.0, -sin).astype(np.float32)
    return jnp.asarray(cos), jnp.asarray(sin_prev), jnp.asarray(sin_next)


def _lane_sum_matrix(rows, cols, value=1.0):
    lane = np.arange(LANES)
    m = np.where(rows(lane)[:, None] & cols(lane)[None, :], value, 0.0).astype(np.float32)
    return jnp.asarray(m, dtype=BF16)


def _group_mean_matrix(group):
    lane = np.arange(PROJ_UNIT)
    m = np.where((lane[:, None] // group) == (lane[None, :] // group), 1.0 / group, 0.0)
    return jnp.asarray(m.astype(np.float32), dtype=BF16)


def _group_sum_matrix():
    lane = np.arange(PROJ_UNIT)
    m = np.where((lane[:, None] // LANES) == (lane[None, :] // LANES), 1.0, 0.0)
    return jnp.asarray(m.astype(np.float32), dtype=BF16)


def _sq_bf16(y):
    return (y * y).astype(BF16)


def _head_norm(y, m_ref, gain):
    return y * lax.rsqrt(_dot(_sq_bf16(y), m_ref[...]) + EPS) * gain


def _halves(y):
    return [y[:, t * LANES:(t + 1) * LANES] for t in range(y.shape[1] // LANES)]


def _matmul_units(h, w_ref, n_units, width, emit):
    def unit(u):
        return _dot(h, w_ref[:, u * width:(u + 1) * width])

    nxt = unit(0)
    for u in range(n_units):
        cur = nxt
        if u + 1 < n_units:
            nxt = unit(u + 1)
        emit(u, cur)


def _cast_once(i, w_ref, w_s):
    @pl.when(i == 0)
    def _():
        w_s[...] = w_ref[...].astype(BF16)


def _by_tile_kind(i, body):
    pl.when(i < N_PROJ_PROMPT)(functools.partial(body, False))
    pl.when(i >= N_PROJ_PROMPT)(functools.partial(body, True))


def _rope_args(lat, cos_ref, sp_ref, sn_ref, rot_dim):
    return (cos_ref[...], sp_ref[...], sn_ref[...], rot_dim // 4) if lat else None


def _maybe_rope(y, rope):
    return y if rope is None else _rope(y, *rope)


def _cache_rows(ref, index, val):
    for b in range(PROJ_BATCHES):
        ref[(b, 0) + tuple(index)] = val[b * SEQ:(b + 1) * SEQ]


def _cache_rows_t(ref, indices, val):
    for b in range(PROJ_BATCHES):
        t = val[b * SEQ:(b + 1) * SEQ].T
        for j, index in enumerate(indices):
            ref[(b, 0) + tuple(index)] = t[j * HALF:(j + 1) * HALF]


def _softmax2_parts(s_list, extra=None):
    m = jnp.max(s_list[0], axis=0, keepdims=True)
    for s in s_list[1:]:
        m = jnp.maximum(m, jnp.max(s, axis=0, keepdims=True))
    if extra is not None:
        m = jnp.maximum(m, extra)
    ps = [jnp.exp2(s - m) for s in s_list]
    mass = ps[0].sum(axis=0, keepdims=True)
    for p in ps[1:]:
        mass = mass + p.sum(axis=0, keepdims=True)
    if extra is not None:
        mass = mass + jnp.exp2(extra - m)
    return [p.astype(BF16) for p in ps], 1.0 / mass


def _head_pipeline(n, scores, finish):
    nxt = scores(0)
    for h in range(n):
        cur = nxt
        if h + 1 < n:
            nxt = scores(h + 1)
        finish(h, cur)


def _seq_pipeline(refs, seqs, n, make):
    fns = []
    for b in range(seqs):
        views = [r.at[b * (r.shape[0] // seqs):(b + 1) * (r.shape[0] // seqs)] for r in refs]
        fns.append(make(views))
    _head_pipeline(seqs * n, lambda i: fns[i // n][0](i % n), lambda i, s: fns[i // n][1](i % n, s))


def _pv(ps, values):
    o = None
    for p, v in zip(ps, values):
        t = _dot_tn(v, p)
        o = t if o is None else o + t
    return o


def _split_halves(q):
    lo = _lane_lo(q.shape)
    zero = jnp.zeros_like(q)
    return jnp.where(lo, q, zero), jnp.where(lo, zero, q)


def _mod_kernel(cond_ref, w_ref, b_ref, o_ref):
    c = cond_ref[...]
    s = (c * jax.nn.sigmoid(c)).astype(BF16)
    o_ref[0] = _dot(s, w_ref[0].astype(BF16)) + b_ref[0]


def _modulation(cond, ada_w, ada_b):
    tn = 1536
    n = MOD_CHUNKS * D_MODEL
    return pl.pallas_call(
        _mod_kernel,
        grid=(DEPTH, n // tn),
        in_specs=[
            pl.BlockSpec((COND_ROWS, D_MODEL), lambda l, j: (0, 0)),
            pl.BlockSpec((1, D_MODEL, tn), lambda l, j: (l, 0, j)),
            pl.BlockSpec((1, 1, tn), lambda l, j: (l, 0, j)),
        ],
        out_specs=pl.BlockSpec((1, COND_ROWS, tn), lambda l, j: (l, 0, j)),
        out_shape=jax.ShapeDtypeStruct((DEPTH, COND_ROWS, n), F32),
        compiler_params=_cparams(2),
        name="modulation",
    )(cond, ada_w, ada_b.reshape(DEPTH, 1, n))


def _mod_spec(layer, chunk):
    return pl.BlockSpec((None, COND_ROWS, D_MODEL), lambda i: (layer, 0, chunk))


def _mod_row(ref, i):
    return ref[pl.ds(_tile_group(i, PROJ_TM), 1), :]


_ROPE_SPEC = pl.BlockSpec((PROJ_TM, LANES), lambda i: (_rope_tile(i), 0))
_LANE_MAT_SPEC = _const_spec((LANES, LANES))
_UNIT_MAT_SPEC = _const_spec((PROJ_UNIT, PROJ_UNIT))


def _tok_spec(width):
    return pl.BlockSpec((PROJ_TM, width), lambda i: (i, 0))


_XP_SPEC = pl.BlockSpec((PROJ_TM, D_MODEL), lambda i: (jnp.minimum(i, N_PROJ_PROMPT - 1), 0))
_XS_SPEC = pl.BlockSpec((PROJ_TM, D_MODEL), lambda i: (jnp.maximum(i - N_PROJ_PROMPT, 0), 0))


def _cache_spec(*dims):
    nd = len(dims)
    return pl.BlockSpec((PROJ_BATCHES, 1) + dims,
                        lambda i: (jnp.minimum(i, N_PROJ_PROMPT - 1), 0) + (0,) * nd)


def _cache_shape(*dims):
    return jax.ShapeDtypeStruct((BATCH, 1) + dims, F32)


def _proj_att_kernel(xp_ref, xs_ref, gain_ref, sh_ref, sc_ref, w_ref, qg_ref, kg_ref, m_ref,
                     cos_ref, sp_ref, sn_ref, q_ref, k_ref, v_ref, ck_ref, cv_ref, w_s):
    i = pl.program_id(0)
    _cast_once(i, w_ref, w_s)
    x = jnp.where(i < N_PROJ_PROMPT, xp_ref[...], xs_ref[...])
    h = _norm_mod(x, gain_ref[...], _mod_row(sh_ref, i), _mod_row(sc_ref, i)).astype(BF16)
    per = PROJ_UNIT // LANES
    nq, nk = ATT_HEADS // per, ATT_KV_HEADS // per

    def body(lat):
        rope = _rope_args(lat, cos_ref, sp_ref, sn_ref, ATT_HEAD_DIM)

        def emit(u, y):
            if u < nq + nk:
                y = _head_norm(y, m_ref, qg_ref[...] if u < nq else kg_ref[...])
            for t, yc in enumerate(_halves(y)):
                if u < nq:
                    _put(q_ref, u * per + t, _maybe_rope(yc, rope))
                elif u < nq + nk:
                    kn = _maybe_rope(yc, rope)
                    _put(k_ref, (u - nq) * per + t, kn)
                    if not lat:
                        _cache_rows(ck_ref, [(u - nq) * per + t], kn)
                else:
                    _put(v_ref, (u - nq - nk) * per + t, yc)
                    if not lat:
                        _cache_rows(cv_ref, [(u - nq - nk) * per + t], yc)

        _matmul_units(h, w_s, nq + 2 * nk, PROJ_UNIT, emit)

    _by_tile_kind(i, body)


def _proj_att(xp, xs, mods, gain, w, qg, kg, tables):
    nq, nk = ATT_HEADS * ATT_HEAD_DIM, ATT_KV_HEADS * ATT_HEAD_DIM
    return pl.pallas_call(
        _proj_att_kernel,
        grid=(N_PROJ_TILES,),
        in_specs=[_XP_SPEC, _XS_SPEC, gain.spec, _mod_spec(0, 0), _mod_spec(0, 1),
                  _const_spec(w.shape), qg.spec, kg.spec,
                  _UNIT_MAT_SPEC, _ROPE_SPEC, _ROPE_SPEC, _ROPE_SPEC],
        out_specs=[_tok_spec(nq), _tok_spec(nk), _tok_spec(nk),
                   _cache_spec(ATT_KV_HEADS, SEQ, ATT_HEAD_DIM), _cache_spec(ATT_KV_HEADS, SEQ, ATT_HEAD_DIM)],
        out_shape=[jax.ShapeDtypeStruct((N_TOK, nq), BF16),
                   jax.ShapeDtypeStruct((N_TOK, nk), BF16),
                   jax.ShapeDtypeStruct((N_TOK, nk), BF16),
                   _cache_shape(ATT_KV_HEADS, SEQ, ATT_HEAD_DIM), _cache_shape(ATT_KV_HEADS, SEQ, ATT_HEAD_DIM)],
        scratch_shapes=[pltpu.VMEM(w.shape, BF16)],
        compiler_params=_cparams(1),
        name="proj_att",
    )(xp, xs, gain.array, mods, mods, w, qg.array, kg.array, _group_mean_matrix(ATT_HEAD_DIM), *tables)


def _proj_diff_kernel(h_ref, w_ref, qg_ref, kg_ref, m_ref,
                      cos_ref, sp_ref, sn_ref, q_ref, k_ref, v_ref, ck_ref, cv_ref, w_s):
    i = pl.program_id(0)
    _cast_once(i, w_ref, w_s)
    h = h_ref[...]
    per = PROJ_UNIT // LANES
    nu = DIFF_HEADS // per

    def body(lat):
        rope = _rope_args(lat, cos_ref, sp_ref, sn_ref, DIFF_HEAD_DIM)

        def emit(u, y):
            if u < 2 * nu:
                y = _head_norm(y, m_ref, qg_ref[...] if u < nu else kg_ref[...])
            for t, yc in enumerate(_halves(y)):
                hd = (u % nu) * per + t
                if u < nu:
                    _put(q_ref, hd, _maybe_rope(yc, rope))
                elif u < 2 * nu:
                    kn = _maybe_rope(yc, rope)
                    _put(k_ref, hd, kn)
                    if not lat:
                        _cache_rows_t(ck_ref, [[hd, 0], [hd, 1]], kn)
                else:
                    _put(v_ref, hd, yc)
                    if not lat:
                        _cache_rows(cv_ref, [hd], yc)

        _matmul_units(h, w_s, 3 * nu, PROJ_UNIT, emit)

    _by_tile_kind(i, body)


def _proj_diff(h, w, qg, kg, tables):
    n = DIFF_HEADS * 2 * DIFF_HEAD_DIM
    return pl.pallas_call(
        _proj_diff_kernel,
        grid=(N_PROJ_TILES,),
        in_specs=[_tok_spec(D_MODEL), _const_spec(w.shape), qg.spec, kg.spec,
                  _UNIT_MAT_SPEC, _ROPE_SPEC, _ROPE_SPEC, _ROPE_SPEC],
        out_specs=[_tok_spec(n), _tok_spec(n), _tok_spec(n),
                   _cache_spec(DIFF_HEADS, 2, DIFF_HEAD_DIM, SEQ), _cache_spec(DIFF_HEADS, SEQ, 2 * DIFF_HEAD_DIM)],
        out_shape=[jax.ShapeDtypeStruct((N_TOK, n), BF16)] * 3
                  + [_cache_shape(DIFF_HEADS, 2, DIFF_HEAD_DIM, SEQ),
                     _cache_shape(DIFF_HEADS, SEQ, 2 * DIFF_HEAD_DIM)],
        scratch_shapes=[pltpu.VMEM(w.shape, BF16)],
        compiler_params=_cparams(1),
        name="proj_diff",
    )(h, w, qg.array, kg.array, _group_mean_matrix(DIFF_HEAD_DIM), *tables)


def _dup_halves(yc):
    lo = _lane_lo(yc.shape)
    sw = pltpu.roll(yc, HALF, 1)
    return jnp.where(lo, yc, sw), jnp.where(lo, sw, yc)


def _proj_swa_kernel(h_ref, w_ref, qg_ref, kg_ref, m_ref,
                     cos_ref, sp_ref, sn_ref, q_ref, kd_ref, vd_ref, ck_ref, cv_ref, w_s):
    i = pl.program_id(0)
    _cast_once(i, w_ref, w_s)
    h = h_ref[...]
    per = PROJ_UNIT // LANES
    nq = SWA_HEADS * SWA_HEAD_DIM // PROJ_UNIT
    nk = SWA_KV_HEADS * SWA_HEAD_DIM // PROJ_UNIT

    def body(lat):
        rope = _rope_args(lat, cos_ref, sp_ref, sn_ref, SWA_HEAD_DIM)

        def emit(u, y):
            if u < nq + nk:
                y = _head_norm(y, m_ref, qg_ref[...] if u < nq else kg_ref[...])
            for t, yc in enumerate(_halves(y)):
                if u < nq:
                    _put(q_ref, u * per + t, _maybe_rope(yc, rope))
                    continue
                if u < nq + nk:
                    j, c_ref, d_ref = (u - nq) * per + t, ck_ref, kd_ref
                    yc = _maybe_rope(yc, rope)
                else:
                    j, c_ref, d_ref = (u - nq - nk) * per + t, cv_ref, vd_ref
                for a, dup in enumerate(_dup_halves(yc)):
                    _put(d_ref, 2 * j + a, dup)
                if not lat:
                    _cache_rows_t(c_ref, [[2 * j], [2 * j + 1]], yc)

        _matmul_units(h, w_s, nq + 2 * nk, PROJ_UNIT, emit)

    _by_tile_kind(i, body)


def _proj_swa(h, w, qg, kg, tables):
    nq, nk = SWA_HEADS * SWA_HEAD_DIM, SWA_KV_HEADS * SWA_HEAD_DIM
    return pl.pallas_call(
        _proj_swa_kernel,
        grid=(N_PROJ_TILES,),
        in_specs=[_tok_spec(D_MODEL), _const_spec(w.shape), qg.spec, kg.spec,
                  _UNIT_MAT_SPEC, _ROPE_SPEC, _ROPE_SPEC, _ROPE_SPEC],
        out_specs=[_tok_spec(nq), _tok_spec(2 * nk), _tok_spec(2 * nk),
                   _cache_spec(SWA_KV_HEADS, SWA_HEAD_DIM, SEQ), _cache_spec(SWA_KV_HEADS, SWA_HEAD_DIM, SEQ)],
        out_shape=[jax.ShapeDtypeStruct((N_TOK, nq), BF16),
                   jax.ShapeDtypeStruct((N_TOK, 2 * nk), BF16),
                   jax.ShapeDtypeStruct((N_TOK, 2 * nk), BF16),
                   _cache_shape(SWA_KV_HEADS, SWA_HEAD_DIM, SEQ), _cache_shape(SWA_KV_HEADS, SWA_HEAD_DIM, SEQ)],
        scratch_shapes=[pltpu.VMEM(w.shape, BF16)],
        compiler_params=_cparams(1),
        name="proj_swa",
    )(h, w, qg.array, kg.array, _group_mean_matrix(SWA_HEAD_DIM), *tables)


def _mla_lane_matrices():
    everything = lambda lane: lane >= 0
    return (_lane_sum_matrix(everything, everything),
            _lane_sum_matrix(lambda lane: lane < HALF, everything),
            _lane_sum_matrix(lambda lane: lane >= HALF, everything))


def _proj_mla_kernel(h_ref, w_in_ref, qa_ref, kva_ref, w_uq_ref,
                     qg_ref, qgp_ref, all_ref, lo_ref, hi_ref, cos_ref, sp_ref, sn_ref,
                     qn_ref, qp_ref, ckv_ref, kpe_ref, c_ckv_ref, c_kpe_ref):
    i = pl.program_id(0)
    y = _dot(h_ref[...], w_in_ref[...])
    c_q = y[:, :MLA_Q_RANK]
    c_kv = y[:, MLA_Q_RANK:MLA_Q_RANK + MLA_KV_RANK]
    kpe = y[:, MLA_Q_RANK + MLA_KV_RANK:]
    kpe_ref[...] = kpe
    ckv = c_kv * lax.rsqrt(jnp.mean(c_kv * c_kv, axis=-1, keepdims=True) + EPS) * kva_ref[...]
    ckv_ref[...] = ckv.astype(BF16)
    cq = (c_q * lax.rsqrt(jnp.mean(c_q * c_q, axis=-1, keepdims=True) + EPS) * qa_ref[...]).astype(BF16)
    lo = _lane_lo((PROJ_TM, LANES))
    inv_d = 1.0 / (MLA_NOPE + MLA_ROPE)

    def body(lat):
        if not lat:
            _cache_rows(c_ckv_ref, [], ckv)
            _cache_rows_t(c_kpe_ref, [[]], kpe)

        def emit(j, yq):
            pe = yq[:, 2 * LANES:]
            pe_sq = _sq_bf16(pe)
            rs = []
            for a, half_ref in enumerate((lo_ref, hi_ref)):
                nope = yq[:, a * LANES:(a + 1) * LANES]
                ss = _dot(_sq_bf16(nope), all_ref[...]) + _dot(pe_sq, half_ref[...])
                r = lax.rsqrt(ss * inv_d + EPS)
                rs.append(r)
                _put(qn_ref, 2 * j + a, nope * r * qg_ref[...])
            pe = pe * jnp.where(lo, rs[0], rs[1]) * qgp_ref[...]
            if lat:
                pe = _rope(pe, cos_ref[...], sp_ref[...], sn_ref[...], MLA_ROPE // 4)
            _put(qp_ref, j, pe)

        _matmul_units(cq, w_uq_ref, MLA_HEADS // 2, 3 * LANES, emit)

    _by_tile_kind(i, body)


def _proj_mla(h, w_in, qa, kva, w_uq, qg, qgp, tables):
    n_nope = MLA_HEADS * MLA_NOPE
    n_pe = MLA_HEADS * MLA_ROPE
    return pl.pallas_call(
        _proj_mla_kernel,
        grid=(N_PROJ_TILES,),
        in_specs=[_tok_spec(D_MODEL),
                  _const_spec(w_in.shape), qa.spec, kva.spec,
                  _const_spec(w_uq.shape), qg.spec, qgp.spec,
                  _LANE_MAT_SPEC, _LANE_MAT_SPEC, _LANE_MAT_SPEC,
                  _ROPE_SPEC, _ROPE_SPEC, _ROPE_SPEC],
        out_specs=[_tok_spec(n_nope), _tok_spec(n_pe), _tok_spec(MLA_KV_RANK), _tok_spec(LANES),
                   _cache_spec(SEQ, MLA_KV_RANK), _cache_spec(MLA_ROPE, SEQ)],
        out_shape=[jax.ShapeDtypeStruct((N_TOK, n_nope), BF16),
                   jax.ShapeDtypeStruct((N_TOK, n_pe), BF16),
                   jax.ShapeDtypeStruct((N_TOK, MLA_KV_RANK), BF16),
                   jax.ShapeDtypeStruct((N_TOK, LANES), F32),
                   _cache_shape(SEQ, MLA_KV_RANK), _cache_shape(MLA_ROPE, SEQ)],
        compiler_params=_cparams(1),
        name="proj_mla",
    )(h, w_in, qa.array, kva.array, w_uq, qg.array, qgp.array, *_mla_lane_matrices(), *tables)


def _mla_expand_kernel(ckv_ref, kpe_ref, w_ref, kg_ref, kgp_ref, sum_ref, lo_ref,
                       cos_ref, sp_ref, sn_ref, kn_ref, kp_ref, v_ref, *, rope):
    i = pl.program_id(0)
    ckv = ckv_ref[...].astype(BF16)
    kpe = kpe_ref[...]
    pe_ss = _dot(_sq_bf16(kpe), lo_ref[...])
    pe_ss = jnp.concatenate([pe_ss, pe_ss], axis=1)
    lo = _lane_lo(kpe.shape)
    inv_d = 1.0 / (MLA_NOPE + MLA_ROPE)

    def body(lat):
        def emit(j, y):
            kn = jnp.concatenate([y[:, :LANES], y[:, 2 * LANES:3 * LANES]], axis=1)
            r = lax.rsqrt((_dot(_sq_bf16(kn), sum_ref[...]) + pe_ss) * inv_d + EPS)
            kn = kn * r * kg_ref[...]
            for a in range(2):
                _put(kn_ref, 2 * j + a, kn[:, a * LANES:(a + 1) * LANES])
                _put(v_ref, 2 * j + a, y[:, (2 * a + 1) * LANES:(2 * a + 2) * LANES])
            pe = kpe * jnp.where(lo, r[:, :LANES], r[:, LANES:]) * kgp_ref[...]
            if lat:
                pe = _rope(pe, cos_ref[...], sp_ref[...], sn_ref[...], MLA_ROPE // 4)
            _put(kp_ref, j, pe)

        _matmul_units(ckv, w_ref, MLA_HEADS // 2, 4 * LANES, emit)

    if rope:
        _by_tile_kind(i, body)
    else:
        body(False)


def _mla_expand(ckv, kpe_dup, w_ukv, kg, kgp, tables, rope):
    n = ckv.shape[0]
    n_nope = MLA_HEADS * MLA_NOPE
    n_pe = MLA_HEADS * MLA_ROPE
    _, m_lo, _ = _mla_lane_matrices()
    return pl.pallas_call(
        functools.partial(_mla_expand_kernel, rope=rope),
        grid=(n // PROJ_TM,),
        in_specs=[_tok_spec(MLA_KV_RANK), _tok_spec(LANES), _const_spec(w_ukv.shape),
                  kg.spec, kgp.spec, _UNIT_MAT_SPEC, _LANE_MAT_SPEC,
                  _ROPE_SPEC, _ROPE_SPEC, _ROPE_SPEC],
        out_specs=[_tok_spec(n_nope), _tok_spec(n_pe), _tok_spec(n_nope)],
        out_shape=[jax.ShapeDtypeStruct((n, n_nope), BF16),
                   jax.ShapeDtypeStruct((n, n_pe), BF16),
                   jax.ShapeDtypeStruct((n, n_nope), BF16)],
        compiler_params=_cparams(1),
        name="mla_expand",
    )(ckv, kpe_dup, w_ukv, kg.array, kgp.array, _group_sum_matrix(), m_lo, *tables)


def _prompt_spec(width):
    return pl.BlockSpec((PROMPT_SEQS * TM, width), lambda b: (b, 0))


def _latq_spec(rows, width):
    per = DEC_SEQ // rows
    return pl.BlockSpec((rows, width), lambda b, t: (N_PROMPT_TOK // rows + b * per + t, 0))


def _latkv_spec(width):
    return pl.BlockSpec((DEC_SEQ, width), lambda b, t: (LAT_BLOCK0 + b, 0))


def _lato_spec(rows):
    per = DEC_SEQ // rows
    return pl.BlockSpec((rows, D_MODEL), lambda b, t: (b * per + t, 0))


def _att_kernel(*refs, with_ctx, seqs):
    if with_ctx:
        q_ref, k_ref, v_ref, kc_ref, vc_ref, o_ref = refs
    else:
        q_ref, k_ref, v_ref, o_ref = refs
    tq = q_ref.shape[0] // seqs
    nu = ATT_UNIT_HEADS
    per_kv = ATT_HEADS // ATT_KV_HEADS // nu

    def make(views):
        q_v, k_v, v_v, o_v = views

        def scores(u):
            q = jnp.concatenate([_chunk(q_v, u * nu + g) for g in range(nu)], axis=0)
            s_list = [_dot_nt(_chunk(k_v, u // per_kv), q)]
            if with_ctx:
                s_list.append(_dot_nt(kc_ref[u // per_kv].astype(BF16), q))
            return s_list

        def finish(u, s_list):
            values = [_chunk(v_v, u // per_kv)]
            if with_ctx:
                values.append(vc_ref[u // per_kv].astype(BF16))
            ps, inv = _softmax2_parts(s_list)
            o = _pv(ps, values) * inv
            for g in range(nu):
                o_v[:, (u * nu + g) * LANES:(u * nu + g + 1) * LANES] = (
                    o[:, g * tq:(g + 1) * tq].T.astype(o_v.dtype))

        return scores, finish

    _seq_pipeline((q_ref, k_ref, v_ref, o_ref), seqs, ATT_HEADS // nu, make)


def _att_attend(q, k, v, cache_k, cache_v):
    nk = ATT_KV_HEADS * ATT_HEAD_DIM
    out_p = pl.pallas_call(
        functools.partial(_att_kernel, with_ctx=False, seqs=PROMPT_SEQS),
        grid=(N_PROMPT_TILES // PROMPT_SEQS,),
        in_specs=[_prompt_spec(D_MODEL), _prompt_spec(nk), _prompt_spec(nk)],
        out_specs=_prompt_spec(D_MODEL),
        out_shape=jax.ShapeDtypeStruct((N_PROMPT_TOK, D_MODEL), BF16),
        compiler_params=_cparams(1),
        name="att_prompt",
    )(q, k, v)
    ctx = pl.BlockSpec((None, None, ATT_KV_HEADS, PAST_LEN, LANES), lambda b, t: (b, 0, 0, 0, 0))
    out_s = pl.pallas_call(
        functools.partial(_att_kernel, with_ctx=True, seqs=1),
        grid=(DEC_BATCH, DEC_SEQ // LAT_TQ),
        in_specs=[_latq_spec(LAT_TQ, D_MODEL), _latkv_spec(nk), _latkv_spec(nk), ctx, ctx],
        out_specs=_lato_spec(LAT_TQ),
        out_shape=jax.ShapeDtypeStruct((N_LAT_TOK, D_MODEL), BF16),
        compiler_params=_cparams(2),
        name="att_latent",
    )(q, k, v, cache_k, cache_v)
    return out_p, out_s


def _diff_kernel(*refs, lam_init, with_ctx, seqs):
    if with_ctx:
        (q_ref, k_ref, v_ref, kc_ref, vc_ref, lq1_ref, lk1_ref, lq2_ref, lk2_ref, sub_ref, o_ref) = refs
    else:
        (q_ref, k_ref, v_ref, lq1_ref, lk1_ref, lq2_ref, lk2_ref, sub_ref, o_ref) = refs
    tq = q_ref.shape[0] // seqs
    lam = (jnp.exp(jnp.sum(lq1_ref[...] * lk1_ref[...], axis=-1, keepdims=True))
           - jnp.exp(jnp.sum(lq2_ref[...] * lk2_ref[...], axis=-1, keepdims=True)) + lam_init)
    diag = (lax.broadcasted_iota(jnp.int32, (LANES, LANES), 0)
            == lax.broadcasted_iota(jnp.int32, (LANES, LANES), 1))
    sub = jnp.sum(jnp.where(diag, sub_ref[...] * (1.0 - lam_init), 0.0), axis=1, keepdims=True)

    def make(views):
        q_v, k_v, v_v, o_v = views

        def scores(hd):
            q = jnp.concatenate(_split_halves(_chunk(q_v, hd)), axis=0)
            s_list = [_dot_nt(_chunk(k_v, hd), q)]
            if with_ctx:
                s_list.append(_dot_nt(kc_ref[hd].astype(BF16), q))
            return s_list

        def finish(hd, s_list):
            values = [_chunk(v_v, hd)]
            if with_ctx:
                values.append(vc_ref[hd].astype(BF16))
            ps, inv = _softmax2_parts(s_list)
            o = (_pv([p[:, :tq] for p in ps], values) * inv[:, :tq]
                 - _pv([p[:, tq:] for p in ps], values) * (lam * inv[:, tq:]))
            o = o * lax.rsqrt(jnp.mean(o * o, axis=0, keepdims=True) + EPS) * sub
            o_v[:, hd * LANES:(hd + 1) * LANES] = o.T.astype(o_v.dtype)

        return scores, finish

    _seq_pipeline((q_ref, k_ref, v_ref, o_ref), seqs, DIFF_HEADS, make)


def _diff_attend(q, k, v, cache_k_pair, cache_v, lq1, lk1, lq2, lk2, subln, lam_init):
    small_specs = [p.spec for p in (lq1, lk1, lq2, lk2, subln)]
    small = [p.array for p in (lq1, lk1, lq2, lk2, subln)]
    out_p = pl.pallas_call(
        functools.partial(_diff_kernel, lam_init=lam_init, with_ctx=False, seqs=PROMPT_SEQS),
        grid=(N_PROMPT_TILES // PROMPT_SEQS,),
        in_specs=[_prompt_spec(D_MODEL)] * 3 + small_specs,
        out_specs=_prompt_spec(D_MODEL),
        out_shape=jax.ShapeDtypeStruct((N_PROMPT_TOK, D_MODEL), BF16),
        compiler_params=_cparams(1),
        name="diff_prompt",
    )(q, k, v, *small)
    out_s = pl.pallas_call(
        functools.partial(_diff_kernel, lam_init=lam_init, with_ctx=True, seqs=1),
        grid=(DEC_BATCH, DEC_SEQ // LAT_TQ),
        in_specs=[_latq_spec(LAT_TQ, D_MODEL), _latkv_spec(D_MODEL), _latkv_spec(D_MODEL),
                  pl.BlockSpec((None, DIFF_HEADS, PAST_LEN, LANES), lambda b, t: (b, 0, 0, 0)),
                  pl.BlockSpec((None, None, DIFF_HEADS, PAST_LEN, LANES), lambda b, t: (b, 0, 0, 0, 0))]
                 + small_specs,
        out_specs=_lato_spec(LAT_TQ),
        out_shape=jax.ShapeDtypeStruct((N_LAT_TOK, D_MODEL), BF16),
        compiler_params=_cparams(2),
        name="diff_latent",
    )(q, k, v, cache_k_pair, cache_v, *small)
    return out_p, out_s


def _swa_pipeline(q_ref, o_ref, seq_refs, sink_ref, score_fns, value_fns, seqs=1):
    tq = q_ref.shape[0] // seqs
    per_kv = SWA_HEADS // SWA_KV_HEADS // 2
    first = lax.broadcasted_iota(jnp.int32, (LANES, tq), 0) < HALF

    def make(views):
        q_v, o_v = views[:2]
        kv_views = views[2:]

        def scores(c):
            q = jnp.concatenate(_split_halves(_chunk(q_v, c)), axis=0)
            return [fn(kv_views, c // per_kv, q) for fn in score_fns]

        def finish(c, s_list):
            sink = jnp.concatenate([jnp.full((1, tq), sink_ref[2 * c + a] * LOG2E, F32) for a in range(2)],
                                   axis=1)
            ps, inv = _softmax2_parts(s_list, extra=sink)
            o = _pv(ps, [fn(kv_views, c // per_kv) for fn in value_fns]) * inv
            oc = jnp.where(first, o[:, :tq], o[:, tq:])
            o_v[:, c * LANES:(c + 1) * LANES] = oc.T.astype(o_v.dtype)

        return scores, finish

    _seq_pipeline((q_ref, o_ref) + tuple(seq_refs), seqs, SWA_HEADS // 2, make)


def _swa_prompt_kernel(sink_ref, q_ref, k_ref, v_ref, o_ref):
    _swa_pipeline(q_ref, o_ref, (k_ref, v_ref), sink_ref,
                  [lambda kv_v, kv, q: _dot_nt(_chunk(kv_v[0], kv), q)],
                  [lambda kv_v, kv: _chunk(kv_v[1], kv)], seqs=PROMPT_SEQS)


def _swa_latent_kernel(sink_ref, q_ref, k_ref, v_ref, kc_ref, vc_ref, o_ref):
    n = pl.program_id(1)
    tq = q_ref.shape[0]
    span = SWA_QB + 2 * WINDOW
    start = pl.multiple_of(jnp.clip(n * SWA_QB - WINDOW, 0, DEC_SEQ - span), WINDOW)
    cols = lax.broadcasted_iota(jnp.int32, (span, 2 * tq), 1)
    qpos = n * SWA_QB + jnp.bitwise_and(cols, tq - 1)
    kpos = start + lax.broadcasted_iota(jnp.int32, (span, 2 * tq), 0)
    valid = jnp.abs(qpos - kpos) <= WINDOW

    def local(ref, kv):
        return ref[pl.ds(start, span), kv * LANES:(kv + 1) * LANES]

    _swa_pipeline(q_ref, o_ref, (), sink_ref,
                  [lambda _, kv, q: jnp.where(valid, _dot_nt(local(k_ref, kv), q), -1e30),
                   lambda _, kv, q: _dot_nt(kc_ref[kv], q)],
                  [lambda _, kv: local(v_ref, kv), lambda _, kv: vc_ref[kv]])


def _swa_attend(q, kd, vd, cache_kd, cache_vd, sink):
    nkd = 2 * SWA_KV_HEADS * SWA_HEAD_DIM
    smem = pl.BlockSpec(memory_space=pltpu.SMEM)
    out_p = pl.pallas_call(
        _swa_prompt_kernel,
        grid=(N_PROMPT_TILES // PROMPT_SEQS,),
        in_specs=[smem, _prompt_spec(D_MODEL), _prompt_spec(nkd), _prompt_spec(nkd)],
        out_specs=_prompt_spec(D_MODEL),
        out_shape=jax.ShapeDtypeStruct((N_PROMPT_TOK, D_MODEL), BF16),
        compiler_params=_cparams(1),
        name="swa_prompt",
    )(sink, q, kd, vd)
    ctx = pl.BlockSpec((None, SWA_KV_HEADS, PAST_LEN, LANES), lambda b, n: (b, 0, 0, 0))
    out_s = pl.pallas_call(
        _swa_latent_kernel,
        grid=(DEC_BATCH, DEC_SEQ // SWA_QB),
        in_specs=[smem, _latq_spec(SWA_QB, D_MODEL), _latkv_spec(nkd), _latkv_spec(nkd), ctx, ctx],
        out_specs=_lato_spec(SWA_QB),
        out_shape=jax.ShapeDtypeStruct((N_LAT_TOK, D_MODEL), BF16),
        compiler_params=_cparams(2),
        name="swa_latent",
    )(sink, q, kd, vd, cache_kd, cache_vd)
    return out_p, out_s


def _mla_kernel(*refs, with_ctx, seqs):
    if with_ctx:
        (qn_ref, qp_ref, kn_ref, kp_ref, v_ref, knc_ref, kpc_ref, vc_ref, o_ref) = refs
    else:
        (qn_ref, qp_ref, kn_ref, kp_ref, v_ref, o_ref) = refs

    def make(views):
        qn_v, qp_v, kn_v, kp_v, v_v, o_v = views

        def scores(hd):
            j, a = hd // 2, hd % 2
            q = jnp.concatenate([_chunk(qn_v, hd), _split_halves(_chunk(qp_v, j))[a]], axis=1)
            s_list = [_dot_nt(jnp.concatenate([_chunk(kn_v, hd), _chunk(kp_v, j)], axis=1), q)]
            if with_ctx:
                s_list.append(_dot_nt(jnp.concatenate([_chunk(knc_ref, hd), _chunk(kpc_ref, j)], axis=1), q))
            return s_list

        def finish(hd, s_list):
            values = [_chunk(v_v, hd)]
            if with_ctx:
                values.append(_chunk(vc_ref, hd))
            ps, inv = _softmax2_parts(s_list)
            o_v[:, hd * LANES:(hd + 1) * LANES] = (_pv(ps, values) * inv).T.astype(o_v.dtype)

        return scores, finish

    _seq_pipeline((qn_ref, qp_ref, kn_ref, kp_ref, v_ref, o_ref), seqs, MLA_HEADS, make)


def _mla_attend(qn, qp, kn, kp, v, knc, kpc, vc):
    n_pe = MLA_HEADS * MLA_ROPE
    out_p = pl.pallas_call(
        functools.partial(_mla_kernel, with_ctx=False, seqs=PROMPT_SEQS),
        grid=(N_PROMPT_TILES // PROMPT_SEQS,),
        in_specs=[_prompt_spec(D_MODEL), _prompt_spec(n_pe), _prompt_spec(D_MODEL), _prompt_spec(n_pe),
                  _prompt_spec(D_MODEL)],
        out_specs=_prompt_spec(D_MODEL),
        out_shape=jax.ShapeDtypeStruct((N_PROMPT_TOK, D_MODEL), BF16),
        compiler_params=_cparams(1),
        name="mla_prompt",
    )(qn, qp, kn, kp, v)

    def ctx(width):
        return pl.BlockSpec((PAST_LEN, width), lambda b, t: (b, 0))

    out_s = pl.pallas_call(
        functools.partial(_mla_kernel, with_ctx=True, seqs=1),
        grid=(DEC_BATCH, DEC_SEQ // LAT_TQ),
        in_specs=[_latq_spec(LAT_TQ, D_MODEL), _latq_spec(LAT_TQ, n_pe),
                  _latkv_spec(D_MODEL), _latkv_spec(n_pe), _latkv_spec(D_MODEL),
                  ctx(D_MODEL), ctx(n_pe), ctx(D_MODEL)],
        out_specs=_lato_spec(LAT_TQ),
        out_shape=jax.ShapeDtypeStruct((N_LAT_TOK, D_MODEL), BF16),
        compiler_params=_cparams(2),
        name="mla_latent",
    )(qn, qp, kn, kp, v, knc, kpc, vc)
    return out_p, out_s


def _omlp_kernel(*refs, first, last):
    refs = list(refs)
    ap_ref, as_ref, wo_ref = refs[:3]
    x_refs = refs[3:5] if first else refs[3:4]
    refs = refs[3 + len(x_refs):]
    g1_ref, gain_ref, sh_ref, sc_ref, g2_ref, w1c_ref, w2c_ref = refs[:7]
    refs = refs[7:]
    if last:
        op_ref, os_ref = refs[:2]
    else:
        ngain_ref, nsh_ref, nsc_ref, o_ref, hn_ref = refs[:5]
    wo_s, w1_s, w2_s, x1_s, acc_s = refs[-5:]
    s = pl.program_id(0)
    per = MLP_FF_CHUNK // MLP_LOAD_COLS
    n_chunks = D_FF // MLP_FF_CHUNK
    n_tiles = N_TOK // MLP_TM
    half = MLP_TM // 2

    def tail(t_prev):
        grp = _tile_group(t_prev, MLP_TM)
        out = x1_s[...] + g2_ref[pl.ds(grp, 1), :] * acc_s[...]
        if not last:
            o_ref[...] = out
            hn_ref[...] = _norm_mod(out, ngain_ref[...], nsh_ref[pl.ds(grp, 1), :],
                                    nsc_ref[pl.ds(grp, 1), :]).astype(BF16)
            return lambda: None

        def stores():
            is_prompt = t_prev < N_MLP_PROMPT_TILES

            @pl.when(is_prompt)
            def _():
                op_ref[...] = out

            @pl.when(jnp.logical_not(is_prompt))
            def _():
                os_ref[...] = out

        return stores

    @pl.when(s == 0)
    def _():
        wo_s[...] = wo_ref[...].astype(BF16)
        x1_s[...] = jnp.zeros_like(x1_s)
        acc_s[...] = jnp.zeros_like(acc_s)

    for part in range(per):
        @pl.when((s < N_LOAD_STEPS) & (s % per == part))
        def _(part=part):
            w1_s[s // per, :, part * MLP_LOAD_COLS:(part + 1) * MLP_LOAD_COLS] = w1c_ref[...].astype(BF16)

    @pl.when(s < N_LOAD_STEPS)
    def _():
        w2_s[s // per, pl.ds(pl.multiple_of((s % per) * MLP_LOAD_COLS, MLP_LOAD_COLS), MLP_LOAD_COLS), :] = (
            w2c_ref[...].astype(BF16))

    @pl.when((s >= N_LOAD_STEPS) & (s < N_LOAD_STEPS + n_tiles))
    def _():
        t = s - N_LOAD_STEPS
        finish_prev = tail(jnp.maximum(t - 1, 0))

        is_prompt = t < N_MLP_PROMPT_TILES
        grp = _tile_group(t, MLP_TM)

        def mod(ref):
            return ref[pl.ds(grp, 1), :]

        rows = [slice(r * half, (r + 1) * half) for r in range(2)]
        o = [_dot(jnp.where(is_prompt, ap_ref[rw, :], as_ref[rw, :]), wo_s[...]) for rw in rows]
        x1, h, u0 = [], [], []
        for r, rw in enumerate(rows):
            x = jnp.where(is_prompt, x_refs[0][rw, :], x_refs[1][rw, :]) if first else x_refs[0][rw, :]
            x1.append(x + mod(g1_ref) * o[r])
            h.append(_norm_mod(x1[r], gain_ref[...], mod(sh_ref), mod(sc_ref)).astype(BF16))
            u0.append(_dot(h[r], w1_s[0]))
        h = jnp.concatenate(h, axis=0)
        acc = []

        def up(c):
            return jnp.concatenate(u0, axis=0) if c == 0 else _dot(h, w1_s[c])

        def down(c, u):
            u = jnp.square(jnp.maximum(u, 0.0)).astype(BF16)
            y = _dot(u, w2_s[c])
            acc[:] = [y if not acc else acc[0] + y]

        _head_pipeline(n_chunks, up, down)
        for r, rw in enumerate(rows):
            x1_s[rw, :] = x1[r]
        acc_s[...] = acc[0]
        finish_prev()

    @pl.when(s == N_LOAD_STEPS + n_tiles)
    def _():
        tail(n_tiles - 1)()


def _omlp(attn_p, attn_s, w_o, x, mods, gain_ffn, w1_all, w2_all, layer, next_gain):
    first, last = layer == 0, next_gain is None
    n_tiles = N_TOK // MLP_TM
    n_lat_tiles = N_LAT_TOK // MLP_TM

    def tok(s):
        return jnp.clip(s - N_LOAD_STEPS, 0, n_tiles - 1)

    def prev(s):
        return jnp.clip(s - N_LOAD_STEPS - 1, 0, n_tiles - 1)

    def side_specs(tile):
        return [pl.BlockSpec((MLP_TM, D_MODEL), lambda s: (jnp.minimum(tile(s), N_MLP_PROMPT_TILES - 1), 0)),
                pl.BlockSpec((MLP_TM, D_MODEL),
                             lambda s: (jnp.clip(tile(s) - N_MLP_PROMPT_TILES, 0, n_lat_tiles - 1), 0))]

    def tile_spec(tile):
        return pl.BlockSpec((MLP_TM, D_MODEL), lambda s: (tile(s), 0))

    w1_spec = pl.BlockSpec((None, D_MODEL, MLP_LOAD_COLS),
                           lambda s: (layer, 0, jnp.minimum(s, N_LOAD_STEPS - 1)))
    w2_spec = pl.BlockSpec((None, MLP_LOAD_COLS, D_MODEL),
                           lambda s: (layer, jnp.minimum(s, N_LOAD_STEPS - 1), 0))
    n_chunks = D_FF // MLP_FF_CHUNK
    in_specs = (side_specs(tok) + [_const_spec(w_o.shape)] + (side_specs(tok) if first else [tile_spec(tok)])
                + [_mod_spec(layer, 2), gain_ffn.spec, _mod_spec(layer, 3), _mod_spec(layer, 4),
                   _mod_spec(layer, 5), w1_spec, w2_spec])
    args = ([attn_p, attn_s, w_o] + (list(x) if first else [x])
            + [mods, gain_ffn.array, mods, mods, mods, w1_all, w2_all])
    if last:
        out_specs = side_specs(prev)
        out_shape = [jax.ShapeDtypeStruct((N_PROMPT_TOK, D_MODEL), F32),
                     jax.ShapeDtypeStruct((N_LAT_TOK, D_MODEL), F32)]
    else:
        in_specs += [next_gain.spec, _mod_spec(layer + 1, 0), _mod_spec(layer + 1, 1)]
        args += [next_gain.array, mods, mods]
        out_specs = [tile_spec(prev), tile_spec(prev)]
        out_shape = [jax.ShapeDtypeStruct((N_TOK, D_MODEL), F32), jax.ShapeDtypeStruct((N_TOK, D_MODEL), BF16)]
    return pl.pallas_call(
        functools.partial(_omlp_kernel, first=first, last=last),
        grid=(N_LOAD_STEPS + n_tiles + 1,),
        in_specs=in_specs,
        out_specs=out_specs,
        out_shape=out_shape,
        scratch_shapes=[pltpu.VMEM((D_MODEL, D_MODEL), BF16),
                        pltpu.VMEM((n_chunks, D_MODEL, MLP_FF_CHUNK), BF16),
                        pltpu.VMEM((n_chunks, MLP_FF_CHUNK, D_MODEL), BF16),
                        pltpu.VMEM((MLP_TM, D_MODEL), F32),
                        pltpu.VMEM((MLP_TM, D_MODEL), F32)],
        compiler_params=_cparams(1),
        name="omlp",
    )(*args)


def kernel(x_prompt, x_sample, cache_att_k, cache_att_v, cache_diff_k, cache_diff_v, cache_swa_k, cache_swa_v, cache_mla_ckv, cache_mla_kpe, c, c_ctx, ada_w, ada_b, norm_mix, norm_ffn, att_w_qkv, att_q_norm, att_k_norm, att_w_o, diff_w_qkv, diff_q_norm, diff_k_norm, diff_lq1, diff_lk1, diff_lq2, diff_lk2, diff_subln, diff_w_o, swa_w_qkv, swa_q_norm, swa_k_norm, swa_sink, swa_w_o, mla_w_in, mla_q_a_norm, mla_kv_a_norm, mla_w_uq, mla_w_ukv, mla_q_norm, mla_k_norm, mla_w_o, mlp_w1, mlp_w2):
    xp = x_prompt.reshape(N_PROMPT_TOK, D_MODEL)
    xs = x_sample.reshape(N_LAT_TOK, D_MODEL)
    cond = jnp.concatenate([c_ctx[None], c, jnp.zeros((COND_ROWS - 1 - DEC_BATCH, D_MODEL), F32)], axis=0)
    mods_all = _modulation(cond, ada_w, ada_b)

    tab_att = _rope_tables(ATT_HEAD_DIM)
    tab_64 = _rope_tables(DIFF_HEAD_DIM)

    pk = _ParamPack()
    g_mix = [pk.add(norm_mix[l]) for l in range(DEPTH)]
    g_ffn = [pk.add(norm_ffn[l]) for l in range(DEPTH)]
    att_qg = pk.add(att_q_norm[0], PROJ_UNIT // ATT_HEAD_DIM, ATT_HEAD_DIM ** -0.5 * LOG2E)
    att_kg = pk.add(att_k_norm[0], PROJ_UNIT // ATT_HEAD_DIM)
    diff_qg = pk.add(diff_q_norm[0], PROJ_UNIT // DIFF_HEAD_DIM, DIFF_HEAD_DIM ** -0.5 * LOG2E)
    diff_kg = pk.add(diff_k_norm[0], PROJ_UNIT // DIFF_HEAD_DIM)
    diff_small = [pk.add(v[0]) for v in (diff_lq1, diff_lk1, diff_lq2, diff_lk2, diff_subln)]
    swa_qg = pk.add(swa_q_norm[0], PROJ_UNIT // SWA_HEAD_DIM, SWA_HEAD_DIM ** -0.5 * LOG2E)
    swa_kg = pk.add(swa_k_norm[0], PROJ_UNIT // SWA_HEAD_DIM)
    mla_qs = (MLA_NOPE + MLA_ROPE) ** -0.5 * LOG2E
    mla_qa = pk.add(mla_q_a_norm[0])
    mla_kva = pk.add(mla_kv_a_norm[0])
    mla_qg = pk.add(mla_q_norm[0][:MLA_NOPE], 1, mla_qs)
    mla_qgp = pk.add(mla_q_norm[0][MLA_NOPE:], LANES // MLA_ROPE, mla_qs)
    mla_kg = pk.add(mla_k_norm[0][:MLA_NOPE], PROJ_UNIT // MLA_NOPE)
    mla_kgp = pk.add(mla_k_norm[0][MLA_NOPE:], LANES // MLA_ROPE)
    pk.build()

    outs = {}
    x = (xp, xs)
    for layer in range(DEPTH):
        gain_ffn = g_ffn[layer]
        if layer == 0:
            q, k, v, outs["att_k"], outs["att_v"] = _proj_att(
                xp, xs, mods_all, g_mix[layer], att_w_qkv[0], att_qg, att_kg, tab_att)
            attn_p, attn_s = _att_attend(q, k, v, cache_att_k, cache_att_v)
            w_o = att_w_o[0]
        elif layer == 1:
            q, k, v, outs["diff_k"], outs["diff_v"] = _proj_diff(h, diff_w_qkv[0], diff_qg, diff_kg, tab_64)
            lam_init = 0.8 - 0.6 * math.exp(-0.3 * layer)
            ck = cache_diff_k[:, 0].transpose(0, 1, 3, 2, 4).reshape(
                DEC_BATCH, DIFF_HEADS, PAST_LEN, LANES)
            attn_p, attn_s = _diff_attend(q, k, v, ck, cache_diff_v, *diff_small, lam_init)
            w_o = diff_w_o[0]
        elif layer == 2:
            q, kd, vd, outs["swa_k"], outs["swa_v"] = _proj_swa(h, swa_w_qkv[0], swa_qg, swa_kg, tab_64)
            ckd = jnp.concatenate([cache_swa_k[:, 0]] * 2, axis=-1).astype(BF16)
            cvd = jnp.concatenate([cache_swa_v[:, 0]] * 2, axis=-1).astype(BF16)
            attn_p, attn_s = _swa_attend(q, kd, vd, ckd, cvd, swa_sink[0].astype(F32))
            w_o = swa_w_o[0]
        else:
            w_in = mla_w_in[0]
            w_in = jnp.concatenate([w_in, w_in[:, -MLA_ROPE:]], axis=1).astype(BF16)
            w_uq = mla_w_uq[0].reshape(MLA_Q_RANK, MLA_HEADS // 2, 2, MLA_NOPE + MLA_ROPE)
            w_uq = jnp.concatenate([w_uq[..., :MLA_NOPE].reshape(MLA_Q_RANK, MLA_HEADS // 2, 2 * MLA_NOPE),
                                    w_uq[..., MLA_NOPE:].reshape(MLA_Q_RANK, MLA_HEADS // 2, 2 * MLA_ROPE)],
                                   axis=-1).reshape(MLA_Q_RANK, -1).astype(BF16)
            w_ukv = mla_w_ukv[0].astype(BF16)
            qn, qp, ckv, kpe, outs["mla_ckv"], outs["mla_kpe"] = _proj_mla(
                h, w_in, mla_qa, mla_kva, w_uq, mla_qg, mla_qgp, tab_64)
            kn, kp, vv = _mla_expand(ckv, kpe, w_ukv, mla_kg, mla_kgp, tab_64, True)
            c_ckv = cache_mla_ckv[:, 0].reshape(DEC_BATCH * PAST_LEN, MLA_KV_RANK)
            c_kpe = cache_mla_kpe[:, 0].reshape(DEC_BATCH * PAST_LEN, MLA_ROPE)
            c_kpe = jnp.concatenate([c_kpe, c_kpe], axis=-1)
            knc, kpc, vc = _mla_expand(c_ckv, c_kpe, w_ukv, mla_kg, mla_kgp, tab_64, False)
            attn_p, attn_s = _mla_attend(qn, qp, kn, kp, vv, knc, kpc, vc)
            w_o = mla_w_o[0]
        if layer + 1 < DEPTH:
            x, h = _omlp(attn_p, attn_s, w_o, x, mods_all, gain_ffn, mlp_w1, mlp_w2, layer,
                         g_mix[layer + 1])
        else:
            xp, xs = _omlp(attn_p, attn_s, w_o, x, mods_all, gain_ffn, mlp_w1, mlp_w2, layer, None)

    y_prompt = xp.reshape(BATCH, SEQ, D_MODEL)
    y_sample = xs.reshape(DEC_BATCH, DEC_SEQ, D_MODEL)
    for name in ("diff_k", "swa_k", "swa_v", "mla_kpe"):
        outs[name] = jnp.swapaxes(outs[name], -1, -2)
    return (y_prompt, y_sample, outs["att_k"], outs["att_v"], outs["diff_k"], outs["diff_v"],
            outs["swa_k"], outs["swa_v"], outs["mla_ckv"], outs["mla_kpe"])
```

```python
import functools
import math

import numpy as np
import jax
import jax.numpy as jnp
from jax import lax
from jax.experimental import pallas as pl
from jax.experimental.pallas import tpu as pltpu

D_MODEL = 1024
BATCH = 16
SEQ = 256
DEPTH = 4
DEC_BATCH = 2
DEC_SEQ = 1024
PAST_LEN = 256
GRID_W = 64
ROPE_THETA = 10000.0
EPS = 1e-6
D_FF = 4 * D_MODEL
MOD_CHUNKS = 6
LOG2E = 1.4426950408889634

ATT_HEADS, ATT_KV_HEADS, ATT_HEAD_DIM = 8, 2, 128
DIFF_HEADS, DIFF_HEAD_DIM = 8, 64
SWA_HEADS, SWA_KV_HEADS, SWA_HEAD_DIM, WINDOW = 16, 4, 64, 128
MLA_HEADS, MLA_NOPE, MLA_ROPE, MLA_VDIM = 8, 128, 64, 128
MLA_Q_RANK, MLA_KV_RANK = 512, 256

LANES = 128
HALF = LANES // 2
TM = 256
N_PROMPT_TOK = BATCH * SEQ
N_LAT_TOK = DEC_BATCH * DEC_SEQ
N_TOK = N_PROMPT_TOK + N_LAT_TOK
N_PROMPT_TILES = N_PROMPT_TOK // TM
LAT_TQ = 512
LAT_BLOCK0 = N_PROMPT_TOK // DEC_SEQ
COND_ROWS = 8
PROJ_TM = 512
PROJ_BATCHES = PROJ_TM // SEQ
N_PROJ_TILES = N_TOK // PROJ_TM
N_PROJ_PROMPT = N_PROMPT_TOK // PROJ_TM
PROJ_UNIT = 2 * LANES
MLP_TM = 512
MLP_FF_CHUNK = 512
MLP_LOAD_COLS = 256
N_LOAD_STEPS = D_FF // MLP_LOAD_COLS
N_MLP_PROMPT_TILES = N_PROMPT_TOK // MLP_TM
SWA_QB = 128
ATT_UNIT_HEADS = 4
PROMPT_SEQS = 4
VMEM_LIMIT = 56 * 1024 * 1024

F32 = jnp.float32
BF16 = jnp.bfloat16


def _cparams(n_axes):
    return pltpu.CompilerParams(dimension_semantics=("arbitrary",) * n_axes,
                                vmem_limit_bytes=VMEM_LIMIT)


def _dot(a, b):
    return jnp.dot(a, b, preferred_element_type=F32)


def _dot_nt(a, b):
    return lax.dot_general(a, b, (((1,), (1,)), ((), ())), preferred_element_type=F32)


def _dot_tn(a, b):
    return lax.dot_general(a, b, (((0,), (0,)), ((), ())), preferred_element_type=F32)


def _const_spec(shape):
    nd = len(shape)
    return pl.BlockSpec(shape, lambda *_: (0,) * nd, pipeline_mode=pl.Buffered(1))


class _ParamPack:
    def __init__(self):
        self._rows, self.array = [], None

    def add(self, v, repeat=1, scale=1.0):
        n = v.shape[0] * repeat
        row = _ParamRow(self, -(-n // LANES) * LANES)
        self._rows.append((row, [v] * repeat, scale, n))
        return row

    def build(self):
        pieces, scales, offset = [], [], 0
        for row, vs, scale, n in sorted(self._rows, key=lambda r: -r[0].width):
            row.offset = offset
            pieces += vs + ([jnp.zeros((row.width - n,), F32)] if row.width > n else [])
            scales.append(np.full((row.width,), scale, np.float32))
            offset += row.width
        flat = jnp.concatenate([p.astype(F32) for p in pieces]) * jnp.asarray(np.concatenate(scales))
        self.array = flat.reshape(1, offset)


class _ParamRow:
    def __init__(self, pack, width):
        self.pack, self.width, self.offset = pack, width, None

    @property
    def array(self):
        return self.pack.array

    @property
    def spec(self):
        block = self.offset // self.width
        return pl.BlockSpec((1, self.width), lambda *_: (0, block), pipeline_mode=pl.Buffered(1))


def _chunk(ref, c, width=LANES):
    return ref[:, c * width:(c + 1) * width]


def _put(ref, c, val):
    ref[:, c * LANES:(c + 1) * LANES] = val.astype(ref.dtype)


def _tile_group(i, rows):
    n_prompt = N_PROMPT_TOK // rows
    return jnp.where(i < n_prompt, 0, 1 + (i - n_prompt) // (DEC_SEQ // rows))


def _rope_tile(i):
    return jnp.maximum(i - N_PROJ_PROMPT, 0) % (DEC_SEQ // PROJ_TM)


def _norm_mod(x, gain, shift, scale):
    ms = jnp.mean(x * x, axis=-1, keepdims=True)
    return x * lax.rsqrt(ms + EPS) * (gain * (1.0 + scale)) + shift


def _lane_lo(shape):
    return lax.broadcasted_iota(jnp.int32, shape, len(shape) - 1) < HALF


def _rope(y, cos, sin_prev, sin_next, quarter):
    return (y * cos + pltpu.roll(y, quarter, 1) * sin_prev
            + pltpu.roll(y, LANES - quarter, 1) * sin_next)


def _rope_tables(rot_dim):
    half = rot_dim // 2
    quarter = rot_dim // 4
    inv = np.float32(ROPE_THETA) ** (-np.arange(0, half, 2, dtype=np.float32) / np.float32(half))
    pos = np.arange(DEC_SEQ)
    row = (pos // GRID_W).astype(np.float32)
    col = (pos % GRID_W).astype(np.float32)
    lane = np.arange(LANES)
    dd = lane % rot_dim
    q = dd // quarter
    f = dd % quarter
    ang = np.where((q < 2)[None, :], row[:, None], col[:, None]) * inv[f][None, :]
    ang = ang.astype(np.float32)
    cos = np.cos(ang).astype(np.float32)
    sin = np.sin(ang).astype(np.float32)
    odd = (q % 2 == 1)[None, :]
    sin_prev = np.where(odd, sin, 0.0).astype(np.float32)
    sin_next = np.where(odd, 0.0, -sin).astype(np.float32)
    return jnp.asarray(cos), jnp.asarray(sin_prev), jnp.asarray(sin_next)


def _lane_sum_matrix(rows, cols, value=1.0):
    lane = np.arange(LANES)
    m = np.where(rows(lane)[:, None] & cols(lane)[None, :], value, 0.0).astype(np.float32)
    return jnp.asarray(m, dtype=BF16)


def _group_mean_matrix(group):
    lane = np.arange(PROJ_UNIT)
    m = np.where((lane[:, None] // group) == (lane[None, :] // group), 1.0 / group, 0.0)
    return jnp.asarray(m.astype(np.float32), dtype=BF16)


def _group_sum_matrix():
    lane = np.arange(PROJ_UNIT)
    m = np.where((lane[:, None] // LANES) == (lane[None, :] // LANES), 1.0, 0.0)
    return jnp.asarray(m.astype(np.float32), dtype=BF16)


def _sq_bf16(y):
    return (y * y).astype(BF16)


def _head_norm(y, m_ref, gain):
    return y * lax.rsqrt(_dot(_sq_bf16(y), m_ref[...]) + EPS) * gain


def _halves(y):
    return [y[:, t * LANES:(t + 1) * LANES] for t in range(y.shape[1] // LANES)]


def _matmul_units(h, w_ref, n_units, width, emit):
    def unit(u):
        return _dot(h, w_ref[:, u * width:(u + 1) * width])

    nxt = unit(0)
    for u in range(n_units):
        cur = nxt
        if u + 1 < n_units:
            nxt = unit(u + 1)
        emit(u, cur)


def _cast_once(i, w_ref, w_s):
    @pl.when(i == 0)
    def _():
        w_s[...] = w_ref[...].astype(BF16)


def _by_tile_kind(i, body):
    pl.when(i < N_PROJ_PROMPT)(functools.partial(body, False))
    pl.when(i >= N_PROJ_PROMPT)(functools.partial(body, True))


def _rope_args(lat, cos_ref, sp_ref, sn_ref, rot_dim):
    return (cos_ref[...], sp_ref[...], sn_ref[...], rot_dim // 4) if lat else None


def _maybe_rope(y, rope):
    return y if rope is None else _rope(y, *rope)


def _cache_rows(ref, index, val):
    for b in range(PROJ_BATCHES):
        ref[(b, 0) + tuple(index)] = val[b * SEQ:(b + 1) * SEQ]


def _cache_rows_t(ref, indices, val):
    for b in range(PROJ_BATCHES):
        t = val[b * SEQ:(b + 1) * SEQ].T
        for j, index in enumerate(indices):
            ref[(b, 0) + tuple(index)] = t[j * HALF:(j + 1) * HALF]


def _softmax2_parts(s_list, extra=None):
    m = jnp.max(s_list[0], axis=0, keepdims=True)
    for s in s_list[1:]:
        m = jnp.maximum(m, jnp.max(s, axis=0, keepdims=True))
    if extra is not None:
        m = jnp.maximum(m, extra)
    ps = [jnp.exp2(s - m) for s in s_list]
    mass = ps[0].sum(axis=0, keepdims=True)
    for p in ps[1:]:
        mass = mass + p.sum(axis=0, keepdims=True)
    if extra is not None:
        mass = mass + jnp.exp2(extra - m)
    return [p.astype(BF16) for p in ps], 1.0 / mass


def _head_pipeline(n, scores, finish):
    nxt = scores(0)
    for h in range(n):
        cur = nxt
        if h + 1 < n:
            nxt = scores(h + 1)
        finish(h, cur)


def _seq_pipeline(refs, seqs, n, make):
    fns = []
    for b in range(seqs):
        views = [r.at[b * (r.shape[0] // seqs):(b + 1) * (r.shape[0] // seqs)] for r in refs]
        fns.append(make(views))
    _head_pipeline(seqs * n, lambda i: fns[i // n][0](i % n), lambda i, s: fns[i // n][1](i % n, s))


def _pv(ps, values):
    o = None
    for p, v in zip(ps, values):
        t = _dot_tn(v, p)
        o = t if o is None else o + t
    return o


def _split_halves(q):
    lo = _lane_lo(q.shape)
    zero = jnp.zeros_like(q)
    return jnp.where(lo, q, zero), jnp.where(lo, zero, q)


def _mod_kernel(cond_ref, w_ref, b_ref, o_ref):
    c = cond_ref[...]
    s = (c * jax.nn.sigmoid(c)).astype(BF16)
    o_ref[0] = _dot(s, w_ref[0].astype(BF16)) + b_ref[0]


def _modulation(cond, ada_w, ada_b):
    tn = 3072
    n = MOD_CHUNKS * D_MODEL
    return pl.pallas_call(
        _mod_kernel,
        grid=(DEPTH, n // tn),
        in_specs=[
            pl.BlockSpec((COND_ROWS, D_MODEL), lambda l, j: (0, 0)),
            pl.BlockSpec((1, D_MODEL, tn), lambda l, j: (l, 0, j)),
            pl.BlockSpec((1, 1, tn), lambda l, j: (l, 0, j)),
        ],
        out_specs=pl.BlockSpec((1, COND_ROWS, tn), lambda l, j: (l, 0, j)),
        out_shape=jax.ShapeDtypeStruct((DEPTH, COND_ROWS, n), F32),
        compiler_params=_cparams(2),
        name="modulation",
    )(cond, ada_w, ada_b.reshape(DEPTH, 1, n))


def _mod_spec(layer, chunk):
    return pl.BlockSpec((None, COND_ROWS, D_MODEL), lambda i: (layer, 0, chunk))


def _mod_row(ref, i):
    return ref[pl.ds(_tile_group(i, PROJ_TM), 1), :]


_ROPE_SPEC = pl.BlockSpec((PROJ_TM, LANES), lambda i: (_rope_tile(i), 0))
_LANE_MAT_SPEC = _const_spec((LANES, LANES))
_UNIT_MAT_SPEC = _const_spec((PROJ_UNIT, PROJ_UNIT))


def _tok_spec(width):
    return pl.BlockSpec((PROJ_TM, width), lambda i: (i, 0))


_XP_SPEC = pl.BlockSpec((PROJ_TM, D_MODEL), lambda i: (jnp.minimum(i, N_PROJ_PROMPT - 1), 0))
_XS_SPEC = pl.BlockSpec((PROJ_TM, D_MODEL), lambda i: (jnp.maximum(i - N_PROJ_PROMPT, 0), 0))


def _cache_spec(*dims):
    nd = len(dims)
    return pl.BlockSpec((PROJ_BATCHES, 1) + dims,
                        lambda i: (jnp.minimum(i, N_PROJ_PROMPT - 1), 0) + (0,) * nd)


def _cache_shape(*dims):
    return jax.ShapeDtypeStruct((BATCH, 1) + dims, F32)


def _proj_att_kernel(xp_ref, xs_ref, gain_ref, sh_ref, sc_ref, w_ref, qg_ref, kg_ref, m_ref,
                     cos_ref, sp_ref, sn_ref, q_ref, k_ref, v_ref, ck_ref, cv_ref, w_s):
    i = pl.program_id(0)
    _cast_once(i, w_ref, w_s)
    x = jnp.where(i < N_PROJ_PROMPT, xp_ref[...], xs_ref[...])
    h = _norm_mod(x, gain_ref[...], _mod_row(sh_ref, i), _mod_row(sc_ref, i)).astype(BF16)
    per = PROJ_UNIT // LANES
    nq, nk = ATT_HEADS // per, ATT_KV_HEADS // per

    def body(lat):
        rope = _rope_args(lat, cos_ref, sp_ref, sn_ref, ATT_HEAD_DIM)

        def emit(u, y):
            if u < nq + nk:
                y = _head_norm(y, m_ref, qg_ref[...] if u < nq else kg_ref[...])
            for t, yc in enumerate(_halves(y)):
                if u < nq:
                    _put(q_ref, u * per + t, _maybe_rope(yc, rope))
                elif u < nq + nk:
                    kn = _maybe_rope(yc, rope)
                    _put(k_ref, (u - nq) * per + t, kn)
                    if not lat:
                        _cache_rows(ck_ref, [(u - nq) * per + t], kn)
                else:
                    _put(v_ref, (u - nq - nk) * per + t, yc)
                    if not lat:
                        _cache_rows(cv_ref, [(u - nq - nk) * per + t], yc)

        _matmul_units(h, w_s, nq + 2 * nk, PROJ_UNIT, emit)

    _by_tile_kind(i, body)


def _proj_att(xp, xs, mods, gain, w, qg, kg, tables):
    nq, nk = ATT_HEADS * ATT_HEAD_DIM, ATT_KV_HEADS * ATT_HEAD_DIM
    return pl.pallas_call(
        _proj_att_kernel,
        grid=(N_PROJ_TILES,),
        in_specs=[_XP_SPEC, _XS_SPEC, gain.spec, _mod_spec(0, 0), _mod_spec(0, 1),
                  _const_spec(w.shape), qg.spec, kg.spec,
                  _UNIT_MAT_SPEC, _ROPE_SPEC, _ROPE_SPEC, _ROPE_SPEC],
        out_specs=[_tok_spec(nq), _tok_spec(nk), _tok_spec(nk),
                   _cache_spec(ATT_KV_HEADS, SEQ, ATT_HEAD_DIM), _cache_spec(ATT_KV_HEADS, SEQ, ATT_HEAD_DIM)],
        out_shape=[jax.ShapeDtypeStruct((N_TOK, nq), BF16),
                   jax.ShapeDtypeStruct((N_TOK, nk), BF16),
                   jax.ShapeDtypeStruct((N_TOK, nk), BF16),
                   _cache_shape(ATT_KV_HEADS, SEQ, ATT_HEAD_DIM), _cache_shape(ATT_KV_HEADS, SEQ, ATT_HEAD_DIM)],
        scratch_shapes=[pltpu.VMEM(w.shape, BF16)],
        compiler_params=_cparams(1),
        name="proj_att",
    )(xp, xs, gain.array, mods, mods, w, qg.array, kg.array, _group_mean_matrix(ATT_HEAD_DIM), *tables)


def _proj_diff_kernel(h_ref, w_ref, qg_ref, kg_ref, m_ref,
                      cos_ref, sp_ref, sn_ref, q_ref, k_ref, v_ref, ck_ref, cv_ref, w_s):
    i = pl.program_id(0)
    _cast_once(i, w_ref, w_s)
    h = h_ref[...]
    per = PROJ_UNIT // LANES
    nu = DIFF_HEADS // per

    def body(lat):
        rope = _rope_args(lat, cos_ref, sp_ref, sn_ref, DIFF_HEAD_DIM)

        def emit(u, y):
            if u < 2 * nu:
                y = _head_norm(y, m_ref, qg_ref[...] if u < nu else kg_ref[...])
            for t, yc in enumerate(_halves(y)):
                hd = (u % nu) * per + t
                if u < nu:
                    _put(q_ref, hd, _maybe_rope(yc, rope))
                elif u < 2 * nu:
                    kn = _maybe_rope(yc, rope)
                    _put(k_ref, hd, kn)
                    if not lat:
                        _cache_rows_t(ck_ref, [[hd, 0], [hd, 1]], kn)
                else:
                    _put(v_ref, hd, yc)
                    if not lat:
                        _cache_rows(cv_ref, [hd], yc)

        _matmul_units(h, w_s, 3 * nu, PROJ_UNIT, emit)

    _by_tile_kind(i, body)


def _proj_diff(h, w, qg, kg, tables):
    n = DIFF_HEADS * 2 * DIFF_HEAD_DIM
    return pl.pallas_call(
        _proj_diff_kernel,
        grid=(N_PROJ_TILES,),
        in_specs=[_tok_spec(D_MODEL), _const_spec(w.shape), qg.spec, kg.spec,
                  _UNIT_MAT_SPEC, _ROPE_SPEC, _ROPE_SPEC, _ROPE_SPEC],
        out_specs=[_tok_spec(n), _tok_spec(n), _tok_spec(n),
                   _cache_spec(DIFF_HEADS, 2, DIFF_HEAD_DIM, SEQ), _cache_spec(DIFF_HEADS, SEQ, 2 * DIFF_HEAD_DIM)],
        out_shape=[jax.ShapeDtypeStruct((N_TOK, n), BF16)] * 3
                  + [_cache_shape(DIFF_HEADS, 2, DIFF_HEAD_DIM, SEQ),
                     _cache_shape(DIFF_HEADS, SEQ, 2 * DIFF_HEAD_DIM)],
        scratch_shapes=[pltpu.VMEM(w.shape, BF16)],
        compiler_params=_cparams(1),
        name="proj_diff",
    )(h, w, qg.array, kg.array, _group_mean_matrix(DIFF_HEAD_DIM), *tables)


def _dup_halves(yc):
    lo = _lane_lo(yc.shape)
    sw = pltpu.roll(yc, HALF, 1)
    return jnp.where(lo, yc, sw), jnp.where(lo, sw, yc)


def _proj_swa_kernel(h_ref, w_ref, qg_ref, kg_ref, m_ref,
                     cos_ref, sp_ref, sn_ref, q_ref, kd_ref, vd_ref, ck_ref, cv_ref, w_s):
    i = pl.program_id(0)
    _cast_once(i, w_ref, w_s)
    h = h_ref[...]
    per = PROJ_UNIT // LANES
    nq = SWA_HEADS * SWA_HEAD_DIM // PROJ_UNIT
    nk = SWA_KV_HEADS * SWA_HEAD_DIM // PROJ_UNIT

    def body(lat):
        rope = _rope_args(lat, cos_ref, sp_ref, sn_ref, SWA_HEAD_DIM)

        def emit(u, y):
            if u < nq + nk:
                y = _head_norm(y, m_ref, qg_ref[...] if u < nq else kg_ref[...])
            for t, yc in enumerate(_halves(y)):
                if u < nq:
                    _put(q_ref, u * per + t, _maybe_rope(yc, rope))
                    continue
                if u < nq + nk:
                    j, c_ref, d_ref = (u - nq) * per + t, ck_ref, kd_ref
                    yc = _maybe_rope(yc, rope)
                else:
                    j, c_ref, d_ref = (u - nq - nk) * per + t, cv_ref, vd_ref
                for a, dup in enumerate(_dup_halves(yc)):
                    _put(d_ref, 2 * j + a, dup)
                if not lat:
                    _cache_rows_t(c_ref, [[2 * j], [2 * j + 1]], yc)

        _matmul_units(h, w_s, nq + 2 * nk, PROJ_UNIT, emit)

    _by_tile_kind(i, body)


def _proj_swa(h, w, qg, kg, tables):
    nq, nk = SWA_HEADS * SWA_HEAD_DIM, SWA_KV_HEADS * SWA_HEAD_DIM
    return pl.pallas_call(
        _proj_swa_kernel,
        grid=(N_PROJ_TILES,),
        in_specs=[_tok_spec(D_MODEL), _const_spec(w.shape), qg.spec, kg.spec,
                  _UNIT_MAT_SPEC, _ROPE_SPEC, _ROPE_SPEC, _ROPE_SPEC],
        out_specs=[_tok_spec(nq), _tok_spec(2 * nk), _tok_spec(2 * nk),
                   _cache_spec(SWA_KV_HEADS, SWA_HEAD_DIM, SEQ), _cache_spec(SWA_KV_HEADS, SWA_HEAD_DIM, SEQ)],
        out_shape=[jax.ShapeDtypeStruct((N_TOK, nq), BF16),
                   jax.ShapeDtypeStruct((N_TOK, 2 * nk), BF16),
                   jax.ShapeDtypeStruct((N_TOK, 2 * nk), BF16),
                   _cache_shape(SWA_KV_HEADS, SWA_HEAD_DIM, SEQ), _cache_shape(SWA_KV_HEADS, SWA_HEAD_DIM, SEQ)],
        scratch_shapes=[pltpu.VMEM(w.shape, BF16)],
        compiler_params=_cparams(1),
        name="proj_swa",
    )(h, w, qg.array, kg.array, _group_mean_matrix(SWA_HEAD_DIM), *tables)


def _mla_lane_matrices():
    everything = lambda lane: lane >= 0
    return (_lane_sum_matrix(everything, everything),
            _lane_sum_matrix(lambda lane: lane < HALF, everything),
            _lane_sum_matrix(lambda lane: lane >= HALF, everything))


def _mla_keys(ckv, kpe, w_s, kg_ref, kgp_ref, sum_ref, lo_ref, rope, kn_ref, kp_ref, v_ref):
    pe_ss = _dot(_sq_bf16(kpe), lo_ref[...])
    pe_ss = jnp.concatenate([pe_ss, pe_ss], axis=1)
    lo = _lane_lo(kpe.shape)
    inv_d = 1.0 / (MLA_NOPE + MLA_ROPE)

    def emit(j, y):
        kn = jnp.concatenate([y[:, :LANES], y[:, 2 * LANES:3 * LANES]], axis=1)
        r = lax.rsqrt((_dot(_sq_bf16(kn), sum_ref[...]) + pe_ss) * inv_d + EPS)
        kn = kn * r * kg_ref[...]
        for a in range(2):
            _put(kn_ref, 2 * j + a, kn[:, a * LANES:(a + 1) * LANES])
            _put(v_ref, 2 * j + a, y[:, (2 * a + 1) * LANES:(2 * a + 2) * LANES])
        pe = kpe * jnp.where(lo, r[:, :LANES], r[:, LANES:]) * kgp_ref[...]
        _put(kp_ref, j, _maybe_rope(pe, rope))

    _matmul_units(ckv, w_s, MLA_HEADS // 2, 4 * LANES, emit)


def _proj_mla_kernel(h_ref, w_in_ref, qa_ref, kva_ref, w_uq_ref, qg_ref, qgp_ref, all_ref, lo_ref, hi_ref,
                     w_ukv_ref, kg_ref, kgp_ref, sum_ref, cos_ref, sp_ref, sn_ref,
                     qn_ref, qp_ref, c_ckv_ref, c_kpe_ref, kn_ref, kp_ref, v_ref,
                     w_in_s, w_uq_s, w_ukv_s):
    i = pl.program_id(0)
    n_in = MLA_Q_RANK + MLA_KV_RANK + MLA_ROPE

    @pl.when(i == 0)
    def _():
        w_in_s[...] = jnp.zeros_like(w_in_s)
        w_in_s[:, :n_in] = w_in_ref[...].astype(BF16)
        w_uq_s[...] = w_uq_ref[...].astype(BF16)
        w_ukv_s[...] = w_ukv_ref[...].astype(BF16)

    y = _dot(h_ref[...], w_in_s[...])
    c_q = y[:, :MLA_Q_RANK]
    c_kv = y[:, MLA_Q_RANK:MLA_Q_RANK + MLA_KV_RANK]
    kpe = y[:, MLA_Q_RANK + MLA_KV_RANK:]
    kpe = kpe + pltpu.roll(kpe, HALF, 1)
    ckv = c_kv * lax.rsqrt(jnp.mean(c_kv * c_kv, axis=-1, keepdims=True) + EPS) * kva_ref[...]
    cq = (c_q * lax.rsqrt(jnp.mean(c_q * c_q, axis=-1, keepdims=True) + EPS) * qa_ref[...]).astype(BF16)
    lo = _lane_lo((PROJ_TM, LANES))
    inv_d = 1.0 / (MLA_NOPE + MLA_ROPE)

    def body(lat):
        rope = _rope_args(lat, cos_ref, sp_ref, sn_ref, MLA_ROPE)
        if not lat:
            _cache_rows(c_ckv_ref, [], ckv)
            _cache_rows_t(c_kpe_ref, [[]], kpe)

        def emit(j, yq):
            y0, y1, y2 = _halves(yq)
            nopes = (y0, jnp.where(lo, pltpu.roll(y1, HALF, 1), pltpu.roll(y2, HALF, 1)))
            pe = jnp.where(lo, y1, y2)
            pe_sq = _sq_bf16(pe)
            rs = []
            for a, half_ref in enumerate((lo_ref, hi_ref)):
                ss = _dot(_sq_bf16(nopes[a]), all_ref[...]) + _dot(pe_sq, half_ref[...])
                r = lax.rsqrt(ss * inv_d + EPS)
                rs.append(r)
                _put(qn_ref, 2 * j + a, nopes[a] * r * qg_ref[...])
            _put(qp_ref, j, _maybe_rope(pe * jnp.where(lo, rs[0], rs[1]) * qgp_ref[...], rope))

        _matmul_units(cq, w_uq_s, MLA_HEADS // 2, 3 * LANES, emit)
        _mla_keys(ckv.astype(BF16), kpe, w_ukv_s, kg_ref, kgp_ref, sum_ref, lo_ref, rope,
                  kn_ref, kp_ref, v_ref)

    _by_tile_kind(i, body)


def _proj_mla(h, w_in, qa, kva, w_uq, qg, qgp, w_ukv, kg, kgp, tables):
    n_nope = MLA_HEADS * MLA_NOPE
    n_pe = MLA_HEADS * MLA_ROPE
    n_in = -(-w_in.shape[1] // LANES) * LANES
    return pl.pallas_call(
        _proj_mla_kernel,
        grid=(N_PROJ_TILES,),
        in_specs=[_tok_spec(D_MODEL),
                  _const_spec(w_in.shape), qa.spec, kva.spec,
                  _const_spec(w_uq.shape), qg.spec, qgp.spec,
                  _LANE_MAT_SPEC, _LANE_MAT_SPEC, _LANE_MAT_SPEC,
                  _const_spec(w_ukv.shape), kg.spec, kgp.spec, _UNIT_MAT_SPEC,
                  _ROPE_SPEC, _ROPE_SPEC, _ROPE_SPEC],
        out_specs=[_tok_spec(n_nope), _tok_spec(n_pe),
                   _cache_spec(SEQ, MLA_KV_RANK), _cache_spec(MLA_ROPE, SEQ),
                   _tok_spec(n_nope), _tok_spec(n_pe), _tok_spec(n_nope)],
        out_shape=[jax.ShapeDtypeStruct((N_TOK, n_nope), BF16),
                   jax.ShapeDtypeStruct((N_TOK, n_pe), BF16),
                   _cache_shape(SEQ, MLA_KV_RANK), _cache_shape(MLA_ROPE, SEQ),
                   jax.ShapeDtypeStruct((N_TOK, n_nope), BF16),
                   jax.ShapeDtypeStruct((N_TOK, n_pe), BF16),
                   jax.ShapeDtypeStruct((N_TOK, n_nope), BF16)],
        scratch_shapes=[pltpu.VMEM((D_MODEL, n_in), BF16), pltpu.VMEM(w_uq.shape, BF16),
                        pltpu.VMEM(w_ukv.shape, BF16)],
        compiler_params=_cparams(1),
        name="proj_mla",
    )(h, w_in, qa.array, kva.array, w_uq, qg.array, qgp.array, *_mla_lane_matrices(),
      w_ukv, kg.array, kgp.array, _group_sum_matrix(), *tables)


def _mla_ctx_kernel(ckv_ref, kpe_ref, w_ref, kg_ref, kgp_ref, sum_ref, lo_ref, kn_ref, kp_ref, v_ref):
    _mla_keys(ckv_ref[...].astype(BF16), kpe_ref[...], w_ref[...].astype(BF16), kg_ref, kgp_ref,
              sum_ref, lo_ref, None, kn_ref, kp_ref, v_ref)


def _mla_ctx(ckv, kpe_dup, w_ukv, kg, kgp):
    n = ckv.shape[0]
    n_nope = MLA_HEADS * MLA_NOPE
    n_pe = MLA_HEADS * MLA_ROPE
    _, m_lo, _ = _mla_lane_matrices()
    return pl.pallas_call(
        _mla_ctx_kernel,
        grid=(n // PROJ_TM,),
        in_specs=[_tok_spec(MLA_KV_RANK), _tok_spec(LANES), _const_spec(w_ukv.shape),
                  kg.spec, kgp.spec, _UNIT_MAT_SPEC, _LANE_MAT_SPEC],
        out_specs=[_tok_spec(n_nope), _tok_spec(n_pe), _tok_spec(n_nope)],
        out_shape=[jax.ShapeDtypeStruct((n, n_nope), BF16),
                   jax.ShapeDtypeStruct((n, n_pe), BF16),
                   jax.ShapeDtypeStruct((n, n_nope), BF16)],
        compiler_params=_cparams(1),
        name="mla_ctx",
    )(ckv, kpe_dup, w_ukv, kg.array, kgp.array, _group_sum_matrix(), m_lo)


def _prompt_spec(width):
    return pl.BlockSpec((PROMPT_SEQS * TM, width), lambda b: (b, 0))


def _latq_spec(rows, width):
    per = DEC_SEQ // rows
    return pl.BlockSpec((rows, width), lambda b, t: (N_PROMPT_TOK // rows + b * per + t, 0))


def _latkv_spec(width):
    return pl.BlockSpec((DEC_SEQ, width), lambda b, t: (LAT_BLOCK0 + b, 0))


def _lato_spec(rows):
    per = DEC_SEQ // rows
    return pl.BlockSpec((rows, D_MODEL), lambda b, t: (b * per + t, 0))


def _att_kernel(*refs, with_ctx, seqs):
    if with_ctx:
        q_ref, k_ref, v_ref, kc_ref, vc_ref, o_ref = refs
    else:
        q_ref, k_ref, v_ref, o_ref = refs
    tq = q_ref.shape[0] // seqs
    nu = ATT_UNIT_HEADS
    per_kv = ATT_HEADS // ATT_KV_HEADS // nu

    def make(views):
        q_v, k_v, v_v, o_v = views

        def scores(u):
            q = jnp.concatenate([_chunk(q_v, u * nu + g) for g in range(nu)], axis=0)
            s_list = [_dot_nt(_chunk(k_v, u // per_kv), q)]
            if with_ctx:
                s_list.append(_dot_nt(kc_ref[u // per_kv].astype(BF16), q))
            return s_list

        def finish(u, s_list):
            values = [_chunk(v_v, u // per_kv)]
            if with_ctx:
                values.append(vc_ref[u // per_kv].astype(BF16))
            ps, inv = _softmax2_parts(s_list)
            o = _pv(ps, values) * inv
            for g in range(nu):
                o_v[:, (u * nu + g) * LANES:(u * nu + g + 1) * LANES] = (
                    o[:, g * tq:(g + 1) * tq].T.astype(o_v.dtype))

        return scores, finish

    _seq_pipeline((q_ref, k_ref, v_ref, o_ref), seqs, ATT_HEADS // nu, make)


def _att_attend(q, k, v, cache_k, cache_v):
    nk = ATT_KV_HEADS * ATT_HEAD_DIM
    out_p = pl.pallas_call(
        functools.partial(_att_kernel, with_ctx=False, seqs=PROMPT_SEQS),
        grid=(N_PROMPT_TILES // PROMPT_SEQS,),
        in_specs=[_prompt_spec(D_MODEL), _prompt_spec(nk), _prompt_spec(nk)],
        out_specs=_prompt_spec(D_MODEL),
        out_shape=jax.ShapeDtypeStruct((N_PROMPT_TOK, D_MODEL), BF16),
        compiler_params=_cparams(1),
        name="att_prompt",
    )(q, k, v)
    ctx = pl.BlockSpec((None, None, ATT_KV_HEADS, PAST_LEN, LANES), lambda b, t: (b, 0, 0, 0, 0))
    out_s = pl.pallas_call(
        functools.partial(_att_kernel, with_ctx=True, seqs=1),
        grid=(DEC_BATCH, DEC_SEQ // LAT_TQ),
        in_specs=[_latq_spec(LAT_TQ, D_MODEL), _latkv_spec(nk), _latkv_spec(nk), ctx, ctx],
        out_specs=_lato_spec(LAT_TQ),
        out_shape=jax.ShapeDtypeStruct((N_LAT_TOK, D_MODEL), BF16),
        compiler_params=_cparams(2),
        name="att_latent",
    )(q, k, v, cache_k, cache_v)
    return out_p, out_s


def _diff_kernel(*refs, lam_init, with_ctx, seqs):
    if with_ctx:
        (q_ref, k_ref, v_ref, kc_ref, vc_ref, lq1_ref, lk1_ref, lq2_ref, lk2_ref, sub_ref, o_ref) = refs
    else:
        (q_ref, k_ref, v_ref, lq1_ref, lk1_ref, lq2_ref, lk2_ref, sub_ref, o_ref) = refs
    tq = q_ref.shape[0] // seqs
    lam = (jnp.exp(jnp.sum(lq1_ref[...] * lk1_ref[...], axis=-1, keepdims=True))
           - jnp.exp(jnp.sum(lq2_ref[...] * lk2_ref[...], axis=-1, keepdims=True)) + lam_init)
    diag = (lax.broadcasted_iota(jnp.int32, (LANES, LANES), 0)
            == lax.broadcasted_iota(jnp.int32, (LANES, LANES), 1))
    sub = jnp.sum(jnp.where(diag, sub_ref[...] * (1.0 - lam_init), 0.0), axis=1, keepdims=True)

    def make(views):
        q_v, k_v, v_v, o_v = views

        def scores(hd):
            q = jnp.concatenate(_split_halves(_chunk(q_v, hd)), axis=0)
            s_list = [_dot_nt(_chunk(k_v, hd), q)]
            if with_ctx:
                s_list.append(_dot_nt(kc_ref[hd].astype(BF16), q))
            return s_list

        def finish(hd, s_list):
            values = [_chunk(v_v, hd)]
            if with_ctx:
                values.append(vc_ref[hd].astype(BF16))
            ps, inv = _softmax2_parts(s_list)
            o = (_pv([p[:, :tq] for p in ps], values) * inv[:, :tq]
                 - _pv([p[:, tq:] for p in ps], values) * (lam * inv[:, tq:]))
            o = o * lax.rsqrt(jnp.mean(o * o, axis=0, keepdims=True) + EPS) * sub
            o_v[:, hd * LANES:(hd + 1) * LANES] = o.T.astype(o_v.dtype)

        return scores, finish

    _seq_pipeline((q_ref, k_ref, v_ref, o_ref), seqs, DIFF_HEADS, make)


def _diff_attend(q, k, v, cache_k_pair, cache_v, lq1, lk1, lq2, lk2, subln, lam_init):
    small_specs = [p.spec for p in (lq1, lk1, lq2, lk2, subln)]
    small = [p.array for p in (lq1, lk1, lq2, lk2, subln)]
    out_p = pl.pallas_call(
        functools.partial(_diff_kernel, lam_init=lam_init, with_ctx=False, seqs=PROMPT_SEQS),
        grid=(N_PROMPT_TILES // PROMPT_SEQS,),
        in_specs=[_prompt_spec(D_MODEL)] * 3 + small_specs,
        out_specs=_prompt_spec(D_MODEL),
        out_shape=jax.ShapeDtypeStruct((N_PROMPT_TOK, D_MODEL), BF16),
        compiler_params=_cparams(1),
        name="diff_prompt",
    )(q, k, v, *small)
    out_s = pl.pallas_call(
        functools.partial(_diff_kernel, lam_init=lam_init, with_ctx=True, seqs=1),
        grid=(DEC_BATCH, DEC_SEQ // LAT_TQ),
        in_specs=[_latq_spec(LAT_TQ, D_MODEL), _latkv_spec(D_MODEL), _latkv_spec(D_MODEL),
                  pl.BlockSpec((None, DIFF_HEADS, PAST_LEN, LANES), lambda b, t: (b, 0, 0, 0)),
                  pl.BlockSpec((None, None, DIFF_HEADS, PAST_LEN, LANES), lambda b, t: (b, 0, 0, 0, 0))]
                 + small_specs,
        out_specs=_lato_spec(LAT_TQ),
        out_shape=jax.ShapeDtypeStruct((N_LAT_TOK, D_MODEL), BF16),
        compiler_params=_cparams(2),
        name="diff_latent",
    )(q, k, v, cache_k_pair, cache_v, *small)
    return out_p, out_s


def _swa_pipeline(q_ref, o_ref, seq_refs, sink_ref, score_fns, value_fns, seqs=1):
    tq = q_ref.shape[0] // seqs
    per_kv = SWA_HEADS // SWA_KV_HEADS // 2
    first = lax.broadcasted_iota(jnp.int32, (LANES, tq), 0) < HALF

    def make(views):
        q_v, o_v = views[:2]
        kv_views = views[2:]

        def scores(c):
            q = jnp.concatenate(_split_halves(_chunk(q_v, c)), axis=0)
            return [fn(kv_views, c // per_kv, q) for fn in score_fns]

        def finish(c, s_list):
            sink = jnp.concatenate([jnp.full((1, tq), sink_ref[2 * c + a] * LOG2E, F32) for a in range(2)],
                                   axis=1)
            ps, inv = _softmax2_parts(s_list, extra=sink)
            o = _pv(ps, [fn(kv_views, c // per_kv) for fn in value_fns]) * inv
            oc = jnp.where(first, o[:, :tq], o[:, tq:])
            o_v[:, c * LANES:(c + 1) * LANES] = oc.T.astype(o_v.dtype)

        return scores, finish

    _seq_pipeline((q_ref, o_ref) + tuple(seq_refs), seqs, SWA_HEADS // 2, make)


def _swa_prompt_kernel(sink_ref, q_ref, k_ref, v_ref, o_ref):
    _swa_pipeline(q_ref, o_ref, (k_ref, v_ref), sink_ref,
                  [lambda kv_v, kv, q: _dot_nt(_chunk(kv_v[0], kv), q)],
                  [lambda kv_v, kv: _chunk(kv_v[1], kv)], seqs=PROMPT_SEQS)


def _swa_latent_kernel(sink_ref, q_ref, k_ref, v_ref, kc_ref, vc_ref, o_ref):
    n = pl.program_id(1)
    tq = q_ref.shape[0]
    span = SWA_QB + 2 * WINDOW
    start = pl.multiple_of(jnp.clip(n * SWA_QB - WINDOW, 0, DEC_SEQ - span), WINDOW)
    cols = lax.broadcasted_iota(jnp.int32, (span, 2 * tq), 1)
    qpos = n * SWA_QB + jnp.bitwise_and(cols, tq - 1)
    kpos = start + lax.broadcasted_iota(jnp.int32, (span, 2 * tq), 0)
    valid = jnp.abs(qpos - kpos) <= WINDOW

    def local(ref, kv):
        return ref[pl.ds(start, span), kv * LANES:(kv + 1) * LANES]

    _swa_pipeline(q_ref, o_ref, (), sink_ref,
                  [lambda _, kv, q: jnp.where(valid, _dot_nt(local(k_ref, kv), q), -1e30),
                   lambda _, kv, q: _dot_nt(kc_ref[kv], q)],
                  [lambda _, kv: local(v_ref, kv), lambda _, kv: vc_ref[kv]])


def _swa_attend(q, kd, vd, cache_kd, cache_vd, sink):
    nkd = 2 * SWA_KV_HEADS * SWA_HEAD_DIM
    smem = pl.BlockSpec(memory_space=pltpu.SMEM)
    out_p = pl.pallas_call(
        _swa_prompt_kernel,
        grid=(N_PROMPT_TILES // PROMPT_SEQS,),
        in_specs=[smem, _prompt_spec(D_MODEL), _prompt_spec(nkd), _prompt_spec(nkd)],
        out_specs=_prompt_spec(D_MODEL),
        out_shape=jax.ShapeDtypeStruct((N_PROMPT_TOK, D_MODEL), BF16),
        compiler_params=_cparams(1),
        name="swa_prompt",
    )(sink, q, kd, vd)
    ctx = pl.BlockSpec((None, SWA_KV_HEADS, PAST_LEN, LANES), lambda b, n: (b, 0, 0, 0))
    out_s = pl.pallas_call(
        _swa_latent_kernel,
        grid=(DEC_BATCH, DEC_SEQ // SWA_QB),
        in_specs=[smem, _latq_spec(SWA_QB, D_MODEL), _latkv_spec(nkd), _latkv_spec(nkd), ctx, ctx],
        out_specs=_lato_spec(SWA_QB),
        out_shape=jax.ShapeDtypeStruct((N_LAT_TOK, D_MODEL), BF16),
        compiler_params=_cparams(2),
        name="swa_latent",
    )(sink, q, kd, vd, cache_kd, cache_vd)
    return out_p, out_s


def _mla_kernel(*refs, with_ctx, seqs):
    if with_ctx:
        (qn_ref, qp_ref, kn_ref, kp_ref, v_ref, knc_ref, kpc_ref, vc_ref, o_ref) = refs
    else:
        (qn_ref, qp_ref, kn_ref, kp_ref, v_ref, o_ref) = refs

    def make(views):
        qn_v, qp_v, kn_v, kp_v, v_v, o_v = views

        def scores(hd):
            j, a = hd // 2, hd % 2
            q = jnp.concatenate([_chunk(qn_v, hd), _split_halves(_chunk(qp_v, j))[a]], axis=1)
            s_list = [_dot_nt(jnp.concatenate([_chunk(kn_v, hd), _chunk(kp_v, j)], axis=1), q)]
            if with_ctx:
                s_list.append(_dot_nt(jnp.concatenate([_chunk(knc_ref, hd), _chunk(kpc_ref, j)], axis=1), q))
            return s_list

        def finish(hd, s_list):
            values = [_chunk(v_v, hd)]
            if with_ctx:
                values.append(_chunk(vc_ref, hd))
            ps, inv = _softmax2_parts(s_list)
            o_v[:, hd * LANES:(hd + 1) * LANES] = (_pv(ps, values) * inv).T.astype(o_v.dtype)

        return scores, finish

    _seq_pipeline((qn_ref, qp_ref, kn_ref, kp_ref, v_ref, o_ref), seqs, MLA_HEADS, make)


def _mla_attend(qn, qp, kn, kp, v, knc, kpc, vc):
    n_pe = MLA_HEADS * MLA_ROPE
    out_p = pl.pallas_call(
        functools.partial(_mla_kernel, with_ctx=False, seqs=PROMPT_SEQS),
        grid=(N_PROMPT_TILES // PROMPT_SEQS,),
        in_specs=[_prompt_spec(D_MODEL), _prompt_spec(n_pe), _prompt_spec(D_MODEL), _prompt_spec(n_pe),
                  _prompt_spec(D_MODEL)],
        out_specs=_prompt_spec(D_MODEL),
        out_shape=jax.ShapeDtypeStruct((N_PROMPT_TOK, D_MODEL), BF16),
        compiler_params=_cparams(1),
        name="mla_prompt",
    )(qn, qp, kn, kp, v)

    def ctx(width):
        return pl.BlockSpec((PAST_LEN, width), lambda b, t: (b, 0))

    out_s = pl.pallas_call(
        functools.partial(_mla_kernel, with_ctx=True, seqs=1),
        grid=(DEC_BATCH, DEC_SEQ // LAT_TQ),
        in_specs=[_latq_spec(LAT_TQ, D_MODEL), _latq_spec(LAT_TQ, n_pe),
                  _latkv_spec(D_MODEL), _latkv_spec(n_pe), _latkv_spec(D_MODEL),
                  ctx(D_MODEL), ctx(n_pe), ctx(D_MODEL)],
        out_specs=_lato_spec(LAT_TQ),
        out_shape=jax.ShapeDtypeStruct((N_LAT_TOK, D_MODEL), BF16),
        compiler_params=_cparams(2),
        name="mla_latent",
    )(qn, qp, kn, kp, v, knc, kpc, vc)
    return out_p, out_s


def _omlp_kernel(*refs, first, last):
    refs = list(refs)
    ap_ref, as_ref, wo_ref = refs[:3]
    x_refs = refs[3:5] if first else refs[3:4]
    refs = refs[3 + len(x_refs):]
    g1_ref, gain_ref, sh_ref, sc_ref, g2_ref, w1c_ref, w2c_ref = refs[:7]
    refs = refs[7:]
    if last:
        op_ref, os_ref, wo_s, w1_s, w2_s = refs
    else:
        ngain_ref, nsh_ref, nsc_ref, o_ref, hn_ref, wo_s, w1_s, w2_s = refs
    s = pl.program_id(0)
    per = MLP_FF_CHUNK // MLP_LOAD_COLS
    n_chunks = D_FF // MLP_FF_CHUNK
    half = MLP_TM // 2

    @pl.when(s == 0)
    def _():
        wo_s[...] = wo_ref[...].astype(BF16)

    for part in range(per):
        @pl.when((s < N_LOAD_STEPS) & (s % per == part))
        def _(part=part):
            w1_s[s // per, :, part * MLP_LOAD_COLS:(part + 1) * MLP_LOAD_COLS] = w1c_ref[...].astype(BF16)

    @pl.when(s < N_LOAD_STEPS)
    def _():
        w2_s[s // per, pl.ds(pl.multiple_of((s % per) * MLP_LOAD_COLS, MLP_LOAD_COLS), MLP_LOAD_COLS), :] = (
            w2c_ref[...].astype(BF16))

    @pl.when(s >= N_LOAD_STEPS)
    def _():
        t = s - N_LOAD_STEPS
        is_prompt = t < N_MLP_PROMPT_TILES
        grp = _tile_group(t, MLP_TM)

        def mod(ref):
            return ref[pl.ds(grp, 1), :]

        rows = [slice(r * half, (r + 1) * half) for r in range(2)]
        o = [_dot(jnp.where(is_prompt, ap_ref[rw, :], as_ref[rw, :]), wo_s[...]) for rw in rows]
        x1, h, u0 = [], [], []
        for r, rw in enumerate(rows):
            x = jnp.where(is_prompt, x_refs[0][rw, :], x_refs[1][rw, :]) if first else x_refs[0][rw, :]
            x1.append(x + mod(g1_ref) * o[r])
            h.append(_norm_mod(x1[r], gain_ref[...], mod(sh_ref), mod(sc_ref)).astype(BF16))
            u0.append(_dot(h[r], w1_s[0]))
        h = jnp.concatenate(h, axis=0)
        acc = []

        def up(c):
            return jnp.concatenate(u0, axis=0) if c == 0 else _dot(h, w1_s[c])

        def down(c, u):
            u = jnp.square(jnp.maximum(u, 0.0)).astype(BF16)
            if c + 1 < n_chunks:
                y = _dot(u, w2_s[c])
                acc[:] = [y if not acc else acc[0] + y]
            else:
                acc[:] = [acc[0][rw] + _dot(u[rw], w2_s[c]) for rw in rows]

        _head_pipeline(n_chunks, up, down)
        for r, rw in enumerate(rows):
            out = x1[r] + mod(g2_ref) * acc[r]
            if last:
                @pl.when(is_prompt)
                def _(out=out, rw=rw):
                    op_ref[rw, :] = out

                @pl.when(jnp.logical_not(is_prompt))
                def _(out=out, rw=rw):
                    os_ref[rw, :] = out
            else:
                o_ref[rw, :] = out
                hn_ref[rw, :] = _norm_mod(out, ngain_ref[...], mod(nsh_ref), mod(nsc_ref)).astype(BF16)


def _omlp(attn_p, attn_s, w_o, x, mods, gain_ffn, w1_all, w2_all, layer, next_gain):
    first, last = layer == 0, next_gain is None
    n_lat_tiles = N_LAT_TOK // MLP_TM

    def tok(s):
        return jnp.maximum(s - N_LOAD_STEPS, 0)

    p_spec = pl.BlockSpec((MLP_TM, D_MODEL), lambda s: (jnp.minimum(tok(s), N_MLP_PROMPT_TILES - 1), 0))
    l_spec = pl.BlockSpec((MLP_TM, D_MODEL),
                          lambda s: (jnp.clip(tok(s) - N_MLP_PROMPT_TILES, 0, n_lat_tiles - 1), 0))
    w1_spec = pl.BlockSpec((None, D_MODEL, MLP_LOAD_COLS),
                           lambda s: (layer, 0, jnp.minimum(s, N_LOAD_STEPS - 1)))
    w2_spec = pl.BlockSpec((None, MLP_LOAD_COLS, D_MODEL),
                           lambda s: (layer, jnp.minimum(s, N_LOAD_STEPS - 1), 0))
    t_spec = pl.BlockSpec((MLP_TM, D_MODEL), lambda s: (tok(s), 0))
    n_chunks = D_FF // MLP_FF_CHUNK
    split = ([p_spec, l_spec], [jax.ShapeDtypeStruct((N_PROMPT_TOK, D_MODEL), F32),
                                jax.ShapeDtypeStruct((N_LAT_TOK, D_MODEL), F32)])
    in_specs = ([p_spec, l_spec, _const_spec(w_o.shape)] + (split[0] if first else [t_spec])
                + [_mod_spec(layer, 2), gain_ffn.spec, _mod_spec(layer, 3), _mod_spec(layer, 4),
                   _mod_spec(layer, 5),
                   w1_spec, w2_spec])
    args = ([attn_p, attn_s, w_o] + (list(x) if first else [x])
            + [mods, gain_ffn.array, mods, mods, mods, w1_all, w2_all])
    if last:
        out_specs, out_shape = split
    else:
        in_specs += [next_gain.spec, _mod_spec(layer + 1, 0), _mod_spec(layer + 1, 1)]
        args += [next_gain.array, mods, mods]
        out_specs = [t_spec, t_spec]
        out_shape = [jax.ShapeDtypeStruct((N_TOK, D_MODEL), F32), jax.ShapeDtypeStruct((N_TOK, D_MODEL), BF16)]
    return pl.pallas_call(
        functools.partial(_omlp_kernel, first=first, last=last),
        grid=(N_LOAD_STEPS + N_TOK // MLP_TM,),
        in_specs=in_specs,
        out_specs=out_specs,
        out_shape=out_shape,
        scratch_shapes=[pltpu.VMEM((D_MODEL, D_MODEL), BF16),
                        pltpu.VMEM((n_chunks, D_MODEL, MLP_FF_CHUNK), BF16),
                        pltpu.VMEM((n_chunks, MLP_FF_CHUNK, D_MODEL), BF16)],
        compiler_params=_cparams(1),
        name="omlp",
    )(*args)


def kernel(x_prompt, x_sample, cache_att_k, cache_att_v, cache_diff_k, cache_diff_v, cache_swa_k, cache_swa_v, cache_mla_ckv, cache_mla_kpe, c, c_ctx, ada_w, ada_b, norm_mix, norm_ffn, att_w_qkv, att_q_norm, att_k_norm, att_w_o, diff_w_qkv, diff_q_norm, diff_k_norm, diff_lq1, diff_lk1, diff_lq2, diff_lk2, diff_subln, diff_w_o, swa_w_qkv, swa_q_norm, swa_k_norm, swa_sink, swa_w_o, mla_w_in, mla_q_a_norm, mla_kv_a_norm, mla_w_uq, mla_w_ukv, mla_q_norm, mla_k_norm, mla_w_o, mlp_w1, mlp_w2):
    xp = x_prompt.reshape(N_PROMPT_TOK, D_MODEL)
    xs = x_sample.reshape(N_LAT_TOK, D_MODEL)
    cond = jnp.concatenate([c_ctx[None], c, jnp.zeros((COND_ROWS - 1 - DEC_BATCH, D_MODEL), F32)], axis=0)
    mods_all = _modulation(cond, ada_w, ada_b)

    tab_att = _rope_tables(ATT_HEAD_DIM)
    tab_64 = _rope_tables(DIFF_HEAD_DIM)

    pk = _ParamPack()
    g_mix = [pk.add(norm_mix[l]) for l in range(DEPTH)]
    g_ffn = [pk.add(norm_ffn[l]) for l in range(DEPTH)]
    att_qg = pk.add(att_q_norm[0], PROJ_UNIT // ATT_HEAD_DIM, ATT_HEAD_DIM ** -0.5 * LOG2E)
    att_kg = pk.add(att_k_norm[0], PROJ_UNIT // ATT_HEAD_DIM)
    diff_qg = pk.add(diff_q_norm[0], PROJ_UNIT // DIFF_HEAD_DIM, DIFF_HEAD_DIM ** -0.5 * LOG2E)
    diff_kg = pk.add(diff_k_norm[0], PROJ_UNIT // DIFF_HEAD_DIM)
    diff_small = [pk.add(v[0]) for v in (diff_lq1, diff_lk1, diff_lq2, diff_lk2, diff_subln)]
    swa_qg = pk.add(swa_q_norm[0], PROJ_UNIT // SWA_HEAD_DIM, SWA_HEAD_DIM ** -0.5 * LOG2E)
    swa_kg = pk.add(swa_k_norm[0], PROJ_UNIT // SWA_HEAD_DIM)
    mla_qs = (MLA_NOPE + MLA_ROPE) ** -0.5 * LOG2E
    mla_qa = pk.add(mla_q_a_norm[0])
    mla_kva = pk.add(mla_kv_a_norm[0])
    mla_qg = pk.add(mla_q_norm[0][:MLA_NOPE], 1, mla_qs)
    mla_qgp = pk.add(mla_q_norm[0][MLA_NOPE:], LANES // MLA_ROPE, mla_qs)
    mla_kg = pk.add(mla_k_norm[0][:MLA_NOPE], PROJ_UNIT // MLA_NOPE)
    mla_kgp = pk.add(mla_k_norm[0][MLA_NOPE:], LANES // MLA_ROPE)
    pk.build()

    outs = {}
    x = (xp, xs)
    for layer in range(DEPTH):
        gain_ffn = g_ffn[layer]
        if layer == 0:
            q, k, v, outs["att_k"], outs["att_v"] = _proj_att(
                xp, xs, mods_all, g_mix[layer], att_w_qkv[0], att_qg, att_kg, tab_att)
            attn_p, attn_s = _att_attend(q, k, v, cache_att_k, cache_att_v)
            w_o = att_w_o[0]
        elif layer == 1:
            q, k, v, outs["diff_k"], outs["diff_v"] = _proj_diff(h, diff_w_qkv[0], diff_qg, diff_kg, tab_64)
            lam_init = 0.8 - 0.6 * math.exp(-0.3 * layer)
            ck = cache_diff_k[:, 0].transpose(0, 1, 3, 2, 4).reshape(
                DEC_BATCH, DIFF_HEADS, PAST_LEN, LANES)
            attn_p, attn_s = _diff_attend(q, k, v, ck, cache_diff_v, *diff_small, lam_init)
            w_o = diff_w_o[0]
        elif layer == 2:
            q, kd, vd, outs["swa_k"], outs["swa_v"] = _proj_swa(h, swa_w_qkv[0], swa_qg, swa_kg, tab_64)
            ckd = jnp.concatenate([cache_swa_k[:, 0]] * 2, axis=-1).astype(BF16)
            cvd = jnp.concatenate([cache_swa_v[:, 0]] * 2, axis=-1).astype(BF16)
            attn_p, attn_s = _swa_attend(q, kd, vd, ckd, cvd, swa_sink[0].astype(F32))
            w_o = swa_w_o[0]
        else:
            qn, qp, outs["mla_ckv"], outs["mla_kpe"], kn, kp, vv = _proj_mla(
                h, mla_w_in[0], mla_qa, mla_kva, mla_w_uq[0], mla_qg, mla_qgp,
                mla_w_ukv[0], mla_kg, mla_kgp, tab_64)
            c_ckv = cache_mla_ckv[:, 0].reshape(DEC_BATCH * PAST_LEN, MLA_KV_RANK)
            c_kpe = cache_mla_kpe[:, 0].reshape(DEC_BATCH * PAST_LEN, MLA_ROPE)
            c_kpe = jnp.concatenate([c_kpe, c_kpe], axis=-1)
            knc, kpc, vc = _mla_ctx(c_ckv, c_kpe, mla_w_ukv[0], mla_kg, mla_kgp)
            attn_p, attn_s = _mla_attend(qn, qp, kn, kp, vv, knc, kpc, vc)
            w_o = mla_w_o[0]
        if layer + 1 < DEPTH:
            x, h = _omlp(attn_p, attn_s, w_o, x, mods_all, gain_ffn, mlp_w1, mlp_w2, layer,
                         g_mix[layer + 1])
        else:
            xp, xs = _omlp(attn_p, attn_s, w_o, x, mods_all, gain_ffn, mlp_w1, mlp_w2, layer, None)

    y_prompt = xp.reshape(BATCH, SEQ, D_MODEL)
    y_sample = xs.reshape(DEC_BATCH, DEC_SEQ, D_MODEL)
    for name in ("diff_k", "swa_k", "swa_v", "mla_kpe"):
        outs[name] = jnp.swapaxes(outs[name], -1, -2)
    return (y_prompt, y_sample, outs["att_k"], outs["att_v"], outs["diff_k"], outs["diff_v"],
            outs["swa_k"], outs["swa_v"], outs["mla_ckv"], outs["mla_kpe"])
```

```python
import functools
import math

import numpy as np
import jax
import jax.numpy as jnp
from jax import lax
from jax.experimental import pallas as pl
from jax.experimental.pallas import tpu as pltpu

D_MODEL = 1024
BATCH = 16
SEQ = 256
DEPTH = 4
DEC_BATCH = 2
DEC_SEQ = 1024
PAST_LEN = 256
GRID_W = 64
ROPE_THETA = 10000.0
EPS = 1e-6
D_FF = 4 * D_MODEL
MOD_CHUNKS = 6
LOG2E = 1.4426950408889634

ATT_HEADS, ATT_KV_HEADS, ATT_HEAD_DIM = 8, 2, 128
DIFF_HEADS, DIFF_HEAD_DIM = 8, 64
SWA_HEADS, SWA_KV_HEADS, SWA_HEAD_DIM, WINDOW = 16, 4, 64, 128
MLA_HEADS, MLA_NOPE, MLA_ROPE, MLA_VDIM = 8, 128, 64, 128
MLA_Q_RANK, MLA_KV_RANK = 512, 256

LANES = 128
HALF = LANES // 2
TM = 256
N_PROMPT_TOK = BATCH * SEQ
N_LAT_TOK = DEC_BATCH * DEC_SEQ
N_TOK = N_PROMPT_TOK + N_LAT_TOK
N_PROMPT_TILES = N_PROMPT_TOK // TM
LAT_TQ = 512
LAT_BLOCK0 = N_PROMPT_TOK // DEC_SEQ
COND_ROWS = 8
PROJ_TM = 512
PROJ_BATCHES = PROJ_TM // SEQ
N_PROJ_TILES = N_TOK // PROJ_TM
N_PROJ_PROMPT = N_PROMPT_TOK // PROJ_TM
PROJ_UNIT = 2 * LANES
MLP_TM = 512
MLP_FF_CHUNK = 512
MLP_LOAD_COLS = 256
N_LOAD_STEPS = D_FF // MLP_LOAD_COLS
N_MLP_PROMPT_TILES = N_PROMPT_TOK // MLP_TM
SWA_QB = 128
ATT_UNIT_HEADS = 4
PROMPT_SEQS = 8
VMEM_LIMIT = 56 * 1024 * 1024

F32 = jnp.float32
BF16 = jnp.bfloat16


def _cparams(n_axes):
    return pltpu.CompilerParams(dimension_semantics=("arbitrary",) * n_axes,
                                vmem_limit_bytes=VMEM_LIMIT)


def _dot(a, b):
    return jnp.dot(a, b, preferred_element_type=F32)


def _dot_nt(a, b):
    return lax.dot_general(a, b, (((1,), (1,)), ((), ())), preferred_element_type=F32)


def _dot_tn(a, b):
    return lax.dot_general(a, b, (((0,), (0,)), ((), ())), preferred_element_type=F32)


def _const_spec(shape):
    nd = len(shape)
    return pl.BlockSpec(shape, lambda *_: (0,) * nd, pipeline_mode=pl.Buffered(1))


class _ParamPack:
    def __init__(self):
        self._rows, self.array = [], None

    def add(self, v, repeat=1, scale=1.0):
        n = v.shape[0] * repeat
        row = _ParamRow(self, -(-n // LANES) * LANES)
        self._rows.append((row, [v] * repeat, scale, n))
        return row

    def build(self):
        pieces, scales, offset = [], [], 0
        for row, vs, scale, n in sorted(self._rows, key=lambda r: -r[0].width):
            row.offset = offset
            pieces += vs + ([jnp.zeros((row.width - n,), F32)] if row.width > n else [])
            scales.append(np.full((row.width,), scale, np.float32))
            offset += row.width
        flat = jnp.concatenate([p.astype(F32) for p in pieces]) * jnp.asarray(np.concatenate(scales))
        self.array = flat.reshape(1, offset)


class _ParamRow:
    def __init__(self, pack, width):
        self.pack, self.width, self.offset = pack, width, None

    @property
    def array(self):
        return self.pack.array

    @property
    def spec(self):
        block = self.offset // self.width
        return pl.BlockSpec((1, self.width), lambda *_: (0, block), pipeline_mode=pl.Buffered(1))


def _chunk(ref, c, width=LANES):
    return ref[:, c * width:(c + 1) * width]


def _put(ref, c, val):
    ref[:, c * LANES:(c + 1) * LANES] = val.astype(ref.dtype)


def _tile_group(i, rows):
    n_prompt = N_PROMPT_TOK // rows
    return jnp.where(i < n_prompt, 0, 1 + (i - n_prompt) // (DEC_SEQ // rows))


def _rope_tile(i):
    return jnp.maximum(i - N_PROJ_PROMPT, 0) % (DEC_SEQ // PROJ_TM)


def _norm_mod(x, gain, shift, scale):
    ms = jnp.mean(x * x, axis=-1, keepdims=True)
    return x * lax.rsqrt(ms + EPS) * (gain * (1.0 + scale)) + shift


def _lane_lo(shape):
    return lax.broadcasted_iota(jnp.int32, shape, len(shape) - 1) < HALF


def _rope(y, cos, sin_prev, sin_next, quarter):
    return (y * cos + pltpu.roll(y, quarter, 1) * sin_prev
            + pltpu.roll(y, LANES - quarter, 1) * sin_next)


def _rope_tables(rot_dim):
    half = rot_dim // 2
    quarter = rot_dim // 4
    inv = np.float32(ROPE_THETA) ** (-np.arange(0, half, 2, dtype=np.float32) / np.float32(half))
    pos = np.arange(DEC_SEQ)
    row = (pos // GRID_W).astype(np.float32)
    col = (pos % GRID_W).astype(np.float32)
    lane = np.arange(LANES)
    dd = lane % rot_dim
    q = dd // quarter
    f = dd % quarter
    ang = np.where((q < 2)[None, :], row[:, None], col[:, None]) * inv[f][None, :]
    ang = ang.astype(np.float32)
    cos = np.cos(ang).astype(np.float32)
    sin = np.sin(ang).astype(np.float32)
    odd = (q % 2 == 1)[None, :]
    sin_prev = np.where(odd, sin, 0.0).astype(np.float32)
    sin_next = np.where(odd, 0.0, -sin).astype(np.float32)
    return jnp.asarray(cos), jnp.asarray(sin_prev), jnp.asarray(sin_next)


def _lane_sum_matrix(rows, cols, value=1.0):
    lane = np.arange(LANES)
    m = np.where(rows(lane)[:, None] & cols(lane)[None, :], value, 0.0).astype(np.float32)
    return jnp.asarray(m, dtype=BF16)


def _group_mean_matrix(group):
    lane = np.arange(PROJ_UNIT)
    m = np.where((lane[:, None] // group) == (lane[None, :] // group), 1.0 / group, 0.0)
    return jnp.asarray(m.astype(np.float32), dtype=BF16)


def _group_sum_matrix():
    lane = np.arange(PROJ_UNIT)
    m = np.where((lane[:, None] // LANES) == (lane[None, :] // LANES), 1.0, 0.0)
    return jnp.asarray(m.astype(np.float32), dtype=BF16)


def _sq_bf16(y):
    return (y * y).astype(BF16)


def _head_norm(y, m_ref, gain):
    return y * lax.rsqrt(_dot(_sq_bf16(y), m_ref[...]) + EPS) * gain


def _halves(y):
    return [y[:, t * LANES:(t + 1) * LANES] for t in range(y.shape[1] // LANES)]


def _matmul_units(h, w_ref, n_units, width, emit):
    def unit(u):
        return _dot(h, w_ref[:, u * width:(u + 1) * width])

    nxt = unit(0)
    for u in range(n_units):
        cur = nxt
        if u + 1 < n_units:
            nxt = unit(u + 1)
        emit(u, cur)


def _cast_once(i, w_ref, w_s):
    @pl.when(i == 0)
    def _():
        w_s[...] = w_ref[...].astype(BF16)


def _by_tile_kind(i, body):
    pl.when(i < N_PROJ_PROMPT)(functools.partial(body, False))
    pl.when(i >= N_PROJ_PROMPT)(functools.partial(body, True))


def _rope_args(lat, cos_ref, sp_ref, sn_ref, rot_dim):
    return (cos_ref[...], sp_ref[...], sn_ref[...], rot_dim // 4) if lat else None


def _maybe_rope(y, rope):
    return y if rope is None else _rope(y, *rope)


def _cache_rows(ref, index, val):
    for b in range(PROJ_BATCHES):
        ref[(b, 0) + tuple(index)] = val[b * SEQ:(b + 1) * SEQ]


def _cache_rows_t(ref, indices, val):
    for b in range(PROJ_BATCHES):
        t = val[b * SEQ:(b + 1) * SEQ].T
        for j, index in enumerate(indices):
            ref[(b, 0) + tuple(index)] = t[j * HALF:(j + 1) * HALF]


def _softmax2_parts(s_list, extra=None):
    m = jnp.max(s_list[0], axis=0, keepdims=True)
    for s in s_list[1:]:
        m = jnp.maximum(m, jnp.max(s, axis=0, keepdims=True))
    if extra is not None:
        m = jnp.maximum(m, extra)
    ps = [jnp.exp2(s - m) for s in s_list]
    mass = ps[0].sum(axis=0, keepdims=True)
    for p in ps[1:]:
        mass = mass + p.sum(axis=0, keepdims=True)
    if extra is not None:
        mass = mass + jnp.exp2(extra - m)
    return [p.astype(BF16) for p in ps], 1.0 / mass


def _head_pipeline(n, scores, finish):
    nxt = scores(0)
    for h in range(n):
        cur = nxt
        if h + 1 < n:
            nxt = scores(h + 1)
        finish(h, cur)


def _seq_pipeline(refs, seqs, n, make):
    fns = []
    for b in range(seqs):
        views = [r.at[b * (r.shape[0] // seqs):(b + 1) * (r.shape[0] // seqs)] for r in refs]
        fns.append(make(views))
    _head_pipeline(seqs * n, lambda i: fns[i // n][0](i % n), lambda i, s: fns[i // n][1](i % n, s))


def _pv(ps, values):
    o = None
    for p, v in zip(ps, values):
        t = _dot_tn(v, p)
        o = t if o is None else o + t
    return o


def _split_halves(q):
    lo = _lane_lo(q.shape)
    zero = jnp.zeros_like(q)
    return jnp.where(lo, q, zero), jnp.where(lo, zero, q)


def _mod_kernel(cond_ref, w_ref, b_ref, o_ref):
    c = cond_ref[...]
    s = (c * jax.nn.sigmoid(c)).astype(BF16)
    o_ref[0] = _dot(s, w_ref[0].astype(BF16)) + b_ref[0]


def _modulation(cond, ada_w, ada_b):
    tn = 3072
    n = MOD_CHUNKS * D_MODEL
    return pl.pallas_call(
        _mod_kernel,
        grid=(DEPTH, n // tn),
        in_specs=[
            pl.BlockSpec((COND_ROWS, D_MODEL), lambda l, j: (0, 0)),
            pl.BlockSpec((1, D_MODEL, tn), lambda l, j: (l, 0, j)),
            pl.BlockSpec((1, 1, tn), lambda l, j: (l, 0, j)),
        ],
        out_specs=pl.BlockSpec((1, COND_ROWS, tn), lambda l, j: (l, 0, j)),
        out_shape=jax.ShapeDtypeStruct((DEPTH, COND_ROWS, n), F32),
        compiler_params=_cparams(2),
        name="modulation",
    )(cond, ada_w, ada_b.reshape(DEPTH, 1, n))


def _mod_spec(layer, chunk):
    return pl.BlockSpec((None, COND_ROWS, D_MODEL), lambda i: (layer, 0, chunk))


def _mod_row(ref, i):
    return ref[pl.ds(_tile_group(i, PROJ_TM), 1), :]


_ROPE_SPEC = pl.BlockSpec((PROJ_TM, LANES), lambda i: (_rope_tile(i), 0))
_LANE_MAT_SPEC = _const_spec((LANES, LANES))
_UNIT_MAT_SPEC = _const_spec((PROJ_UNIT, PROJ_UNIT))


def _tok_spec(width):
    return pl.BlockSpec((PROJ_TM, width), lambda i: (i, 0))


_XP_SPEC = pl.BlockSpec((PROJ_TM, D_MODEL), lambda i: (jnp.minimum(i, N_PROJ_PROMPT - 1), 0))
_XS_SPEC = pl.BlockSpec((PROJ_TM, D_MODEL), lambda i: (jnp.maximum(i - N_PROJ_PROMPT, 0), 0))


def _cache_spec(*dims):
    nd = len(dims)
    return pl.BlockSpec((PROJ_BATCHES, 1) + dims,
                        lambda i: (jnp.minimum(i, N_PROJ_PROMPT - 1), 0) + (0,) * nd)


def _cache_shape(*dims):
    return jax.ShapeDtypeStruct((BATCH, 1) + dims, F32)


def _proj_att_kernel(xp_ref, xs_ref, gain_ref, sh_ref, sc_ref, w_ref, qg_ref, kg_ref, m_ref,
                     cos_ref, sp_ref, sn_ref, q_ref, k_ref, v_ref, ck_ref, cv_ref, w_s):
    i = pl.program_id(0)
    _cast_once(i, w_ref, w_s)
    x = jnp.where(i < N_PROJ_PROMPT, xp_ref[...], xs_ref[...])
    h = _norm_mod(x, gain_ref[...], _mod_row(sh_ref, i), _mod_row(sc_ref, i)).astype(BF16)
    per = PROJ_UNIT // LANES
    nq, nk = ATT_HEADS // per, ATT_KV_HEADS // per

    def body(lat):
        rope = _rope_args(lat, cos_ref, sp_ref, sn_ref, ATT_HEAD_DIM)

        def emit(u, y):
            if u < nq + nk:
                y = _head_norm(y, m_ref, qg_ref[...] if u < nq else kg_ref[...])
            for t, yc in enumerate(_halves(y)):
                if u < nq:
                    _put(q_ref, u * per + t, _maybe_rope(yc, rope))
                elif u < nq + nk:
                    kn = _maybe_rope(yc, rope)
                    _put(k_ref, (u - nq) * per + t, kn)
                    if not lat:
                        _cache_rows(ck_ref, [(u - nq) * per + t], kn)
                else:
                    _put(v_ref, (u - nq - nk) * per + t, yc)
                    if not lat:
                        _cache_rows(cv_ref, [(u - nq - nk) * per + t], yc)

        _matmul_units(h, w_s, nq + 2 * nk, PROJ_UNIT, emit)

    _by_tile_kind(i, body)


def _proj_att(xp, xs, mods, gain, w, qg, kg, tables):
    nq, nk = ATT_HEADS * ATT_HEAD_DIM, ATT_KV_HEADS * ATT_HEAD_DIM
    return pl.pallas_call(
        _proj_att_kernel,
        grid=(N_PROJ_TILES,),
        in_specs=[_XP_SPEC, _XS_SPEC, gain.spec, _mod_spec(0, 0), _mod_spec(0, 1),
                  _const_spec(w.shape), qg.spec, kg.spec,
                  _UNIT_MAT_SPEC, _ROPE_SPEC, _ROPE_SPEC, _ROPE_SPEC],
        out_specs=[_tok_spec(nq), _tok_spec(nk), _tok_spec(nk),
                   _cache_spec(ATT_KV_HEADS, SEQ, ATT_HEAD_DIM), _cache_spec(ATT_KV_HEADS, SEQ, ATT_HEAD_DIM)],
        out_shape=[jax.ShapeDtypeStruct((N_TOK, nq), BF16),
                   jax.ShapeDtypeStruct((N_TOK, nk), BF16),
                   jax.ShapeDtypeStruct((N_TOK, nk), BF16),
                   _cache_shape(ATT_KV_HEADS, SEQ, ATT_HEAD_DIM), _cache_shape(ATT_KV_HEADS, SEQ, ATT_HEAD_DIM)],
        scratch_shapes=[pltpu.VMEM(w.shape, BF16)],
        compiler_params=_cparams(1),
        name="proj_att",
    )(xp, xs, gain.array, mods, mods, w, qg.array, kg.array, _group_mean_matrix(ATT_HEAD_DIM), *tables)


def _proj_diff_kernel(h_ref, w_ref, qg_ref, kg_ref, m_ref,
                      cos_ref, sp_ref, sn_ref, q_ref, k_ref, v_ref, ck_ref, cv_ref, w_s):
    i = pl.program_id(0)
    _cast_once(i, w_ref, w_s)
    h = h_ref[...]
    per = PROJ_UNIT // LANES
    nu = DIFF_HEADS // per

    def body(lat):
        rope = _rope_args(lat, cos_ref, sp_ref, sn_ref, DIFF_HEAD_DIM)

        def emit(u, y):
            if u < 2 * nu:
                y = _head_norm(y, m_ref, qg_ref[...] if u < nu else kg_ref[...])
            for t, yc in enumerate(_halves(y)):
                hd = (u % nu) * per + t
                if u < nu:
                    _put(q_ref, hd, _maybe_rope(yc, rope))
                elif u < 2 * nu:
                    kn = _maybe_rope(yc, rope)
                    _put(k_ref, hd, kn)
                    if not lat:
                        _cache_rows_t(ck_ref, [[hd, 0], [hd, 1]], kn)
                else:
                    _put(v_ref, hd, yc)
                    if not lat:
                        _cache_rows(cv_ref, [hd], yc)

        _matmul_units(h, w_s, 3 * nu, PROJ_UNIT, emit)

    _by_tile_kind(i, body)


def _proj_diff(h, w, qg, kg, tables):
    n = DIFF_HEADS * 2 * DIFF_HEAD_DIM
    return pl.pallas_call(
        _proj_diff_kernel,
        grid=(N_PROJ_TILES,),
        in_specs=[_tok_spec(D_MODEL), _const_spec(w.shape), qg.spec, kg.spec,
                  _UNIT_MAT_SPEC, _ROPE_SPEC, _ROPE_SPEC, _ROPE_SPEC],
        out_specs=[_tok_spec(n), _tok_spec(n), _tok_spec(n),
                   _cache_spec(DIFF_HEADS, 2, DIFF_HEAD_DIM, SEQ), _cache_spec(DIFF_HEADS, SEQ, 2 * DIFF_HEAD_DIM)],
        out_shape=[jax.ShapeDtypeStruct((N_TOK, n), BF16)] * 3
                  + [_cache_shape(DIFF_HEADS, 2, DIFF_HEAD_DIM, SEQ),
                     _cache_shape(DIFF_HEADS, SEQ, 2 * DIFF_HEAD_DIM)],
        scratch_shapes=[pltpu.VMEM(w.shape, BF16)],
        compiler_params=_cparams(1),
        name="proj_diff",
    )(h, w, qg.array, kg.array, _group_mean_matrix(DIFF_HEAD_DIM), *tables)


def _dup_halves(yc):
    lo = _lane_lo(yc.shape)
    sw = pltpu.roll(yc, HALF, 1)
    return jnp.where(lo, yc, sw), jnp.where(lo, sw, yc)


def _proj_swa_kernel(h_ref, w_ref, qg_ref, kg_ref, m_ref,
                     cos_ref, sp_ref, sn_ref, q_ref, kd_ref, vd_ref, ck_ref, cv_ref, w_s):
    i = pl.program_id(0)
    _cast_once(i, w_ref, w_s)
    h = h_ref[...]
    per = PROJ_UNIT // LANES
    nq = SWA_HEADS * SWA_HEAD_DIM // PROJ_UNIT
    nk = SWA_KV_HEADS * SWA_HEAD_DIM // PROJ_UNIT

    def body(lat):
        rope = _rope_args(lat, cos_ref, sp_ref, sn_ref, SWA_HEAD_DIM)

        def emit(u, y):
            if u < nq + nk:
                y = _head_norm(y, m_ref, qg_ref[...] if u < nq else kg_ref[...])
            for t, yc in enumerate(_halves(y)):
                if u < nq:
                    _put(q_ref, u * per + t, _maybe_rope(yc, rope))
                    continue
                if u < nq + nk:
                    j, c_ref, d_ref = (u - nq) * per + t, ck_ref, kd_ref
                    yc = _maybe_rope(yc, rope)
                else:
                    j, c_ref, d_ref = (u - nq - nk) * per + t, cv_ref, vd_ref
                for a, dup in enumerate(_dup_halves(yc)):
                    _put(d_ref, 2 * j + a, dup)
                if not lat:
                    _cache_rows_t(c_ref, [[2 * j], [2 * j + 1]], yc)

        _matmul_units(h, w_s, nq + 2 * nk, PROJ_UNIT, emit)

    _by_tile_kind(i, body)


def _proj_swa(h, w, qg, kg, tables):
    nq, nk = SWA_HEADS * SWA_HEAD_DIM, SWA_KV_HEADS * SWA_HEAD_DIM
    return pl.pallas_call(
        _proj_swa_kernel,
        grid=(N_PROJ_TILES,),
        in_specs=[_tok_spec(D_MODEL), _const_spec(w.shape), qg.spec, kg.spec,
                  _UNIT_MAT_SPEC, _ROPE_SPEC, _ROPE_SPEC, _ROPE_SPEC],
        out_specs=[_tok_spec(nq), _tok_spec(2 * nk), _tok_spec(2 * nk),
                   _cache_spec(SWA_KV_HEADS, SWA_HEAD_DIM, SEQ), _cache_spec(SWA_KV_HEADS, SWA_HEAD_DIM, SEQ)],
        out_shape=[jax.ShapeDtypeStruct((N_TOK, nq), BF16),
                   jax.ShapeDtypeStruct((N_TOK, 2 * nk), BF16),
                   jax.ShapeDtypeStruct((N_TOK, 2 * nk), BF16),
                   _cache_shape(SWA_KV_HEADS, SWA_HEAD_DIM, SEQ), _cache_shape(SWA_KV_HEADS, SWA_HEAD_DIM, SEQ)],
        scratch_shapes=[pltpu.VMEM(w.shape, BF16)],
        compiler_params=_cparams(1),
        name="proj_swa",
    )(h, w, qg.array, kg.array, _group_mean_matrix(SWA_HEAD_DIM), *tables)


def _mla_lane_matrices():
    everything = lambda lane: lane >= 0
    lo = _lane_sum_matrix(lambda lane: lane < HALF, everything)
    hi = _lane_sum_matrix(lambda lane: lane >= HALF, everything)
    return jnp.concatenate([lo, hi], axis=1), lo


def _mla_keys(ckv, kpe, w_s, kg_ref, kgp_ref, sum_ref, lo_ref, rope, kn_ref, kp_ref, v_ref):
    pe_ss = _dot(_sq_bf16(kpe), lo_ref[...])
    pe_ss = jnp.concatenate([pe_ss, pe_ss], axis=1)
    lo = _lane_lo(kpe.shape)
    inv_d = 1.0 / (MLA_NOPE + MLA_ROPE)

    def emit(j, y):
        kn = jnp.concatenate([y[:, :LANES], y[:, 2 * LANES:3 * LANES]], axis=1)
        r = lax.rsqrt((_dot(_sq_bf16(kn), sum_ref[...]) + pe_ss) * inv_d + EPS)
        kn = kn * r * kg_ref[...]
        for a in range(2):
            _put(kn_ref, 2 * j + a, kn[:, a * LANES:(a + 1) * LANES])
            _put(v_ref, 2 * j + a, y[:, (2 * a + 1) * LANES:(2 * a + 2) * LANES])
        pe = kpe * jnp.where(lo, r[:, :LANES], r[:, LANES:]) * kgp_ref[...]
        _put(kp_ref, j, _maybe_rope(pe, rope))

    _matmul_units(ckv, w_s, MLA_HEADS // 2, 4 * LANES, emit)


def _proj_mla_kernel(h_ref, w_in_ref, qa_ref, kva_ref, w_uq_ref, qg_ref, qgp_ref, lohi_ref, lo_ref,
                     w_ukv_ref, kg_ref, kgp_ref, sum_ref, cos_ref, sp_ref, sn_ref,
                     qn_ref, qp_ref, c_ckv_ref, c_kpe_ref, kn_ref, kp_ref, v_ref,
                     w_in_s, w_uq_s, w_ukv_s):
    i = pl.program_id(0)
    n_in = MLA_Q_RANK + MLA_KV_RANK + MLA_ROPE

    @pl.when(i == 0)
    def _():
        w_in_s[...] = jnp.zeros_like(w_in_s)
        w_in_s[:, :n_in] = w_in_ref[...].astype(BF16)
        w_uq_s[...] = w_uq_ref[...].astype(BF16)
        w_ukv_s[...] = w_ukv_ref[...].astype(BF16)

    y = _dot(h_ref[...], w_in_s[...])
    c_q = y[:, :MLA_Q_RANK]
    c_kv = y[:, MLA_Q_RANK:MLA_Q_RANK + MLA_KV_RANK]
    kpe = y[:, MLA_Q_RANK + MLA_KV_RANK:]
    kpe = kpe + pltpu.roll(kpe, HALF, 1)
    ckv = c_kv * lax.rsqrt(jnp.mean(c_kv * c_kv, axis=-1, keepdims=True) + EPS) * kva_ref[...]
    cq = (c_q * lax.rsqrt(jnp.mean(c_q * c_q, axis=-1, keepdims=True) + EPS) * qa_ref[...]).astype(BF16)
    lo = _lane_lo((PROJ_TM, LANES))
    inv_d = 1.0 / (MLA_NOPE + MLA_ROPE)

    def body(lat):
        rope = _rope_args(lat, cos_ref, sp_ref, sn_ref, MLA_ROPE)
        if not lat:
            _cache_rows(c_ckv_ref, [], ckv)
            _cache_rows_t(c_kpe_ref, [[]], kpe)

        def emit(j, yq):
            y0, y1, y2 = _halves(yq)
            nopes = (y0, jnp.where(lo, pltpu.roll(y1, HALF, 1), pltpu.roll(y2, HALF, 1)))
            pe = jnp.where(lo, y1, y2)
            ss = (_dot(_sq_bf16(jnp.concatenate(nopes, axis=1)), sum_ref[...])
                  + _dot(_sq_bf16(pe), lohi_ref[...]))
            rs = _halves(lax.rsqrt(ss * inv_d + EPS))
            for a in range(2):
                _put(qn_ref, 2 * j + a, nopes[a] * rs[a] * qg_ref[...])
            _put(qp_ref, j, _maybe_rope(pe * jnp.where(lo, rs[0], rs[1]) * qgp_ref[...], rope))

        _matmul_units(cq, w_uq_s, MLA_HEADS // 2, 3 * LANES, emit)
        _mla_keys(ckv.astype(BF16), kpe, w_ukv_s, kg_ref, kgp_ref, sum_ref, lo_ref, rope,
                  kn_ref, kp_ref, v_ref)

    _by_tile_kind(i, body)


def _proj_mla(h, w_in, qa, kva, w_uq, qg, qgp, w_ukv, kg, kgp, tables):
    n_nope = MLA_HEADS * MLA_NOPE
    n_pe = MLA_HEADS * MLA_ROPE
    n_in = -(-w_in.shape[1] // LANES) * LANES
    return pl.pallas_call(
        _proj_mla_kernel,
        grid=(N_PROJ_TILES,),
        in_specs=[_tok_spec(D_MODEL),
                  _const_spec(w_in.shape), qa.spec, kva.spec,
                  _const_spec(w_uq.shape), qg.spec, qgp.spec,
                  _const_spec((LANES, PROJ_UNIT)), _LANE_MAT_SPEC,
                  _const_spec(w_ukv.shape), kg.spec, kgp.spec, _UNIT_MAT_SPEC,
                  _ROPE_SPEC, _ROPE_SPEC, _ROPE_SPEC],
        out_specs=[_tok_spec(n_nope), _tok_spec(n_pe),
                   _cache_spec(SEQ, MLA_KV_RANK), _cache_spec(MLA_ROPE, SEQ),
                   _tok_spec(n_nope), _tok_spec(n_pe), _tok_spec(n_nope)],
        out_shape=[jax.ShapeDtypeStruct((N_TOK, n_nope), BF16),
                   jax.ShapeDtypeStruct((N_TOK, n_pe), BF16),
                   _cache_shape(SEQ, MLA_KV_RANK), _cache_shape(MLA_ROPE, SEQ),
                   jax.ShapeDtypeStruct((N_TOK, n_nope), BF16),
                   jax.ShapeDtypeStruct((N_TOK, n_pe), BF16),
                   jax.ShapeDtypeStruct((N_TOK, n_nope), BF16)],
        scratch_shapes=[pltpu.VMEM((D_MODEL, n_in), BF16), pltpu.VMEM(w_uq.shape, BF16),
                        pltpu.VMEM(w_ukv.shape, BF16)],
        compiler_params=_cparams(1),
        name="proj_mla",
    )(h, w_in, qa.array, kva.array, w_uq, qg.array, qgp.array, *_mla_lane_matrices(),
      w_ukv, kg.array, kgp.array, _group_sum_matrix(), *tables)


def _mla_ctx_kernel(ckv_ref, kpe_ref, w_ref, kg_ref, kgp_ref, sum_ref, lo_ref, kn_ref, kp_ref, v_ref):
    _mla_keys(ckv_ref[...].astype(BF16), kpe_ref[...], w_ref[...].astype(BF16), kg_ref, kgp_ref,
              sum_ref, lo_ref, None, kn_ref, kp_ref, v_ref)


def _mla_ctx(ckv, kpe_dup, w_ukv, kg, kgp):
    n = ckv.shape[0]
    n_nope = MLA_HEADS * MLA_NOPE
    n_pe = MLA_HEADS * MLA_ROPE
    _, m_lo = _mla_lane_matrices()
    return pl.pallas_call(
        _mla_ctx_kernel,
        grid=(n // PROJ_TM,),
        in_specs=[_tok_spec(MLA_KV_RANK), _tok_spec(LANES), _const_spec(w_ukv.shape),
                  kg.spec, kgp.spec, _UNIT_MAT_SPEC, _LANE_MAT_SPEC],
        out_specs=[_tok_spec(n_nope), _tok_spec(n_pe), _tok_spec(n_nope)],
        out_shape=[jax.ShapeDtypeStruct((n, n_nope), BF16),
                   jax.ShapeDtypeStruct((n, n_pe), BF16),
                   jax.ShapeDtypeStruct((n, n_nope), BF16)],
        compiler_params=_cparams(1),
        name="mla_ctx",
    )(ckv, kpe_dup, w_ukv, kg.array, kgp.array, _group_sum_matrix(), m_lo)


def _prompt_spec(width):
    return pl.BlockSpec((PROMPT_SEQS * TM, width), lambda b: (b, 0))


def _latq_spec(rows, width):
    per = DEC_SEQ // rows
    return pl.BlockSpec((rows, width), lambda b, t: (N_PROMPT_TOK // rows + b * per + t, 0))


def _latkv_spec(width):
    return pl.BlockSpec((DEC_SEQ, width), lambda b, t: (LAT_BLOCK0 + b, 0))


def _lato_spec(rows):
    per = DEC_SEQ // rows
    return pl.BlockSpec((rows, D_MODEL), lambda b, t: (b * per + t, 0))


def _att_kernel(*refs, with_ctx, seqs):
    if with_ctx:
        q_ref, k_ref, v_ref, kc_ref, vc_ref, o_ref = refs
    else:
        q_ref, k_ref, v_ref, o_ref = refs
    tq = q_ref.shape[0] // seqs
    nu = ATT_UNIT_HEADS
    per_kv = ATT_HEADS // ATT_KV_HEADS // nu

    def make(views):
        q_v, k_v, v_v, o_v = views

        def scores(u):
            q = jnp.concatenate([_chunk(q_v, u * nu + g) for g in range(nu)], axis=0)
            s_list = [_dot_nt(_chunk(k_v, u // per_kv), q)]
            if with_ctx:
                s_list.append(_dot_nt(kc_ref[u // per_kv].astype(BF16), q))
            return s_list

        def finish(u, s_list):
            values = [_chunk(v_v, u // per_kv)]
            if with_ctx:
                values.append(vc_ref[u // per_kv].astype(BF16))
            ps, inv = _softmax2_parts(s_list)
            o = _pv(ps, values) * inv
            for g in range(nu):
                o_v[:, (u * nu + g) * LANES:(u * nu + g + 1) * LANES] = (
                    o[:, g * tq:(g + 1) * tq].T.astype(o_v.dtype))

        return scores, finish

    _seq_pipeline((q_ref, k_ref, v_ref, o_ref), seqs, ATT_HEADS // nu, make)


def _att_attend(q, k, v, cache_k, cache_v):
    nk = ATT_KV_HEADS * ATT_HEAD_DIM
    out_p = pl.pallas_call(
        functools.partial(_att_kernel, with_ctx=False, seqs=PROMPT_SEQS),
        grid=(N_PROMPT_TILES // PROMPT_SEQS,),
        in_specs=[_prompt_spec(D_MODEL), _prompt_spec(nk), _prompt_spec(nk)],
        out_specs=_prompt_spec(D_MODEL),
        out_shape=jax.ShapeDtypeStruct((N_PROMPT_TOK, D_MODEL), BF16),
        compiler_params=_cparams(1),
        name="att_prompt",
    )(q, k, v)
    ctx = pl.BlockSpec((None, None, ATT_KV_HEADS, PAST_LEN, LANES), lambda b, t: (b, 0, 0, 0, 0))
    out_s = pl.pallas_call(
        functools.partial(_att_kernel, with_ctx=True, seqs=1),
        grid=(DEC_BATCH, DEC_SEQ // LAT_TQ),
        in_specs=[_latq_spec(LAT_TQ, D_MODEL), _latkv_spec(nk), _latkv_spec(nk), ctx, ctx],
        out_specs=_lato_spec(LAT_TQ),
        out_shape=jax.ShapeDtypeStruct((N_LAT_TOK, D_MODEL), BF16),
        compiler_params=_cparams(2),
        name="att_latent",
    )(q, k, v, cache_k, cache_v)
    return out_p, out_s


def _diff_kernel(*refs, lam_init, with_ctx, seqs):
    if with_ctx:
        (q_ref, k_ref, v_ref, kc_ref, vc_ref, lq1_ref, lk1_ref, lq2_ref, lk2_ref, sub_ref, o_ref) = refs
    else:
        (q_ref, k_ref, v_ref, lq1_ref, lk1_ref, lq2_ref, lk2_ref, sub_ref, o_ref) = refs
    tq = q_ref.shape[0] // seqs
    lam = (jnp.exp(jnp.sum(lq1_ref[...] * lk1_ref[...], axis=-1, keepdims=True))
           - jnp.exp(jnp.sum(lq2_ref[...] * lk2_ref[...], axis=-1, keepdims=True)) + lam_init)
    diag = (lax.broadcasted_iota(jnp.int32, (LANES, LANES), 0)
            == lax.broadcasted_iota(jnp.int32, (LANES, LANES), 1))
    sub = jnp.sum(jnp.where(diag, sub_ref[...] * (1.0 - lam_init), 0.0), axis=1, keepdims=True)

    def make(views):
        q_v, k_v, v_v, o_v = views

        def scores(hd):
            q = jnp.concatenate(_split_halves(_chunk(q_v, hd)), axis=0)
            s_list = [_dot_nt(_chunk(k_v, hd), q)]
            if with_ctx:
                s_list.append(_dot_nt(kc_ref[hd].astype(BF16), q))
            return s_list

        def finish(hd, s_list):
            values = [_chunk(v_v, hd)]
            if with_ctx:
                values.append(vc_ref[hd].astype(BF16))
            ps, inv = _softmax2_parts(s_list)
            o = (_pv([p[:, :tq] for p in ps], values) * inv[:, :tq]
                 - _pv([p[:, tq:] for p in ps], values) * (lam * inv[:, tq:]))
            o = o * lax.rsqrt(jnp.mean(o * o, axis=0, keepdims=True) + EPS) * sub
            o_v[:, hd * LANES:(hd + 1) * LANES] = o.T.astype(o_v.dtype)

        return scores, finish

    _seq_pipeline((q_ref, k_ref, v_ref, o_ref), seqs, DIFF_HEADS, make)


def _diff_attend(q, k, v, cache_k_pair, cache_v, lq1, lk1, lq2, lk2, subln, lam_init):
    small_specs = [p.spec for p in (lq1, lk1, lq2, lk2, subln)]
    small = [p.array for p in (lq1, lk1, lq2, lk2, subln)]
    out_p = pl.pallas_call(
        functools.partial(_diff_kernel, lam_init=lam_init, with_ctx=False, seqs=PROMPT_SEQS),
        grid=(N_PROMPT_TILES // PROMPT_SEQS,),
        in_specs=[_prompt_spec(D_MODEL)] * 3 + small_specs,
        out_specs=_prompt_spec(D_MODEL),
        out_shape=jax.ShapeDtypeStruct((N_PROMPT_TOK, D_MODEL), BF16),
        compiler_params=_cparams(1),
        name="diff_prompt",
    )(q, k, v, *small)
    out_s = pl.pallas_call(
        functools.partial(_diff_kernel, lam_init=lam_init, with_ctx=True, seqs=1),
        grid=(DEC_BATCH, DEC_SEQ // LAT_TQ),
        in_specs=[_latq_spec(LAT_TQ, D_MODEL), _latkv_spec(D_MODEL), _latkv_spec(D_MODEL),
                  pl.BlockSpec((None, DIFF_HEADS, PAST_LEN, LANES), lambda b, t: (b, 0, 0, 0)),
                  pl.BlockSpec((None, None, DIFF_HEADS, PAST_LEN, LANES), lambda b, t: (b, 0, 0, 0, 0))]
                 + small_specs,
        out_specs=_lato_spec(LAT_TQ),
        out_shape=jax.ShapeDtypeStruct((N_LAT_TOK, D_MODEL), BF16),
        compiler_params=_cparams(2),
        name="diff_latent",
    )(q, k, v, cache_k_pair, cache_v, *small)
    return out_p, out_s


def _swa_pipeline(q_ref, o_ref, seq_refs, sink_ref, score_fns, value_fns, seqs=1):
    tq = q_ref.shape[0] // seqs
    per_kv = SWA_HEADS // SWA_KV_HEADS // 2
    first = lax.broadcasted_iota(jnp.int32, (LANES, tq), 0) < HALF

    def make(views):
        q_v, o_v = views[:2]
        kv_views = views[2:]

        def scores(c):
            q = jnp.concatenate(_split_halves(_chunk(q_v, c)), axis=0)
            return [fn(kv_views, c // per_kv, q) for fn in score_fns]

        def finish(c, s_list):
            sink = jnp.concatenate([jnp.full((1, tq), sink_ref[2 * c + a] * LOG2E, F32) for a in range(2)],
                                   axis=1)
            ps, inv = _softmax2_parts(s_list, extra=sink)
            o = _pv(ps, [fn(kv_views, c // per_kv) for fn in value_fns]) * inv
            oc = jnp.where(first, o[:, :tq], o[:, tq:])
            o_v[:, c * LANES:(c + 1) * LANES] = oc.T.astype(o_v.dtype)

        return scores, finish

    _seq_pipeline((q_ref, o_ref) + tuple(seq_refs), seqs, SWA_HEADS // 2, make)


def _swa_prompt_kernel(sink_ref, q_ref, k_ref, v_ref, o_ref):
    _swa_pipeline(q_ref, o_ref, (k_ref, v_ref), sink_ref,
                  [lambda kv_v, kv, q: _dot_nt(_chunk(kv_v[0], kv), q)],
                  [lambda kv_v, kv: _chunk(kv_v[1], kv)], seqs=PROMPT_SEQS)


def _swa_latent_kernel(sink_ref, q_ref, k_ref, v_ref, kc_ref, vc_ref, o_ref):
    n = pl.program_id(1)
    tq = q_ref.shape[0]
    span = SWA_QB + 2 * WINDOW
    start = pl.multiple_of(jnp.clip(n * SWA_QB - WINDOW, 0, DEC_SEQ - span), WINDOW)
    cols = lax.broadcasted_iota(jnp.int32, (span, 2 * tq), 1)
    qpos = n * SWA_QB + jnp.bitwise_and(cols, tq - 1)
    kpos = start + lax.broadcasted_iota(jnp.int32, (span, 2 * tq), 0)
    valid = jnp.abs(qpos - kpos) <= WINDOW

    def local(ref, kv):
        return ref[pl.ds(start, span), kv * LANES:(kv + 1) * LANES]

    _swa_pipeline(q_ref, o_ref, (), sink_ref,
                  [lambda _, kv, q: jnp.where(valid, _dot_nt(local(k_ref, kv), q), -1e30),
                   lambda _, kv, q: _dot_nt(kc_ref[kv], q)],
                  [lambda _, kv: local(v_ref, kv), lambda _, kv: vc_ref[kv]])


def _swa_attend(q, kd, vd, cache_kd, cache_vd, sink):
    nkd = 2 * SWA_KV_HEADS * SWA_HEAD_DIM
    smem = pl.BlockSpec(memory_space=pltpu.SMEM)
    out_p = pl.pallas_call(
        _swa_prompt_kernel,
        grid=(N_PROMPT_TILES // PROMPT_SEQS,),
        in_specs=[smem, _prompt_spec(D_MODEL), _prompt_spec(nkd), _prompt_spec(nkd)],
        out_specs=_prompt_spec(D_MODEL),
        out_shape=jax.ShapeDtypeStruct((N_PROMPT_TOK, D_MODEL), BF16),
        compiler_params=_cparams(1),
        name="swa_prompt",
    )(sink, q, kd, vd)
    ctx = pl.BlockSpec((None, SWA_KV_HEADS, PAST_LEN, LANES), lambda b, n: (b, 0, 0, 0))
    out_s = pl.pallas_call(
        _swa_latent_kernel,
        grid=(DEC_BATCH, DEC_SEQ // SWA_QB),
        in_specs=[smem, _latq_spec(SWA_QB, D_MODEL), _latkv_spec(nkd), _latkv_spec(nkd), ctx, ctx],
        out_specs=_lato_spec(SWA_QB),
        out_shape=jax.ShapeDtypeStruct((N_LAT_TOK, D_MODEL), BF16),
        compiler_params=_cparams(2),
        name="swa_latent",
    )(sink, q, kd, vd, cache_kd, cache_vd)
    return out_p, out_s


def _mla_kernel(*refs, with_ctx, seqs):
    if with_ctx:
        (qn_ref, qp_ref, kn_ref, kp_ref, v_ref, knc_ref, kpc_ref, vc_ref, o_ref) = refs
    else:
        (qn_ref, qp_ref, kn_ref, kp_ref, v_ref, o_ref) = refs

    def make(views):
        qn_v, qp_v, kn_v, kp_v, v_v, o_v = views

        def scores(hd):
            j, a = hd // 2, hd % 2
            q = jnp.concatenate([_chunk(qn_v, hd), _split_halves(_chunk(qp_v, j))[a]], axis=1)
            s_list = [_dot_nt(jnp.concatenate([_chunk(kn_v, hd), _chunk(kp_v, j)], axis=1), q)]
            if with_ctx:
                s_list.append(_dot_nt(jnp.concatenate([_chunk(knc_ref, hd), _chunk(kpc_ref, j)], axis=1), q))
            return s_list

        def finish(hd, s_list):
            values = [_chunk(v_v, hd)]
            if with_ctx:
                values.append(_chunk(vc_ref, hd))
            ps, inv = _softmax2_parts(s_list)
            o_v[:, hd * LANES:(hd + 1) * LANES] = (_pv(ps, values) * inv).T.astype(o_v.dtype)

        return scores, finish

    _seq_pipeline((qn_ref, qp_ref, kn_ref, kp_ref, v_ref, o_ref), seqs, MLA_HEADS, make)


def _mla_attend(qn, qp, kn, kp, v, knc, kpc, vc):
    n_pe = MLA_HEADS * MLA_ROPE
    out_p = pl.pallas_call(
        functools.partial(_mla_kernel, with_ctx=False, seqs=PROMPT_SEQS),
        grid=(N_PROMPT_TILES // PROMPT_SEQS,),
        in_specs=[_prompt_spec(D_MODEL), _prompt_spec(n_pe), _prompt_spec(D_MODEL), _prompt_spec(n_pe),
                  _prompt_spec(D_MODEL)],
        out_specs=_prompt_spec(D_MODEL),
        out_shape=jax.ShapeDtypeStruct((N_PROMPT_TOK, D_MODEL), BF16),
        compiler_params=_cparams(1),
        name="mla_prompt",
    )(qn, qp, kn, kp, v)

    def ctx(width):
        return pl.BlockSpec((PAST_LEN, width), lambda b, t: (b, 0))

    out_s = pl.pallas_call(
        functools.partial(_mla_kernel, with_ctx=True, seqs=1),
        grid=(DEC_BATCH, DEC_SEQ // LAT_TQ),
        in_specs=[_latq_spec(LAT_TQ, D_MODEL), _latq_spec(LAT_TQ, n_pe),
                  _latkv_spec(D_MODEL), _latkv_spec(n_pe), _latkv_spec(D_MODEL),
                  ctx(D_MODEL), ctx(n_pe), ctx(D_MODEL)],
        out_specs=_lato_spec(LAT_TQ),
        out_shape=jax.ShapeDtypeStruct((N_LAT_TOK, D_MODEL), BF16),
        compiler_params=_cparams(2),
        name="mla_latent",
    )(qn, qp, kn, kp, v, knc, kpc, vc)
    return out_p, out_s


def _omlp_kernel(*refs, first, last):
    refs = list(refs)
    ap_ref, as_ref, wo_ref = refs[:3]
    x_refs = refs[3:5] if first else refs[3:4]
    refs = refs[3 + len(x_refs):]
    g1_ref, gain_ref, sh_ref, sc_ref, g2_ref, w1c_ref, w2c_ref = refs[:7]
    refs = refs[7:]
    if last:
        op_ref, os_ref, wo_s, w1_s, w2_s = refs
    else:
        ngain_ref, nsh_ref, nsc_ref, o_ref, hn_ref, wo_s, w1_s, w2_s = refs
    s = pl.program_id(0)
    per = MLP_FF_CHUNK // MLP_LOAD_COLS
    n_chunks = D_FF // MLP_FF_CHUNK
    half = MLP_TM // 2

    @pl.when(s == 0)
    def _():
        wo_s[...] = wo_ref[...].astype(BF16)

    for part in range(per):
        @pl.when((s < N_LOAD_STEPS) & (s % per == part))
        def _(part=part):
            w1_s[s // per, :, part * MLP_LOAD_COLS:(part + 1) * MLP_LOAD_COLS] = w1c_ref[...].astype(BF16)

    @pl.when(s < N_LOAD_STEPS)
    def _():
        w2_s[s // per, pl.ds(pl.multiple_of((s % per) * MLP_LOAD_COLS, MLP_LOAD_COLS), MLP_LOAD_COLS), :] = (
            w2c_ref[...].astype(BF16))

    @pl.when(s >= N_LOAD_STEPS)
    def _():
        t = s - N_LOAD_STEPS
        is_prompt = t < N_MLP_PROMPT_TILES
        grp = _tile_group(t, MLP_TM)

        def mod(ref):
            return ref[pl.ds(grp, 1), :]

        rows = [slice(r * half, (r + 1) * half) for r in range(2)]
        o = [_dot(jnp.where(is_prompt, ap_ref[rw, :], as_ref[rw, :]), wo_s[...]) for rw in rows]
        x1, h, u0 = [], [], []
        for r, rw in enumerate(rows):
            x = jnp.where(is_prompt, x_refs[0][rw, :], x_refs[1][rw, :]) if first else x_refs[0][rw, :]
            x1.append(x + mod(g1_ref) * o[r])
            h.append(_norm_mod(x1[r], gain_ref[...], mod(sh_ref), mod(sc_ref)).astype(BF16))
            u0.append(_dot(h[r], w1_s[0]))
        h = jnp.concatenate(h, axis=0)
        acc = []

        def up(c):
            return jnp.concatenate(u0, axis=0) if c == 0 else _dot(h, w1_s[c])

        def down(c, u):
            u = jnp.square(jnp.maximum(u, 0.0)).astype(BF16)
            if c + 1 < n_chunks:
                y = _dot(u, w2_s[c])
                acc[:] = [y if not acc else acc[0] + y]
            else:
                acc[:] = [acc[0][rw] + _dot(u[rw], w2_s[c]) for rw in rows]

        _head_pipeline(n_chunks, up, down)
        for r, rw in enumerate(rows):
            out = x1[r] + mod(g2_ref) * acc[r]
            if last:
                @pl.when(is_prompt)
                def _(out=out, rw=rw):
                    op_ref[rw, :] = out

                @pl.when(jnp.logical_not(is_prompt))
                def _(out=out, rw=rw):
                    os_ref[rw, :] = out
            else:
                o_ref[rw, :] = out
                hn_ref[rw, :] = _norm_mod(out, ngain_ref[...], mod(nsh_ref), mod(nsc_ref)).astype(BF16)


def _omlp(attn_p, attn_s, w_o, x, mods, gain_ffn, w1_all, w2_all, layer, next_gain):
    first, last = layer == 0, next_gain is None
    n_lat_tiles = N_LAT_TOK // MLP_TM

    def tok(s):
        return jnp.maximum(s - N_LOAD_STEPS, 0)

    p_spec = pl.BlockSpec((MLP_TM, D_MODEL), lambda s: (jnp.minimum(tok(s), N_MLP_PROMPT_TILES - 1), 0))
    l_spec = pl.BlockSpec((MLP_TM, D_MODEL),
                          lambda s: (jnp.clip(tok(s) - N_MLP_PROMPT_TILES, 0, n_lat_tiles - 1), 0))
    w1_spec = pl.BlockSpec((None, D_MODEL, MLP_LOAD_COLS),
                           lambda s: (layer, 0, jnp.minimum(s, N_LOAD_STEPS - 1)))
    w2_spec = pl.BlockSpec((None, MLP_LOAD_COLS, D_MODEL),
                           lambda s: (layer, jnp.minimum(s, N_LOAD_STEPS - 1), 0))
    t_spec = pl.BlockSpec((MLP_TM, D_MODEL), lambda s: (tok(s), 0))
    n_chunks = D_FF // MLP_FF_CHUNK
    split = ([p_spec, l_spec], [jax.ShapeDtypeStruct((N_PROMPT_TOK, D_MODEL), F32),
                                jax.ShapeDtypeStruct((N_LAT_TOK, D_MODEL), F32)])
    in_specs = ([p_spec, l_spec, _const_spec(w_o.shape)] + (split[0] if first else [t_spec])
                + [_mod_spec(layer, 2), gain_ffn.spec, _mod_spec(layer, 3), _mod_spec(layer, 4),
                   _mod_spec(layer, 5),
                   w1_spec, w2_spec])
    args = ([attn_p, attn_s, w_o] + (list(x) if first else [x])
            + [mods, gain_ffn.array, mods, mods, mods, w1_all, w2_all])
    if last:
        out_specs, out_shape = split
    else:
        in_specs += [next_gain.spec, _mod_spec(layer + 1, 0), _mod_spec(layer + 1, 1)]
        args += [next_gain.array, mods, mods]
        out_specs = [t_spec, t_spec]
        out_shape = [jax.ShapeDtypeStruct((N_TOK, D_MODEL), F32), jax.ShapeDtypeStruct((N_TOK, D_MODEL), BF16)]
    return pl.pallas_call(
        functools.partial(_omlp_kernel, first=first, last=last),
        grid=(N_LOAD_STEPS + N_TOK // MLP_TM,),
        in_specs=in_specs,
        out_specs=out_specs,
        out_shape=out_shape,
        scratch_shapes=[pltpu.VMEM((D_MODEL, D_MODEL), BF16),
                        pltpu.VMEM((n_chunks, D_MODEL, MLP_FF_CHUNK), BF16),
                        pltpu.VMEM((n_chunks, MLP_FF_CHUNK, D_MODEL), BF16)],
        compiler_params=_cparams(1),
        name="omlp",
    )(*args)


def kernel(x_prompt, x_sample, cache_att_k, cache_att_v, cache_diff_k, cache_diff_v, cache_swa_k, cache_swa_v, cache_mla_ckv, cache_mla_kpe, c, c_ctx, ada_w, ada_b, norm_mix, norm_ffn, att_w_qkv, att_q_norm, att_k_norm, att_w_o, diff_w_qkv, diff_q_norm, diff_k_norm, diff_lq1, diff_lk1, diff_lq2, diff_lk2, diff_subln, diff_w_o, swa_w_qkv, swa_q_norm, swa_k_norm, swa_sink, swa_w_o, mla_w_in, mla_q_a_norm, mla_kv_a_norm, mla_w_uq, mla_w_ukv, mla_q_norm, mla_k_norm, mla_w_o, mlp_w1, mlp_w2):
    xp = x_prompt.reshape(N_PROMPT_TOK, D_MODEL)
    xs = x_sample.reshape(N_LAT_TOK, D_MODEL)
    cond = jnp.concatenate([c_ctx[None], c, jnp.zeros((COND_ROWS - 1 - DEC_BATCH, D_MODEL), F32)], axis=0)
    mods_all = _modulation(cond, ada_w, ada_b)

    tab_att = _rope_tables(ATT_HEAD_DIM)
    tab_64 = _rope_tables(DIFF_HEAD_DIM)

    pk = _ParamPack()
    g_mix = [pk.add(norm_mix[l]) for l in range(DEPTH)]
    g_ffn = [pk.add(norm_ffn[l]) for l in range(DEPTH)]
    att_qg = pk.add(att_q_norm[0], PROJ_UNIT // ATT_HEAD_DIM, ATT_HEAD_DIM ** -0.5 * LOG2E)
    att_kg = pk.add(att_k_norm[0], PROJ_UNIT // ATT_HEAD_DIM)
    diff_qg = pk.add(diff_q_norm[0], PROJ_UNIT // DIFF_HEAD_DIM, DIFF_HEAD_DIM ** -0.5 * LOG2E)
    diff_kg = pk.add(diff_k_norm[0], PROJ_UNIT // DIFF_HEAD_DIM)
    diff_small = [pk.add(v[0]) for v in (diff_lq1, diff_lk1, diff_lq2, diff_lk2, diff_subln)]
    swa_qg = pk.add(swa_q_norm[0], PROJ_UNIT // SWA_HEAD_DIM, SWA_HEAD_DIM ** -0.5 * LOG2E)
    swa_kg = pk.add(swa_k_norm[0], PROJ_UNIT // SWA_HEAD_DIM)
    mla_qs = (MLA_NOPE + MLA_ROPE) ** -0.5 * LOG2E
    mla_qa = pk.add(mla_q_a_norm[0])
    mla_kva = pk.add(mla_kv_a_norm[0])
    mla_qg = pk.add(mla_q_norm[0][:MLA_NOPE], 1, mla_qs)
    mla_qgp = pk.add(mla_q_norm[0][MLA_NOPE:], LANES // MLA_ROPE, mla_qs)
    mla_kg = pk.add(mla_k_norm[0][:MLA_NOPE], PROJ_UNIT // MLA_NOPE)
    mla_kgp = pk.add(mla_k_norm[0][MLA_NOPE:], LANES // MLA_ROPE)
    pk.build()

    outs = {}
    x = (xp, xs)
    for layer in range(DEPTH):
        gain_ffn = g_ffn[layer]
        if layer == 0:
            q, k, v, outs["att_k"], outs["att_v"] = _proj_att(
                xp, xs, mods_all, g_mix[layer], att_w_qkv[0], att_qg, att_kg, tab_att)
            attn_p, attn_s = _att_attend(q, k, v, cache_att_k, cache_att_v)
            w_o = att_w_o[0]
        elif layer == 1:
            q, k, v, outs["diff_k"], outs["diff_v"] = _proj_diff(h, diff_w_qkv[0], diff_qg, diff_kg, tab_64)
            lam_init = 0.8 - 0.6 * math.exp(-0.3 * layer)
            ck = cache_diff_k[:, 0].transpose(0, 1, 3, 2, 4).reshape(
                DEC_BATCH, DIFF_HEADS, PAST_LEN, LANES)
            attn_p, attn_s = _diff_attend(q, k, v, ck, cache_diff_v, *diff_small, lam_init)
            w_o = diff_w_o[0]
        elif layer == 2:
            q, kd, vd, outs["swa_k"], outs["swa_v"] = _proj_swa(h, swa_w_qkv[0], swa_qg, swa_kg, tab_64)
            ckd = jnp.concatenate([cache_swa_k[:, 0]] * 2, axis=-1).astype(BF16)
            cvd = jnp.concatenate([cache_swa_v[:, 0]] * 2, axis=-1).astype(BF16)
            attn_p, attn_s = _swa_attend(q, kd, vd, ckd, cvd, swa_sink[0].astype(F32))
            w_o = swa_w_o[0]
        else:
            qn, qp, outs["mla_ckv"], outs["mla_kpe"], kn, kp, vv = _proj_mla(
                h, mla_w_in[0], mla_qa, mla_kva, mla_w_uq[0], mla_qg, mla_qgp,
                mla_w_ukv[0], mla_kg, mla_kgp, tab_64)
            c_ckv = cache_mla_ckv[:, 0].reshape(DEC_BATCH * PAST_LEN, MLA_KV_RANK)
            c_kpe = cache_mla_kpe[:, 0].reshape(DEC_BATCH * PAST_LEN, MLA_ROPE)
            c_kpe = jnp.concatenate([c_kpe, c_kpe], axis=-1)
            knc, kpc, vc = _mla_ctx(c_ckv, c_kpe, mla_w_ukv[0], mla_kg, mla_kgp)
            attn_p, attn_s = _mla_attend(qn, qp, kn, kp, vv, knc, kpc, vc)
            w_o = mla_w_o[0]
        if layer + 1 < DEPTH:
            x, h = _omlp(attn_p, attn_s, w_o, x, mods_all, gain_ffn, mlp_w1, mlp_w2, layer,
                         g_mix[layer + 1])
        else:
            xp, xs = _omlp(attn_p, attn_s, w_o, x, mods_all, gain_ffn, mlp_w1, mlp_w2, layer, None)

    y_prompt = xp.reshape(BATCH, SEQ, D_MODEL)
    y_sample = xs.reshape(DEC_BATCH, DEC_SEQ, D_MODEL)
    for name in ("diff_k", "swa_k", "swa_v", "mla_kpe"):
        outs[name] = jnp.swapaxes(outs[name], -1, -2)
    return (y_prompt, y_sample, outs["att_k"], outs["att_v"], outs["diff_k"], outs["diff_v"],
            outs["swa_k"], outs["swa_v"], outs["mla_ckv"], outs["mla_kpe"])
```

```python
import functools
import math

import numpy as np
import jax
import jax.numpy as jnp
from jax import lax
from jax.experimental import pallas as pl
from jax.experimental.pallas import tpu as pltpu

D_MODEL = 1024
BATCH = 16
SEQ = 256
DEPTH = 4
DEC_BATCH = 2
DEC_SEQ = 1024
PAST_LEN = 256
GRID_W = 64
ROPE_THETA = 10000.0
EPS = 1e-6
D_FF = 4 * D_MODEL
MOD_CHUNKS = 6
LOG2E = 1.4426950408889634

ATT_HEADS, ATT_KV_HEADS, ATT_HEAD_DIM = 8, 2, 128
DIFF_HEADS, DIFF_HEAD_DIM = 8, 64
SWA_HEADS, SWA_KV_HEADS, SWA_HEAD_DIM, WINDOW = 16, 4, 64, 128
MLA_HEADS, MLA_NOPE, MLA_ROPE, MLA_VDIM = 8, 128, 64, 128
MLA_Q_RANK, MLA_KV_RANK = 512, 256

LANES = 128
HALF = LANES // 2
TM = 256
N_PROMPT_TOK = BATCH * SEQ
N_LAT_TOK = DEC_BATCH * DEC_SEQ
N_TOK = N_PROMPT_TOK + N_LAT_TOK
N_PROMPT_TILES = N_PROMPT_TOK // TM
LAT_TQ = 512
LAT_BLOCK0 = N_PROMPT_TOK // DEC_SEQ
COND_ROWS = 8
PROJ_TM = 512
PROJ_BATCHES = PROJ_TM // SEQ
N_PROJ_TILES = N_TOK // PROJ_TM
N_PROJ_PROMPT = N_PROMPT_TOK // PROJ_TM
PROJ_UNIT = 2 * LANES
MLP_TM = 512
MLP_FF_CHUNK = 512
MLP_LOAD_COLS = 256
N_LOAD_STEPS = D_FF // MLP_LOAD_COLS
N_MLP_PROMPT_TILES = N_PROMPT_TOK // MLP_TM
SWA_QB = 128
SWA_BLOCKS = 2
ATT_UNIT_HEADS = 4
PROMPT_SEQS = 4
VMEM_LIMIT = 56 * 1024 * 1024

F32 = jnp.float32
BF16 = jnp.bfloat16


def _cparams(n_axes):
    return pltpu.CompilerParams(dimension_semantics=("arbitrary",) * n_axes,
                                vmem_limit_bytes=VMEM_LIMIT)


def _dot(a, b):
    return jnp.dot(a, b, preferred_element_type=F32)


def _dot_nt(a, b):
    return lax.dot_general(a, b, (((1,), (1,)), ((), ())), preferred_element_type=F32)


def _dot_tn(a, b):
    return lax.dot_general(a, b, (((0,), (0,)), ((), ())), preferred_element_type=F32)


def _const_spec(shape):
    nd = len(shape)
    return pl.BlockSpec(shape, lambda *_: (0,) * nd, pipeline_mode=pl.Buffered(1))


class _ParamPack:
    def __init__(self):
        self._rows, self.array = [], None

    def add(self, v, repeat=1, scale=1.0):
        n = v.shape[0] * repeat
        row = _ParamRow(self, -(-n // LANES) * LANES)
        self._rows.append((row, [v] * repeat, scale, n))
        return row

    def build(self):
        pieces, scales, offset = [], [], 0
        for row, vs, scale, n in sorted(self._rows, key=lambda r: -r[0].width):
            row.offset = offset
            pieces += vs + ([jnp.zeros((row.width - n,), F32)] if row.width > n else [])
            scales.append(np.full((row.width,), scale, np.float32))
            offset += row.width
        flat = jnp.concatenate([p.astype(F32) for p in pieces]) * jnp.asarray(np.concatenate(scales))
        self.array = flat.reshape(1, offset)


class _ParamRow:
    def __init__(self, pack, width):
        self.pack, self.width, self.offset = pack, width, None

    @property
    def array(self):
        return self.pack.array

    @property
    def spec(self):
        block = self.offset // self.width
        return pl.BlockSpec((1, self.width), lambda *_: (0, block), pipeline_mode=pl.Buffered(1))


def _chunk(ref, c, width=LANES):
    return ref[:, c * width:(c + 1) * width]


def _put(ref, c, val):
    ref[:, c * LANES:(c + 1) * LANES] = val.astype(ref.dtype)


def _tile_group(i, rows):
    n_prompt = N_PROMPT_TOK // rows
    return jnp.where(i < n_prompt, 0, 1 + (i - n_prompt) // (DEC_SEQ // rows))


def _rope_tile(i):
    return jnp.maximum(i - N_PROJ_PROMPT, 0) % (DEC_SEQ // PROJ_TM)


def _norm_mod(x, gain, shift, scale):
    ms = jnp.mean(x * x, axis=-1, keepdims=True)
    return x * lax.rsqrt(ms + EPS) * (gain * (1.0 + scale)) + shift


def _lane_lo(shape):
    return lax.broadcasted_iota(jnp.int32, shape, len(shape) - 1) < HALF


def _rope(y, cos, sin_prev, sin_next, quarter):
    return (y * cos + pltpu.roll(y, quarter, 1) * sin_prev
            + pltpu.roll(y, LANES - quarter, 1) * sin_next)


def _rope_tables(rot_dim):
    half = rot_dim // 2
    quarter = rot_dim // 4
    inv = np.float32(ROPE_THETA) ** (-np.arange(0, half, 2, dtype=np.float32) / np.float32(half))
    pos = np.arange(DEC_SEQ)
    row = (pos // GRID_W).astype(np.float32)
    col = (pos % GRID_W).astype(np.float32)
    lane = np.arange(LANES)
    dd = lane % rot_dim
    q = dd // quarter
    f = dd % quarter
    ang = np.where((q < 2)[None, :], row[:, None], col[:, None]) * inv[f][None, :]
    ang = ang.astype(np.float32)
    cos = np.cos(ang).astype(np.float32)
    sin = np.sin(ang).astype(np.float32)
    odd = (q % 2 == 1)[None, :]
    sin_prev = np.where(odd, sin, 0.0).astype(np.float32)
    sin_next = np.where(odd, 0.0, -sin).astype(np.float32)
    return jnp.asarray(cos), jnp.asarray(sin_prev), jnp.asarray(sin_next)


def _lane_sum_matrix(rows, cols, value=1.0):
    lane = np.arange(LANES)
    m = np.where(rows(lane)[:, None] & cols(lane)[None, :], value, 0.0).astype(np.float32)
    return jnp.asarray(m, dtype=BF16)


def _group_mean_matrix(group):
    lane = np.arange(PROJ_UNIT)
    m = np.where((lane[:, None] // group) == (lane[None, :] // group), 1.0 / group, 0.0)
    return jnp.asarray(m.astype(np.float32), dtype=BF16)


def _group_sum_matrix():
    lane = np.arange(PROJ_UNIT)
    m = np.where((lane[:, None] // LANES) == (lane[None, :] // LANES), 1.0, 0.0)
    return jnp.asarray(m.astype(np.float32), dtype=BF16)


def _sq_bf16(y):
    return (y * y).astype(BF16)


def _head_norm(y, m_ref, gain):
    return y * lax.rsqrt(_dot(_sq_bf16(y), m_ref[...]) + EPS) * gain


def _halves(y):
    return [y[:, t * LANES:(t + 1) * LANES] for t in range(y.shape[1] // LANES)]


def _matmul_units(h, w_ref, n_units, width, emit):
    def unit(u):
        return _dot(h, w_ref[:, u * width:(u + 1) * width])

    nxt = unit(0)
    for u in range(n_units):
        cur = nxt
        if u + 1 < n_units:
            nxt = unit(u + 1)
        emit(u, cur)


def _cast_once(i, w_ref, w_s):
    @pl.when(i == 0)
    def _():
        w_s[...] = w_ref[...].astype(BF16)


def _by_tile_kind(i, body):
    pl.when(i < N_PROJ_PROMPT)(functools.partial(body, False))
    pl.when(i >= N_PROJ_PROMPT)(functools.partial(body, True))


def _rope_args(lat, cos_ref, sp_ref, sn_ref, rot_dim):
    return (cos_ref[...], sp_ref[...], sn_ref[...], rot_dim // 4) if lat else None


def _maybe_rope(y, rope):
    return y if rope is None else _rope(y, *rope)


def _cache_rows(ref, index, val):
    for b in range(PROJ_BATCHES):
        ref[(b, 0) + tuple(index)] = val[b * SEQ:(b + 1) * SEQ]


def _cache_rows_t(ref, indices, val):
    for b in range(PROJ_BATCHES):
        t = val[b * SEQ:(b + 1) * SEQ].T
        for j, index in enumerate(indices):
            ref[(b, 0) + tuple(index)] = t[j * HALF:(j + 1) * HALF]


def _softmax2_parts(s_list, extra=None):
    m = jnp.max(s_list[0], axis=0, keepdims=True)
    for s in s_list[1:]:
        m = jnp.maximum(m, jnp.max(s, axis=0, keepdims=True))
    if extra is not None:
        m = jnp.maximum(m, extra)
    ps = [jnp.exp2(s - m) for s in s_list]
    mass = ps[0].sum(axis=0, keepdims=True)
    for p in ps[1:]:
        mass = mass + p.sum(axis=0, keepdims=True)
    if extra is not None:
        mass = mass + jnp.exp2(extra - m)
    return [p.astype(BF16) for p in ps], 1.0 / mass


def _head_pipeline(n, scores, finish):
    nxt = scores(0)
    for h in range(n):
        cur = nxt
        if h + 1 < n:
            nxt = scores(h + 1)
        finish(h, cur)


def _seq_pipeline(refs, seqs, n, make):
    fns = []
    for b in range(seqs):
        views = [r.at[b * (r.shape[0] // seqs):(b + 1) * (r.shape[0] // seqs)] for r in refs]
        fns.append(make(views, b))
    _head_pipeline(seqs * n, lambda i: fns[i // n][0](i % n), lambda i, s: fns[i // n][1](i % n, s))


def _pv(ps, values):
    o = None
    for p, v in zip(ps, values):
        t = _dot_tn(v, p)
        o = t if o is None else o + t
    return o


def _split_halves(q):
    lo = _lane_lo(q.shape)
    zero = jnp.zeros_like(q)
    return jnp.where(lo, q, zero), jnp.where(lo, zero, q)


def _mod_kernel(cond_ref, w_ref, b_ref, o_ref):
    c = cond_ref[...]
    s = (c * jax.nn.sigmoid(c)).astype(BF16)
    o_ref[0] = _dot(s, w_ref[0].astype(BF16)) + b_ref[0]


def _modulation(cond, ada_w, ada_b):
    tn = 3072
    n = MOD_CHUNKS * D_MODEL
    return pl.pallas_call(
        _mod_kernel,
        grid=(DEPTH, n // tn),
        in_specs=[
            pl.BlockSpec((COND_ROWS, D_MODEL), lambda l, j: (0, 0)),
            pl.BlockSpec((1, D_MODEL, tn), lambda l, j: (l, 0, j)),
            pl.BlockSpec((1, 1, tn), lambda l, j: (l, 0, j)),
        ],
        out_specs=pl.BlockSpec((1, COND_ROWS, tn), lambda l, j: (l, 0, j)),
        out_shape=jax.ShapeDtypeStruct((DEPTH, COND_ROWS, n), F32),
        compiler_params=_cparams(2),
        name="modulation",
    )(cond, ada_w, ada_b.reshape(DEPTH, 1, n))


def _mod_spec(layer, chunk):
    return pl.BlockSpec((None, COND_ROWS, D_MODEL), lambda i: (layer, 0, chunk))


def _mod_row(ref, i):
    return ref[pl.ds(_tile_group(i, PROJ_TM), 1), :]


_ROPE_SPEC = pl.BlockSpec((PROJ_TM, LANES), lambda i: (_rope_tile(i), 0))
_LANE_MAT_SPEC = _const_spec((LANES, LANES))
_UNIT_MAT_SPEC = _const_spec((PROJ_UNIT, PROJ_UNIT))


def _tok_spec(width):
    return pl.BlockSpec((PROJ_TM, width), lambda i: (i, 0))


_XP_SPEC = pl.BlockSpec((PROJ_TM, D_MODEL), lambda i: (jnp.minimum(i, N_PROJ_PROMPT - 1), 0))
_XS_SPEC = pl.BlockSpec((PROJ_TM, D_MODEL), lambda i: (jnp.maximum(i - N_PROJ_PROMPT, 0), 0))


def _cache_spec(*dims):
    nd = len(dims)
    return pl.BlockSpec((PROJ_BATCHES, 1) + dims,
                        lambda i: (jnp.minimum(i, N_PROJ_PROMPT - 1), 0) + (0,) * nd)


def _cache_shape(*dims):
    return jax.ShapeDtypeStruct((BATCH, 1) + dims, F32)


def _proj_att_kernel(xp_ref, xs_ref, gain_ref, sh_ref, sc_ref, w_ref, qg_ref, kg_ref, m_ref,
                     cos_ref, sp_ref, sn_ref, q_ref, k_ref, v_ref, ck_ref, cv_ref, w_s):
    i = pl.program_id(0)
    _cast_once(i, w_ref, w_s)
    x = jnp.where(i < N_PROJ_PROMPT, xp_ref[...], xs_ref[...])
    h = _norm_mod(x, gain_ref[...], _mod_row(sh_ref, i), _mod_row(sc_ref, i)).astype(BF16)
    per = PROJ_UNIT // LANES
    nq, nk = ATT_HEADS // per, ATT_KV_HEADS // per

    def body(lat):
        rope = _rope_args(lat, cos_ref, sp_ref, sn_ref, ATT_HEAD_DIM)

        def emit(u, y):
            if u < nq + nk:
                y = _head_norm(y, m_ref, qg_ref[...] if u < nq else kg_ref[...])
            for t, yc in enumerate(_halves(y)):
                if u < nq:
                    _put(q_ref, u * per + t, _maybe_rope(yc, rope))
                elif u < nq + nk:
                    kn = _maybe_rope(yc, rope)
                    _put(k_ref, (u - nq) * per + t, kn)
                    if not lat:
                        _cache_rows(ck_ref, [(u - nq) * per + t], kn)
                else:
                    _put(v_ref, (u - nq - nk) * per + t, yc)
                    if not lat:
                        _cache_rows(cv_ref, [(u - nq - nk) * per + t], yc)

        _matmul_units(h, w_s, nq + 2 * nk, PROJ_UNIT, emit)

    _by_tile_kind(i, body)


def _proj_att(xp, xs, mods, gain, w, qg, kg, tables):
    nq, nk = ATT_HEADS * ATT_HEAD_DIM, ATT_KV_HEADS * ATT_HEAD_DIM
    return pl.pallas_call(
        _proj_att_kernel,
        grid=(N_PROJ_TILES,),
        in_specs=[_XP_SPEC, _XS_SPEC, gain.spec, _mod_spec(0, 0), _mod_spec(0, 1),
                  _const_spec(w.shape), qg.spec, kg.spec,
                  _UNIT_MAT_SPEC, _ROPE_SPEC, _ROPE_SPEC, _ROPE_SPEC],
        out_specs=[_tok_spec(nq), _tok_spec(nk), _tok_spec(nk),
                   _cache_spec(ATT_KV_HEADS, SEQ, ATT_HEAD_DIM), _cache_spec(ATT_KV_HEADS, SEQ, ATT_HEAD_DIM)],
        out_shape=[jax.ShapeDtypeStruct((N_TOK, nq), BF16),
                   jax.ShapeDtypeStruct((N_TOK, nk), BF16),
                   jax.ShapeDtypeStruct((N_TOK, nk), BF16),
                   _cache_shape(ATT_KV_HEADS, SEQ, ATT_HEAD_DIM), _cache_shape(ATT_KV_HEADS, SEQ, ATT_HEAD_DIM)],
        scratch_shapes=[pltpu.VMEM(w.shape, BF16)],
        compiler_params=_cparams(1),
        name="proj_att",
    )(xp, xs, gain.array, mods, mods, w, qg.array, kg.array, _group_mean_matrix(ATT_HEAD_DIM), *tables)


def _proj_diff_kernel(h_ref, w_ref, qg_ref, kg_ref, m_ref,
                      cos_ref, sp_ref, sn_ref, q_ref, k_ref, v_ref, ck_ref, cv_ref, w_s):
    i = pl.program_id(0)
    _cast_once(i, w_ref, w_s)
    h = h_ref[...]
    per = PROJ_UNIT // LANES
    nu = DIFF_HEADS // per

    def body(lat):
        rope = _rope_args(lat, cos_ref, sp_ref, sn_ref, DIFF_HEAD_DIM)

        def emit(u, y):
            if u < 2 * nu:
                y = _head_norm(y, m_ref, qg_ref[...] if u < nu else kg_ref[...])
            for t, yc in enumerate(_halves(y)):
                hd = (u % nu) * per + t
                if u < nu:
                    _put(q_ref, hd, _maybe_rope(yc, rope))
                elif u < 2 * nu:
                    kn = _maybe_rope(yc, rope)
                    _put(k_ref, hd, kn)
                    if not lat:
                        _cache_rows_t(ck_ref, [[hd, 0], [hd, 1]], kn)
                else:
                    _put(v_ref, hd, yc)
                    if not lat:
                        _cache_rows(cv_ref, [hd], yc)

        _matmul_units(h, w_s, 3 * nu, PROJ_UNIT, emit)

    _by_tile_kind(i, body)


def _proj_diff(h, w, qg, kg, tables):
    n = DIFF_HEADS * 2 * DIFF_HEAD_DIM
    return pl.pallas_call(
        _proj_diff_kernel,
        grid=(N_PROJ_TILES,),
        in_specs=[_tok_spec(D_MODEL), _const_spec(w.shape), qg.spec, kg.spec,
                  _UNIT_MAT_SPEC, _ROPE_SPEC, _ROPE_SPEC, _ROPE_SPEC],
        out_specs=[_tok_spec(n), _tok_spec(n), _tok_spec(n),
                   _cache_spec(DIFF_HEADS, 2, DIFF_HEAD_DIM, SEQ), _cache_spec(DIFF_HEADS, SEQ, 2 * DIFF_HEAD_DIM)],
        out_shape=[jax.ShapeDtypeStruct((N_TOK, n), BF16)] * 3
                  + [_cache_shape(DIFF_HEADS, 2, DIFF_HEAD_DIM, SEQ),
                     _cache_shape(DIFF_HEADS, SEQ, 2 * DIFF_HEAD_DIM)],
        scratch_shapes=[pltpu.VMEM(w.shape, BF16)],
        compiler_params=_cparams(1),
        name="proj_diff",
    )(h, w, qg.array, kg.array, _group_mean_matrix(DIFF_HEAD_DIM), *tables)


def _dup_halves(yc):
    lo = _lane_lo(yc.shape)
    sw = pltpu.roll(yc, HALF, 1)
    return jnp.where(lo, yc, sw), jnp.where(lo, sw, yc)


def _proj_swa_kernel(h_ref, w_ref, qg_ref, kg_ref, m_ref,
                     cos_ref, sp_ref, sn_ref, q_ref, kd_ref, vd_ref, ck_ref, cv_ref, w_s):
    i = pl.program_id(0)
    _cast_once(i, w_ref, w_s)
    h = h_ref[...]
    per = PROJ_UNIT // LANES
    nq = SWA_HEADS * SWA_HEAD_DIM // PROJ_UNIT
    nk = SWA_KV_HEADS * SWA_HEAD_DIM // PROJ_UNIT

    def body(lat):
        rope = _rope_args(lat, cos_ref, sp_ref, sn_ref, SWA_HEAD_DIM)

        def emit(u, y):
            if u < nq + nk:
                y = _head_norm(y, m_ref, qg_ref[...] if u < nq else kg_ref[...])
            for t, yc in enumerate(_halves(y)):
                if u < nq:
                    _put(q_ref, u * per + t, _maybe_rope(yc, rope))
                    continue
                if u < nq + nk:
                    j, c_ref, d_ref = (u - nq) * per + t, ck_ref, kd_ref
                    yc = _maybe_rope(yc, rope)
                else:
                    j, c_ref, d_ref = (u - nq - nk) * per + t, cv_ref, vd_ref
                for a, dup in enumerate(_dup_halves(yc)):
                    _put(d_ref, 2 * j + a, dup)
                if not lat:
                    _cache_rows_t(c_ref, [[2 * j], [2 * j + 1]], yc)

        _matmul_units(h, w_s, nq + 2 * nk, PROJ_UNIT, emit)

    _by_tile_kind(i, body)


def _proj_swa(h, w, qg, kg, tables):
    nq, nk = SWA_HEADS * SWA_HEAD_DIM, SWA_KV_HEADS * SWA_HEAD_DIM
    return pl.pallas_call(
        _proj_swa_kernel,
        grid=(N_PROJ_TILES,),
        in_specs=[_tok_spec(D_MODEL), _const_spec(w.shape), qg.spec, kg.spec,
                  _UNIT_MAT_SPEC, _ROPE_SPEC, _ROPE_SPEC, _ROPE_SPEC],
        out_specs=[_tok_spec(nq), _tok_spec(2 * nk), _tok_spec(2 * nk),
                   _cache_spec(SWA_KV_HEADS, SWA_HEAD_DIM, SEQ), _cache_spec(SWA_KV_HEADS, SWA_HEAD_DIM, SEQ)],
        out_shape=[jax.ShapeDtypeStruct((N_TOK, nq), BF16),
                   jax.ShapeDtypeStruct((N_TOK, 2 * nk), BF16),
                   jax.ShapeDtypeStruct((N_TOK, 2 * nk), BF16),
                   _cache_shape(SWA_KV_HEADS, SWA_HEAD_DIM, SEQ), _cache_shape(SWA_KV_HEADS, SWA_HEAD_DIM, SEQ)],
        scratch_shapes=[pltpu.VMEM(w.shape, BF16)],
        compiler_params=_cparams(1),
        name="proj_swa",
    )(h, w, qg.array, kg.array, _group_mean_matrix(SWA_HEAD_DIM), *tables)


def _mla_lane_matrices():
    everything = lambda lane: lane >= 0
    lo = _lane_sum_matrix(lambda lane: lane < HALF, everything)
    hi = _lane_sum_matrix(lambda lane: lane >= HALF, everything)
    return jnp.concatenate([lo, hi], axis=1), lo


def _mla_keys(ckv, kpe, w_s, kg_ref, kgp_ref, sum_ref, lo_ref, rope, kn_ref, kp_ref, v_ref):
    pe_ss = _dot(_sq_bf16(kpe), lo_ref[...])
    pe_ss = jnp.concatenate([pe_ss, pe_ss], axis=1)
    lo = _lane_lo(kpe.shape)
    inv_d = 1.0 / (MLA_NOPE + MLA_ROPE)

    def emit(j, y):
        kn = jnp.concatenate([y[:, :LANES], y[:, 2 * LANES:3 * LANES]], axis=1)
        r = lax.rsqrt((_dot(_sq_bf16(kn), sum_ref[...]) + pe_ss) * inv_d + EPS)
        kn = kn * r * kg_ref[...]
        for a in range(2):
            _put(kn_ref, 2 * j + a, kn[:, a * LANES:(a + 1) * LANES])
            _put(v_ref, 2 * j + a, y[:, (2 * a + 1) * LANES:(2 * a + 2) * LANES])
        pe = kpe * jnp.where(lo, r[:, :LANES], r[:, LANES:]) * kgp_ref[...]
        _put(kp_ref, j, _maybe_rope(pe, rope))

    _matmul_units(ckv, w_s, MLA_HEADS // 2, 4 * LANES, emit)


def _proj_mla_kernel(h_ref, w_in_ref, qa_ref, kva_ref, w_uq_ref, qg_ref, qgp_ref, lohi_ref, lo_ref,
                     w_ukv_ref, kg_ref, kgp_ref, sum_ref, cos_ref, sp_ref, sn_ref,
                     qn_ref, qp_ref, c_ckv_ref, c_kpe_ref, kn_ref, kp_ref, v_ref,
                     w_in_s, w_uq_s, w_ukv_s):
    i = pl.program_id(0)
    n_in = MLA_Q_RANK + MLA_KV_RANK + MLA_ROPE

    @pl.when(i == 0)
    def _():
        w_in_s[...] = jnp.zeros_like(w_in_s)
        w_in_s[:, :n_in] = w_in_ref[...].astype(BF16)
        w_uq_s[...] = w_uq_ref[...].astype(BF16)
        w_ukv_s[...] = w_ukv_ref[...].astype(BF16)

    y = _dot(h_ref[...], w_in_s[...])
    c_q = y[:, :MLA_Q_RANK]
    c_kv = y[:, MLA_Q_RANK:MLA_Q_RANK + MLA_KV_RANK]
    kpe = y[:, MLA_Q_RANK + MLA_KV_RANK:]
    kpe = kpe + pltpu.roll(kpe, HALF, 1)
    ckv = c_kv * lax.rsqrt(jnp.mean(c_kv * c_kv, axis=-1, keepdims=True) + EPS) * kva_ref[...]
    cq = (c_q * lax.rsqrt(jnp.mean(c_q * c_q, axis=-1, keepdims=True) + EPS) * qa_ref[...]).astype(BF16)
    lo = _lane_lo((PROJ_TM, LANES))
    inv_d = 1.0 / (MLA_NOPE + MLA_ROPE)

    def body(lat):
        rope = _rope_args(lat, cos_ref, sp_ref, sn_ref, MLA_ROPE)
        if not lat:
            _cache_rows(c_ckv_ref, [], ckv)
            _cache_rows_t(c_kpe_ref, [[]], kpe)

        def emit(j, yq):
            y0, y1, y2 = _halves(yq)
            nopes = (y0, jnp.where(lo, pltpu.roll(y1, HALF, 1), pltpu.roll(y2, HALF, 1)))
            pe = jnp.where(lo, y1, y2)
            ss = (_dot(_sq_bf16(jnp.concatenate(nopes, axis=1)), sum_ref[...])
                  + _dot(_sq_bf16(pe), lohi_ref[...]))
            rs = _halves(lax.rsqrt(ss * inv_d + EPS))
            for a in range(2):
                _put(qn_ref, 2 * j + a, nopes[a] * rs[a] * qg_ref[...])
            _put(qp_ref, j, _maybe_rope(pe * jnp.where(lo, rs[0], rs[1]) * qgp_ref[...], rope))

        _matmul_units(cq, w_uq_s, MLA_HEADS // 2, 3 * LANES, emit)
        _mla_keys(ckv.astype(BF16), kpe, w_ukv_s, kg_ref, kgp_ref, sum_ref, lo_ref, rope,
                  kn_ref, kp_ref, v_ref)

    _by_tile_kind(i, body)


def _proj_mla(h, w_in, qa, kva, w_uq, qg, qgp, w_ukv, kg, kgp, tables):
    n_nope = MLA_HEADS * MLA_NOPE
    n_pe = MLA_HEADS * MLA_ROPE
    n_in = -(-w_in.shape[1] // LANES) * LANES
    return pl.pallas_call(
        _proj_mla_kernel,
        grid=(N_PROJ_TILES,),
        in_specs=[_tok_spec(D_MODEL),
                  _const_spec(w_in.shape), qa.spec, kva.spec,
                  _const_spec(w_uq.shape), qg.spec, qgp.spec,
                  _const_spec((LANES, PROJ_UNIT)), _LANE_MAT_SPEC,
                  _const_spec(w_ukv.shape), kg.spec, kgp.spec, _UNIT_MAT_SPEC,
                  _ROPE_SPEC, _ROPE_SPEC, _ROPE_SPEC],
        out_specs=[_tok_spec(n_nope), _tok_spec(n_pe),
                   _cache_spec(SEQ, MLA_KV_RANK), _cache_spec(MLA_ROPE, SEQ),
                   _tok_spec(n_nope), _tok_spec(n_pe), _tok_spec(n_nope)],
        out_shape=[jax.ShapeDtypeStruct((N_TOK, n_nope), BF16),
                   jax.ShapeDtypeStruct((N_TOK, n_pe), BF16),
                   _cache_shape(SEQ, MLA_KV_RANK), _cache_shape(MLA_ROPE, SEQ),
                   jax.ShapeDtypeStruct((N_TOK, n_nope), BF16),
                   jax.ShapeDtypeStruct((N_TOK, n_pe), BF16),
                   jax.ShapeDtypeStruct((N_TOK, n_nope), BF16)],
        scratch_shapes=[pltpu.VMEM((D_MODEL, n_in), BF16), pltpu.VMEM(w_uq.shape, BF16),
                        pltpu.VMEM(w_ukv.shape, BF16)],
        compiler_params=_cparams(1),
        name="proj_mla",
    )(h, w_in, qa.array, kva.array, w_uq, qg.array, qgp.array, *_mla_lane_matrices(),
      w_ukv, kg.array, kgp.array, _group_sum_matrix(), *tables)


def _mla_ctx_kernel(ckv_ref, kpe_ref, w_ref, kg_ref, kgp_ref, sum_ref, lo_ref, kn_ref, kp_ref, v_ref):
    _mla_keys(ckv_ref[...].astype(BF16), kpe_ref[...], w_ref[...].astype(BF16), kg_ref, kgp_ref,
              sum_ref, lo_ref, None, kn_ref, kp_ref, v_ref)


def _mla_ctx(ckv, kpe_dup, w_ukv, kg, kgp):
    n = ckv.shape[0]
    n_nope = MLA_HEADS * MLA_NOPE
    n_pe = MLA_HEADS * MLA_ROPE
    _, m_lo = _mla_lane_matrices()
    return pl.pallas_call(
        _mla_ctx_kernel,
        grid=(n // PROJ_TM,),
        in_specs=[_tok_spec(MLA_KV_RANK), _tok_spec(LANES), _const_spec(w_ukv.shape),
                  kg.spec, kgp.spec, _UNIT_MAT_SPEC, _LANE_MAT_SPEC],
        out_specs=[_tok_spec(n_nope), _tok_spec(n_pe), _tok_spec(n_nope)],
        out_shape=[jax.ShapeDtypeStruct((n, n_nope), BF16),
                   jax.ShapeDtypeStruct((n, n_pe), BF16),
                   jax.ShapeDtypeStruct((n, n_nope), BF16)],
        compiler_params=_cparams(1),
        name="mla_ctx",
    )(ckv, kpe_dup, w_ukv, kg.array, kgp.array, _group_sum_matrix(), m_lo)


def _prompt_spec(width):
    return pl.BlockSpec((PROMPT_SEQS * TM, width), lambda b: (b, 0))


def _latq_spec(rows, width):
    per = DEC_SEQ // rows
    return pl.BlockSpec((rows, width), lambda b, t: (N_PROMPT_TOK // rows + b * per + t, 0))


def _latkv_spec(width):
    return pl.BlockSpec((DEC_SEQ, width), lambda b, t: (LAT_BLOCK0 + b, 0))


def _lato_spec(rows):
    per = DEC_SEQ // rows
    return pl.BlockSpec((rows, D_MODEL), lambda b, t: (b * per + t, 0))


def _att_kernel(*refs, with_ctx, seqs):
    if with_ctx:
        q_ref, k_ref, v_ref, kc_ref, vc_ref, o_ref = refs
    else:
        q_ref, k_ref, v_ref, o_ref = refs
    tq = q_ref.shape[0] // seqs
    nu = ATT_UNIT_HEADS
    per_kv = ATT_HEADS // ATT_KV_HEADS // nu

    def make(views, _):
        q_v, k_v, v_v, o_v = views

        def scores(u):
            q = jnp.concatenate([_chunk(q_v, u * nu + g) for g in range(nu)], axis=0)
            s_list = [_dot_nt(_chunk(k_v, u // per_kv), q)]
            if with_ctx:
                s_list.append(_dot_nt(kc_ref[u // per_kv].astype(BF16), q))
            return s_list

        def finish(u, s_list):
            values = [_chunk(v_v, u // per_kv)]
            if with_ctx:
                values.append(vc_ref[u // per_kv].astype(BF16))
            ps, inv = _softmax2_parts(s_list)
            o = _pv(ps, values) * inv
            for g in range(nu):
                o_v[:, (u * nu + g) * LANES:(u * nu + g + 1) * LANES] = (
                    o[:, g * tq:(g + 1) * tq].T.astype(o_v.dtype))

        return scores, finish

    _seq_pipeline((q_ref, k_ref, v_ref, o_ref), seqs, ATT_HEADS // nu, make)


def _att_attend(q, k, v, cache_k, cache_v):
    nk = ATT_KV_HEADS * ATT_HEAD_DIM
    out_p = pl.pallas_call(
        functools.partial(_att_kernel, with_ctx=False, seqs=PROMPT_SEQS),
        grid=(N_PROMPT_TILES // PROMPT_SEQS,),
        in_specs=[_prompt_spec(D_MODEL), _prompt_spec(nk), _prompt_spec(nk)],
        out_specs=_prompt_spec(D_MODEL),
        out_shape=jax.ShapeDtypeStruct((N_PROMPT_TOK, D_MODEL), BF16),
        compiler_params=_cparams(1),
        name="att_prompt",
    )(q, k, v)
    ctx = pl.BlockSpec((None, None, ATT_KV_HEADS, PAST_LEN, LANES), lambda b, t: (b, 0, 0, 0, 0))
    out_s = pl.pallas_call(
        functools.partial(_att_kernel, with_ctx=True, seqs=1),
        grid=(DEC_BATCH, DEC_SEQ // LAT_TQ),
        in_specs=[_latq_spec(LAT_TQ, D_MODEL), _latkv_spec(nk), _latkv_spec(nk), ctx, ctx],
        out_specs=_lato_spec(LAT_TQ),
        out_shape=jax.ShapeDtypeStruct((N_LAT_TOK, D_MODEL), BF16),
        compiler_params=_cparams(2),
        name="att_latent",
    )(q, k, v, cache_k, cache_v)
    return out_p, out_s


def _diff_kernel(*refs, lam_init, with_ctx, seqs):
    if with_ctx:
        (q_ref, k_ref, v_ref, kc_ref, vc_ref, lq1_ref, lk1_ref, lq2_ref, lk2_ref, sub_ref, o_ref) = refs
    else:
        (q_ref, k_ref, v_ref, lq1_ref, lk1_ref, lq2_ref, lk2_ref, sub_ref, o_ref) = refs
    tq = q_ref.shape[0] // seqs
    lam = (jnp.exp(jnp.sum(lq1_ref[...] * lk1_ref[...], axis=-1, keepdims=True))
           - jnp.exp(jnp.sum(lq2_ref[...] * lk2_ref[...], axis=-1, keepdims=True)) + lam_init)
    diag = (lax.broadcasted_iota(jnp.int32, (LANES, LANES), 0)
            == lax.broadcasted_iota(jnp.int32, (LANES, LANES), 1))
    sub = jnp.sum(jnp.where(diag, sub_ref[...] * (1.0 - lam_init), 0.0), axis=1, keepdims=True)

    def make(views, _):
        q_v, k_v, v_v, o_v = views

        def scores(hd):
            q = jnp.concatenate(_split_halves(_chunk(q_v, hd)), axis=0)
            s_list = [_dot_nt(_chunk(k_v, hd), q)]
            if with_ctx:
                s_list.append(_dot_nt(kc_ref[hd].astype(BF16), q))
            return s_list

        def finish(hd, s_list):
            values = [_chunk(v_v, hd)]
            if with_ctx:
                values.append(vc_ref[hd].astype(BF16))
            ps, inv = _softmax2_parts(s_list)
            o = (_pv([p[:, :tq] for p in ps], values) * inv[:, :tq]
                 - _pv([p[:, tq:] for p in ps], values) * (lam * inv[:, tq:]))
            o = o * lax.rsqrt(jnp.mean(o * o, axis=0, keepdims=True) + EPS) * sub
            o_v[:, hd * LANES:(hd + 1) * LANES] = o.T.astype(o_v.dtype)

        return scores, finish

    _seq_pipeline((q_ref, k_ref, v_ref, o_ref), seqs, DIFF_HEADS, make)


def _diff_attend(q, k, v, cache_k_pair, cache_v, lq1, lk1, lq2, lk2, subln, lam_init):
    small_specs = [p.spec for p in (lq1, lk1, lq2, lk2, subln)]
    small = [p.array for p in (lq1, lk1, lq2, lk2, subln)]
    out_p = pl.pallas_call(
        functools.partial(_diff_kernel, lam_init=lam_init, with_ctx=False, seqs=PROMPT_SEQS),
        grid=(N_PROMPT_TILES // PROMPT_SEQS,),
        in_specs=[_prompt_spec(D_MODEL)] * 3 + small_specs,
        out_specs=_prompt_spec(D_MODEL),
        out_shape=jax.ShapeDtypeStruct((N_PROMPT_TOK, D_MODEL), BF16),
        compiler_params=_cparams(1),
        name="diff_prompt",
    )(q, k, v, *small)
    out_s = pl.pallas_call(
        functools.partial(_diff_kernel, lam_init=lam_init, with_ctx=True, seqs=1),
        grid=(DEC_BATCH, DEC_SEQ // LAT_TQ),
        in_specs=[_latq_spec(LAT_TQ, D_MODEL), _latkv_spec(D_MODEL), _latkv_spec(D_MODEL),
                  pl.BlockSpec((None, DIFF_HEADS, PAST_LEN, LANES), lambda b, t: (b, 0, 0, 0)),
                  pl.BlockSpec((None, None, DIFF_HEADS, PAST_LEN, LANES), lambda b, t: (b, 0, 0, 0, 0))]
                 + small_specs,
        out_specs=_lato_spec(LAT_TQ),
        out_shape=jax.ShapeDtypeStruct((N_LAT_TOK, D_MODEL), BF16),
        compiler_params=_cparams(2),
        name="diff_latent",
    )(q, k, v, cache_k_pair, cache_v, *small)
    return out_p, out_s


def _swa_pipeline(q_ref, o_ref, seq_refs, sink_ref, score_fns, value_fns, seqs=1):
    tq = q_ref.shape[0] // seqs
    per_kv = SWA_HEADS // SWA_KV_HEADS // 2
    first = lax.broadcasted_iota(jnp.int32, (LANES, tq), 0) < HALF

    def make(views, b):
        q_v, o_v = views[:2]
        kv_views = (b, views[2:])

        def scores(c):
            q = jnp.concatenate(_split_halves(_chunk(q_v, c)), axis=0)
            return [fn(kv_views, c // per_kv, q) for fn in score_fns]

        def finish(c, s_list):
            sink = jnp.concatenate([jnp.full((1, tq), sink_ref[2 * c + a] * LOG2E, F32) for a in range(2)],
                                   axis=1)
            ps, inv = _softmax2_parts(s_list, extra=sink)
            o = _pv(ps, [fn(kv_views, c // per_kv) for fn in value_fns]) * inv
            oc = jnp.where(first, o[:, :tq], o[:, tq:])
            o_v[:, c * LANES:(c + 1) * LANES] = oc.T.astype(o_v.dtype)

        return scores, finish

    _seq_pipeline((q_ref, o_ref) + tuple(seq_refs), seqs, SWA_HEADS // 2, make)


def _swa_prompt_kernel(sink_ref, q_ref, k_ref, v_ref, o_ref):
    _swa_pipeline(q_ref, o_ref, (k_ref, v_ref), sink_ref,
                  [lambda seq, kv, q: _dot_nt(_chunk(seq[1][0], kv), q)],
                  [lambda seq, kv: _chunk(seq[1][1], kv)], seqs=PROMPT_SEQS)


def _swa_latent_kernel(sink_ref, q_ref, k_ref, v_ref, kc_ref, vc_ref, o_ref):
    tq = SWA_QB
    span = SWA_QB + 2 * WINDOW
    cols = lax.broadcasted_iota(jnp.int32, (span, 2 * tq), 1)
    rows = lax.broadcasted_iota(jnp.int32, (span, 2 * tq), 0)
    starts, valids = [], []
    for b in range(SWA_BLOCKS):
        n = pl.program_id(1) * SWA_BLOCKS + b
        start = pl.multiple_of(jnp.clip(n * SWA_QB - WINDOW, 0, DEC_SEQ - span), WINDOW)
        qpos = n * SWA_QB + jnp.bitwise_and(cols, tq - 1)
        starts.append(start)
        valids.append(jnp.abs(qpos - (start + rows)) <= WINDOW)

    def local(ref, seq, kv):
        return ref[pl.ds(starts[seq[0]], span), kv * LANES:(kv + 1) * LANES]

    _swa_pipeline(q_ref, o_ref, (), sink_ref,
                  [lambda seq, kv, q: jnp.where(valids[seq[0]], _dot_nt(local(k_ref, seq, kv), q), -1e30),
                   lambda seq, kv, q: _dot_nt(kc_ref[kv], q)],
                  [lambda seq, kv: local(v_ref, seq, kv), lambda seq, kv: vc_ref[kv]], seqs=SWA_BLOCKS)


def _swa_attend(q, kd, vd, cache_kd, cache_vd, sink):
    nkd = 2 * SWA_KV_HEADS * SWA_HEAD_DIM
    smem = pl.BlockSpec(memory_space=pltpu.SMEM)
    out_p = pl.pallas_call(
        _swa_prompt_kernel,
        grid=(N_PROMPT_TILES // PROMPT_SEQS,),
        in_specs=[smem, _prompt_spec(D_MODEL), _prompt_spec(nkd), _prompt_spec(nkd)],
        out_specs=_prompt_spec(D_MODEL),
        out_shape=jax.ShapeDtypeStruct((N_PROMPT_TOK, D_MODEL), BF16),
        compiler_params=_cparams(1),
        name="swa_prompt",
    )(sink, q, kd, vd)
    ctx = pl.BlockSpec((None, SWA_KV_HEADS, PAST_LEN, LANES), lambda b, n: (b, 0, 0, 0))
    out_s = pl.pallas_call(
        _swa_latent_kernel,
        grid=(DEC_BATCH, DEC_SEQ // (SWA_QB * SWA_BLOCKS)),
        in_specs=[smem, _latq_spec(SWA_QB * SWA_BLOCKS, D_MODEL), _latkv_spec(nkd), _latkv_spec(nkd), ctx, ctx],
        out_specs=_lato_spec(SWA_QB * SWA_BLOCKS),
        out_shape=jax.ShapeDtypeStruct((N_LAT_TOK, D_MODEL), BF16),
        compiler_params=_cparams(2),
        name="swa_latent",
    )(sink, q, kd, vd, cache_kd, cache_vd)
    return out_p, out_s


def _mla_kernel(*refs, with_ctx, seqs):
    if with_ctx:
        (qn_ref, qp_ref, kn_ref, kp_ref, v_ref, knc_ref, kpc_ref, vc_ref, o_ref) = refs
    else:
        (qn_ref, qp_ref, kn_ref, kp_ref, v_ref, o_ref) = refs

    def make(views, _):
        qn_v, qp_v, kn_v, kp_v, v_v, o_v = views

        def scores(hd):
            j, a = hd // 2, hd % 2
            q = jnp.concatenate([_chunk(qn_v, hd), _split_halves(_chunk(qp_v, j))[a]], axis=1)
            s_list = [_dot_nt(jnp.concatenate([_chunk(kn_v, hd), _chunk(kp_v, j)], axis=1), q)]
            if with_ctx:
                s_list.append(_dot_nt(jnp.concatenate([_chunk(knc_ref, hd), _chunk(kpc_ref, j)], axis=1), q))
            return s_list

        def finish(hd, s_list):
            values = [_chunk(v_v, hd)]
            if with_ctx:
                values.append(_chunk(vc_ref, hd))
            ps, inv = _softmax2_parts(s_list)
            o_v[:, hd * LANES:(hd + 1) * LANES] = (_pv(ps, values) * inv).T.astype(o_v.dtype)

        return scores, finish

    _seq_pipeline((qn_ref, qp_ref, kn_ref, kp_ref, v_ref, o_ref), seqs, MLA_HEADS, make)


def _mla_attend(qn, qp, kn, kp, v, knc, kpc, vc):
    n_pe = MLA_HEADS * MLA_ROPE
    out_p = pl.pallas_call(
        functools.partial(_mla_kernel, with_ctx=False, seqs=PROMPT_SEQS),
        grid=(N_PROMPT_TILES // PROMPT_SEQS,),
        in_specs=[_prompt_spec(D_MODEL), _prompt_spec(n_pe), _prompt_spec(D_MODEL), _prompt_spec(n_pe),
                  _prompt_spec(D_MODEL)],
        out_specs=_prompt_spec(D_MODEL),
        out_shape=jax.ShapeDtypeStruct((N_PROMPT_TOK, D_MODEL), BF16),
        compiler_params=_cparams(1),
        name="mla_prompt",
    )(qn, qp, kn, kp, v)

    def ctx(width):
        return pl.BlockSpec((PAST_LEN, width), lambda b, t: (b, 0))

    out_s = pl.pallas_call(
        functools.partial(_mla_kernel, with_ctx=True, seqs=1),
        grid=(DEC_BATCH, DEC_SEQ // LAT_TQ),
        in_specs=[_latq_spec(LAT_TQ, D_MODEL), _latq_spec(LAT_TQ, n_pe),
                  _latkv_spec(D_MODEL), _latkv_spec(n_pe), _latkv_spec(D_MODEL),
                  ctx(D_MODEL), ctx(n_pe), ctx(D_MODEL)],
        out_specs=_lato_spec(LAT_TQ),
        out_shape=jax.ShapeDtypeStruct((N_LAT_TOK, D_MODEL), BF16),
        compiler_params=_cparams(2),
        name="mla_latent",
    )(qn, qp, kn, kp, v, knc, kpc, vc)
    return out_p, out_s


def _omlp_kernel(*refs, first, last):
    refs = list(refs)
    ap_ref, as_ref, wo_ref = refs[:3]
    x_refs = refs[3:5] if first else refs[3:4]
    refs = refs[3 + len(x_refs):]
    g1_ref, gain_ref, sh_ref, sc_ref, g2_ref, w1c_ref, w2c_ref = refs[:7]
    refs = refs[7:]
    if last:
        op_ref, os_ref, wo_s, w1_s, w2_s = refs
    else:
        ngain_ref, nsh_ref, nsc_ref, o_ref, hn_ref, wo_s, w1_s, w2_s = refs
    s = pl.program_id(0)
    per = MLP_FF_CHUNK // MLP_LOAD_COLS
    n_chunks = D_FF // MLP_FF_CHUNK
    half = MLP_TM // 2

    @pl.when(s == 0)
    def _():
        wo_s[...] = wo_ref[...].astype(BF16)

    for part in range(per):
        @pl.when((s < N_LOAD_STEPS) & (s % per == part))
        def _(part=part):
            w1_s[s // per, :, part * MLP_LOAD_COLS:(part + 1) * MLP_LOAD_COLS] = w1c_ref[...].astype(BF16)

    @pl.when(s < N_LOAD_STEPS)
    def _():
        w2_s[s // per, pl.ds(pl.multiple_of((s % per) * MLP_LOAD_COLS, MLP_LOAD_COLS), MLP_LOAD_COLS), :] = (
            w2c_ref[...].astype(BF16))

    @pl.when(s >= N_LOAD_STEPS)
    def _():
        t = s - N_LOAD_STEPS
        is_prompt = t < N_MLP_PROMPT_TILES
        grp = _tile_group(t, MLP_TM)

        def mod(ref):
            return ref[pl.ds(grp, 1), :]

        rows = [slice(r * half, (r + 1) * half) for r in range(2)]
        o = [_dot(jnp.where(is_prompt, ap_ref[rw, :], as_ref[rw, :]), wo_s[...]) for rw in rows]
        x1, h, u0 = [], [], []
        for r, rw in enumerate(rows):
            x = jnp.where(is_prompt, x_refs[0][rw, :], x_refs[1][rw, :]) if first else x_refs[0][rw, :]
            x1.append(x + mod(g1_ref) * o[r])
            h.append(_norm_mod(x1[r], gain_ref[...], mod(sh_ref), mod(sc_ref)).astype(BF16))
            u0.append(_dot(h[r], w1_s[0]))
        h = jnp.concatenate(h, axis=0)
        acc = []

        def up(c):
            return jnp.concatenate(u0, axis=0) if c == 0 else _dot(h, w1_s[c])

        def down(c, u):
            u = jnp.square(jnp.maximum(u, 0.0)).astype(BF16)
            if c + 1 < n_chunks:
                y = _dot(u, w2_s[c])
                acc[:] = [y if not acc else acc[0] + y]
            else:
                acc[:] = [acc[0][rw] + _dot(u[rw], w2_s[c]) for rw in rows]

        _head_pipeline(n_chunks, up, down)
        for r, rw in enumerate(rows):
            out = x1[r] + mod(g2_ref) * acc[r]
            if last:
                @pl.when(is_prompt)
                def _(out=out, rw=rw):
                    op_ref[rw, :] = out

                @pl.when(jnp.logical_not(is_prompt))
                def _(out=out, rw=rw):
                    os_ref[rw, :] = out
            else:
                o_ref[rw, :] = out
                hn_ref[rw, :] = _norm_mod(out, ngain_ref[...], mod(nsh_ref), mod(nsc_ref)).astype(BF16)


def _omlp(attn_p, attn_s, w_o, x, mods, gain_ffn, w1_all, w2_all, layer, next_gain):
    first, last = layer == 0, next_gain is None
    n_lat_tiles = N_LAT_TOK // MLP_TM

    def tok(s):
        return jnp.maximum(s - N_LOAD_STEPS, 0)

    p_spec = pl.BlockSpec((MLP_TM, D_MODEL), lambda s: (jnp.minimum(tok(s), N_MLP_PROMPT_TILES - 1), 0))
    l_spec = pl.BlockSpec((MLP_TM, D_MODEL),
                          lambda s: (jnp.clip(tok(s) - N_MLP_PROMPT_TILES, 0, n_lat_tiles - 1), 0))
    w1_spec = pl.BlockSpec((None, D_MODEL, MLP_LOAD_COLS),
                           lambda s: (layer, 0, jnp.minimum(s, N_LOAD_STEPS - 1)))
    w2_spec = pl.BlockSpec((None, MLP_LOAD_COLS, D_MODEL),
                           lambda s: (layer, jnp.minimum(s, N_LOAD_STEPS - 1), 0))
    t_spec = pl.BlockSpec((MLP_TM, D_MODEL), lambda s: (tok(s), 0))
    n_chunks = D_FF // MLP_FF_CHUNK
    split = ([p_spec, l_spec], [jax.ShapeDtypeStruct((N_PROMPT_TOK, D_MODEL), F32),
                                jax.ShapeDtypeStruct((N_LAT_TOK, D_MODEL), F32)])
    in_specs = ([p_spec, l_spec, _const_spec(w_o.shape)] + (split[0] if first else [t_spec])
                + [_mod_spec(layer, 2), gain_ffn.spec, _mod_spec(layer, 3), _mod_spec(layer, 4),
                   _mod_spec(layer, 5),
                   w1_spec, w2_spec])
    args = ([attn_p, attn_s, w_o] + (list(x) if first else [x])
            + [mods, gain_ffn.array, mods, mods, mods, w1_all, w2_all])
    if last:
        out_specs, out_shape = split
    else:
        in_specs += [next_gain.spec, _mod_spec(layer + 1, 0), _mod_spec(layer + 1, 1)]
        args += [next_gain.array, mods, mods]
        out_specs = [t_spec, t_spec]
        out_shape = [jax.ShapeDtypeStruct((N_TOK, D_MODEL), F32), jax.ShapeDtypeStruct((N_TOK, D_MODEL), BF16)]
    return pl.pallas_call(
        functools.partial(_omlp_kernel, first=first, last=last),
        grid=(N_LOAD_STEPS + N_TOK // MLP_TM,),
        in_specs=in_specs,
        out_specs=out_specs,
        out_shape=out_shape,
        scratch_shapes=[pltpu.VMEM((D_MODEL, D_MODEL), BF16),
                        pltpu.VMEM((n_chunks, D_MODEL, MLP_FF_CHUNK), BF16),
                        pltpu.VMEM((n_chunks, MLP_FF_CHUNK, D_MODEL), BF16)],
        compiler_params=_cparams(1),
        name="omlp",
    )(*args)


def kernel(x_prompt, x_sample, cache_att_k, cache_att_v, cache_diff_k, cache_diff_v, cache_swa_k, cache_swa_v, cache_mla_ckv, cache_mla_kpe, c, c_ctx, ada_w, ada_b, norm_mix, norm_ffn, att_w_qkv, att_q_norm, att_k_norm, att_w_o, diff_w_qkv, diff_q_norm, diff_k_norm, diff_lq1, diff_lk1, diff_lq2, diff_lk2, diff_subln, diff_w_o, swa_w_qkv, swa_q_norm, swa_k_norm, swa_sink, swa_w_o, mla_w_in, mla_q_a_norm, mla_kv_a_norm, mla_w_uq, mla_w_ukv, mla_q_norm, mla_k_norm, mla_w_o, mlp_w1, mlp_w2):
    xp = x_prompt.reshape(N_PROMPT_TOK, D_MODEL)
    xs = x_sample.reshape(N_LAT_TOK, D_MODEL)
    cond = jnp.concatenate([c_ctx[None], c, jnp.zeros((COND_ROWS - 1 - DEC_BATCH, D_MODEL), F32)], axis=0)
    mods_all = _modulation(cond, ada_w, ada_b)

    tab_att = _rope_tables(ATT_HEAD_DIM)
    tab_64 = _rope_tables(DIFF_HEAD_DIM)

    pk = _ParamPack()
    g_mix = [pk.add(norm_mix[l]) for l in range(DEPTH)]
    g_ffn = [pk.add(norm_ffn[l]) for l in range(DEPTH)]
    att_qg = pk.add(att_q_norm[0], PROJ_UNIT // ATT_HEAD_DIM, ATT_HEAD_DIM ** -0.5 * LOG2E)
    att_kg = pk.add(att_k_norm[0], PROJ_UNIT // ATT_HEAD_DIM)
    diff_qg = pk.add(diff_q_norm[0], PROJ_UNIT // DIFF_HEAD_DIM, DIFF_HEAD_DIM ** -0.5 * LOG2E)
    diff_kg = pk.add(diff_k_norm[0], PROJ_UNIT // DIFF_HEAD_DIM)
    diff_small = [pk.add(v[0]) for v in (diff_lq1, diff_lk1, diff_lq2, diff_lk2, diff_subln)]
    swa_qg = pk.add(swa_q_norm[0], PROJ_UNIT // SWA_HEAD_DIM, SWA_HEAD_DIM ** -0.5 * LOG2E)
    swa_kg = pk.add(swa_k_norm[0], PROJ_UNIT // SWA_HEAD_DIM)
    mla_qs = (MLA_NOPE + MLA_ROPE) ** -0.5 * LOG2E
    mla_qa = pk.add(mla_q_a_norm[0])
    mla_kva = pk.add(mla_kv_a_norm[0])
    mla_qg = pk.add(mla_q_norm[0][:MLA_NOPE], 1, mla_qs)
    mla_qgp = pk.add(mla_q_norm[0][MLA_NOPE:], LANES // MLA_ROPE, mla_qs)
    mla_kg = pk.add(mla_k_norm[0][:MLA_NOPE], PROJ_UNIT // MLA_NOPE)
    mla_kgp = pk.add(mla_k_norm[0][MLA_NOPE:], LANES // MLA_ROPE)
    pk.build()

    outs = {}
    x = (xp, xs)
    for layer in range(DEPTH):
        gain_ffn = g_ffn[layer]
        if layer == 0:
            q, k, v, outs["att_k"], outs["att_v"] = _proj_att(
                xp, xs, mods_all, g_mix[layer], att_w_qkv[0], att_qg, att_kg, tab_att)
            attn_p, attn_s = _att_attend(q, k, v, cache_att_k, cache_att_v)
            w_o = att_w_o[0]
        elif layer == 1:
            q, k, v, outs["diff_k"], outs["diff_v"] = _proj_diff(h, diff_w_qkv[0], diff_qg, diff_kg, tab_64)
            lam_init = 0.8 - 0.6 * math.exp(-0.3 * layer)
            ck = cache_diff_k[:, 0].transpose(0, 1, 3, 2, 4).reshape(
                DEC_BATCH, DIFF_HEADS, PAST_LEN, LANES)
            attn_p, attn_s = _diff_attend(q, k, v, ck, cache_diff_v, *diff_small, lam_init)
            w_o = diff_w_o[0]
        elif layer == 2:
            q, kd, vd, outs["swa_k"], outs["swa_v"] = _proj_swa(h, swa_w_qkv[0], swa_qg, swa_kg, tab_64)
            ckd = jnp.concatenate([cache_swa_k[:, 0]] * 2, axis=-1).astype(BF16)
            cvd = jnp.concatenate([cache_swa_v[:, 0]] * 2, axis=-1).astype(BF16)
            attn_p, attn_s = _swa_attend(q, kd, vd, ckd, cvd, swa_sink[0].astype(F32))
            w_o = swa_w_o[0]
        else:
            qn, qp, outs["mla_ckv"], outs["mla_kpe"], kn, kp, vv = _proj_mla(
                h, mla_w_in[0], mla_qa, mla_kva, mla_w_uq[0], mla_qg, mla_qgp,
                mla_w_ukv[0], mla_kg, mla_kgp, tab_64)
            c_ckv = cache_mla_ckv[:, 0].reshape(DEC_BATCH * PAST_LEN, MLA_KV_RANK)
            c_kpe = cache_mla_kpe[:, 0].reshape(DEC_BATCH * PAST_LEN, MLA_ROPE)
            c_kpe = jnp.concatenate([c_kpe, c_kpe], axis=-1)
            knc, kpc, vc = _mla_ctx(c_ckv, c_kpe, mla_w_ukv[0], mla_kg, mla_kgp)
            attn_p, attn_s = _mla_attend(qn, qp, kn, kp, vv, knc, kpc, vc)
            w_o = mla_w_o[0]
        if layer + 1 < DEPTH:
            x, h = _omlp(attn_p, attn_s, w_o, x, mods_all, gain_ffn, mlp_w1, mlp_w2, layer,
                         g_mix[layer + 1])
        else:
            xp, xs = _omlp(attn_p, attn_s, w_o, x, mods_all, gain_ffn, mlp_w1, mlp_w2, layer, None)

    y_prompt = xp.reshape(BATCH, SEQ, D_MODEL)
    y_sample = xs.reshape(DEC_BATCH, DEC_SEQ, D_MODEL)
    for name in ("diff_k", "swa_k", "swa_v", "mla_kpe"):
        outs[name] = jnp.swapaxes(outs[name], -1, -2)
    return (y_prompt, y_sample, outs["att_k"], outs["att_v"], outs["diff_k"], outs["diff_v"],
            outs["swa_k"], outs["swa_v"], outs["mla_ckv"], outs["mla_kpe"])
```

```python
import functools
import math

import numpy as np
import jax
import jax.numpy as jnp
from jax import lax
from jax.experimental import pallas as pl
from jax.experimental.pallas import tpu as pltpu

D_MODEL = 1024
BATCH = 16
SEQ = 256
DEPTH = 4
DEC_BATCH = 2
DEC_SEQ = 1024
PAST_LEN = 256
GRID_W = 64
ROPE_THETA = 10000.0
EPS = 1e-6
D_FF = 4 * D_MODEL
MOD_CHUNKS = 6
LOG2E = 1.4426950408889634

ATT_HEADS, ATT_KV_HEADS, ATT_HEAD_DIM = 8, 2, 128
DIFF_HEADS, DIFF_HEAD_DIM = 8, 64
SWA_HEADS, SWA_KV_HEADS, SWA_HEAD_DIM, WINDOW = 16, 4, 64, 128
MLA_HEADS, MLA_NOPE, MLA_ROPE, MLA_VDIM = 8, 128, 64, 128
MLA_Q_RANK, MLA_KV_RANK = 512, 256

LANES = 128
HALF = LANES // 2
TM = 256
N_PROMPT_TOK = BATCH * SEQ
N_LAT_TOK = DEC_BATCH * DEC_SEQ
N_TOK = N_PROMPT_TOK + N_LAT_TOK
N_PROMPT_TILES = N_PROMPT_TOK // TM
LAT_TQ = 512
LAT_BLOCK0 = N_PROMPT_TOK // DEC_SEQ
COND_ROWS = 8
PROJ_TM = 512
PROJ_BATCHES = PROJ_TM // SEQ
N_PROJ_TILES = N_TOK // PROJ_TM
N_PROJ_PROMPT = N_PROMPT_TOK // PROJ_TM
PROJ_UNIT = 2 * LANES
MLP_TM = 512
MLP_FF_CHUNK = 512
MLP_LOAD_COLS = 256
N_LOAD_STEPS = D_FF // MLP_LOAD_COLS
WO_LOAD_ROWS = D_MODEL // N_LOAD_STEPS
N_MLP_PROMPT_TILES = N_PROMPT_TOK // MLP_TM
SWA_QB = 128
SWA_BLOCKS = 4
ATT_UNIT_HEADS = 4
PROMPT_SEQS = 4
VMEM_LIMIT = 56 * 1024 * 1024

F32 = jnp.float32
BF16 = jnp.bfloat16


def _cparams(n_axes):
    return pltpu.CompilerParams(dimension_semantics=("arbitrary",) * n_axes,
                                vmem_limit_bytes=VMEM_LIMIT)


def _dot(a, b):
    return jnp.dot(a, b, preferred_element_type=F32)


def _dot_nt(a, b):
    return lax.dot_general(a, b, (((1,), (1,)), ((), ())), preferred_element_type=F32)


def _dot_tn(a, b):
    return lax.dot_general(a, b, (((0,), (0,)), ((), ())), preferred_element_type=F32)


def _const_spec(shape):
    nd = len(shape)
    return pl.BlockSpec(shape, lambda *_: (0,) * nd, pipeline_mode=pl.Buffered(1))


class _ParamPack:
    def __init__(self):
        self._rows, self.array = [], None

    def add(self, v, repeat=1, scale=1.0):
        n = v.shape[0] * repeat
        row = _ParamRow(self, -(-n // LANES) * LANES)
        self._rows.append((row, [v] * repeat, scale, n))
        return row

    def build(self):
        pieces, scales, offset = [], [], 0
        for row, vs, scale, n in sorted(self._rows, key=lambda r: -r[0].width):
            row.offset = offset
            pieces += vs + ([jnp.zeros((row.width - n,), F32)] if row.width > n else [])
            scales.append(np.full((row.width,), scale, np.float32))
            offset += row.width
        flat = jnp.concatenate([p.astype(F32) for p in pieces]) * jnp.asarray(np.concatenate(scales))
        self.array = flat.reshape(1, offset)


class _ParamRow:
    def __init__(self, pack, width):
        self.pack, self.width, self.offset = pack, width, None

    @property
    def array(self):
        return self.pack.array

    @property
    def spec(self):
        block = self.offset // self.width
        return pl.BlockSpec((1, self.width), lambda *_: (0, block), pipeline_mode=pl.Buffered(1))


def _chunk(ref, c, width=LANES):
    return ref[:, c * width:(c + 1) * width]


def _put(ref, c, val):
    ref[:, c * LANES:(c + 1) * LANES] = val.astype(ref.dtype)


def _tile_group(i, rows):
    n_prompt = N_PROMPT_TOK // rows
    return jnp.where(i < n_prompt, 0, 1 + (i - n_prompt) // (DEC_SEQ // rows))


def _rope_tile(i):
    return jnp.maximum(i - N_PROJ_PROMPT, 0) % (DEC_SEQ // PROJ_TM)


def _norm_mod(x, gain, shift, scale):
    ms = jnp.mean(x * x, axis=-1, keepdims=True)
    return x * lax.rsqrt(ms + EPS) * (gain * (1.0 + scale)) + shift


def _lane_lo(shape):
    return lax.broadcasted_iota(jnp.int32, shape, len(shape) - 1) < HALF


def _rope(y, cos, sin_prev, sin_next, quarter):
    return (y * cos + pltpu.roll(y, quarter, 1) * sin_prev
            + pltpu.roll(y, LANES - quarter, 1) * sin_next)


def _rope_tables(rot_dim):
    half = rot_dim // 2
    quarter = rot_dim // 4
    inv = np.float32(ROPE_THETA) ** (-np.arange(0, half, 2, dtype=np.float32) / np.float32(half))
    pos = np.arange(DEC_SEQ)
    row = (pos // GRID_W).astype(np.float32)
    col = (pos % GRID_W).astype(np.float32)
    lane = np.arange(LANES)
    dd = lane % rot_dim
    q = dd // quarter
    f = dd % quarter
    ang = np.where((q < 2)[None, :], row[:, None], col[:, None]) * inv[f][None, :]
    ang = ang.astype(np.float32)
    cos = np.cos(ang).astype(np.float32)
    sin = np.sin(ang).astype(np.float32)
    odd = (q % 2 == 1)[None, :]
    sin_prev = np.where(odd, sin, 0.0).astype(np.float32)
    sin_next = np.where(odd, 0.0, -sin).astype(np.float32)
    return jnp.asarray(cos), jnp.asarray(sin_prev), jnp.asarray(sin_next)


def _lane_sum_matrix(rows, cols, value=1.0):
    lane = np.arange(LANES)
    m = np.where(rows(lane)[:, None] & cols(lane)[None, :], value, 0.0).astype(np.float32)
    return jnp.asarray(m, dtype=BF16)


def _group_mean_matrix(group):
    lane = np.arange(PROJ_UNIT)
    m = np.where((lane[:, None] // group) == (lane[None, :] // group), 1.0 / group, 0.0)
    return jnp.asarray(m.astype(np.float32), dtype=BF16)


def _group_sum_matrix():
    lane = np.arange(PROJ_UNIT)
    m = np.where((lane[:, None] // LANES) == (lane[None, :] // LANES), 1.0, 0.0)
    return jnp.asarray(m.astype(np.float32), dtype=BF16)


def _sq_bf16(y):
    return (y * y).astype(BF16)


def _head_norm(y, m_ref, gain):
    return y * lax.rsqrt(_dot(_sq_bf16(y), m_ref[...]) + EPS) * gain


def _halves(y):
    return [y[:, t * LANES:(t + 1) * LANES] for t in range(y.shape[1] // LANES)]


def _matmul_units(h, w_ref, n_units, width, emit):
    def unit(u):
        return _dot(h, w_ref[:, u * width:(u + 1) * width])

    nxt = unit(0)
    for u in range(n_units):
        cur = nxt
        if u + 1 < n_units:
            nxt = unit(u + 1)
        emit(u, cur)


def _cast_once(i, w_ref, w_s):
    @pl.when(i == 0)
    def _():
        w_s[...] = w_ref[...].astype(BF16)


def _by_tile_kind(i, body):
    pl.when(i < N_PROJ_PROMPT)(functools.partial(body, False))
    pl.when(i >= N_PROJ_PROMPT)(functools.partial(body, True))


def _rope_args(lat, cos_ref, sp_ref, sn_ref, rot_dim):
    return (cos_ref[...], sp_ref[...], sn_ref[...], rot_dim // 4) if lat else None


def _maybe_rope(y, rope):
    return y if rope is None else _rope(y, *rope)


def _cache_rows(ref, index, val):
    for b in range(PROJ_BATCHES):
        ref[(b, 0) + tuple(index)] = val[b * SEQ:(b + 1) * SEQ]


def _cache_rows_t(ref, indices, val):
    for b in range(PROJ_BATCHES):
        t = val[b * SEQ:(b + 1) * SEQ].T
        for j, index in enumerate(indices):
            ref[(b, 0) + tuple(index)] = t[j * HALF:(j + 1) * HALF]


def _softmax2_parts(s_list, extra=None):
    m = jnp.max(s_list[0], axis=0, keepdims=True)
    for s in s_list[1:]:
        m = jnp.maximum(m, jnp.max(s, axis=0, keepdims=True))
    if extra is not None:
        m = jnp.maximum(m, extra)
    ps = [jnp.exp2(s - m) for s in s_list]
    mass = ps[0].sum(axis=0, keepdims=True)
    for p in ps[1:]:
        mass = mass + p.sum(axis=0, keepdims=True)
    if extra is not None:
        mass = mass + jnp.exp2(extra - m)
    return [p.astype(BF16) for p in ps], 1.0 / mass


def _head_pipeline(n, scores, finish):
    nxt = scores(0)
    for h in range(n):
        cur = nxt
        if h + 1 < n:
            nxt = scores(h + 1)
        finish(h, cur)


def _seq_pipeline(refs, seqs, n, make):
    fns = []
    for b in range(seqs):
        views = [r.at[b * (r.shape[0] // seqs):(b + 1) * (r.shape[0] // seqs)] for r in refs]
        fns.append(make(views, b))
    _head_pipeline(seqs * n, lambda i: fns[i // n][0](i % n), lambda i, s: fns[i // n][1](i % n, s))


def _pv(ps, values):
    o = None
    for p, v in zip(ps, values):
        t = _dot_tn(v, p)
        o = t if o is None else o + t
    return o


def _split_halves(q):
    lo = _lane_lo(q.shape)
    zero = jnp.zeros_like(q)
    return jnp.where(lo, q, zero), jnp.where(lo, zero, q)


def _mod_kernel(cond_ref, w_ref, b_ref, o_ref):
    c = cond_ref[...]
    s = (c * jax.nn.sigmoid(c)).astype(BF16)
    o_ref[0] = _dot(s, w_ref[0].astype(BF16)) + b_ref[0]


def _modulation(cond, ada_w, ada_b):
    tn = 3072
    n = MOD_CHUNKS * D_MODEL
    return pl.pallas_call(
        _mod_kernel,
        grid=(DEPTH, n // tn),
        in_specs=[
            pl.BlockSpec((COND_ROWS, D_MODEL), lambda l, j: (0, 0)),
            pl.BlockSpec((1, D_MODEL, tn), lambda l, j: (l, 0, j)),
            pl.BlockSpec((1, 1, tn), lambda l, j: (l, 0, j)),
        ],
        out_specs=pl.BlockSpec((1, COND_ROWS, tn), lambda l, j: (l, 0, j)),
        out_shape=jax.ShapeDtypeStruct((DEPTH, COND_ROWS, n), F32),
        compiler_params=_cparams(2),
        name="modulation",
    )(cond, ada_w, ada_b.reshape(DEPTH, 1, n))


def _mod_spec(layer, chunk):
    return pl.BlockSpec((None, COND_ROWS, D_MODEL), lambda i: (layer, 0, chunk))


def _mod_row(ref, i):
    return ref[pl.ds(_tile_group(i, PROJ_TM), 1), :]


_ROPE_SPEC = pl.BlockSpec((PROJ_TM, LANES), lambda i: (_rope_tile(i), 0))
_LANE_MAT_SPEC = _const_spec((LANES, LANES))
_UNIT_MAT_SPEC = _const_spec((PROJ_UNIT, PROJ_UNIT))


def _tok_spec(width):
    return pl.BlockSpec((PROJ_TM, width), lambda i: (i, 0))


_XP_SPEC = pl.BlockSpec((PROJ_TM, D_MODEL), lambda i: (jnp.minimum(i, N_PROJ_PROMPT - 1), 0))
_XS_SPEC = pl.BlockSpec((PROJ_TM, D_MODEL), lambda i: (jnp.maximum(i - N_PROJ_PROMPT, 0), 0))


def _cache_spec(*dims):
    nd = len(dims)
    return pl.BlockSpec((PROJ_BATCHES, 1) + dims,
                        lambda i: (jnp.minimum(i, N_PROJ_PROMPT - 1), 0) + (0,) * nd)


def _cache_shape(*dims):
    return jax.ShapeDtypeStruct((BATCH, 1) + dims, F32)


def _proj_att_kernel(xp_ref, xs_ref, gain_ref, sh_ref, sc_ref, w_ref, qg_ref, kg_ref, m_ref,
                     cos_ref, sp_ref, sn_ref, q_ref, k_ref, v_ref, ck_ref, cv_ref, w_s):
    i = pl.program_id(0)
    _cast_once(i, w_ref, w_s)
    x = jnp.where(i < N_PROJ_PROMPT, xp_ref[...], xs_ref[...])
    h = _norm_mod(x, gain_ref[...], _mod_row(sh_ref, i), _mod_row(sc_ref, i)).astype(BF16)
    per = PROJ_UNIT // LANES
    nq, nk = ATT_HEADS // per, ATT_KV_HEADS // per

    def body(lat):
        rope = _rope_args(lat, cos_ref, sp_ref, sn_ref, ATT_HEAD_DIM)

        def emit(u, y):
            if u < nq + nk:
                y = _head_norm(y, m_ref, qg_ref[...] if u < nq else kg_ref[...])
            for t, yc in enumerate(_halves(y)):
                if u < nq:
                    _put(q_ref, u * per + t, _maybe_rope(yc, rope))
                elif u < nq + nk:
                    kn = _maybe_rope(yc, rope)
                    _put(k_ref, (u - nq) * per + t, kn)
                    if not lat:
                        _cache_rows(ck_ref, [(u - nq) * per + t], kn)
                else:
                    _put(v_ref, (u - nq - nk) * per + t, yc)
                    if not lat:
                        _cache_rows(cv_ref, [(u - nq - nk) * per + t], yc)

        _matmul_units(h, w_s, nq + 2 * nk, PROJ_UNIT, emit)

    _by_tile_kind(i, body)


def _proj_att(xp, xs, mods, gain, w, qg, kg, tables):
    nq, nk = ATT_HEADS * ATT_HEAD_DIM, ATT_KV_HEADS * ATT_HEAD_DIM
    return pl.pallas_call(
        _proj_att_kernel,
        grid=(N_PROJ_TILES,),
        in_specs=[_XP_SPEC, _XS_SPEC, gain.spec, _mod_spec(0, 0), _mod_spec(0, 1),
                  _const_spec(w.shape), qg.spec, kg.spec,
                  _UNIT_MAT_SPEC, _ROPE_SPEC, _ROPE_SPEC, _ROPE_SPEC],
        out_specs=[_tok_spec(nq), _tok_spec(nk), _tok_spec(nk),
                   _cache_spec(ATT_KV_HEADS, SEQ, ATT_HEAD_DIM), _cache_spec(ATT_KV_HEADS, SEQ, ATT_HEAD_DIM)],
        out_shape=[jax.ShapeDtypeStruct((N_TOK, nq), BF16),
                   jax.ShapeDtypeStruct((N_TOK, nk), BF16),
                   jax.ShapeDtypeStruct((N_TOK, nk), BF16),
                   _cache_shape(ATT_KV_HEADS, SEQ, ATT_HEAD_DIM), _cache_shape(ATT_KV_HEADS, SEQ, ATT_HEAD_DIM)],
        scratch_shapes=[pltpu.VMEM(w.shape, BF16)],
        compiler_params=_cparams(1),
        name="proj_att",
    )(xp, xs, gain.array, mods, mods, w, qg.array, kg.array, _group_mean_matrix(ATT_HEAD_DIM), *tables)


def _proj_diff_kernel(h_ref, w_ref, qg_ref, kg_ref, m_ref,
                      cos_ref, sp_ref, sn_ref, q_ref, k_ref, v_ref, ck_ref, cv_ref, w_s):
    i = pl.program_id(0)
    _cast_once(i, w_ref, w_s)
    h = h_ref[...]
    per = PROJ_UNIT // LANES
    nu = DIFF_HEADS // per

    def body(lat):
        rope = _rope_args(lat, cos_ref, sp_ref, sn_ref, DIFF_HEAD_DIM)

        def emit(u, y):
            if u < 2 * nu:
                y = _head_norm(y, m_ref, qg_ref[...] if u < nu else kg_ref[...])
            for t, yc in enumerate(_halves(y)):
                hd = (u % nu) * per + t
                if u < nu:
                    _put(q_ref, hd, _maybe_rope(yc, rope))
                elif u < 2 * nu:
                    kn = _maybe_rope(yc, rope)
                    _put(k_ref, hd, kn)
                    if not lat:
                        _cache_rows_t(ck_ref, [[hd, 0], [hd, 1]], kn)
                else:
                    _put(v_ref, hd, yc)
                    if not lat:
                        _cache_rows(cv_ref, [hd], yc)

        _matmul_units(h, w_s, 3 * nu, PROJ_UNIT, emit)

    _by_tile_kind(i, body)


def _proj_diff(h, w, qg, kg, tables):
    n = DIFF_HEADS * 2 * DIFF_HEAD_DIM
    return pl.pallas_call(
        _proj_diff_kernel,
        grid=(N_PROJ_TILES,),
        in_specs=[_tok_spec(D_MODEL), _const_spec(w.shape), qg.spec, kg.spec,
                  _UNIT_MAT_SPEC, _ROPE_SPEC, _ROPE_SPEC, _ROPE_SPEC],
        out_specs=[_tok_spec(n), _tok_spec(n), _tok_spec(n),
                   _cache_spec(DIFF_HEADS, 2, DIFF_HEAD_DIM, SEQ), _cache_spec(DIFF_HEADS, SEQ, 2 * DIFF_HEAD_DIM)],
        out_shape=[jax.ShapeDtypeStruct((N_TOK, n), BF16)] * 3
                  + [_cache_shape(DIFF_HEADS, 2, DIFF_HEAD_DIM, SEQ),
                     _cache_shape(DIFF_HEADS, SEQ, 2 * DIFF_HEAD_DIM)],
        scratch_shapes=[pltpu.VMEM(w.shape, BF16)],
        compiler_params=_cparams(1),
        name="proj_diff",
    )(h, w, qg.array, kg.array, _group_mean_matrix(DIFF_HEAD_DIM), *tables)


def _dup_halves(yc):
    lo = _lane_lo(yc.shape)
    sw = pltpu.roll(yc, HALF, 1)
    return jnp.where(lo, yc, sw), jnp.where(lo, sw, yc)


def _proj_swa_kernel(h_ref, w_ref, qg_ref, kg_ref, m_ref,
                     cos_ref, sp_ref, sn_ref, q_ref, kd_ref, vd_ref, ck_ref, cv_ref, w_s):
    i = pl.program_id(0)
    _cast_once(i, w_ref, w_s)
    h = h_ref[...]
    per = PROJ_UNIT // LANES
    nq = SWA_HEADS * SWA_HEAD_DIM // PROJ_UNIT
    nk = SWA_KV_HEADS * SWA_HEAD_DIM // PROJ_UNIT

    def body(lat):
        rope = _rope_args(lat, cos_ref, sp_ref, sn_ref, SWA_HEAD_DIM)

        def emit(u, y):
            if u < nq + nk:
                y = _head_norm(y, m_ref, qg_ref[...] if u < nq else kg_ref[...])
            for t, yc in enumerate(_halves(y)):
                if u < nq:
                    _put(q_ref, u * per + t, _maybe_rope(yc, rope))
                    continue
                if u < nq + nk:
                    j, c_ref, d_ref = (u - nq) * per + t, ck_ref, kd_ref
                    yc = _maybe_rope(yc, rope)
                else:
                    j, c_ref, d_ref = (u - nq - nk) * per + t, cv_ref, vd_ref
                for a, dup in enumerate(_dup_halves(yc)):
                    _put(d_ref, 2 * j + a, dup)
                if not lat:
                    _cache_rows_t(c_ref, [[2 * j], [2 * j + 1]], yc)

        _matmul_units(h, w_s, nq + 2 * nk, PROJ_UNIT, emit)

    _by_tile_kind(i, body)


def _proj_swa(h, w, qg, kg, tables):
    nq, nk = SWA_HEADS * SWA_HEAD_DIM, SWA_KV_HEADS * SWA_HEAD_DIM
    return pl.pallas_call(
        _proj_swa_kernel,
        grid=(N_PROJ_TILES,),
        in_specs=[_tok_spec(D_MODEL), _const_spec(w.shape), qg.spec, kg.spec,
                  _UNIT_MAT_SPEC, _ROPE_SPEC, _ROPE_SPEC, _ROPE_SPEC],
        out_specs=[_tok_spec(nq), _tok_spec(2 * nk), _tok_spec(2 * nk),
                   _cache_spec(SWA_KV_HEADS, SWA_HEAD_DIM, SEQ), _cache_spec(SWA_KV_HEADS, SWA_HEAD_DIM, SEQ)],
        out_shape=[jax.ShapeDtypeStruct((N_TOK, nq), BF16),
                   jax.ShapeDtypeStruct((N_TOK, 2 * nk), BF16),
                   jax.ShapeDtypeStruct((N_TOK, 2 * nk), BF16),
                   _cache_shape(SWA_KV_HEADS, SWA_HEAD_DIM, SEQ), _cache_shape(SWA_KV_HEADS, SWA_HEAD_DIM, SEQ)],
        scratch_shapes=[pltpu.VMEM(w.shape, BF16)],
        compiler_params=_cparams(1),
        name="proj_swa",
    )(h, w, qg.array, kg.array, _group_mean_matrix(SWA_HEAD_DIM), *tables)


def _mla_lane_matrices():
    everything = lambda lane: lane >= 0
    lo = _lane_sum_matrix(lambda lane: lane < HALF, everything)
    hi = _lane_sum_matrix(lambda lane: lane >= HALF, everything)
    return jnp.concatenate([lo, hi], axis=1), lo


def _mla_keys(ckv, kpe, w_s, kg_ref, kgp_ref, sum_ref, lo_ref, rope, kn_ref, kp_ref, v_ref):
    pe_ss = _dot(_sq_bf16(kpe), lo_ref[...])
    pe_ss = jnp.concatenate([pe_ss, pe_ss], axis=1)
    lo = _lane_lo(kpe.shape)
    inv_d = 1.0 / (MLA_NOPE + MLA_ROPE)

    def emit(j, y):
        kn = jnp.concatenate([y[:, :LANES], y[:, 2 * LANES:3 * LANES]], axis=1)
        r = lax.rsqrt((_dot(_sq_bf16(kn), sum_ref[...]) + pe_ss) * inv_d + EPS)
        kn = kn * r * kg_ref[...]
        for a in range(2):
            _put(kn_ref, 2 * j + a, kn[:, a * LANES:(a + 1) * LANES])
            _put(v_ref, 2 * j + a, y[:, (2 * a + 1) * LANES:(2 * a + 2) * LANES])
        pe = kpe * jnp.where(lo, r[:, :LANES], r[:, LANES:]) * kgp_ref[...]
        _put(kp_ref, j, _maybe_rope(pe, rope))

    _matmul_units(ckv, w_s, MLA_HEADS // 2, 4 * LANES, emit)


def _proj_mla_kernel(h_ref, w_in_ref, qa_ref, kva_ref, w_uq_ref, qg_ref, qgp_ref, lohi_ref, lo_ref,
                     w_ukv_ref, kg_ref, kgp_ref, sum_ref, cos_ref, sp_ref, sn_ref,
                     qn_ref, qp_ref, c_ckv_ref, c_kpe_ref, kn_ref, kp_ref, v_ref,
                     w_in_s, w_uq_s, w_ukv_s):
    i = pl.program_id(0)
    n_in = MLA_Q_RANK + MLA_KV_RANK + MLA_ROPE

    @pl.when(i == 0)
    def _():
        w_in_s[...] = jnp.zeros_like(w_in_s)
        w_in_s[:, :n_in] = w_in_ref[...].astype(BF16)
        w_uq_s[...] = w_uq_ref[...].astype(BF16)
        w_ukv_s[...] = w_ukv_ref[...].astype(BF16)

    y = _dot(h_ref[...], w_in_s[...])
    c_q = y[:, :MLA_Q_RANK]
    c_kv = y[:, MLA_Q_RANK:MLA_Q_RANK + MLA_KV_RANK]
    kpe = y[:, MLA_Q_RANK + MLA_KV_RANK:]
    kpe = kpe + pltpu.roll(kpe, HALF, 1)
    ckv = c_kv * lax.rsqrt(jnp.mean(c_kv * c_kv, axis=-1, keepdims=True) + EPS) * kva_ref[...]
    cq = (c_q * lax.rsqrt(jnp.mean(c_q * c_q, axis=-1, keepdims=True) + EPS) * qa_ref[...]).astype(BF16)
    lo = _lane_lo((PROJ_TM, LANES))
    inv_d = 1.0 / (MLA_NOPE + MLA_ROPE)

    def body(lat):
        rope = _rope_args(lat, cos_ref, sp_ref, sn_ref, MLA_ROPE)
        if not lat:
            _cache_rows(c_ckv_ref, [], ckv)
            _cache_rows_t(c_kpe_ref, [[]], kpe)

        def emit(j, yq):
            y0, y1, y2 = _halves(yq)
            nopes = (y0, jnp.where(lo, pltpu.roll(y1, HALF, 1), pltpu.roll(y2, HALF, 1)))
            pe = jnp.where(lo, y1, y2)
            ss = (_dot(_sq_bf16(jnp.concatenate(nopes, axis=1)), sum_ref[...])
                  + _dot(_sq_bf16(pe), lohi_ref[...]))
            rs = _halves(lax.rsqrt(ss * inv_d + EPS))
            for a in range(2):
                _put(qn_ref, 2 * j + a, nopes[a] * rs[a] * qg_ref[...])
            _put(qp_ref, j, _maybe_rope(pe * jnp.where(lo, rs[0], rs[1]) * qgp_ref[...], rope))

        _matmul_units(cq, w_uq_s, MLA_HEADS // 2, 3 * LANES, emit)
        _mla_keys(ckv.astype(BF16), kpe, w_ukv_s, kg_ref, kgp_ref, sum_ref, lo_ref, rope,
                  kn_ref, kp_ref, v_ref)

    _by_tile_kind(i, body)


def _proj_mla(h, w_in, qa, kva, w_uq, qg, qgp, w_ukv, kg, kgp, tables):
    n_nope = MLA_HEADS * MLA_NOPE
    n_pe = MLA_HEADS * MLA_ROPE
    n_in = -(-w_in.shape[1] // LANES) * LANES
    return pl.pallas_call(
        _proj_mla_kernel,
        grid=(N_PROJ_TILES,),
        in_specs=[_tok_spec(D_MODEL),
                  _const_spec(w_in.shape), qa.spec, kva.spec,
                  _const_spec(w_uq.shape), qg.spec, qgp.spec,
                  _const_spec((LANES, PROJ_UNIT)), _LANE_MAT_SPEC,
                  _const_spec(w_ukv.shape), kg.spec, kgp.spec, _UNIT_MAT_SPEC,
                  _ROPE_SPEC, _ROPE_SPEC, _ROPE_SPEC],
        out_specs=[_tok_spec(n_nope), _tok_spec(n_pe),
                   _cache_spec(SEQ, MLA_KV_RANK), _cache_spec(MLA_ROPE, SEQ),
                   _tok_spec(n_nope), _tok_spec(n_pe), _tok_spec(n_nope)],
        out_shape=[jax.ShapeDtypeStruct((N_TOK, n_nope), BF16),
                   jax.ShapeDtypeStruct((N_TOK, n_pe), BF16),
                   _cache_shape(SEQ, MLA_KV_RANK), _cache_shape(MLA_ROPE, SEQ),
                   jax.ShapeDtypeStruct((N_TOK, n_nope), BF16),
                   jax.ShapeDtypeStruct((N_TOK, n_pe), BF16),
                   jax.ShapeDtypeStruct((N_TOK, n_nope), BF16)],
        scratch_shapes=[pltpu.VMEM((D_MODEL, n_in), BF16), pltpu.VMEM(w_uq.shape, BF16),
                        pltpu.VMEM(w_ukv.shape, BF16)],
        compiler_params=_cparams(1),
        name="proj_mla",
    )(h, w_in, qa.array, kva.array, w_uq, qg.array, qgp.array, *_mla_lane_matrices(),
      w_ukv, kg.array, kgp.array, _group_sum_matrix(), *tables)


def _mla_ctx_kernel(ckv_ref, kpe_ref, w_ref, kg_ref, kgp_ref, sum_ref, lo_ref, kn_ref, kp_ref, v_ref):
    _mla_keys(ckv_ref[...].astype(BF16), kpe_ref[...], w_ref[...].astype(BF16), kg_ref, kgp_ref,
              sum_ref, lo_ref, None, kn_ref, kp_ref, v_ref)


def _mla_ctx(ckv, kpe_dup, w_ukv, kg, kgp):
    n = ckv.shape[0]
    n_nope = MLA_HEADS * MLA_NOPE
    n_pe = MLA_HEADS * MLA_ROPE
    _, m_lo = _mla_lane_matrices()
    return pl.pallas_call(
        _mla_ctx_kernel,
        grid=(n // PROJ_TM,),
        in_specs=[_tok_spec(MLA_KV_RANK), _tok_spec(LANES), _const_spec(w_ukv.shape),
                  kg.spec, kgp.spec, _UNIT_MAT_SPEC, _LANE_MAT_SPEC],
        out_specs=[_tok_spec(n_nope), _tok_spec(n_pe), _tok_spec(n_nope)],
        out_shape=[jax.ShapeDtypeStruct((n, n_nope), BF16),
                   jax.ShapeDtypeStruct((n, n_pe), BF16),
                   jax.ShapeDtypeStruct((n, n_nope), BF16)],
        compiler_params=_cparams(1),
        name="mla_ctx",
    )(ckv, kpe_dup, w_ukv, kg.array, kgp.array, _group_sum_matrix(), m_lo)


def _prompt_spec(width):
    return pl.BlockSpec((PROMPT_SEQS * TM, width), lambda b: (b, 0))


def _latq_spec(rows, width):
    per = DEC_SEQ // rows
    return pl.BlockSpec((rows, width), lambda b, t: (N_PROMPT_TOK // rows + b * per + t, 0))


def _latkv_spec(width):
    return pl.BlockSpec((DEC_SEQ, width), lambda b, t: (LAT_BLOCK0 + b, 0))


def _lato_spec(rows):
    per = DEC_SEQ // rows
    return pl.BlockSpec((rows, D_MODEL), lambda b, t: (b * per + t, 0))


def _att_kernel(*refs, with_ctx, seqs):
    if with_ctx:
        q_ref, k_ref, v_ref, kc_ref, vc_ref, o_ref = refs
    else:
        q_ref, k_ref, v_ref, o_ref = refs
    tq = q_ref.shape[0] // seqs
    nu = ATT_UNIT_HEADS
    per_kv = ATT_HEADS // ATT_KV_HEADS // nu

    def make(views, _):
        q_v, k_v, v_v, o_v = views

        def scores(u):
            q = jnp.concatenate([_chunk(q_v, u * nu + g) for g in range(nu)], axis=0)
            s_list = [_dot_nt(_chunk(k_v, u // per_kv), q)]
            if with_ctx:
                s_list.append(_dot_nt(kc_ref[u // per_kv].astype(BF16), q))
            return s_list

        def finish(u, s_list):
            values = [_chunk(v_v, u // per_kv)]
            if with_ctx:
                values.append(vc_ref[u // per_kv].astype(BF16))
            ps, inv = _softmax2_parts(s_list)
            o = _pv(ps, values) * inv
            for g in range(nu):
                o_v[:, (u * nu + g) * LANES:(u * nu + g + 1) * LANES] = (
                    o[:, g * tq:(g + 1) * tq].T.astype(o_v.dtype))

        return scores, finish

    _seq_pipeline((q_ref, k_ref, v_ref, o_ref), seqs, ATT_HEADS // nu, make)


def _att_attend(q, k, v, cache_k, cache_v):
    nk = ATT_KV_HEADS * ATT_HEAD_DIM
    out_p = pl.pallas_call(
        functools.partial(_att_kernel, with_ctx=False, seqs=PROMPT_SEQS),
        grid=(N_PROMPT_TILES // PROMPT_SEQS,),
        in_specs=[_prompt_spec(D_MODEL), _prompt_spec(nk), _prompt_spec(nk)],
        out_specs=_prompt_spec(D_MODEL),
        out_shape=jax.ShapeDtypeStruct((N_PROMPT_TOK, D_MODEL), BF16),
        compiler_params=_cparams(1),
        name="att_prompt",
    )(q, k, v)
    ctx = pl.BlockSpec((None, None, ATT_KV_HEADS, PAST_LEN, LANES), lambda b, t: (b, 0, 0, 0, 0))
    out_s = pl.pallas_call(
        functools.partial(_att_kernel, with_ctx=True, seqs=1),
        grid=(DEC_BATCH, DEC_SEQ // LAT_TQ),
        in_specs=[_latq_spec(LAT_TQ, D_MODEL), _latkv_spec(nk), _latkv_spec(nk), ctx, ctx],
        out_specs=_lato_spec(LAT_TQ),
        out_shape=jax.ShapeDtypeStruct((N_LAT_TOK, D_MODEL), BF16),
        compiler_params=_cparams(2),
        name="att_latent",
    )(q, k, v, cache_k, cache_v)
    return out_p, out_s


def _diff_kernel(*refs, lam_init, with_ctx, seqs):
    if with_ctx:
        (q_ref, k_ref, v_ref, kc_ref, vc_ref, lq1_ref, lk1_ref, lq2_ref, lk2_ref, sub_ref, o_ref) = refs
    else:
        (q_ref, k_ref, v_ref, lq1_ref, lk1_ref, lq2_ref, lk2_ref, sub_ref, o_ref) = refs
    tq = q_ref.shape[0] // seqs
    lam = (jnp.exp(jnp.sum(lq1_ref[...] * lk1_ref[...], axis=-1, keepdims=True))
           - jnp.exp(jnp.sum(lq2_ref[...] * lk2_ref[...], axis=-1, keepdims=True)) + lam_init)
    diag = (lax.broadcasted_iota(jnp.int32, (LANES, LANES), 0)
            == lax.broadcasted_iota(jnp.int32, (LANES, LANES), 1))
    sub = jnp.sum(jnp.where(diag, sub_ref[...] * (1.0 - lam_init), 0.0), axis=1, keepdims=True)

    def make(views, _):
        q_v, k_v, v_v, o_v = views

        def scores(hd):
            q = jnp.concatenate(_split_halves(_chunk(q_v, hd)), axis=0)
            s_list = [_dot_nt(_chunk(k_v, hd), q)]
            if with_ctx:
                s_list.append(_dot_nt(kc_ref[hd].astype(BF16), q))
            return s_list

        def finish(hd, s_list):
            values = [_chunk(v_v, hd)]
            if with_ctx:
                values.append(vc_ref[hd].astype(BF16))
            ps, inv = _softmax2_parts(s_list)
            o = (_pv([p[:, :tq] for p in ps], values) * inv[:, :tq]
                 - _pv([p[:, tq:] for p in ps], values) * (lam * inv[:, tq:]))
            o = o * lax.rsqrt(jnp.mean(o * o, axis=0, keepdims=True) + EPS) * sub
            o_v[:, hd * LANES:(hd + 1) * LANES] = o.T.astype(o_v.dtype)

        return scores, finish

    _seq_pipeline((q_ref, k_ref, v_ref, o_ref), seqs, DIFF_HEADS, make)


def _diff_attend(q, k, v, cache_k_pair, cache_v, lq1, lk1, lq2, lk2, subln, lam_init):
    small_specs = [p.spec for p in (lq1, lk1, lq2, lk2, subln)]
    small = [p.array for p in (lq1, lk1, lq2, lk2, subln)]
    out_p = pl.pallas_call(
        functools.partial(_diff_kernel, lam_init=lam_init, with_ctx=False, seqs=PROMPT_SEQS),
        grid=(N_PROMPT_TILES // PROMPT_SEQS,),
        in_specs=[_prompt_spec(D_MODEL)] * 3 + small_specs,
        out_specs=_prompt_spec(D_MODEL),
        out_shape=jax.ShapeDtypeStruct((N_PROMPT_TOK, D_MODEL), BF16),
        compiler_params=_cparams(1),
        name="diff_prompt",
    )(q, k, v, *small)
    out_s = pl.pallas_call(
        functools.partial(_diff_kernel, lam_init=lam_init, with_ctx=True, seqs=1),
        grid=(DEC_BATCH, DEC_SEQ // LAT_TQ),
        in_specs=[_latq_spec(LAT_TQ, D_MODEL), _latkv_spec(D_MODEL), _latkv_spec(D_MODEL),
                  pl.BlockSpec((None, DIFF_HEADS, PAST_LEN, LANES), lambda b, t: (b, 0, 0, 0)),
                  pl.BlockSpec((None, None, DIFF_HEADS, PAST_LEN, LANES), lambda b, t: (b, 0, 0, 0, 0))]
                 + small_specs,
        out_specs=_lato_spec(LAT_TQ),
        out_shape=jax.ShapeDtypeStruct((N_LAT_TOK, D_MODEL), BF16),
        compiler_params=_cparams(2),
        name="diff_latent",
    )(q, k, v, cache_k_pair, cache_v, *small)
    return out_p, out_s


def _swa_pipeline(q_ref, o_ref, seq_refs, sink_ref, score_fns, value_fns, seqs=1):
    tq = q_ref.shape[0] // seqs
    per_kv = SWA_HEADS // SWA_KV_HEADS // 2
    first = lax.broadcasted_iota(jnp.int32, (LANES, tq), 0) < HALF

    def make(views, b):
        q_v, o_v = views[:2]
        kv_views = (b, views[2:])

        def scores(c):
            q = jnp.concatenate(_split_halves(_chunk(q_v, c)), axis=0)
            return [fn(kv_views, c // per_kv, q) for fn in score_fns]

        def finish(c, s_list):
            sink = jnp.concatenate([jnp.full((1, tq), sink_ref[2 * c + a] * LOG2E, F32) for a in range(2)],
                                   axis=1)
            ps, inv = _softmax2_parts(s_list, extra=sink)
            o = _pv(ps, [fn(kv_views, c // per_kv) for fn in value_fns]) * inv
            oc = jnp.where(first, o[:, :tq], o[:, tq:])
            o_v[:, c * LANES:(c + 1) * LANES] = oc.T.astype(o_v.dtype)

        return scores, finish

    _seq_pipeline((q_ref, o_ref) + tuple(seq_refs), seqs, SWA_HEADS // 2, make)


def _swa_prompt_kernel(sink_ref, q_ref, k_ref, v_ref, o_ref):
    _swa_pipeline(q_ref, o_ref, (k_ref, v_ref), sink_ref,
                  [lambda seq, kv, q: _dot_nt(_chunk(seq[1][0], kv), q)],
                  [lambda seq, kv: _chunk(seq[1][1], kv)], seqs=PROMPT_SEQS)


def _swa_latent_kernel(sink_ref, q_ref, k_ref, v_ref, kc_ref, vc_ref, o_ref):
    tq = SWA_QB
    span = SWA_QB + 2 * WINDOW
    cols = lax.broadcasted_iota(jnp.int32, (span, 2 * tq), 1)
    rows = lax.broadcasted_iota(jnp.int32, (span, 2 * tq), 0)
    starts, valids = [], []
    for b in range(SWA_BLOCKS):
        n = pl.program_id(1) * SWA_BLOCKS + b
        start = pl.multiple_of(jnp.clip(n * SWA_QB - WINDOW, 0, DEC_SEQ - span), WINDOW)
        qpos = n * SWA_QB + jnp.bitwise_and(cols, tq - 1)
        starts.append(start)
        valids.append(jnp.abs(qpos - (start + rows)) <= WINDOW)

    def local(ref, seq, kv):
        return ref[pl.ds(starts[seq[0]], span), kv * LANES:(kv + 1) * LANES]

    _swa_pipeline(q_ref, o_ref, (), sink_ref,
                  [lambda seq, kv, q: jnp.where(valids[seq[0]], _dot_nt(local(k_ref, seq, kv), q), -1e30),
                   lambda seq, kv, q: _dot_nt(kc_ref[kv], q)],
                  [lambda seq, kv: local(v_ref, seq, kv), lambda seq, kv: vc_ref[kv]], seqs=SWA_BLOCKS)


def _swa_attend(q, kd, vd, cache_kd, cache_vd, sink):
    nkd = 2 * SWA_KV_HEADS * SWA_HEAD_DIM
    smem = pl.BlockSpec(memory_space=pltpu.SMEM)
    out_p = pl.pallas_call(
        _swa_prompt_kernel,
        grid=(N_PROMPT_TILES // PROMPT_SEQS,),
        in_specs=[smem, _prompt_spec(D_MODEL), _prompt_spec(nkd), _prompt_spec(nkd)],
        out_specs=_prompt_spec(D_MODEL),
        out_shape=jax.ShapeDtypeStruct((N_PROMPT_TOK, D_MODEL), BF16),
        compiler_params=_cparams(1),
        name="swa_prompt",
    )(sink, q, kd, vd)
    ctx = pl.BlockSpec((None, SWA_KV_HEADS, PAST_LEN, LANES), lambda b, n: (b, 0, 0, 0))
    out_s = pl.pallas_call(
        _swa_latent_kernel,
        grid=(DEC_BATCH, DEC_SEQ // (SWA_QB * SWA_BLOCKS)),
        in_specs=[smem, _latq_spec(SWA_QB * SWA_BLOCKS, D_MODEL), _latkv_spec(nkd), _latkv_spec(nkd), ctx, ctx],
        out_specs=_lato_spec(SWA_QB * SWA_BLOCKS),
        out_shape=jax.ShapeDtypeStruct((N_LAT_TOK, D_MODEL), BF16),
        compiler_params=_cparams(2),
        name="swa_latent",
    )(sink, q, kd, vd, cache_kd, cache_vd)
    return out_p, out_s


def _mla_kernel(*refs, with_ctx, seqs):
    if with_ctx:
        (qn_ref, qp_ref, kn_ref, kp_ref, v_ref, knc_ref, kpc_ref, vc_ref, o_ref) = refs
    else:
        (qn_ref, qp_ref, kn_ref, kp_ref, v_ref, o_ref) = refs

    def make(views, _):
        qn_v, qp_v, kn_v, kp_v, v_v, o_v = views

        def scores(hd):
            j, a = hd // 2, hd % 2
            q = jnp.concatenate([_chunk(qn_v, hd), _split_halves(_chunk(qp_v, j))[a]], axis=1)
            s_list = [_dot_nt(jnp.concatenate([_chunk(kn_v, hd), _chunk(kp_v, j)], axis=1), q)]
            if with_ctx:
                s_list.append(_dot_nt(jnp.concatenate([_chunk(knc_ref, hd), _chunk(kpc_ref, j)], axis=1), q))
            return s_list

        def finish(hd, s_list):
            values = [_chunk(v_v, hd)]
            if with_ctx:
                values.append(_chunk(vc_ref, hd))
            ps, inv = _softmax2_parts(s_list)
            o_v[:, hd * LANES:(hd + 1) * LANES] = (_pv(ps, values) * inv).T.astype(o_v.dtype)

        return scores, finish

    _seq_pipeline((qn_ref, qp_ref, kn_ref, kp_ref, v_ref, o_ref), seqs, MLA_HEADS, make)


def _mla_attend(qn, qp, kn, kp, v, knc, kpc, vc):
    n_pe = MLA_HEADS * MLA_ROPE
    out_p = pl.pallas_call(
        functools.partial(_mla_kernel, with_ctx=False, seqs=PROMPT_SEQS),
        grid=(N_PROMPT_TILES // PROMPT_SEQS,),
        in_specs=[_prompt_spec(D_MODEL), _prompt_spec(n_pe), _prompt_spec(D_MODEL), _prompt_spec(n_pe),
                  _prompt_spec(D_MODEL)],
        out_specs=_prompt_spec(D_MODEL),
        out_shape=jax.ShapeDtypeStruct((N_PROMPT_TOK, D_MODEL), BF16),
        compiler_params=_cparams(1),
        name="mla_prompt",
    )(qn, qp, kn, kp, v)

    def ctx(width):
        return pl.BlockSpec((PAST_LEN, width), lambda b, t: (b, 0))

    out_s = pl.pallas_call(
        functools.partial(_mla_kernel, with_ctx=True, seqs=1),
        grid=(DEC_BATCH, DEC_SEQ // LAT_TQ),
        in_specs=[_latq_spec(LAT_TQ, D_MODEL), _latq_spec(LAT_TQ, n_pe),
                  _latkv_spec(D_MODEL), _latkv_spec(n_pe), _latkv_spec(D_MODEL),
                  ctx(D_MODEL), ctx(n_pe), ctx(D_MODEL)],
        out_specs=_lato_spec(LAT_TQ),
        out_shape=jax.ShapeDtypeStruct((N_LAT_TOK, D_MODEL), BF16),
        compiler_params=_cparams(2),
        name="mla_latent",
    )(qn, qp, kn, kp, v, knc, kpc, vc)
    return out_p, out_s


def _omlp_kernel(*refs, first, last):
    refs = list(refs)
    ap_ref, as_ref, wo_ref = refs[:3]
    x_refs = refs[3:5] if first else refs[3:4]
    refs = refs[3 + len(x_refs):]
    g1_ref, gain_ref, sh_ref, sc_ref, g2_ref, w1c_ref, w2c_ref = refs[:7]
    refs = refs[7:]
    if last:
        op_ref, os_ref, wo_s, w1_s, w2_s = refs
    else:
        ngain_ref, nsh_ref, nsc_ref, o_ref, hn_ref, wo_s, w1_s, w2_s = refs
    s = pl.program_id(0)
    per = MLP_FF_CHUNK // MLP_LOAD_COLS
    n_chunks = D_FF // MLP_FF_CHUNK
    half = MLP_TM // 2

    for part in range(per):
        @pl.when((s < N_LOAD_STEPS) & (s % per == part))
        def _(part=part):
            w1_s[s // per, :, part * MLP_LOAD_COLS:(part + 1) * MLP_LOAD_COLS] = w1c_ref[...].astype(BF16)

    @pl.when(s < N_LOAD_STEPS)
    def _():
        w2_s[s // per, pl.ds(pl.multiple_of((s % per) * MLP_LOAD_COLS, MLP_LOAD_COLS), MLP_LOAD_COLS), :] = (
            w2c_ref[...].astype(BF16))
        wo_s[pl.ds(pl.multiple_of(s * WO_LOAD_ROWS, WO_LOAD_ROWS), WO_LOAD_ROWS), :] = wo_ref[...].astype(BF16)

    @pl.when(s >= N_LOAD_STEPS)
    def _():
        t = s - N_LOAD_STEPS
        is_prompt = t < N_MLP_PROMPT_TILES
        grp = _tile_group(t, MLP_TM)

        def mod(ref):
            return ref[pl.ds(grp, 1), :]

        rows = [slice(r * half, (r + 1) * half) for r in range(2)]
        o = [_dot(jnp.where(is_prompt, ap_ref[rw, :], as_ref[rw, :]), wo_s[...]) for rw in rows]
        x1, h, u0 = [], [], []
        for r, rw in enumerate(rows):
            x = jnp.where(is_prompt, x_refs[0][rw, :], x_refs[1][rw, :]) if first else x_refs[0][rw, :]
            x1.append(x + mod(g1_ref) * o[r])
            h.append(_norm_mod(x1[r], gain_ref[...], mod(sh_ref), mod(sc_ref)).astype(BF16))
            u0.append(_dot(h[r], w1_s[0]))
        h = jnp.concatenate(h, axis=0)
        acc = []

        def up(c):
            return jnp.concatenate(u0, axis=0) if c == 0 else _dot(h, w1_s[c])

        def down(c, u):
            u = jnp.square(jnp.maximum(u, 0.0)).astype(BF16)
            if c + 1 < n_chunks:
                y = _dot(u, w2_s[c])
                acc[:] = [y if not acc else acc[0] + y]
            else:
                acc[:] = [acc[0][rw] + _dot(u[rw], w2_s[c]) for rw in rows]

        _head_pipeline(n_chunks, up, down)
        for r, rw in enumerate(rows):
            out = x1[r] + mod(g2_ref) * acc[r]
            if last:
                @pl.when(is_prompt)
                def _(out=out, rw=rw):
                    op_ref[rw, :] = out

                @pl.when(jnp.logical_not(is_prompt))
                def _(out=out, rw=rw):
                    os_ref[rw, :] = out
            else:
                o_ref[rw, :] = out
                hn_ref[rw, :] = _norm_mod(out, ngain_ref[...], mod(nsh_ref), mod(nsc_ref)).astype(BF16)


def _omlp(attn_p, attn_s, w_o, x, mods, gain_ffn, w1_all, w2_all, layer, next_gain):
    first, last = layer == 0, next_gain is None
    n_lat_tiles = N_LAT_TOK // MLP_TM

    def tok(s):
        return jnp.maximum(s - N_LOAD_STEPS, 0)

    p_spec = pl.BlockSpec((MLP_TM, D_MODEL), lambda s: (jnp.minimum(tok(s), N_MLP_PROMPT_TILES - 1), 0))
    l_spec = pl.BlockSpec((MLP_TM, D_MODEL),
                          lambda s: (jnp.clip(tok(s) - N_MLP_PROMPT_TILES, 0, n_lat_tiles - 1), 0))
    w1_spec = pl.BlockSpec((None, D_MODEL, MLP_LOAD_COLS),
                           lambda s: (layer, 0, jnp.minimum(s, N_LOAD_STEPS - 1)))
    w2_spec = pl.BlockSpec((None, MLP_LOAD_COLS, D_MODEL),
                           lambda s: (layer, jnp.minimum(s, N_LOAD_STEPS - 1), 0))
    t_spec = pl.BlockSpec((MLP_TM, D_MODEL), lambda s: (tok(s), 0))
    n_chunks = D_FF // MLP_FF_CHUNK
    split = ([p_spec, l_spec], [jax.ShapeDtypeStruct((N_PROMPT_TOK, D_MODEL), F32),
                                jax.ShapeDtypeStruct((N_LAT_TOK, D_MODEL), F32)])
    wo_spec = pl.BlockSpec((WO_LOAD_ROWS, D_MODEL), lambda s: (jnp.minimum(s, N_LOAD_STEPS - 1), 0))
    in_specs = ([p_spec, l_spec, wo_spec] + (split[0] if first else [t_spec])
                + [_mod_spec(layer, 2), gain_ffn.spec, _mod_spec(layer, 3), _mod_spec(layer, 4),
                   _mod_spec(layer, 5),
                   w1_spec, w2_spec])
    args = ([attn_p, attn_s, w_o] + (list(x) if first else [x])
            + [mods, gain_ffn.array, mods, mods, mods, w1_all, w2_all])
    if last:
        out_specs, out_shape = split
    else:
        in_specs += [next_gain.spec, _mod_spec(layer + 1, 0), _mod_spec(layer + 1, 1)]
        args += [next_gain.array, mods, mods]
        out_specs = [t_spec, t_spec]
        out_shape = [jax.ShapeDtypeStruct((N_TOK, D_MODEL), F32), jax.ShapeDtypeStruct((N_TOK, D_MODEL), BF16)]
    return pl.pallas_call(
        functools.partial(_omlp_kernel, first=first, last=last),
        grid=(N_LOAD_STEPS + N_TOK // MLP_TM,),
        in_specs=in_specs,
        out_specs=out_specs,
        out_shape=out_shape,
        scratch_shapes=[pltpu.VMEM((D_MODEL, D_MODEL), BF16),
                        pltpu.VMEM((n_chunks, D_MODEL, MLP_FF_CHUNK), BF16),
                        pltpu.VMEM((n_chunks, MLP_FF_CHUNK, D_MODEL), BF16)],
        compiler_params=_cparams(1),
        name="omlp",
    )(*args)


def kernel(x_prompt, x_sample, cache_att_k, cache_att_v, cache_diff_k, cache_diff_v, cache_swa_k, cache_swa_v, cache_mla_ckv, cache_mla_kpe, c, c_ctx, ada_w, ada_b, norm_mix, norm_ffn, att_w_qkv, att_q_norm, att_k_norm, att_w_o, diff_w_qkv, diff_q_norm, diff_k_norm, diff_lq1, diff_lk1, diff_lq2, diff_lk2, diff_subln, diff_w_o, swa_w_qkv, swa_q_norm, swa_k_norm, swa_sink, swa_w_o, mla_w_in, mla_q_a_norm, mla_kv_a_norm, mla_w_uq, mla_w_ukv, mla_q_norm, mla_k_norm, mla_w_o, mlp_w1, mlp_w2):
    xp = x_prompt.reshape(N_PROMPT_TOK, D_MODEL)
    xs = x_sample.reshape(N_LAT_TOK, D_MODEL)
    cond = jnp.concatenate([c_ctx[None], c, jnp.zeros((COND_ROWS - 1 - DEC_BATCH, D_MODEL), F32)], axis=0)
    mods_all = _modulation(cond, ada_w, ada_b)

    tab_att = _rope_tables(ATT_HEAD_DIM)
    tab_64 = _rope_tables(DIFF_HEAD_DIM)

    pk = _ParamPack()
    g_mix = [pk.add(norm_mix[l]) for l in range(DEPTH)]
    g_ffn = [pk.add(norm_ffn[l]) for l in range(DEPTH)]
    att_qg = pk.add(att_q_norm[0], PROJ_UNIT // ATT_HEAD_DIM, ATT_HEAD_DIM ** -0.5 * LOG2E)
    att_kg = pk.add(att_k_norm[0], PROJ_UNIT // ATT_HEAD_DIM)
    diff_qg = pk.add(diff_q_norm[0], PROJ_UNIT // DIFF_HEAD_DIM, DIFF_HEAD_DIM ** -0.5 * LOG2E)
    diff_kg = pk.add(diff_k_norm[0], PROJ_UNIT // DIFF_HEAD_DIM)
    diff_small = [pk.add(v[0]) for v in (diff_lq1, diff_lk1, diff_lq2, diff_lk2, diff_subln)]
    swa_qg = pk.add(swa_q_norm[0], PROJ_UNIT // SWA_HEAD_DIM, SWA_HEAD_DIM ** -0.5 * LOG2E)
    swa_kg = pk.add(swa_k_norm[0], PROJ_UNIT // SWA_HEAD_DIM)
    mla_qs = (MLA_NOPE + MLA_ROPE) ** -0.5 * LOG2E
    mla_qa = pk.add(mla_q_a_norm[0])
    mla_kva = pk.add(mla_kv_a_norm[0])
    mla_qg = pk.add(mla_q_norm[0][:MLA_NOPE], 1, mla_qs)
    mla_qgp = pk.add(mla_q_norm[0][MLA_NOPE:], LANES // MLA_ROPE, mla_qs)
    mla_kg = pk.add(mla_k_norm[0][:MLA_NOPE], PROJ_UNIT // MLA_NOPE)
    mla_kgp = pk.add(mla_k_norm[0][MLA_NOPE:], LANES // MLA_ROPE)
    pk.build()

    outs = {}
    x = (xp, xs)
    for layer in range(DEPTH):
        gain_ffn = g_ffn[layer]
        if layer == 0:
            q, k, v, outs["att_k"], outs["att_v"] = _proj_att(
                xp, xs, mods_all, g_mix[layer], att_w_qkv[0], att_qg, att_kg, tab_att)
            attn_p, attn_s = _att_attend(q, k, v, cache_att_k, cache_att_v)
            w_o = att_w_o[0]
        elif layer == 1:
            q, k, v, outs["diff_k"], outs["diff_v"] = _proj_diff(h, diff_w_qkv[0], diff_qg, diff_kg, tab_64)
            lam_init = 0.8 - 0.6 * math.exp(-0.3 * layer)
            ck = cache_diff_k[:, 0].transpose(0, 1, 3, 2, 4).reshape(
                DEC_BATCH, DIFF_HEADS, PAST_LEN, LANES)
            attn_p, attn_s = _diff_attend(q, k, v, ck, cache_diff_v, *diff_small, lam_init)
            w_o = diff_w_o[0]
        elif layer == 2:
            q, kd, vd, outs["swa_k"], outs["swa_v"] = _proj_swa(h, swa_w_qkv[0], swa_qg, swa_kg, tab_64)
            ckd = jnp.concatenate([cache_swa_k[:, 0]] * 2, axis=-1).astype(BF16)
            cvd = jnp.concatenate([cache_swa_v[:, 0]] * 2, axis=-1).astype(BF16)
            attn_p, attn_s = _swa_attend(q, kd, vd, ckd, cvd, swa_sink[0].astype(F32))
            w_o = swa_w_o[0]
        else:
            qn, qp, outs["mla_ckv"], outs["mla_kpe"], kn, kp, vv = _proj_mla(
                h, mla_w_in[0], mla_qa, mla_kva, mla_w_uq[0], mla_qg, mla_qgp,
                mla_w_ukv[0], mla_kg, mla_kgp, tab_64)
            c_ckv = cache_mla_ckv[:, 0].reshape(DEC_BATCH * PAST_LEN, MLA_KV_RANK)
            c_kpe = cache_mla_kpe[:, 0].reshape(DEC_BATCH * PAST_LEN, MLA_ROPE)
            c_kpe = jnp.concatenate([c_kpe, c_kpe], axis=-1)
            knc, kpc, vc = _mla_ctx(c_ckv, c_kpe, mla_w_ukv[0], mla_kg, mla_kgp)
            attn_p, attn_s = _mla_attend(qn, qp, kn, kp, vv, knc, kpc, vc)
            w_o = mla_w_o[0]
        if layer + 1 < DEPTH:
            x, h = _omlp(attn_p, attn_s, w_o, x, mods_all, gain_ffn, mlp_w1, mlp_w2, layer,
                         g_mix[layer + 1])
        else:
            xp, xs = _omlp(attn_p, attn_s, w_o, x, mods_all, gain_ffn, mlp_w1, mlp_w2, layer, None)

    y_prompt = xp.reshape(BATCH, SEQ, D_MODEL)
    y_sample = xs.reshape(DEC_BATCH, DEC_SEQ, D_MODEL)
    for name in ("diff_k", "swa_k", "swa_v", "mla_kpe"):
        outs[name] = jnp.swapaxes(outs[name], -1, -2)
    return (y_prompt, y_sample, outs["att_k"], outs["att_v"], outs["diff_k"], outs["diff_v"],
            outs["swa_k"], outs["swa_v"], outs["mla_ckv"], outs["mla_kpe"])
```

```python
import functools
import math

import numpy as np
import jax
import jax.numpy as jnp
from jax import lax
from jax.experimental import pallas as pl
from jax.experimental.pallas import tpu as pltpu

D_MODEL = 1024
BATCH = 16
SEQ = 256
DEPTH = 4
DEC_BATCH = 2
DEC_SEQ = 1024
PAST_LEN = 256
GRID_W = 64
ROPE_THETA = 10000.0
EPS = 1e-6
D_FF = 4 * D_MODEL
MOD_CHUNKS = 6
LOG2E = 1.4426950408889634

ATT_HEADS, ATT_KV_HEADS, ATT_HEAD_DIM = 8, 2, 128
DIFF_HEADS, DIFF_HEAD_DIM = 8, 64
SWA_HEADS, SWA_KV_HEADS, SWA_HEAD_DIM, WINDOW = 16, 4, 64, 128
MLA_HEADS, MLA_NOPE, MLA_ROPE, MLA_VDIM = 8, 128, 64, 128
MLA_Q_RANK, MLA_KV_RANK = 512, 256

LANES = 128
HALF = LANES // 2
TM = 256
N_PROMPT_TOK = BATCH * SEQ
N_LAT_TOK = DEC_BATCH * DEC_SEQ
N_TOK = N_PROMPT_TOK + N_LAT_TOK
N_PROMPT_TILES = N_PROMPT_TOK // TM
LAT_TQ = 512
LAT_BLOCK0 = N_PROMPT_TOK // DEC_SEQ
COND_ROWS = 8
PROJ_TM = 512
PROJ_BATCHES = PROJ_TM // SEQ
N_PROJ_TILES = N_TOK // PROJ_TM
N_PROJ_PROMPT = N_PROMPT_TOK // PROJ_TM
PROJ_UNIT = 2 * LANES
MLP_TM = 512
MLP_FF_CHUNK = 512
MLP_LOAD_COLS = 512
N_LOAD_STEPS = D_FF // MLP_LOAD_COLS
WO_LOAD_ROWS = D_MODEL // N_LOAD_STEPS
N_MLP_PROMPT_TILES = N_PROMPT_TOK // MLP_TM
SWA_QB = 128
SWA_BLOCKS = 4
ATT_UNIT_HEADS = 4
PROMPT_SEQS = 4
VMEM_LIMIT = 56 * 1024 * 1024

F32 = jnp.float32
BF16 = jnp.bfloat16


def _cparams(n_axes):
    return pltpu.CompilerParams(dimension_semantics=("arbitrary",) * n_axes,
                                vmem_limit_bytes=VMEM_LIMIT)


def _dot(a, b):
    return jnp.dot(a, b, preferred_element_type=F32)


def _dot_nt(a, b):
    return lax.dot_general(a, b, (((1,), (1,)), ((), ())), preferred_element_type=F32)


def _dot_tn(a, b):
    return lax.dot_general(a, b, (((0,), (0,)), ((), ())), preferred_element_type=F32)


def _const_spec(shape):
    nd = len(shape)
    return pl.BlockSpec(shape, lambda *_: (0,) * nd, pipeline_mode=pl.Buffered(1))


class _ParamPack:
    def __init__(self):
        self._rows, self.array = [], None

    def add(self, v, repeat=1, scale=1.0):
        n = v.shape[0] * repeat
        row = _ParamRow(self, -(-n // LANES) * LANES)
        self._rows.append((row, [v] * repeat, scale, n))
        return row

    def build(self):
        pieces, scales, offset = [], [], 0
        for row, vs, scale, n in sorted(self._rows, key=lambda r: -r[0].width):
            row.offset = offset
            pieces += vs + ([jnp.zeros((row.width - n,), F32)] if row.width > n else [])
            scales.append(np.full((row.width,), scale, np.float32))
            offset += row.width
        flat = jnp.concatenate([p.astype(F32) for p in pieces]) * jnp.asarray(np.concatenate(scales))
        self.array = flat.reshape(1, offset)


class _ParamRow:
    def __init__(self, pack, width):
        self.pack, self.width, self.offset = pack, width, None

    @property
    def array(self):
        return self.pack.array

    @property
    def spec(self):
        block = self.offset // self.width
        return pl.BlockSpec((1, self.width), lambda *_: (0, block), pipeline_mode=pl.Buffered(1))


def _chunk(ref, c, width=LANES):
    return ref[:, c * width:(c + 1) * width]


def _put(ref, c, val):
    ref[:, c * LANES:(c + 1) * LANES] = val.astype(ref.dtype)


def _tile_group(i, rows):
    n_prompt = N_PROMPT_TOK // rows
    return jnp.where(i < n_prompt, 0, 1 + (i - n_prompt) // (DEC_SEQ // rows))


def _rope_tile(i):
    return jnp.maximum(i - N_PROJ_PROMPT, 0) % (DEC_SEQ // PROJ_TM)


def _norm_mod(x, gain, shift, scale):
    ms = jnp.mean(x * x, axis=-1, keepdims=True)
    return x * lax.rsqrt(ms + EPS) * (gain * (1.0 + scale)) + shift


def _lane_lo(shape):
    return lax.broadcasted_iota(jnp.int32, shape, len(shape) - 1) < HALF


def _rope(y, cos, sin_prev, sin_next, quarter):
    return (y * cos + pltpu.roll(y, quarter, 1) * sin_prev
            + pltpu.roll(y, LANES - quarter, 1) * sin_next)


def _rope_tables(rot_dim):
    half = rot_dim // 2
    quarter = rot_dim // 4
    inv = np.float32(ROPE_THETA) ** (-np.arange(0, half, 2, dtype=np.float32) / np.float32(half))
    pos = np.arange(DEC_SEQ)
    row = (pos // GRID_W).astype(np.float32)
    col = (pos % GRID_W).astype(np.float32)
    lane = np.arange(LANES)
    dd = lane % rot_dim
    q = dd // quarter
    f = dd % quarter
    ang = np.where((q < 2)[None, :], row[:, None], col[:, None]) * inv[f][None, :]
    ang = ang.astype(np.float32)
    cos = np.cos(ang).astype(np.float32)
    sin = np.sin(ang).astype(np.float32)
    odd = (q % 2 == 1)[None, :]
    sin_prev = np.where(odd, sin, 0.0).astype(np.float32)
    sin_next = np.where(odd, 0.0, -sin).astype(np.float32)
    return jnp.asarray(cos), jnp.asarray(sin_prev), jnp.asarray(sin_next)


def _lane_sum_matrix(rows, cols, value=1.0):
    lane = np.arange(LANES)
    m = np.where(rows(lane)[:, None] & cols(lane)[None, :], value, 0.0).astype(np.float32)
    return jnp.asarray(m, dtype=BF16)


def _group_mean_matrix(group):
    lane = np.arange(PROJ_UNIT)
    m = np.where((lane[:, None] // group) == (lane[None, :] // group), 1.0 / group, 0.0)
    return jnp.asarray(m.astype(np.float32), dtype=BF16)


def _group_sum_matrix():
    lane = np.arange(PROJ_UNIT)
    m = np.where((lane[:, None] // LANES) == (lane[None, :] // LANES), 1.0, 0.0)
    return jnp.asarray(m.astype(np.float32), dtype=BF16)


def _sq_bf16(y):
    return (y * y).astype(BF16)


def _head_norm(y, m_ref, gain):
    return y * lax.rsqrt(_dot(_sq_bf16(y), m_ref[...]) + EPS) * gain


def _halves(y):
    return [y[:, t * LANES:(t + 1) * LANES] for t in range(y.shape[1] // LANES)]


def _matmul_units(h, w_ref, n_units, width, emit):
    def unit(u):
        return _dot(h, w_ref[:, u * width:(u + 1) * width])

    nxt = unit(0)
    for u in range(n_units):
        cur = nxt
        if u + 1 < n_units:
            nxt = unit(u + 1)
        emit(u, cur)


def _cast_once(i, w_ref, w_s):
    @pl.when(i == 0)
    def _():
        w_s[...] = w_ref[...].astype(BF16)


def _by_tile_kind(i, body):
    pl.when(i < N_PROJ_PROMPT)(functools.partial(body, False))
    pl.when(i >= N_PROJ_PROMPT)(functools.partial(body, True))


def _rope_args(lat, cos_ref, sp_ref, sn_ref, rot_dim):
    return (cos_ref[...], sp_ref[...], sn_ref[...], rot_dim // 4) if lat else None


def _maybe_rope(y, rope):
    return y if rope is None else _rope(y, *rope)


def _cache_rows(ref, index, val):
    for b in range(PROJ_BATCHES):
        ref[(b, 0) + tuple(index)] = val[b * SEQ:(b + 1) * SEQ]


def _cache_rows_t(ref, indices, val):
    for b in range(PROJ_BATCHES):
        t = val[b * SEQ:(b + 1) * SEQ].T
        for j, index in enumerate(indices):
            ref[(b, 0) + tuple(index)] = t[j * HALF:(j + 1) * HALF]


def _softmax2_parts(s_list, extra=None):
    m = jnp.max(s_list[0], axis=0, keepdims=True)
    for s in s_list[1:]:
        m = jnp.maximum(m, jnp.max(s, axis=0, keepdims=True))
    if extra is not None:
        m = jnp.maximum(m, extra)
    ps = [jnp.exp2(s - m) for s in s_list]
    mass = ps[0].sum(axis=0, keepdims=True)
    for p in ps[1:]:
        mass = mass + p.sum(axis=0, keepdims=True)
    if extra is not None:
        mass = mass + jnp.exp2(extra - m)
    return [p.astype(BF16) for p in ps], 1.0 / mass


def _head_pipeline(n, scores, finish):
    nxt = scores(0)
    for h in range(n):
        cur = nxt
        if h + 1 < n:
            nxt = scores(h + 1)
        finish(h, cur)


def _seq_pipeline(refs, seqs, n, make):
    fns = []
    for b in range(seqs):
        views = [r.at[b * (r.shape[0] // seqs):(b + 1) * (r.shape[0] // seqs)] for r in refs]
        fns.append(make(views, b))
    _head_pipeline(seqs * n, lambda i: fns[i // n][0](i % n), lambda i, s: fns[i // n][1](i % n, s))


def _pv(ps, values):
    o = None
    for p, v in zip(ps, values):
        t = _dot_tn(v, p)
        o = t if o is None else o + t
    return o


def _split_halves(q):
    lo = _lane_lo(q.shape)
    zero = jnp.zeros_like(q)
    return jnp.where(lo, q, zero), jnp.where(lo, zero, q)


def _mod_kernel(cond_ref, w_ref, b_ref, o_ref):
    c = cond_ref[...]
    s = (c * jax.nn.sigmoid(c)).astype(BF16)
    o_ref[0] = _dot(s, w_ref[0].astype(BF16)) + b_ref[0]


def _modulation(cond, ada_w, ada_b):
    tn = 3072
    n = MOD_CHUNKS * D_MODEL
    return pl.pallas_call(
        _mod_kernel,
        grid=(DEPTH, n // tn),
        in_specs=[
            pl.BlockSpec((COND_ROWS, D_MODEL), lambda l, j: (0, 0)),
            pl.BlockSpec((1, D_MODEL, tn), lambda l, j: (l, 0, j)),
            pl.BlockSpec((1, 1, tn), lambda l, j: (l, 0, j)),
        ],
        out_specs=pl.BlockSpec((1, COND_ROWS, tn), lambda l, j: (l, 0, j)),
        out_shape=jax.ShapeDtypeStruct((DEPTH, COND_ROWS, n), F32),
        compiler_params=_cparams(2),
        name="modulation",
    )(cond, ada_w, ada_b.reshape(DEPTH, 1, n))


def _mod_spec(layer, chunk):
    return pl.BlockSpec((None, COND_ROWS, D_MODEL), lambda i: (layer, 0, chunk))


def _mod_row(ref, i):
    return ref[pl.ds(_tile_group(i, PROJ_TM), 1), :]


_ROPE_SPEC = pl.BlockSpec((PROJ_TM, LANES), lambda i: (_rope_tile(i), 0))
_LANE_MAT_SPEC = _const_spec((LANES, LANES))
_UNIT_MAT_SPEC = _const_spec((PROJ_UNIT, PROJ_UNIT))


def _tok_spec(width):
    return pl.BlockSpec((PROJ_TM, width), lambda i: (i, 0))


_XP_SPEC = pl.BlockSpec((PROJ_TM, D_MODEL), lambda i: (jnp.minimum(i, N_PROJ_PROMPT - 1), 0))
_XS_SPEC = pl.BlockSpec((PROJ_TM, D_MODEL), lambda i: (jnp.maximum(i - N_PROJ_PROMPT, 0), 0))


def _cache_spec(*dims):
    nd = len(dims)
    return pl.BlockSpec((PROJ_BATCHES, 1) + dims,
                        lambda i: (jnp.minimum(i, N_PROJ_PROMPT - 1), 0) + (0,) * nd)


def _cache_shape(*dims):
    return jax.ShapeDtypeStruct((BATCH, 1) + dims, F32)


def _proj_att_kernel(xp_ref, xs_ref, gain_ref, sh_ref, sc_ref, w_ref, qg_ref, kg_ref, m_ref,
                     cos_ref, sp_ref, sn_ref, q_ref, k_ref, v_ref, ck_ref, cv_ref, w_s):
    i = pl.program_id(0)
    _cast_once(i, w_ref, w_s)
    x = jnp.where(i < N_PROJ_PROMPT, xp_ref[...], xs_ref[...])
    h = _norm_mod(x, gain_ref[...], _mod_row(sh_ref, i), _mod_row(sc_ref, i)).astype(BF16)
    per = PROJ_UNIT // LANES
    nq, nk = ATT_HEADS // per, ATT_KV_HEADS // per

    def body(lat):
        rope = _rope_args(lat, cos_ref, sp_ref, sn_ref, ATT_HEAD_DIM)

        def emit(u, y):
            if u < nq + nk:
                y = _head_norm(y, m_ref, qg_ref[...] if u < nq else kg_ref[...])
            for t, yc in enumerate(_halves(y)):
                if u < nq:
                    _put(q_ref, u * per + t, _maybe_rope(yc, rope))
                elif u < nq + nk:
                    kn = _maybe_rope(yc, rope)
                    _put(k_ref, (u - nq) * per + t, kn)
                    if not lat:
                        _cache_rows(ck_ref, [(u - nq) * per + t], kn)
                else:
                    _put(v_ref, (u - nq - nk) * per + t, yc)
                    if not lat:
                        _cache_rows(cv_ref, [(u - nq - nk) * per + t], yc)

        _matmul_units(h, w_s, nq + 2 * nk, PROJ_UNIT, emit)

    _by_tile_kind(i, body)


def _proj_att(xp, xs, mods, gain, w, qg, kg, tables):
    nq, nk = ATT_HEADS * ATT_HEAD_DIM, ATT_KV_HEADS * ATT_HEAD_DIM
    return pl.pallas_call(
        _proj_att_kernel,
        grid=(N_PROJ_TILES,),
        in_specs=[_XP_SPEC, _XS_SPEC, gain.spec, _mod_spec(0, 0), _mod_spec(0, 1),
                  _const_spec(w.shape), qg.spec, kg.spec,
                  _UNIT_MAT_SPEC, _ROPE_SPEC, _ROPE_SPEC, _ROPE_SPEC],
        out_specs=[_tok_spec(nq), _tok_spec(nk), _tok_spec(nk),
                   _cache_spec(ATT_KV_HEADS, SEQ, ATT_HEAD_DIM), _cache_spec(ATT_KV_HEADS, SEQ, ATT_HEAD_DIM)],
        out_shape=[jax.ShapeDtypeStruct((N_TOK, nq), BF16),
                   jax.ShapeDtypeStruct((N_TOK, nk), BF16),
                   jax.ShapeDtypeStruct((N_TOK, nk), BF16),
                   _cache_shape(ATT_KV_HEADS, SEQ, ATT_HEAD_DIM), _cache_shape(ATT_KV_HEADS, SEQ, ATT_HEAD_DIM)],
        scratch_shapes=[pltpu.VMEM(w.shape, BF16)],
        compiler_params=_cparams(1),
        name="proj_att",
    )(xp, xs, gain.array, mods, mods, w, qg.array, kg.array, _group_mean_matrix(ATT_HEAD_DIM), *tables)


def _proj_diff_kernel(h_ref, w_ref, qg_ref, kg_ref, m_ref,
                      cos_ref, sp_ref, sn_ref, q_ref, k_ref, v_ref, ck_ref, cv_ref, w_s):
    i = pl.program_id(0)
    _cast_once(i, w_ref, w_s)
    h = h_ref[...]
    per = PROJ_UNIT // LANES
    nu = DIFF_HEADS // per

    def body(lat):
        rope = _rope_args(lat, cos_ref, sp_ref, sn_ref, DIFF_HEAD_DIM)

        def emit(u, y):
            if u < 2 * nu:
                y = _head_norm(y, m_ref, qg_ref[...] if u < nu else kg_ref[...])
            for t, yc in enumerate(_halves(y)):
                hd = (u % nu) * per + t
                if u < nu:
                    _put(q_ref, hd, _maybe_rope(yc, rope))
                elif u < 2 * nu:
                    kn = _maybe_rope(yc, rope)
                    _put(k_ref, hd, kn)
                    if not lat:
                        _cache_rows_t(ck_ref, [[hd, 0], [hd, 1]], kn)
                else:
                    _put(v_ref, hd, yc)
                    if not lat:
                        _cache_rows(cv_ref, [hd], yc)

        _matmul_units(h, w_s, 3 * nu, PROJ_UNIT, emit)

    _by_tile_kind(i, body)


def _proj_diff(h, w, qg, kg, tables):
    n = DIFF_HEADS * 2 * DIFF_HEAD_DIM
    return pl.pallas_call(
        _proj_diff_kernel,
        grid=(N_PROJ_TILES,),
        in_specs=[_tok_spec(D_MODEL), _const_spec(w.shape), qg.spec, kg.spec,
                  _UNIT_MAT_SPEC, _ROPE_SPEC, _ROPE_SPEC, _ROPE_SPEC],
        out_specs=[_tok_spec(n), _tok_spec(n), _tok_spec(n),
                   _cache_spec(DIFF_HEADS, 2, DIFF_HEAD_DIM, SEQ), _cache_spec(DIFF_HEADS, SEQ, 2 * DIFF_HEAD_DIM)],
        out_shape=[jax.ShapeDtypeStruct((N_TOK, n), BF16)] * 3
                  + [_cache_shape(DIFF_HEADS, 2, DIFF_HEAD_DIM, SEQ),
                     _cache_shape(DIFF_HEADS, SEQ, 2 * DIFF_HEAD_DIM)],
        scratch_shapes=[pltpu.VMEM(w.shape, BF16)],
        compiler_params=_cparams(1),
        name="proj_diff",
    )(h, w, qg.array, kg.array, _group_mean_matrix(DIFF_HEAD_DIM), *tables)


def _dup_halves(yc):
    lo = _lane_lo(yc.shape)
    sw = pltpu.roll(yc, HALF, 1)
    return jnp.where(lo, yc, sw), jnp.where(lo, sw, yc)


def _proj_swa_kernel(h_ref, w_ref, qg_ref, kg_ref, m_ref,
                     cos_ref, sp_ref, sn_ref, q_ref, kd_ref, vd_ref, ck_ref, cv_ref, w_s):
    i = pl.program_id(0)
    _cast_once(i, w_ref, w_s)
    h = h_ref[...]
    per = PROJ_UNIT // LANES
    nq = SWA_HEADS * SWA_HEAD_DIM // PROJ_UNIT
    nk = SWA_KV_HEADS * SWA_HEAD_DIM // PROJ_UNIT

    def body(lat):
        rope = _rope_args(lat, cos_ref, sp_ref, sn_ref, SWA_HEAD_DIM)

        def emit(u, y):
            if u < nq + nk:
                y = _head_norm(y, m_ref, qg_ref[...] if u < nq else kg_ref[...])
            for t, yc in enumerate(_halves(y)):
                if u < nq:
                    _put(q_ref, u * per + t, _maybe_rope(yc, rope))
                    continue
                if u < nq + nk:
                    j, c_ref, d_ref = (u - nq) * per + t, ck_ref, kd_ref
                    yc = _maybe_rope(yc, rope)
                else:
                    j, c_ref, d_ref = (u - nq - nk) * per + t, cv_ref, vd_ref
                for a, dup in enumerate(_dup_halves(yc)):
                    _put(d_ref, 2 * j + a, dup)
                if not lat:
                    _cache_rows_t(c_ref, [[2 * j], [2 * j + 1]], yc)

        _matmul_units(h, w_s, nq + 2 * nk, PROJ_UNIT, emit)

    _by_tile_kind(i, body)


def _proj_swa(h, w, qg, kg, tables):
    nq, nk = SWA_HEADS * SWA_HEAD_DIM, SWA_KV_HEADS * SWA_HEAD_DIM
    return pl.pallas_call(
        _proj_swa_kernel,
        grid=(N_PROJ_TILES,),
        in_specs=[_tok_spec(D_MODEL), _const_spec(w.shape), qg.spec, kg.spec,
                  _UNIT_MAT_SPEC, _ROPE_SPEC, _ROPE_SPEC, _ROPE_SPEC],
        out_specs=[_tok_spec(nq), _tok_spec(2 * nk), _tok_spec(2 * nk),
                   _cache_spec(SWA_KV_HEADS, SWA_HEAD_DIM, SEQ), _cache_spec(SWA_KV_HEADS, SWA_HEAD_DIM, SEQ)],
        out_shape=[jax.ShapeDtypeStruct((N_TOK, nq), BF16),
                   jax.ShapeDtypeStruct((N_TOK, 2 * nk), BF16),
                   jax.ShapeDtypeStruct((N_TOK, 2 * nk), BF16),
                   _cache_shape(SWA_KV_HEADS, SWA_HEAD_DIM, SEQ), _cache_shape(SWA_KV_HEADS, SWA_HEAD_DIM, SEQ)],
        scratch_shapes=[pltpu.VMEM(w.shape, BF16)],
        compiler_params=_cparams(1),
        name="proj_swa",
    )(h, w, qg.array, kg.array, _group_mean_matrix(SWA_HEAD_DIM), *tables)


def _mla_lane_matrices():
    everything = lambda lane: lane >= 0
    lo = _lane_sum_matrix(lambda lane: lane < HALF, everything)
    hi = _lane_sum_matrix(lambda lane: lane >= HALF, everything)
    return jnp.concatenate([lo, hi], axis=1), lo


def _mla_keys(ckv, kpe, w_s, kg_ref, kgp_ref, sum_ref, lo_ref, rope, kn_ref, kp_ref, v_ref):
    pe_ss = _dot(_sq_bf16(kpe), lo_ref[...])
    pe_ss = jnp.concatenate([pe_ss, pe_ss], axis=1)
    lo = _lane_lo(kpe.shape)
    inv_d = 1.0 / (MLA_NOPE + MLA_ROPE)

    def emit(j, y):
        kn = jnp.concatenate([y[:, :LANES], y[:, 2 * LANES:3 * LANES]], axis=1)
        r = lax.rsqrt((_dot(_sq_bf16(kn), sum_ref[...]) + pe_ss) * inv_d + EPS)
        kn = kn * r * kg_ref[...]
        for a in range(2):
            _put(kn_ref, 2 * j + a, kn[:, a * LANES:(a + 1) * LANES])
            _put(v_ref, 2 * j + a, y[:, (2 * a + 1) * LANES:(2 * a + 2) * LANES])
        pe = kpe * jnp.where(lo, r[:, :LANES], r[:, LANES:]) * kgp_ref[...]
        _put(kp_ref, j, _maybe_rope(pe, rope))

    _matmul_units(ckv, w_s, MLA_HEADS // 2, 4 * LANES, emit)


def _proj_mla_kernel(h_ref, w_in_ref, qa_ref, kva_ref, w_uq_ref, qg_ref, qgp_ref, lohi_ref, lo_ref,
                     w_ukv_ref, kg_ref, kgp_ref, sum_ref, cos_ref, sp_ref, sn_ref,
                     qn_ref, qp_ref, c_ckv_ref, c_kpe_ref, kn_ref, kp_ref, v_ref,
                     w_in_s, w_uq_s, w_ukv_s):
    i = pl.program_id(0)
    n_in = MLA_Q_RANK + MLA_KV_RANK + MLA_ROPE

    @pl.when(i == 0)
    def _():
        w_in_s[...] = jnp.zeros_like(w_in_s)
        w_in_s[:, :n_in] = w_in_ref[...].astype(BF16)
        w_uq_s[...] = w_uq_ref[...].astype(BF16)
        w_ukv_s[...] = w_ukv_ref[...].astype(BF16)

    y = _dot(h_ref[...], w_in_s[...])
    c_q = y[:, :MLA_Q_RANK]
    c_kv = y[:, MLA_Q_RANK:MLA_Q_RANK + MLA_KV_RANK]
    kpe = y[:, MLA_Q_RANK + MLA_KV_RANK:]
    kpe = kpe + pltpu.roll(kpe, HALF, 1)
    ckv = c_kv * lax.rsqrt(jnp.mean(c_kv * c_kv, axis=-1, keepdims=True) + EPS) * kva_ref[...]
    cq = (c_q * lax.rsqrt(jnp.mean(c_q * c_q, axis=-1, keepdims=True) + EPS) * qa_ref[...]).astype(BF16)
    lo = _lane_lo((PROJ_TM, LANES))
    inv_d = 1.0 / (MLA_NOPE + MLA_ROPE)

    def body(lat):
        rope = _rope_args(lat, cos_ref, sp_ref, sn_ref, MLA_ROPE)
        if not lat:
            _cache_rows(c_ckv_ref, [], ckv)
            _cache_rows_t(c_kpe_ref, [[]], kpe)

        def emit(j, yq):
            y0, y1, y2 = _halves(yq)
            nopes = (y0, jnp.where(lo, pltpu.roll(y1, HALF, 1), pltpu.roll(y2, HALF, 1)))
            pe = jnp.where(lo, y1, y2)
            ss = (_dot(_sq_bf16(jnp.concatenate(nopes, axis=1)), sum_ref[...])
                  + _dot(_sq_bf16(pe), lohi_ref[...]))
            rs = _halves(lax.rsqrt(ss * inv_d + EPS))
            for a in range(2):
                _put(qn_ref, 2 * j + a, nopes[a] * rs[a] * qg_ref[...])
            _put(qp_ref, j, _maybe_rope(pe * jnp.where(lo, rs[0], rs[1]) * qgp_ref[...], rope))

        _matmul_units(cq, w_uq_s, MLA_HEADS // 2, 3 * LANES, emit)
        _mla_keys(ckv.astype(BF16), kpe, w_ukv_s, kg_ref, kgp_ref, sum_ref, lo_ref, rope,
                  kn_ref, kp_ref, v_ref)

    _by_tile_kind(i, body)


def _proj_mla(h, w_in, qa, kva, w_uq, qg, qgp, w_ukv, kg, kgp, tables):
    n_nope = MLA_HEADS * MLA_NOPE
    n_pe = MLA_HEADS * MLA_ROPE
    n_in = -(-w_in.shape[1] // LANES) * LANES
    return pl.pallas_call(
        _proj_mla_kernel,
        grid=(N_PROJ_TILES,),
        in_specs=[_tok_spec(D_MODEL),
                  _const_spec(w_in.shape), qa.spec, kva.spec,
                  _const_spec(w_uq.shape), qg.spec, qgp.spec,
                  _const_spec((LANES, PROJ_UNIT)), _LANE_MAT_SPEC,
                  _const_spec(w_ukv.shape), kg.spec, kgp.spec, _UNIT_MAT_SPEC,
                  _ROPE_SPEC, _ROPE_SPEC, _ROPE_SPEC],
        out_specs=[_tok_spec(n_nope), _tok_spec(n_pe),
                   _cache_spec(SEQ, MLA_KV_RANK), _cache_spec(MLA_ROPE, SEQ),
                   _tok_spec(n_nope), _tok_spec(n_pe), _tok_spec(n_nope)],
        out_shape=[jax.ShapeDtypeStruct((N_TOK, n_nope), BF16),
                   jax.ShapeDtypeStruct((N_TOK, n_pe), BF16),
                   _cache_shape(SEQ, MLA_KV_RANK), _cache_shape(MLA_ROPE, SEQ),
                   jax.ShapeDtypeStruct((N_TOK, n_nope), BF16),
                   jax.ShapeDtypeStruct((N_TOK, n_pe), BF16),
                   jax.ShapeDtypeStruct((N_TOK, n_nope), BF16)],
        scratch_shapes=[pltpu.VMEM((D_MODEL, n_in), BF16), pltpu.VMEM(w_uq.shape, BF16),
                        pltpu.VMEM(w_ukv.shape, BF16)],
        compiler_params=_cparams(1),
        name="proj_mla",
    )(h, w_in, qa.array, kva.array, w_uq, qg.array, qgp.array, *_mla_lane_matrices(),
      w_ukv, kg.array, kgp.array, _group_sum_matrix(), *tables)


def _mla_ctx_kernel(ckv_ref, kpe_ref, w_ref, kg_ref, kgp_ref, sum_ref, lo_ref, kn_ref, kp_ref, v_ref):
    _mla_keys(ckv_ref[...].astype(BF16), kpe_ref[...], w_ref[...].astype(BF16), kg_ref, kgp_ref,
              sum_ref, lo_ref, None, kn_ref, kp_ref, v_ref)


def _mla_ctx(ckv, kpe_dup, w_ukv, kg, kgp):
    n = ckv.shape[0]
    n_nope = MLA_HEADS * MLA_NOPE
    n_pe = MLA_HEADS * MLA_ROPE
    _, m_lo = _mla_lane_matrices()
    return pl.pallas_call(
        _mla_ctx_kernel,
        grid=(n // PROJ_TM,),
        in_specs=[_tok_spec(MLA_KV_RANK), _tok_spec(LANES), _const_spec(w_ukv.shape),
                  kg.spec, kgp.spec, _UNIT_MAT_SPEC, _LANE_MAT_SPEC],
        out_specs=[_tok_spec(n_nope), _tok_spec(n_pe), _tok_spec(n_nope)],
        out_shape=[jax.ShapeDtypeStruct((n, n_nope), BF16),
                   jax.ShapeDtypeStruct((n, n_pe), BF16),
                   jax.ShapeDtypeStruct((n, n_nope), BF16)],
        compiler_params=_cparams(1),
        name="mla_ctx",
    )(ckv, kpe_dup, w_ukv, kg.array, kgp.array, _group_sum_matrix(), m_lo)


def _prompt_spec(width):
    return pl.BlockSpec((PROMPT_SEQS * TM, width), lambda b: (b, 0))


def _latq_spec(rows, width):
    per = DEC_SEQ // rows
    return pl.BlockSpec((rows, width), lambda b, t: (N_PROMPT_TOK // rows + b * per + t, 0))


def _latkv_spec(width):
    return pl.BlockSpec((DEC_SEQ, width), lambda b, t: (LAT_BLOCK0 + b, 0))


def _lato_spec(rows):
    per = DEC_SEQ // rows
    return pl.BlockSpec((rows, D_MODEL), lambda b, t: (b * per + t, 0))


def _att_kernel(*refs, with_ctx, seqs):
    if with_ctx:
        q_ref, k_ref, v_ref, kc_ref, vc_ref, o_ref = refs
    else:
        q_ref, k_ref, v_ref, o_ref = refs
    tq = q_ref.shape[0] // seqs
    nu = ATT_UNIT_HEADS
    per_kv = ATT_HEADS // ATT_KV_HEADS // nu

    def make(views, _):
        q_v, k_v, v_v, o_v = views

        def scores(u):
            q = jnp.concatenate([_chunk(q_v, u * nu + g) for g in range(nu)], axis=0)
            s_list = [_dot_nt(_chunk(k_v, u // per_kv), q)]
            if with_ctx:
                s_list.append(_dot_nt(kc_ref[u // per_kv].astype(BF16), q))
            return s_list

        def finish(u, s_list):
            values = [_chunk(v_v, u // per_kv)]
            if with_ctx:
                values.append(vc_ref[u // per_kv].astype(BF16))
            ps, inv = _softmax2_parts(s_list)
            o = _pv(ps, values) * inv
            for g in range(nu):
                o_v[:, (u * nu + g) * LANES:(u * nu + g + 1) * LANES] = (
                    o[:, g * tq:(g + 1) * tq].T.astype(o_v.dtype))

        return scores, finish

    _seq_pipeline((q_ref, k_ref, v_ref, o_ref), seqs, ATT_HEADS // nu, make)


def _att_attend(q, k, v, cache_k, cache_v):
    nk = ATT_KV_HEADS * ATT_HEAD_DIM
    out_p = pl.pallas_call(
        functools.partial(_att_kernel, with_ctx=False, seqs=PROMPT_SEQS),
        grid=(N_PROMPT_TILES // PROMPT_SEQS,),
        in_specs=[_prompt_spec(D_MODEL), _prompt_spec(nk), _prompt_spec(nk)],
        out_specs=_prompt_spec(D_MODEL),
        out_shape=jax.ShapeDtypeStruct((N_PROMPT_TOK, D_MODEL), BF16),
        compiler_params=_cparams(1),
        name="att_prompt",
    )(q, k, v)
    ctx = pl.BlockSpec((None, None, ATT_KV_HEADS, PAST_LEN, LANES), lambda b, t: (b, 0, 0, 0, 0))
    out_s = pl.pallas_call(
        functools.partial(_att_kernel, with_ctx=True, seqs=1),
        grid=(DEC_BATCH, DEC_SEQ // LAT_TQ),
        in_specs=[_latq_spec(LAT_TQ, D_MODEL), _latkv_spec(nk), _latkv_spec(nk), ctx, ctx],
        out_specs=_lato_spec(LAT_TQ),
        out_shape=jax.ShapeDtypeStruct((N_LAT_TOK, D_MODEL), BF16),
        compiler_params=_cparams(2),
        name="att_latent",
    )(q, k, v, cache_k, cache_v)
    return out_p, out_s


def _diff_kernel(*refs, lam_init, with_ctx, seqs):
    if with_ctx:
        (q_ref, k_ref, v_ref, kc_ref, vc_ref, lq1_ref, lk1_ref, lq2_ref, lk2_ref, sub_ref, o_ref) = refs
    else:
        (q_ref, k_ref, v_ref, lq1_ref, lk1_ref, lq2_ref, lk2_ref, sub_ref, o_ref) = refs
    tq = q_ref.shape[0] // seqs
    lam = (jnp.exp(jnp.sum(lq1_ref[...] * lk1_ref[...], axis=-1, keepdims=True))
           - jnp.exp(jnp.sum(lq2_ref[...] * lk2_ref[...], axis=-1, keepdims=True)) + lam_init)
    diag = (lax.broadcasted_iota(jnp.int32, (LANES, LANES), 0)
            == lax.broadcasted_iota(jnp.int32, (LANES, LANES), 1))
    sub = jnp.sum(jnp.where(diag, sub_ref[...] * (1.0 - lam_init), 0.0), axis=1, keepdims=True)

    def make(views, _):
        q_v, k_v, v_v, o_v = views

        def scores(hd):
            q = jnp.concatenate(_split_halves(_chunk(q_v, hd)), axis=0)
            s_list = [_dot_nt(_chunk(k_v, hd), q)]
            if with_ctx:
                s_list.append(_dot_nt(kc_ref[hd].astype(BF16), q))
            return s_list

        def finish(hd, s_list):
            values = [_chunk(v_v, hd)]
            if with_ctx:
                values.append(vc_ref[hd].astype(BF16))
            ps, inv = _softmax2_parts(s_list)
            o = (_pv([p[:, :tq] for p in ps], values) * inv[:, :tq]
                 - _pv([p[:, tq:] for p in ps], values) * (lam * inv[:, tq:]))
            o = o * lax.rsqrt(jnp.mean(o * o, axis=0, keepdims=True) + EPS) * sub
            o_v[:, hd * LANES:(hd + 1) * LANES] = o.T.astype(o_v.dtype)

        return scores, finish

    _seq_pipeline((q_ref, k_ref, v_ref, o_ref), seqs, DIFF_HEADS, make)


def _diff_attend(q, k, v, cache_k_pair, cache_v, lq1, lk1, lq2, lk2, subln, lam_init):
    small_specs = [p.spec for p in (lq1, lk1, lq2, lk2, subln)]
    small = [p.array for p in (lq1, lk1, lq2, lk2, subln)]
    out_p = pl.pallas_call(
        functools.partial(_diff_kernel, lam_init=lam_init, with_ctx=False, seqs=PROMPT_SEQS),
        grid=(N_PROMPT_TILES // PROMPT_SEQS,),
        in_specs=[_prompt_spec(D_MODEL)] * 3 + small_specs,
        out_specs=_prompt_spec(D_MODEL),
        out_shape=jax.ShapeDtypeStruct((N_PROMPT_TOK, D_MODEL), BF16),
        compiler_params=_cparams(1),
        name="diff_prompt",
    )(q, k, v, *small)
    out_s = pl.pallas_call(
        functools.partial(_diff_kernel, lam_init=lam_init, with_ctx=True, seqs=1),
        grid=(DEC_BATCH, DEC_SEQ // LAT_TQ),
        in_specs=[_latq_spec(LAT_TQ, D_MODEL), _latkv_spec(D_MODEL), _latkv_spec(D_MODEL),
                  pl.BlockSpec((None, DIFF_HEADS, PAST_LEN, LANES), lambda b, t: (b, 0, 0, 0)),
                  pl.BlockSpec((None, None, DIFF_HEADS, PAST_LEN, LANES), lambda b, t: (b, 0, 0, 0, 0))]
                 + small_specs,
        out_specs=_lato_spec(LAT_TQ),
        out_shape=jax.ShapeDtypeStruct((N_LAT_TOK, D_MODEL), BF16),
        compiler_params=_cparams(2),
        name="diff_latent",
    )(q, k, v, cache_k_pair, cache_v, *small)
    return out_p, out_s


def _swa_pipeline(q_ref, o_ref, seq_refs, sink_ref, score_fns, value_fns, seqs=1):
    tq = q_ref.shape[0] // seqs
    per_kv = SWA_HEADS // SWA_KV_HEADS // 2
    first = lax.broadcasted_iota(jnp.int32, (LANES, tq), 0) < HALF

    def make(views, b):
        q_v, o_v = views[:2]
        kv_views = (b, views[2:])

        def scores(c):
            q = jnp.concatenate(_split_halves(_chunk(q_v, c)), axis=0)
            return [fn(kv_views, c // per_kv, q) for fn in score_fns]

        def finish(c, s_list):
            sink = jnp.concatenate([jnp.full((1, tq), sink_ref[2 * c + a] * LOG2E, F32) for a in range(2)],
                                   axis=1)
            ps, inv = _softmax2_parts(s_list, extra=sink)
            o = _pv(ps, [fn(kv_views, c // per_kv) for fn in value_fns]) * inv
            oc = jnp.where(first, o[:, :tq], o[:, tq:])
            o_v[:, c * LANES:(c + 1) * LANES] = oc.T.astype(o_v.dtype)

        return scores, finish

    _seq_pipeline((q_ref, o_ref) + tuple(seq_refs), seqs, SWA_HEADS // 2, make)


def _swa_prompt_kernel(sink_ref, q_ref, k_ref, v_ref, o_ref):
    _swa_pipeline(q_ref, o_ref, (k_ref, v_ref), sink_ref,
                  [lambda seq, kv, q: _dot_nt(_chunk(seq[1][0], kv), q)],
                  [lambda seq, kv: _chunk(seq[1][1], kv)], seqs=PROMPT_SEQS)


def _swa_latent_kernel(sink_ref, q_ref, k_ref, v_ref, kc_ref, vc_ref, o_ref):
    tq = SWA_QB
    span = SWA_QB + 2 * WINDOW
    cols = lax.broadcasted_iota(jnp.int32, (span, 2 * tq), 1)
    rows = lax.broadcasted_iota(jnp.int32, (span, 2 * tq), 0)
    starts, valids = [], []
    for b in range(SWA_BLOCKS):
        n = pl.program_id(1) * SWA_BLOCKS + b
        start = pl.multiple_of(jnp.clip(n * SWA_QB - WINDOW, 0, DEC_SEQ - span), WINDOW)
        qpos = n * SWA_QB + jnp.bitwise_and(cols, tq - 1)
        starts.append(start)
        valids.append(jnp.abs(qpos - (start + rows)) <= WINDOW)

    def local(ref, seq, kv):
        return ref[pl.ds(starts[seq[0]], span), kv * LANES:(kv + 1) * LANES]

    _swa_pipeline(q_ref, o_ref, (), sink_ref,
                  [lambda seq, kv, q: jnp.where(valids[seq[0]], _dot_nt(local(k_ref, seq, kv), q), -1e30),
                   lambda seq, kv, q: _dot_nt(kc_ref[kv], q)],
                  [lambda seq, kv: local(v_ref, seq, kv), lambda seq, kv: vc_ref[kv]], seqs=SWA_BLOCKS)


def _swa_attend(q, kd, vd, cache_kd, cache_vd, sink):
    nkd = 2 * SWA_KV_HEADS * SWA_HEAD_DIM
    smem = pl.BlockSpec(memory_space=pltpu.SMEM)
    out_p = pl.pallas_call(
        _swa_prompt_kernel,
        grid=(N_PROMPT_TILES // PROMPT_SEQS,),
        in_specs=[smem, _prompt_spec(D_MODEL), _prompt_spec(nkd), _prompt_spec(nkd)],
        out_specs=_prompt_spec(D_MODEL),
        out_shape=jax.ShapeDtypeStruct((N_PROMPT_TOK, D_MODEL), BF16),
        compiler_params=_cparams(1),
        name="swa_prompt",
    )(sink, q, kd, vd)
    ctx = pl.BlockSpec((None, SWA_KV_HEADS, PAST_LEN, LANES), lambda b, n: (b, 0, 0, 0))
    out_s = pl.pallas_call(
        _swa_latent_kernel,
        grid=(DEC_BATCH, DEC_SEQ // (SWA_QB * SWA_BLOCKS)),
        in_specs=[smem, _latq_spec(SWA_QB * SWA_BLOCKS, D_MODEL), _latkv_spec(nkd), _latkv_spec(nkd), ctx, ctx],
        out_specs=_lato_spec(SWA_QB * SWA_BLOCKS),
        out_shape=jax.ShapeDtypeStruct((N_LAT_TOK, D_MODEL), BF16),
        compiler_params=_cparams(2),
        name="swa_latent",
    )(sink, q, kd, vd, cache_kd, cache_vd)
    return out_p, out_s


def _mla_kernel(*refs, with_ctx, seqs):
    if with_ctx:
        (qn_ref, qp_ref, kn_ref, kp_ref, v_ref, knc_ref, kpc_ref, vc_ref, o_ref) = refs
    else:
        (qn_ref, qp_ref, kn_ref, kp_ref, v_ref, o_ref) = refs

    def make(views, _):
        qn_v, qp_v, kn_v, kp_v, v_v, o_v = views

        def scores(hd):
            j, a = hd // 2, hd % 2
            q = jnp.concatenate([_chunk(qn_v, hd), _split_halves(_chunk(qp_v, j))[a]], axis=1)
            s_list = [_dot_nt(jnp.concatenate([_chunk(kn_v, hd), _chunk(kp_v, j)], axis=1), q)]
            if with_ctx:
                s_list.append(_dot_nt(jnp.concatenate([_chunk(knc_ref, hd), _chunk(kpc_ref, j)], axis=1), q))
            return s_list

        def finish(hd, s_list):
            values = [_chunk(v_v, hd)]
            if with_ctx:
                values.append(_chunk(vc_ref, hd))
            ps, inv = _softmax2_parts(s_list)
            o_v[:, hd * LANES:(hd + 1) * LANES] = (_pv(ps, values) * inv).T.astype(o_v.dtype)

        return scores, finish

    _seq_pipeline((qn_ref, qp_ref, kn_ref, kp_ref, v_ref, o_ref), seqs, MLA_HEADS, make)


def _mla_attend(qn, qp, kn, kp, v, knc, kpc, vc):
    n_pe = MLA_HEADS * MLA_ROPE
    out_p = pl.pallas_call(
        functools.partial(_mla_kernel, with_ctx=False, seqs=PROMPT_SEQS),
        grid=(N_PROMPT_TILES // PROMPT_SEQS,),
        in_specs=[_prompt_spec(D_MODEL), _prompt_spec(n_pe), _prompt_spec(D_MODEL), _prompt_spec(n_pe),
                  _prompt_spec(D_MODEL)],
        out_specs=_prompt_spec(D_MODEL),
        out_shape=jax.ShapeDtypeStruct((N_PROMPT_TOK, D_MODEL), BF16),
        compiler_params=_cparams(1),
        name="mla_prompt",
    )(qn, qp, kn, kp, v)

    def ctx(width):
        return pl.BlockSpec((PAST_LEN, width), lambda b, t: (b, 0))

    out_s = pl.pallas_call(
        functools.partial(_mla_kernel, with_ctx=True, seqs=1),
        grid=(DEC_BATCH, DEC_SEQ // LAT_TQ),
        in_specs=[_latq_spec(LAT_TQ, D_MODEL), _latq_spec(LAT_TQ, n_pe),
                  _latkv_spec(D_MODEL), _latkv_spec(n_pe), _latkv_spec(D_MODEL),
                  ctx(D_MODEL), ctx(n_pe), ctx(D_MODEL)],
        out_specs=_lato_spec(LAT_TQ),
        out_shape=jax.ShapeDtypeStruct((N_LAT_TOK, D_MODEL), BF16),
        compiler_params=_cparams(2),
        name="mla_latent",
    )(qn, qp, kn, kp, v, knc, kpc, vc)
    return out_p, out_s


def _omlp_kernel(*refs, first, last):
    refs = list(refs)
    ap_ref, as_ref, wo_ref = refs[:3]
    x_refs = refs[3:5] if first else refs[3:4]
    refs = refs[3 + len(x_refs):]
    g1_ref, gain_ref, sh_ref, sc_ref, g2_ref, w1c_ref, w2c_ref = refs[:7]
    refs = refs[7:]
    if last:
        op_ref, os_ref, wo_s, w1_s, w2_s = refs
    else:
        ngain_ref, nsh_ref, nsc_ref, o_ref, hn_ref, wo_s, w1_s, w2_s = refs
    s = pl.program_id(0)
    per = MLP_FF_CHUNK // MLP_LOAD_COLS
    n_chunks = D_FF // MLP_FF_CHUNK
    half = MLP_TM // 2

    for part in range(per):
        @pl.when((s < N_LOAD_STEPS) & (s % per == part))
        def _(part=part):
            w1_s[s // per, :, part * MLP_LOAD_COLS:(part + 1) * MLP_LOAD_COLS] = w1c_ref[...].astype(BF16)

    @pl.when(s < N_LOAD_STEPS)
    def _():
        w2_s[s // per, pl.ds(pl.multiple_of((s % per) * MLP_LOAD_COLS, MLP_LOAD_COLS), MLP_LOAD_COLS), :] = (
            w2c_ref[...].astype(BF16))
        wo_s[pl.ds(pl.multiple_of(s * WO_LOAD_ROWS, WO_LOAD_ROWS), WO_LOAD_ROWS), :] = wo_ref[...].astype(BF16)

    @pl.when(s >= N_LOAD_STEPS)
    def _():
        t = s - N_LOAD_STEPS
        is_prompt = t < N_MLP_PROMPT_TILES
        grp = _tile_group(t, MLP_TM)

        def mod(ref):
            return ref[pl.ds(grp, 1), :]

        rows = [slice(r * half, (r + 1) * half) for r in range(2)]
        o = [_dot(jnp.where(is_prompt, ap_ref[rw, :], as_ref[rw, :]), wo_s[...]) for rw in rows]
        x1, h, u0 = [], [], []
        for r, rw in enumerate(rows):
            x = jnp.where(is_prompt, x_refs[0][rw, :], x_refs[1][rw, :]) if first else x_refs[0][rw, :]
            x1.append(x + mod(g1_ref) * o[r])
            h.append(_norm_mod(x1[r], gain_ref[...], mod(sh_ref), mod(sc_ref)).astype(BF16))
            u0.append(_dot(h[r], w1_s[0]))
        h = jnp.concatenate(h, axis=0)
        acc = []

        def up(c):
            return jnp.concatenate(u0, axis=0) if c == 0 else _dot(h, w1_s[c])

        def down(c, u):
            u = jnp.square(jnp.maximum(u, 0.0)).astype(BF16)
            if c + 1 < n_chunks:
                y = _dot(u, w2_s[c])
                acc[:] = [y if not acc else acc[0] + y]
            else:
                acc[:] = [acc[0][rw] + _dot(u[rw], w2_s[c]) for rw in rows]

        _head_pipeline(n_chunks, up, down)
        for r, rw in enumerate(rows):
            out = x1[r] + mod(g2_ref) * acc[r]
            if last:
                @pl.when(is_prompt)
                def _(out=out, rw=rw):
                    op_ref[rw, :] = out

                @pl.when(jnp.logical_not(is_prompt))
                def _(out=out, rw=rw):
                    os_ref[rw, :] = out
            else:
                o_ref[rw, :] = out
                hn_ref[rw, :] = _norm_mod(out, ngain_ref[...], mod(nsh_ref), mod(nsc_ref)).astype(BF16)


def _omlp(attn_p, attn_s, w_o, x, mods, gain_ffn, w1_all, w2_all, layer, next_gain):
    first, last = layer == 0, next_gain is None
    n_lat_tiles = N_LAT_TOK // MLP_TM

    def tok(s):
        return jnp.maximum(s - N_LOAD_STEPS, 0)

    p_spec = pl.BlockSpec((MLP_TM, D_MODEL), lambda s: (jnp.minimum(tok(s), N_MLP_PROMPT_TILES - 1), 0))
    l_spec = pl.BlockSpec((MLP_TM, D_MODEL),
                          lambda s: (jnp.clip(tok(s) - N_MLP_PROMPT_TILES, 0, n_lat_tiles - 1), 0))
    w1_spec = pl.BlockSpec((None, D_MODEL, MLP_LOAD_COLS),
                           lambda s: (layer, 0, jnp.minimum(s, N_LOAD_STEPS - 1)))
    w2_spec = pl.BlockSpec((None, MLP_LOAD_COLS, D_MODEL),
                           lambda s: (layer, jnp.minimum(s, N_LOAD_STEPS - 1), 0))
    t_spec = pl.BlockSpec((MLP_TM, D_MODEL), lambda s: (tok(s), 0))
    n_chunks = D_FF // MLP_FF_CHUNK
    split = ([p_spec, l_spec], [jax.ShapeDtypeStruct((N_PROMPT_TOK, D_MODEL), F32),
                                jax.ShapeDtypeStruct((N_LAT_TOK, D_MODEL), F32)])
    wo_spec = pl.BlockSpec((WO_LOAD_ROWS, D_MODEL), lambda s: (jnp.minimum(s, N_LOAD_STEPS - 1), 0))
    in_specs = ([p_spec, l_spec, wo_spec] + (split[0] if first else [t_spec])
                + [_mod_spec(layer, 2), gain_ffn.spec, _mod_spec(layer, 3), _mod_spec(layer, 4),
                   _mod_spec(layer, 5),
                   w1_spec, w2_spec])
    args = ([attn_p, attn_s, w_o] + (list(x) if first else [x])
            + [mods, gain_ffn.array, mods, mods, mods, w1_all, w2_all])
    if last:
        out_specs, out_shape = split
    else:
        in_specs += [next_gain.spec, _mod_spec(layer + 1, 0), _mod_spec(layer + 1, 1)]
        args += [next_gain.array, mods, mods]
        out_specs = [t_spec, t_spec]
        out_shape = [jax.ShapeDtypeStruct((N_TOK, D_MODEL), F32), jax.ShapeDtypeStruct((N_TOK, D_MODEL), BF16)]
    return pl.pallas_call(
        functools.partial(_omlp_kernel, first=first, last=last),
        grid=(N_LOAD_STEPS + N_TOK // MLP_TM,),
        in_specs=in_specs,
        out_specs=out_specs,
        out_shape=out_shape,
        scratch_shapes=[pltpu.VMEM((D_MODEL, D_MODEL), BF16),
                        pltpu.VMEM((n_chunks, D_MODEL, MLP_FF_CHUNK), BF16),
                        pltpu.VMEM((n_chunks, MLP_FF_CHUNK, D_MODEL), BF16)],
        compiler_params=_cparams(1),
        name="omlp",
    )(*args)


def kernel(x_prompt, x_sample, cache_att_k, cache_att_v, cache_diff_k, cache_diff_v, cache_swa_k, cache_swa_v, cache_mla_ckv, cache_mla_kpe, c, c_ctx, ada_w, ada_b, norm_mix, norm_ffn, att_w_qkv, att_q_norm, att_k_norm, att_w_o, diff_w_qkv, diff_q_norm, diff_k_norm, diff_lq1, diff_lk1, diff_lq2, diff_lk2, diff_subln, diff_w_o, swa_w_qkv, swa_q_norm, swa_k_norm, swa_sink, swa_w_o, mla_w_in, mla_q_a_norm, mla_kv_a_norm, mla_w_uq, mla_w_ukv, mla_q_norm, mla_k_norm, mla_w_o, mlp_w1, mlp_w2):
    xp = x_prompt.reshape(N_PROMPT_TOK, D_MODEL)
    xs = x_sample.reshape(N_LAT_TOK, D_MODEL)
    cond = jnp.concatenate([c_ctx[None], c, jnp.zeros((COND_ROWS - 1 - DEC_BATCH, D_MODEL), F32)], axis=0)
    mods_all = _modulation(cond, ada_w, ada_b)

    tab_att = _rope_tables(ATT_HEAD_DIM)
    tab_64 = _rope_tables(DIFF_HEAD_DIM)

    pk = _ParamPack()
    g_mix = [pk.add(norm_mix[l]) for l in range(DEPTH)]
    g_ffn = [pk.add(norm_ffn[l]) for l in range(DEPTH)]
    att_qg = pk.add(att_q_norm[0], PROJ_UNIT // ATT_HEAD_DIM, ATT_HEAD_DIM ** -0.5 * LOG2E)
    att_kg = pk.add(att_k_norm[0], PROJ_UNIT // ATT_HEAD_DIM)
    diff_qg = pk.add(diff_q_norm[0], PROJ_UNIT // DIFF_HEAD_DIM, DIFF_HEAD_DIM ** -0.5 * LOG2E)
    diff_kg = pk.add(diff_k_norm[0], PROJ_UNIT // DIFF_HEAD_DIM)
    diff_small = [pk.add(v[0]) for v in (diff_lq1, diff_lk1, diff_lq2, diff_lk2, diff_subln)]
    swa_qg = pk.add(swa_q_norm[0], PROJ_UNIT // SWA_HEAD_DIM, SWA_HEAD_DIM ** -0.5 * LOG2E)
    swa_kg = pk.add(swa_k_norm[0], PROJ_UNIT // SWA_HEAD_DIM)
    mla_qs = (MLA_NOPE + MLA_ROPE) ** -0.5 * LOG2E
    mla_qa = pk.add(mla_q_a_norm[0])
    mla_kva = pk.add(mla_kv_a_norm[0])
    mla_qg = pk.add(mla_q_norm[0][:MLA_NOPE], 1, mla_qs)
    mla_qgp = pk.add(mla_q_norm[0][MLA_NOPE:], LANES // MLA_ROPE, mla_qs)
    mla_kg = pk.add(mla_k_norm[0][:MLA_NOPE], PROJ_UNIT // MLA_NOPE)
    mla_kgp = pk.add(mla_k_norm[0][MLA_NOPE:], LANES // MLA_ROPE)
    pk.build()

    outs = {}
    x = (xp, xs)
    for layer in range(DEPTH):
        gain_ffn = g_ffn[layer]
        if layer == 0:
            q, k, v, outs["att_k"], outs["att_v"] = _proj_att(
                xp, xs, mods_all, g_mix[layer], att_w_qkv[0], att_qg, att_kg, tab_att)
            attn_p, attn_s = _att_attend(q, k, v, cache_att_k, cache_att_v)
            w_o = att_w_o[0]
        elif layer == 1:
            q, k, v, outs["diff_k"], outs["diff_v"] = _proj_diff(h, diff_w_qkv[0], diff_qg, diff_kg, tab_64)
            lam_init = 0.8 - 0.6 * math.exp(-0.3 * layer)
            ck = cache_diff_k[:, 0].transpose(0, 1, 3, 2, 4).reshape(
                DEC_BATCH, DIFF_HEADS, PAST_LEN, LANES)
            attn_p, attn_s = _diff_attend(q, k, v, ck, cache_diff_v, *diff_small, lam_init)
            w_o = diff_w_o[0]
        elif layer == 2:
            q, kd, vd, outs["swa_k"], outs["swa_v"] = _proj_swa(h, swa_w_qkv[0], swa_qg, swa_kg, tab_64)
            ckd = jnp.concatenate([cache_swa_k[:, 0]] * 2, axis=-1).astype(BF16)
            cvd = jnp.concatenate([cache_swa_v[:, 0]] * 2, axis=-1).astype(BF16)
            attn_p, attn_s = _swa_attend(q, kd, vd, ckd, cvd, swa_sink[0].astype(F32))
            w_o = swa_w_o[0]
        else:
            qn, qp, outs["mla_ckv"], outs["mla_kpe"], kn, kp, vv = _proj_mla(
                h, mla_w_in[0], mla_qa, mla_kva, mla_w_uq[0], mla_qg, mla_qgp,
                mla_w_ukv[0], mla_kg, mla_kgp, tab_64)
            c_ckv = cache_mla_ckv[:, 0].reshape(DEC_BATCH * PAST_LEN, MLA_KV_RANK)
            c_kpe = cache_mla_kpe[:, 0].reshape(DEC_BATCH * PAST_LEN, MLA_ROPE)
            c_kpe = jnp.concatenate([c_kpe, c_kpe], axis=-1)
            knc, kpc, vc = _mla_ctx(c_ckv, c_kpe, mla_w_ukv[0], mla_kg, mla_kgp)
            attn_p, attn_s = _mla_attend(qn, qp, kn, kp, vv, knc, kpc, vc)
            w_o = mla_w_o[0]
        if layer + 1 < DEPTH:
            x, h = _omlp(attn_p, attn_s, w_o, x, mods_all, gain_ffn, mlp_w1, mlp_w2, layer,
                         g_mix[layer + 1])
        else:
            xp, xs = _omlp(attn_p, attn_s, w_o, x, mods_all, gain_ffn, mlp_w1, mlp_w2, layer, None)

    y_prompt = xp.reshape(BATCH, SEQ, D_MODEL)
    y_sample = xs.reshape(DEC_BATCH, DEC_SEQ, D_MODEL)
    for name in ("diff_k", "swa_k", "swa_v", "mla_kpe"):
        outs[name] = jnp.swapaxes(outs[name], -1, -2)
    return (y_prompt, y_sample, outs["att_k"], outs["att_v"], outs["diff_k"], outs["diff_v"],
            outs["swa_k"], outs["swa_v"], outs["mla_ckv"], outs["mla_kpe"])
```
